```python
import jax, jax.numpy as jnp
from jax import lax
import numpy as np

D_MODEL = 2048
BATCH = 8
SEQ = 4096
DEPTH = 1

CHUNK = 64
Q_BLOCK = 128
D_MIX = D_MODEL
FOX_WIDTH = D_MIX // 2
GLA_WIDTH = D_MIX - FOX_WIDTH
FOX_HEAD_DIM = 128
FOX_HEADS = FOX_WIDTH // FOX_HEAD_DIM
GLA_HEADS = 4
GLA_DV = GLA_WIDTH // GLA_HEADS
GLA_DK = GLA_DV // 2
GLA_KEY_WIDTH = GLA_HEADS * GLA_DK
GLA_GATE_RANK = 16
GLA_GATE_TEMP = 16.0
D_FF = 4 * D_MODEL
N_MOD = 6
EPS = 1e-6

IN_SIZES = (FOX_WIDTH, FOX_WIDTH, FOX_WIDTH, FOX_HEADS,
            GLA_KEY_WIDTH, GLA_KEY_WIDTH, GLA_WIDTH, GLA_GATE_RANK, GLA_WIDTH)
IN_OFFSETS = tuple(int(o) for o in np.cumsum(IN_SIZES)[:-1])
D_IN_PROJ = int(sum(IN_SIZES))

kernel_name = "hymba_fox_gla_sandwich_adaln_block"


def rmsnorm(x, g):
    xf = x.astype(jnp.float32)
    y = xf * lax.rsqrt(jnp.mean(xf * xf, axis=-1, keepdims=True) + EPS)
    return (y * g.astype(jnp.float32)).astype(x.dtype)


def head_rmsnorm(x, g):
    xf = x.astype(jnp.float32)
    y = xf * lax.rsqrt(jnp.mean(xf * xf, axis=-1, keepdims=True) + EPS)
    return (y * g.astype(jnp.float32)).astype(x.dtype)


def forgetting_attention(q, k, v, log_f):
    s = q.shape[1]
    scale = FOX_HEAD_DIM ** -0.5
    cum = jnp.transpose(jnp.cumsum(log_f, axis=1), (0, 2, 1))
    neg = jnp.finfo(jnp.float32).min
    outs = []
    for i in range(s // Q_BLOCK):
        q0, q1 = i * Q_BLOCK, (i + 1) * Q_BLOCK
        qb, kb, vb = q[:, q0:q1], k[:, :q1], v[:, :q1]
        logits = jnp.einsum('bqhd,bkhd->bhqk', qb, kb,
                            preferred_element_type=jnp.float32) * scale
        bias = cum[:, :, q0:q1, None] - cum[:, :, None, :q1]
        q_pos = jnp.arange(q0, q1)[:, None]
        k_pos = jnp.arange(q1)[None, :]
        logits = jnp.where(q_pos >= k_pos, logits + bias, neg)
        p = jax.nn.softmax(logits, axis=-1)
        outs.append(jnp.einsum('bhqk,bkhd->bqhd', p.astype(vb.dtype), vb))
    return jnp.concatenate(outs, axis=1)


def gla_chunk_causal(q, k, v, log_a):
    b, s, h, dk = q.shape
    dv = v.shape[-1]
    nc = s // CHUNK
    qc = q.reshape(b, nc, CHUNK, h, dk).astype(jnp.float32) * (dk ** -0.5)
    kc = k.reshape(b, nc, CHUNK, h, dk).astype(jnp.float32)
    vc = v.reshape(b, nc, CHUNK, h, dv).astype(jnp.float32)
    la = log_a.reshape(b, nc, CHUNK, h, dk)
    cum = jnp.cumsum(la, axis=2)
    total = cum[:, :, -1]
    k_dec = kc * jnp.exp(total[:, :, None] - cum)
    u = jnp.einsum('bnchk,bnchv->nbhkv', k_dec, vc)
    decay = jnp.transpose(jnp.exp(total), (1, 0, 2, 3))

    def step(state, inp):
        d_n, u_n = inp
        state = d_n[..., None] * state + u_n
        return state, state

    init = jnp.zeros((b, h, dk, dv), jnp.float32)
    _, states = lax.scan(step, init, (decay, u))
    o = jnp.einsum('bnchk,nbhkv->bnchv', qc, states)
    return o.reshape(b, s, h, dv).astype(v.dtype)


def token_mixer(h, w_in, b_fgate, w_gla_a2, b_gla_a2, g_fox_out, g_gla_out, w_out):
    b, s, _ = h.shape
    proj = h @ w_in
    fq, fk, fv, ff, gq, gk, gv, ga, gr = jnp.split(proj, IN_OFFSETS, axis=-1)

    log_f = jax.nn.log_sigmoid((ff + b_fgate).astype(jnp.float32))
    fox = forgetting_attention(fq.reshape(b, s, FOX_HEADS, FOX_HEAD_DIM),
                               fk.reshape(b, s, FOX_HEADS, FOX_HEAD_DIM),
                               fv.reshape(b, s, FOX_HEADS, FOX_HEAD_DIM), log_f)
    fox = head_rmsnorm(fox, g_fox_out).reshape(b, s, FOX_WIDTH)

    log_a = jax.nn.log_sigmoid((ga @ w_gla_a2 + b_gla_a2).astype(jnp.float32)) / GLA_GATE_TEMP
    gla = gla_chunk_causal(gq.reshape(b, s, GLA_HEADS, GLA_DK),
                           gk.reshape(b, s, GLA_HEADS, GLA_DK),
                           gv.reshape(b, s, GLA_HEADS, GLA_DV),
                           log_a.reshape(b, s, GLA_HEADS, GLA_DK))
    gla = head_rmsnorm(gla, g_gla_out).reshape(b, s, GLA_WIDTH) * jax.nn.silu(gr)

    return jnp.concatenate([fox, gla], axis=-1) @ w_out


def squared_relu_mlp(h, w_mlp_in, w_mlp_out):
    return jnp.square(jax.nn.relu(h @ w_mlp_in)) @ w_mlp_out


def _fwd_setup_inputs(seed: int = 0) -> dict:
    key = jax.random.key(seed)
    ks = jax.random.split(key, 20)
    nrm = lambda k, shape, s: jax.random.normal(k, shape, jnp.float32) * s
    gain = lambda k, shape: 1.0 + nrm(k, shape, 0.02)
    L = DEPTH
    return {
        "x": nrm(ks[0], (BATCH, SEQ, D_MODEL), 1.0),
        "c": nrm(ks[1], (BATCH, D_MODEL), 1.0),
        "w_ada": nrm(ks[2], (L, D_MODEL, N_MOD * D_MODEL), 0.5 * D_MODEL ** -0.5),
        "b_ada": nrm(ks[3], (L, N_MOD * D_MODEL), 0.02),
        "g_pre_mix": gain(ks[4], (L, D_MODEL)),
        "g_post_mix": gain(ks[5], (L, D_MODEL)),
        "w_in": nrm(ks[6], (L, D_MODEL, D_IN_PROJ), D_MODEL ** -0.5),
        "b_fgate": 2.0 + nrm(ks[7], (L, FOX_HEADS), 0.1),
        "w_gla_a2": nrm(ks[8], (L, GLA_GATE_RANK, GLA_KEY_WIDTH), GLA_GATE_RANK ** -0.5),
        "b_gla_a2": nrm(ks[9], (L, GLA_KEY_WIDTH), 0.02),
        "g_fox_out": gain(ks[10], (L, FOX_HEADS, FOX_HEAD_DIM)),
        "g_gla_out": gain(ks[11], (L, GLA_HEADS, GLA_DV)),
        "w_out": nrm(ks[12], (L, D_MIX, D_MODEL), D_MIX ** -0.5),
        "g_pre_mlp": gain(ks[13], (L, D_MODEL)),
        "g_post_mlp": gain(ks[14], (L, D_MODEL)),
        "w_mlp_in": nrm(ks[15], (L, D_MODEL, D_FF), D_MODEL ** -0.5),
        "w_mlp_out": nrm(ks[16], (L, D_FF, D_MODEL), D_FF ** -0.5),
    }


def _fwd_reference(x, c, w_ada, b_ada, g_pre_mix, g_post_mix, w_in, b_fgate, w_gla_a2,
              b_gla_a2, g_fox_out, g_gla_out, w_out, g_pre_mlp, g_post_mlp,
              w_mlp_in, w_mlp_out):
    c_act = jax.nn.silu(c)
    for i in range(DEPTH):
        mod = (c_act @ w_ada[i] + b_ada[i])[:, None, :]
        shift_m, scale_m, gate_m, shift_f, scale_f, gate_f = jnp.split(mod, N_MOD, axis=-1)

        h = rmsnorm(x, g_pre_mix[i]) * (1.0 + scale_m) + shift_m
        y = token_mixer(h, w_in[i], b_fgate[i], w_gla_a2[i], b_gla_a2[i],
                        g_fox_out[i], g_gla_out[i], w_out[i])
        x = x + gate_m * rmsnorm(y, g_post_mix[i])

        h = rmsnorm(x, g_pre_mlp[i]) * (1.0 + scale_f) + shift_f
        y = squared_relu_mlp(h, w_mlp_in[i], w_mlp_out[i])
        x = x + gate_f * rmsnorm(y, g_post_mlp[i])
    return x


import jax as _jax
import jax.numpy as _jnp

TWIN_FORMAT = 'train_step'
FWD_PARAMS = ['x', 'c', 'w_ada', 'b_ada', 'g_pre_mix', 'g_post_mix', 'w_in', 'b_fgate', 'w_gla_a2', 'b_gla_a2', 'g_fox_out', 'g_gla_out', 'w_out', 'g_pre_mlp', 'g_post_mlp', 'w_mlp_in', 'w_mlp_out']
TWIN_WEIGHTS = ['w_ada', 'b_ada', 'g_pre_mix', 'g_post_mix', 'w_in', 'b_fgate', 'w_gla_a2', 'b_gla_a2', 'g_fox_out', 'g_gla_out', 'w_out', 'g_pre_mlp', 'g_post_mlp', 'w_mlp_in', 'w_mlp_out']
TWIN_DIFF_INPUT = 'x'
TWIN_INPUTS = ['x', 'c', 'w_ada', 'b_ada', 'g_pre_mix', 'g_post_mix', 'w_in', 'b_fgate', 'w_gla_a2', 'b_gla_a2', 'g_fox_out', 'g_gla_out', 'w_out', 'g_pre_mlp', 'g_post_mlp', 'w_mlp_in', 'w_mlp_out', 'loss_target', 'm_w_ada', 'm_b_ada', 'm_g_pre_mix', 'm_g_post_mix', 'm_w_in', 'm_b_fgate', 'm_w_gla_a2', 'm_b_gla_a2', 'm_g_fox_out', 'm_g_gla_out', 'm_w_out', 'm_g_pre_mlp', 'm_g_post_mlp', 'm_w_mlp_in', 'm_w_mlp_out', 'v_w_ada', 'v_b_ada', 'v_g_pre_mix', 'v_g_post_mix', 'v_w_in', 'v_b_fgate', 'v_w_gla_a2', 'v_b_gla_a2', 'v_g_fox_out', 'v_g_gla_out', 'v_w_out', 'v_g_pre_mlp', 'v_g_post_mlp', 'v_w_mlp_in', 'v_w_mlp_out']
TWIN_OUTPUTS = ['loss', 'grad_x', 'grad_w_ada', 'grad_b_ada', 'grad_g_pre_mix', 'grad_g_post_mix', 'grad_w_in', 'grad_b_fgate', 'grad_w_gla_a2', 'grad_b_gla_a2', 'grad_g_fox_out', 'grad_g_gla_out', 'grad_w_out', 'grad_g_pre_mlp', 'grad_g_post_mlp', 'grad_w_mlp_in', 'grad_w_mlp_out', 'delta_w_ada', 'delta_b_ada', 'delta_g_pre_mix', 'delta_g_post_mix', 'delta_w_in', 'delta_b_fgate', 'delta_w_gla_a2', 'delta_b_gla_a2', 'delta_g_fox_out', 'delta_g_gla_out', 'delta_w_out', 'delta_g_pre_mlp', 'delta_g_post_mlp', 'delta_w_mlp_in', 'delta_w_mlp_out', 'new_m_w_ada', 'new_m_b_ada', 'new_m_g_pre_mix', 'new_m_g_post_mix', 'new_m_w_in', 'new_m_b_fgate', 'new_m_w_gla_a2', 'new_m_b_gla_a2', 'new_m_g_fox_out', 'new_m_g_gla_out', 'new_m_w_out', 'new_m_g_pre_mlp', 'new_m_g_post_mlp', 'new_m_w_mlp_in', 'new_m_w_mlp_out', 'new_v_w_ada', 'new_v_b_ada', 'new_v_g_pre_mix', 'new_v_g_post_mix', 'new_v_w_in', 'new_v_b_fgate', 'new_v_w_gla_a2', 'new_v_b_gla_a2', 'new_v_g_fox_out', 'new_v_g_gla_out', 'new_v_w_out', 'new_v_g_pre_mlp', 'new_v_g_post_mlp', 'new_v_w_mlp_in', 'new_v_w_mlp_out']
TWIN_LEAF_KINDS = {'loss': 'loss', 'grad_x': 'grad_x', 'grad_w_ada': 'grad_w', 'grad_b_ada': 'grad_w', 'grad_g_pre_mix': 'grad_w', 'grad_g_post_mix': 'grad_w', 'grad_w_in': 'grad_w', 'grad_b_fgate': 'grad_w', 'grad_w_gla_a2': 'grad_w', 'grad_b_gla_a2': 'grad_w', 'grad_g_fox_out': 'grad_w', 'grad_g_gla_out': 'grad_w', 'grad_w_out': 'grad_w', 'grad_g_pre_mlp': 'grad_w', 'grad_g_post_mlp': 'grad_w', 'grad_w_mlp_in': 'grad_w', 'grad_w_mlp_out': 'grad_w', 'delta_w_ada': 'delta_w', 'delta_b_ada': 'delta_w', 'delta_g_pre_mix': 'delta_w', 'delta_g_post_mix': 'delta_w', 'delta_w_in': 'delta_w', 'delta_b_fgate': 'delta_w', 'delta_w_gla_a2': 'delta_w', 'delta_b_gla_a2': 'delta_w', 'delta_g_fox_out': 'delta_w', 'delta_g_gla_out': 'delta_w', 'delta_w_out': 'delta_w', 'delta_g_pre_mlp': 'delta_w', 'delta_g_post_mlp': 'delta_w', 'delta_w_mlp_in': 'delta_w', 'delta_w_mlp_out': 'delta_w', 'new_m_w_ada': 'new_m', 'new_m_b_ada': 'new_m', 'new_m_g_pre_mix': 'new_m', 'new_m_g_post_mix': 'new_m', 'new_m_w_in': 'new_m', 'new_m_b_fgate': 'new_m', 'new_m_w_gla_a2': 'new_m', 'new_m_b_gla_a2': 'new_m', 'new_m_g_fox_out': 'new_m', 'new_m_g_gla_out': 'new_m', 'new_m_w_out': 'new_m', 'new_m_g_pre_mlp': 'new_m', 'new_m_g_post_mlp': 'new_m', 'new_m_w_mlp_in': 'new_m', 'new_m_w_mlp_out': 'new_m', 'new_v_w_ada': 'new_v', 'new_v_b_ada': 'new_v', 'new_v_g_pre_mix': 'new_v', 'new_v_g_post_mix': 'new_v', 'new_v_w_in': 'new_v', 'new_v_b_fgate': 'new_v', 'new_v_w_gla_a2': 'new_v', 'new_v_b_gla_a2': 'new_v', 'new_v_g_fox_out': 'new_v', 'new_v_g_gla_out': 'new_v', 'new_v_w_out': 'new_v', 'new_v_g_pre_mlp': 'new_v', 'new_v_g_post_mlp': 'new_v', 'new_v_w_mlp_in': 'new_v', 'new_v_w_mlp_out': 'new_v'}


def _forward(args):
    return _fwd_reference(*[args[k] for k in FWD_PARAMS])


def _output_shape():
    def fwd():
        inp = _fwd_setup_inputs(0)
        return _fwd_reference(*[inp[k] for k in FWD_PARAMS])
    out = _jax.eval_shape(fwd)
    return out.shape, out.dtype

N_MICROBATCH = 1
ADAM_LR = 0.001
ADAM_B1 = 0.9
ADAM_B2 = 0.999
ADAM_EPS = 1e-08
ADAM_WD = 0.01
ADAM_STEP = 10
PER_EXAMPLE_BATCH_AXIS = {'x': 0, 'c': 0, 'loss_target': 0}
SHARED_INPUTS = []
_WEIGHT_DTYPES = {'w_ada': _jnp.float32, 'b_ada': _jnp.float32, 'g_pre_mix': _jnp.float32, 'g_post_mix': _jnp.float32, 'w_in': _jnp.float32, 'b_fgate': _jnp.float32, 'w_gla_a2': _jnp.float32, 'b_gla_a2': _jnp.float32, 'g_fox_out': _jnp.float32, 'g_gla_out': _jnp.float32, 'w_out': _jnp.float32, 'g_pre_mlp': _jnp.float32, 'g_post_mlp': _jnp.float32, 'w_mlp_in': _jnp.float32, 'w_mlp_out': _jnp.float32}
MOMENT_SCALE = {'w_ada': 6.672256e-01, 'b_ada': 1.401821e+00, 'g_pre_mix': 6.456156e-02, 'g_post_mix': 1.659762e+00, 'w_in': 1.057247e-01, 'b_fgate': 4.339997e-01, 'w_gla_a2': 8.806372e-03, 'b_gla_a2': 2.435132e-02, 'g_fox_out': 2.527232e-01, 'g_gla_out': 3.572290e-02, 'w_out': 1.907890e-01, 'g_pre_mlp': 6.141826e-02, 'g_post_mlp': 1.670116e+00, 'w_mlp_in': 4.261585e-02, 'w_mlp_out': 1.710698e-01}


def _to_microbatches(a, axis):
    t = _jnp.moveaxis(a, axis, 0)
    t = t.reshape((N_MICROBATCH, t.shape[0] // N_MICROBATCH) + t.shape[1:])
    return _jnp.moveaxis(t, 1, axis + 1)


def setup_inputs(seed: int = 0) -> dict:
    inp = _fwd_setup_inputs(seed)
    key = _jax.random.fold_in(_jax.random.key(seed), 7919)
    shape, _ = _output_shape()
    out = dict(inp)
    out["loss_target"] = _jax.random.normal(_jax.random.fold_in(key, 0), shape, _jnp.float32)
    for i, name in enumerate(TWIN_WEIGHTS):
        w = inp[name].astype(_jnp.float32)
        if MOMENT_SCALE is None:
            s = _jnp.sqrt(_jnp.mean(_jnp.square(w)) + 1e-30)
        else:
            s = MOMENT_SCALE[name]
        km, kv = _jax.random.split(_jax.random.fold_in(key, i + 1))
        out[name] = w
        out["m_" + name] = s * _jax.random.normal(km, w.shape, _jnp.float32)
        out["v_" + name] = (s * s) * _jax.random.uniform(kv, w.shape, _jnp.float32, 0.5, 1.5)
    if N_MICROBATCH > 1:
        for name, axis in PER_EXAMPLE_BATCH_AXIS.items():
            out[name] = _to_microbatches(out[name], axis)
    return {'x': out['x'], 'c': out['c'], 'w_ada': out['w_ada'], 'b_ada': out['b_ada'], 'g_pre_mix': out['g_pre_mix'], 'g_post_mix': out['g_post_mix'], 'w_in': out['w_in'], 'b_fgate': out['b_fgate'], 'w_gla_a2': out['w_gla_a2'], 'b_gla_a2': out['b_gla_a2'], 'g_fox_out': out['g_fox_out'], 'g_gla_out': out['g_gla_out'], 'w_out': out['w_out'], 'g_pre_mlp': out['g_pre_mlp'], 'g_post_mlp': out['g_post_mlp'], 'w_mlp_in': out['w_mlp_in'], 'w_mlp_out': out['w_mlp_out'], 'loss_target': out['loss_target'], 'm_w_ada': out['m_w_ada'], 'm_b_ada': out['m_b_ada'], 'm_g_pre_mix': out['m_g_pre_mix'], 'm_g_post_mix': out['m_g_post_mix'], 'm_w_in': out['m_w_in'], 'm_b_fgate': out['m_b_fgate'], 'm_w_gla_a2': out['m_w_gla_a2'], 'm_b_gla_a2': out['m_b_gla_a2'], 'm_g_fox_out': out['m_g_fox_out'], 'm_g_gla_out': out['m_g_gla_out'], 'm_w_out': out['m_w_out'], 'm_g_pre_mlp': out['m_g_pre_mlp'], 'm_g_post_mlp': out['m_g_post_mlp'], 'm_w_mlp_in': out['m_w_mlp_in'], 'm_w_mlp_out': out['m_w_mlp_out'], 'v_w_ada': out['v_w_ada'], 'v_b_ada': out['v_b_ada'], 'v_g_pre_mix': out['v_g_pre_mix'], 'v_g_post_mix': out['v_g_post_mix'], 'v_w_in': out['v_w_in'], 'v_b_fgate': out['v_b_fgate'], 'v_w_gla_a2': out['v_w_gla_a2'], 'v_b_gla_a2': out['v_b_gla_a2'], 'v_g_fox_out': out['v_g_fox_out'], 'v_g_gla_out': out['v_g_gla_out'], 'v_w_out': out['v_w_out'], 'v_g_pre_mlp': out['v_g_pre_mlp'], 'v_g_post_mlp': out['v_g_post_mlp'], 'v_w_mlp_in': out['v_w_mlp_in'], 'v_w_mlp_out': out['v_w_mlp_out']}


def _loss(weights, diff, rest, loss_target):
    with _jax.named_scope("forward"):
        args = {**rest, TWIN_DIFF_INPUT: diff, **{k: w.astype(_WEIGHT_DTYPES[k]) for k, w in weights.items()}}
        y = _forward(args)
    with _jax.named_scope("loss_head"):
        err = _jnp.square(y.astype(_jnp.float32) - loss_target)
        return 0.5 * _jnp.sum(_jnp.mean(err, axis=-1)) if err.ndim else 0.5 * err


def _adamw(w, g, m, v):
    m = ADAM_B1 * m + (1.0 - ADAM_B1) * g
    v = ADAM_B2 * v + (1.0 - ADAM_B2) * _jnp.square(g)
    m_hat = m / (1.0 - ADAM_B1 ** ADAM_STEP)
    v_hat = v / (1.0 - ADAM_B2 ** ADAM_STEP)
    delta = -ADAM_LR * (m_hat / (_jnp.sqrt(v_hat) + ADAM_EPS) + ADAM_WD * w)
    return delta, m, v


def reference(x, c, w_ada, b_ada, g_pre_mix, g_post_mix, w_in, b_fgate, w_gla_a2, b_gla_a2, g_fox_out, g_gla_out, w_out, g_pre_mlp, g_post_mlp, w_mlp_in, w_mlp_out, loss_target, m_w_ada, m_b_ada, m_g_pre_mix, m_g_post_mix, m_w_in, m_b_fgate, m_w_gla_a2, m_b_gla_a2, m_g_fox_out, m_g_gla_out, m_w_out, m_g_pre_mlp, m_g_post_mlp, m_w_mlp_in, m_w_mlp_out, v_w_ada, v_b_ada, v_g_pre_mix, v_g_post_mix, v_w_in, v_b_fgate, v_w_gla_a2, v_b_gla_a2, v_g_fox_out, v_g_gla_out, v_w_out, v_g_pre_mlp, v_g_post_mlp, v_w_mlp_in, v_w_mlp_out):
    given = dict(x=x, c=c, w_ada=w_ada, b_ada=b_ada, g_pre_mix=g_pre_mix, g_post_mix=g_post_mix, w_in=w_in, b_fgate=b_fgate, w_gla_a2=w_gla_a2, b_gla_a2=b_gla_a2, g_fox_out=g_fox_out, g_gla_out=g_gla_out, w_out=w_out, g_pre_mlp=g_pre_mlp, g_post_mlp=g_post_mlp, w_mlp_in=w_mlp_in, w_mlp_out=w_mlp_out, loss_target=loss_target, m_w_ada=m_w_ada, m_b_ada=m_b_ada, m_g_pre_mix=m_g_pre_mix, m_g_post_mix=m_g_post_mix, m_w_in=m_w_in, m_b_fgate=m_b_fgate, m_w_gla_a2=m_w_gla_a2, m_b_gla_a2=m_b_gla_a2, m_g_fox_out=m_g_fox_out, m_g_gla_out=m_g_gla_out, m_w_out=m_w_out, m_g_pre_mlp=m_g_pre_mlp, m_g_post_mlp=m_g_post_mlp, m_w_mlp_in=m_w_mlp_in, m_w_mlp_out=m_w_mlp_out, v_w_ada=v_w_ada, v_b_ada=v_b_ada, v_g_pre_mix=v_g_pre_mix, v_g_post_mix=v_g_post_mix, v_w_in=v_w_in, v_b_fgate=v_b_fgate, v_w_gla_a2=v_w_gla_a2, v_b_gla_a2=v_b_gla_a2, v_g_fox_out=v_g_fox_out, v_g_gla_out=v_g_gla_out, v_w_out=v_w_out, v_g_pre_mlp=v_g_pre_mlp, v_g_post_mlp=v_g_post_mlp, v_w_mlp_in=v_w_mlp_in, v_w_mlp_out=v_w_mlp_out)
    weights = {n: given[n] for n in TWIN_WEIGHTS}
    shared = {n: given[n] for n in SHARED_INPUTS}
    per_example = {n: given[n] for n in ['x', 'c']}
    grad_fn = _jax.value_and_grad(_loss, argnums=(0, 1))

    def one_microbatch(ex, loss_target):
        ex = dict(ex)
        diff = ex.pop(TWIN_DIFF_INPUT)
        return grad_fn(weights, diff, {**shared, **ex}, loss_target)

    if N_MICROBATCH == 1:
        loss, (grad_w, grad_x) = one_microbatch(per_example, given["loss_target"])
    else:
        def body(carry, xs):
            loss_sum, grad_sum = carry
            l_k, (gw_k, gx_k) = one_microbatch(xs[0], xs[1])
            with _jax.named_scope("update"):
                return (loss_sum + l_k, _jax.tree.map(_jnp.add, grad_sum, gw_k)), gx_k

        init = (_jnp.zeros((), _jnp.float32), _jax.tree.map(_jnp.zeros_like, weights))
        (loss, grad_w), grad_x = _jax.lax.scan(body, init, (per_example, given["loss_target"]))
    with _jax.named_scope("update"):
        delta_w, new_m, new_v = {}, {}, {}
        for n in TWIN_WEIGHTS:
            delta_w[n], new_m[n], new_v[n] = _adamw(weights[n], grad_w[n], given["m_" + n], given["v_" + n])
    return (loss, grad_x, *[grad_w[n] for n in TWIN_WEIGHTS], *[delta_w[n] for n in TWIN_WEIGHTS],
            *[new_m[n] for n in TWIN_WEIGHTS], *[new_v[n] for n in TWIN_WEIGHTS])
```

```python
import functools

import jax
import jax.numpy as jnp
from jax import lax
from jax.experimental import pallas as pl
from jax.experimental.pallas import tpu as pltpu

F32 = jnp.float32
BF16 = jnp.bfloat16
MESH = pl.DeviceIdType.MESH
HIGHEST = lax.Precision.HIGHEST

D_MODEL = 2048
FOX_HEADS = 8
FOX_HD = 128
FOX_W = FOX_HEADS * FOX_HD
GLA_HEADS = 4
GLA_DK = 128
GLA_DV = 256
GLA_KW = GLA_HEADS * GLA_DK
GLA_W = GLA_HEADS * GLA_DV
GLA_RANK = 16
GLA_TEMP = 16.0
CHUNK = 64
D_FF = 4 * D_MODEL
EPS = 1e-6
MAIN_W = 3 * FOX_W + 2 * GLA_KW + 2 * GLA_W
SMALL_W = 128
NEG = -1e30

ADAM_LR = 0.001
ADAM_B1 = 0.9
ADAM_B2 = 0.999
ADAM_EPS = 1e-08
ADAM_WD = 0.01
ADAM_STEP = 10

VMEM_LIMIT = 52 * 1024 * 1024
ROW_TILE = 256
FOX_TQ = 512
FOX_TK = 512
GLA_ROWS = 512
GATE_TS = 512
MM_T = 1024


def _cp(*sem):
    return pltpu.CompilerParams(dimension_semantics=sem, vmem_limit_bytes=VMEM_LIMIT)


def _dot_nn(a, b, precision=None):
    return jnp.dot(a, b, preferred_element_type=F32, precision=precision)


def _dot_nt(a, b, precision=None):
    return lax.dot_general(a, b, (((1,), (1,)), ((), ())), preferred_element_type=F32, precision=precision)


def _dot_tn(a, b, precision=None):
    return lax.dot_general(a, b, (((0,), (0,)), ((), ())), preferred_element_type=F32, precision=precision)


def _sigmoid(x):
    return 1.0 / (1.0 + jnp.exp(-x))


def _log_sigmoid(x):
    return jnp.minimum(x, 0.0) - jnp.log(1.0 + jnp.exp(-jnp.abs(x)))


def _mm(a, b, *, mode, out_dtypes, name, tm=None, tn=None, tk=None, extras=(), epi=None,
        out_shapes=None, out_specs=None):
    tm, tn, tk = tm or MM_T, tn or MM_T, tk or MM_T
    if mode == "nn":
        (m, k), n = a.shape, b.shape[1]
    elif mode == "nt":
        (m, k), n = a.shape, b.shape[0]
    else:
        (k, m), n = a.shape, b.shape[1]
    tm, tn, tk = min(tm, m), min(tn, n), min(tk, k)
    assert m % tm == 0 and n % tn == 0 and k % tk == 0, (name, m, n, k)
    nk = k // tk
    n_out, n_ex = len(out_dtypes), len(extras)
    if epi is None:
        epi = lambda acc: tuple(acc for _ in range(n_out))
    dot = {"nn": _dot_nn, "nt": _dot_nt, "tn": _dot_tn}[mode]

    def body(*refs):
        a_ref, b_ref = refs[0], refs[1]
        ex_refs = refs[2:2 + n_ex]
        o_refs = refs[2 + n_ex:2 + n_ex + n_out]
        part = dot(a_ref[...], b_ref[...])

        def finish(acc):
            outs = epi(acc, *[e[...] for e in ex_refs])
            for o_ref, val in zip(o_refs, outs):
                o_ref[...] = val.reshape(o_ref.shape).astype(o_ref.dtype)

        if nk == 1:
            finish(part)
        else:
            acc_ref = refs[-1]
            kk = pl.program_id(2)

            @pl.when(kk == 0)
            def _():
                acc_ref[...] = part

            @pl.when(kk > 0)
            def _():
                acc_ref[...] += part

            @pl.when(kk == nk - 1)
            def _():
                finish(acc_ref[...])

    if mode == "nn":
        a_spec = pl.BlockSpec((tm, tk), lambda i, j, kk: (i, kk))
        b_spec = pl.BlockSpec((tk, tn), lambda i, j, kk: (kk, j))
    elif mode == "nt":
        a_spec = pl.BlockSpec((tm, tk), lambda i, j, kk: (i, kk))
        b_spec = pl.BlockSpec((tn, tk), lambda i, j, kk: (j, kk))
    else:
        a_spec = pl.BlockSpec((tk, tm), lambda i, j, kk: (kk, i))
        b_spec = pl.BlockSpec((tk, tn), lambda i, j, kk: (kk, j))
    tile_spec = pl.BlockSpec((tm, tn), lambda i, j, kk: (i, j))
    if out_shapes is None:
        out_shapes = [jax.ShapeDtypeStruct((m, n), dt) for dt in out_dtypes]
    if out_specs is None:
        out_specs = [tile_spec for _ in out_dtypes]
    res = pl.pallas_call(
        body,
        grid=(m // tm, n // tn, nk),
        in_specs=[a_spec, b_spec] + [tile_spec for _ in extras],
        out_specs=out_specs,
        out_shape=out_shapes,
        scratch_shapes=[pltpu.VMEM((tm, tn), F32)] if nk > 1 else [],
        compiler_params=_cp("parallel", "parallel", "arbitrary"),
        name=name,
    )(a, b, *extras)
    return res


def _row_spec(ts, d):
    return pl.BlockSpec((ts, d), lambda i: (i, 0))


def _vec_spec(d):
    return pl.BlockSpec((1, d), lambda i: (0, 0))


def _pre_fwd(x, avec, shift, *, name):
    s, d = x.shape
    ts = min(ROW_TILE, s)

    def body(x_ref, a_ref, s_ref, h_ref):
        xv = x_ref[...]
        r = lax.rsqrt(jnp.mean(xv * xv, axis=-1, keepdims=True) + EPS)
        h_ref[...] = (xv * r * a_ref[...] + s_ref[...]).astype(BF16)

    return pl.pallas_call(
        body, grid=(s // ts,),
        in_specs=[_row_spec(ts, d), _vec_spec(d), _vec_spec(d)],
        out_specs=_row_spec(ts, d),
        out_shape=jax.ShapeDtypeStruct((s, d), BF16),
        compiler_params=_cp("parallel"), name=name,
    )(x, avec, shift)


def _post_fwd(x, y, gate, g, *, name):
    s, d = x.shape
    ts = min(ROW_TILE, s)

    def body(x_ref, y_ref, gate_ref, g_ref, o_ref):
        yv = y_ref[...]
        r = lax.rsqrt(jnp.mean(yv * yv, axis=-1, keepdims=True) + EPS)
        o_ref[...] = x_ref[...] + gate_ref[...] * (yv * r * g_ref[...])

    return pl.pallas_call(
        body, grid=(s // ts,),
        in_specs=[_row_spec(ts, d), _row_spec(ts, d), _vec_spec(d), _vec_spec(d)],
        out_specs=_row_spec(ts, d),
        out_shape=jax.ShapeDtypeStruct((s, d), F32),
        compiler_params=_cp("parallel"), name=name,
    )(x, y, gate, g)


def _post_fwd_loss(x, y, gate, g, target, *, name):
    s, d = x.shape
    ts = min(ROW_TILE, s)

    def body(x_ref, y_ref, gate_ref, g_ref, t_ref, dx_ref, loss_ref):
        yv = y_ref[...]
        r = lax.rsqrt(jnp.mean(yv * yv, axis=-1, keepdims=True) + EPS)
        diff = x_ref[...] + gate_ref[...] * (yv * r * g_ref[...]) - t_ref[...]
        dx_ref[...] = diff * (1.0 / d)

        @pl.when(pl.program_id(0) == 0)
        def _():
            loss_ref[...] = jnp.zeros_like(loss_ref)

        loss_ref[...] += jnp.sum(jnp.mean(diff * diff, axis=-1, keepdims=True)) * 0.5

    return pl.pallas_call(
        body, grid=(s // ts,),
        in_specs=[_row_spec(ts, d), _row_spec(ts, d), _vec_spec(d), _vec_spec(d), _row_spec(ts, d)],
        out_specs=[_row_spec(ts, d), pl.BlockSpec((1, 128), lambda i: (0, 0))],
        out_shape=[jax.ShapeDtypeStruct((s, d), F32), jax.ShapeDtypeStruct((1, 128), F32)],
        compiler_params=_cp("arbitrary"), name=name,
    )(x, y, gate, g, target)


def _post_bwd(dxo, y, gate, g, *, name):
    s, d = y.shape
    ts = min(ROW_TILE, s)

    def body(dx_ref, y_ref, gate_ref, g_ref, dy_ref, dgate_ref, dg_ref):
        yv, dxv, gv = y_ref[...], dx_ref[...], g_ref[...]
        r = lax.rsqrt(jnp.mean(yv * yv, axis=-1, keepdims=True) + EPS)
        yhat = yv * r
        dn = dxv * gate_ref[...]
        dyhat = dn * gv
        dy = r * (dyhat - yhat * jnp.mean(dyhat * yhat, axis=-1, keepdims=True))
        dy_ref[...] = dy.astype(BF16)

        @pl.when(pl.program_id(0) == 0)
        def _():
            dgate_ref[...] = jnp.zeros_like(dgate_ref)
            dg_ref[...] = jnp.zeros_like(dg_ref)

        dgate_ref[...] += jnp.sum(dxv * (yhat * gv), axis=0, keepdims=True)
        dg_ref[...] += jnp.sum(dn * yhat, axis=0, keepdims=True)

    return pl.pallas_call(
        body, grid=(s // ts,),
        in_specs=[_row_spec(ts, d), _row_spec(ts, d), _vec_spec(d), _vec_spec(d)],
        out_specs=[_row_spec(ts, d), _vec_spec(d), _vec_spec(d)],
        out_shape=[jax.ShapeDtypeStruct((s, d), BF16), jax.ShapeDtypeStruct((1, d), F32),
                   jax.ShapeDtypeStruct((1, d), F32)],
        compiler_params=_cp("arbitrary"), name=name,
    )(dxo, y, gate, g)


def _pre_bwd(dh, xin, dres, avec, *, name):
    s, d = xin.shape
    ts = min(ROW_TILE, s)

    def body(dh_ref, x_ref, dres_ref, a_ref, dx_ref, dshift_ref, da_ref):
        xv, dhv = x_ref[...], dh_ref[...]
        r = lax.rsqrt(jnp.mean(xv * xv, axis=-1, keepdims=True) + EPS)
        xhat = xv * r
        dxhat = dhv * a_ref[...]
        dx_ref[...] = dres_ref[...] + r * (dxhat - xhat * jnp.mean(dxhat * xhat, axis=-1, keepdims=True))

        @pl.when(pl.program_id(0) == 0)
        def _():
            dshift_ref[...] = jnp.zeros_like(dshift_ref)
            da_ref[...] = jnp.zeros_like(da_ref)

        dshift_ref[...] += jnp.sum(dhv, axis=0, keepdims=True)
        da_ref[...] += jnp.sum(dhv * xhat, axis=0, keepdims=True)

    return pl.pallas_call(
        body, grid=(s // ts,),
        in_specs=[_row_spec(ts, d), _row_spec(ts, d), _row_spec(ts, d), _vec_spec(d)],
        out_specs=[_row_spec(ts, d), _vec_spec(d), _vec_spec(d)],
        out_shape=[jax.ShapeDtypeStruct((s, d), F32), jax.ShapeDtypeStruct((1, d), F32),
                   jax.ShapeDtypeStruct((1, d), F32)],
        compiler_params=_cp("arbitrary"), name=name,
    )(dh, xin, dres, avec)


def _tri(n, strict=False, upper=False):
    r = lax.broadcasted_iota(jnp.int32, (n, n), 0)
    c = lax.broadcasted_iota(jnp.int32, (n, n), 1)
    if upper:
        r, c = c, r
    return ((r > c) if strict else (r >= c)).astype(F32)


def _gates_fwd(ps, bf, w2p, b2, *, name):
    s = ps.shape[0]
    ts = min(GATE_TS, s)

    def body(ps_ref, bf_ref, w_ref, b2_ref, cum_ref, la_ref, carry_ref):
        @pl.when(pl.program_id(0) == 0)
        def _():
            carry_ref[...] = jnp.zeros_like(carry_ref)

        psv = ps_ref[...]
        lf = _log_sigmoid(psv + bf_ref[...])
        cum = _dot_nn(_tri(ts), lf, HIGHEST) + carry_ref[...]
        cum_ref[...] = cum
        carry_ref[...] = cum[ts - 1:ts, :]
        z = _dot_nn(psv, w_ref[...], HIGHEST) + b2_ref[...]
        la_ref[...] = _log_sigmoid(z) * (1.0 / GLA_TEMP)

    return pl.pallas_call(
        body, grid=(s // ts,),
        in_specs=[_row_spec(ts, SMALL_W), _vec_spec(SMALL_W),
                  pl.BlockSpec((SMALL_W, GLA_KW), lambda i: (0, 0)), _vec_spec(GLA_KW)],
        out_specs=[_row_spec(ts, SMALL_W), _row_spec(ts, GLA_KW)],
        out_shape=[jax.ShapeDtypeStruct((s, SMALL_W), F32), jax.ShapeDtypeStruct((s, GLA_KW), F32)],
        scratch_shapes=[pltpu.VMEM((1, SMALL_W), F32)],
        compiler_params=_cp("arbitrary"), name=name,
    )(ps, bf, w2p, b2)


def _gates_bwd(dck, ps, bf, w2p, b2, dla, *, name):
    s = ps.shape[0]
    ts = min(GATE_TS, s)
    nb = s // ts
    rev = lambda i: (nb - 1 - i, 0)

    def body(dck_ref, ps_ref, bf_ref, w_ref, b2_ref, dla_ref, dps_ref, dbf_ref, dw_ref, db2_ref, carry_ref):
        @pl.when(pl.program_id(0) == 0)
        def _():
            carry_ref[...] = jnp.zeros_like(carry_ref)
            dbf_ref[...] = jnp.zeros_like(dbf_ref)
            dw_ref[...] = jnp.zeros_like(dw_ref)
            db2_ref[...] = jnp.zeros_like(db2_ref)

        psv, dckv = ps_ref[...], dck_ref[...]
        dlf = _dot_nn(_tri(ts, upper=True), dckv, HIGHEST) + carry_ref[...]
        carry_ref[...] += jnp.sum(dckv, axis=0, keepdims=True)
        lane = lax.broadcasted_iota(jnp.int32, (ts, SMALL_W), 1)
        dff = jnp.where(lane < FOX_HEADS, dlf * _sigmoid(-(psv + bf_ref[...])), 0.0)
        z = _dot_nn(psv, w_ref[...], HIGHEST) + b2_ref[...]
        dz = dla_ref[...] * _sigmoid(-z) * (1.0 / GLA_TEMP)
        dps_ref[...] = (_dot_nt(dz, w_ref[...], HIGHEST) + dff).astype(BF16)
        dbf_ref[...] += jnp.sum(dff, axis=0, keepdims=True)
        dw_ref[...] += _dot_tn(psv, dz, HIGHEST)
        db2_ref[...] += jnp.sum(dz, axis=0, keepdims=True)

    return pl.pallas_call(
        body, grid=(nb,),
        in_specs=[pl.BlockSpec((ts, SMALL_W), rev), pl.BlockSpec((ts, SMALL_W), rev), _vec_spec(SMALL_W),
                  pl.BlockSpec((SMALL_W, GLA_KW), lambda i: (0, 0)), _vec_spec(GLA_KW),
                  pl.BlockSpec((ts, GLA_KW), rev)],
        out_specs=[pl.BlockSpec((ts, SMALL_W), rev), _vec_spec(SMALL_W),
                   pl.BlockSpec((SMALL_W, GLA_KW), lambda i: (0, 0)), _vec_spec(GLA_KW)],
        out_shape=[jax.ShapeDtypeStruct((s, SMALL_W), BF16), jax.ShapeDtypeStruct((1, SMALL_W), F32),
                   jax.ShapeDtypeStruct((SMALL_W, GLA_KW), F32), jax.ShapeDtypeStruct((1, GLA_KW), F32)],
        scratch_shapes=[pltpu.VMEM((1, SMALL_W), F32)],
        compiler_params=_cp("arbitrary"), name=name,
    )(dck, ps, bf, w2p, b2, dla)


def _causal_mask(i, j, tq, tk):
    rows = i * tq + lax.broadcasted_iota(jnp.int32, (tq, tk), 0)
    cols = j * tk + lax.broadcasted_iota(jnp.int32, (tq, tk), 1)
    return rows >= cols


def _hs(h, hd=FOX_HD):
    return slice(h * hd, (h + 1) * hd)


def _fox_fwd(proj, cum, cum_t, g_fox, *, name):
    s = proj.shape[0]
    tq, tk = min(FOX_TQ, s), min(FOX_TK, s)
    scale = FOX_HD ** -0.5

    def body(q_ref, k_ref, v_ref, cq_ref, ck_ref, g_ref, o_ref, n_ref, lse_ref, m_sc, l_sc, acc_sc):
        i, j = pl.program_id(0), pl.program_id(1)

        @pl.when(j == 0)
        def _():
            m_sc[...] = jnp.full_like(m_sc, NEG)
            l_sc[...] = jnp.zeros_like(l_sc)
            acc_sc[...] = jnp.zeros_like(acc_sc)

        @pl.when(j <= i)
        def _():
            mask = _causal_mask(i, j, tq, tk)
            for h in range(FOX_HEADS):
                sc = _dot_nt(q_ref[:, _hs(h)], k_ref[:, _hs(h)]) * scale
                sc = sc + (cq_ref[:, h:h + 1] - ck_ref[h:h + 1, :])
                sc = jnp.where(mask, sc, NEG)
                m_prev = m_sc[h]
                m_new = jnp.maximum(m_prev, jnp.max(sc, axis=-1, keepdims=True))
                alpha = jnp.exp(m_prev - m_new)
                p = jnp.exp(sc - m_new)
                l_sc[h] = alpha * l_sc[h] + jnp.sum(p, axis=-1, keepdims=True)
                acc_sc[:, _hs(h)] = alpha * acc_sc[:, _hs(h)] + _dot_nn(p.astype(BF16), v_ref[:, _hs(h)])
                m_sc[h] = m_new

        @pl.when(j == i)
        def _():
            lane = lax.broadcasted_iota(jnp.int32, (tq, 128), 1)
            lse = jnp.zeros((tq, 128), F32)
            for h in range(FOX_HEADS):
                o = acc_sc[:, _hs(h)] / l_sc[h]
                o_ref[:, _hs(h)] = o
                r = lax.rsqrt(jnp.mean(o * o, axis=-1, keepdims=True) + EPS)
                n_ref[:, _hs(h)] = (o * r * g_ref[h:h + 1, :]).astype(BF16)
                lse = jnp.where(lane == h, m_sc[h] + jnp.log(l_sc[h]), lse)
            lse_ref[...] = lse

    kv = lambda col: (lambda i, j: (jnp.minimum(j, i), col))
    return pl.pallas_call(
        body, grid=(s // tq, s // tk),
        in_specs=[pl.BlockSpec((tq, FOX_W), lambda i, j: (i, 0)),
                  pl.BlockSpec((tk, FOX_W), kv(1)),
                  pl.BlockSpec((tk, FOX_W), kv(2)),
                  pl.BlockSpec((tq, 128), lambda i, j: (i, 0)),
                  pl.BlockSpec((FOX_HEADS, tk), lambda i, j: (0, jnp.minimum(j, i))),
                  pl.BlockSpec((FOX_HEADS, FOX_HD), lambda i, j: (0, 0))],
        out_specs=[pl.BlockSpec((tq, FOX_W), lambda i, j: (i, 0)),
                   pl.BlockSpec((tq, FOX_W), lambda i, j: (i, 0)),
                   pl.BlockSpec((tq, 128), lambda i, j: (i, 0))],
        out_shape=[jax.ShapeDtypeStruct((s, FOX_W), F32), jax.ShapeDtypeStruct((s, FOX_W), BF16),
                   jax.ShapeDtypeStruct((s, 128), F32)],
        scratch_shapes=[pltpu.VMEM((FOX_HEADS, tq, 1), F32), pltpu.VMEM((FOX_HEADS, tq, 1), F32),
                        pltpu.VMEM((tq, FOX_W), F32)],
        compiler_params=_cp("parallel", "arbitrary"), name=name,
    )(proj, proj, proj, cum, cum_t, g_fox)


def _fox_p_ds(q, k, v, do, cq, ck, lse, delta, mask, scale):
    sc = _dot_nt(q, k) * scale + (cq - ck)
    sc = jnp.where(mask, sc, NEG)
    p = jnp.exp(sc - lse)
    dp = _dot_nt(do, v)
    return p, p * (dp - delta)


def _fox_bwd_dq(proj, do, cum, cum_t, lse, delta, *, name):
    s = proj.shape[0]
    tq, tk = min(FOX_TQ, s), min(FOX_TK, s)
    scale = FOX_HD ** -0.5

    def body(q_ref, k_ref, v_ref, do_ref, cq_ref, ck_ref, lse_ref, dl_ref, dq_ref, dcq_ref, acc_sc, row_sc):
        i, j = pl.program_id(0), pl.program_id(1)

        @pl.when(j == 0)
        def _():
            acc_sc[...] = jnp.zeros_like(acc_sc)
            row_sc[...] = jnp.zeros_like(row_sc)

        @pl.when(j <= i)
        def _():
            mask = _causal_mask(i, j, tq, tk)
            for h in range(FOX_HEADS):
                _, ds = _fox_p_ds(q_ref[:, _hs(h)], k_ref[:, _hs(h)], v_ref[:, _hs(h)], do_ref[:, _hs(h)],
                                  cq_ref[:, h:h + 1], ck_ref[h:h + 1, :], lse_ref[:, h:h + 1],
                                  dl_ref[:, h:h + 1], mask, scale)
                acc_sc[:, _hs(h)] += _dot_nn(ds.astype(BF16), k_ref[:, _hs(h)])
                row_sc[h] += jnp.sum(ds, axis=-1, keepdims=True)

        @pl.when(j == i)
        def _():
            dq_ref[...] = (acc_sc[...] * scale).astype(BF16)
            lane = lax.broadcasted_iota(jnp.int32, (tq, 128), 1)
            dcq = jnp.zeros((tq, 128), F32)
            for h in range(FOX_HEADS):
                dcq = jnp.where(lane == h, row_sc[h], dcq)
            dcq_ref[...] = dcq

    kv = lambda col: (lambda i, j: (jnp.minimum(j, i), col))
    qrow = lambda i, j: (i, 0)
    return pl.pallas_call(
        body, grid=(s // tq, s // tk),
        in_specs=[pl.BlockSpec((tq, FOX_W), qrow), pl.BlockSpec((tk, FOX_W), kv(1)), pl.BlockSpec((tk, FOX_W), kv(2)),
                  pl.BlockSpec((tq, FOX_W), qrow), pl.BlockSpec((tq, 128), qrow),
                  pl.BlockSpec((FOX_HEADS, tk), lambda i, j: (0, jnp.minimum(j, i))),
                  pl.BlockSpec((tq, 128), qrow), pl.BlockSpec((tq, 128), qrow)],
        out_specs=[pl.BlockSpec((tq, FOX_W), qrow), pl.BlockSpec((tq, 128), qrow)],
        out_shape=[jax.ShapeDtypeStruct((s, FOX_W), BF16), jax.ShapeDtypeStruct((s, 128), F32)],
        scratch_shapes=[pltpu.VMEM((tq, FOX_W), F32), pltpu.VMEM((FOX_HEADS, tq, 1), F32)],
        compiler_params=_cp("parallel", "arbitrary"), name=name,
    )(proj, proj, proj, do, cum, cum_t, lse, delta)


def _fox_bwd_dkv(proj, do, cum, cum_t, lse, delta, *, name):
    s = proj.shape[0]
    tq, tk = min(FOX_TQ, s), min(FOX_TK, s)
    nq = s // tq
    scale = FOX_HD ** -0.5

    def body(q_ref, k_ref, v_ref, do_ref, cq_ref, ck_ref, lse_ref, dl_ref, dk_ref, dv_ref, dck_ref,
             dk_sc, dv_sc, dck_sc):
        j, i = pl.program_id(0), pl.program_id(1)

        @pl.when(i == 0)
        def _():
            dk_sc[...] = jnp.zeros_like(dk_sc)
            dv_sc[...] = jnp.zeros_like(dv_sc)
            dck_sc[...] = jnp.zeros_like(dck_sc)

        @pl.when(i >= j)
        def _():
            mask = _causal_mask(i, j, tq, tk)
            for h in range(FOX_HEADS):
                p, ds = _fox_p_ds(q_ref[:, _hs(h)], k_ref[:, _hs(h)], v_ref[:, _hs(h)], do_ref[:, _hs(h)],
                                  cq_ref[:, h:h + 1], ck_ref[h:h + 1, :], lse_ref[:, h:h + 1],
                                  dl_ref[:, h:h + 1], mask, scale)
                dv_sc[:, _hs(h)] += _dot_tn(p.astype(BF16), do_ref[:, _hs(h)])
                dk_sc[:, _hs(h)] += _dot_tn(ds.astype(BF16), q_ref[:, _hs(h)])
                dck_sc[h:h + 1, :] -= jnp.sum(ds, axis=0, keepdims=True)

        @pl.when(i == nq - 1)
        def _():
            dk_ref[...] = (dk_sc[...] * scale).astype(BF16)
            dv_ref[...] = dv_sc[...].astype(BF16)
            dck_ref[...] = dck_sc[...]

    qrow = lambda j, i: (jnp.maximum(i, j), 0)
    krow = lambda col: (lambda j, i: (j, col))
    return pl.pallas_call(
        body, grid=(s // tk, nq),
        in_specs=[pl.BlockSpec((tq, FOX_W), qrow), pl.BlockSpec((tk, FOX_W), krow(1)),
                  pl.BlockSpec((tk, FOX_W), krow(2)),
                  pl.BlockSpec((tq, FOX_W), qrow), pl.BlockSpec((tq, 128), qrow),
                  pl.BlockSpec((FOX_HEADS, tk), lambda j, i: (0, j)),
                  pl.BlockSpec((tq, 128), qrow), pl.BlockSpec((tq, 128), qrow)],
        out_specs=[pl.BlockSpec((tk, FOX_W), lambda j, i: (j, 0)), pl.BlockSpec((tk, FOX_W), lambda j, i: (j, 0)),
                   pl.BlockSpec((FOX_HEADS, tk), lambda j, i: (0, j))],
        out_shape=[jax.ShapeDtypeStruct((s, FOX_W), BF16), jax.ShapeDtypeStruct((s, FOX_W), BF16),
                   jax.ShapeDtypeStruct((FOX_HEADS, s), F32)],
        scratch_shapes=[pltpu.VMEM((tk, FOX_W), F32), pltpu.VMEM((tk, FOX_W), F32),
                        pltpu.VMEM((FOX_HEADS, tk), F32)],
        compiler_params=_cp("parallel", "arbitrary"), name=name,
    )(proj, proj, proj, do, cum, cum_t, lse, delta)


def _head_norm_bwd(dn_in, o, g, gr_src, *, nh, hd, gr_col, name):
    s, w = o.shape
    ts = min(ROW_TILE, s)
    gated = gr_src is not None

    def body(*refs):
        if gated:
            dn_ref, o_ref, g_ref, gr_ref, do_ref, dgr_ref, dl_ref, dg_ref = refs
        else:
            dn_ref, o_ref, g_ref, do_ref, dl_ref, dg_ref = refs

        @pl.when(pl.program_id(0) == 0)
        def _():
            dg_ref[...] = jnp.zeros_like(dg_ref)

        lane = lax.broadcasted_iota(jnp.int32, (ts, 128), 1)
        delta = jnp.zeros((ts, 128), F32)
        for h in range(nh):
            sl = _hs(h, hd)
            ov = o_ref[:, sl]
            dnv = dn_ref[:, sl].astype(F32)
            gv = g_ref[h:h + 1, :]
            r = lax.rsqrt(jnp.mean(ov * ov, axis=-1, keepdims=True) + EPS)
            ohat = ov * r
            if gated:
                grv = gr_ref[:, sl].astype(F32)
                sig = _sigmoid(grv)
                dgr_ref[:, sl] = (dnv * (ohat * gv) * (sig * (1.0 + grv * (1.0 - sig)))).astype(BF16)
                dnv = dnv * (grv * sig)
            dg_ref[h:h + 1, :] += jnp.sum(dnv * ohat, axis=0, keepdims=True)
            dohat = dnv * gv
            do = r * (dohat - ohat * jnp.mean(dohat * ohat, axis=-1, keepdims=True))
            do_ref[:, sl] = do.astype(BF16)
            delta = jnp.where(lane == h, jnp.sum(do * ov, axis=-1, keepdims=True), delta)
        dl_ref[...] = delta

    in_specs = [_row_spec(ts, w), _row_spec(ts, w), pl.BlockSpec((nh, hd), lambda i: (0, 0))]
    args = [dn_in, o, g]
    out_specs = [_row_spec(ts, w)]
    out_shape = [jax.ShapeDtypeStruct((s, w), BF16)]
    if gated:
        in_specs.append(pl.BlockSpec((ts, w), lambda i: (i, gr_col)))
        args.append(gr_src)
        out_specs.append(_row_spec(ts, w))
        out_shape.append(jax.ShapeDtypeStruct((s, w), BF16))
    out_specs += [_row_spec(ts, 128), pl.BlockSpec((nh, hd), lambda i: (0, 0))]
    out_shape += [jax.ShapeDtypeStruct((s, 128), F32), jax.ShapeDtypeStruct((nh, hd), F32)]
    return pl.pallas_call(
        body, grid=(s // ts,), in_specs=in_specs, out_specs=out_specs, out_shape=out_shape,
        compiler_params=_cp("arbitrary"), name=name,
    )(*args)


GQ_BLK = 3 * FOX_W // GLA_DK
GK_BLK = GQ_BLK + GLA_HEADS
GV_BLK = (3 * FOX_W + 2 * GLA_KW) // GLA_DV
GR_BLK = GV_BLK + GLA_HEADS


def _gla_chunk_terms(la):
    cum = _dot_nn(_tri(CHUNK), la, HIGHEST)
    total = cum[CHUNK - 1:CHUNK, :]
    return jnp.exp(total - cum), jnp.exp(total)


def _gla_fwd(proj, log_a, g_gla, *, name):
    s = proj.shape[0]
    rows = min(GLA_ROWS, s)
    cb = rows // CHUNK
    nblk = s // rows
    scale = GLA_DK ** -0.5

    def body(q_ref, k_ref, v_ref, gr_ref, la_ref, g_ref, o_ref, n_ref, st_ref, st_sc):
        h = pl.program_id(0)

        @pl.when(pl.program_id(1) == 0)
        def _():
            st_sc[...] = jnp.zeros_like(st_sc)

        gv = g_ref[pl.ds(h, 1), :]
        for ci in range(cb):
            sl = slice(ci * CHUNK, (ci + 1) * CHUNK)
            e, dec = _gla_chunk_terms(la_ref[sl, :])
            k_dec = (k_ref[sl, :].astype(F32) * e).astype(BF16)
            st = st_sc[...] * dec + _dot_tn(v_ref[sl, :], k_dec)
            st_sc[...] = st
            st_ref[0, ci] = st
            qs = (q_ref[sl, :].astype(F32) * scale).astype(BF16)
            o = _dot_nt(qs, st.astype(BF16))
            o_ref[sl, :] = o
            r = lax.rsqrt(jnp.mean(o * o, axis=-1, keepdims=True) + EPS)
            grv = gr_ref[sl, :].astype(F32)
            n_ref[sl, :] = (o * r * gv * (grv * _sigmoid(grv))).astype(BF16)

    return pl.pallas_call(
        body, grid=(GLA_HEADS, nblk),
        in_specs=[pl.BlockSpec((rows, GLA_DK), lambda h, n: (n, GQ_BLK + h)),
                  pl.BlockSpec((rows, GLA_DK), lambda h, n: (n, GK_BLK + h)),
                  pl.BlockSpec((rows, GLA_DV), lambda h, n: (n, GV_BLK + h)),
                  pl.BlockSpec((rows, GLA_DV), lambda h, n: (n, GR_BLK + h)),
                  pl.BlockSpec((rows, GLA_DK), lambda h, n: (n, h)),
                  pl.BlockSpec((GLA_HEADS, GLA_DV), lambda h, n: (0, 0))],
        out_specs=[pl.BlockSpec((rows, GLA_DV), lambda h, n: (n, h)),
                   pl.BlockSpec((rows, GLA_DV), lambda h, n: (n, h)),
                   pl.BlockSpec((1, cb, GLA_DV, GLA_DK), lambda h, n: (h, n, 0, 0))],
        out_shape=[jax.ShapeDtypeStruct((s, GLA_W), F32), jax.ShapeDtypeStruct((s, GLA_W), BF16),
                   jax.ShapeDtypeStruct((GLA_HEADS, s // CHUNK, GLA_DV, GLA_DK), F32)],
        scratch_shapes=[pltpu.VMEM((GLA_DV, GLA_DK), F32)],
        compiler_params=_cp("parallel", "arbitrary"), name=name,
    )(proj, proj, proj, proj, log_a, g_gla)


def _gla_bwd(proj, log_a, do, states, *, name):
    s = proj.shape[0]
    rows = min(GLA_ROWS, s)
    cb = rows // CHUNK
    nblk = s // rows
    scale = GLA_DK ** -0.5

    def body(q_ref, k_ref, v_ref, la_ref, do_ref, st_ref, prev_ref, dq_ref, dk_ref, dv_ref, dla_ref, g_sc):
        nrev = pl.program_id(1)
        blk = nblk - 1 - nrev

        @pl.when(nrev == 0)
        def _():
            g_sc[...] = jnp.zeros_like(g_sc)

        for ci in reversed(range(cb)):
            sl = slice(ci * CHUNK, (ci + 1) * CHUNK)
            e, dec = _gla_chunk_terms(la_ref[sl, :])
            kd = k_ref[sl, :].astype(F32) * e
            qs = (q_ref[sl, :].astype(F32) * scale).astype(BF16)
            dov = do_ref[sl, :]
            st = st_ref[0, ci]
            if ci > 0:
                st_prev = st_ref[0, ci - 1]
            else:
                st_prev = prev_ref[0, 0] * (blk > 0).astype(F32)
            dq_ref[sl, :] = (_dot_nn(dov, st.astype(BF16)) * scale).astype(BF16)
            gt = g_sc[...] + _dot_tn(dov, qs)
            gtb = gt.astype(BF16)
            dkd = _dot_nn(v_ref[sl, :], gtb)
            dv_ref[sl, :] = _dot_nt(kd.astype(BF16), gtb).astype(BF16)
            dk_ref[sl, :] = (dkd * e).astype(BF16)
            ddec = jnp.sum(gt * st_prev, axis=0, keepdims=True) * dec
            dla_ref[sl, :] = _dot_nn(_tri(CHUNK, strict=True), dkd * kd, HIGHEST) + ddec
            g_sc[...] = gt * dec

    rev = lambda col0: (lambda h, n: (nblk - 1 - n, col0 + h))
    return pl.pallas_call(
        body, grid=(GLA_HEADS, nblk),
        in_specs=[pl.BlockSpec((rows, GLA_DK), rev(GQ_BLK)),
                  pl.BlockSpec((rows, GLA_DK), rev(GK_BLK)),
                  pl.BlockSpec((rows, GLA_DV), rev(GV_BLK)),
                  pl.BlockSpec((rows, GLA_DK), rev(0)),
                  pl.BlockSpec((rows, GLA_DV), rev(0)),
                  pl.BlockSpec((1, cb, GLA_DV, GLA_DK), lambda h, n: (h, nblk - 1 - n, 0, 0)),
                  pl.BlockSpec((1, 1, GLA_DV, GLA_DK),
                               lambda h, n: (h, jnp.maximum((nblk - 1 - n) * cb - 1, 0), 0, 0))],
        out_specs=[pl.BlockSpec((rows, GLA_DK), rev(0)), pl.BlockSpec((rows, GLA_DK), rev(0)),
                   pl.BlockSpec((rows, GLA_DV), rev(0)), pl.BlockSpec((rows, GLA_DK), rev(0))],
        out_shape=[jax.ShapeDtypeStruct((s, GLA_KW), BF16), jax.ShapeDtypeStruct((s, GLA_KW), BF16),
                   jax.ShapeDtypeStruct((s, GLA_W), BF16), jax.ShapeDtypeStruct((s, GLA_KW), F32)],
        scratch_shapes=[pltpu.VMEM((GLA_DV, GLA_DK), F32)],
        compiler_params=_cp("parallel", "arbitrary"), name=name,
    )(proj, proj, proj, log_a, do, states, states)


def _row_tile(r):
    tr = min(ROW_TILE, r)
    while r % tr or tr % 8:
        tr -= 1
    return tr


def _adamw_math(w, g, m, v):
    m = ADAM_B1 * m + (1.0 - ADAM_B1) * g
    v = ADAM_B2 * v + (1.0 - ADAM_B2) * (g * g)
    m_hat = m / (1.0 - ADAM_B1 ** ADAM_STEP)
    v_hat = v / (1.0 - ADAM_B2 ** ADAM_STEP)
    delta = -ADAM_LR * (m_hat / (jnp.sqrt(v_hat) + ADAM_EPS) + ADAM_WD * w)
    return delta, m, v


def _adam(g, w, m, v, *, name):
    r, c = w.shape
    tr = _row_tile(r)
    assert r % tr == 0

    def body(g_ref, w_ref, m_ref, v_ref, d_ref, mo_ref, vo_ref):
        d, mn, vn = _adamw_math(w_ref[...], g_ref[...], m_ref[...], v_ref[...])
        d_ref[...] = d
        mo_ref[...] = mn
        vo_ref[...] = vn

    spec = pl.BlockSpec((tr, c), lambda i: (i, 0))
    return pl.pallas_call(
        body, grid=(r // tr,), in_specs=[spec] * 4, out_specs=[spec] * 3,
        out_shape=[jax.ShapeDtypeStruct((r, c), F32)] * 3,
        compiler_params=_cp("parallel"), name=name,
    )(g, w, m, v)


def _ada_grad_adam(c_all_t, dmod_cols, w, m, v, *, name):
    r, c = w.shape
    tr, tc = min(512, r), min(1024, c)

    def body(ct_ref, dm_ref, w_ref, m_ref, v_ref, g_ref, d_ref, mo_ref, vo_ref):
        g = _dot_nn(ct_ref[...], dm_ref[...], HIGHEST)
        g_ref[...] = g
        d, mn, vn = _adamw_math(w_ref[...], g, m_ref[...], v_ref[...])
        d_ref[...] = d
        mo_ref[...] = mn
        vo_ref[...] = vn

    spec = pl.BlockSpec((tr, tc), lambda i, j: (i, j))
    nb = c_all_t.shape[1]
    return pl.pallas_call(
        body, grid=(r // tr, c // tc),
        in_specs=[pl.BlockSpec((tr, nb), lambda i, j: (i, 0)), pl.BlockSpec((nb, tc), lambda i, j: (0, j)),
                  spec, spec, spec],
        out_specs=[spec] * 4, out_shape=[jax.ShapeDtypeStruct((r, c), F32)] * 4,
        compiler_params=_cp("parallel", "parallel"), name=name,
    )(c_all_t, dmod_cols, w, m, v)


def _mod_shard(c_all, w, b, *, name):
    k, c = w.shape
    tc = min(512, c)
    nb = c_all.shape[0]

    def body(c_ref, w_ref, b_ref, o_ref):
        o_ref[...] = _dot_nn(c_ref[...], w_ref[...], HIGHEST) + b_ref[...]

    return pl.pallas_call(
        body, grid=(c // tc,),
        in_specs=[pl.BlockSpec((nb, k), lambda j: (0, 0)), pl.BlockSpec((k, tc), lambda j: (0, j)),
                  pl.BlockSpec((1, tc), lambda j: (0, j))],
        out_specs=pl.BlockSpec((nb, tc), lambda j: (0, j)),
        out_shape=jax.ShapeDtypeStruct((nb, c), F32),
        compiler_params=_cp("parallel"), name=name,
    )(c_all, w, b)


def _silu_rows(c, *, name):
    def body(c_ref, o_ref):
        cv = c_ref[...]
        o_ref[...] = cv * _sigmoid(cv)

    return pl.pallas_call(body, out_shape=jax.ShapeDtypeStruct(c.shape, F32), name=name)(c)


def _pair_sum(a, b, *, name):
    p, r, c = a.shape
    tr = _row_tile(r)

    def body(a_ref, b_ref, o_ref):
        o_ref[...] = (a_ref[...].astype(F32) + b_ref[...].astype(F32)).astype(BF16)

    spec = pl.BlockSpec((1, tr, c), lambda i, j: (i, j, 0))
    return pl.pallas_call(
        body, grid=(p, r // tr), in_specs=[spec, spec], out_specs=spec,
        out_shape=jax.ShapeDtypeStruct((p, r, c), BF16),
        compiler_params=_cp("parallel", "parallel"), name=name,
    )(a, b)


def _stack_sum(x, *, name):
    p, r, c = x.shape
    tr = _row_tile(r)

    def body(x_ref, o_ref):
        acc = x_ref[0].astype(F32)
        for q in range(1, p):
            acc = acc + x_ref[q].astype(F32)
        o_ref[...] = acc

    return pl.pallas_call(
        body, grid=(r // tr,),
        in_specs=[pl.BlockSpec((p, tr, c), lambda i: (0, i, 0))],
        out_specs=pl.BlockSpec((tr, c), lambda i: (i, 0)),
        out_shape=jax.ShapeDtypeStruct((r, c), F32),
        compiler_params=_cp("parallel"), name=name,
    )(x)


def _place():
    x, y, c = lax.axis_index("x"), lax.axis_index("y"), lax.axis_index("c")
    chips = [(1 - x, y), (x, 1 - y), (1 - x, 1 - y)]
    return x, y, c, chips


def _gather8(x_shard, *, name):
    m_per, n = x_shard.shape

    def body(x_ref, out_ref, send_sems, recv_sems, local_sem):
        x, y, c, chips = _place()
        me, sibling = (x, y, c), (x, y, 1 - c)

        def rows(px, py, pc):
            return out_ref.at[pl.ds((4 * px + 2 * py + pc) * m_per, m_per), :]

        def copy(k, block, to, src=None):
            return pltpu.make_async_remote_copy(
                src_ref=rows(*block) if src is None else src, dst_ref=rows(*block),
                send_sem=send_sems.at[k], recv_sem=recv_sems.at[k], device_id=to, device_id_type=MESH)

        mine = pltpu.make_async_copy(x_ref, rows(*me), local_sem)
        mine.start()
        first = [copy(0, me, sibling, src=x_ref)]
        first += [copy(1 + j, me, (*chip, c), src=x_ref) for j, chip in enumerate(chips)]
        for cp in first:
            cp.start()
        passed = [copy(4 + j, (*chip, c), sibling) for j, chip in enumerate(chips)]
        for j, chip in enumerate(chips):
            copy(1 + j, (*chip, c), me).wait_recv()
            passed[j].start()
        copy(0, sibling, me).wait_recv()
        for j, chip in enumerate(chips):
            copy(4 + j, (*chip, 1 - c), me).wait_recv()
        for cp in first + passed:
            cp.wait_send()
        mine.wait()

    return pl.pallas_call(
        body,
        out_shape=jax.ShapeDtypeStruct((8 * m_per, n), x_shard.dtype),
        in_specs=[pl.BlockSpec(memory_space=pltpu.VMEM)],
        out_specs=pl.BlockSpec(memory_space=pltpu.VMEM),
        scratch_shapes=[pltpu.SemaphoreType.DMA((7,)), pltpu.SemaphoreType.DMA((7,)), pltpu.SemaphoreType.DMA],
        name=name,
    )(x_shard)


def _gather_weights(shards, *, name):
    nw = len(shards)

    def body(*refs):
        ins, outs = refs[:nw], refs[nw:2 * nw]
        send_sems, recv_sems, local_sems = refs[2 * nw:]
        x, y, c, chips = _place()
        sibling = (x, y, 1 - c)
        me_chip = 2 * x + y
        chip_ids = [2 * cx + cy for cx, cy in chips]

        def half(w, slot, hc):
            hr = ins[w].shape[0] // 2
            return outs[w].at[slot, pl.ds(hc * hr, hr), :]

        def copy(w, k, slot, hc, to, src=None):
            return pltpu.make_async_remote_copy(
                src_ref=half(w, slot, hc) if src is None else src, dst_ref=half(w, slot, hc),
                send_sem=send_sems.at[6 * w + k], recv_sem=recv_sems.at[6 * w + k],
                device_id=to, device_id_type=MESH)

        started = []
        for w in range(nw):
            hr = ins[w].shape[0] // 2
            loc = pltpu.make_async_copy(ins[w], outs[w].at[me_chip], local_sems.at[w])
            loc.start()
            started.append(loc)
        sends = []
        for w in range(nw):
            hr = ins[w].shape[0] // 2
            for j, chip in enumerate(chips):
                cp = copy(w, j, me_chip, c, (*chip, c), src=ins[w].at[pl.ds(c * hr, hr), :])
                cp.start()
                sends.append(cp)
        for w in range(nw):
            for j in range(3):
                copy(w, j, chip_ids[j], c, sibling).wait_recv()
                fwd = copy(w, 3 + j, chip_ids[j], c, sibling)
                fwd.start()
                sends.append(fwd)
        for w in range(nw):
            for j in range(3):
                copy(w, 3 + j, chip_ids[j], 1 - c, sibling).wait_recv()
        for cp in sends:
            cp.wait_send()
        for loc in started:
            loc.wait()

    any_spec = pl.BlockSpec(memory_space=pl.ANY)
    return pl.pallas_call(
        body,
        out_shape=[jax.ShapeDtypeStruct((4,) + s.shape, s.dtype) for s in shards],
        in_specs=[any_spec] * nw, out_specs=[any_spec] * nw,
        scratch_shapes=[pltpu.SemaphoreType.DMA((6 * nw,)), pltpu.SemaphoreType.DMA((6 * nw,)),
                        pltpu.SemaphoreType.DMA((nw,))],
        name=name,
    )(*shards)


def _pair_exchange(grads, *, name):
    nw = len(grads)

    def body(*refs):
        ins = refs[:nw]
        own, got = refs[nw:2 * nw], refs[2 * nw:3 * nw]
        send_sems, recv_sems, local_sems = refs[3 * nw:]
        x, y, c, _ = _place()
        sibling = (x, y, 1 - c)
        cps, locs = [], []
        for w in range(nw):
            hr = ins[w].shape[1] // 2
            loc = pltpu.make_async_copy(ins[w].at[:, pl.ds(c * hr, hr), :], own[w], local_sems.at[w])
            loc.start()
            locs.append(loc)
            cp = pltpu.make_async_remote_copy(
                src_ref=ins[w].at[:, pl.ds((1 - c) * hr, hr), :], dst_ref=got[w],
                send_sem=send_sems.at[w], recv_sem=recv_sems.at[w], device_id=sibling, device_id_type=MESH)
            cp.start()
            cps.append(cp)
        for cp in cps:
            cp.wait_recv()
        for cp in cps:
            cp.wait_send()
        for loc in locs:
            loc.wait()

    any_spec = pl.BlockSpec(memory_space=pl.ANY)
    halves = [jax.ShapeDtypeStruct((4, g.shape[1] // 2, g.shape[2]), g.dtype) for g in grads]
    res = pl.pallas_call(
        body, out_shape=halves + halves,
        in_specs=[any_spec] * nw, out_specs=[any_spec] * (2 * nw),
        scratch_shapes=[pltpu.SemaphoreType.DMA((nw,)), pltpu.SemaphoreType.DMA((nw,)),
                        pltpu.SemaphoreType.DMA((nw,))],
        name=name,
    )(*grads)
    return res[:nw], res[nw:]


def _shard_exchange(parts, *, name):
    nw = len(parts)

    def body(*refs):
        ins, outs = refs[:nw], refs[nw:2 * nw]
        send_sems, recv_sems, local_sems = refs[2 * nw:]
        x, y, c, chips = _place()
        me_chip = 2 * x + y
        chip_ids = [2 * cx + cy for cx, cy in chips]
        cps, locs = [], []
        for w in range(nw):
            loc = pltpu.make_async_copy(ins[w].at[me_chip], outs[w].at[me_chip], local_sems.at[w])
            loc.start()
            locs.append(loc)
            for j, chip in enumerate(chips):
                cp = pltpu.make_async_remote_copy(
                    src_ref=ins[w].at[chip_ids[j]], dst_ref=outs[w].at[me_chip],
                    send_sem=send_sems.at[3 * w + j], recv_sem=recv_sems.at[3 * w + j],
                    device_id=(*chip, c), device_id_type=MESH)
                cp.start()
                cps.append(cp)
        for w in range(nw):
            for j in range(3):
                pltpu.make_async_remote_copy(
                    src_ref=ins[w].at[chip_ids[j]], dst_ref=outs[w].at[chip_ids[j]],
                    send_sem=send_sems.at[3 * w + j], recv_sem=recv_sems.at[3 * w + j],
                    device_id=(*chips[j], c), device_id_type=MESH).wait_recv()
        for cp in cps:
            cp.wait_send()
        for loc in locs:
            loc.wait()

    any_spec = pl.BlockSpec(memory_space=pl.ANY)
    return pl.pallas_call(
        body, out_shape=[jax.ShapeDtypeStruct(p.shape, p.dtype) for p in parts],
        in_specs=[any_spec] * nw, out_specs=[any_spec] * nw,
        scratch_shapes=[pltpu.SemaphoreType.DMA((3 * nw,)), pltpu.SemaphoreType.DMA((3 * nw,)),
                        pltpu.SemaphoreType.DMA((nw,))],
        name=name,
    )(*parts)


def _half_exchange(halves, *, name):
    nw = len(halves)

    def body(*refs):
        ins, outs = refs[:nw], refs[nw:2 * nw]
        send_sems, recv_sems, local_sems = refs[2 * nw:]
        x, y, c, _ = _place()
        sibling = (x, y, 1 - c)
        cps, locs = [], []
        for w in range(nw):
            hr = ins[w].shape[0]
            mine = outs[w].at[pl.ds(c * hr, hr), :]
            loc = pltpu.make_async_copy(ins[w], mine, local_sems.at[w])
            loc.start()
            locs.append(loc)
            cp = pltpu.make_async_remote_copy(
                src_ref=ins[w], dst_ref=mine, send_sem=send_sems.at[w], recv_sem=recv_sems.at[w],
                device_id=sibling, device_id_type=MESH)
            cp.start()
            cps.append(cp)
        for w in range(nw):
            hr = ins[w].shape[0]
            theirs = outs[w].at[pl.ds((1 - c) * hr, hr), :]
            pltpu.make_async_remote_copy(
                src_ref=ins[w], dst_ref=theirs, send_sem=send_sems.at[w], recv_sem=recv_sems.at[w],
                device_id=sibling, device_id_type=MESH).wait_recv()
        for cp in cps:
            cp.wait_send()
        for loc in locs:
            loc.wait()

    any_spec = pl.BlockSpec(memory_space=pl.ANY)
    return pl.pallas_call(
        body, out_shape=[jax.ShapeDtypeStruct((2 * h.shape[0], h.shape[1]), h.dtype) for h in halves],
        in_specs=[any_spec] * nw, out_specs=[any_spec] * nw,
        scratch_shapes=[pltpu.SemaphoreType.DMA((nw,)), pltpu.SemaphoreType.DMA((nw,)),
                        pltpu.SemaphoreType.DMA((nw,))],
        name=name,
    )(*halves)


def _split_w_in(w_in_full):
    d = w_in_full.shape[0]
    main = jnp.concatenate([w_in_full[:, 0:3072], w_in_full[:, 3080:5128], w_in_full[:, 5144:6168]], axis=1)
    small = jnp.concatenate([w_in_full[:, 3072:3080], w_in_full[:, 5128:5144],
                             jnp.zeros((d, SMALL_W - 24), w_in_full.dtype)], axis=1)
    return main, small


def _merge_dw_in(dw_main, dw_small):
    return jnp.concatenate([dw_main[:, 0:3072], dw_small[:, 0:8], dw_main[:, 3072:5120], dw_small[:, 8:24],
                            dw_main[:, 5120:6144]], axis=1)


def _local_step(x, target, mod, g_pre_mix, g_post_mix, g_pre_mlp, g_post_mlp, w_in_full, b_fgate, w_gla_a2,
                b_gla_a2, g_fox, g_gla, w_out_full, w_mlp_in_full, w_mlp_out_full):
    s, d = x.shape
    shift_m, scale_m, gate_m, shift_f, scale_f, gate_f = [mod[:, i * d:(i + 1) * d] for i in range(6)]
    a1 = g_pre_mix * (1.0 + scale_m)
    a2 = g_pre_mlp * (1.0 + scale_f)
    w_main, w_small = _split_w_in(w_in_full)
    bf = jnp.concatenate([b_fgate, jnp.zeros((1, SMALL_W - FOX_HEADS), F32)], axis=1)
    w2p = jnp.zeros((SMALL_W, GLA_KW), F32).at[FOX_HEADS:FOX_HEADS + GLA_RANK].set(w_gla_a2)

    h1 = _pre_fwd(x, a1, shift_m, name="pre_mix_fwd")
    proj, = _mm(h1, w_main, mode="nn", out_dtypes=[BF16], name="in_proj_main")
    ps, = _mm(h1, w_small, mode="nn", out_dtypes=[F32], name="in_proj_small")
    cum, log_a = _gates_fwd(ps, bf, w2p, b_gla_a2, name="gates_fwd")
    cum_t = cum[:, :FOX_HEADS].T
    o_fox, fox_n, lse = _fox_fwd(proj, cum, cum_t, g_fox, name="fox_fwd")
    o_gla, gla_n, states = _gla_fwd(proj, log_a, g_gla, name="gla_fwd")
    mixed = jnp.concatenate([fox_n, gla_n], axis=1)
    y1, = _mm(mixed, w_out_full, mode="nn", out_dtypes=[F32], name="out_proj")
    x1 = _post_fwd(x, y1, gate_m, g_post_mix, name="post_mix_fwd")
    h2 = _pre_fwd(x1, a2, shift_f, name="pre_mlp_fwd")

    def mlp_act(acc):
        r = jnp.maximum(acc, 0.0)
        return acc, r * r

    u, act = _mm(h2, w_mlp_in_full, mode="nn", out_dtypes=[BF16, BF16], epi=mlp_act, name="mlp_in")
    y2, = _mm(act, w_mlp_out_full, mode="nn", out_dtypes=[F32], name="mlp_out")
    dx2, loss_part = _post_fwd_loss(x1, y2, gate_f, g_post_mlp, target, name="post_mlp_fwd_loss")

    dy2, dgate_f, dg_post_mlp = _post_bwd(dx2, y2, gate_f, g_post_mlp, name="post_mlp_bwd")
    dw_mlp_out, = _mm(act, dy2, mode="tn", out_dtypes=[BF16], name="dw_mlp_out")

    def act_bwd(acc, uv):
        return (acc * (2.0 * jnp.maximum(uv.astype(F32), 0.0)),)

    du, = _mm(dy2, w_mlp_out_full, mode="nt", out_dtypes=[BF16], extras=[u], epi=act_bwd, name="d_mlp_hidden")
    nj = D_FF // 4 // min(MM_T, D_FF // 4)
    tmw = min(MM_T, d)
    dw_mlp_in, = _mm(h2, du, mode="tn", out_dtypes=[BF16], name="dw_mlp_in",
                     out_shapes=[jax.ShapeDtypeStruct((4, d, D_FF // 4), BF16)],
                     out_specs=[pl.BlockSpec((1, tmw, min(MM_T, D_FF // 4)), lambda i, j, kk: (j // nj, i, j % nj))])
    dh2, = _mm(du, w_mlp_in_full, mode="nt", out_dtypes=[F32], name="d_mlp_in")
    dx1, dshift_f, da2 = _pre_bwd(dh2, x1, dx2, a2, name="pre_mlp_bwd")

    dy1, dgate_m, dg_post_mix = _post_bwd(dx1, y1, gate_m, g_post_mix, name="post_mix_bwd")
    dw_out, = _mm(mixed, dy1, mode="tn", out_dtypes=[BF16], name="dw_out")
    dmixed, = _mm(dy1, w_out_full, mode="nt", out_dtypes=[BF16], name="d_mixed")
    do_fox, delta, dg_fox = _head_norm_bwd(dmixed[:, :FOX_W], o_fox, g_fox, None, nh=FOX_HEADS, hd=FOX_HD,
                                           gr_col=0, name="fox_norm_bwd")
    do_gla, dgr, _, dg_gla = _head_norm_bwd(dmixed[:, FOX_W:], o_gla, g_gla, proj, nh=GLA_HEADS, hd=GLA_DV,
                                            gr_col=(3 * FOX_W + 2 * GLA_KW + GLA_W) // GLA_W, name="gla_norm_bwd")
    dq_fox, dcq = _fox_bwd_dq(proj, do_fox, cum, cum_t, lse, delta, name="fox_bwd_dq")
    dk_fox, dv_fox, dck_t = _fox_bwd_dkv(proj, do_fox, cum, cum_t, lse, delta, name="fox_bwd_dkv")
    dgq, dgk, dgv, dla = _gla_bwd(proj, log_a, do_gla, states, name="gla_bwd")
    dck = dcq + jnp.concatenate([dck_t.T, jnp.zeros((s, SMALL_W - FOX_HEADS), F32)], axis=1)
    dps, dbf, dw2p, db2 = _gates_bwd(dck, ps, bf, w2p, b_gla_a2, dla, name="gates_bwd")
    dproj = jnp.concatenate([dq_fox, dk_fox, dv_fox, dgq, dgk, dgv, dgr], axis=1)
    dw_main, = _mm(h1, dproj, mode="tn", out_dtypes=[BF16], name="dw_in_main")
    dw_small, = _mm(h1, dps, mode="tn", out_dtypes=[BF16], name="dw_in_small")
    dh1_small, = _mm(dps, w_small, mode="nt", out_dtypes=[F32], name="d_h1_small")
    dh1, = _mm(dproj, w_main, mode="nt", out_dtypes=[F32], extras=[dh1_small], epi=lambda acc, e: (acc + e,),
               name="d_h1")
    grad_x, dshift_m, da1 = _pre_bwd(dh1, x, dx1, a1, name="pre_mix_bwd")

    dmod = jnp.concatenate([dshift_m, da1 * g_pre_mix, dgate_m, dshift_f, da2 * g_pre_mlp, dgate_f], axis=1)
    small = dict(
        dmod=dmod, g_pre_mix=da1 * (1.0 + scale_m), g_post_mix=dg_post_mix, g_pre_mlp=da2 * (1.0 + scale_f),
        g_post_mlp=dg_post_mlp, b_fgate=dbf[:, :FOX_HEADS], w_gla_a2=dw2p[FOX_HEADS:FOX_HEADS + GLA_RANK],
        b_gla_a2=db2, g_fox_out=dg_fox, g_gla_out=dg_gla)
    big = dict(w_in=_merge_dw_in(dw_main, dw_small), w_out=dw_out, w_mlp_in=dw_mlp_in, w_mlp_out=dw_mlp_out)
    return loss_part, grad_x, big, small


def _pack(arrays):
    flat = jnp.concatenate([a.reshape(-1).astype(F32) for a in arrays])
    n = flat.shape[0]
    rows = -(-n // 128)
    rows = -(-rows // 8) * 8
    return jnp.pad(flat, (0, rows * 128 - n)).reshape(rows, 128)


def _unpack(buf, shapes):
    flat = buf.reshape(-1)
    out, off = [], 0
    for shp in shapes:
        n = 1
        for q in shp:
            n *= q
        out.append(flat[off:off + n].reshape(shp))
        off += n
    return out


SMALL_GRAD_ORDER = ["dmod", "g_pre_mix", "g_post_mix", "g_pre_mlp", "g_post_mlp", "b_fgate", "w_gla_a2", "b_gla_a2",
                    "g_fox_out", "g_gla_out"]


def kernel(x, c, w_ada, b_ada, g_pre_mix, g_post_mix, w_in, b_fgate, w_gla_a2, b_gla_a2, g_fox_out, g_gla_out, w_out, g_pre_mlp, g_post_mlp, w_mlp_in, w_mlp_out, loss_target, m_w_ada, m_b_ada, m_g_pre_mix, m_g_post_mix, m_w_in, m_b_fgate, m_w_gla_a2, m_b_gla_a2, m_g_fox_out, m_g_gla_out, m_w_out, m_g_pre_mlp, m_g_post_mlp, m_w_mlp_in, m_w_mlp_out, v_w_ada, v_b_ada, v_g_pre_mix, v_g_post_mix, v_w_in, v_b_fgate, v_w_gla_a2, v_b_gla_a2, v_g_fox_out, v_g_gla_out, v_w_out, v_g_pre_mlp, v_g_post_mlp, v_w_mlp_in, v_w_mlp_out):
    ix, iy, ic = lax.axis_index("x"), lax.axis_index("y"), lax.axis_index("c")
    chip = 2 * ix + iy
    dev = 4 * ix + 2 * iy + ic
    d = D_MODEL

    c_act = _silu_rows(c, name="silu_c")
    pack1 = _pack([c_act, w_gla_a2[0], g_gla_out[0]])
    rows1 = pack1.shape[0]
    got1 = _gather8(pack1, name="gather_small_fwd").reshape(8, rows1, 128)
    per_dev = [_unpack(got1[q], [(d,), (GLA_RANK, GLA_KW // 4), (GLA_HEADS, GLA_DV // 4)]) for q in range(8)]
    c_all = jnp.stack([p[0] for p in per_dev])
    w_gla_a2_full = jnp.concatenate([per_dev[2 * j][1] for j in range(4)], axis=1)
    g_gla_full = jnp.concatenate([per_dev[2 * j][2] for j in range(4)], axis=1)
    cols = w_ada.shape[2]
    b_ada_shard = lax.dynamic_slice_in_dim(b_ada, chip * cols, cols, axis=1)
    mod_sh = _mod_shard(c_all, w_ada[0], b_ada_shard, name="ada_mod")
    got2 = _gather8(mod_sh, name="gather_mod").reshape(8, 8, cols)
    mod_all = jnp.concatenate([got2[2 * j] for j in range(4)], axis=1)
    mod = lax.dynamic_slice_in_dim(mod_all, dev, 1, axis=0)

    gw_in, gw_out, gw_mlp_in, gw_mlp_out = _gather_weights(
        [w_in[0].astype(BF16), w_out[0].astype(BF16), w_mlp_in[0].astype(BF16), w_mlp_out[0].astype(BF16)],
        name="gather_weights")
    w_in_full = jnp.transpose(gw_in, (1, 0, 2)).reshape(d, -1)
    w_out_full = gw_out.reshape(-1, d)
    w_mlp_in_full = jnp.transpose(gw_mlp_in, (1, 0, 2)).reshape(d, -1)
    w_mlp_out_full = gw_mlp_out.reshape(-1, d)

    loss_part, grad_x, big, small = _local_step(
        x[0], loss_target[0], mod, g_pre_mix, g_post_mix, g_pre_mlp, g_post_mlp, w_in_full, b_fgate,
        w_gla_a2_full, b_gla_a2, g_fox_out[0], g_gla_full, w_out_full, w_mlp_in_full, w_mlp_out_full)
    loss = lax.psum(loss_part[0, 0], ("x", "y", "c"))

    cs_in = w_in.shape[2]
    g_stack = [
        jnp.transpose(big["w_in"].reshape(d, 4, cs_in), (1, 0, 2)),
        big["w_out"].reshape(4, d // 4, d),
        big["w_mlp_in"],
        big["w_mlp_out"].reshape(4, D_FF // 4, d),
    ]
    own, got = _pair_exchange(g_stack, name="grad_pair_exchange")
    chip_sums = [_pair_sum(a, b, name=f"grad_pair_sum_{q}") for q, (a, b) in enumerate(zip(own, got))]
    parts = _shard_exchange(chip_sums, name="grad_shard_exchange")
    halves = [_stack_sum(p, name=f"grad_chip_sum_{q}") for q, p in enumerate(parts)]
    g_big = _half_exchange(halves, name="grad_half_exchange")
    big_w = [(w_in, m_w_in, v_w_in), (w_out, m_w_out, v_w_out), (w_mlp_in, m_w_mlp_in, v_w_mlp_in),
             (w_mlp_out, m_w_mlp_out, v_w_mlp_out)]
    big_res = []
    for q, (g, (w, m, v)) in enumerate(zip(g_big, big_w)):
        dl, mn, vn = _adam(g, w[0], m[0], v[0], name=f"adam_big_{q}")
        big_res.append((g[None], dl[None], mn[None], vn[None]))

    pack2 = _pack([small[k] for k in SMALL_GRAD_ORDER])
    rows2 = pack2.shape[0]
    got3 = _gather8(pack2, name="gather_small_grads").reshape(8, rows2, 128)
    dmod_all = got3[:, :6 * d // 128, :].reshape(8, 6 * d)
    sums = _stack_sum(got3, name="small_grad_sum")
    shapes = [(1, 6 * d), (1, d), (1, d), (1, d), (1, d), (1, FOX_HEADS), (1, GLA_RANK, GLA_KW), (1, GLA_KW),
              (1, FOX_HEADS, FOX_HD), (1, GLA_HEADS, GLA_DV)]
    sg = dict(zip(["b_ada"] + SMALL_GRAD_ORDER[1:], _unpack(sums, shapes)))
    sg["w_gla_a2"] = lax.dynamic_slice_in_dim(sg["w_gla_a2"], chip * (GLA_KW // 4), GLA_KW // 4, axis=2)
    sg["g_gla_out"] = lax.dynamic_slice_in_dim(sg["g_gla_out"], chip * (GLA_DV // 4), GLA_DV // 4, axis=2)
    small_names = ["b_ada", "g_pre_mix", "g_post_mix", "b_fgate", "w_gla_a2", "b_gla_a2", "g_fox_out", "g_gla_out",
                   "g_pre_mlp", "g_post_mlp"]
    small_w = dict(b_ada=(b_ada, m_b_ada, v_b_ada), g_pre_mix=(g_pre_mix, m_g_pre_mix, v_g_pre_mix),
                   g_post_mix=(g_post_mix, m_g_post_mix, v_g_post_mix), b_fgate=(b_fgate, m_b_fgate, v_b_fgate),
                   w_gla_a2=(w_gla_a2, m_w_gla_a2, v_w_gla_a2), b_gla_a2=(b_gla_a2, m_b_gla_a2, v_b_gla_a2),
                   g_fox_out=(g_fox_out, m_g_fox_out, v_g_fox_out), g_gla_out=(g_gla_out, m_g_gla_out, v_g_gla_out),
                   g_pre_mlp=(g_pre_mlp, m_g_pre_mlp, v_g_pre_mlp), g_post_mlp=(g_post_mlp, m_g_post_mlp, v_g_post_mlp))
    sshapes = [small_w[k][0].shape for k in small_names]
    pg = _pack([sg[k] for k in small_names])
    pw, pm, pv = [_pack([small_w[k][q] for k in small_names]) for q in range(3)]
    pd, pmn, pvn = _adam(pg, pw, pm, pv, name="adam_small")
    s_delta = dict(zip(small_names, _unpack(pd, sshapes)))
    s_m = dict(zip(small_names, _unpack(pmn, sshapes)))
    s_v = dict(zip(small_names, _unpack(pvn, sshapes)))

    dmod_cols = lax.dynamic_slice_in_dim(dmod_all, chip * cols, cols, axis=1)
    g_ada, d_ada, m_ada, v_ada = _ada_grad_adam(c_all.T, dmod_cols, w_ada[0], m_w_ada[0], v_w_ada[0], name="ada_grad_adam")

    order = ["w_ada", "b_ada", "g_pre_mix", "g_post_mix", "w_in", "b_fgate", "w_gla_a2", "b_gla_a2", "g_fox_out",
             "g_gla_out", "w_out", "g_pre_mlp", "g_post_mlp", "w_mlp_in", "w_mlp_out"]
    res = {"w_ada": (g_ada[None], d_ada[None], m_ada[None], v_ada[None]),
           "w_in": big_res[0], "w_out": big_res[1], "w_mlp_in": big_res[2], "w_mlp_out": big_res[3]}
    for k in small_names:
        res[k] = (sg[k], s_delta[k], s_m[k], s_v[k])
    return (loss, grad_x[None], *[res[k][0] for k in order], *[res[k][1] for k in order],
            *[res[k][2] for k in order], *[res[k][3] for k in order])
```

```python
import functools

import jax
import jax.numpy as jnp
from jax import lax
from jax.experimental import pallas as pl
from jax.experimental.pallas import tpu as pltpu

F32 = jnp.float32
BF16 = jnp.bfloat16
MESH = pl.DeviceIdType.MESH
HIGHEST = lax.Precision.HIGHEST

D_MODEL = 2048
FOX_HEADS = 8
FOX_HD = 128
FOX_W = FOX_HEADS * FOX_HD
GLA_HEADS = 4
GLA_DK = 128
GLA_DV = 256
GLA_KW = GLA_HEADS * GLA_DK
GLA_W = GLA_HEADS * GLA_DV
GLA_RANK = 16
GLA_TEMP = 16.0
CHUNK = 64
D_FF = 4 * D_MODEL
EPS = 1e-6
MAIN_W = 3 * FOX_W + 2 * GLA_KW + 2 * GLA_W
SMALL_W = 128
NEG = -1e30

ADAM_LR = 0.001
ADAM_B1 = 0.9
ADAM_B2 = 0.999
ADAM_EPS = 1e-08
ADAM_WD = 0.01
ADAM_STEP = 10

VMEM_LIMIT = 52 * 1024 * 1024
ROW_TILE = 256
FOX_TQ = 512
FOX_TK = 512
GLA_ROWS = 512
GATE_TS = 512
MM_T = 1024


def _cp(*sem):
    return pltpu.CompilerParams(dimension_semantics=sem, vmem_limit_bytes=VMEM_LIMIT)


def _dot_nn(a, b, precision=None):
    return jnp.dot(a, b, preferred_element_type=F32, precision=precision)


def _dot_nt(a, b, precision=None):
    return lax.dot_general(a, b, (((1,), (1,)), ((), ())), preferred_element_type=F32, precision=precision)


def _dot_tn(a, b, precision=None):
    return lax.dot_general(a, b, (((0,), (0,)), ((), ())), preferred_element_type=F32, precision=precision)


def _sigmoid(x):
    return 1.0 / (1.0 + jnp.exp(-x))


def _log_sigmoid(x):
    return jnp.minimum(x, 0.0) - jnp.log(1.0 + jnp.exp(-jnp.abs(x)))


def _mm(a, b, *, mode, out_dtypes, name, tm=None, tn=None, tk=None, extras=(), epi=None,
        out_shapes=None, out_specs=None):
    tm, tn, tk = tm or MM_T, tn or MM_T, tk or MM_T
    if mode == "nn":
        (m, k), n = a.shape, b.shape[1]
    elif mode == "nt":
        (m, k), n = a.shape, b.shape[0]
    else:
        (k, m), n = a.shape, b.shape[1]
    tm, tn, tk = min(tm, m), min(tn, n), min(tk, k)
    assert m % tm == 0 and n % tn == 0 and k % tk == 0, (name, m, n, k)
    nk = k // tk
    n_out, n_ex = len(out_dtypes), len(extras)
    if epi is None:
        epi = lambda acc: tuple(acc for _ in range(n_out))
    dot = {"nn": _dot_nn, "nt": _dot_nt, "tn": _dot_tn}[mode]

    def body(*refs):
        a_ref, b_ref = refs[0], refs[1]
        ex_refs = refs[2:2 + n_ex]
        o_refs = refs[2 + n_ex:2 + n_ex + n_out]
        part = dot(a_ref[...], b_ref[...])

        def finish(acc):
            outs = epi(acc, *[e[...] for e in ex_refs])
            for o_ref, val in zip(o_refs, outs):
                o_ref[...] = val.reshape(o_ref.shape).astype(o_ref.dtype)

        if nk == 1:
            finish(part)
        else:
            acc_ref = refs[-1]
            kk = pl.program_id(2)

            @pl.when(kk == 0)
            def _():
                acc_ref[...] = part

            @pl.when(kk > 0)
            def _():
                acc_ref[...] += part

            @pl.when(kk == nk - 1)
            def _():
                finish(acc_ref[...])

    if mode == "nn":
        a_spec = pl.BlockSpec((tm, tk), lambda i, j, kk: (i, kk))
        b_spec = pl.BlockSpec((tk, tn), lambda i, j, kk: (kk, j))
    elif mode == "nt":
        a_spec = pl.BlockSpec((tm, tk), lambda i, j, kk: (i, kk))
        b_spec = pl.BlockSpec((tn, tk), lambda i, j, kk: (j, kk))
    else:
        a_spec = pl.BlockSpec((tk, tm), lambda i, j, kk: (kk, i))
        b_spec = pl.BlockSpec((tk, tn), lambda i, j, kk: (kk, j))
    tile_spec = pl.BlockSpec((tm, tn), lambda i, j, kk: (i, j))
    if out_shapes is None:
        out_shapes = [jax.ShapeDtypeStruct((m, n), dt) for dt in out_dtypes]
    if out_specs is None:
        out_specs = [tile_spec for _ in out_dtypes]
    res = pl.pallas_call(
        body,
        grid=(m // tm, n // tn, nk),
        in_specs=[a_spec, b_spec] + [tile_spec for _ in extras],
        out_specs=out_specs,
        out_shape=out_shapes,
        scratch_shapes=[pltpu.VMEM((tm, tn), F32)] if nk > 1 else [],
        compiler_params=_cp("parallel", "parallel", "arbitrary"),
        name=name,
    )(a, b, *extras)
    return res


def _row_spec(ts, d):
    return pl.BlockSpec((ts, d), lambda i: (i, 0))


def _vec_spec(d):
    return pl.BlockSpec((1, d), lambda i: (0, 0))


def _pre_fwd(x, avec, shift, *, name):
    s, d = x.shape
    ts = min(ROW_TILE, s)

    def body(x_ref, a_ref, s_ref, h_ref):
        xv = x_ref[...]
        r = lax.rsqrt(jnp.mean(xv * xv, axis=-1, keepdims=True) + EPS)
        h_ref[...] = (xv * r * a_ref[...] + s_ref[...]).astype(BF16)

    return pl.pallas_call(
        body, grid=(s // ts,),
        in_specs=[_row_spec(ts, d), _vec_spec(d), _vec_spec(d)],
        out_specs=_row_spec(ts, d),
        out_shape=jax.ShapeDtypeStruct((s, d), BF16),
        compiler_params=_cp("parallel"), name=name,
    )(x, avec, shift)


def _post_fwd(x, y, gate, g, *, name):
    s, d = x.shape
    ts = min(ROW_TILE, s)

    def body(x_ref, y_ref, gate_ref, g_ref, o_ref):
        yv = y_ref[...]
        r = lax.rsqrt(jnp.mean(yv * yv, axis=-1, keepdims=True) + EPS)
        o_ref[...] = x_ref[...] + gate_ref[...] * (yv * r * g_ref[...])

    return pl.pallas_call(
        body, grid=(s // ts,),
        in_specs=[_row_spec(ts, d), _row_spec(ts, d), _vec_spec(d), _vec_spec(d)],
        out_specs=_row_spec(ts, d),
        out_shape=jax.ShapeDtypeStruct((s, d), F32),
        compiler_params=_cp("parallel"), name=name,
    )(x, y, gate, g)


def _post_fwd_loss(x, y, gate, g, target, *, name):
    s, d = x.shape
    ts = min(ROW_TILE, s)

    def body(x_ref, y_ref, gate_ref, g_ref, t_ref, dx_ref, loss_ref):
        yv = y_ref[...]
        r = lax.rsqrt(jnp.mean(yv * yv, axis=-1, keepdims=True) + EPS)
        diff = x_ref[...] + gate_ref[...] * (yv * r * g_ref[...]) - t_ref[...]
        dx_ref[...] = diff * (1.0 / d)

        @pl.when(pl.program_id(0) == 0)
        def _():
            loss_ref[...] = jnp.zeros_like(loss_ref)

        loss_ref[...] += jnp.sum(jnp.mean(diff * diff, axis=-1, keepdims=True)) * 0.5

    return pl.pallas_call(
        body, grid=(s // ts,),
        in_specs=[_row_spec(ts, d), _row_spec(ts, d), _vec_spec(d), _vec_spec(d), _row_spec(ts, d)],
        out_specs=[_row_spec(ts, d), pl.BlockSpec((1, 128), lambda i: (0, 0))],
        out_shape=[jax.ShapeDtypeStruct((s, d), F32), jax.ShapeDtypeStruct((1, 128), F32)],
        compiler_params=_cp("arbitrary"), name=name,
    )(x, y, gate, g, target)


def _post_bwd(dxo, y, gate, g, *, name):
    s, d = y.shape
    ts = min(ROW_TILE, s)

    def body(dx_ref, y_ref, gate_ref, g_ref, dy_ref, dgate_ref, dg_ref):
        yv, dxv, gv = y_ref[...], dx_ref[...], g_ref[...]
        r = lax.rsqrt(jnp.mean(yv * yv, axis=-1, keepdims=True) + EPS)
        yhat = yv * r
        dn = dxv * gate_ref[...]
        dyhat = dn * gv
        dy = r * (dyhat - yhat * jnp.mean(dyhat * yhat, axis=-1, keepdims=True))
        dy_ref[...] = dy.astype(BF16)

        @pl.when(pl.program_id(0) == 0)
        def _():
            dgate_ref[...] = jnp.zeros_like(dgate_ref)
            dg_ref[...] = jnp.zeros_like(dg_ref)

        dgate_ref[...] += jnp.sum(dxv * (yhat * gv), axis=0, keepdims=True)
        dg_ref[...] += jnp.sum(dn * yhat, axis=0, keepdims=True)

    return pl.pallas_call(
        body, grid=(s // ts,),
        in_specs=[_row_spec(ts, d), _row_spec(ts, d), _vec_spec(d), _vec_spec(d)],
        out_specs=[_row_spec(ts, d), _vec_spec(d), _vec_spec(d)],
        out_shape=[jax.ShapeDtypeStruct((s, d), BF16), jax.ShapeDtypeStruct((1, d), F32),
                   jax.ShapeDtypeStruct((1, d), F32)],
        compiler_params=_cp("arbitrary"), name=name,
    )(dxo, y, gate, g)


def _pre_bwd(dh, xin, dres, avec, *, name):
    s, d = xin.shape
    ts = min(ROW_TILE, s)

    def body(dh_ref, x_ref, dres_ref, a_ref, dx_ref, dshift_ref, da_ref):
        xv, dhv = x_ref[...], dh_ref[...]
        r = lax.rsqrt(jnp.mean(xv * xv, axis=-1, keepdims=True) + EPS)
        xhat = xv * r
        dxhat = dhv * a_ref[...]
        dx_ref[...] = dres_ref[...] + r * (dxhat - xhat * jnp.mean(dxhat * xhat, axis=-1, keepdims=True))

        @pl.when(pl.program_id(0) == 0)
        def _():
            dshift_ref[...] = jnp.zeros_like(dshift_ref)
            da_ref[...] = jnp.zeros_like(da_ref)

        dshift_ref[...] += jnp.sum(dhv, axis=0, keepdims=True)
        da_ref[...] += jnp.sum(dhv * xhat, axis=0, keepdims=True)

    return pl.pallas_call(
        body, grid=(s // ts,),
        in_specs=[_row_spec(ts, d), _row_spec(ts, d), _row_spec(ts, d), _vec_spec(d)],
        out_specs=[_row_spec(ts, d), _vec_spec(d), _vec_spec(d)],
        out_shape=[jax.ShapeDtypeStruct((s, d), F32), jax.ShapeDtypeStruct((1, d), F32),
                   jax.ShapeDtypeStruct((1, d), F32)],
        compiler_params=_cp("arbitrary"), name=name,
    )(dh, xin, dres, avec)


def _tri(n, strict=False, upper=False):
    r = lax.broadcasted_iota(jnp.int32, (n, n), 0)
    c = lax.broadcasted_iota(jnp.int32, (n, n), 1)
    if upper:
        r, c = c, r
    return ((r > c) if strict else (r >= c)).astype(F32)


def _gates_fwd(ps, bf, w2p, b2, *, name):
    s = ps.shape[0]
    ts = min(GATE_TS, s)

    def body(ps_ref, bf_ref, w_ref, b2_ref, cum_ref, la_ref, carry_ref):
        @pl.when(pl.program_id(0) == 0)
        def _():
            carry_ref[...] = jnp.zeros_like(carry_ref)

        psv = ps_ref[...]
        lf = _log_sigmoid(psv + bf_ref[...])
        cum = _dot_nn(_tri(ts), lf, HIGHEST) + carry_ref[...]
        cum_ref[...] = cum
        carry_ref[...] = cum[ts - 1:ts, :]
        z = _dot_nn(psv, w_ref[...], HIGHEST) + b2_ref[...]
        la_ref[...] = _log_sigmoid(z) * (1.0 / GLA_TEMP)

    return pl.pallas_call(
        body, grid=(s // ts,),
        in_specs=[_row_spec(ts, SMALL_W), _vec_spec(SMALL_W),
                  pl.BlockSpec((SMALL_W, GLA_KW), lambda i: (0, 0)), _vec_spec(GLA_KW)],
        out_specs=[_row_spec(ts, SMALL_W), _row_spec(ts, GLA_KW)],
        out_shape=[jax.ShapeDtypeStruct((s, SMALL_W), F32), jax.ShapeDtypeStruct((s, GLA_KW), F32)],
        scratch_shapes=[pltpu.VMEM((1, SMALL_W), F32)],
        compiler_params=_cp("arbitrary"), name=name,
    )(ps, bf, w2p, b2)


def _gates_bwd(dck, ps, bf, w2p, b2, dla, *, name):
    s = ps.shape[0]
    ts = min(GATE_TS, s)
    nb = s // ts
    rev = lambda i: (nb - 1 - i, 0)

    def body(dck_ref, ps_ref, bf_ref, w_ref, b2_ref, dla_ref, dps_ref, dbf_ref, dw_ref, db2_ref, carry_ref):
        @pl.when(pl.program_id(0) == 0)
        def _():
            carry_ref[...] = jnp.zeros_like(carry_ref)
            dbf_ref[...] = jnp.zeros_like(dbf_ref)
            dw_ref[...] = jnp.zeros_like(dw_ref)
            db2_ref[...] = jnp.zeros_like(db2_ref)

        psv, dckv = ps_ref[...], dck_ref[...]
        dlf = _dot_nn(_tri(ts, upper=True), dckv, HIGHEST) + carry_ref[...]
        carry_ref[...] += jnp.sum(dckv, axis=0, keepdims=True)
        lane = lax.broadcasted_iota(jnp.int32, (ts, SMALL_W), 1)
        dff = jnp.where(lane < FOX_HEADS, dlf * _sigmoid(-(psv + bf_ref[...])), 0.0)
        z = _dot_nn(psv, w_ref[...], HIGHEST) + b2_ref[...]
        dz = dla_ref[...] * _sigmoid(-z) * (1.0 / GLA_TEMP)
        dps_ref[...] = (_dot_nt(dz, w_ref[...], HIGHEST) + dff).astype(BF16)
        dbf_ref[...] += jnp.sum(dff, axis=0, keepdims=True)
        dw_ref[...] += _dot_tn(psv, dz, HIGHEST)
        db2_ref[...] += jnp.sum(dz, axis=0, keepdims=True)

    return pl.pallas_call(
        body, grid=(nb,),
        in_specs=[pl.BlockSpec((ts, SMALL_W), rev), pl.BlockSpec((ts, SMALL_W), rev), _vec_spec(SMALL_W),
                  pl.BlockSpec((SMALL_W, GLA_KW), lambda i: (0, 0)), _vec_spec(GLA_KW),
                  pl.BlockSpec((ts, GLA_KW), rev)],
        out_specs=[pl.BlockSpec((ts, SMALL_W), rev), _vec_spec(SMALL_W),
                   pl.BlockSpec((SMALL_W, GLA_KW), lambda i: (0, 0)), _vec_spec(GLA_KW)],
        out_shape=[jax.ShapeDtypeStruct((s, SMALL_W), BF16), jax.ShapeDtypeStruct((1, SMALL_W), F32),
                   jax.ShapeDtypeStruct((SMALL_W, GLA_KW), F32), jax.ShapeDtypeStruct((1, GLA_KW), F32)],
        scratch_shapes=[pltpu.VMEM((1, SMALL_W), F32)],
        compiler_params=_cp("arbitrary"), name=name,
    )(dck, ps, bf, w2p, b2, dla)


def _causal_mask(i, j, tq, tk):
    rows = i * tq + lax.broadcasted_iota(jnp.int32, (tq, tk), 0)
    cols = j * tk + lax.broadcasted_iota(jnp.int32, (tq, tk), 1)
    return rows >= cols


def _hs(h, hd=FOX_HD):
    return slice(h * hd, (h + 1) * hd)


def _fox_fwd(proj, cum, cum_t, g_fox, *, name):
    s = proj.shape[0]
    tq, tk = min(FOX_TQ, s), min(FOX_TK, s)
    scale = FOX_HD ** -0.5

    def body(q_ref, k_ref, v_ref, cq_ref, ck_ref, g_ref, o_ref, n_ref, lse_ref, m_sc, l_sc, acc_sc):
        i, j = pl.program_id(0), pl.program_id(1)

        @pl.when(j == 0)
        def _():
            m_sc[...] = jnp.full_like(m_sc, NEG)
            l_sc[...] = jnp.zeros_like(l_sc)
            acc_sc[...] = jnp.zeros_like(acc_sc)

        @pl.when(j <= i)
        def _():
            mask = _causal_mask(i, j, tq, tk)
            for h in range(FOX_HEADS):
                sc = _dot_nt(q_ref[:, _hs(h)], k_ref[:, _hs(h)]) * scale
                sc = sc + (cq_ref[:, h:h + 1] - ck_ref[h:h + 1, :])
                sc = jnp.where(mask, sc, NEG)
                m_prev = m_sc[h]
                m_new = jnp.maximum(m_prev, jnp.max(sc, axis=-1, keepdims=True))
                alpha = jnp.exp(m_prev - m_new)
                p = jnp.exp(sc - m_new)
                l_sc[h] = alpha * l_sc[h] + jnp.sum(p, axis=-1, keepdims=True)
                acc_sc[:, _hs(h)] = alpha * acc_sc[:, _hs(h)] + _dot_nn(p.astype(BF16), v_ref[:, _hs(h)])
                m_sc[h] = m_new

        @pl.when(j == i)
        def _():
            lane = lax.broadcasted_iota(jnp.int32, (tq, 128), 1)
            lse = jnp.zeros((tq, 128), F32)
            for h in range(FOX_HEADS):
                o = acc_sc[:, _hs(h)] / l_sc[h]
                o_ref[:, _hs(h)] = o
                r = lax.rsqrt(jnp.mean(o * o, axis=-1, keepdims=True) + EPS)
                n_ref[:, _hs(h)] = (o * r * g_ref[h:h + 1, :]).astype(BF16)
                lse = jnp.where(lane == h, m_sc[h] + jnp.log(l_sc[h]), lse)
            lse_ref[...] = lse

    kv = lambda col: (lambda i, j: (jnp.minimum(j, i), col))
    return pl.pallas_call(
        body, grid=(s // tq, s // tk),
        in_specs=[pl.BlockSpec((tq, FOX_W), lambda i, j: (i, 0)),
                  pl.BlockSpec((tk, FOX_W), kv(1)),
                  pl.BlockSpec((tk, FOX_W), kv(2)),
                  pl.BlockSpec((tq, 128), lambda i, j: (i, 0)),
                  pl.BlockSpec((FOX_HEADS, tk), lambda i, j: (0, jnp.minimum(j, i))),
                  pl.BlockSpec((FOX_HEADS, FOX_HD), lambda i, j: (0, 0))],
        out_specs=[pl.BlockSpec((tq, FOX_W), lambda i, j: (i, 0)),
                   pl.BlockSpec((tq, FOX_W), lambda i, j: (i, 0)),
                   pl.BlockSpec((tq, 128), lambda i, j: (i, 0))],
        out_shape=[jax.ShapeDtypeStruct((s, FOX_W), F32), jax.ShapeDtypeStruct((s, FOX_W), BF16),
                   jax.ShapeDtypeStruct((s, 128), F32)],
        scratch_shapes=[pltpu.VMEM((FOX_HEADS, tq, 1), F32), pltpu.VMEM((FOX_HEADS, tq, 1), F32),
                        pltpu.VMEM((tq, FOX_W), F32)],
        compiler_params=_cp("parallel", "arbitrary"), name=name,
    )(proj, proj, proj, cum, cum_t, g_fox)


def _fox_p_ds(q, k, v, do, cq, ck, lse, delta, mask, scale):
    sc = _dot_nt(q, k) * scale + (cq - ck)
    sc = jnp.where(mask, sc, NEG)
    p = jnp.exp(sc - lse)
    dp = _dot_nt(do, v)
    return p, p * (dp - delta)


def _fox_bwd_dq(proj, do, cum, cum_t, lse, delta, *, name):
    s = proj.shape[0]
    tq, tk = min(FOX_TQ, s), min(FOX_TK, s)
    scale = FOX_HD ** -0.5

    def body(q_ref, k_ref, v_ref, do_ref, cq_ref, ck_ref, lse_ref, dl_ref, dq_ref, dcq_ref, acc_sc, row_sc):
        i, j = pl.program_id(0), pl.program_id(1)

        @pl.when(j == 0)
        def _():
            acc_sc[...] = jnp.zeros_like(acc_sc)
            row_sc[...] = jnp.zeros_like(row_sc)

        @pl.when(j <= i)
        def _():
            mask = _causal_mask(i, j, tq, tk)
            for h in range(FOX_HEADS):
                _, ds = _fox_p_ds(q_ref[:, _hs(h)], k_ref[:, _hs(h)], v_ref[:, _hs(h)], do_ref[:, _hs(h)],
                                  cq_ref[:, h:h + 1], ck_ref[h:h + 1, :], lse_ref[:, h:h + 1],
                                  dl_ref[:, h:h + 1], mask, scale)
                acc_sc[:, _hs(h)] += _dot_nn(ds.astype(BF16), k_ref[:, _hs(h)])
                row_sc[h] += jnp.sum(ds, axis=-1, keepdims=True)

        @pl.when(j == i)
        def _():
            dq_ref[...] = (acc_sc[...] * scale).astype(BF16)
            lane = lax.broadcasted_iota(jnp.int32, (tq, 128), 1)
            dcq = jnp.zeros((tq, 128), F32)
            for h in range(FOX_HEADS):
                dcq = jnp.where(lane == h, row_sc[h], dcq)
            dcq_ref[...] = dcq

    kv = lambda col: (lambda i, j: (jnp.minimum(j, i), col))
    qrow = lambda i, j: (i, 0)
    return pl.pallas_call(
        body, grid=(s // tq, s // tk),
        in_specs=[pl.BlockSpec((tq, FOX_W), qrow), pl.BlockSpec((tk, FOX_W), kv(1)), pl.BlockSpec((tk, FOX_W), kv(2)),
                  pl.BlockSpec((tq, FOX_W), qrow), pl.BlockSpec((tq, 128), qrow),
                  pl.BlockSpec((FOX_HEADS, tk), lambda i, j: (0, jnp.minimum(j, i))),
                  pl.BlockSpec((tq, 128), qrow), pl.BlockSpec((tq, 128), qrow)],
        out_specs=[pl.BlockSpec((tq, FOX_W), qrow), pl.BlockSpec((tq, 128), qrow)],
        out_shape=[jax.ShapeDtypeStruct((s, FOX_W), BF16), jax.ShapeDtypeStruct((s, 128), F32)],
        scratch_shapes=[pltpu.VMEM((tq, FOX_W), F32), pltpu.VMEM((FOX_HEADS, tq, 1), F32)],
        compiler_params=_cp("parallel", "arbitrary"), name=name,
    )(proj, proj, proj, do, cum, cum_t, lse, delta)


def _fox_bwd_dkv(proj, do, cum, cum_t, lse, delta, *, name):
    s = proj.shape[0]
    tq, tk = min(FOX_TQ, s), min(FOX_TK, s)
    nq = s // tq
    scale = FOX_HD ** -0.5

    def body(q_ref, k_ref, v_ref, do_ref, cq_ref, ck_ref, lse_ref, dl_ref, dk_ref, dv_ref, dck_ref,
             dk_sc, dv_sc, dck_sc):
        j, i = pl.program_id(0), pl.program_id(1)

        @pl.when(i == 0)
        def _():
            dk_sc[...] = jnp.zeros_like(dk_sc)
            dv_sc[...] = jnp.zeros_like(dv_sc)
            dck_sc[...] = jnp.zeros_like(dck_sc)

        @pl.when(i >= j)
        def _():
            mask = _causal_mask(i, j, tq, tk)
            for h in range(FOX_HEADS):
                p, ds = _fox_p_ds(q_ref[:, _hs(h)], k_ref[:, _hs(h)], v_ref[:, _hs(h)], do_ref[:, _hs(h)],
                                  cq_ref[:, h:h + 1], ck_ref[h:h + 1, :], lse_ref[:, h:h + 1],
                                  dl_ref[:, h:h + 1], mask, scale)
                dv_sc[:, _hs(h)] += _dot_tn(p.astype(BF16), do_ref[:, _hs(h)])
                dk_sc[:, _hs(h)] += _dot_tn(ds.astype(BF16), q_ref[:, _hs(h)])
                dck_sc[h:h + 1, :] -= jnp.sum(ds, axis=0, keepdims=True)

        @pl.when(i == nq - 1)
        def _():
            dk_ref[...] = (dk_sc[...] * scale).astype(BF16)
            dv_ref[...] = dv_sc[...].astype(BF16)
            dck_ref[...] = dck_sc[...]

    qrow = lambda j, i: (jnp.maximum(i, j), 0)
    krow = lambda col: (lambda j, i: (j, col))
    return pl.pallas_call(
        body, grid=(s // tk, nq),
        in_specs=[pl.BlockSpec((tq, FOX_W), qrow), pl.BlockSpec((tk, FOX_W), krow(1)),
                  pl.BlockSpec((tk, FOX_W), krow(2)),
                  pl.BlockSpec((tq, FOX_W), qrow), pl.BlockSpec((tq, 128), qrow),
                  pl.BlockSpec((FOX_HEADS, tk), lambda j, i: (0, j)),
                  pl.BlockSpec((tq, 128), qrow), pl.BlockSpec((tq, 128), qrow)],
        out_specs=[pl.BlockSpec((tk, FOX_W), lambda j, i: (j, 0)), pl.BlockSpec((tk, FOX_W), lambda j, i: (j, 0)),
                   pl.BlockSpec((FOX_HEADS, tk), lambda j, i: (0, j))],
        out_shape=[jax.ShapeDtypeStruct((s, FOX_W), BF16), jax.ShapeDtypeStruct((s, FOX_W), BF16),
                   jax.ShapeDtypeStruct((FOX_HEADS, s), F32)],
        scratch_shapes=[pltpu.VMEM((tk, FOX_W), F32), pltpu.VMEM((tk, FOX_W), F32),
                        pltpu.VMEM((FOX_HEADS, tk), F32)],
        compiler_params=_cp("parallel", "arbitrary"), name=name,
    )(proj, proj, proj, do, cum, cum_t, lse, delta)


def _head_norm_bwd(dn_in, o, g, gr_src, *, nh, hd, gr_col, name):
    s, w = o.shape
    ts = min(ROW_TILE, s)
    gated = gr_src is not None

    def body(*refs):
        if gated:
            dn_ref, o_ref, g_ref, gr_ref, do_ref, dgr_ref, dl_ref, dg_ref = refs
        else:
            dn_ref, o_ref, g_ref, do_ref, dl_ref, dg_ref = refs

        @pl.when(pl.program_id(0) == 0)
        def _():
            dg_ref[...] = jnp.zeros_like(dg_ref)

        lane = lax.broadcasted_iota(jnp.int32, (ts, 128), 1)
        delta = jnp.zeros((ts, 128), F32)
        for h in range(nh):
            sl = _hs(h, hd)
            ov = o_ref[:, sl]
            dnv = dn_ref[:, sl].astype(F32)
            gv = g_ref[h:h + 1, :]
            r = lax.rsqrt(jnp.mean(ov * ov, axis=-1, keepdims=True) + EPS)
            ohat = ov * r
            if gated:
                grv = gr_ref[:, sl].astype(F32)
                sig = _sigmoid(grv)
                dgr_ref[:, sl] = (dnv * (ohat * gv) * (sig * (1.0 + grv * (1.0 - sig)))).astype(BF16)
                dnv = dnv * (grv * sig)
            dg_ref[h:h + 1, :] += jnp.sum(dnv * ohat, axis=0, keepdims=True)
            dohat = dnv * gv
            do = r * (dohat - ohat * jnp.mean(dohat * ohat, axis=-1, keepdims=True))
            do_ref[:, sl] = do.astype(BF16)
            delta = jnp.where(lane == h, jnp.sum(do * ov, axis=-1, keepdims=True), delta)
        dl_ref[...] = delta

    in_specs = [_row_spec(ts, w), _row_spec(ts, w), pl.BlockSpec((nh, hd), lambda i: (0, 0))]
    args = [dn_in, o, g]
    out_specs = [_row_spec(ts, w)]
    out_shape = [jax.ShapeDtypeStruct((s, w), BF16)]
    if gated:
        in_specs.append(pl.BlockSpec((ts, w), lambda i: (i, gr_col)))
        args.append(gr_src)
        out_specs.append(_row_spec(ts, w))
        out_shape.append(jax.ShapeDtypeStruct((s, w), BF16))
    out_specs += [_row_spec(ts, 128), pl.BlockSpec((nh, hd), lambda i: (0, 0))]
    out_shape += [jax.ShapeDtypeStruct((s, 128), F32), jax.ShapeDtypeStruct((nh, hd), F32)]
    return pl.pallas_call(
        body, grid=(s // ts,), in_specs=in_specs, out_specs=out_specs, out_shape=out_shape,
        compiler_params=_cp("arbitrary"), name=name,
    )(*args)


GQ_BLK = 3 * FOX_W // GLA_DK
GK_BLK = GQ_BLK + GLA_HEADS
GV_BLK = (3 * FOX_W + 2 * GLA_KW) // GLA_DV
GR_BLK = GV_BLK + GLA_HEADS


def _gla_chunk_terms(la):
    cum = _dot_nn(_tri(CHUNK), la, HIGHEST)
    total = cum[CHUNK - 1:CHUNK, :]
    return jnp.exp(total - cum), jnp.exp(total)


def _gla_fwd(proj, log_a, g_gla, *, name):
    s = proj.shape[0]
    rows = min(GLA_ROWS, s)
    cb = rows // CHUNK
    nblk = s // rows
    scale = GLA_DK ** -0.5

    def body(q_ref, k_ref, v_ref, gr_ref, la_ref, g_ref, o_ref, n_ref, st_ref, st_sc):
        h = pl.program_id(0)

        @pl.when(pl.program_id(1) == 0)
        def _():
            st_sc[...] = jnp.zeros_like(st_sc)

        gv = g_ref[pl.ds(h, 1), :]
        for ci in range(cb):
            sl = slice(ci * CHUNK, (ci + 1) * CHUNK)
            e, dec = _gla_chunk_terms(la_ref[sl, :])
            k_dec = (k_ref[sl, :].astype(F32) * e).astype(BF16)
            st = st_sc[...] * dec + _dot_tn(v_ref[sl, :], k_dec)
            st_sc[...] = st
            st_ref[0, ci] = st
            qs = (q_ref[sl, :].astype(F32) * scale).astype(BF16)
            o = _dot_nt(qs, st.astype(BF16))
            o_ref[sl, :] = o
            r = lax.rsqrt(jnp.mean(o * o, axis=-1, keepdims=True) + EPS)
            grv = gr_ref[sl, :].astype(F32)
            n_ref[sl, :] = (o * r * gv * (grv * _sigmoid(grv))).astype(BF16)

    return pl.pallas_call(
        body, grid=(GLA_HEADS, nblk),
        in_specs=[pl.BlockSpec((rows, GLA_DK), lambda h, n: (n, GQ_BLK + h)),
                  pl.BlockSpec((rows, GLA_DK), lambda h, n: (n, GK_BLK + h)),
                  pl.BlockSpec((rows, GLA_DV), lambda h, n: (n, GV_BLK + h)),
                  pl.BlockSpec((rows, GLA_DV), lambda h, n: (n, GR_BLK + h)),
                  pl.BlockSpec((rows, GLA_DK), lambda h, n: (n, h)),
                  pl.BlockSpec((GLA_HEADS, GLA_DV), lambda h, n: (0, 0))],
        out_specs=[pl.BlockSpec((rows, GLA_DV), lambda h, n: (n, h)),
                   pl.BlockSpec((rows, GLA_DV), lambda h, n: (n, h)),
                   pl.BlockSpec((1, cb, GLA_DV, GLA_DK), lambda h, n: (h, n, 0, 0))],
        out_shape=[jax.ShapeDtypeStruct((s, GLA_W), F32), jax.ShapeDtypeStruct((s, GLA_W), BF16),
                   jax.ShapeDtypeStruct((GLA_HEADS, s // CHUNK, GLA_DV, GLA_DK), F32)],
        scratch_shapes=[pltpu.VMEM((GLA_DV, GLA_DK), F32)],
        compiler_params=_cp("parallel", "arbitrary"), name=name,
    )(proj, proj, proj, proj, log_a, g_gla)


def _gla_bwd(proj, log_a, do, states, *, name):
    s = proj.shape[0]
    rows = min(GLA_ROWS, s)
    cb = rows // CHUNK
    nblk = s // rows
    scale = GLA_DK ** -0.5

    def body(q_ref, k_ref, v_ref, la_ref, do_ref, st_ref, prev_ref, dq_ref, dk_ref, dv_ref, dla_ref, g_sc):
        nrev = pl.program_id(1)
        blk = nblk - 1 - nrev

        @pl.when(nrev == 0)
        def _():
            g_sc[...] = jnp.zeros_like(g_sc)

        for ci in reversed(range(cb)):
            sl = slice(ci * CHUNK, (ci + 1) * CHUNK)
            e, dec = _gla_chunk_terms(la_ref[sl, :])
            kd = k_ref[sl, :].astype(F32) * e
            qs = (q_ref[sl, :].astype(F32) * scale).astype(BF16)
            dov = do_ref[sl, :]
            st = st_ref[0, ci]
            if ci > 0:
                st_prev = st_ref[0, ci - 1]
            else:
                st_prev = prev_ref[0, 0] * (blk > 0).astype(F32)
            dq_ref[sl, :] = (_dot_nn(dov, st.astype(BF16)) * scale).astype(BF16)
            gt = g_sc[...] + _dot_tn(dov, qs)
            gtb = gt.astype(BF16)
            dkd = _dot_nn(v_ref[sl, :], gtb)
            dv_ref[sl, :] = _dot_nt(kd.astype(BF16), gtb).astype(BF16)
            dk_ref[sl, :] = (dkd * e).astype(BF16)
            ddec = jnp.sum(gt * st_prev, axis=0, keepdims=True) * dec
            dla_ref[sl, :] = _dot_nn(_tri(CHUNK, strict=True), dkd * kd, HIGHEST) + ddec
            g_sc[...] = gt * dec

    rev = lambda col0: (lambda h, n: (nblk - 1 - n, col0 + h))
    return pl.pallas_call(
        body, grid=(GLA_HEADS, nblk),
        in_specs=[pl.BlockSpec((rows, GLA_DK), rev(GQ_BLK)),
                  pl.BlockSpec((rows, GLA_DK), rev(GK_BLK)),
                  pl.BlockSpec((rows, GLA_DV), rev(GV_BLK)),
                  pl.BlockSpec((rows, GLA_DK), rev(0)),
                  pl.BlockSpec((rows, GLA_DV), rev(0)),
                  pl.BlockSpec((1, cb, GLA_DV, GLA_DK), lambda h, n: (h, nblk - 1 - n, 0, 0)),
                  pl.BlockSpec((1, 1, GLA_DV, GLA_DK),
                               lambda h, n: (h, jnp.maximum((nblk - 1 - n) * cb - 1, 0), 0, 0))],
        out_specs=[pl.BlockSpec((rows, GLA_DK), rev(0)), pl.BlockSpec((rows, GLA_DK), rev(0)),
                   pl.BlockSpec((rows, GLA_DV), rev(0)), pl.BlockSpec((rows, GLA_DK), rev(0))],
        out_shape=[jax.ShapeDtypeStruct((s, GLA_KW), BF16), jax.ShapeDtypeStruct((s, GLA_KW), BF16),
                   jax.ShapeDtypeStruct((s, GLA_W), BF16), jax.ShapeDtypeStruct((s, GLA_KW), F32)],
        scratch_shapes=[pltpu.VMEM((GLA_DV, GLA_DK), F32)],
        compiler_params=_cp("parallel", "arbitrary"), name=name,
    )(proj, proj, proj, log_a, do, states, states)


def _row_tile(r):
    tr = min(ROW_TILE, r)
    while r % tr or tr % 8:
        tr -= 1
    return tr


def _adamw_math(w, g, m, v):
    m = ADAM_B1 * m + (1.0 - ADAM_B1) * g
    v = ADAM_B2 * v + (1.0 - ADAM_B2) * (g * g)
    m_hat = m / (1.0 - ADAM_B1 ** ADAM_STEP)
    v_hat = v / (1.0 - ADAM_B2 ** ADAM_STEP)
    delta = -ADAM_LR * (m_hat / (jnp.sqrt(v_hat) + ADAM_EPS) + ADAM_WD * w)
    return delta, m, v


def _adam(g, w, m, v, *, name):
    r, c = w.shape
    tr = _row_tile(r)
    assert r % tr == 0

    def body(g_ref, w_ref, m_ref, v_ref, d_ref, mo_ref, vo_ref):
        d, mn, vn = _adamw_math(w_ref[...], g_ref[...], m_ref[...], v_ref[...])
        d_ref[...] = d
        mo_ref[...] = mn
        vo_ref[...] = vn

    spec = pl.BlockSpec((tr, c), lambda i: (i, 0))
    return pl.pallas_call(
        body, grid=(r // tr,), in_specs=[spec] * 4, out_specs=[spec] * 3,
        out_shape=[jax.ShapeDtypeStruct((r, c), F32)] * 3,
        compiler_params=_cp("parallel"), name=name,
    )(g, w, m, v)


def _ada_grad_adam(c_all_t, dmod_cols, w, m, v, *, name):
    r, c = w.shape
    tr, tc = min(512, r), min(1024, c)

    def body(ct_ref, dm_ref, w_ref, m_ref, v_ref, g_ref, d_ref, mo_ref, vo_ref):
        g = _dot_nn(ct_ref[...], dm_ref[...], HIGHEST)
        g_ref[...] = g
        d, mn, vn = _adamw_math(w_ref[...], g, m_ref[...], v_ref[...])
        d_ref[...] = d
        mo_ref[...] = mn
        vo_ref[...] = vn

    spec = pl.BlockSpec((tr, tc), lambda i, j: (i, j))
    nb = c_all_t.shape[1]
    return pl.pallas_call(
        body, grid=(r // tr, c // tc),
        in_specs=[pl.BlockSpec((tr, nb), lambda i, j: (i, 0)), pl.BlockSpec((nb, tc), lambda i, j: (0, j)),
                  spec, spec, spec],
        out_specs=[spec] * 4, out_shape=[jax.ShapeDtypeStruct((r, c), F32)] * 4,
        compiler_params=_cp("parallel", "parallel"), name=name,
    )(c_all_t, dmod_cols, w, m, v)


def _mod_shard(c_all, w, b, *, name):
    k, c = w.shape
    tc = min(512, c)
    nb = c_all.shape[0]

    def body(c_ref, w_ref, b_ref, o_ref):
        o_ref[...] = _dot_nn(c_ref[...], w_ref[...], HIGHEST) + b_ref[...]

    return pl.pallas_call(
        body, grid=(c // tc,),
        in_specs=[pl.BlockSpec((nb, k), lambda j: (0, 0)), pl.BlockSpec((k, tc), lambda j: (0, j)),
                  pl.BlockSpec((1, tc), lambda j: (0, j))],
        out_specs=pl.BlockSpec((nb, tc), lambda j: (0, j)),
        out_shape=jax.ShapeDtypeStruct((nb, c), F32),
        compiler_params=_cp("parallel"), name=name,
    )(c_all, w, b)


def _silu_rows(c, *, name):
    def body(c_ref, o_ref):
        cv = c_ref[...]
        o_ref[...] = cv * _sigmoid(cv)

    return pl.pallas_call(body, out_shape=jax.ShapeDtypeStruct(c.shape, F32), name=name)(c)


def _pair_sum(g, got, idx, *, name):
    p, r, c = g.shape
    hr = r // 2
    tr = _row_tile(hr)
    nb = hr // tr

    def body(idx_ref, a_ref, b_ref, o_ref):
        o_ref[...] = (a_ref[...].astype(F32) + b_ref[...].astype(F32)).astype(BF16)

    half_spec = pl.BlockSpec((1, tr, c), lambda i, j, idx_ref: (i, j, 0))
    return pl.pallas_call(
        body,
        grid_spec=pltpu.PrefetchScalarGridSpec(
            num_scalar_prefetch=1, grid=(p, nb),
            in_specs=[pl.BlockSpec((1, tr, c), lambda i, j, idx_ref: (i, idx_ref[0] * nb + j, 0)), half_spec],
            out_specs=half_spec),
        out_shape=jax.ShapeDtypeStruct((p, hr, c), BF16),
        compiler_params=_cp("parallel", "parallel"), name=name,
    )(idx, g, got)


def _final_sum(own, parts, idx, *, name):
    _, hr, c = own.shape
    tr = _row_tile(hr)
    nb = hr // tr

    def body(idx_ref, own_ref, parts_ref, o_ref):
        acc = own_ref[0].astype(F32)
        for q in range(3):
            acc = acc + parts_ref[q].astype(F32)
        o_ref[...] = acc

    return pl.pallas_call(
        body,
        grid_spec=pltpu.PrefetchScalarGridSpec(
            num_scalar_prefetch=1, grid=(nb,),
            in_specs=[pl.BlockSpec((1, tr, c), lambda i, idx_ref: (idx_ref[1], i, 0)),
                      pl.BlockSpec((3, tr, c), lambda i, idx_ref: (0, i, 0))],
            out_specs=pl.BlockSpec((tr, c), lambda i, idx_ref: (idx_ref[0] * nb + i, 0))),
        out_shape=jax.ShapeDtypeStruct((2 * hr, c), F32),
        compiler_params=_cp("parallel"), name=name,
    )(idx, own, parts)


def _stack_sum(x, *, name):
    p, r, c = x.shape
    tr = _row_tile(r)

    def body(x_ref, o_ref):
        acc = x_ref[0].astype(F32)
        for q in range(1, p):
            acc = acc + x_ref[q].astype(F32)
        o_ref[...] = acc

    return pl.pallas_call(
        body, grid=(r // tr,),
        in_specs=[pl.BlockSpec((p, tr, c), lambda i: (0, i, 0))],
        out_specs=pl.BlockSpec((tr, c), lambda i: (i, 0)),
        out_shape=jax.ShapeDtypeStruct((r, c), F32),
        compiler_params=_cp("parallel"), name=name,
    )(x)


def _place():
    x, y, c = lax.axis_index("x"), lax.axis_index("y"), lax.axis_index("c")
    chips = [(1 - x, y), (x, 1 - y), (1 - x, 1 - y)]
    return x, y, c, chips


def _gather8(x_shard, *, name):
    m_per, n = x_shard.shape

    def body(x_ref, out_ref, send_sems, recv_sems, local_sem):
        x, y, c, chips = _place()
        me, sibling = (x, y, c), (x, y, 1 - c)

        def rows(px, py, pc):
            return out_ref.at[pl.ds((4 * px + 2 * py + pc) * m_per, m_per), :]

        def copy(k, block, to, src=None):
            return pltpu.make_async_remote_copy(
                src_ref=rows(*block) if src is None else src, dst_ref=rows(*block),
                send_sem=send_sems.at[k], recv_sem=recv_sems.at[k], device_id=to, device_id_type=MESH)

        mine = pltpu.make_async_copy(x_ref, rows(*me), local_sem)
        mine.start()
        first = [copy(0, me, sibling, src=x_ref)]
        first += [copy(1 + j, me, (*chip, c), src=x_ref) for j, chip in enumerate(chips)]
        for cp in first:
            cp.start()
        passed = [copy(4 + j, (*chip, c), sibling) for j, chip in enumerate(chips)]
        for j, chip in enumerate(chips):
            copy(1 + j, (*chip, c), me).wait_recv()
            passed[j].start()
        copy(0, sibling, me).wait_recv()
        for j, chip in enumerate(chips):
            copy(4 + j, (*chip, 1 - c), me).wait_recv()
        for cp in first + passed:
            cp.wait_send()
        mine.wait()

    return pl.pallas_call(
        body,
        out_shape=jax.ShapeDtypeStruct((8 * m_per, n), x_shard.dtype),
        in_specs=[pl.BlockSpec(memory_space=pltpu.VMEM)],
        out_specs=pl.BlockSpec(memory_space=pltpu.VMEM),
        scratch_shapes=[pltpu.SemaphoreType.DMA((7,)), pltpu.SemaphoreType.DMA((7,)), pltpu.SemaphoreType.DMA],
        name=name,
    )(x_shard)


def _gather_weights(shards, *, name):
    return _comm_call(lambda ins, outs: [cp for i, o in zip(ins, outs) for cp in _plan_gather_ici(i, o)],
                      shards, [jax.ShapeDtypeStruct((4,) + s.shape, s.dtype) for s in shards], name=name)


def _plan_start(plan, send_sems, recv_sems):
    for k, (src, dst, _, peer) in enumerate(plan):
        pltpu.make_async_remote_copy(src_ref=src, dst_ref=dst, send_sem=send_sems.at[k], recv_sem=recv_sems.at[k],
                                     device_id=peer, device_id_type=MESH).start()


def _plan_wait(plan, send_sems, recv_sems):
    for k, (src, _, land, peer) in enumerate(plan):
        pltpu.make_async_remote_copy(src_ref=src, dst_ref=land, send_sem=send_sems.at[k], recv_sem=recv_sems.at[k],
                                     device_id=peer, device_id_type=MESH).wait_recv()
    for k, (src, dst, _, peer) in enumerate(plan):
        pltpu.make_async_remote_copy(src_ref=src, dst_ref=dst, send_sem=send_sems.at[k], recv_sem=recv_sems.at[k],
                                     device_id=peer, device_id_type=MESH).wait_send()


def _rows_half(ref, hc, axis):
    hr = ref.shape[axis] // 2
    idx = [slice(None)] * len(ref.shape)
    idx[axis] = pl.ds(hc * hr, hr)
    return ref.at[tuple(idx)]


def _plan_gather_ici(shard, full):
    x, y, c, chips = _place()
    src = _rows_half(shard, c, 0)
    return [(src, _rows_half(full.at[2 * x + y], c, 0), _rows_half(full.at[2 * cx + cy], c, 0), (cx, cy, c))
            for cx, cy in chips]


def _plan_gather_d2d(full):
    x, y, c, chips = _place()
    plan = []
    for cx, cy in chips:
        slot = full.at[2 * cx + cy]
        plan.append((_rows_half(slot, c, 0), _rows_half(slot, c, 0), _rows_half(slot, 1 - c, 0), (x, y, 1 - c)))
    return plan


def _plan_pair(grad, got):
    x, y, c, _ = _place()
    return [(_rows_half(grad, 1 - c, 1), got, got, (x, y, 1 - c))]


def _plan_shard_ici(sums, parts):
    _, _, c, chips = _place()
    return [(sums.at[2 * cx + cy], parts.at[k], parts.at[k], (cx, cy, c)) for k, (cx, cy) in enumerate(chips)]


def _plan_half(buf):
    x, y, c, _ = _place()
    mine = _rows_half(buf, c, 0)
    return [(mine, mine, _rows_half(buf, 1 - c, 0), (x, y, 1 - c))]


def _comm_call(plan_fn, inputs, out_shapes, *, name, aliases=None):
    ni, no = len(inputs), len(out_shapes)

    def body(*refs):
        plan = plan_fn(refs[:ni], refs[ni:ni + no])
        send_sems, recv_sems = refs[ni + no:]
        _plan_start(plan, send_sems, recv_sems)
        _plan_wait(plan, send_sems, recv_sems)

    any_spec = pl.BlockSpec(memory_space=pl.ANY)
    n_copies = 3 * max(ni, no)
    return pl.pallas_call(
        body, out_shape=list(out_shapes), in_specs=[any_spec] * ni, out_specs=[any_spec] * no,
        scratch_shapes=[pltpu.SemaphoreType.DMA((n_copies,)), pltpu.SemaphoreType.DMA((n_copies,))],
        input_output_aliases=aliases or {}, name=name,
    )(*inputs)


def _gather_forward(fulls, *, name):
    return _comm_call(lambda ins, outs: [cp for o in outs for cp in _plan_gather_d2d(o)],
                      fulls, [jax.ShapeDtypeStruct(f.shape, f.dtype) for f in fulls], name=name,
                      aliases={k: k for k in range(len(fulls))})


def _pair_exchange(grads, *, name):
    return _comm_call(lambda ins, outs: [cp for i, o in zip(ins, outs) for cp in _plan_pair(i, o)],
                      grads, [jax.ShapeDtypeStruct((4, g.shape[1] // 2, g.shape[2]), g.dtype) for g in grads],
                      name=name)


def _shard_exchange(sums, *, name):
    return _comm_call(lambda ins, outs: [cp for i, o in zip(ins, outs) for cp in _plan_shard_ici(i, o)],
                      sums, [jax.ShapeDtypeStruct((3,) + s.shape[1:], s.dtype) for s in sums], name=name)


def _half_exchange(bufs, *, name):
    return _comm_call(lambda ins, outs: [cp for o in outs for cp in _plan_half(o)],
                      bufs, [jax.ShapeDtypeStruct(b.shape, b.dtype) for b in bufs], name=name,
                      aliases={k: k for k in range(len(bufs))})


def _split_w_in(w_in_full):
    d = w_in_full.shape[0]
    main = jnp.concatenate([w_in_full[:, 0:3072], w_in_full[:, 3080:5128], w_in_full[:, 5144:6168]], axis=1)
    small = jnp.concatenate([w_in_full[:, 3072:3080], w_in_full[:, 5128:5144],
                             jnp.zeros((d, SMALL_W - 24), w_in_full.dtype)], axis=1)
    return main, small


def _merge_dw_in(dw_main, dw_small):
    return jnp.concatenate([dw_main[:, 0:3072], dw_small[:, 0:8], dw_main[:, 3072:5120], dw_small[:, 8:24],
                            dw_main[:, 5120:6144]], axis=1)


def _local_step(x, target, mod, g_pre_mix, g_post_mix, g_pre_mlp, g_post_mlp, w_in_full, b_fgate, w_gla_a2,
                b_gla_a2, g_fox, g_gla, w_out_full, w_mlp_in_full, w_mlp_out_full):
    s, d = x.shape
    shift_m, scale_m, gate_m, shift_f, scale_f, gate_f = [mod[:, i * d:(i + 1) * d] for i in range(6)]
    a1 = g_pre_mix * (1.0 + scale_m)
    a2 = g_pre_mlp * (1.0 + scale_f)
    w_main, w_small = _split_w_in(w_in_full)
    bf = jnp.concatenate([b_fgate, jnp.zeros((1, SMALL_W - FOX_HEADS), F32)], axis=1)
    w2p = jnp.zeros((SMALL_W, GLA_KW), F32).at[FOX_HEADS:FOX_HEADS + GLA_RANK].set(w_gla_a2)

    h1 = _pre_fwd(x, a1, shift_m, name="pre_mix_fwd")
    proj, = _mm(h1, w_main, mode="nn", out_dtypes=[BF16], name="in_proj_main")
    ps, = _mm(h1, w_small, mode="nn", out_dtypes=[F32], name="in_proj_small")
    cum, log_a = _gates_fwd(ps, bf, w2p, b_gla_a2, name="gates_fwd")
    cum_t = cum[:, :FOX_HEADS].T
    o_fox, fox_n, lse = _fox_fwd(proj, cum, cum_t, g_fox, name="fox_fwd")
    o_gla, gla_n, states = _gla_fwd(proj, log_a, g_gla, name="gla_fwd")
    mixed = jnp.concatenate([fox_n, gla_n], axis=1)
    y1, = _mm(mixed, w_out_full, mode="nn", out_dtypes=[F32], name="out_proj")
    x1 = _post_fwd(x, y1, gate_m, g_post_mix, name="post_mix_fwd")
    h2 = _pre_fwd(x1, a2, shift_f, name="pre_mlp_fwd")

    def mlp_act(acc):
        r = jnp.maximum(acc, 0.0)
        return acc, r * r

    u, act = _mm(h2, w_mlp_in_full, mode="nn", out_dtypes=[BF16, BF16], epi=mlp_act, name="mlp_in")
    y2, = _mm(act, w_mlp_out_full, mode="nn", out_dtypes=[F32], name="mlp_out")
    dx2, loss_part = _post_fwd_loss(x1, y2, gate_f, g_post_mlp, target, name="post_mlp_fwd_loss")

    dy2, dgate_f, dg_post_mlp = _post_bwd(dx2, y2, gate_f, g_post_mlp, name="post_mlp_bwd")
    dw_mlp_out, = _mm(act, dy2, mode="tn", out_dtypes=[BF16], name="dw_mlp_out")

    def act_bwd(acc, uv):
        return (acc * (2.0 * jnp.maximum(uv.astype(F32), 0.0)),)

    du, = _mm(dy2, w_mlp_out_full, mode="nt", out_dtypes=[BF16], extras=[u], epi=act_bwd, name="d_mlp_hidden")
    nj = D_FF // 4 // min(MM_T, D_FF // 4)
    tmw = min(MM_T, d)
    dw_mlp_in, = _mm(h2, du, mode="tn", out_dtypes=[BF16], name="dw_mlp_in",
                     out_shapes=[jax.ShapeDtypeStruct((4, d, D_FF // 4), BF16)],
                     out_specs=[pl.BlockSpec((1, tmw, min(MM_T, D_FF // 4)), lambda i, j, kk: (j // nj, i, j % nj))])
    dh2, = _mm(du, w_mlp_in_full, mode="nt", out_dtypes=[F32], name="d_mlp_in")
    dx1, dshift_f, da2 = _pre_bwd(dh2, x1, dx2, a2, name="pre_mlp_bwd")

    dy1, dgate_m, dg_post_mix = _post_bwd(dx1, y1, gate_m, g_post_mix, name="post_mix_bwd")
    dw_out, = _mm(mixed, dy1, mode="tn", out_dtypes=[BF16], name="dw_out")
    dmixed, = _mm(dy1, w_out_full, mode="nt", out_dtypes=[BF16], name="d_mixed")
    do_fox, delta, dg_fox = _head_norm_bwd(dmixed[:, :FOX_W], o_fox, g_fox, None, nh=FOX_HEADS, hd=FOX_HD,
                                           gr_col=0, name="fox_norm_bwd")
    do_gla, dgr, _, dg_gla = _head_norm_bwd(dmixed[:, FOX_W:], o_gla, g_gla, proj, nh=GLA_HEADS, hd=GLA_DV,
                                            gr_col=(3 * FOX_W + 2 * GLA_KW + GLA_W) // GLA_W, name="gla_norm_bwd")
    dq_fox, dcq = _fox_bwd_dq(proj, do_fox, cum, cum_t, lse, delta, name="fox_bwd_dq")
    dk_fox, dv_fox, dck_t = _fox_bwd_dkv(proj, do_fox, cum, cum_t, lse, delta, name="fox_bwd_dkv")
    dgq, dgk, dgv, dla = _gla_bwd(proj, log_a, do_gla, states, name="gla_bwd")
    dck = dcq + jnp.concatenate([dck_t.T, jnp.zeros((s, SMALL_W - FOX_HEADS), F32)], axis=1)
    dps, dbf, dw2p, db2 = _gates_bwd(dck, ps, bf, w2p, b_gla_a2, dla, name="gates_bwd")
    dproj = jnp.concatenate([dq_fox, dk_fox, dv_fox, dgq, dgk, dgv, dgr], axis=1)
    dw_main, = _mm(h1, dproj, mode="tn", out_dtypes=[BF16], name="dw_in_main")
    dw_small, = _mm(h1, dps, mode="tn", out_dtypes=[BF16], name="dw_in_small")
    dh1_small, = _mm(dps, w_small, mode="nt", out_dtypes=[F32], name="d_h1_small")
    dh1, = _mm(dproj, w_main, mode="nt", out_dtypes=[F32], extras=[dh1_small], epi=lambda acc, e: (acc + e,),
               name="d_h1")
    grad_x, dshift_m, da1 = _pre_bwd(dh1, x, dx1, a1, name="pre_mix_bwd")

    dmod = jnp.concatenate([dshift_m, da1 * g_pre_mix, dgate_m, dshift_f, da2 * g_pre_mlp, dgate_f], axis=1)
    small = dict(
        dmod=dmod, g_pre_mix=da1 * (1.0 + scale_m), g_post_mix=dg_post_mix, g_pre_mlp=da2 * (1.0 + scale_f),
        g_post_mlp=dg_post_mlp, b_fgate=dbf[:, :FOX_HEADS], w_gla_a2=dw2p[FOX_HEADS:FOX_HEADS + GLA_RANK],
        b_gla_a2=db2, g_fox_out=dg_fox, g_gla_out=dg_gla)
    big = dict(w_in=_merge_dw_in(dw_main, dw_small), w_out=dw_out, w_mlp_in=dw_mlp_in, w_mlp_out=dw_mlp_out)
    return loss_part, grad_x, big, small


def _pack(arrays):
    flat = jnp.concatenate([a.reshape(-1).astype(F32) for a in arrays])
    n = flat.shape[0]
    rows = -(-n // 128)
    rows = -(-rows // 8) * 8
    return jnp.pad(flat, (0, rows * 128 - n)).reshape(rows, 128)


def _unpack(buf, shapes):
    flat = buf.reshape(-1)
    out, off = [], 0
    for shp in shapes:
        n = 1
        for q in shp:
            n *= q
        out.append(flat[off:off + n].reshape(shp))
        off += n
    return out


SMALL_GRAD_ORDER = ["dmod", "g_pre_mix", "g_post_mix", "g_pre_mlp", "g_post_mlp", "b_fgate", "w_gla_a2", "b_gla_a2",
                    "g_fox_out", "g_gla_out"]


def kernel(x, c, w_ada, b_ada, g_pre_mix, g_post_mix, w_in, b_fgate, w_gla_a2, b_gla_a2, g_fox_out, g_gla_out, w_out, g_pre_mlp, g_post_mlp, w_mlp_in, w_mlp_out, loss_target, m_w_ada, m_b_ada, m_g_pre_mix, m_g_post_mix, m_w_in, m_b_fgate, m_w_gla_a2, m_b_gla_a2, m_g_fox_out, m_g_gla_out, m_w_out, m_g_pre_mlp, m_g_post_mlp, m_w_mlp_in, m_w_mlp_out, v_w_ada, v_b_ada, v_g_pre_mix, v_g_post_mix, v_w_in, v_b_fgate, v_w_gla_a2, v_b_gla_a2, v_g_fox_out, v_g_gla_out, v_w_out, v_g_pre_mlp, v_g_post_mlp, v_w_mlp_in, v_w_mlp_out):
    ix, iy, ic = lax.axis_index("x"), lax.axis_index("y"), lax.axis_index("c")
    chip = 2 * ix + iy
    dev = 4 * ix + 2 * iy + ic
    d = D_MODEL

    c_act = _silu_rows(c, name="silu_c")
    pack1 = _pack([c_act, w_gla_a2[0], g_gla_out[0]])
    rows1 = pack1.shape[0]
    got1 = _gather8(pack1, name="gather_small_fwd").reshape(8, rows1, 128)
    per_dev = [_unpack(got1[q], [(d,), (GLA_RANK, GLA_KW // 4), (GLA_HEADS, GLA_DV // 4)]) for q in range(8)]
    c_all = jnp.stack([p[0] for p in per_dev])
    w_gla_a2_full = jnp.concatenate([per_dev[2 * j][1] for j in range(4)], axis=1)
    g_gla_full = jnp.concatenate([per_dev[2 * j][2] for j in range(4)], axis=1)
    cols = w_ada.shape[2]
    b_ada_shard = lax.dynamic_slice_in_dim(b_ada, chip * cols, cols, axis=1)
    mod_sh = _mod_shard(c_all, w_ada[0], b_ada_shard, name="ada_mod")
    got2 = _gather8(mod_sh, name="gather_mod").reshape(8, 8, cols)
    mod_all = jnp.concatenate([got2[2 * j] for j in range(4)], axis=1)
    mod = lax.dynamic_slice_in_dim(mod_all, dev, 1, axis=0)

    own_bf = [w_in[0].astype(BF16), w_out[0].astype(BF16), w_mlp_in[0].astype(BF16), w_mlp_out[0].astype(BF16)]
    fulls = _gather_forward(_gather_weights(own_bf, name="gather_weights_ici"), name="gather_weights_d2d")
    gw_in, gw_out, gw_mlp_in, gw_mlp_out = [
        lax.dynamic_update_index_in_dim(f, o, chip, 0) for f, o in zip(fulls, own_bf)]
    w_in_full = jnp.transpose(gw_in, (1, 0, 2)).reshape(d, -1)
    w_out_full = gw_out.reshape(-1, d)
    w_mlp_in_full = jnp.transpose(gw_mlp_in, (1, 0, 2)).reshape(d, -1)
    w_mlp_out_full = gw_mlp_out.reshape(-1, d)

    loss_part, grad_x, big, small = _local_step(
        x[0], loss_target[0], mod, g_pre_mix, g_post_mix, g_pre_mlp, g_post_mlp, w_in_full, b_fgate,
        w_gla_a2_full, b_gla_a2, g_fox_out[0], g_gla_full, w_out_full, w_mlp_in_full, w_mlp_out_full)
    loss = lax.psum(loss_part[0, 0], ("x", "y", "c"))

    cs_in = w_in.shape[2]
    g_stack = [
        jnp.transpose(big["w_in"].reshape(d, 4, cs_in), (1, 0, 2)),
        big["w_out"].reshape(4, d // 4, d),
        big["w_mlp_in"],
        big["w_mlp_out"].reshape(4, D_FF // 4, d),
    ]
    idx = jnp.stack([ic, chip]).astype(jnp.int32)
    got = _pair_exchange(g_stack, name="grad_pair_exchange")
    chip_sums = [_pair_sum(g, q, idx, name=f"grad_pair_sum_{n}") for n, (g, q) in enumerate(zip(g_stack, got))]
    parts = _shard_exchange(chip_sums, name="grad_shard_exchange")
    bufs = [_final_sum(s, p, idx, name=f"grad_final_sum_{n}") for n, (s, p) in enumerate(zip(chip_sums, parts))]
    g_big = _half_exchange(bufs, name="grad_half_exchange")
    big_w = [(w_in, m_w_in, v_w_in), (w_out, m_w_out, v_w_out), (w_mlp_in, m_w_mlp_in, v_w_mlp_in),
             (w_mlp_out, m_w_mlp_out, v_w_mlp_out)]
    big_res = []
    for q, (g, (w, m, v)) in enumerate(zip(g_big, big_w)):
        dl, mn, vn = _adam(g, w[0], m[0], v[0], name=f"adam_big_{q}")
        big_res.append((g[None], dl[None], mn[None], vn[None]))

    pack2 = _pack([small[k] for k in SMALL_GRAD_ORDER])
    rows2 = pack2.shape[0]
    got3 = _gather8(pack2, name="gather_small_grads").reshape(8, rows2, 128)
    dmod_all = got3[:, :6 * d // 128, :].reshape(8, 6 * d)
    sums = _stack_sum(got3, name="small_grad_sum")
    shapes = [(1, 6 * d), (1, d), (1, d), (1, d), (1, d), (1, FOX_HEADS), (1, GLA_RANK, GLA_KW), (1, GLA_KW),
              (1, FOX_HEADS, FOX_HD), (1, GLA_HEADS, GLA_DV)]
    sg = dict(zip(["b_ada"] + SMALL_GRAD_ORDER[1:], _unpack(sums, shapes)))
    sg["w_gla_a2"] = lax.dynamic_slice_in_dim(sg["w_gla_a2"], chip * (GLA_KW // 4), GLA_KW // 4, axis=2)
    sg["g_gla_out"] = lax.dynamic_slice_in_dim(sg["g_gla_out"], chip * (GLA_DV // 4), GLA_DV // 4, axis=2)
    small_names = ["b_ada", "g_pre_mix", "g_post_mix", "b_fgate", "w_gla_a2", "b_gla_a2", "g_fox_out", "g_gla_out",
                   "g_pre_mlp", "g_post_mlp"]
    small_w = dict(b_ada=(b_ada, m_b_ada, v_b_ada), g_pre_mix=(g_pre_mix, m_g_pre_mix, v_g_pre_mix),
                   g_post_mix=(g_post_mix, m_g_post_mix, v_g_post_mix), b_fgate=(b_fgate, m_b_fgate, v_b_fgate),
                   w_gla_a2=(w_gla_a2, m_w_gla_a2, v_w_gla_a2), b_gla_a2=(b_gla_a2, m_b_gla_a2, v_b_gla_a2),
                   g_fox_out=(g_fox_out, m_g_fox_out, v_g_fox_out), g_gla_out=(g_gla_out, m_g_gla_out, v_g_gla_out),
                   g_pre_mlp=(g_pre_mlp, m_g_pre_mlp, v_g_pre_mlp), g_post_mlp=(g_post_mlp, m_g_post_mlp, v_g_post_mlp))
    sshapes = [small_w[k][0].shape for k in small_names]
    pg = _pack([sg[k] for k in small_names])
    pw, pm, pv = [_pack([small_w[k][q] for k in small_names]) for q in range(3)]
    pd, pmn, pvn = _adam(pg, pw, pm, pv, name="adam_small")
    s_delta = dict(zip(small_names, _unpack(pd, sshapes)))
    s_m = dict(zip(small_names, _unpack(pmn, sshapes)))
    s_v = dict(zip(small_names, _unpack(pvn, sshapes)))

    dmod_cols = lax.dynamic_slice_in_dim(dmod_all, chip * cols, cols, axis=1)
    g_ada, d_ada, m_ada, v_ada = _ada_grad_adam(c_all.T, dmod_cols, w_ada[0], m_w_ada[0], v_w_ada[0], name="ada_grad_adam")

    order = ["w_ada", "b_ada", "g_pre_mix", "g_post_mix", "w_in", "b_fgate", "w_gla_a2", "b_gla_a2", "g_fox_out",
             "g_gla_out", "w_out", "g_pre_mlp", "g_post_mlp", "w_mlp_in", "w_mlp_out"]
    res = {"w_ada": (g_ada[None], d_ada[None], m_ada[None], v_ada[None]),
           "w_in": big_res[0], "w_out": big_res[1], "w_mlp_in": big_res[2], "w_mlp_out": big_res[3]}
    for k in small_names:
        res[k] = (sg[k], s_delta[k], s_m[k], s_v[k])
    return (loss, grad_x[None], *[res[k][0] for k in order], *[res[k][1] for k in order],
            *[res[k][2] for k in order], *[res[k][3] for k in order])
```

```python
import functools

import jax
import jax.numpy as jnp
from jax import lax
from jax.experimental import pallas as pl
from jax.experimental.pallas import tpu as pltpu

F32 = jnp.float32
BF16 = jnp.bfloat16
MESH = pl.DeviceIdType.MESH
HIGHEST = lax.Precision.HIGHEST

D_MODEL = 2048
FOX_HEADS = 8
FOX_HD = 128
FOX_W = FOX_HEADS * FOX_HD
GLA_HEADS = 4
GLA_DK = 128
GLA_DV = 256
GLA_KW = GLA_HEADS * GLA_DK
GLA_W = GLA_HEADS * GLA_DV
GLA_RANK = 16
GLA_TEMP = 16.0
CHUNK = 64
D_FF = 4 * D_MODEL
EPS = 1e-6
MAIN_W = 3 * FOX_W + 2 * GLA_KW + 2 * GLA_W
SMALL_W = 128
NEG = -1e30

ADAM_LR = 0.001
ADAM_B1 = 0.9
ADAM_B2 = 0.999
ADAM_EPS = 1e-08
ADAM_WD = 0.01
ADAM_STEP = 10

VMEM_LIMIT = 52 * 1024 * 1024
ROW_TILE = 256
FOX_TQ = 512
FOX_TK = 512
GLA_ROWS = 512
GATE_TS = 512
MM_T = 1024


def _cp(*sem):
    return pltpu.CompilerParams(dimension_semantics=sem, vmem_limit_bytes=VMEM_LIMIT)


def _dot_nn(a, b, precision=None):
    return jnp.dot(a, b, preferred_element_type=F32, precision=precision)


def _dot_nt(a, b, precision=None):
    return lax.dot_general(a, b, (((1,), (1,)), ((), ())), preferred_element_type=F32, precision=precision)


def _dot_tn(a, b, precision=None):
    return lax.dot_general(a, b, (((0,), (0,)), ((), ())), preferred_element_type=F32, precision=precision)


def _sigmoid(x):
    return 1.0 / (1.0 + jnp.exp(-x))


def _log_sigmoid(x):
    return jnp.minimum(x, 0.0) - jnp.log(1.0 + jnp.exp(-jnp.abs(x)))


class _Side:
    def __init__(self, inputs, out_shapes, plan_fn, n_copies):
        self.inputs, self.out_shapes, self.plan_fn, self.n_copies = list(inputs), list(out_shapes), plan_fn, n_copies

    def scratch(self):
        return [pltpu.SemaphoreType.DMA((self.n_copies,)), pltpu.SemaphoreType.DMA((self.n_copies,))]

    def run(self, in_refs, out_refs, sems, first, last):
        @pl.when(first)
        def _():
            _plan_start(self.plan_fn(in_refs, out_refs), *sems)

        @pl.when(last)
        def _():
            _plan_wait(self.plan_fn(in_refs, out_refs), *sems)


def _mm(a, b, *, mode, out_dtypes, name, tm=None, tn=None, tk=None, extras=(), epi=None,
        out_shapes=None, out_specs=None, side=None):
    tm, tn, tk = tm or MM_T, tn or MM_T, tk or MM_T
    if mode == "nn":
        (m, k), n = a.shape, b.shape[1]
    elif mode == "nt":
        (m, k), n = a.shape, b.shape[0]
    else:
        (k, m), n = a.shape, b.shape[1]
    tm, tn, tk = min(tm, m), min(tn, n), min(tk, k)
    assert m % tm == 0 and n % tn == 0 and k % tk == 0, (name, m, n, k)
    nk = k // tk
    n_out, n_ex = len(out_dtypes), len(extras)
    if epi is None:
        epi = lambda acc: tuple(acc for _ in range(n_out))
    dot = {"nn": _dot_nn, "nt": _dot_nt, "tn": _dot_tn}[mode]

    n_si = len(side.inputs) if side else 0
    n_so = len(side.out_shapes) if side else 0
    grid = (m // tm, n // tn, nk)

    def body(*refs):
        a_ref, b_ref = refs[0], refs[1]
        ex_refs = refs[2:2 + n_ex]
        base = 2 + n_ex + n_si
        o_refs = refs[base:base + n_out]
        scratch = refs[base + n_out + n_so:]
        if side:
            pos = [pl.program_id(q) for q in range(3)]
            first = (pos[0] == 0) & (pos[1] == 0) & (pos[2] == 0)
            last = (pos[0] == grid[0] - 1) & (pos[1] == grid[1] - 1) & (pos[2] == grid[2] - 1)
            side.run(refs[2 + n_ex:base], refs[base + n_out:base + n_out + n_so], scratch[-2:], first, last)
        part = dot(a_ref[...], b_ref[...])

        def finish(acc):
            outs = epi(acc, *[e[...] for e in ex_refs])
            for o_ref, val in zip(o_refs, outs):
                o_ref[...] = val.reshape(o_ref.shape).astype(o_ref.dtype)

        if nk == 1:
            finish(part)
        else:
            acc_ref = scratch[0]
            kk = pl.program_id(2)

            @pl.when(kk == 0)
            def _():
                acc_ref[...] = part

            @pl.when(kk > 0)
            def _():
                acc_ref[...] += part

            @pl.when(kk == nk - 1)
            def _():
                finish(acc_ref[...])

    if mode == "nn":
        a_spec = pl.BlockSpec((tm, tk), lambda i, j, kk: (i, kk))
        b_spec = pl.BlockSpec((tk, tn), lambda i, j, kk: (kk, j))
    elif mode == "nt":
        a_spec = pl.BlockSpec((tm, tk), lambda i, j, kk: (i, kk))
        b_spec = pl.BlockSpec((tn, tk), lambda i, j, kk: (j, kk))
    else:
        a_spec = pl.BlockSpec((tk, tm), lambda i, j, kk: (kk, i))
        b_spec = pl.BlockSpec((tk, tn), lambda i, j, kk: (kk, j))
    tile_spec = pl.BlockSpec((tm, tn), lambda i, j, kk: (i, j))
    if out_shapes is None:
        out_shapes = [jax.ShapeDtypeStruct((m, n), dt) for dt in out_dtypes]
    if out_specs is None:
        out_specs = [tile_spec for _ in out_dtypes]
    any_spec = pl.BlockSpec(memory_space=pl.ANY)
    res = pl.pallas_call(
        body,
        grid=grid,
        in_specs=[a_spec, b_spec] + [tile_spec for _ in extras] + [any_spec] * n_si,
        out_specs=list(out_specs) + [any_spec] * n_so,
        out_shape=list(out_shapes) + (side.out_shapes if side else []),
        scratch_shapes=([pltpu.VMEM((tm, tn), F32)] if nk > 1 else []) + (side.scratch() if side else []),
        compiler_params=_cp("arbitrary", "arbitrary", "arbitrary") if side else _cp("parallel", "parallel", "arbitrary"),
        name=name,
    )(a, b, *extras, *(side.inputs if side else []))
    return res


def _row_spec(ts, d):
    return pl.BlockSpec((ts, d), lambda i: (i, 0))


def _vec_spec(d):
    return pl.BlockSpec((1, d), lambda i: (0, 0))


def _pre_fwd(x, avec, shift, *, name):
    s, d = x.shape
    ts = min(ROW_TILE, s)

    def body(x_ref, a_ref, s_ref, h_ref):
        xv = x_ref[...]
        r = lax.rsqrt(jnp.mean(xv * xv, axis=-1, keepdims=True) + EPS)
        h_ref[...] = (xv * r * a_ref[...] + s_ref[...]).astype(BF16)

    return pl.pallas_call(
        body, grid=(s // ts,),
        in_specs=[_row_spec(ts, d), _vec_spec(d), _vec_spec(d)],
        out_specs=_row_spec(ts, d),
        out_shape=jax.ShapeDtypeStruct((s, d), BF16),
        compiler_params=_cp("parallel"), name=name,
    )(x, avec, shift)


def _post_fwd(x, y, gate, g, *, name):
    s, d = x.shape
    ts = min(ROW_TILE, s)

    def body(x_ref, y_ref, gate_ref, g_ref, o_ref):
        yv = y_ref[...]
        r = lax.rsqrt(jnp.mean(yv * yv, axis=-1, keepdims=True) + EPS)
        o_ref[...] = x_ref[...] + gate_ref[...] * (yv * r * g_ref[...])

    return pl.pallas_call(
        body, grid=(s // ts,),
        in_specs=[_row_spec(ts, d), _row_spec(ts, d), _vec_spec(d), _vec_spec(d)],
        out_specs=_row_spec(ts, d),
        out_shape=jax.ShapeDtypeStruct((s, d), F32),
        compiler_params=_cp("parallel"), name=name,
    )(x, y, gate, g)


def _post_fwd_loss(x, y, gate, g, target, *, name):
    s, d = x.shape
    ts = min(ROW_TILE, s)

    def body(x_ref, y_ref, gate_ref, g_ref, t_ref, dx_ref, loss_ref):
        yv = y_ref[...]
        r = lax.rsqrt(jnp.mean(yv * yv, axis=-1, keepdims=True) + EPS)
        diff = x_ref[...] + gate_ref[...] * (yv * r * g_ref[...]) - t_ref[...]
        dx_ref[...] = diff * (1.0 / d)

        @pl.when(pl.program_id(0) == 0)
        def _():
            loss_ref[...] = jnp.zeros_like(loss_ref)

        loss_ref[...] += jnp.sum(jnp.mean(diff * diff, axis=-1, keepdims=True)) * 0.5

    return pl.pallas_call(
        body, grid=(s // ts,),
        in_specs=[_row_spec(ts, d), _row_spec(ts, d), _vec_spec(d), _vec_spec(d), _row_spec(ts, d)],
        out_specs=[_row_spec(ts, d), pl.BlockSpec((1, 128), lambda i: (0, 0))],
        out_shape=[jax.ShapeDtypeStruct((s, d), F32), jax.ShapeDtypeStruct((1, 128), F32)],
        compiler_params=_cp("arbitrary"), name=name,
    )(x, y, gate, g, target)


def _post_bwd(dxo, y, gate, g, *, name):
    s, d = y.shape
    ts = min(ROW_TILE, s)

    def body(dx_ref, y_ref, gate_ref, g_ref, dy_ref, dgate_ref, dg_ref):
        yv, dxv, gv = y_ref[...], dx_ref[...], g_ref[...]
        r = lax.rsqrt(jnp.mean(yv * yv, axis=-1, keepdims=True) + EPS)
        yhat = yv * r
        dn = dxv * gate_ref[...]
        dyhat = dn * gv
        dy = r * (dyhat - yhat * jnp.mean(dyhat * yhat, axis=-1, keepdims=True))
        dy_ref[...] = dy.astype(BF16)

        @pl.when(pl.program_id(0) == 0)
        def _():
            dgate_ref[...] = jnp.zeros_like(dgate_ref)
            dg_ref[...] = jnp.zeros_like(dg_ref)

        dgate_ref[...] += jnp.sum(dxv * (yhat * gv), axis=0, keepdims=True)
        dg_ref[...] += jnp.sum(dn * yhat, axis=0, keepdims=True)

    return pl.pallas_call(
        body, grid=(s // ts,),
        in_specs=[_row_spec(ts, d), _row_spec(ts, d), _vec_spec(d), _vec_spec(d)],
        out_specs=[_row_spec(ts, d), _vec_spec(d), _vec_spec(d)],
        out_shape=[jax.ShapeDtypeStruct((s, d), BF16), jax.ShapeDtypeStruct((1, d), F32),
                   jax.ShapeDtypeStruct((1, d), F32)],
        compiler_params=_cp("arbitrary"), name=name,
    )(dxo, y, gate, g)


def _pre_bwd(dh, xin, dres, avec, *, name):
    s, d = xin.shape
    ts = min(ROW_TILE, s)

    def body(dh_ref, x_ref, dres_ref, a_ref, dx_ref, dshift_ref, da_ref):
        xv, dhv = x_ref[...], dh_ref[...]
        r = lax.rsqrt(jnp.mean(xv * xv, axis=-1, keepdims=True) + EPS)
        xhat = xv * r
        dxhat = dhv * a_ref[...]
        dx_ref[...] = dres_ref[...] + r * (dxhat - xhat * jnp.mean(dxhat * xhat, axis=-1, keepdims=True))

        @pl.when(pl.program_id(0) == 0)
        def _():
            dshift_ref[...] = jnp.zeros_like(dshift_ref)
            da_ref[...] = jnp.zeros_like(da_ref)

        dshift_ref[...] += jnp.sum(dhv, axis=0, keepdims=True)
        da_ref[...] += jnp.sum(dhv * xhat, axis=0, keepdims=True)

    return pl.pallas_call(
        body, grid=(s // ts,),
        in_specs=[_row_spec(ts, d), _row_spec(ts, d), _row_spec(ts, d), _vec_spec(d)],
        out_specs=[_row_spec(ts, d), _vec_spec(d), _vec_spec(d)],
        out_shape=[jax.ShapeDtypeStruct((s, d), F32), jax.ShapeDtypeStruct((1, d), F32),
                   jax.ShapeDtypeStruct((1, d), F32)],
        compiler_params=_cp("arbitrary"), name=name,
    )(dh, xin, dres, avec)


def _tri(n, strict=False, upper=False):
    r = lax.broadcasted_iota(jnp.int32, (n, n), 0)
    c = lax.broadcasted_iota(jnp.int32, (n, n), 1)
    if upper:
        r, c = c, r
    return ((r > c) if strict else (r >= c)).astype(F32)


def _gates_fwd(ps, bf, w2p, b2, *, name):
    s = ps.shape[0]
    ts = min(GATE_TS, s)

    def body(ps_ref, bf_ref, w_ref, b2_ref, cum_ref, la_ref, carry_ref):
        @pl.when(pl.program_id(0) == 0)
        def _():
            carry_ref[...] = jnp.zeros_like(carry_ref)

        psv = ps_ref[...]
        lf = _log_sigmoid(psv + bf_ref[...])
        cum = _dot_nn(_tri(ts), lf, HIGHEST) + carry_ref[...]
        cum_ref[...] = cum
        carry_ref[...] = cum[ts - 1:ts, :]
        z = _dot_nn(psv, w_ref[...], HIGHEST) + b2_ref[...]
        la_ref[...] = _log_sigmoid(z) * (1.0 / GLA_TEMP)

    return pl.pallas_call(
        body, grid=(s // ts,),
        in_specs=[_row_spec(ts, SMALL_W), _vec_spec(SMALL_W),
                  pl.BlockSpec((SMALL_W, GLA_KW), lambda i: (0, 0)), _vec_spec(GLA_KW)],
        out_specs=[_row_spec(ts, SMALL_W), _row_spec(ts, GLA_KW)],
        out_shape=[jax.ShapeDtypeStruct((s, SMALL_W), F32), jax.ShapeDtypeStruct((s, GLA_KW), F32)],
        scratch_shapes=[pltpu.VMEM((1, SMALL_W), F32)],
        compiler_params=_cp("arbitrary"), name=name,
    )(ps, bf, w2p, b2)


def _gates_bwd(dck, ps, bf, w2p, b2, dla, *, name):
    s = ps.shape[0]
    ts = min(GATE_TS, s)
    nb = s // ts
    rev = lambda i: (nb - 1 - i, 0)

    def body(dck_ref, ps_ref, bf_ref, w_ref, b2_ref, dla_ref, dps_ref, dbf_ref, dw_ref, db2_ref, carry_ref):
        @pl.when(pl.program_id(0) == 0)
        def _():
            carry_ref[...] = jnp.zeros_like(carry_ref)
            dbf_ref[...] = jnp.zeros_like(dbf_ref)
            dw_ref[...] = jnp.zeros_like(dw_ref)
            db2_ref[...] = jnp.zeros_like(db2_ref)

        psv, dckv = ps_ref[...], dck_ref[...]
        dlf = _dot_nn(_tri(ts, upper=True), dckv, HIGHEST) + carry_ref[...]
        carry_ref[...] += jnp.sum(dckv, axis=0, keepdims=True)
        lane = lax.broadcasted_iota(jnp.int32, (ts, SMALL_W), 1)
        dff = jnp.where(lane < FOX_HEADS, dlf * _sigmoid(-(psv + bf_ref[...])), 0.0)
        z = _dot_nn(psv, w_ref[...], HIGHEST) + b2_ref[...]
        dz = dla_ref[...] * _sigmoid(-z) * (1.0 / GLA_TEMP)
        dps_ref[...] = (_dot_nt(dz, w_ref[...], HIGHEST) + dff).astype(BF16)
        dbf_ref[...] += jnp.sum(dff, axis=0, keepdims=True)
        dw_ref[...] += _dot_tn(psv, dz, HIGHEST)
        db2_ref[...] += jnp.sum(dz, axis=0, keepdims=True)

    return pl.pallas_call(
        body, grid=(nb,),
        in_specs=[pl.BlockSpec((ts, SMALL_W), rev), pl.BlockSpec((ts, SMALL_W), rev), _vec_spec(SMALL_W),
                  pl.BlockSpec((SMALL_W, GLA_KW), lambda i: (0, 0)), _vec_spec(GLA_KW),
                  pl.BlockSpec((ts, GLA_KW), rev)],
        out_specs=[pl.BlockSpec((ts, SMALL_W), rev), _vec_spec(SMALL_W),
                   pl.BlockSpec((SMALL_W, GLA_KW), lambda i: (0, 0)), _vec_spec(GLA_KW)],
        out_shape=[jax.ShapeDtypeStruct((s, SMALL_W), BF16), jax.ShapeDtypeStruct((1, SMALL_W), F32),
                   jax.ShapeDtypeStruct((SMALL_W, GLA_KW), F32), jax.ShapeDtypeStruct((1, GLA_KW), F32)],
        scratch_shapes=[pltpu.VMEM((1, SMALL_W), F32)],
        compiler_params=_cp("arbitrary"), name=name,
    )(dck, ps, bf, w2p, b2, dla)


def _causal_mask(i, j, tq, tk):
    rows = i * tq + lax.broadcasted_iota(jnp.int32, (tq, tk), 0)
    cols = j * tk + lax.broadcasted_iota(jnp.int32, (tq, tk), 1)
    return rows >= cols


def _hs(h, hd=FOX_HD):
    return slice(h * hd, (h + 1) * hd)


def _fox_fwd(proj, cum, cum_t, g_fox, *, name, side=None):
    s = proj.shape[0]
    tq, tk = min(FOX_TQ, s), min(FOX_TK, s)
    scale = FOX_HD ** -0.5
    n_si = len(side.inputs) if side else 0
    n_so = len(side.out_shapes) if side else 0
    grid = (s // tq, s // tk)

    def body(*refs):
        q_ref, k_ref, v_ref, cq_ref, ck_ref, g_ref = refs[:6]
        o_ref, n_ref, lse_ref = refs[6 + n_si:9 + n_si]
        m_sc, l_sc, acc_sc = refs[9 + n_si + n_so:12 + n_si + n_so]
        i, j = pl.program_id(0), pl.program_id(1)
        if side:
            side.run(refs[6:6 + n_si], refs[9 + n_si:9 + n_si + n_so], refs[12 + n_si + n_so:],
                     (i == 0) & (j == 0), (i == grid[0] - 1) & (j == grid[1] - 1))

        @pl.when(j == 0)
        def _():
            m_sc[...] = jnp.full_like(m_sc, NEG)
            l_sc[...] = jnp.zeros_like(l_sc)
            acc_sc[...] = jnp.zeros_like(acc_sc)

        @pl.when(j <= i)
        def _():
            mask = _causal_mask(i, j, tq, tk)
            for h in range(FOX_HEADS):
                sc = _dot_nt(q_ref[:, _hs(h)], k_ref[:, _hs(h)]) * scale
                sc = sc + (cq_ref[:, h:h + 1] - ck_ref[h:h + 1, :])
                sc = jnp.where(mask, sc, NEG)
                m_prev = m_sc[h]
                m_new = jnp.maximum(m_prev, jnp.max(sc, axis=-1, keepdims=True))
                alpha = jnp.exp(m_prev - m_new)
                p = jnp.exp(sc - m_new)
                l_sc[h] = alpha * l_sc[h] + jnp.sum(p, axis=-1, keepdims=True)
                acc_sc[:, _hs(h)] = alpha * acc_sc[:, _hs(h)] + _dot_nn(p.astype(BF16), v_ref[:, _hs(h)])
                m_sc[h] = m_new

        @pl.when(j == i)
        def _():
            lane = lax.broadcasted_iota(jnp.int32, (tq, 128), 1)
            lse = jnp.zeros((tq, 128), F32)
            for h in range(FOX_HEADS):
                o = acc_sc[:, _hs(h)] / l_sc[h]
                o_ref[:, _hs(h)] = o
                r = lax.rsqrt(jnp.mean(o * o, axis=-1, keepdims=True) + EPS)
                n_ref[:, _hs(h)] = (o * r * g_ref[h:h + 1, :]).astype(BF16)
                lse = jnp.where(lane == h, m_sc[h] + jnp.log(l_sc[h]), lse)
            lse_ref[...] = lse

    kv = lambda col: (lambda i, j: (jnp.minimum(j, i), col))
    any_spec = pl.BlockSpec(memory_space=pl.ANY)
    return pl.pallas_call(
        body, grid=grid,
        in_specs=[pl.BlockSpec((tq, FOX_W), lambda i, j: (i, 0)),
                  pl.BlockSpec((tk, FOX_W), kv(1)),
                  pl.BlockSpec((tk, FOX_W), kv(2)),
                  pl.BlockSpec((tq, 128), lambda i, j: (i, 0)),
                  pl.BlockSpec((FOX_HEADS, tk), lambda i, j: (0, jnp.minimum(j, i))),
                  pl.BlockSpec((FOX_HEADS, FOX_HD), lambda i, j: (0, 0))] + [any_spec] * n_si,
        out_specs=[pl.BlockSpec((tq, FOX_W), lambda i, j: (i, 0)),
                   pl.BlockSpec((tq, FOX_W), lambda i, j: (i, 0)),
                   pl.BlockSpec((tq, 128), lambda i, j: (i, 0))] + [any_spec] * n_so,
        out_shape=[jax.ShapeDtypeStruct((s, FOX_W), F32), jax.ShapeDtypeStruct((s, FOX_W), BF16),
                   jax.ShapeDtypeStruct((s, 128), F32)] + (side.out_shapes if side else []),
        scratch_shapes=[pltpu.VMEM((FOX_HEADS, tq, 1), F32), pltpu.VMEM((FOX_HEADS, tq, 1), F32),
                        pltpu.VMEM((tq, FOX_W), F32)] + (side.scratch() if side else []),
        compiler_params=_cp("arbitrary", "arbitrary"), name=name,
    )(proj, proj, proj, cum, cum_t, g_fox, *(side.inputs if side else []))


def _fox_p_ds(q, k, v, do, cq, ck, lse, delta, mask, scale):
    sc = _dot_nt(q, k) * scale + (cq - ck)
    sc = jnp.where(mask, sc, NEG)
    p = jnp.exp(sc - lse)
    dp = _dot_nt(do, v)
    return p, p * (dp - delta)


def _fox_bwd_dq(proj, do, cum, cum_t, lse, delta, *, name):
    s = proj.shape[0]
    tq, tk = min(FOX_TQ, s), min(FOX_TK, s)
    scale = FOX_HD ** -0.5

    def body(q_ref, k_ref, v_ref, do_ref, cq_ref, ck_ref, lse_ref, dl_ref, dq_ref, dcq_ref, acc_sc, row_sc):
        i, j = pl.program_id(0), pl.program_id(1)

        @pl.when(j == 0)
        def _():
            acc_sc[...] = jnp.zeros_like(acc_sc)
            row_sc[...] = jnp.zeros_like(row_sc)

        @pl.when(j <= i)
        def _():
            mask = _causal_mask(i, j, tq, tk)
            for h in range(FOX_HEADS):
                _, ds = _fox_p_ds(q_ref[:, _hs(h)], k_ref[:, _hs(h)], v_ref[:, _hs(h)], do_ref[:, _hs(h)],
                                  cq_ref[:, h:h + 1], ck_ref[h:h + 1, :], lse_ref[:, h:h + 1],
                                  dl_ref[:, h:h + 1], mask, scale)
                acc_sc[:, _hs(h)] += _dot_nn(ds.astype(BF16), k_ref[:, _hs(h)])
                row_sc[h] += jnp.sum(ds, axis=-1, keepdims=True)

        @pl.when(j == i)
        def _():
            dq_ref[...] = (acc_sc[...] * scale).astype(BF16)
            lane = lax.broadcasted_iota(jnp.int32, (tq, 128), 1)
            dcq = jnp.zeros((tq, 128), F32)
            for h in range(FOX_HEADS):
                dcq = jnp.where(lane == h, row_sc[h], dcq)
            dcq_ref[...] = dcq

    kv = lambda col: (lambda i, j: (jnp.minimum(j, i), col))
    qrow = lambda i, j: (i, 0)
    return pl.pallas_call(
        body, grid=(s // tq, s // tk),
        in_specs=[pl.BlockSpec((tq, FOX_W), qrow), pl.BlockSpec((tk, FOX_W), kv(1)), pl.BlockSpec((tk, FOX_W), kv(2)),
                  pl.BlockSpec((tq, FOX_W), qrow), pl.BlockSpec((tq, 128), qrow),
                  pl.BlockSpec((FOX_HEADS, tk), lambda i, j: (0, jnp.minimum(j, i))),
                  pl.BlockSpec((tq, 128), qrow), pl.BlockSpec((tq, 128), qrow)],
        out_specs=[pl.BlockSpec((tq, FOX_W), qrow), pl.BlockSpec((tq, 128), qrow)],
        out_shape=[jax.ShapeDtypeStruct((s, FOX_W), BF16), jax.ShapeDtypeStruct((s, 128), F32)],
        scratch_shapes=[pltpu.VMEM((tq, FOX_W), F32), pltpu.VMEM((FOX_HEADS, tq, 1), F32)],
        compiler_params=_cp("parallel", "arbitrary"), name=name,
    )(proj, proj, proj, do, cum, cum_t, lse, delta)


def _fox_bwd_dkv(proj, do, cum, cum_t, lse, delta, *, name):
    s = proj.shape[0]
    tq, tk = min(FOX_TQ, s), min(FOX_TK, s)
    nq = s // tq
    scale = FOX_HD ** -0.5

    def body(q_ref, k_ref, v_ref, do_ref, cq_ref, ck_ref, lse_ref, dl_ref, dk_ref, dv_ref, dck_ref,
             dk_sc, dv_sc, dck_sc):
        j, i = pl.program_id(0), pl.program_id(1)

        @pl.when(i == 0)
        def _():
            dk_sc[...] = jnp.zeros_like(dk_sc)
            dv_sc[...] = jnp.zeros_like(dv_sc)
            dck_sc[...] = jnp.zeros_like(dck_sc)

        @pl.when(i >= j)
        def _():
            mask = _causal_mask(i, j, tq, tk)
            for h in range(FOX_HEADS):
                p, ds = _fox_p_ds(q_ref[:, _hs(h)], k_ref[:, _hs(h)], v_ref[:, _hs(h)], do_ref[:, _hs(h)],
                                  cq_ref[:, h:h + 1], ck_ref[h:h + 1, :], lse_ref[:, h:h + 1],
                                  dl_ref[:, h:h + 1], mask, scale)
                dv_sc[:, _hs(h)] += _dot_tn(p.astype(BF16), do_ref[:, _hs(h)])
                dk_sc[:, _hs(h)] += _dot_tn(ds.astype(BF16), q_ref[:, _hs(h)])
                dck_sc[h:h + 1, :] -= jnp.sum(ds, axis=0, keepdims=True)

        @pl.when(i == nq - 1)
        def _():
            dk_ref[...] = (dk_sc[...] * scale).astype(BF16)
            dv_ref[...] = dv_sc[...].astype(BF16)
            dck_ref[...] = dck_sc[...]

    qrow = lambda j, i: (jnp.maximum(i, j), 0)
    krow = lambda col: (lambda j, i: (j, col))
    return pl.pallas_call(
        body, grid=(s // tk, nq),
        in_specs=[pl.BlockSpec((tq, FOX_W), qrow), pl.BlockSpec((tk, FOX_W), krow(1)),
                  pl.BlockSpec((tk, FOX_W), krow(2)),
                  pl.BlockSpec((tq, FOX_W), qrow), pl.BlockSpec((tq, 128), qrow),
                  pl.BlockSpec((FOX_HEADS, tk), lambda j, i: (0, j)),
                  pl.BlockSpec((tq, 128), qrow), pl.BlockSpec((tq, 128), qrow)],
        out_specs=[pl.BlockSpec((tk, FOX_W), lambda j, i: (j, 0)), pl.BlockSpec((tk, FOX_W), lambda j, i: (j, 0)),
                   pl.BlockSpec((FOX_HEADS, tk), lambda j, i: (0, j))],
        out_shape=[jax.ShapeDtypeStruct((s, FOX_W), BF16), jax.ShapeDtypeStruct((s, FOX_W), BF16),
                   jax.ShapeDtypeStruct((FOX_HEADS, s), F32)],
        scratch_shapes=[pltpu.VMEM((tk, FOX_W), F32), pltpu.VMEM((tk, FOX_W), F32),
                        pltpu.VMEM((FOX_HEADS, tk), F32)],
        compiler_params=_cp("parallel", "arbitrary"), name=name,
    )(proj, proj, proj, do, cum, cum_t, lse, delta)


def _head_norm_bwd(dn_in, o, g, gr_src, *, nh, hd, gr_col, name):
    s, w = o.shape
    ts = min(ROW_TILE, s)
    gated = gr_src is not None

    def body(*refs):
        if gated:
            dn_ref, o_ref, g_ref, gr_ref, do_ref, dgr_ref, dl_ref, dg_ref = refs
        else:
            dn_ref, o_ref, g_ref, do_ref, dl_ref, dg_ref = refs

        @pl.when(pl.program_id(0) == 0)
        def _():
            dg_ref[...] = jnp.zeros_like(dg_ref)

        lane = lax.broadcasted_iota(jnp.int32, (ts, 128), 1)
        delta = jnp.zeros((ts, 128), F32)
        for h in range(nh):
            sl = _hs(h, hd)
            ov = o_ref[:, sl]
            dnv = dn_ref[:, sl].astype(F32)
            gv = g_ref[h:h + 1, :]
            r = lax.rsqrt(jnp.mean(ov * ov, axis=-1, keepdims=True) + EPS)
            ohat = ov * r
            if gated:
                grv = gr_ref[:, sl].astype(F32)
                sig = _sigmoid(grv)
                dgr_ref[:, sl] = (dnv * (ohat * gv) * (sig * (1.0 + grv * (1.0 - sig)))).astype(BF16)
                dnv = dnv * (grv * sig)
            dg_ref[h:h + 1, :] += jnp.sum(dnv * ohat, axis=0, keepdims=True)
            dohat = dnv * gv
            do = r * (dohat - ohat * jnp.mean(dohat * ohat, axis=-1, keepdims=True))
            do_ref[:, sl] = do.astype(BF16)
            delta = jnp.where(lane == h, jnp.sum(do * ov, axis=-1, keepdims=True), delta)
        dl_ref[...] = delta

    in_specs = [_row_spec(ts, w), _row_spec(ts, w), pl.BlockSpec((nh, hd), lambda i: (0, 0))]
    args = [dn_in, o, g]
    out_specs = [_row_spec(ts, w)]
    out_shape = [jax.ShapeDtypeStruct((s, w), BF16)]
    if gated:
        in_specs.append(pl.BlockSpec((ts, w), lambda i: (i, gr_col)))
        args.append(gr_src)
        out_specs.append(_row_spec(ts, w))
        out_shape.append(jax.ShapeDtypeStruct((s, w), BF16))
    out_specs += [_row_spec(ts, 128), pl.BlockSpec((nh, hd), lambda i: (0, 0))]
    out_shape += [jax.ShapeDtypeStruct((s, 128), F32), jax.ShapeDtypeStruct((nh, hd), F32)]
    return pl.pallas_call(
        body, grid=(s // ts,), in_specs=in_specs, out_specs=out_specs, out_shape=out_shape,
        compiler_params=_cp("arbitrary"), name=name,
    )(*args)


GQ_BLK = 3 * FOX_W // GLA_DK
GK_BLK = GQ_BLK + GLA_HEADS
GV_BLK = (3 * FOX_W + 2 * GLA_KW) // GLA_DV
GR_BLK = GV_BLK + GLA_HEADS


def _gla_chunk_terms(la):
    cum = _dot_nn(_tri(CHUNK), la, HIGHEST)
    total = cum[CHUNK - 1:CHUNK, :]
    return jnp.exp(total - cum), jnp.exp(total)


def _gla_fwd(proj, log_a, g_gla, *, name):
    s = proj.shape[0]
    rows = min(GLA_ROWS, s)
    cb = rows // CHUNK
    nblk = s // rows
    scale = GLA_DK ** -0.5

    def body(q_ref, k_ref, v_ref, gr_ref, la_ref, g_ref, o_ref, n_ref, st_ref, st_sc):
        h = pl.program_id(0)

        @pl.when(pl.program_id(1) == 0)
        def _():
            st_sc[...] = jnp.zeros_like(st_sc)

        gv = g_ref[pl.ds(h, 1), :]
        for ci in range(cb):
            sl = slice(ci * CHUNK, (ci + 1) * CHUNK)
            e, dec = _gla_chunk_terms(la_ref[sl, :])
            k_dec = (k_ref[sl, :].astype(F32) * e).astype(BF16)
            st = st_sc[...] * dec + _dot_tn(v_ref[sl, :], k_dec)
            st_sc[...] = st
            st_ref[0, ci] = st
            qs = (q_ref[sl, :].astype(F32) * scale).astype(BF16)
            o = _dot_nt(qs, st.astype(BF16))
            o_ref[sl, :] = o
            r = lax.rsqrt(jnp.mean(o * o, axis=-1, keepdims=True) + EPS)
            grv = gr_ref[sl, :].astype(F32)
            n_ref[sl, :] = (o * r * gv * (grv * _sigmoid(grv))).astype(BF16)

    return pl.pallas_call(
        body, grid=(GLA_HEADS, nblk),
        in_specs=[pl.BlockSpec((rows, GLA_DK), lambda h, n: (n, GQ_BLK + h)),
                  pl.BlockSpec((rows, GLA_DK), lambda h, n: (n, GK_BLK + h)),
                  pl.BlockSpec((rows, GLA_DV), lambda h, n: (n, GV_BLK + h)),
                  pl.BlockSpec((rows, GLA_DV), lambda h, n: (n, GR_BLK + h)),
                  pl.BlockSpec((rows, GLA_DK), lambda h, n: (n, h)),
                  pl.BlockSpec((GLA_HEADS, GLA_DV), lambda h, n: (0, 0))],
        out_specs=[pl.BlockSpec((rows, GLA_DV), lambda h, n: (n, h)),
                   pl.BlockSpec((rows, GLA_DV), lambda h, n: (n, h)),
                   pl.BlockSpec((1, cb, GLA_DV, GLA_DK), lambda h, n: (h, n, 0, 0))],
        out_shape=[jax.ShapeDtypeStruct((s, GLA_W), F32), jax.ShapeDtypeStruct((s, GLA_W), BF16),
                   jax.ShapeDtypeStruct((GLA_HEADS, s // CHUNK, GLA_DV, GLA_DK), F32)],
        scratch_shapes=[pltpu.VMEM((GLA_DV, GLA_DK), F32)],
        compiler_params=_cp("parallel", "arbitrary"), name=name,
    )(proj, proj, proj, proj, log_a, g_gla)


def _gla_bwd(proj, log_a, do, states, *, name):
    s = proj.shape[0]
    rows = min(GLA_ROWS, s)
    cb = rows // CHUNK
    nblk = s // rows
    scale = GLA_DK ** -0.5

    def body(q_ref, k_ref, v_ref, la_ref, do_ref, st_ref, prev_ref, dq_ref, dk_ref, dv_ref, dla_ref, g_sc):
        nrev = pl.program_id(1)
        blk = nblk - 1 - nrev

        @pl.when(nrev == 0)
        def _():
            g_sc[...] = jnp.zeros_like(g_sc)

        for ci in reversed(range(cb)):
            sl = slice(ci * CHUNK, (ci + 1) * CHUNK)
            e, dec = _gla_chunk_terms(la_ref[sl, :])
            kd = k_ref[sl, :].astype(F32) * e
            qs = (q_ref[sl, :].astype(F32) * scale).astype(BF16)
            dov = do_ref[sl, :]
            st = st_ref[0, ci]
            if ci > 0:
                st_prev = st_ref[0, ci - 1]
            else:
                st_prev = prev_ref[0, 0] * (blk > 0).astype(F32)
            dq_ref[sl, :] = (_dot_nn(dov, st.astype(BF16)) * scale).astype(BF16)
            gt = g_sc[...] + _dot_tn(dov, qs)
            gtb = gt.astype(BF16)
            dkd = _dot_nn(v_ref[sl, :], gtb)
            dv_ref[sl, :] = _dot_nt(kd.astype(BF16), gtb).astype(BF16)
            dk_ref[sl, :] = (dkd * e).astype(BF16)
            ddec = jnp.sum(gt * st_prev, axis=0, keepdims=True) * dec
            dla_ref[sl, :] = _dot_nn(_tri(CHUNK, strict=True), dkd * kd, HIGHEST) + ddec
            g_sc[...] = gt * dec

    rev = lambda col0: (lambda h, n: (nblk - 1 - n, col0 + h))
    return pl.pallas_call(
        body, grid=(GLA_HEADS, nblk),
        in_specs=[pl.BlockSpec((rows, GLA_DK), rev(GQ_BLK)),
                  pl.BlockSpec((rows, GLA_DK), rev(GK_BLK)),
                  pl.BlockSpec((rows, GLA_DV), rev(GV_BLK)),
                  pl.BlockSpec((rows, GLA_DK), rev(0)),
                  pl.BlockSpec((rows, GLA_DV), rev(0)),
                  pl.BlockSpec((1, cb, GLA_DV, GLA_DK), lambda h, n: (h, nblk - 1 - n, 0, 0)),
                  pl.BlockSpec((1, 1, GLA_DV, GLA_DK),
                               lambda h, n: (h, jnp.maximum((nblk - 1 - n) * cb - 1, 0), 0, 0))],
        out_specs=[pl.BlockSpec((rows, GLA_DK), rev(0)), pl.BlockSpec((rows, GLA_DK), rev(0)),
                   pl.BlockSpec((rows, GLA_DV), rev(0)), pl.BlockSpec((rows, GLA_DK), rev(0))],
        out_shape=[jax.ShapeDtypeStruct((s, GLA_KW), BF16), jax.ShapeDtypeStruct((s, GLA_KW), BF16),
                   jax.ShapeDtypeStruct((s, GLA_W), BF16), jax.ShapeDtypeStruct((s, GLA_KW), F32)],
        scratch_shapes=[pltpu.VMEM((GLA_DV, GLA_DK), F32)],
        compiler_params=_cp("parallel", "arbitrary"), name=name,
    )(proj, proj, proj, log_a, do, states, states)


def _row_tile(r):
    tr = min(ROW_TILE, r)
    while r % tr or tr % 8:
        tr -= 1
    return tr


def _adamw_math(w, g, m, v):
    m = ADAM_B1 * m + (1.0 - ADAM_B1) * g
    v = ADAM_B2 * v + (1.0 - ADAM_B2) * (g * g)
    m_hat = m / (1.0 - ADAM_B1 ** ADAM_STEP)
    v_hat = v / (1.0 - ADAM_B2 ** ADAM_STEP)
    delta = -ADAM_LR * (m_hat / (jnp.sqrt(v_hat) + ADAM_EPS) + ADAM_WD * w)
    return delta, m, v


def _adam(g, w, m, v, *, name):
    r, c = w.shape
    tr = _row_tile(r)
    assert r % tr == 0

    def body(g_ref, w_ref, m_ref, v_ref, d_ref, mo_ref, vo_ref):
        d, mn, vn = _adamw_math(w_ref[...], g_ref[...], m_ref[...], v_ref[...])
        d_ref[...] = d
        mo_ref[...] = mn
        vo_ref[...] = vn

    spec = pl.BlockSpec((tr, c), lambda i: (i, 0))
    return pl.pallas_call(
        body, grid=(r // tr,), in_specs=[spec] * 4, out_specs=[spec] * 3,
        out_shape=[jax.ShapeDtypeStruct((r, c), F32)] * 3,
        compiler_params=_cp("parallel"), name=name,
    )(g, w, m, v)


def _ada_grad_adam(c_all_t, dmod_cols, w, m, v, *, name):
    r, c = w.shape
    tr, tc = min(512, r), min(1024, c)

    def body(ct_ref, dm_ref, w_ref, m_ref, v_ref, g_ref, d_ref, mo_ref, vo_ref):
        g = _dot_nn(ct_ref[...], dm_ref[...], HIGHEST)
        g_ref[...] = g
        d, mn, vn = _adamw_math(w_ref[...], g, m_ref[...], v_ref[...])
        d_ref[...] = d
        mo_ref[...] = mn
        vo_ref[...] = vn

    spec = pl.BlockSpec((tr, tc), lambda i, j: (i, j))
    nb = c_all_t.shape[1]
    return pl.pallas_call(
        body, grid=(r // tr, c // tc),
        in_specs=[pl.BlockSpec((tr, nb), lambda i, j: (i, 0)), pl.BlockSpec((nb, tc), lambda i, j: (0, j)),
                  spec, spec, spec],
        out_specs=[spec] * 4, out_shape=[jax.ShapeDtypeStruct((r, c), F32)] * 4,
        compiler_params=_cp("parallel", "parallel"), name=name,
    )(c_all_t, dmod_cols, w, m, v)


def _mod_shard(c_all, w, b, *, name):
    k, c = w.shape
    tc = min(512, c)
    nb = c_all.shape[0]

    def body(c_ref, w_ref, b_ref, o_ref):
        o_ref[...] = _dot_nn(c_ref[...], w_ref[...], HIGHEST) + b_ref[...]

    return pl.pallas_call(
        body, grid=(c // tc,),
        in_specs=[pl.BlockSpec((nb, k), lambda j: (0, 0)), pl.BlockSpec((k, tc), lambda j: (0, j)),
                  pl.BlockSpec((1, tc), lambda j: (0, j))],
        out_specs=pl.BlockSpec((nb, tc), lambda j: (0, j)),
        out_shape=jax.ShapeDtypeStruct((nb, c), F32),
        compiler_params=_cp("parallel"), name=name,
    )(c_all, w, b)


def _silu_rows(c, *, name):
    def body(c_ref, o_ref):
        cv = c_ref[...]
        o_ref[...] = cv * _sigmoid(cv)

    return pl.pallas_call(body, out_shape=jax.ShapeDtypeStruct(c.shape, F32), name=name)(c)


def _pair_sum(g, got, idx, *, name):
    p, r, c = g.shape
    hr = r // 2
    tr = _row_tile(hr)
    nb = hr // tr

    def body(idx_ref, a_ref, b_ref, o_ref):
        o_ref[...] = (a_ref[...].astype(F32) + b_ref[...].astype(F32)).astype(BF16)

    half_spec = pl.BlockSpec((1, tr, c), lambda i, j, idx_ref: (i, j, 0))
    return pl.pallas_call(
        body,
        grid_spec=pltpu.PrefetchScalarGridSpec(
            num_scalar_prefetch=1, grid=(p, nb),
            in_specs=[pl.BlockSpec((1, tr, c), lambda i, j, idx_ref: (i, idx_ref[0] * nb + j, 0)), half_spec],
            out_specs=half_spec),
        out_shape=jax.ShapeDtypeStruct((p, hr, c), BF16),
        compiler_params=_cp("parallel", "parallel"), name=name,
    )(idx, g, got)


def _final_sum(own, parts, idx, *, name):
    _, hr, c = own.shape
    tr = _row_tile(hr)
    nb = hr // tr

    def body(idx_ref, own_ref, parts_ref, o_ref):
        acc = own_ref[0].astype(F32)
        for q in range(3):
            acc = acc + parts_ref[q].astype(F32)
        o_ref[...] = acc

    return pl.pallas_call(
        body,
        grid_spec=pltpu.PrefetchScalarGridSpec(
            num_scalar_prefetch=1, grid=(nb,),
            in_specs=[pl.BlockSpec((1, tr, c), lambda i, idx_ref: (idx_ref[1], i, 0)),
                      pl.BlockSpec((3, tr, c), lambda i, idx_ref: (0, i, 0))],
            out_specs=pl.BlockSpec((tr, c), lambda i, idx_ref: (idx_ref[0] * nb + i, 0))),
        out_shape=jax.ShapeDtypeStruct((2 * hr, c), F32),
        compiler_params=_cp("parallel"), name=name,
    )(idx, own, parts)


def _stack_sum(x, *, name):
    p, r, c = x.shape
    tr = _row_tile(r)

    def body(x_ref, o_ref):
        acc = x_ref[0].astype(F32)
        for q in range(1, p):
            acc = acc + x_ref[q].astype(F32)
        o_ref[...] = acc

    return pl.pallas_call(
        body, grid=(r // tr,),
        in_specs=[pl.BlockSpec((p, tr, c), lambda i: (0, i, 0))],
        out_specs=pl.BlockSpec((tr, c), lambda i: (i, 0)),
        out_shape=jax.ShapeDtypeStruct((r, c), F32),
        compiler_params=_cp("parallel"), name=name,
    )(x)


def _place():
    x, y, c = lax.axis_index("x"), lax.axis_index("y"), lax.axis_index("c")
    chips = [(1 - x, y), (x, 1 - y), (1 - x, 1 - y)]
    return x, y, c, chips


def _gather8(x_shard, *, name):
    m_per, n = x_shard.shape

    def body(x_ref, out_ref, send_sems, recv_sems, local_sem):
        x, y, c, chips = _place()
        me, sibling = (x, y, c), (x, y, 1 - c)

        def rows(px, py, pc):
            return out_ref.at[pl.ds((4 * px + 2 * py + pc) * m_per, m_per), :]

        def copy(k, block, to, src=None):
            return pltpu.make_async_remote_copy(
                src_ref=rows(*block) if src is None else src, dst_ref=rows(*block),
                send_sem=send_sems.at[k], recv_sem=recv_sems.at[k], device_id=to, device_id_type=MESH)

        mine = pltpu.make_async_copy(x_ref, rows(*me), local_sem)
        mine.start()
        first = [copy(0, me, sibling, src=x_ref)]
        first += [copy(1 + j, me, (*chip, c), src=x_ref) for j, chip in enumerate(chips)]
        for cp in first:
            cp.start()
        passed = [copy(4 + j, (*chip, c), sibling) for j, chip in enumerate(chips)]
        for j, chip in enumerate(chips):
            copy(1 + j, (*chip, c), me).wait_recv()
            passed[j].start()
        copy(0, sibling, me).wait_recv()
        for j, chip in enumerate(chips):
            copy(4 + j, (*chip, 1 - c), me).wait_recv()
        for cp in first + passed:
            cp.wait_send()
        mine.wait()

    return pl.pallas_call(
        body,
        out_shape=jax.ShapeDtypeStruct((8 * m_per, n), x_shard.dtype),
        in_specs=[pl.BlockSpec(memory_space=pltpu.VMEM)],
        out_specs=pl.BlockSpec(memory_space=pltpu.VMEM),
        scratch_shapes=[pltpu.SemaphoreType.DMA((7,)), pltpu.SemaphoreType.DMA((7,)), pltpu.SemaphoreType.DMA],
        name=name,
    )(x_shard)


def _gather_weights(shards, *, name):
    return _comm_call(lambda ins, outs: [cp for i, o in zip(ins, outs) for cp in _plan_gather_ici(i, o)],
                      shards, [jax.ShapeDtypeStruct((4,) + s.shape, s.dtype) for s in shards], name=name)


def _plan_start(plan, send_sems, recv_sems):
    for k, (src, dst, _, peer) in enumerate(plan):
        pltpu.make_async_remote_copy(src_ref=src, dst_ref=dst, send_sem=send_sems.at[k], recv_sem=recv_sems.at[k],
                                     device_id=peer, device_id_type=MESH).start()


def _plan_wait(plan, send_sems, recv_sems):
    for k, (src, _, land, peer) in enumerate(plan):
        pltpu.make_async_remote_copy(src_ref=src, dst_ref=land, send_sem=send_sems.at[k], recv_sem=recv_sems.at[k],
                                     device_id=peer, device_id_type=MESH).wait_recv()
    for k, (src, dst, _, peer) in enumerate(plan):
        pltpu.make_async_remote_copy(src_ref=src, dst_ref=dst, send_sem=send_sems.at[k], recv_sem=recv_sems.at[k],
                                     device_id=peer, device_id_type=MESH).wait_send()


def _rows_half(ref, hc, axis):
    hr = ref.shape[axis] // 2
    idx = [slice(None)] * len(ref.shape)
    idx[axis] = pl.ds(hc * hr, hr)
    return ref.at[tuple(idx)]


def _plan_gather_ici(shard, full):
    x, y, c, chips = _place()
    src = _rows_half(shard, c, 0)
    return [(src, _rows_half(full.at[2 * x + y], c, 0), _rows_half(full.at[2 * cx + cy], c, 0), (cx, cy, c))
            for cx, cy in chips]


def _plan_gather_d2d(full):
    x, y, c, chips = _place()
    plan = []
    for cx, cy in chips:
        slot = full.at[2 * cx + cy]
        plan.append((_rows_half(slot, c, 0), _rows_half(slot, c, 0), _rows_half(slot, 1 - c, 0), (x, y, 1 - c)))
    return plan


def _plan_pair(grad, got):
    x, y, c, _ = _place()
    return [(_rows_half(grad, 1 - c, 1), got, got, (x, y, 1 - c))]


def _plan_shard_ici(sums, parts):
    _, _, c, chips = _place()
    return [(sums.at[2 * cx + cy], parts.at[k], parts.at[k], (cx, cy, c)) for k, (cx, cy) in enumerate(chips)]


def _plan_half(buf):
    x, y, c, _ = _place()
    mine = _rows_half(buf, c, 0)
    return [(mine, mine, _rows_half(buf, 1 - c, 0), (x, y, 1 - c))]


def _comm_call(plan_fn, inputs, out_shapes, *, name, aliases=None):
    ni, no = len(inputs), len(out_shapes)

    def body(*refs):
        plan = plan_fn(refs[:ni], refs[ni:ni + no])
        send_sems, recv_sems = refs[ni + no:]
        _plan_start(plan, send_sems, recv_sems)
        _plan_wait(plan, send_sems, recv_sems)

    any_spec = pl.BlockSpec(memory_space=pl.ANY)
    n_copies = 3 * max(ni, no)
    return pl.pallas_call(
        body, out_shape=list(out_shapes), in_specs=[any_spec] * ni, out_specs=[any_spec] * no,
        scratch_shapes=[pltpu.SemaphoreType.DMA((n_copies,)), pltpu.SemaphoreType.DMA((n_copies,))],
        input_output_aliases=aliases or {}, name=name,
    )(*inputs)


def _gather_forward(fulls, *, name):
    return _comm_call(lambda ins, outs: [cp for o in outs for cp in _plan_gather_d2d(o)],
                      fulls, [jax.ShapeDtypeStruct(f.shape, f.dtype) for f in fulls], name=name,
                      aliases={k: k for k in range(len(fulls))})


def _pair_exchange(grads, *, name):
    return _comm_call(lambda ins, outs: [cp for i, o in zip(ins, outs) for cp in _plan_pair(i, o)],
                      grads, [jax.ShapeDtypeStruct((4, g.shape[1] // 2, g.shape[2]), g.dtype) for g in grads],
                      name=name)


def _half_exchange(bufs, *, name):
    return _comm_call(lambda ins, outs: [cp for o in outs for cp in _plan_half(o)],
                      bufs, [jax.ShapeDtypeStruct(b.shape, b.dtype) for b in bufs], name=name,
                      aliases={k: k for k in range(len(bufs))})


def _split_w_in(w_in_full):
    d = w_in_full.shape[0]
    main = jnp.concatenate([w_in_full[:, 0:3072], w_in_full[:, 3080:5128], w_in_full[:, 5144:6168]], axis=1)
    small = jnp.concatenate([w_in_full[:, 3072:3080], w_in_full[:, 5128:5144],
                             jnp.zeros((d, SMALL_W - 24), w_in_full.dtype)], axis=1)
    return main, small


def _merge_dw_in(dw_main, dw_small):
    return jnp.concatenate([dw_main[:, 0:3072], dw_small[:, 0:8], dw_main[:, 3072:5120], dw_small[:, 8:24],
                            dw_main[:, 5120:6144]], axis=1)


def _gather_side(shards):
    return _Side(shards, [jax.ShapeDtypeStruct((4,) + w.shape, w.dtype) for w in shards],
                 lambda ins, outs: [cp for i, o in zip(ins, outs) for cp in _plan_gather_ici(i, o)], 3 * len(shards))


def _finish_gather(fulls, owns, chip, *, name):
    fulls = _gather_forward(list(fulls), name=name)
    return [lax.dynamic_update_index_in_dim(f, o, chip, 0) for f, o in zip(fulls, owns)]


def _shard_side(sums):
    return _Side([sums], [jax.ShapeDtypeStruct((3,) + sums.shape[1:], sums.dtype)],
                 lambda ins, outs: _plan_shard_ici(ins[0], outs[0]), 3)


def _chip_sum(grad, idx, tag):
    got, = _pair_exchange([grad], name=f"grad_pair_exchange_{tag}")
    return _pair_sum(grad, got, idx, name=f"grad_pair_sum_{tag}")


def _local_step(x, target, mod, g_pre_mix, g_post_mix, g_pre_mlp, g_post_mlp, w_in_full, b_fgate, w_gla_a2,
                b_gla_a2, g_fox, g_gla, own_w_out, own_w_mlp_in, own_w_mlp_out, chip, idx):
    s, d = x.shape
    shift_m, scale_m, gate_m, shift_f, scale_f, gate_f = [mod[:, i * d:(i + 1) * d] for i in range(6)]
    a1 = g_pre_mix * (1.0 + scale_m)
    a2 = g_pre_mlp * (1.0 + scale_f)
    w_main, w_small = _split_w_in(w_in_full)
    bf = jnp.concatenate([b_fgate, jnp.zeros((1, SMALL_W - FOX_HEADS), F32)], axis=1)
    w2p = jnp.zeros((SMALL_W, GLA_KW), F32).at[FOX_HEADS:FOX_HEADS + GLA_RANK].set(w_gla_a2)

    h1 = _pre_fwd(x, a1, shift_m, name="pre_mix_fwd")
    proj, gw_out = _mm(h1, w_main, mode="nn", out_dtypes=[BF16], name="in_proj_main",
                       side=_gather_side([own_w_out]))
    ps, = _mm(h1, w_small, mode="nn", out_dtypes=[F32], name="in_proj_small")
    gw_out, = _finish_gather([gw_out], [own_w_out], chip, name="gather_w_out_d2d")
    w_out_full = gw_out.reshape(-1, d)
    cum, log_a = _gates_fwd(ps, bf, w2p, b_gla_a2, name="gates_fwd")
    cum_t = cum[:, :FOX_HEADS].T
    o_fox, fox_n, lse, gw_mlp_in, gw_mlp_out = _fox_fwd(proj, cum, cum_t, g_fox, name="fox_fwd",
                                                        side=_gather_side([own_w_mlp_in, own_w_mlp_out]))
    gw_mlp_in, gw_mlp_out = _finish_gather([gw_mlp_in, gw_mlp_out], [own_w_mlp_in, own_w_mlp_out], chip,
                                           name="gather_w_mlp_d2d")
    w_mlp_in_full = jnp.transpose(gw_mlp_in, (1, 0, 2)).reshape(d, -1)
    w_mlp_out_full = gw_mlp_out.reshape(-1, d)
    o_gla, gla_n, states = _gla_fwd(proj, log_a, g_gla, name="gla_fwd")
    mixed = jnp.concatenate([fox_n, gla_n], axis=1)
    y1, = _mm(mixed, w_out_full, mode="nn", out_dtypes=[F32], name="out_proj")
    x1 = _post_fwd(x, y1, gate_m, g_post_mix, name="post_mix_fwd")
    h2 = _pre_fwd(x1, a2, shift_f, name="pre_mlp_fwd")

    def mlp_act(acc):
        r = jnp.maximum(acc, 0.0)
        return acc, r * r

    u, act = _mm(h2, w_mlp_in_full, mode="nn", out_dtypes=[BF16, BF16], epi=mlp_act, name="mlp_in")
    y2, = _mm(act, w_mlp_out_full, mode="nn", out_dtypes=[F32], name="mlp_out")
    dx2, loss_part = _post_fwd_loss(x1, y2, gate_f, g_post_mlp, target, name="post_mlp_fwd_loss")

    dy2, dgate_f, dg_post_mlp = _post_bwd(dx2, y2, gate_f, g_post_mlp, name="post_mlp_bwd")
    dw_mlp_out, = _mm(act, dy2, mode="tn", out_dtypes=[BF16], name="dw_mlp_out")
    sum_mlp_out = _chip_sum(dw_mlp_out.reshape(4, D_FF // 4, d), idx, "mlp_out")

    def act_bwd(acc, uv):
        return (acc * (2.0 * jnp.maximum(uv.astype(F32), 0.0)),)

    du, parts_mlp_out = _mm(dy2, w_mlp_out_full, mode="nt", out_dtypes=[BF16], extras=[u], epi=act_bwd,
                            name="d_mlp_hidden", side=_shard_side(sum_mlp_out))
    nj = D_FF // 4 // min(MM_T, D_FF // 4)
    tmw = min(MM_T, d)
    dw_mlp_in, = _mm(h2, du, mode="tn", out_dtypes=[BF16], name="dw_mlp_in",
                     out_shapes=[jax.ShapeDtypeStruct((4, d, D_FF // 4), BF16)],
                     out_specs=[pl.BlockSpec((1, tmw, min(MM_T, D_FF // 4)), lambda i, j, kk: (j // nj, i, j % nj))])
    sum_mlp_in = _chip_sum(dw_mlp_in, idx, "mlp_in")
    dh2, parts_mlp_in = _mm(du, w_mlp_in_full, mode="nt", out_dtypes=[F32], name="d_mlp_in",
                            side=_shard_side(sum_mlp_in))
    dx1, dshift_f, da2 = _pre_bwd(dh2, x1, dx2, a2, name="pre_mlp_bwd")

    dy1, dgate_m, dg_post_mix = _post_bwd(dx1, y1, gate_m, g_post_mix, name="post_mix_bwd")
    dw_out, = _mm(mixed, dy1, mode="tn", out_dtypes=[BF16], name="dw_out")
    sum_out = _chip_sum(dw_out.reshape(4, d // 4, d), idx, "out")
    dmixed, parts_out = _mm(dy1, w_out_full, mode="nt", out_dtypes=[BF16], name="d_mixed", side=_shard_side(sum_out))
    do_fox, delta, dg_fox = _head_norm_bwd(dmixed[:, :FOX_W], o_fox, g_fox, None, nh=FOX_HEADS, hd=FOX_HD,
                                           gr_col=0, name="fox_norm_bwd")
    do_gla, dgr, _, dg_gla = _head_norm_bwd(dmixed[:, FOX_W:], o_gla, g_gla, proj, nh=GLA_HEADS, hd=GLA_DV,
                                            gr_col=(3 * FOX_W + 2 * GLA_KW + GLA_W) // GLA_W, name="gla_norm_bwd")
    dq_fox, dcq = _fox_bwd_dq(proj, do_fox, cum, cum_t, lse, delta, name="fox_bwd_dq")
    dk_fox, dv_fox, dck_t = _fox_bwd_dkv(proj, do_fox, cum, cum_t, lse, delta, name="fox_bwd_dkv")
    dgq, dgk, dgv, dla = _gla_bwd(proj, log_a, do_gla, states, name="gla_bwd")
    dck = dcq + jnp.concatenate([dck_t.T, jnp.zeros((s, SMALL_W - FOX_HEADS), F32)], axis=1)
    dps, dbf, dw2p, db2 = _gates_bwd(dck, ps, bf, w2p, b_gla_a2, dla, name="gates_bwd")
    dproj = jnp.concatenate([dq_fox, dk_fox, dv_fox, dgq, dgk, dgv, dgr], axis=1)
    dw_main, = _mm(h1, dproj, mode="tn", out_dtypes=[BF16], name="dw_in_main")
    dw_small, = _mm(h1, dps, mode="tn", out_dtypes=[BF16], name="dw_in_small")
    cs_in = w_in_full.shape[1] // 4
    dw_in = jnp.transpose(_merge_dw_in(dw_main, dw_small).reshape(d, 4, cs_in), (1, 0, 2))
    sum_in = _chip_sum(dw_in, idx, "in")
    dh1_small, = _mm(dps, w_small, mode="nt", out_dtypes=[F32], name="d_h1_small")
    dh1, parts_in = _mm(dproj, w_main, mode="nt", out_dtypes=[F32], extras=[dh1_small],
                        epi=lambda acc, e: (acc + e,), name="d_h1", side=_shard_side(sum_in))
    grad_x, dshift_m, da1 = _pre_bwd(dh1, x, dx1, a1, name="pre_mix_bwd")
    bufs = [_final_sum(sm, pt, idx, name=f"grad_final_sum_{tag}")
            for tag, sm, pt in [("in", sum_in, parts_in), ("out", sum_out, parts_out),
                                ("mlp_in", sum_mlp_in, parts_mlp_in), ("mlp_out", sum_mlp_out, parts_mlp_out)]]

    dmod = jnp.concatenate([dshift_m, da1 * g_pre_mix, dgate_m, dshift_f, da2 * g_pre_mlp, dgate_f], axis=1)
    small = dict(
        dmod=dmod, g_pre_mix=da1 * (1.0 + scale_m), g_post_mix=dg_post_mix, g_pre_mlp=da2 * (1.0 + scale_f),
        g_post_mlp=dg_post_mlp, b_fgate=dbf[:, :FOX_HEADS], w_gla_a2=dw2p[FOX_HEADS:FOX_HEADS + GLA_RANK],
        b_gla_a2=db2, g_fox_out=dg_fox, g_gla_out=dg_gla)
    return loss_part, grad_x, bufs, small


def _pack(arrays):
    flat = jnp.concatenate([a.reshape(-1).astype(F32) for a in arrays])
    n = flat.shape[0]
    rows = -(-n // 128)
    rows = -(-rows // 8) * 8
    return jnp.pad(flat, (0, rows * 128 - n)).reshape(rows, 128)


def _unpack(buf, shapes):
    flat = buf.reshape(-1)
    out, off = [], 0
    for shp in shapes:
        n = 1
        for q in shp:
            n *= q
        out.append(flat[off:off + n].reshape(shp))
        off += n
    return out


SMALL_GRAD_ORDER = ["dmod", "g_pre_mix", "g_post_mix", "g_pre_mlp", "g_post_mlp", "b_fgate", "w_gla_a2", "b_gla_a2",
                    "g_fox_out", "g_gla_out"]


def kernel(x, c, w_ada, b_ada, g_pre_mix, g_post_mix, w_in, b_fgate, w_gla_a2, b_gla_a2, g_fox_out, g_gla_out, w_out, g_pre_mlp, g_post_mlp, w_mlp_in, w_mlp_out, loss_target, m_w_ada, m_b_ada, m_g_pre_mix, m_g_post_mix, m_w_in, m_b_fgate, m_w_gla_a2, m_b_gla_a2, m_g_fox_out, m_g_gla_out, m_w_out, m_g_pre_mlp, m_g_post_mlp, m_w_mlp_in, m_w_mlp_out, v_w_ada, v_b_ada, v_g_pre_mix, v_g_post_mix, v_w_in, v_b_fgate, v_w_gla_a2, v_b_gla_a2, v_g_fox_out, v_g_gla_out, v_w_out, v_g_pre_mlp, v_g_post_mlp, v_w_mlp_in, v_w_mlp_out):
    ix, iy, ic = lax.axis_index("x"), lax.axis_index("y"), lax.axis_index("c")
    chip = 2 * ix + iy
    dev = 4 * ix + 2 * iy + ic
    d = D_MODEL

    c_act = _silu_rows(c, name="silu_c")
    pack1 = _pack([c_act, w_gla_a2[0], g_gla_out[0]])
    rows1 = pack1.shape[0]
    got1 = _gather8(pack1, name="gather_small_fwd").reshape(8, rows1, 128)
    per_dev = [_unpack(got1[q], [(d,), (GLA_RANK, GLA_KW // 4), (GLA_HEADS, GLA_DV // 4)]) for q in range(8)]
    c_all = jnp.stack([p[0] for p in per_dev])
    w_gla_a2_full = jnp.concatenate([per_dev[2 * j][1] for j in range(4)], axis=1)
    g_gla_full = jnp.concatenate([per_dev[2 * j][2] for j in range(4)], axis=1)
    cols = w_ada.shape[2]
    b_ada_shard = lax.dynamic_slice_in_dim(b_ada, chip * cols, cols, axis=1)
    mod_sh = _mod_shard(c_all, w_ada[0], b_ada_shard, name="ada_mod")
    got2 = _gather8(mod_sh, name="gather_mod").reshape(8, 8, cols)
    mod_all = jnp.concatenate([got2[2 * j] for j in range(4)], axis=1)
    mod = lax.dynamic_slice_in_dim(mod_all, dev, 1, axis=0)

    own_bf = [w_in[0].astype(BF16), w_out[0].astype(BF16), w_mlp_in[0].astype(BF16), w_mlp_out[0].astype(BF16)]
    gw_in, = _finish_gather(_gather_weights(own_bf[:1], name="gather_w_in_ici"), own_bf[:1], chip,
                            name="gather_w_in_d2d")
    w_in_full = jnp.transpose(gw_in, (1, 0, 2)).reshape(d, -1)
    idx = jnp.stack([ic, chip]).astype(jnp.int32)

    loss_part, grad_x, bufs, small = _local_step(
        x[0], loss_target[0], mod, g_pre_mix, g_post_mix, g_pre_mlp, g_post_mlp, w_in_full, b_fgate,
        w_gla_a2_full, b_gla_a2, g_fox_out[0], g_gla_full, own_bf[1], own_bf[2], own_bf[3], chip, idx)
    loss = lax.psum(loss_part[0, 0], ("x", "y", "c"))

    g_big = _half_exchange(bufs, name="grad_half_exchange")
    big_w = [(w_in, m_w_in, v_w_in), (w_out, m_w_out, v_w_out), (w_mlp_in, m_w_mlp_in, v_w_mlp_in),
             (w_mlp_out, m_w_mlp_out, v_w_mlp_out)]
    big_res = []
    for q, (g, (w, m, v)) in enumerate(zip(g_big, big_w)):
        dl, mn, vn = _adam(g, w[0], m[0], v[0], name=f"adam_big_{q}")
        big_res.append((g[None], dl[None], mn[None], vn[None]))

    pack2 = _pack([small[k] for k in SMALL_GRAD_ORDER])
    rows2 = pack2.shape[0]
    got3 = _gather8(pack2, name="gather_small_grads").reshape(8, rows2, 128)
    dmod_all = got3[:, :6 * d // 128, :].reshape(8, 6 * d)
    sums = _stack_sum(got3, name="small_grad_sum")
    shapes = [(1, 6 * d), (1, d), (1, d), (1, d), (1, d), (1, FOX_HEADS), (1, GLA_RANK, GLA_KW), (1, GLA_KW),
              (1, FOX_HEADS, FOX_HD), (1, GLA_HEADS, GLA_DV)]
    sg = dict(zip(["b_ada"] + SMALL_GRAD_ORDER[1:], _unpack(sums, shapes)))
    sg["w_gla_a2"] = lax.dynamic_slice_in_dim(sg["w_gla_a2"], chip * (GLA_KW // 4), GLA_KW // 4, axis=2)
    sg["g_gla_out"] = lax.dynamic_slice_in_dim(sg["g_gla_out"], chip * (GLA_DV // 4), GLA_DV // 4, axis=2)
    small_names = ["b_ada", "g_pre_mix", "g_post_mix", "b_fgate", "w_gla_a2", "b_gla_a2", "g_fox_out", "g_gla_out",
                   "g_pre_mlp", "g_post_mlp"]
    small_w = dict(b_ada=(b_ada, m_b_ada, v_b_ada), g_pre_mix=(g_pre_mix, m_g_pre_mix, v_g_pre_mix),
                   g_post_mix=(g_post_mix, m_g_post_mix, v_g_post_mix), b_fgate=(b_fgate, m_b_fgate, v_b_fgate),
                   w_gla_a2=(w_gla_a2, m_w_gla_a2, v_w_gla_a2), b_gla_a2=(b_gla_a2, m_b_gla_a2, v_b_gla_a2),
                   g_fox_out=(g_fox_out, m_g_fox_out, v_g_fox_out), g_gla_out=(g_gla_out, m_g_gla_out, v_g_gla_out),
                   g_pre_mlp=(g_pre_mlp, m_g_pre_mlp, v_g_pre_mlp), g_post_mlp=(g_post_mlp, m_g_post_mlp, v_g_post_mlp))
    sshapes = [small_w[k][0].shape for k in small_names]
    pg = _pack([sg[k] for k in small_names])
    pw, pm, pv = [_pack([small_w[k][q] for k in small_names]) for q in range(3)]
    pd, pmn, pvn = _adam(pg, pw, pm, pv, name="adam_small")
    s_delta = dict(zip(small_names, _unpack(pd, sshapes)))
    s_m = dict(zip(small_names, _unpack(pmn, sshapes)))
    s_v = dict(zip(small_names, _unpack(pvn, sshapes)))

    dmod_cols = lax.dynamic_slice_in_dim(dmod_all, chip * cols, cols, axis=1)
    g_ada, d_ada, m_ada, v_ada = _ada_grad_adam(c_all.T, dmod_cols, w_ada[0], m_w_ada[0], v_w_ada[0], name="ada_grad_adam")

    order = ["w_ada", "b_ada", "g_pre_mix", "g_post_mix", "w_in", "b_fgate", "w_gla_a2", "b_gla_a2", "g_fox_out",
             "g_gla_out", "w_out", "g_pre_mlp", "g_post_mlp", "w_mlp_in", "w_mlp_out"]
    res = {"w_ada": (g_ada[None], d_ada[None], m_ada[None], v_ada[None]),
           "w_in": big_res[0], "w_out": big_res[1], "w_mlp_in": big_res[2], "w_mlp_out": big_res[3]}
    for k in small_names:
        res[k] = (sg[k], s_delta[k], s_m[k], s_v[k])
    return (loss, grad_x[None], *[res[k][0] for k in order], *[res[k][1] for k in order],
            *[res[k][2] for k in order], *[res[k][3] for k in order])
```

```python
import functools

import jax
import jax.numpy as jnp
from jax import lax
from jax.experimental import pallas as pl
from jax.experimental.pallas import tpu as pltpu

F32 = jnp.float32
BF16 = jnp.bfloat16
MESH = pl.DeviceIdType.MESH
HIGHEST = lax.Precision.HIGHEST

D_MODEL = 2048
FOX_HEADS = 8
FOX_HD = 128
FOX_W = FOX_HEADS * FOX_HD
GLA_HEADS = 4
GLA_DK = 128
GLA_DV = 256
GLA_KW = GLA_HEADS * GLA_DK
GLA_W = GLA_HEADS * GLA_DV
GLA_RANK = 16
GLA_TEMP = 16.0
CHUNK = 64
D_FF = 4 * D_MODEL
EPS = 1e-6
MAIN_W = 3 * FOX_W + 2 * GLA_KW + 2 * GLA_W
SMALL_W = 128
NEG = -1e30

ADAM_LR = 0.001
ADAM_B1 = 0.9
ADAM_B2 = 0.999
ADAM_EPS = 1e-08
ADAM_WD = 0.01
ADAM_STEP = 10

VMEM_LIMIT = 52 * 1024 * 1024
ROW_TILE = 256
FOX_TQ = 512
FOX_TK = 512
GLA_ROWS = 512
GATE_TS = 512
MM_T = 1024
MM_TK = 2048


def _cp(*sem):
    return pltpu.CompilerParams(dimension_semantics=sem, vmem_limit_bytes=VMEM_LIMIT)


def _dot_nn(a, b, precision=None):
    return jnp.dot(a, b, preferred_element_type=F32, precision=precision)


def _dot_nt(a, b, precision=None):
    return lax.dot_general(a, b, (((1,), (1,)), ((), ())), preferred_element_type=F32, precision=precision)


def _dot_tn(a, b, precision=None):
    return lax.dot_general(a, b, (((0,), (0,)), ((), ())), preferred_element_type=F32, precision=precision)


def _sigmoid(x):
    return 1.0 / (1.0 + jnp.exp(-x))


def _log_sigmoid(x):
    return jnp.minimum(x, 0.0) - jnp.log(1.0 + jnp.exp(-jnp.abs(x)))


class _Side:
    def __init__(self, inputs, out_shapes, plan_fn, n_copies):
        self.inputs, self.out_shapes, self.plan_fn, self.n_copies = list(inputs), list(out_shapes), plan_fn, n_copies

    def scratch(self):
        return [pltpu.SemaphoreType.DMA((self.n_copies,)), pltpu.SemaphoreType.DMA((self.n_copies,))]

    def run(self, in_refs, out_refs, sems, first, last):
        @pl.when(first)
        def _():
            _plan_start(self.plan_fn(in_refs, out_refs), *sems)

        @pl.when(last)
        def _():
            _plan_wait(self.plan_fn(in_refs, out_refs), *sems)


def _mm(a, b, *, mode, out_dtypes, name, tm=None, tn=None, tk=None, extras=(), epi=None,
        out_shapes=None, out_specs=None, side=None, b_slots=0):
    tm, tn, tk = tm or MM_T, tn or MM_T, tk or MM_TK
    b2 = (b.shape[1], b_slots * b.shape[2]) if b_slots else b.shape
    if mode == "nn":
        (m, k), n = a.shape, b2[1]
    elif mode == "nt":
        (m, k), n = a.shape, b2[0]
    else:
        (k, m), n = a.shape, b2[1]
    tm, tn, tk = min(tm, m), min(tn, n), min(tk, k)
    if b_slots:
        tn = min(tn, b.shape[2]) if mode == "nn" else tn
        tk = min(tk, b.shape[2]) if mode == "nt" else tk
    assert m % tm == 0 and n % tn == 0 and k % tk == 0, (name, m, n, k)
    nk = k // tk
    n_out, n_ex = len(out_dtypes), len(extras)
    if epi is None:
        epi = lambda acc: tuple(acc for _ in range(n_out))
    dot = {"nn": _dot_nn, "nt": _dot_nt, "tn": _dot_tn}[mode]

    n_si = len(side.inputs) if side else 0
    n_so = len(side.out_shapes) if side else 0
    grid = (m // tm, n // tn, nk)

    def body(*refs):
        a_ref, b_ref = refs[0], refs[1]
        ex_refs = refs[2:2 + n_ex]
        base = 2 + n_ex + n_si
        o_refs = refs[base:base + n_out]
        scratch = refs[base + n_out + n_so:]
        if side:
            pos = [pl.program_id(q) for q in range(3)]
            first = (pos[0] == 0) & (pos[1] == 0) & (pos[2] == 0)
            last = (pos[0] == grid[0] - 1) & (pos[1] == grid[1] - 1) & (pos[2] == grid[2] - 1)
            side.run(refs[2 + n_ex:base], refs[base + n_out:base + n_out + n_so], scratch[-2:], first, last)
        part = dot(a_ref[...], b_ref[...])

        def finish(acc):
            outs = epi(acc, *[e[...] for e in ex_refs])
            for o_ref, val in zip(o_refs, outs):
                o_ref[...] = val.reshape(o_ref.shape).astype(o_ref.dtype)

        if nk == 1:
            finish(part)
        else:
            acc_ref = scratch[0]
            kk = pl.program_id(2)

            @pl.when(kk == 0)
            def _():
                acc_ref[...] = part

            @pl.when(kk > 0)
            def _():
                acc_ref[...] += part

            @pl.when(kk == nk - 1)
            def _():
                finish(acc_ref[...])

    if mode == "nn":
        a_spec = pl.BlockSpec((tm, tk), lambda i, j, kk: (i, kk))
        b_spec = pl.BlockSpec((tk, tn), lambda i, j, kk: (kk, j))
        if b_slots:
            per = b.shape[2] // tn
            b_spec = pl.BlockSpec((None, tk, tn), lambda i, j, kk: (j // per, kk, j % per))
    elif mode == "nt":
        a_spec = pl.BlockSpec((tm, tk), lambda i, j, kk: (i, kk))
        b_spec = pl.BlockSpec((tn, tk), lambda i, j, kk: (j, kk))
        if b_slots:
            per = b.shape[2] // tk
            b_spec = pl.BlockSpec((None, tn, tk), lambda i, j, kk: (kk // per, j, kk % per))
    else:
        assert not b_slots
        a_spec = pl.BlockSpec((tk, tm), lambda i, j, kk: (kk, i))
        b_spec = pl.BlockSpec((tk, tn), lambda i, j, kk: (kk, j))
    tile_spec = pl.BlockSpec((tm, tn), lambda i, j, kk: (i, j))
    if out_shapes is None:
        out_shapes = [jax.ShapeDtypeStruct((m, n), dt) for dt in out_dtypes]
    if out_specs is None:
        out_specs = [tile_spec for _ in out_dtypes]
    any_spec = pl.BlockSpec(memory_space=pl.ANY)
    res = pl.pallas_call(
        body,
        grid=grid,
        in_specs=[a_spec, b_spec] + [tile_spec for _ in extras] + [any_spec] * n_si,
        out_specs=list(out_specs) + [any_spec] * n_so,
        out_shape=list(out_shapes) + (side.out_shapes if side else []),
        scratch_shapes=([pltpu.VMEM((tm, tn), F32)] if nk > 1 else []) + (side.scratch() if side else []),
        compiler_params=_cp("arbitrary", "arbitrary", "arbitrary") if side else _cp("parallel", "parallel", "arbitrary"),
        name=name,
    )(a, b, *extras, *(side.inputs if side else []))
    return res


def _row_spec(ts, d):
    return pl.BlockSpec((ts, d), lambda i: (i, 0))


def _vec_spec(d):
    return pl.BlockSpec((1, d), lambda i: (0, 0))


def _pre_fwd(x, avec, shift, *, name):
    s, d = x.shape
    ts = min(ROW_TILE, s)

    def body(x_ref, a_ref, s_ref, h_ref):
        xv = x_ref[...]
        r = lax.rsqrt(jnp.mean(xv * xv, axis=-1, keepdims=True) + EPS)
        h_ref[...] = (xv * r * a_ref[...] + s_ref[...]).astype(BF16)

    return pl.pallas_call(
        body, grid=(s // ts,),
        in_specs=[_row_spec(ts, d), _vec_spec(d), _vec_spec(d)],
        out_specs=_row_spec(ts, d),
        out_shape=jax.ShapeDtypeStruct((s, d), BF16),
        compiler_params=_cp("parallel"), name=name,
    )(x, avec, shift)


def _post_fwd(x, y, gate, g, *, name):
    s, d = x.shape
    ts = min(ROW_TILE, s)

    def body(x_ref, y_ref, gate_ref, g_ref, o_ref):
        yv = y_ref[...]
        r = lax.rsqrt(jnp.mean(yv * yv, axis=-1, keepdims=True) + EPS)
        o_ref[...] = x_ref[...] + gate_ref[...] * (yv * r * g_ref[...])

    return pl.pallas_call(
        body, grid=(s // ts,),
        in_specs=[_row_spec(ts, d), _row_spec(ts, d), _vec_spec(d), _vec_spec(d)],
        out_specs=_row_spec(ts, d),
        out_shape=jax.ShapeDtypeStruct((s, d), F32),
        compiler_params=_cp("parallel"), name=name,
    )(x, y, gate, g)


def _post_fwd_loss(x, y, gate, g, target, *, name):
    s, d = x.shape
    ts = min(ROW_TILE, s)

    def body(x_ref, y_ref, gate_ref, g_ref, t_ref, dx_ref, loss_ref):
        yv = y_ref[...]
        r = lax.rsqrt(jnp.mean(yv * yv, axis=-1, keepdims=True) + EPS)
        diff = x_ref[...] + gate_ref[...] * (yv * r * g_ref[...]) - t_ref[...]
        dx_ref[...] = diff * (1.0 / d)

        @pl.when(pl.program_id(0) == 0)
        def _():
            loss_ref[...] = jnp.zeros_like(loss_ref)

        loss_ref[...] += jnp.sum(jnp.mean(diff * diff, axis=-1, keepdims=True)) * 0.5

    return pl.pallas_call(
        body, grid=(s // ts,),
        in_specs=[_row_spec(ts, d), _row_spec(ts, d), _vec_spec(d), _vec_spec(d), _row_spec(ts, d)],
        out_specs=[_row_spec(ts, d), pl.BlockSpec((1, 128), lambda i: (0, 0))],
        out_shape=[jax.ShapeDtypeStruct((s, d), F32), jax.ShapeDtypeStruct((1, 128), F32)],
        compiler_params=_cp("arbitrary"), name=name,
    )(x, y, gate, g, target)


def _post_bwd(dxo, y, gate, g, *, name):
    s, d = y.shape
    ts = min(ROW_TILE, s)

    def body(dx_ref, y_ref, gate_ref, g_ref, dy_ref, dgate_ref, dg_ref):
        yv, dxv, gv = y_ref[...], dx_ref[...], g_ref[...]
        r = lax.rsqrt(jnp.mean(yv * yv, axis=-1, keepdims=True) + EPS)
        yhat = yv * r
        dn = dxv * gate_ref[...]
        dyhat = dn * gv
        dy = r * (dyhat - yhat * jnp.mean(dyhat * yhat, axis=-1, keepdims=True))
        dy_ref[...] = dy.astype(BF16)

        @pl.when(pl.program_id(0) == 0)
        def _():
            dgate_ref[...] = jnp.zeros_like(dgate_ref)
            dg_ref[...] = jnp.zeros_like(dg_ref)

        dgate_ref[...] += jnp.sum(dxv * (yhat * gv), axis=0, keepdims=True)
        dg_ref[...] += jnp.sum(dn * yhat, axis=0, keepdims=True)

    return pl.pallas_call(
        body, grid=(s // ts,),
        in_specs=[_row_spec(ts, d), _row_spec(ts, d), _vec_spec(d), _vec_spec(d)],
        out_specs=[_row_spec(ts, d), _vec_spec(d), _vec_spec(d)],
        out_shape=[jax.ShapeDtypeStruct((s, d), BF16), jax.ShapeDtypeStruct((1, d), F32),
                   jax.ShapeDtypeStruct((1, d), F32)],
        compiler_params=_cp("arbitrary"), name=name,
    )(dxo, y, gate, g)


def _pre_bwd(dh, xin, dres, avec, *, name):
    s, d = xin.shape
    ts = min(ROW_TILE, s)

    def body(dh_ref, x_ref, dres_ref, a_ref, dx_ref, dshift_ref, da_ref):
        xv, dhv = x_ref[...], dh_ref[...]
        r = lax.rsqrt(jnp.mean(xv * xv, axis=-1, keepdims=True) + EPS)
        xhat = xv * r
        dxhat = dhv * a_ref[...]
        dx_ref[...] = dres_ref[...] + r * (dxhat - xhat * jnp.mean(dxhat * xhat, axis=-1, keepdims=True))

        @pl.when(pl.program_id(0) == 0)
        def _():
            dshift_ref[...] = jnp.zeros_like(dshift_ref)
            da_ref[...] = jnp.zeros_like(da_ref)

        dshift_ref[...] += jnp.sum(dhv, axis=0, keepdims=True)
        da_ref[...] += jnp.sum(dhv * xhat, axis=0, keepdims=True)

    return pl.pallas_call(
        body, grid=(s // ts,),
        in_specs=[_row_spec(ts, d), _row_spec(ts, d), _row_spec(ts, d), _vec_spec(d)],
        out_specs=[_row_spec(ts, d), _vec_spec(d), _vec_spec(d)],
        out_shape=[jax.ShapeDtypeStruct((s, d), F32), jax.ShapeDtypeStruct((1, d), F32),
                   jax.ShapeDtypeStruct((1, d), F32)],
        compiler_params=_cp("arbitrary"), name=name,
    )(dh, xin, dres, avec)


def _tri(n, strict=False, upper=False):
    r = lax.broadcasted_iota(jnp.int32, (n, n), 0)
    c = lax.broadcasted_iota(jnp.int32, (n, n), 1)
    if upper:
        r, c = c, r
    return ((r > c) if strict else (r >= c)).astype(F32)


def _gates_fwd(ps, bf, w2p, b2, *, name):
    s = ps.shape[0]
    ts = min(GATE_TS, s)

    def body(ps_ref, bf_ref, w_ref, b2_ref, cum_ref, la_ref, carry_ref):
        @pl.when(pl.program_id(0) == 0)
        def _():
            carry_ref[...] = jnp.zeros_like(carry_ref)

        psv = ps_ref[...]
        lf = _log_sigmoid(psv + bf_ref[...])
        cum = _dot_nn(_tri(ts), lf, HIGHEST) + carry_ref[...]
        cum_ref[...] = cum
        carry_ref[...] = cum[ts - 1:ts, :]
        z = _dot_nn(psv, w_ref[...], HIGHEST) + b2_ref[...]
        la_ref[...] = _log_sigmoid(z) * (1.0 / GLA_TEMP)

    return pl.pallas_call(
        body, grid=(s // ts,),
        in_specs=[_row_spec(ts, SMALL_W), _vec_spec(SMALL_W),
                  pl.BlockSpec((SMALL_W, GLA_KW), lambda i: (0, 0)), _vec_spec(GLA_KW)],
        out_specs=[_row_spec(ts, SMALL_W), _row_spec(ts, GLA_KW)],
        out_shape=[jax.ShapeDtypeStruct((s, SMALL_W), F32), jax.ShapeDtypeStruct((s, GLA_KW), F32)],
        scratch_shapes=[pltpu.VMEM((1, SMALL_W), F32)],
        compiler_params=_cp("arbitrary"), name=name,
    )(ps, bf, w2p, b2)


def _gates_bwd(dck, ps, bf, w2p, b2, dla, *, name):
    s = ps.shape[0]
    ts = min(GATE_TS, s)
    nb = s // ts
    rev = lambda i: (nb - 1 - i, 0)

    def body(dck_ref, ps_ref, bf_ref, w_ref, b2_ref, dla_ref, dps_ref, dbf_ref, dw_ref, db2_ref, carry_ref):
        @pl.when(pl.program_id(0) == 0)
        def _():
            carry_ref[...] = jnp.zeros_like(carry_ref)
            dbf_ref[...] = jnp.zeros_like(dbf_ref)
            dw_ref[...] = jnp.zeros_like(dw_ref)
            db2_ref[...] = jnp.zeros_like(db2_ref)

        psv, dckv = ps_ref[...], dck_ref[...]
        dlf = _dot_nn(_tri(ts, upper=True), dckv, HIGHEST) + carry_ref[...]
        carry_ref[...] += jnp.sum(dckv, axis=0, keepdims=True)
        lane = lax.broadcasted_iota(jnp.int32, (ts, SMALL_W), 1)
        dff = jnp.where(lane < FOX_HEADS, dlf * _sigmoid(-(psv + bf_ref[...])), 0.0)
        z = _dot_nn(psv, w_ref[...], HIGHEST) + b2_ref[...]
        dz = dla_ref[...] * _sigmoid(-z) * (1.0 / GLA_TEMP)
        dps_ref[...] = (_dot_nt(dz, w_ref[...], HIGHEST) + dff).astype(BF16)
        dbf_ref[...] += jnp.sum(dff, axis=0, keepdims=True)
        dw_ref[...] += _dot_tn(psv, dz, HIGHEST)
        db2_ref[...] += jnp.sum(dz, axis=0, keepdims=True)

    return pl.pallas_call(
        body, grid=(nb,),
        in_specs=[pl.BlockSpec((ts, SMALL_W), rev), pl.BlockSpec((ts, SMALL_W), rev), _vec_spec(SMALL_W),
                  pl.BlockSpec((SMALL_W, GLA_KW), lambda i: (0, 0)), _vec_spec(GLA_KW),
                  pl.BlockSpec((ts, GLA_KW), rev)],
        out_specs=[pl.BlockSpec((ts, SMALL_W), rev), _vec_spec(SMALL_W),
                   pl.BlockSpec((SMALL_W, GLA_KW), lambda i: (0, 0)), _vec_spec(GLA_KW)],
        out_shape=[jax.ShapeDtypeStruct((s, SMALL_W), BF16), jax.ShapeDtypeStruct((1, SMALL_W), F32),
                   jax.ShapeDtypeStruct((SMALL_W, GLA_KW), F32), jax.ShapeDtypeStruct((1, GLA_KW), F32)],
        scratch_shapes=[pltpu.VMEM((1, SMALL_W), F32)],
        compiler_params=_cp("arbitrary"), name=name,
    )(dck, ps, bf, w2p, b2, dla)


def _causal_mask(i, j, tq, tk):
    rows = i * tq + lax.broadcasted_iota(jnp.int32, (tq, tk), 0)
    cols = j * tk + lax.broadcasted_iota(jnp.int32, (tq, tk), 1)
    return rows >= cols


def _hs(h, hd=FOX_HD):
    return slice(h * hd, (h + 1) * hd)


def _fox_fwd(proj, cum_t, g_fox, *, name, side=None):
    s = proj.shape[0]
    tq, tk = min(FOX_TQ, s), min(FOX_TK, s)
    scale = FOX_HD ** -0.5
    n_si = len(side.inputs) if side else 0
    n_so = len(side.out_shapes) if side else 0
    grid = (s // tq, s // tk)

    def body(*refs):
        q_ref, k_ref, v_ref, ck_ref, g_ref = refs[:5]
        o_ref, n_ref, lse_ref = refs[5 + n_si:8 + n_si]
        m_sc, l_sc, acc_sc = refs[8 + n_si + n_so:11 + n_si + n_so]
        i, j = pl.program_id(0), pl.program_id(1)
        if side:
            side.run(refs[5:5 + n_si], refs[8 + n_si:8 + n_si + n_so], refs[11 + n_si + n_so:],
                     (i == 0) & (j == 0), (i == grid[0] - 1) & (j == grid[1] - 1))

        @pl.when(j == 0)
        def _():
            m_sc[...] = jnp.full_like(m_sc, NEG)
            l_sc[...] = jnp.zeros_like(l_sc)
            acc_sc[...] = jnp.zeros_like(acc_sc)

        def block(masked):
            mask = _causal_mask(i, j, tq, tk) if masked else None
            for h in range(FOX_HEADS):
                sc = _fox_logits(q_ref[:, _hs(h)], k_ref[:, _hs(h)], ck_ref[h:h + 1, :], mask, scale)
                m_prev = m_sc[h]
                m_new = jnp.maximum(m_prev, jnp.max(sc, axis=-1, keepdims=True))
                alpha = jnp.exp(m_prev - m_new)
                p = jnp.exp(sc - m_new)
                l_sc[h] = alpha * l_sc[h] + jnp.sum(p, axis=-1, keepdims=True)
                acc_sc[:, _hs(h)] = alpha * acc_sc[:, _hs(h)] + _dot_nn(p.astype(BF16), v_ref[:, _hs(h)])
                m_sc[h] = m_new

        pl.when(j < i)(functools.partial(block, False))

        @pl.when(j == i)
        def _():
            block(True)
            lane = lax.broadcasted_iota(jnp.int32, (tq, 128), 1)
            lse = jnp.zeros((tq, 128), F32)
            for h in range(FOX_HEADS):
                o = acc_sc[:, _hs(h)] / l_sc[h]
                o_ref[:, _hs(h)] = o
                r = lax.rsqrt(jnp.mean(o * o, axis=-1, keepdims=True) + EPS)
                n_ref[:, _hs(h)] = (o * r * g_ref[h:h + 1, :]).astype(BF16)
                lse = jnp.where(lane == h, m_sc[h] + jnp.log(l_sc[h]), lse)
            lse_ref[...] = lse

    kv = lambda col: (lambda i, j: (jnp.minimum(j, i), col))
    any_spec = pl.BlockSpec(memory_space=pl.ANY)
    return pl.pallas_call(
        body, grid=grid,
        in_specs=[pl.BlockSpec((tq, FOX_W), lambda i, j: (i, 0)),
                  pl.BlockSpec((tk, FOX_W), kv(1)),
                  pl.BlockSpec((tk, FOX_W), kv(2)),
                  pl.BlockSpec((FOX_HEADS, tk), lambda i, j: (0, jnp.minimum(j, i))),
                  pl.BlockSpec((FOX_HEADS, FOX_HD), lambda i, j: (0, 0))] + [any_spec] * n_si,
        out_specs=[pl.BlockSpec((tq, FOX_W), lambda i, j: (i, 0)),
                   pl.BlockSpec((tq, FOX_W), lambda i, j: (i, 0)),
                   pl.BlockSpec((tq, 128), lambda i, j: (i, 0))] + [any_spec] * n_so,
        out_shape=[jax.ShapeDtypeStruct((s, FOX_W), F32), jax.ShapeDtypeStruct((s, FOX_W), BF16),
                   jax.ShapeDtypeStruct((s, 128), F32)] + (side.out_shapes if side else []),
        scratch_shapes=[pltpu.VMEM((FOX_HEADS, tq, 1), F32), pltpu.VMEM((FOX_HEADS, tq, 1), F32),
                        pltpu.VMEM((tq, FOX_W), F32)] + (side.scratch() if side else []),
        compiler_params=_cp("arbitrary", "arbitrary"), name=name,
    )(proj, proj, proj, cum_t, g_fox, *(side.inputs if side else []))


def _fox_logits(q, k, ck, mask, scale):
    sc = _dot_nt(q, k) * scale - ck
    return sc if mask is None else jnp.where(mask, sc, NEG)


def _fox_p_ds(q, k, v, do, ck, lse, delta, mask, scale):
    p = jnp.exp(_fox_logits(q, k, ck, mask, scale) - lse)
    dp = _dot_nt(do, v)
    return p, p * (dp - delta)


def _fox_bwd_dq(proj, do, cum_t, lse, delta, *, name):
    s = proj.shape[0]
    tq, tk = min(FOX_TQ, s), min(FOX_TK, s)
    scale = FOX_HD ** -0.5

    def body(q_ref, k_ref, v_ref, do_ref, ck_ref, lse_ref, dl_ref, dq_ref, dcq_ref, acc_sc, row_sc):
        i, j = pl.program_id(0), pl.program_id(1)

        @pl.when(j == 0)
        def _():
            acc_sc[...] = jnp.zeros_like(acc_sc)
            row_sc[...] = jnp.zeros_like(row_sc)

        def block(masked):
            mask = _causal_mask(i, j, tq, tk) if masked else None
            for h in range(FOX_HEADS):
                _, ds = _fox_p_ds(q_ref[:, _hs(h)], k_ref[:, _hs(h)], v_ref[:, _hs(h)], do_ref[:, _hs(h)],
                                  ck_ref[h:h + 1, :], lse_ref[:, h:h + 1], dl_ref[:, h:h + 1], mask, scale)
                acc_sc[:, _hs(h)] += _dot_nn(ds.astype(BF16), k_ref[:, _hs(h)])
                row_sc[h] += jnp.sum(ds, axis=-1, keepdims=True)

        pl.when(j < i)(functools.partial(block, False))

        @pl.when(j == i)
        def _():
            block(True)
            dq_ref[...] = (acc_sc[...] * scale).astype(BF16)
            lane = lax.broadcasted_iota(jnp.int32, (tq, 128), 1)
            dcq = jnp.zeros((tq, 128), F32)
            for h in range(FOX_HEADS):
                dcq = jnp.where(lane == h, row_sc[h], dcq)
            dcq_ref[...] = dcq

    kv = lambda col: (lambda i, j: (jnp.minimum(j, i), col))
    qrow = lambda i, j: (i, 0)
    return pl.pallas_call(
        body, grid=(s // tq, s // tk),
        in_specs=[pl.BlockSpec((tq, FOX_W), qrow), pl.BlockSpec((tk, FOX_W), kv(1)), pl.BlockSpec((tk, FOX_W), kv(2)),
                  pl.BlockSpec((tq, FOX_W), qrow),
                  pl.BlockSpec((FOX_HEADS, tk), lambda i, j: (0, jnp.minimum(j, i))),
                  pl.BlockSpec((tq, 128), qrow), pl.BlockSpec((tq, 128), qrow)],
        out_specs=[pl.BlockSpec((tq, FOX_W), qrow), pl.BlockSpec((tq, 128), qrow)],
        out_shape=[jax.ShapeDtypeStruct((s, FOX_W), BF16), jax.ShapeDtypeStruct((s, 128), F32)],
        scratch_shapes=[pltpu.VMEM((tq, FOX_W), F32), pltpu.VMEM((FOX_HEADS, tq, 1), F32)],
        compiler_params=_cp("parallel", "arbitrary"), name=name,
    )(proj, proj, proj, do, cum_t, lse, delta)


def _fox_bwd_dkv(proj, do, cum_t, lse, delta, *, name):
    s = proj.shape[0]
    tq, tk = min(FOX_TQ, s), min(FOX_TK, s)
    nq = s // tq
    scale = FOX_HD ** -0.5

    def body(q_ref, k_ref, v_ref, do_ref, ck_ref, lse_ref, dl_ref, dk_ref, dv_ref, dck_ref,
             dk_sc, dv_sc, dck_sc):
        j, i = pl.program_id(0), pl.program_id(1)

        @pl.when(i == 0)
        def _():
            dk_sc[...] = jnp.zeros_like(dk_sc)
            dv_sc[...] = jnp.zeros_like(dv_sc)
            dck_sc[...] = jnp.zeros_like(dck_sc)

        def block(masked):
            mask = _causal_mask(i, j, tq, tk) if masked else None
            for h in range(FOX_HEADS):
                p, ds = _fox_p_ds(q_ref[:, _hs(h)], k_ref[:, _hs(h)], v_ref[:, _hs(h)], do_ref[:, _hs(h)],
                                  ck_ref[h:h + 1, :], lse_ref[:, h:h + 1], dl_ref[:, h:h + 1], mask, scale)
                dv_sc[:, _hs(h)] += _dot_tn(p.astype(BF16), do_ref[:, _hs(h)])
                dk_sc[:, _hs(h)] += _dot_tn(ds.astype(BF16), q_ref[:, _hs(h)])
                dck_sc[h:h + 1, :] -= jnp.sum(ds, axis=0, keepdims=True)

        pl.when(i > j)(functools.partial(block, False))
        pl.when(i == j)(functools.partial(block, True))

        @pl.when(i == nq - 1)
        def _():
            dk_ref[...] = (dk_sc[...] * scale).astype(BF16)
            dv_ref[...] = dv_sc[...].astype(BF16)
            dck_ref[...] = dck_sc[...]

    qrow = lambda j, i: (jnp.maximum(i, j), 0)
    krow = lambda col: (lambda j, i: (j, col))
    return pl.pallas_call(
        body, grid=(s // tk, nq),
        in_specs=[pl.BlockSpec((tq, FOX_W), qrow), pl.BlockSpec((tk, FOX_W), krow(1)),
                  pl.BlockSpec((tk, FOX_W), krow(2)),
                  pl.BlockSpec((tq, FOX_W), qrow),
                  pl.BlockSpec((FOX_HEADS, tk), lambda j, i: (0, j)),
                  pl.BlockSpec((tq, 128), qrow), pl.BlockSpec((tq, 128), qrow)],
        out_specs=[pl.BlockSpec((tk, FOX_W), lambda j, i: (j, 0)), pl.BlockSpec((tk, FOX_W), lambda j, i: (j, 0)),
                   pl.BlockSpec((FOX_HEADS, tk), lambda j, i: (0, j))],
        out_shape=[jax.ShapeDtypeStruct((s, FOX_W), BF16), jax.ShapeDtypeStruct((s, FOX_W), BF16),
                   jax.ShapeDtypeStruct((FOX_HEADS, s), F32)],
        scratch_shapes=[pltpu.VMEM((tk, FOX_W), F32), pltpu.VMEM((tk, FOX_W), F32),
                        pltpu.VMEM((FOX_HEADS, tk), F32)],
        compiler_params=_cp("parallel", "arbitrary"), name=name,
    )(proj, proj, proj, do, cum_t, lse, delta)


def _head_norm_bwd(dn_in, o, g, gr_src, *, nh, hd, dn_col, gr_col, name):
    s, w = o.shape
    ts = min(ROW_TILE, s)
    gated = gr_src is not None

    def body(*refs):
        if gated:
            dn_ref, o_ref, g_ref, gr_ref, do_ref, dgr_ref, dl_ref, dg_ref = refs
        else:
            dn_ref, o_ref, g_ref, do_ref, dl_ref, dg_ref = refs

        @pl.when(pl.program_id(0) == 0)
        def _():
            dg_ref[...] = jnp.zeros_like(dg_ref)

        lane = lax.broadcasted_iota(jnp.int32, (ts, 128), 1)
        delta = jnp.zeros((ts, 128), F32)
        for h in range(nh):
            sl = _hs(h, hd)
            ov = o_ref[:, sl]
            dnv = dn_ref[:, sl].astype(F32)
            gv = g_ref[h:h + 1, :]
            r = lax.rsqrt(jnp.mean(ov * ov, axis=-1, keepdims=True) + EPS)
            ohat = ov * r
            if gated:
                grv = gr_ref[:, sl].astype(F32)
                sig = _sigmoid(grv)
                dgr_ref[:, sl] = (dnv * (ohat * gv) * (sig * (1.0 + grv * (1.0 - sig)))).astype(BF16)
                dnv = dnv * (grv * sig)
            dg_ref[h:h + 1, :] += jnp.sum(dnv * ohat, axis=0, keepdims=True)
            dohat = dnv * gv
            do = r * (dohat - ohat * jnp.mean(dohat * ohat, axis=-1, keepdims=True))
            do_ref[:, sl] = do.astype(BF16)
            delta = jnp.where(lane == h, jnp.sum(do * ov, axis=-1, keepdims=True), delta)
        dl_ref[...] = delta

    in_specs = [pl.BlockSpec((ts, w), lambda i: (i, dn_col)), _row_spec(ts, w),
                pl.BlockSpec((nh, hd), lambda i: (0, 0))]
    args = [dn_in, o, g]
    out_specs = [_row_spec(ts, w)]
    out_shape = [jax.ShapeDtypeStruct((s, w), BF16)]
    if gated:
        in_specs.append(pl.BlockSpec((ts, w), lambda i: (i, gr_col)))
        args.append(gr_src)
        out_specs.append(_row_spec(ts, w))
        out_shape.append(jax.ShapeDtypeStruct((s, w), BF16))
    out_specs += [_row_spec(ts, 128), pl.BlockSpec((nh, hd), lambda i: (0, 0))]
    out_shape += [jax.ShapeDtypeStruct((s, 128), F32), jax.ShapeDtypeStruct((nh, hd), F32)]
    return pl.pallas_call(
        body, grid=(s // ts,), in_specs=in_specs, out_specs=out_specs, out_shape=out_shape,
        compiler_params=_cp("arbitrary"), name=name,
    )(*args)


GQ_BLK = 3 * FOX_W // GLA_DK
GK_BLK = GQ_BLK + GLA_HEADS
GV_BLK = (3 * FOX_W + 2 * GLA_KW) // GLA_DV
GR_BLK = GV_BLK + GLA_HEADS


def _gla_chunk_terms(la):
    cum = _dot_nn(_tri(CHUNK), la, HIGHEST)
    total = cum[CHUNK - 1:CHUNK, :]
    return jnp.exp(total - cum), jnp.exp(total)


def _gla_fwd(proj, log_a, g_gla, *, name):
    s = proj.shape[0]
    rows = min(GLA_ROWS, s)
    cb = rows // CHUNK
    nblk = s // rows
    scale = GLA_DK ** -0.5

    def body(q_ref, k_ref, v_ref, gr_ref, la_ref, g_ref, o_ref, n_ref, st_ref, st_sc):
        h = pl.program_id(0)

        @pl.when(pl.program_id(1) == 0)
        def _():
            st_sc[...] = jnp.zeros_like(st_sc)

        gv = g_ref[pl.ds(h, 1), :]
        for ci in range(cb):
            sl = slice(ci * CHUNK, (ci + 1) * CHUNK)
            e, dec = _gla_chunk_terms(la_ref[sl, :])
            k_dec = (k_ref[sl, :].astype(F32) * e).astype(BF16)
            st = st_sc[...] * dec + _dot_tn(v_ref[sl, :], k_dec)
            st_sc[...] = st
            st_ref[0, ci] = st
            qs = (q_ref[sl, :].astype(F32) * scale).astype(BF16)
            o = _dot_nt(qs, st.astype(BF16))
            o_ref[sl, :] = o
            r = lax.rsqrt(jnp.mean(o * o, axis=-1, keepdims=True) + EPS)
            grv = gr_ref[sl, :].astype(F32)
            n_ref[sl, :] = (o * r * gv * (grv * _sigmoid(grv))).astype(BF16)

    return pl.pallas_call(
        body, grid=(GLA_HEADS, nblk),
        in_specs=[pl.BlockSpec((rows, GLA_DK), lambda h, n: (n, GQ_BLK + h)),
                  pl.BlockSpec((rows, GLA_DK), lambda h, n: (n, GK_BLK + h)),
                  pl.BlockSpec((rows, GLA_DV), lambda h, n: (n, GV_BLK + h)),
                  pl.BlockSpec((rows, GLA_DV), lambda h, n: (n, GR_BLK + h)),
                  pl.BlockSpec((rows, GLA_DK), lambda h, n: (n, h)),
                  pl.BlockSpec((GLA_HEADS, GLA_DV), lambda h, n: (0, 0))],
        out_specs=[pl.BlockSpec((rows, GLA_DV), lambda h, n: (n, h)),
                   pl.BlockSpec((rows, GLA_DV), lambda h, n: (n, h)),
                   pl.BlockSpec((1, cb, GLA_DV, GLA_DK), lambda h, n: (h, n, 0, 0))],
        out_shape=[jax.ShapeDtypeStruct((s, GLA_W), F32), jax.ShapeDtypeStruct((s, GLA_W), BF16),
                   jax.ShapeDtypeStruct((GLA_HEADS, s // CHUNK, GLA_DV, GLA_DK), F32)],
        scratch_shapes=[pltpu.VMEM((GLA_DV, GLA_DK), F32)],
        compiler_params=_cp("parallel", "arbitrary"), name=name,
    )(proj, proj, proj, proj, log_a, g_gla)


def _gla_bwd(proj, log_a, do, states, *, name):
    s = proj.shape[0]
    rows = min(GLA_ROWS, s)
    cb = rows // CHUNK
    nblk = s // rows
    scale = GLA_DK ** -0.5

    def body(q_ref, k_ref, v_ref, la_ref, do_ref, st_ref, prev_ref, dq_ref, dk_ref, dv_ref, dla_ref, g_sc):
        nrev = pl.program_id(1)
        blk = nblk - 1 - nrev

        @pl.when(nrev == 0)
        def _():
            g_sc[...] = jnp.zeros_like(g_sc)

        for ci in reversed(range(cb)):
            sl = slice(ci * CHUNK, (ci + 1) * CHUNK)
            e, dec = _gla_chunk_terms(la_ref[sl, :])
            kd = k_ref[sl, :].astype(F32) * e
            qs = (q_ref[sl, :].astype(F32) * scale).astype(BF16)
            dov = do_ref[sl, :]
            st = st_ref[0, ci]
            if ci > 0:
                st_prev = st_ref[0, ci - 1]
            else:
                st_prev = prev_ref[0, 0] * (blk > 0).astype(F32)
            dq_ref[sl, :] = (_dot_nn(dov, st.astype(BF16)) * scale).astype(BF16)
            gt = g_sc[...] + _dot_tn(dov, qs)
            gtb = gt.astype(BF16)
            dkd = _dot_nn(v_ref[sl, :], gtb)
            dv_ref[sl, :] = _dot_nt(kd.astype(BF16), gtb).astype(BF16)
            dk_ref[sl, :] = (dkd * e).astype(BF16)
            ddec = jnp.sum(gt * st_prev, axis=0, keepdims=True) * dec
            dla_ref[sl, :] = _dot_nn(_tri(CHUNK, strict=True), dkd * kd, HIGHEST) + ddec
            g_sc[...] = gt * dec

    rev = lambda col0: (lambda h, n: (nblk - 1 - n, col0 + h))
    return pl.pallas_call(
        body, grid=(GLA_HEADS, nblk),
        in_specs=[pl.BlockSpec((rows, GLA_DK), rev(GQ_BLK)),
                  pl.BlockSpec((rows, GLA_DK), rev(GK_BLK)),
                  pl.BlockSpec((rows, GLA_DV), rev(GV_BLK)),
                  pl.BlockSpec((rows, GLA_DK), rev(0)),
                  pl.BlockSpec((rows, GLA_DV), rev(0)),
                  pl.BlockSpec((1, cb, GLA_DV, GLA_DK), lambda h, n: (h, nblk - 1 - n, 0, 0)),
                  pl.BlockSpec((1, 1, GLA_DV, GLA_DK),
                               lambda h, n: (h, jnp.maximum((nblk - 1 - n) * cb - 1, 0), 0, 0))],
        out_specs=[pl.BlockSpec((rows, GLA_DK), rev(0)), pl.BlockSpec((rows, GLA_DK), rev(0)),
                   pl.BlockSpec((rows, GLA_DV), rev(0)), pl.BlockSpec((rows, GLA_DK), rev(0))],
        out_shape=[jax.ShapeDtypeStruct((s, GLA_KW), BF16), jax.ShapeDtypeStruct((s, GLA_KW), BF16),
                   jax.ShapeDtypeStruct((s, GLA_W), BF16), jax.ShapeDtypeStruct((s, GLA_KW), F32)],
        scratch_shapes=[pltpu.VMEM((GLA_DV, GLA_DK), F32)],
        compiler_params=_cp("parallel", "arbitrary"), name=name,
    )(proj, proj, proj, log_a, do, states, states)


def _row_tile(r):
    tr = min(ROW_TILE, r)
    while r % tr or tr % 8:
        tr -= 1
    return tr


def _adamw_math(w, g, m, v):
    m = ADAM_B1 * m + (1.0 - ADAM_B1) * g
    v = ADAM_B2 * v + (1.0 - ADAM_B2) * (g * g)
    m_hat = m / (1.0 - ADAM_B1 ** ADAM_STEP)
    v_hat = v / (1.0 - ADAM_B2 ** ADAM_STEP)
    delta = -ADAM_LR * (m_hat / (jnp.sqrt(v_hat) + ADAM_EPS) + ADAM_WD * w)
    return delta, m, v


COL_TILE = 256


def _tile_2d(r, c):
    if r % 8 == 0:
        return _row_tile(r), c
    assert c % COL_TILE == 0, (r, c)
    return r, COL_TILE


def _half_shape(shape):
    r, c = shape[-2:]
    return tuple(shape[:-2]) + ((r // 2, c) if _half_axis(r) == 0 else (r, c // 2))


def _adam(g, w, m, v, *, name):
    r, c = w.shape
    tr, tc = _tile_2d(r, c)

    def body(g_ref, w_ref, m_ref, v_ref, d_ref, mo_ref, vo_ref):
        d, mn, vn = _adamw_math(w_ref[...], g_ref[...], m_ref[...], v_ref[...])
        d_ref[...] = d
        mo_ref[...] = mn
        vo_ref[...] = vn

    spec = pl.BlockSpec((tr, tc), lambda i, j: (i, j))
    return pl.pallas_call(
        body, grid=(r // tr, c // tc), in_specs=[spec] * 4, out_specs=[spec] * 3,
        out_shape=[jax.ShapeDtypeStruct((r, c), F32)] * 3,
        compiler_params=_cp("parallel", "parallel"), name=name,
    )(g, w, m, v)


def _ada_grad_adam(c_all_t, dmod_cols, w, m, v, *, name):
    r, c = w.shape
    tr, tc = min(512, r), min(1024, c)

    def body(ct_ref, dm_ref, w_ref, m_ref, v_ref, g_ref, d_ref, mo_ref, vo_ref):
        g = _dot_nn(ct_ref[...], dm_ref[...], HIGHEST)
        g_ref[...] = g
        d, mn, vn = _adamw_math(w_ref[...], g, m_ref[...], v_ref[...])
        d_ref[...] = d
        mo_ref[...] = mn
        vo_ref[...] = vn

    spec = pl.BlockSpec((tr, tc), lambda i, j: (i, j))
    nb = c_all_t.shape[1]
    return pl.pallas_call(
        body, grid=(r // tr, c // tc),
        in_specs=[pl.BlockSpec((tr, nb), lambda i, j: (i, 0)), pl.BlockSpec((nb, tc), lambda i, j: (0, j)),
                  spec, spec, spec],
        out_specs=[spec] * 4, out_shape=[jax.ShapeDtypeStruct((r, c), F32)] * 4,
        compiler_params=_cp("parallel", "parallel"), name=name,
    )(c_all_t, dmod_cols, w, m, v)


def _mod_shard(c_all, w, b, *, name):
    k, c = w.shape
    tc = min(512, c)
    nb = c_all.shape[0]

    def body(c_ref, w_ref, b_ref, o_ref):
        o_ref[...] = _dot_nn(c_ref[...], w_ref[...], HIGHEST) + b_ref[...]

    return pl.pallas_call(
        body, grid=(c // tc,),
        in_specs=[pl.BlockSpec((nb, k), lambda j: (0, 0)), pl.BlockSpec((k, tc), lambda j: (0, j)),
                  pl.BlockSpec((1, tc), lambda j: (0, j))],
        out_specs=pl.BlockSpec((nb, tc), lambda j: (0, j)),
        out_shape=jax.ShapeDtypeStruct((nb, c), F32),
        compiler_params=_cp("parallel"), name=name,
    )(c_all, w, b)


def _silu_rows(c, *, name):
    def body(c_ref, o_ref):
        cv = c_ref[...]
        o_ref[...] = cv * _sigmoid(cv)

    return pl.pallas_call(body, out_shape=jax.ShapeDtypeStruct(c.shape, F32), name=name)(c)


def _pair_sum(g, got, idx, *, name):
    p, r, c = g.shape
    ax = _half_axis(r)
    hr, hc = _half_shape((r, c))
    tr, tc = _tile_2d(hr, hc)
    nbr, nbc = hr // tr, hc // tc

    def body(idx_ref, a_ref, b_ref, o_ref):
        o_ref[...] = (a_ref[...].astype(F32) + b_ref[...].astype(F32)).astype(BF16)

    def own_map(i, j, k, idx_ref):
        return (i, j + (idx_ref[0] * nbr if ax == 0 else 0), k + (idx_ref[0] * nbc if ax == 1 else 0))

    half_spec = pl.BlockSpec((1, tr, tc), lambda i, j, k, idx_ref: (i, j, k))
    return pl.pallas_call(
        body,
        grid_spec=pltpu.PrefetchScalarGridSpec(
            num_scalar_prefetch=1, grid=(p, nbr, nbc),
            in_specs=[pl.BlockSpec((1, tr, tc), own_map), half_spec],
            out_specs=half_spec),
        out_shape=jax.ShapeDtypeStruct((p, hr, hc), BF16),
        compiler_params=_cp("parallel", "parallel", "parallel"), name=name,
    )(idx, g, got)


def _final_sum(own, parts, idx, shard_shape, *, name):
    ax = _half_axis(shard_shape[0])
    hr, hc = own.shape[1:]
    tr, tc = _tile_2d(hr, hc)
    nbr, nbc = hr // tr, hc // tc

    def body(idx_ref, own_ref, parts_ref, o_ref):
        acc = own_ref[0].astype(F32)
        for q in range(3):
            acc = acc + parts_ref[q].astype(F32)
        o_ref[...] = acc

    def out_map(j, k, idx_ref):
        return (j + (idx_ref[0] * nbr if ax == 0 else 0), k + (idx_ref[0] * nbc if ax == 1 else 0))

    return pl.pallas_call(
        body,
        grid_spec=pltpu.PrefetchScalarGridSpec(
            num_scalar_prefetch=1, grid=(nbr, nbc),
            in_specs=[pl.BlockSpec((1, tr, tc), lambda j, k, idx_ref: (idx_ref[1], j, k)),
                      pl.BlockSpec((3, tr, tc), lambda j, k, idx_ref: (0, j, k))],
            out_specs=pl.BlockSpec((tr, tc), out_map)),
        out_shape=jax.ShapeDtypeStruct(tuple(shard_shape), F32),
        compiler_params=_cp("parallel", "parallel"), name=name,
    )(idx, own, parts)


def _stack_sum(x, *, name):
    p, r, c = x.shape
    tr = _row_tile(r)

    def body(x_ref, o_ref):
        acc = x_ref[0].astype(F32)
        for q in range(1, p):
            acc = acc + x_ref[q].astype(F32)
        o_ref[...] = acc

    return pl.pallas_call(
        body, grid=(r // tr,),
        in_specs=[pl.BlockSpec((p, tr, c), lambda i: (0, i, 0))],
        out_specs=pl.BlockSpec((tr, c), lambda i: (i, 0)),
        out_shape=jax.ShapeDtypeStruct((r, c), F32),
        compiler_params=_cp("parallel"), name=name,
    )(x)


def _place():
    x, y, c = lax.axis_index("x"), lax.axis_index("y"), lax.axis_index("c")
    chips = [(1 - x, y), (x, 1 - y), (1 - x, 1 - y)]
    return x, y, c, chips


def _gather8(x_shard, *, name):
    m_per, n = x_shard.shape

    def body(x_ref, out_ref, send_sems, recv_sems, local_sem):
        x, y, c, chips = _place()
        me, sibling = (x, y, c), (x, y, 1 - c)

        def rows(px, py, pc):
            return out_ref.at[pl.ds((4 * px + 2 * py + pc) * m_per, m_per), :]

        def copy(k, block, to, src=None):
            return pltpu.make_async_remote_copy(
                src_ref=rows(*block) if src is None else src, dst_ref=rows(*block),
                send_sem=send_sems.at[k], recv_sem=recv_sems.at[k], device_id=to, device_id_type=MESH)

        mine = pltpu.make_async_copy(x_ref, rows(*me), local_sem)
        mine.start()
        first = [copy(0, me, sibling, src=x_ref)]
        first += [copy(1 + j, me, (*chip, c), src=x_ref) for j, chip in enumerate(chips)]
        for cp in first:
            cp.start()
        passed = [copy(4 + j, (*chip, c), sibling) for j, chip in enumerate(chips)]
        for j, chip in enumerate(chips):
            copy(1 + j, (*chip, c), me).wait_recv()
            passed[j].start()
        copy(0, sibling, me).wait_recv()
        for j, chip in enumerate(chips):
            copy(4 + j, (*chip, 1 - c), me).wait_recv()
        for cp in first + passed:
            cp.wait_send()
        mine.wait()

    return pl.pallas_call(
        body,
        out_shape=jax.ShapeDtypeStruct((8 * m_per, n), x_shard.dtype),
        in_specs=[pl.BlockSpec(memory_space=pltpu.VMEM)],
        out_specs=pl.BlockSpec(memory_space=pltpu.VMEM),
        scratch_shapes=[pltpu.SemaphoreType.DMA((7,)), pltpu.SemaphoreType.DMA((7,)), pltpu.SemaphoreType.DMA],
        name=name,
    )(x_shard)


def _gather_weights(shards, *, name):
    return _comm_call(lambda ins, outs: [cp for i, o in zip(ins, outs) for cp in _plan_gather_ici(i, o)],
                      shards, [jax.ShapeDtypeStruct((4,) + s.shape, s.dtype) for s in shards], name=name)


def _plan_start(plan, send_sems, recv_sems):
    for k, (src, dst, _, peer) in enumerate(plan):
        pltpu.make_async_remote_copy(src_ref=src, dst_ref=dst, send_sem=send_sems.at[k], recv_sem=recv_sems.at[k],
                                     device_id=peer, device_id_type=MESH).start()


def _plan_wait(plan, send_sems, recv_sems):
    for k, (src, _, land, peer) in enumerate(plan):
        pltpu.make_async_remote_copy(src_ref=src, dst_ref=land, send_sem=send_sems.at[k], recv_sem=recv_sems.at[k],
                                     device_id=peer, device_id_type=MESH).wait_recv()
    for k, (src, dst, _, peer) in enumerate(plan):
        pltpu.make_async_remote_copy(src_ref=src, dst_ref=dst, send_sem=send_sems.at[k], recv_sem=recv_sems.at[k],
                                     device_id=peer, device_id_type=MESH).wait_send()


def _half_axis(rows):
    return 0 if rows % 32 == 0 else 1


def _rows_half(ref, hc, axis):
    hr = ref.shape[axis] // 2
    idx = [slice(None)] * len(ref.shape)
    idx[axis] = pl.ds(hc * hr, hr)
    return ref.at[tuple(idx)]


def _plan_gather_ici(shard, full):
    x, y, c, chips = _place()
    ax = _half_axis(shard.shape[0])
    src = _rows_half(shard, c, ax)
    return [(src, _rows_half(full.at[2 * x + y], c, ax), _rows_half(full.at[2 * cx + cy], c, ax), (cx, cy, c))
            for cx, cy in chips]


def _plan_gather_d2d(full):
    x, y, c, chips = _place()
    ax = _half_axis(full.shape[1])
    plan = []
    for cx, cy in chips:
        slot = full.at[2 * cx + cy]
        plan.append((_rows_half(slot, c, ax), _rows_half(slot, c, ax), _rows_half(slot, 1 - c, ax), (x, y, 1 - c)))
    return plan


def _plan_pair(grad, got):
    x, y, c, _ = _place()
    return [(_rows_half(grad, 1 - c, 1 + _half_axis(grad.shape[1])), got, got, (x, y, 1 - c))]


def _plan_shard_ici(sums, parts):
    _, _, c, chips = _place()
    return [(sums.at[2 * cx + cy], parts.at[k], parts.at[k], (cx, cy, c)) for k, (cx, cy) in enumerate(chips)]


def _plan_half(buf):
    x, y, c, _ = _place()
    ax = _half_axis(buf.shape[0])
    mine = _rows_half(buf, c, ax)
    return [(mine, mine, _rows_half(buf, 1 - c, ax), (x, y, 1 - c))]


def _comm_call(plan_fn, inputs, out_shapes, *, name, aliases=None):
    ni, no = len(inputs), len(out_shapes)

    def body(*refs):
        plan = plan_fn(refs[:ni], refs[ni:ni + no])
        send_sems, recv_sems = refs[ni + no:]
        _plan_start(plan, send_sems, recv_sems)
        _plan_wait(plan, send_sems, recv_sems)

    any_spec = pl.BlockSpec(memory_space=pl.ANY)
    n_copies = 3 * max(ni, no)
    return pl.pallas_call(
        body, out_shape=list(out_shapes), in_specs=[any_spec] * ni, out_specs=[any_spec] * no,
        scratch_shapes=[pltpu.SemaphoreType.DMA((n_copies,)), pltpu.SemaphoreType.DMA((n_copies,))],
        input_output_aliases=aliases or {}, name=name,
    )(*inputs)


def _gather_forward(fulls, *, name):
    return _comm_call(lambda ins, outs: [cp for o in outs for cp in _plan_gather_d2d(o)],
                      fulls, [jax.ShapeDtypeStruct(f.shape, f.dtype) for f in fulls], name=name,
                      aliases={k: k for k in range(len(fulls))})


def _pair_exchange(grads, *, name):
    return _comm_call(lambda ins, outs: [cp for i, o in zip(ins, outs) for cp in _plan_pair(i, o)],
                      grads, [jax.ShapeDtypeStruct(_half_shape(g.shape), g.dtype) for g in grads], name=name)


def _half_exchange(bufs, *, name):
    return _comm_call(lambda ins, outs: [cp for o in outs for cp in _plan_half(o)],
                      bufs, [jax.ShapeDtypeStruct(b.shape, b.dtype) for b in bufs], name=name,
                      aliases={k: k for k in range(len(bufs))})


def _split_w_in(w_in_t):
    d = w_in_t.shape[1]
    main = jnp.concatenate([w_in_t[0:3072], w_in_t[3080:5128], w_in_t[5144:6168]], axis=0)
    small = jnp.concatenate([w_in_t[3072:3080], w_in_t[5128:5144], jnp.zeros((SMALL_W - 24, d), w_in_t.dtype)], axis=0)
    return main, small


def _merge_dw_in(dw_main, dw_small):
    return jnp.concatenate([dw_main[0:3072], dw_small[0:8], dw_main[3072:5120], dw_small[8:24], dw_main[5120:6144]],
                           axis=0)


def _gather_side(shards):
    return _Side(shards, [jax.ShapeDtypeStruct((4,) + w.shape, w.dtype) for w in shards],
                 lambda ins, outs: [cp for i, o in zip(ins, outs) for cp in _plan_gather_ici(i, o)], 3 * len(shards))


def _finish_gather(fulls, owns, chip, *, name):
    fulls = _gather_forward(list(fulls), name=name)
    return [lax.dynamic_update_index_in_dim(f, o, chip, 0) for f, o in zip(fulls, owns)]


def _shard_side(sums):
    return _Side([sums], [jax.ShapeDtypeStruct((3,) + sums.shape[1:], sums.dtype)],
                 lambda ins, outs: _plan_shard_ici(ins[0], outs[0]), 3)


def _chip_sum(grad, idx, tag):
    got, = _pair_exchange([grad], name=f"grad_pair_exchange_{tag}")
    return _pair_sum(grad, got, idx, name=f"grad_pair_sum_{tag}")


def _local_step(x, target, mod, g_pre_mix, g_post_mix, g_pre_mlp, g_post_mlp, w_in_t, b_fgate, w_gla_a2,
                b_gla_a2, g_fox, g_gla, own_w_out, own_w_mlp_in, own_w_mlp_out, chip, idx):
    s, d = x.shape
    shift_m, scale_m, gate_m, shift_f, scale_f, gate_f = [mod[:, i * d:(i + 1) * d] for i in range(6)]
    a1 = g_pre_mix * (1.0 + scale_m)
    a2 = g_pre_mlp * (1.0 + scale_f)
    w_main, w_small = _split_w_in(w_in_t)
    bf = jnp.concatenate([b_fgate, jnp.zeros((1, SMALL_W - FOX_HEADS), F32)], axis=1)
    w2p = jnp.zeros((SMALL_W, GLA_KW), F32).at[FOX_HEADS:FOX_HEADS + GLA_RANK].set(w_gla_a2)

    h1 = _pre_fwd(x, a1, shift_m, name="pre_mix_fwd")
    proj, gw_out = _mm(h1, w_main, mode="nt", out_dtypes=[BF16], name="in_proj_main",
                       side=_gather_side([own_w_out]))
    ps, = _mm(h1, w_small, mode="nt", out_dtypes=[F32], name="in_proj_small")
    gw_out, = _finish_gather([gw_out], [own_w_out], chip, name="gather_w_out_d2d")
    w_out_full = gw_out.reshape(-1, d)
    cum, log_a = _gates_fwd(ps, bf, w2p, b_gla_a2, name="gates_fwd")
    cum_t = cum[:, :FOX_HEADS].T
    o_fox, fox_n, lse, gw_mlp_in, gw_mlp_out = _fox_fwd(proj, cum_t, g_fox, name="fox_fwd",
                                                        side=_gather_side([own_w_mlp_in, own_w_mlp_out]))
    gw_mlp_in, gw_mlp_out = _finish_gather([gw_mlp_in, gw_mlp_out], [own_w_mlp_in, own_w_mlp_out], chip,
                                           name="gather_w_mlp_d2d")
    w_mlp_out_full = gw_mlp_out.reshape(-1, d)
    o_gla, gla_n, states = _gla_fwd(proj, log_a, g_gla, name="gla_fwd")
    mixed = jnp.concatenate([fox_n, gla_n], axis=1)
    y1, = _mm(mixed, w_out_full, mode="nn", out_dtypes=[F32], name="out_proj")
    x1 = _post_fwd(x, y1, gate_m, g_post_mix, name="post_mix_fwd")
    h2 = _pre_fwd(x1, a2, shift_f, name="pre_mlp_fwd")

    def mlp_act(acc):
        r = jnp.maximum(acc, 0.0)
        return acc, r * r

    u, act = _mm(h2, gw_mlp_in, mode="nn", out_dtypes=[BF16, BF16], epi=mlp_act, name="mlp_in", b_slots=4)
    y2, = _mm(act, w_mlp_out_full, mode="nn", out_dtypes=[F32], name="mlp_out")
    dx2, loss_part = _post_fwd_loss(x1, y2, gate_f, g_post_mlp, target, name="post_mlp_fwd_loss")

    dy2, dgate_f, dg_post_mlp = _post_bwd(dx2, y2, gate_f, g_post_mlp, name="post_mlp_bwd")
    dw_mlp_out, = _mm(act, dy2, mode="tn", out_dtypes=[BF16], name="dw_mlp_out")
    sum_mlp_out = _chip_sum(dw_mlp_out.reshape(4, D_FF // 4, d), idx, "mlp_out")

    def act_bwd(acc, uv):
        return (acc * (2.0 * jnp.maximum(uv.astype(F32), 0.0)),)

    du, parts_mlp_out = _mm(dy2, w_mlp_out_full, mode="nt", out_dtypes=[BF16], extras=[u], epi=act_bwd,
                            name="d_mlp_hidden", side=_shard_side(sum_mlp_out))
    nj = D_FF // 4 // min(MM_T, D_FF // 4)
    tmw = min(MM_T, d)
    dw_mlp_in, = _mm(h2, du, mode="tn", out_dtypes=[BF16], name="dw_mlp_in",
                     out_shapes=[jax.ShapeDtypeStruct((4, d, D_FF // 4), BF16)],
                     out_specs=[pl.BlockSpec((1, tmw, min(MM_T, D_FF // 4)), lambda i, j, kk: (j // nj, i, j % nj))])
    sum_mlp_in = _chip_sum(dw_mlp_in, idx, "mlp_in")
    dh2, parts_mlp_in = _mm(du, gw_mlp_in, mode="nt", out_dtypes=[F32], name="d_mlp_in", b_slots=4,
                            side=_shard_side(sum_mlp_in))
    dx1, dshift_f, da2 = _pre_bwd(dh2, x1, dx2, a2, name="pre_mlp_bwd")

    dy1, dgate_m, dg_post_mix = _post_bwd(dx1, y1, gate_m, g_post_mix, name="post_mix_bwd")
    dw_out, = _mm(mixed, dy1, mode="tn", out_dtypes=[BF16], name="dw_out")
    sum_out = _chip_sum(dw_out.reshape(4, d // 4, d), idx, "out")
    dmixed, parts_out = _mm(dy1, w_out_full, mode="nt", out_dtypes=[BF16], name="d_mixed", side=_shard_side(sum_out))
    do_fox, delta, dg_fox = _head_norm_bwd(dmixed, o_fox, g_fox, None, nh=FOX_HEADS, hd=FOX_HD, dn_col=0,
                                           gr_col=0, name="fox_norm_bwd")
    do_gla, dgr, _, dg_gla = _head_norm_bwd(dmixed, o_gla, g_gla, proj, nh=GLA_HEADS, hd=GLA_DV, dn_col=1,
                                            gr_col=(3 * FOX_W + 2 * GLA_KW + GLA_W) // GLA_W, name="gla_norm_bwd")
    dq_fox, dcq = _fox_bwd_dq(proj, do_fox, cum_t, lse, delta, name="fox_bwd_dq")
    dk_fox, dv_fox, dck_t = _fox_bwd_dkv(proj, do_fox, cum_t, lse, delta, name="fox_bwd_dkv")
    dgq, dgk, dgv, dla = _gla_bwd(proj, log_a, do_gla, states, name="gla_bwd")
    dck = dcq + jnp.concatenate([dck_t.T, jnp.zeros((s, SMALL_W - FOX_HEADS), F32)], axis=1)
    dps, dbf, dw2p, db2 = _gates_bwd(dck, ps, bf, w2p, b_gla_a2, dla, name="gates_bwd")
    dproj = jnp.concatenate([dq_fox, dk_fox, dv_fox, dgq, dgk, dgv, dgr], axis=1)
    dw_main, = _mm(dproj, h1, mode="tn", out_dtypes=[BF16], name="dw_in_main")
    dw_small, = _mm(dps, h1, mode="tn", out_dtypes=[BF16], name="dw_in_small")
    rs_in = w_in_t.shape[0] // 4
    dw_in = _merge_dw_in(dw_main, dw_small).reshape(4, rs_in, d)
    sum_in = _chip_sum(dw_in, idx, "in")
    dh1_small, = _mm(dps, w_small, mode="nn", out_dtypes=[F32], name="d_h1_small")
    dh1, parts_in = _mm(dproj, w_main, mode="nn", out_dtypes=[F32], extras=[dh1_small],
                        epi=lambda acc, e: (acc + e,), name="d_h1", side=_shard_side(sum_in))
    grad_x, dshift_m, da1 = _pre_bwd(dh1, x, dx1, a1, name="pre_mix_bwd")
    bufs = [_final_sum(sm, pt, idx, shp, name=f"grad_final_sum_{tag}")
            for tag, sm, pt, shp in [("in", sum_in, parts_in, (rs_in, d)), ("out", sum_out, parts_out, (d // 4, d)),
                                     ("mlp_in", sum_mlp_in, parts_mlp_in, (d, D_FF // 4)),
                                     ("mlp_out", sum_mlp_out, parts_mlp_out, (D_FF // 4, d))]]

    dmod = jnp.concatenate([dshift_m, da1 * g_pre_mix, dgate_m, dshift_f, da2 * g_pre_mlp, dgate_f], axis=1)
    small = dict(
        dmod=dmod, g_pre_mix=da1 * (1.0 + scale_m), g_post_mix=dg_post_mix, g_pre_mlp=da2 * (1.0 + scale_f),
        g_post_mlp=dg_post_mlp, b_fgate=dbf[:, :FOX_HEADS], w_gla_a2=dw2p[FOX_HEADS:FOX_HEADS + GLA_RANK],
        b_gla_a2=db2, g_fox_out=dg_fox, g_gla_out=dg_gla)
    return loss_part, grad_x, bufs, small


def _pack(arrays):
    flat = jnp.concatenate([a.reshape(-1).astype(F32) for a in arrays])
    n = flat.shape[0]
    rows = -(-n // 128)
    rows = -(-rows // 8) * 8
    return jnp.pad(flat, (0, rows * 128 - n)).reshape(rows, 128)


def _unpack(buf, shapes):
    flat = buf.reshape(-1)
    out, off = [], 0
    for shp in shapes:
        n = 1
        for q in shp:
            n *= q
        out.append(flat[off:off + n].reshape(shp))
        off += n
    return out


SMALL_GRAD_ORDER = ["dmod", "g_pre_mix", "g_post_mix", "g_pre_mlp", "g_post_mlp", "b_fgate", "w_gla_a2", "b_gla_a2",
                    "g_fox_out", "g_gla_out"]


def kernel(x, c, w_ada, b_ada, g_pre_mix, g_post_mix, w_in, b_fgate, w_gla_a2, b_gla_a2, g_fox_out, g_gla_out, w_out, g_pre_mlp, g_post_mlp, w_mlp_in, w_mlp_out, loss_target, m_w_ada, m_b_ada, m_g_pre_mix, m_g_post_mix, m_w_in, m_b_fgate, m_w_gla_a2, m_b_gla_a2, m_g_fox_out, m_g_gla_out, m_w_out, m_g_pre_mlp, m_g_post_mlp, m_w_mlp_in, m_w_mlp_out, v_w_ada, v_b_ada, v_g_pre_mix, v_g_post_mix, v_w_in, v_b_fgate, v_w_gla_a2, v_b_gla_a2, v_g_fox_out, v_g_gla_out, v_w_out, v_g_pre_mlp, v_g_post_mlp, v_w_mlp_in, v_w_mlp_out):
    ix, iy, ic = lax.axis_index("x"), lax.axis_index("y"), lax.axis_index("c")
    chip = 2 * ix + iy
    dev = 4 * ix + 2 * iy + ic
    d = D_MODEL

    c_act = _silu_rows(c, name="silu_c")
    pack1 = _pack([c_act, w_gla_a2[0], g_gla_out[0]])
    rows1 = pack1.shape[0]
    got1 = _gather8(pack1, name="gather_small_fwd").reshape(8, rows1, 128)
    per_dev = [_unpack(got1[q], [(d,), (GLA_RANK, GLA_KW // 4), (GLA_HEADS, GLA_DV // 4)]) for q in range(8)]
    c_all = jnp.stack([p[0] for p in per_dev])
    w_gla_a2_full = jnp.concatenate([per_dev[2 * j][1] for j in range(4)], axis=1)
    g_gla_full = jnp.concatenate([per_dev[2 * j][2] for j in range(4)], axis=1)
    cols = w_ada.shape[2]
    b_ada_shard = lax.dynamic_slice_in_dim(b_ada, chip * cols, cols, axis=1)
    mod_sh = _mod_shard(c_all, w_ada[0], b_ada_shard, name="ada_mod")
    got2 = _gather8(mod_sh, name="gather_mod").reshape(8, 8, cols)
    mod_all = jnp.concatenate([got2[2 * j] for j in range(4)], axis=1)
    mod = lax.dynamic_slice_in_dim(mod_all, dev, 1, axis=0)

    tr_in = lambda a: jnp.transpose(a[0])
    own_bf = [tr_in(w_in).astype(BF16), w_out[0].astype(BF16), w_mlp_in[0].astype(BF16), w_mlp_out[0].astype(BF16)]
    gw_in, = _finish_gather(_gather_weights(own_bf[:1], name="gather_w_in_ici"), own_bf[:1], chip,
                            name="gather_w_in_d2d")
    w_in_t = gw_in.reshape(-1, d)
    idx = jnp.stack([ic, chip]).astype(jnp.int32)

    loss_part, grad_x, bufs, small = _local_step(
        x[0], loss_target[0], mod, g_pre_mix, g_post_mix, g_pre_mlp, g_post_mlp, w_in_t, b_fgate,
        w_gla_a2_full, b_gla_a2, g_fox_out[0], g_gla_full, own_bf[1], own_bf[2], own_bf[3], chip, idx)
    loss = lax.psum(loss_part[0, 0], ("x", "y", "c"))

    g_big = _half_exchange(bufs, name="grad_half_exchange")
    big_w = [(tr_in(w_in), tr_in(m_w_in), tr_in(v_w_in)), (w_out[0], m_w_out[0], v_w_out[0]),
             (w_mlp_in[0], m_w_mlp_in[0], v_w_mlp_in[0]), (w_mlp_out[0], m_w_mlp_out[0], v_w_mlp_out[0])]
    big_res = []
    for q, (g, (w, m, v)) in enumerate(zip(g_big, big_w)):
        res4 = (g,) + tuple(_adam(g, w, m, v, name=f"adam_big_{q}"))
        big_res.append(tuple((jnp.transpose(a) if q == 0 else a)[None] for a in res4))

    pack2 = _pack([small[k] for k in SMALL_GRAD_ORDER])
    rows2 = pack2.shape[0]
    got3 = _gather8(pack2, name="gather_small_grads").reshape(8, rows2, 128)
    dmod_all = got3[:, :6 * d // 128, :].reshape(8, 6 * d)
    sums = _stack_sum(got3, name="small_grad_sum")
    shapes = [(1, 6 * d), (1, d), (1, d), (1, d), (1, d), (1, FOX_HEADS), (1, GLA_RANK, GLA_KW), (1, GLA_KW),
              (1, FOX_HEADS, FOX_HD), (1, GLA_HEADS, GLA_DV)]
    sg = dict(zip(["b_ada"] + SMALL_GRAD_ORDER[1:], _unpack(sums, shapes)))
    sg["w_gla_a2"] = lax.dynamic_slice_in_dim(sg["w_gla_a2"], chip * (GLA_KW // 4), GLA_KW // 4, axis=2)
    sg["g_gla_out"] = lax.dynamic_slice_in_dim(sg["g_gla_out"], chip * (GLA_DV // 4), GLA_DV // 4, axis=2)
    small_names = ["b_ada", "g_pre_mix", "g_post_mix", "b_fgate", "w_gla_a2", "b_gla_a2", "g_fox_out", "g_gla_out",
                   "g_pre_mlp", "g_post_mlp"]
    small_w = dict(b_ada=(b_ada, m_b_ada, v_b_ada), g_pre_mix=(g_pre_mix, m_g_pre_mix, v_g_pre_mix),
                   g_post_mix=(g_post_mix, m_g_post_mix, v_g_post_mix), b_fgate=(b_fgate, m_b_fgate, v_b_fgate),
                   w_gla_a2=(w_gla_a2, m_w_gla_a2, v_w_gla_a2), b_gla_a2=(b_gla_a2, m_b_gla_a2, v_b_gla_a2),
                   g_fox_out=(g_fox_out, m_g_fox_out, v_g_fox_out), g_gla_out=(g_gla_out, m_g_gla_out, v_g_gla_out),
                   g_pre_mlp=(g_pre_mlp, m_g_pre_mlp, v_g_pre_mlp), g_post_mlp=(g_post_mlp, m_g_post_mlp, v_g_post_mlp))
    sshapes = [small_w[k][0].shape for k in small_names]
    pg = _pack([sg[k] for k in small_names])
    pw, pm, pv = [_pack([small_w[k][q] for k in small_names]) for q in range(3)]
    pd, pmn, pvn = _adam(pg, pw, pm, pv, name="adam_small")
    s_delta = dict(zip(small_names, _unpack(pd, sshapes)))
    s_m = dict(zip(small_names, _unpack(pmn, sshapes)))
    s_v = dict(zip(small_names, _unpack(pvn, sshapes)))

    dmod_cols = lax.dynamic_slice_in_dim(dmod_all, chip * cols, cols, axis=1)
    g_ada, d_ada, m_ada, v_ada = _ada_grad_adam(c_all.T, dmod_cols, w_ada[0], m_w_ada[0], v_w_ada[0], name="ada_grad_adam")

    order = ["w_ada", "b_ada", "g_pre_mix", "g_post_mix", "w_in", "b_fgate", "w_gla_a2", "b_gla_a2", "g_fox_out",
             "g_gla_out", "w_out", "g_pre_mlp", "g_post_mlp", "w_mlp_in", "w_mlp_out"]
    res = {"w_ada": (g_ada[None], d_ada[None], m_ada[None], v_ada[None]),
           "w_in": big_res[0], "w_out": big_res[1], "w_mlp_in": big_res[2], "w_mlp_out": big_res[3]}
    for k in small_names:
        res[k] = (sg[k], s_delta[k], s_m[k], s_v[k])
    return (loss, grad_x[None], *[res[k][0] for k in order], *[res[k][1] for k in order],
            *[res[k][2] for k in order], *[res[k][3] for k in order])
```

```python
import functools

import jax
import jax.numpy as jnp
from jax import lax
from jax.experimental import pallas as pl
from jax.experimental.pallas import tpu as pltpu

F32 = jnp.float32
BF16 = jnp.bfloat16
MESH = pl.DeviceIdType.MESH
HIGHEST = lax.Precision.HIGHEST

D_MODEL = 2048
FOX_HEADS = 8
FOX_HD = 128
FOX_W = FOX_HEADS * FOX_HD
GLA_HEADS = 4
GLA_DK = 128
GLA_DV = 256
GLA_KW = GLA_HEADS * GLA_DK
GLA_W = GLA_HEADS * GLA_DV
GLA_RANK = 16
GLA_TEMP = 16.0
CHUNK = 64
D_FF = 4 * D_MODEL
EPS = 1e-6
MAIN_W = 3 * FOX_W + 2 * GLA_KW + 2 * GLA_W
SMALL_W = 128
NEG = -1e30

ADAM_LR = 0.001
ADAM_B1 = 0.9
ADAM_B2 = 0.999
ADAM_EPS = 1e-08
ADAM_WD = 0.01
ADAM_STEP = 10

VMEM_LIMIT = 52 * 1024 * 1024
ROW_TILE = 256
FOX_TQ = 512
FOX_TK = 512
GLA_ROWS = 512
GATE_TS = 512
MM_T = 1024
MM_TK = 2048


def _cp(*sem):
    return pltpu.CompilerParams(dimension_semantics=sem, vmem_limit_bytes=VMEM_LIMIT)


def _dot_nn(a, b, precision=None):
    return jnp.dot(a, b, preferred_element_type=F32, precision=precision)


def _dot_nt(a, b, precision=None):
    return lax.dot_general(a, b, (((1,), (1,)), ((), ())), preferred_element_type=F32, precision=precision)


def _dot_tn(a, b, precision=None):
    return lax.dot_general(a, b, (((0,), (0,)), ((), ())), preferred_element_type=F32, precision=precision)


def _sigmoid(x):
    return 1.0 / (1.0 + jnp.exp(-x))


def _log_sigmoid(x):
    return jnp.minimum(x, 0.0) - jnp.log(1.0 + jnp.exp(-jnp.abs(x)))


class _Side:
    def __init__(self, inputs, out_shapes, plan_fn, n_copies):
        self.inputs, self.out_shapes, self.plan_fn, self.n_copies = list(inputs), list(out_shapes), plan_fn, n_copies

    def scratch(self):
        return [pltpu.SemaphoreType.DMA((self.n_copies,)), pltpu.SemaphoreType.DMA((self.n_copies,))]

    def run(self, in_refs, out_refs, sems, first, last):
        @pl.when(first)
        def _():
            _plan_start(self.plan_fn(in_refs, out_refs), *sems)

        @pl.when(last)
        def _():
            _plan_wait(self.plan_fn(in_refs, out_refs), *sems)


def _mm(a, b, *, mode, out_dtypes, name, tm=None, tn=None, tk=None, extras=(), epi=None,
        out_shapes=None, out_specs=None, side=None, b_slots=0):
    tm, tn, tk = tm or MM_T, tn or MM_T, tk or MM_TK
    b2 = (b.shape[1], b_slots * b.shape[2]) if b_slots else b.shape
    if mode == "nn":
        (m, k), n = a.shape, b2[1]
    elif mode == "nt":
        (m, k), n = a.shape, b2[0]
    else:
        (k, m), n = a.shape, b2[1]
    tm, tn, tk = min(tm, m), min(tn, n), min(tk, k)
    if b_slots:
        tn = min(tn, b.shape[2]) if mode == "nn" else tn
        tk = min(tk, b.shape[2]) if mode == "nt" else tk
    assert m % tm == 0 and n % tn == 0 and k % tk == 0, (name, m, n, k)
    nk = k // tk
    n_out, n_ex = len(out_dtypes), len(extras)
    if epi is None:
        epi = lambda acc: tuple(acc for _ in range(n_out))
    dot = {"nn": _dot_nn, "nt": _dot_nt, "tn": _dot_tn}[mode]

    n_si = len(side.inputs) if side else 0
    n_so = len(side.out_shapes) if side else 0
    grid = (m // tm, n // tn, nk)

    def body(*refs):
        a_ref, b_ref = refs[0], refs[1]
        ex_refs = refs[2:2 + n_ex]
        base = 2 + n_ex + n_si
        o_refs = refs[base:base + n_out]
        scratch = refs[base + n_out + n_so:]
        if side:
            pos = [pl.program_id(q) for q in range(3)]
            first = (pos[0] == 0) & (pos[1] == 0) & (pos[2] == 0)
            last = (pos[0] == grid[0] - 1) & (pos[1] == grid[1] - 1) & (pos[2] == grid[2] - 1)
            side.run(refs[2 + n_ex:base], refs[base + n_out:base + n_out + n_so], scratch[-2:], first, last)
        part = dot(a_ref[...], b_ref[...])

        def finish(acc):
            outs = epi(acc, *[e[...] for e in ex_refs])
            for o_ref, val in zip(o_refs, outs):
                o_ref[...] = val.reshape(o_ref.shape).astype(o_ref.dtype)

        if nk == 1:
            finish(part)
        else:
            acc_ref = scratch[0]
            kk = pl.program_id(2)

            @pl.when(kk == 0)
            def _():
                acc_ref[...] = part

            @pl.when(kk > 0)
            def _():
                acc_ref[...] += part

            @pl.when(kk == nk - 1)
            def _():
                finish(acc_ref[...])

    if mode == "nn":
        a_spec = pl.BlockSpec((tm, tk), lambda i, j, kk: (i, kk))
        b_spec = pl.BlockSpec((tk, tn), lambda i, j, kk: (kk, j))
        if b_slots:
            per = b.shape[2] // tn
            b_spec = pl.BlockSpec((None, tk, tn), lambda i, j, kk: (j // per, kk, j % per))
    elif mode == "nt":
        a_spec = pl.BlockSpec((tm, tk), lambda i, j, kk: (i, kk))
        b_spec = pl.BlockSpec((tn, tk), lambda i, j, kk: (j, kk))
        if b_slots:
            per = b.shape[2] // tk
            b_spec = pl.BlockSpec((None, tn, tk), lambda i, j, kk: (kk // per, j, kk % per))
    else:
        assert not b_slots
        a_spec = pl.BlockSpec((tk, tm), lambda i, j, kk: (kk, i))
        b_spec = pl.BlockSpec((tk, tn), lambda i, j, kk: (kk, j))
    tile_spec = pl.BlockSpec((tm, tn), lambda i, j, kk: (i, j))
    if out_shapes is None:
        out_shapes = [jax.ShapeDtypeStruct((m, n), dt) for dt in out_dtypes]
    if out_specs is None:
        out_specs = [tile_spec for _ in out_dtypes]
    any_spec = pl.BlockSpec(memory_space=pl.ANY)
    res = pl.pallas_call(
        body,
        grid=grid,
        in_specs=[a_spec, b_spec] + [tile_spec for _ in extras] + [any_spec] * n_si,
        out_specs=list(out_specs) + [any_spec] * n_so,
        out_shape=list(out_shapes) + (side.out_shapes if side else []),
        scratch_shapes=([pltpu.VMEM((tm, tn), F32)] if nk > 1 else []) + (side.scratch() if side else []),
        compiler_params=_cp("arbitrary", "arbitrary", "arbitrary") if side else _cp("parallel", "parallel", "arbitrary"),
        name=name,
    )(a, b, *extras, *(side.inputs if side else []))
    return res


def _row_spec(ts, d):
    return pl.BlockSpec((ts, d), lambda i: (i, 0))


def _vec_spec(d):
    return pl.BlockSpec((1, d), lambda i: (0, 0))


def _pre_fwd(x, avec, shift, *, name):
    s, d = x.shape
    ts = min(ROW_TILE, s)

    def body(x_ref, a_ref, s_ref, h_ref):
        xv = x_ref[...]
        r = lax.rsqrt(jnp.mean(xv * xv, axis=-1, keepdims=True) + EPS)
        h_ref[...] = (xv * r * a_ref[...] + s_ref[...]).astype(BF16)

    return pl.pallas_call(
        body, grid=(s // ts,),
        in_specs=[_row_spec(ts, d), _vec_spec(d), _vec_spec(d)],
        out_specs=_row_spec(ts, d),
        out_shape=jax.ShapeDtypeStruct((s, d), BF16),
        compiler_params=_cp("parallel"), name=name,
    )(x, avec, shift)


def _post_fwd(x, y, gate, g, *, name):
    s, d = x.shape
    ts = min(ROW_TILE, s)

    def body(x_ref, y_ref, gate_ref, g_ref, o_ref):
        yv = y_ref[...]
        r = lax.rsqrt(jnp.mean(yv * yv, axis=-1, keepdims=True) + EPS)
        o_ref[...] = x_ref[...] + gate_ref[...] * (yv * r * g_ref[...])

    return pl.pallas_call(
        body, grid=(s // ts,),
        in_specs=[_row_spec(ts, d), _row_spec(ts, d), _vec_spec(d), _vec_spec(d)],
        out_specs=_row_spec(ts, d),
        out_shape=jax.ShapeDtypeStruct((s, d), F32),
        compiler_params=_cp("parallel"), name=name,
    )(x, y, gate, g)


def _post_fwd_loss(x, y, gate, g, target, *, name):
    s, d = x.shape
    ts = min(ROW_TILE, s)

    def body(x_ref, y_ref, gate_ref, g_ref, t_ref, dx_ref, loss_ref):
        yv = y_ref[...]
        r = lax.rsqrt(jnp.mean(yv * yv, axis=-1, keepdims=True) + EPS)
        diff = x_ref[...] + gate_ref[...] * (yv * r * g_ref[...]) - t_ref[...]
        dx_ref[...] = diff * (1.0 / d)

        @pl.when(pl.program_id(0) == 0)
        def _():
            loss_ref[...] = jnp.zeros_like(loss_ref)

        loss_ref[...] += jnp.sum(jnp.mean(diff * diff, axis=-1, keepdims=True)) * 0.5

    return pl.pallas_call(
        body, grid=(s // ts,),
        in_specs=[_row_spec(ts, d), _row_spec(ts, d), _vec_spec(d), _vec_spec(d), _row_spec(ts, d)],
        out_specs=[_row_spec(ts, d), pl.BlockSpec((1, 128), lambda i: (0, 0))],
        out_shape=[jax.ShapeDtypeStruct((s, d), F32), jax.ShapeDtypeStruct((1, 128), F32)],
        compiler_params=_cp("arbitrary"), name=name,
    )(x, y, gate, g, target)


def _post_bwd(dxo, y, gate, g, *, name):
    s, d = y.shape
    ts = min(ROW_TILE, s)

    def body(dx_ref, y_ref, gate_ref, g_ref, dy_ref, dgate_ref, dg_ref):
        yv, dxv, gv = y_ref[...], dx_ref[...], g_ref[...]
        r = lax.rsqrt(jnp.mean(yv * yv, axis=-1, keepdims=True) + EPS)
        yhat = yv * r
        dn = dxv * gate_ref[...]
        dyhat = dn * gv
        dy = r * (dyhat - yhat * jnp.mean(dyhat * yhat, axis=-1, keepdims=True))
        dy_ref[...] = dy.astype(BF16)

        @pl.when(pl.program_id(0) == 0)
        def _():
            dgate_ref[...] = jnp.zeros_like(dgate_ref)
            dg_ref[...] = jnp.zeros_like(dg_ref)

        dgate_ref[...] += jnp.sum(dxv * (yhat * gv), axis=0, keepdims=True)
        dg_ref[...] += jnp.sum(dn * yhat, axis=0, keepdims=True)

    return pl.pallas_call(
        body, grid=(s // ts,),
        in_specs=[_row_spec(ts, d), _row_spec(ts, d), _vec_spec(d), _vec_spec(d)],
        out_specs=[_row_spec(ts, d), _vec_spec(d), _vec_spec(d)],
        out_shape=[jax.ShapeDtypeStruct((s, d), BF16), jax.ShapeDtypeStruct((1, d), F32),
                   jax.ShapeDtypeStruct((1, d), F32)],
        compiler_params=_cp("arbitrary"), name=name,
    )(dxo, y, gate, g)


def _pre_bwd(dh, xin, dres, avec, *, name):
    s, d = xin.shape
    ts = min(ROW_TILE, s)

    def body(dh_ref, x_ref, dres_ref, a_ref, dx_ref, dshift_ref, da_ref):
        xv, dhv = x_ref[...], dh_ref[...]
        r = lax.rsqrt(jnp.mean(xv * xv, axis=-1, keepdims=True) + EPS)
        xhat = xv * r
        dxhat = dhv * a_ref[...]
        dx_ref[...] = dres_ref[...] + r * (dxhat - xhat * jnp.mean(dxhat * xhat, axis=-1, keepdims=True))

        @pl.when(pl.program_id(0) == 0)
        def _():
            dshift_ref[...] = jnp.zeros_like(dshift_ref)
            da_ref[...] = jnp.zeros_like(da_ref)

        dshift_ref[...] += jnp.sum(dhv, axis=0, keepdims=True)
        da_ref[...] += jnp.sum(dhv * xhat, axis=0, keepdims=True)

    return pl.pallas_call(
        body, grid=(s // ts,),
        in_specs=[_row_spec(ts, d), _row_spec(ts, d), _row_spec(ts, d), _vec_spec(d)],
        out_specs=[_row_spec(ts, d), _vec_spec(d), _vec_spec(d)],
        out_shape=[jax.ShapeDtypeStruct((s, d), F32), jax.ShapeDtypeStruct((1, d), F32),
                   jax.ShapeDtypeStruct((1, d), F32)],
        compiler_params=_cp("arbitrary"), name=name,
    )(dh, xin, dres, avec)


def _tri(n, strict=False, upper=False):
    r = lax.broadcasted_iota(jnp.int32, (n, n), 0)
    c = lax.broadcasted_iota(jnp.int32, (n, n), 1)
    if upper:
        r, c = c, r
    return ((r > c) if strict else (r >= c)).astype(F32)


def _gates_fwd(ps, bf, w2p, b2, *, name):
    s = ps.shape[0]
    ts = min(GATE_TS, s)

    def body(ps_ref, bf_ref, w_ref, b2_ref, cum_ref, la_ref, carry_ref):
        @pl.when(pl.program_id(0) == 0)
        def _():
            carry_ref[...] = jnp.zeros_like(carry_ref)

        psv = ps_ref[...]
        lf = _log_sigmoid(psv + bf_ref[...])
        cum = _dot_nn(_tri(ts), lf, HIGHEST) + carry_ref[...]
        cum_ref[...] = cum
        carry_ref[...] = cum[ts - 1:ts, :]
        z = _dot_nn(psv, w_ref[...], HIGHEST) + b2_ref[...]
        la_ref[...] = _log_sigmoid(z) * (1.0 / GLA_TEMP)

    return pl.pallas_call(
        body, grid=(s // ts,),
        in_specs=[_row_spec(ts, SMALL_W), _vec_spec(SMALL_W),
                  pl.BlockSpec((SMALL_W, GLA_KW), lambda i: (0, 0)), _vec_spec(GLA_KW)],
        out_specs=[_row_spec(ts, SMALL_W), _row_spec(ts, GLA_KW)],
        out_shape=[jax.ShapeDtypeStruct((s, SMALL_W), F32), jax.ShapeDtypeStruct((s, GLA_KW), F32)],
        scratch_shapes=[pltpu.VMEM((1, SMALL_W), F32)],
        compiler_params=_cp("arbitrary"), name=name,
    )(ps, bf, w2p, b2)


def _gates_bwd(dck, ps, bf, w2p, b2, dla, *, name):
    s = ps.shape[0]
    ts = min(GATE_TS, s)
    nb = s // ts
    rev = lambda i: (nb - 1 - i, 0)

    def body(dck_ref, ps_ref, bf_ref, w_ref, b2_ref, dla_ref, dps_ref, dbf_ref, dw_ref, db2_ref, carry_ref):
        @pl.when(pl.program_id(0) == 0)
        def _():
            carry_ref[...] = jnp.zeros_like(carry_ref)
            dbf_ref[...] = jnp.zeros_like(dbf_ref)
            dw_ref[...] = jnp.zeros_like(dw_ref)
            db2_ref[...] = jnp.zeros_like(db2_ref)

        psv, dckv = ps_ref[...], dck_ref[...]
        dlf = _dot_nn(_tri(ts, upper=True), dckv, HIGHEST) + carry_ref[...]
        carry_ref[...] += jnp.sum(dckv, axis=0, keepdims=True)
        lane = lax.broadcasted_iota(jnp.int32, (ts, SMALL_W), 1)
        dff = jnp.where(lane < FOX_HEADS, dlf * _sigmoid(-(psv + bf_ref[...])), 0.0)
        z = _dot_nn(psv, w_ref[...], HIGHEST) + b2_ref[...]
        dz = dla_ref[...] * _sigmoid(-z) * (1.0 / GLA_TEMP)
        dps_ref[...] = (_dot_nt(dz, w_ref[...], HIGHEST) + dff).astype(BF16)
        dbf_ref[...] += jnp.sum(dff, axis=0, keepdims=True)
        dw_ref[...] += _dot_tn(psv, dz, HIGHEST)
        db2_ref[...] += jnp.sum(dz, axis=0, keepdims=True)

    return pl.pallas_call(
        body, grid=(nb,),
        in_specs=[pl.BlockSpec((ts, SMALL_W), rev), pl.BlockSpec((ts, SMALL_W), rev), _vec_spec(SMALL_W),
                  pl.BlockSpec((SMALL_W, GLA_KW), lambda i: (0, 0)), _vec_spec(GLA_KW),
                  pl.BlockSpec((ts, GLA_KW), rev)],
        out_specs=[pl.BlockSpec((ts, SMALL_W), rev), _vec_spec(SMALL_W),
                   pl.BlockSpec((SMALL_W, GLA_KW), lambda i: (0, 0)), _vec_spec(GLA_KW)],
        out_shape=[jax.ShapeDtypeStruct((s, SMALL_W), BF16), jax.ShapeDtypeStruct((1, SMALL_W), F32),
                   jax.ShapeDtypeStruct((SMALL_W, GLA_KW), F32), jax.ShapeDtypeStruct((1, GLA_KW), F32)],
        scratch_shapes=[pltpu.VMEM((1, SMALL_W), F32)],
        compiler_params=_cp("arbitrary"), name=name,
    )(dck, ps, bf, w2p, b2, dla)


def _hs(h, hd=FOX_HD):
    return slice(h * hd, (h + 1) * hd)


def _fox_fwd(proj, cum_t, g_fox, *, name, side=None):
    s = proj.shape[0]
    tq, tk = min(FOX_TQ, s), min(FOX_TK, s)
    scale = FOX_HD ** -0.5
    n_si = len(side.inputs) if side else 0
    n_so = len(side.out_shapes) if side else 0
    grid = (s // tq, s // tk)

    def body(*refs):
        q_ref, k_ref, v_ref, ck_ref, g_ref = refs[:5]
        o_ref, n_ref, lse_ref = refs[5 + n_si:8 + n_si]
        m_sc, acc_sc = refs[8 + n_si + n_so:10 + n_si + n_so]
        i, j = pl.program_id(0), pl.program_id(1)
        if side:
            side.run(refs[5:5 + n_si], refs[8 + n_si:8 + n_si + n_so], refs[10 + n_si + n_so:],
                     (i == 0) & (j == 0), (i == grid[0] - 1) & (j == grid[1] - 1))

        @pl.when(j == 0)
        def _():
            m_sc[...] = jnp.full_like(m_sc, NEG)
            acc_sc[...] = jnp.zeros_like(acc_sc)

        def block(masked):
            mask = _causal_mask(i, j, tq, tk) if masked else None
            ones = jnp.ones((tk, FOX_HD), BF16)
            for h in range(FOX_HEADS):
                sc = _fox_logits(_dot_nt(q_ref[:, _hs(h)], k_ref[:, _hs(h)]), ck_ref[h:h + 1, :], mask, scale)
                m_prev = m_sc[h]
                m_new = jnp.maximum(m_prev, jnp.max(sc, axis=-1, keepdims=True))
                alpha = jnp.exp(m_prev - m_new)
                p = jnp.exp(sc - m_new).astype(BF16)
                v_one = jnp.concatenate([v_ref[:, _hs(h)], ones], axis=1)
                acc_sc[:, _hs(h, 2 * FOX_HD)] = alpha * acc_sc[:, _hs(h, 2 * FOX_HD)] + _dot_nn(p, v_one)
                m_sc[h] = m_new

        pl.when(j < i)(functools.partial(block, False))

        @pl.when(j == i)
        def _():
            block(True)
            lane = lax.broadcasted_iota(jnp.int32, (tq, 128), 1)
            lse = jnp.zeros((tq, 128), F32)
            for h in range(FOX_HEADS):
                l_rep = acc_sc[:, 2 * h * FOX_HD + FOX_HD:2 * (h + 1) * FOX_HD]
                o = acc_sc[:, 2 * h * FOX_HD:2 * h * FOX_HD + FOX_HD] / l_rep
                o_ref[:, _hs(h)] = o
                r = lax.rsqrt(jnp.mean(o * o, axis=-1, keepdims=True) + EPS)
                n_ref[:, _hs(h)] = (o * r * g_ref[h:h + 1, :]).astype(BF16)
                lse = jnp.where(lane == h, m_sc[h] + jnp.log(l_rep), lse)
            lse_ref[...] = lse

    kv = lambda col: (lambda i, j: (jnp.minimum(j, i), col))
    any_spec = pl.BlockSpec(memory_space=pl.ANY)
    return pl.pallas_call(
        body, grid=grid,
        in_specs=[pl.BlockSpec((tq, FOX_W), lambda i, j: (i, 0)),
                  pl.BlockSpec((tk, FOX_W), kv(1)),
                  pl.BlockSpec((tk, FOX_W), kv(2)),
                  pl.BlockSpec((FOX_HEADS, tk), lambda i, j: (0, jnp.minimum(j, i))),
                  pl.BlockSpec((FOX_HEADS, FOX_HD), lambda i, j: (0, 0))] + [any_spec] * n_si,
        out_specs=[pl.BlockSpec((tq, FOX_W), lambda i, j: (i, 0)),
                   pl.BlockSpec((tq, FOX_W), lambda i, j: (i, 0)),
                   pl.BlockSpec((tq, 128), lambda i, j: (i, 0))] + [any_spec] * n_so,
        out_shape=[jax.ShapeDtypeStruct((s, FOX_W), F32), jax.ShapeDtypeStruct((s, FOX_W), BF16),
                   jax.ShapeDtypeStruct((s, 128), F32)] + (side.out_shapes if side else []),
        scratch_shapes=[pltpu.VMEM((FOX_HEADS, tq, 1), F32), pltpu.VMEM((tq, 2 * FOX_W), F32)]
        + (side.scratch() if side else []),
        compiler_params=_cp("arbitrary", "arbitrary"), name=name,
    )(proj, proj, proj, cum_t, g_fox, *(side.inputs if side else []))


def _causal_mask(i, j, tq, tk):
    rows = i * tq + lax.broadcasted_iota(jnp.int32, (tq, tk), 0)
    cols = j * tk + lax.broadcasted_iota(jnp.int32, (tq, tk), 1)
    return rows >= cols


def _fox_logits(qk, ck, mask, scale):
    sc = qk * scale - ck
    return sc if mask is None else jnp.where(mask, sc, NEG)


def _fox_bwd(proj, do, cum_t, lse, delta, *, name):
    s = proj.shape[0]
    tq, tk = min(FOX_TQ, s), min(FOX_TK, s)
    nk, nq = s // tk, s // tq
    scale = FOX_HD ** -0.5

    def body(q_ref, k_ref, v_ref, do_ref, ck_ref, lse_ref, dl_ref, dq_hbm, dk_ref, dv_ref, dcq_hbm, dck_ref,
             dq_sc, dcq_sc, dk_sc, dv_sc, dck_sc, out_sems):
        j, i = pl.program_id(0), pl.program_id(1)

        @pl.when((j == 0) & (i == 0))
        def _():
            dq_sc[...] = jnp.zeros_like(dq_sc)
            dcq_sc[...] = jnp.zeros_like(dcq_sc)

        @pl.when(i == 0)
        def _():
            dk_sc[...] = jnp.zeros_like(dk_sc)
            dv_sc[...] = jnp.zeros_like(dv_sc)
            dck_sc[...] = jnp.zeros_like(dck_sc)

        def block(masked):
            mask = _causal_mask(i, j, tq, tk) if masked else None
            qrows = pl.ds(pl.multiple_of(i * tq, tq), tq)
            for h in range(FOX_HEADS):
                sc = _fox_logits(_dot_nt(q_ref[:, _hs(h)], k_ref[:, _hs(h)]), ck_ref[h:h + 1, :], mask, scale)
                p = jnp.exp(sc - lse_ref[:, h:h + 1])
                ds = p * (_dot_nt(do_ref[:, _hs(h)], v_ref[:, _hs(h)]) - dl_ref[:, h:h + 1])
                dsb = ds.astype(BF16)
                dv_sc[:, _hs(h)] += _dot_tn(p.astype(BF16), do_ref[:, _hs(h)])
                dk_sc[:, _hs(h)] += _dot_tn(dsb, q_ref[:, _hs(h)])
                dq_sc[qrows, _hs(h)] += _dot_nn(dsb, k_ref[:, _hs(h)]) * scale
                dck_sc[h:h + 1, :] -= jnp.sum(ds, axis=0, keepdims=True)
                dcq_sc[qrows, h:h + 1] += jnp.sum(ds, axis=-1, keepdims=True)

        pl.when(i > j)(functools.partial(block, False))
        pl.when(i == j)(functools.partial(block, True))

        @pl.when(i == nq - 1)
        def _():
            dk_ref[...] = (dk_sc[...] * scale).astype(BF16)
            dv_ref[...] = dv_sc[...].astype(BF16)
            dck_ref[...] = dck_sc[...]

        @pl.when((j == nk - 1) & (i == nq - 1))
        def _():
            out_q = pltpu.make_async_copy(dq_sc, dq_hbm, out_sems.at[0])
            out_c = pltpu.make_async_copy(dcq_sc, dcq_hbm, out_sems.at[1])
            out_q.start()
            out_c.start()
            out_q.wait()
            out_c.wait()

    qrow = lambda j, i: (jnp.maximum(i, j), 0)
    krow = lambda col: (lambda j, i: (j, col))
    any_spec = pl.BlockSpec(memory_space=pl.ANY)
    return pl.pallas_call(
        body, grid=(nk, nq),
        in_specs=[pl.BlockSpec((tq, FOX_W), qrow), pl.BlockSpec((tk, FOX_W), krow(1)),
                  pl.BlockSpec((tk, FOX_W), krow(2)),
                  pl.BlockSpec((tq, FOX_W), qrow),
                  pl.BlockSpec((FOX_HEADS, tk), lambda j, i: (0, j)),
                  pl.BlockSpec((tq, 128), qrow), pl.BlockSpec((tq, 128), qrow)],
        out_specs=[any_spec, pl.BlockSpec((tk, FOX_W), lambda j, i: (j, 0)),
                   pl.BlockSpec((tk, FOX_W), lambda j, i: (j, 0)), any_spec,
                   pl.BlockSpec((FOX_HEADS, tk), lambda j, i: (0, j))],
        out_shape=[jax.ShapeDtypeStruct((s, FOX_W), F32), jax.ShapeDtypeStruct((s, FOX_W), BF16),
                   jax.ShapeDtypeStruct((s, FOX_W), BF16), jax.ShapeDtypeStruct((s, 128), F32),
                   jax.ShapeDtypeStruct((FOX_HEADS, s), F32)],
        scratch_shapes=[pltpu.VMEM((s, FOX_W), F32), pltpu.VMEM((s, 128), F32),
                        pltpu.VMEM((tk, FOX_W), F32), pltpu.VMEM((tk, FOX_W), F32), pltpu.VMEM((FOX_HEADS, tk), F32),
                        pltpu.SemaphoreType.DMA((2,))],
        compiler_params=_cp("arbitrary", "arbitrary"), name=name,
    )(proj, proj, proj, do, cum_t, lse, delta)


def _head_norm_bwd(dn_in, o, g, gr_src, *, nh, hd, dn_col, gr_col, name):
    s, w = o.shape
    ts = min(ROW_TILE, s)
    gated = gr_src is not None

    def body(*refs):
        if gated:
            dn_ref, o_ref, g_ref, gr_ref, do_ref, dgr_ref, dl_ref, dg_ref = refs
        else:
            dn_ref, o_ref, g_ref, do_ref, dl_ref, dg_ref = refs

        @pl.when(pl.program_id(0) == 0)
        def _():
            dg_ref[...] = jnp.zeros_like(dg_ref)

        lane = lax.broadcasted_iota(jnp.int32, (ts, 128), 1)
        delta = jnp.zeros((ts, 128), F32)
        for h in range(nh):
            sl = _hs(h, hd)
            ov = o_ref[:, sl]
            dnv = dn_ref[:, sl].astype(F32)
            gv = g_ref[h:h + 1, :]
            r = lax.rsqrt(jnp.mean(ov * ov, axis=-1, keepdims=True) + EPS)
            ohat = ov * r
            if gated:
                grv = gr_ref[:, sl].astype(F32)
                sig = _sigmoid(grv)
                dgr_ref[:, sl] = (dnv * (ohat * gv) * (sig * (1.0 + grv * (1.0 - sig)))).astype(BF16)
                dnv = dnv * (grv * sig)
            dg_ref[h:h + 1, :] += jnp.sum(dnv * ohat, axis=0, keepdims=True)
            dohat = dnv * gv
            do = r * (dohat - ohat * jnp.mean(dohat * ohat, axis=-1, keepdims=True))
            do_ref[:, sl] = do.astype(BF16)
            delta = jnp.where(lane == h, jnp.sum(do * ov, axis=-1, keepdims=True), delta)
        dl_ref[...] = delta

    in_specs = [pl.BlockSpec((ts, w), lambda i: (i, dn_col)), _row_spec(ts, w),
                pl.BlockSpec((nh, hd), lambda i: (0, 0))]
    args = [dn_in, o, g]
    out_specs = [_row_spec(ts, w)]
    out_shape = [jax.ShapeDtypeStruct((s, w), BF16)]
    if gated:
        in_specs.append(pl.BlockSpec((ts, w), lambda i: (i, gr_col)))
        args.append(gr_src)
        out_specs.append(_row_spec(ts, w))
        out_shape.append(jax.ShapeDtypeStruct((s, w), BF16))
    out_specs += [_row_spec(ts, 128), pl.BlockSpec((nh, hd), lambda i: (0, 0))]
    out_shape += [jax.ShapeDtypeStruct((s, 128), F32), jax.ShapeDtypeStruct((nh, hd), F32)]
    return pl.pallas_call(
        body, grid=(s // ts,), in_specs=in_specs, out_specs=out_specs, out_shape=out_shape,
        compiler_params=_cp("arbitrary"), name=name,
    )(*args)


GQ_BLK = 3 * FOX_W // GLA_DK
GK_BLK = GQ_BLK + GLA_HEADS
GV_BLK = (3 * FOX_W + 2 * GLA_KW) // GLA_DV
GR_BLK = GV_BLK + GLA_HEADS


def _gla_chunk_terms(la):
    cum = _dot_nn(_tri(CHUNK), la, HIGHEST)
    total = cum[CHUNK - 1:CHUNK, :]
    return jnp.exp(total - cum), jnp.exp(total)


def _gla_fwd(proj, log_a, g_gla, *, name):
    s = proj.shape[0]
    rows = min(GLA_ROWS, s)
    cb = rows // CHUNK
    nblk = s // rows
    scale = GLA_DK ** -0.5

    def body(q_ref, k_ref, v_ref, gr_ref, la_ref, g_ref, o_ref, n_ref, st_ref, st_sc):
        h = pl.program_id(0)

        @pl.when(pl.program_id(1) == 0)
        def _():
            st_sc[...] = jnp.zeros_like(st_sc)

        gv = g_ref[pl.ds(h, 1), :]
        for ci in range(cb):
            sl = slice(ci * CHUNK, (ci + 1) * CHUNK)
            e, dec = _gla_chunk_terms(la_ref[sl, :])
            k_dec = (k_ref[sl, :].astype(F32) * e).astype(BF16)
            st = st_sc[...] * dec + _dot_tn(v_ref[sl, :], k_dec)
            st_sc[...] = st
            st_ref[0, ci] = st
            qs = (q_ref[sl, :].astype(F32) * scale).astype(BF16)
            o = _dot_nt(qs, st.astype(BF16))
            o_ref[sl, :] = o
            r = lax.rsqrt(jnp.mean(o * o, axis=-1, keepdims=True) + EPS)
            grv = gr_ref[sl, :].astype(F32)
            n_ref[sl, :] = (o * r * gv * (grv * _sigmoid(grv))).astype(BF16)

    return pl.pallas_call(
        body, grid=(GLA_HEADS, nblk),
        in_specs=[pl.BlockSpec((rows, GLA_DK), lambda h, n: (n, GQ_BLK + h)),
                  pl.BlockSpec((rows, GLA_DK), lambda h, n: (n, GK_BLK + h)),
                  pl.BlockSpec((rows, GLA_DV), lambda h, n: (n, GV_BLK + h)),
                  pl.BlockSpec((rows, GLA_DV), lambda h, n: (n, GR_BLK + h)),
                  pl.BlockSpec((rows, GLA_DK), lambda h, n: (n, h)),
                  pl.BlockSpec((GLA_HEADS, GLA_DV), lambda h, n: (0, 0))],
        out_specs=[pl.BlockSpec((rows, GLA_DV), lambda h, n: (n, h)),
                   pl.BlockSpec((rows, GLA_DV), lambda h, n: (n, h)),
                   pl.BlockSpec((1, cb, GLA_DV, GLA_DK), lambda h, n: (h, n, 0, 0))],
        out_shape=[jax.ShapeDtypeStruct((s, GLA_W), F32), jax.ShapeDtypeStruct((s, GLA_W), BF16),
                   jax.ShapeDtypeStruct((GLA_HEADS, s // CHUNK, GLA_DV, GLA_DK), F32)],
        scratch_shapes=[pltpu.VMEM((GLA_DV, GLA_DK), F32)],
        compiler_params=_cp("parallel", "arbitrary"), name=name,
    )(proj, proj, proj, proj, log_a, g_gla)


def _gla_bwd(proj, log_a, do, states, *, name):
    s = proj.shape[0]
    rows = min(GLA_ROWS, s)
    cb = rows // CHUNK
    nblk = s // rows
    scale = GLA_DK ** -0.5

    def body(q_ref, k_ref, v_ref, la_ref, do_ref, st_ref, prev_ref, dq_ref, dk_ref, dv_ref, dla_ref, g_sc):
        nrev = pl.program_id(1)
        blk = nblk - 1 - nrev

        @pl.when(nrev == 0)
        def _():
            g_sc[...] = jnp.zeros_like(g_sc)

        for ci in reversed(range(cb)):
            sl = slice(ci * CHUNK, (ci + 1) * CHUNK)
            e, dec = _gla_chunk_terms(la_ref[sl, :])
            kd = k_ref[sl, :].astype(F32) * e
            qs = (q_ref[sl, :].astype(F32) * scale).astype(BF16)
            dov = do_ref[sl, :]
            st = st_ref[0, ci]
            if ci > 0:
                st_prev = st_ref[0, ci - 1]
            else:
                st_prev = prev_ref[0, 0] * (blk > 0).astype(F32)
            dq_ref[sl, :] = (_dot_nn(dov, st.astype(BF16)) * scale).astype(BF16)
            gt = g_sc[...] + _dot_tn(dov, qs)
            gtb = gt.astype(BF16)
            dkd = _dot_nn(v_ref[sl, :], gtb)
            dv_ref[sl, :] = _dot_nt(kd.astype(BF16), gtb).astype(BF16)
            dk_ref[sl, :] = (dkd * e).astype(BF16)
            ddec = jnp.sum(gt * st_prev, axis=0, keepdims=True) * dec
            dla_ref[sl, :] = _dot_nn(_tri(CHUNK, strict=True), dkd * kd, HIGHEST) + ddec
            g_sc[...] = gt * dec

    rev = lambda col0: (lambda h, n: (nblk - 1 - n, col0 + h))
    return pl.pallas_call(
        body, grid=(GLA_HEADS, nblk),
        in_specs=[pl.BlockSpec((rows, GLA_DK), rev(GQ_BLK)),
                  pl.BlockSpec((rows, GLA_DK), rev(GK_BLK)),
                  pl.BlockSpec((rows, GLA_DV), rev(GV_BLK)),
                  pl.BlockSpec((rows, GLA_DK), rev(0)),
                  pl.BlockSpec((rows, GLA_DV), rev(0)),
                  pl.BlockSpec((1, cb, GLA_DV, GLA_DK), lambda h, n: (h, nblk - 1 - n, 0, 0)),
                  pl.BlockSpec((1, 1, GLA_DV, GLA_DK),
                               lambda h, n: (h, jnp.maximum((nblk - 1 - n) * cb - 1, 0), 0, 0))],
        out_specs=[pl.BlockSpec((rows, GLA_DK), rev(0)), pl.BlockSpec((rows, GLA_DK), rev(0)),
                   pl.BlockSpec((rows, GLA_DV), rev(0)), pl.BlockSpec((rows, GLA_DK), rev(0))],
        out_shape=[jax.ShapeDtypeStruct((s, GLA_KW), BF16), jax.ShapeDtypeStruct((s, GLA_KW), BF16),
                   jax.ShapeDtypeStruct((s, GLA_W), BF16), jax.ShapeDtypeStruct((s, GLA_KW), F32)],
        scratch_shapes=[pltpu.VMEM((GLA_DV, GLA_DK), F32)],
        compiler_params=_cp("parallel", "arbitrary"), name=name,
    )(proj, proj, proj, log_a, do, states, states)


def _row_tile(r):
    tr = min(ROW_TILE, r)
    while r % tr or tr % 8:
        tr -= 1
    return tr


def _adamw_math(w, g, m, v):
    m = ADAM_B1 * m + (1.0 - ADAM_B1) * g
    v = ADAM_B2 * v + (1.0 - ADAM_B2) * (g * g)
    m_hat = m / (1.0 - ADAM_B1 ** ADAM_STEP)
    v_hat = v / (1.0 - ADAM_B2 ** ADAM_STEP)
    delta = -ADAM_LR * (m_hat / (jnp.sqrt(v_hat) + ADAM_EPS) + ADAM_WD * w)
    return delta, m, v


COL_TILE = 256


def _tile_2d(r, c):
    if r % 8 == 0:
        return _row_tile(r), c
    assert c % COL_TILE == 0, (r, c)
    return r, COL_TILE


def _half_shape(shape):
    r, c = shape[-2:]
    return tuple(shape[:-2]) + ((r // 2, c) if _half_axis(r) == 0 else (r, c // 2))


def _adam(g, w, m, v, *, name):
    r, c = w.shape
    tr, tc = _tile_2d(r, c)

    def body(g_ref, w_ref, m_ref, v_ref, d_ref, mo_ref, vo_ref):
        d, mn, vn = _adamw_math(w_ref[...], g_ref[...], m_ref[...], v_ref[...])
        d_ref[...] = d
        mo_ref[...] = mn
        vo_ref[...] = vn

    spec = pl.BlockSpec((tr, tc), lambda i, j: (i, j))
    return pl.pallas_call(
        body, grid=(r // tr, c // tc), in_specs=[spec] * 4, out_specs=[spec] * 3,
        out_shape=[jax.ShapeDtypeStruct((r, c), F32)] * 3,
        compiler_params=_cp("parallel", "parallel"), name=name,
    )(g, w, m, v)


def _ada_grad_adam(c_all_t, dmod_cols, w, m, v, *, name):
    r, c = w.shape
    tr, tc = min(512, r), min(1024, c)

    def body(ct_ref, dm_ref, w_ref, m_ref, v_ref, g_ref, d_ref, mo_ref, vo_ref):
        g = _dot_nn(ct_ref[...], dm_ref[...], HIGHEST)
        g_ref[...] = g
        d, mn, vn = _adamw_math(w_ref[...], g, m_ref[...], v_ref[...])
        d_ref[...] = d
        mo_ref[...] = mn
        vo_ref[...] = vn

    spec = pl.BlockSpec((tr, tc), lambda i, j: (i, j))
    nb = c_all_t.shape[1]
    return pl.pallas_call(
        body, grid=(r // tr, c // tc),
        in_specs=[pl.BlockSpec((tr, nb), lambda i, j: (i, 0)), pl.BlockSpec((nb, tc), lambda i, j: (0, j)),
                  spec, spec, spec],
        out_specs=[spec] * 4, out_shape=[jax.ShapeDtypeStruct((r, c), F32)] * 4,
        compiler_params=_cp("parallel", "parallel"), name=name,
    )(c_all_t, dmod_cols, w, m, v)


def _mod_shard(c_all, w, b, *, name):
    k, c = w.shape
    tc = min(512, c)
    nb = c_all.shape[0]

    def body(c_ref, w_ref, b_ref, o_ref):
        o_ref[...] = _dot_nn(c_ref[...], w_ref[...], HIGHEST) + b_ref[...]

    return pl.pallas_call(
        body, grid=(c // tc,),
        in_specs=[pl.BlockSpec((nb, k), lambda j: (0, 0)), pl.BlockSpec((k, tc), lambda j: (0, j)),
                  pl.BlockSpec((1, tc), lambda j: (0, j))],
        out_specs=pl.BlockSpec((nb, tc), lambda j: (0, j)),
        out_shape=jax.ShapeDtypeStruct((nb, c), F32),
        compiler_params=_cp("parallel"), name=name,
    )(c_all, w, b)


def _silu_rows(c, *, name):
    def body(c_ref, o_ref):
        cv = c_ref[...]
        o_ref[...] = cv * _sigmoid(cv)

    return pl.pallas_call(body, out_shape=jax.ShapeDtypeStruct(c.shape, F32), name=name)(c)


def _pair_sum(g, got, idx, *, name):
    p, r, c = g.shape
    ax = _half_axis(r)
    hr, hc = _half_shape((r, c))
    tr, tc = _tile_2d(hr, hc)
    nbr, nbc = hr // tr, hc // tc

    def body(idx_ref, a_ref, b_ref, o_ref):
        o_ref[...] = (a_ref[...].astype(F32) + b_ref[...].astype(F32)).astype(BF16)

    def own_map(i, j, k, idx_ref):
        return (i, j + (idx_ref[0] * nbr if ax == 0 else 0), k + (idx_ref[0] * nbc if ax == 1 else 0))

    half_spec = pl.BlockSpec((1, tr, tc), lambda i, j, k, idx_ref: (i, j, k))
    return pl.pallas_call(
        body,
        grid_spec=pltpu.PrefetchScalarGridSpec(
            num_scalar_prefetch=1, grid=(p, nbr, nbc),
            in_specs=[pl.BlockSpec((1, tr, tc), own_map), half_spec],
            out_specs=half_spec),
        out_shape=jax.ShapeDtypeStruct((p, hr, hc), BF16),
        compiler_params=_cp("parallel", "parallel", "parallel"), name=name,
    )(idx, g, got)


def _final_sum(own, parts, idx, shard_shape, *, name):
    ax = _half_axis(shard_shape[0])
    hr, hc = own.shape[1:]
    tr, tc = _tile_2d(hr, hc)
    nbr, nbc = hr // tr, hc // tc

    def body(idx_ref, own_ref, parts_ref, o_ref):
        acc = own_ref[0].astype(F32)
        for q in range(3):
            acc = acc + parts_ref[q].astype(F32)
        o_ref[...] = acc

    def out_map(j, k, idx_ref):
        return (j + (idx_ref[0] * nbr if ax == 0 else 0), k + (idx_ref[0] * nbc if ax == 1 else 0))

    return pl.pallas_call(
        body,
        grid_spec=pltpu.PrefetchScalarGridSpec(
            num_scalar_prefetch=1, grid=(nbr, nbc),
            in_specs=[pl.BlockSpec((1, tr, tc), lambda j, k, idx_ref: (idx_ref[1], j, k)),
                      pl.BlockSpec((3, tr, tc), lambda j, k, idx_ref: (0, j, k))],
            out_specs=pl.BlockSpec((tr, tc), out_map)),
        out_shape=jax.ShapeDtypeStruct(tuple(shard_shape), F32),
        compiler_params=_cp("parallel", "parallel"), name=name,
    )(idx, own, parts)


def _stack_sum(x, *, name):
    p, r, c = x.shape
    tr = _row_tile(r)

    def body(x_ref, o_ref):
        acc = x_ref[0].astype(F32)
        for q in range(1, p):
            acc = acc + x_ref[q].astype(F32)
        o_ref[...] = acc

    return pl.pallas_call(
        body, grid=(r // tr,),
        in_specs=[pl.BlockSpec((p, tr, c), lambda i: (0, i, 0))],
        out_specs=pl.BlockSpec((tr, c), lambda i: (i, 0)),
        out_shape=jax.ShapeDtypeStruct((r, c), F32),
        compiler_params=_cp("parallel"), name=name,
    )(x)


def _place():
    x, y, c = lax.axis_index("x"), lax.axis_index("y"), lax.axis_index("c")
    chips = [(1 - x, y), (x, 1 - y), (1 - x, 1 - y)]
    return x, y, c, chips


def _gather8(x_shard, *, name):
    m_per, n = x_shard.shape

    def body(x_ref, out_ref, send_sems, recv_sems, local_sem):
        x, y, c, chips = _place()
        me, sibling = (x, y, c), (x, y, 1 - c)

        def rows(px, py, pc):
            return out_ref.at[pl.ds((4 * px + 2 * py + pc) * m_per, m_per), :]

        def copy(k, block, to, src=None):
            return pltpu.make_async_remote_copy(
                src_ref=rows(*block) if src is None else src, dst_ref=rows(*block),
                send_sem=send_sems.at[k], recv_sem=recv_sems.at[k], device_id=to, device_id_type=MESH)

        mine = pltpu.make_async_copy(x_ref, rows(*me), local_sem)
        mine.start()
        first = [copy(0, me, sibling, src=x_ref)]
        first += [copy(1 + j, me, (*chip, c), src=x_ref) for j, chip in enumerate(chips)]
        for cp in first:
            cp.start()
        passed = [copy(4 + j, (*chip, c), sibling) for j, chip in enumerate(chips)]
        for j, chip in enumerate(chips):
            copy(1 + j, (*chip, c), me).wait_recv()
            passed[j].start()
        copy(0, sibling, me).wait_recv()
        for j, chip in enumerate(chips):
            copy(4 + j, (*chip, 1 - c), me).wait_recv()
        for cp in first + passed:
            cp.wait_send()
        mine.wait()

    return pl.pallas_call(
        body,
        out_shape=jax.ShapeDtypeStruct((8 * m_per, n), x_shard.dtype),
        in_specs=[pl.BlockSpec(memory_space=pltpu.VMEM)],
        out_specs=pl.BlockSpec(memory_space=pltpu.VMEM),
        scratch_shapes=[pltpu.SemaphoreType.DMA((7,)), pltpu.SemaphoreType.DMA((7,)), pltpu.SemaphoreType.DMA],
        name=name,
    )(x_shard)


def _gather_weights(shards, *, name):
    return _comm_call(lambda ins, outs: [cp for i, o in zip(ins, outs) for cp in _plan_gather_ici(i, o)],
                      shards, [jax.ShapeDtypeStruct((4,) + s.shape, s.dtype) for s in shards], name=name)


def _plan_start(plan, send_sems, recv_sems):
    for k, (src, dst, _, peer) in enumerate(plan):
        pltpu.make_async_remote_copy(src_ref=src, dst_ref=dst, send_sem=send_sems.at[k], recv_sem=recv_sems.at[k],
                                     device_id=peer, device_id_type=MESH).start()


def _plan_wait(plan, send_sems, recv_sems):
    for k, (src, _, land, peer) in enumerate(plan):
        pltpu.make_async_remote_copy(src_ref=src, dst_ref=land, send_sem=send_sems.at[k], recv_sem=recv_sems.at[k],
                                     device_id=peer, device_id_type=MESH).wait_recv()
    for k, (src, dst, _, peer) in enumerate(plan):
        pltpu.make_async_remote_copy(src_ref=src, dst_ref=dst, send_sem=send_sems.at[k], recv_sem=recv_sems.at[k],
                                     device_id=peer, device_id_type=MESH).wait_send()


def _half_axis(rows):
    return 0 if rows % 32 == 0 else 1


def _rows_half(ref, hc, axis):
    hr = ref.shape[axis] // 2
    idx = [slice(None)] * len(ref.shape)
    idx[axis] = pl.ds(hc * hr, hr)
    return ref.at[tuple(idx)]


def _plan_gather_ici(shard, full):
    x, y, c, chips = _place()
    ax = _half_axis(shard.shape[0])
    src = _rows_half(shard, c, ax)
    return [(src, _rows_half(full.at[2 * x + y], c, ax), _rows_half(full.at[2 * cx + cy], c, ax), (cx, cy, c))
            for cx, cy in chips]


def _plan_gather_d2d(full):
    x, y, c, chips = _place()
    ax = _half_axis(full.shape[1])
    plan = []
    for cx, cy in chips:
        slot = full.at[2 * cx + cy]
        plan.append((_rows_half(slot, c, ax), _rows_half(slot, c, ax), _rows_half(slot, 1 - c, ax), (x, y, 1 - c)))
    return plan


def _plan_pair(grad, got):
    x, y, c, _ = _place()
    return [(_rows_half(grad, 1 - c, 1 + _half_axis(grad.shape[1])), got, got, (x, y, 1 - c))]


def _plan_shard_ici(sums, parts):
    _, _, c, chips = _place()
    return [(sums.at[2 * cx + cy], parts.at[k], parts.at[k], (cx, cy, c)) for k, (cx, cy) in enumerate(chips)]


def _plan_half(buf):
    x, y, c, _ = _place()
    ax = _half_axis(buf.shape[0])
    mine = _rows_half(buf, c, ax)
    return [(mine, mine, _rows_half(buf, 1 - c, ax), (x, y, 1 - c))]


def _comm_call(plan_fn, inputs, out_shapes, *, name, aliases=None):
    ni, no = len(inputs), len(out_shapes)

    def body(*refs):
        plan = plan_fn(refs[:ni], refs[ni:ni + no])
        send_sems, recv_sems = refs[ni + no:]
        _plan_start(plan, send_sems, recv_sems)
        _plan_wait(plan, send_sems, recv_sems)

    any_spec = pl.BlockSpec(memory_space=pl.ANY)
    n_copies = 3 * max(ni, no)
    return pl.pallas_call(
        body, out_shape=list(out_shapes), in_specs=[any_spec] * ni, out_specs=[any_spec] * no,
        scratch_shapes=[pltpu.SemaphoreType.DMA((n_copies,)), pltpu.SemaphoreType.DMA((n_copies,))],
        input_output_aliases=aliases or {}, name=name,
    )(*inputs)


def _gather_forward(fulls, *, name):
    return _comm_call(lambda ins, outs: [cp for o in outs for cp in _plan_gather_d2d(o)],
                      fulls, [jax.ShapeDtypeStruct(f.shape, f.dtype) for f in fulls], name=name,
                      aliases={k: k for k in range(len(fulls))})


def _pair_exchange(grads, *, name):
    return _comm_call(lambda ins, outs: [cp for i, o in zip(ins, outs) for cp in _plan_pair(i, o)],
                      grads, [jax.ShapeDtypeStruct(_half_shape(g.shape), g.dtype) for g in grads], name=name)


def _half_exchange(bufs, *, name):
    return _comm_call(lambda ins, outs: [cp for o in outs for cp in _plan_half(o)],
                      bufs, [jax.ShapeDtypeStruct(b.shape, b.dtype) for b in bufs], name=name,
                      aliases={k: k for k in range(len(bufs))})


def _split_w_in(w_in_t):
    d = w_in_t.shape[1]
    main = jnp.concatenate([w_in_t[0:3072], w_in_t[3080:5128], w_in_t[5144:6168]], axis=0)
    small = jnp.concatenate([w_in_t[3072:3080], w_in_t[5128:5144], jnp.zeros((SMALL_W - 24, d), w_in_t.dtype)], axis=0)
    return main, small


def _merge_dw_in(dw_main, dw_small):
    return jnp.concatenate([dw_main[0:3072], dw_small[0:8], dw_main[3072:5120], dw_small[8:24], dw_main[5120:6144]],
                           axis=0)


def _gather_side(shards):
    return _Side(shards, [jax.ShapeDtypeStruct((4,) + w.shape, w.dtype) for w in shards],
                 lambda ins, outs: [cp for i, o in zip(ins, outs) for cp in _plan_gather_ici(i, o)], 3 * len(shards))


def _finish_gather(fulls, owns, chip, *, name):
    fulls = _gather_forward(list(fulls), name=name)
    return [lax.dynamic_update_index_in_dim(f, o, chip, 0) for f, o in zip(fulls, owns)]


def _shard_side(sums):
    return _Side([sums], [jax.ShapeDtypeStruct((3,) + sums.shape[1:], sums.dtype)],
                 lambda ins, outs: _plan_shard_ici(ins[0], outs[0]), 3)


def _chip_sum(grad, idx, tag):
    got, = _pair_exchange([grad], name=f"grad_pair_exchange_{tag}")
    return _pair_sum(grad, got, idx, name=f"grad_pair_sum_{tag}")


def _local_step(x, target, mod, g_pre_mix, g_post_mix, g_pre_mlp, g_post_mlp, w_in_t, b_fgate, w_gla_a2,
                b_gla_a2, g_fox, g_gla, own_w_out, own_w_mlp_in, own_w_mlp_out, chip, idx):
    s, d = x.shape
    shift_m, scale_m, gate_m, shift_f, scale_f, gate_f = [mod[:, i * d:(i + 1) * d] for i in range(6)]
    a1 = g_pre_mix * (1.0 + scale_m)
    a2 = g_pre_mlp * (1.0 + scale_f)
    w_main, w_small = _split_w_in(w_in_t)
    bf = jnp.concatenate([b_fgate, jnp.zeros((1, SMALL_W - FOX_HEADS), F32)], axis=1)
    w2p = jnp.zeros((SMALL_W, GLA_KW), F32).at[FOX_HEADS:FOX_HEADS + GLA_RANK].set(w_gla_a2)

    h1 = _pre_fwd(x, a1, shift_m, name="pre_mix_fwd")
    proj, gw_out = _mm(h1, w_main, mode="nt", out_dtypes=[BF16], name="in_proj_main",
                       side=_gather_side([own_w_out]))
    ps, = _mm(h1, w_small, mode="nt", out_dtypes=[F32], name="in_proj_small")
    gw_out, = _finish_gather([gw_out], [own_w_out], chip, name="gather_w_out_d2d")
    w_out_full = gw_out.reshape(-1, d)
    cum, log_a = _gates_fwd(ps, bf, w2p, b_gla_a2, name="gates_fwd")
    cum_t = cum[:, :FOX_HEADS].T
    o_fox, fox_n, lse, gw_mlp_in, gw_mlp_out = _fox_fwd(proj, cum_t, g_fox, name="fox_fwd",
                                                        side=_gather_side([own_w_mlp_in, own_w_mlp_out]))
    gw_mlp_in, gw_mlp_out = _finish_gather([gw_mlp_in, gw_mlp_out], [own_w_mlp_in, own_w_mlp_out], chip,
                                           name="gather_w_mlp_d2d")
    w_mlp_out_full = gw_mlp_out.reshape(-1, d)
    o_gla, gla_n, states = _gla_fwd(proj, log_a, g_gla, name="gla_fwd")
    mixed = jnp.concatenate([fox_n, gla_n], axis=1)
    y1, = _mm(mixed, w_out_full, mode="nn", out_dtypes=[F32], name="out_proj")
    x1 = _post_fwd(x, y1, gate_m, g_post_mix, name="post_mix_fwd")
    h2 = _pre_fwd(x1, a2, shift_f, name="pre_mlp_fwd")

    def mlp_act(acc):
        r = jnp.maximum(acc, 0.0)
        return acc, r * r

    u, act = _mm(h2, gw_mlp_in, mode="nn", out_dtypes=[BF16, BF16], epi=mlp_act, name="mlp_in", b_slots=4)
    y2, = _mm(act, w_mlp_out_full, mode="nn", out_dtypes=[F32], name="mlp_out")
    dx2, loss_part = _post_fwd_loss(x1, y2, gate_f, g_post_mlp, target, name="post_mlp_fwd_loss")

    dy2, dgate_f, dg_post_mlp = _post_bwd(dx2, y2, gate_f, g_post_mlp, name="post_mlp_bwd")
    dw_mlp_out, = _mm(act, dy2, mode="tn", out_dtypes=[BF16], name="dw_mlp_out")
    sum_mlp_out = _chip_sum(dw_mlp_out.reshape(4, D_FF // 4, d), idx, "mlp_out")

    def act_bwd(acc, uv):
        return (acc * (2.0 * jnp.maximum(uv.astype(F32), 0.0)),)

    du, parts_mlp_out = _mm(dy2, w_mlp_out_full, mode="nt", out_dtypes=[BF16], extras=[u], epi=act_bwd,
                            name="d_mlp_hidden", side=_shard_side(sum_mlp_out))
    nj = D_FF // 4 // min(MM_T, D_FF // 4)
    tmw = min(MM_T, d)
    dw_mlp_in, = _mm(h2, du, mode="tn", out_dtypes=[BF16], name="dw_mlp_in",
                     out_shapes=[jax.ShapeDtypeStruct((4, d, D_FF // 4), BF16)],
                     out_specs=[pl.BlockSpec((1, tmw, min(MM_T, D_FF // 4)), lambda i, j, kk: (j // nj, i, j % nj))])
    sum_mlp_in = _chip_sum(dw_mlp_in, idx, "mlp_in")
    dh2, parts_mlp_in = _mm(du, gw_mlp_in, mode="nt", out_dtypes=[F32], name="d_mlp_in", b_slots=4,
                            side=_shard_side(sum_mlp_in))
    dx1, dshift_f, da2 = _pre_bwd(dh2, x1, dx2, a2, name="pre_mlp_bwd")

    dy1, dgate_m, dg_post_mix = _post_bwd(dx1, y1, gate_m, g_post_mix, name="post_mix_bwd")
    dw_out, = _mm(mixed, dy1, mode="tn", out_dtypes=[BF16], name="dw_out")
    sum_out = _chip_sum(dw_out.reshape(4, d // 4, d), idx, "out")
    dmixed, parts_out = _mm(dy1, w_out_full, mode="nt", out_dtypes=[BF16], name="d_mixed", side=_shard_side(sum_out))
    do_fox, delta, dg_fox = _head_norm_bwd(dmixed, o_fox, g_fox, None, nh=FOX_HEADS, hd=FOX_HD, dn_col=0,
                                           gr_col=0, name="fox_norm_bwd")
    do_gla, dgr, _, dg_gla = _head_norm_bwd(dmixed, o_gla, g_gla, proj, nh=GLA_HEADS, hd=GLA_DV, dn_col=1,
                                            gr_col=(3 * FOX_W + 2 * GLA_KW + GLA_W) // GLA_W, name="gla_norm_bwd")
    dq_fox, dk_fox, dv_fox, dcq, dck_t = _fox_bwd(proj, do_fox, cum_t, lse, delta, name="fox_bwd")
    dgq, dgk, dgv, dla = _gla_bwd(proj, log_a, do_gla, states, name="gla_bwd")
    dck = dcq + jnp.concatenate([dck_t.T, jnp.zeros((s, SMALL_W - FOX_HEADS), F32)], axis=1)
    dps, dbf, dw2p, db2 = _gates_bwd(dck, ps, bf, w2p, b_gla_a2, dla, name="gates_bwd")
    dproj = jnp.concatenate([dq_fox.astype(BF16), dk_fox, dv_fox, dgq, dgk, dgv, dgr], axis=1)
    dw_main, = _mm(dproj, h1, mode="tn", out_dtypes=[BF16], name="dw_in_main")
    dw_small, = _mm(dps, h1, mode="tn", out_dtypes=[BF16], name="dw_in_small")
    rs_in = w_in_t.shape[0] // 4
    dw_in = _merge_dw_in(dw_main, dw_small).reshape(4, rs_in, d)
    sum_in = _chip_sum(dw_in, idx, "in")
    dh1_small, = _mm(dps, w_small, mode="nn", out_dtypes=[F32], name="d_h1_small")
    dh1, parts_in = _mm(dproj, w_main, mode="nn", out_dtypes=[F32], extras=[dh1_small],
                        epi=lambda acc, e: (acc + e,), name="d_h1", side=_shard_side(sum_in))
    grad_x, dshift_m, da1 = _pre_bwd(dh1, x, dx1, a1, name="pre_mix_bwd")
    bufs = [_final_sum(sm, pt, idx, shp, name=f"grad_final_sum_{tag}")
            for tag, sm, pt, shp in [("in", sum_in, parts_in, (rs_in, d)), ("out", sum_out, parts_out, (d // 4, d)),
                                     ("mlp_in", sum_mlp_in, parts_mlp_in, (d, D_FF // 4)),
                                     ("mlp_out", sum_mlp_out, parts_mlp_out, (D_FF // 4, d))]]

    dmod = jnp.concatenate([dshift_m, da1 * g_pre_mix, dgate_m, dshift_f, da2 * g_pre_mlp, dgate_f], axis=1)
    small = dict(
        dmod=dmod, g_pre_mix=da1 * (1.0 + scale_m), g_post_mix=dg_post_mix, g_pre_mlp=da2 * (1.0 + scale_f),
        g_post_mlp=dg_post_mlp, b_fgate=dbf[:, :FOX_HEADS], w_gla_a2=dw2p[FOX_HEADS:FOX_HEADS + GLA_RANK],
        b_gla_a2=db2, g_fox_out=dg_fox, g_gla_out=dg_gla)
    return loss_part, grad_x, bufs, small


def _pack(arrays):
    flat = jnp.concatenate([a.reshape(-1).astype(F32) for a in arrays])
    n = flat.shape[0]
    rows = -(-n // 128)
    rows = -(-rows // 8) * 8
    return jnp.pad(flat, (0, rows * 128 - n)).reshape(rows, 128)


def _unpack(buf, shapes):
    flat = buf.reshape(-1)
    out, off = [], 0
    for shp in shapes:
        n = 1
        for q in shp:
            n *= q
        out.append(flat[off:off + n].reshape(shp))
        off += n
    return out


SMALL_GRAD_ORDER = ["dmod", "g_pre_mix", "g_post_mix", "g_pre_mlp", "g_post_mlp", "b_fgate", "w_gla_a2", "b_gla_a2",
                    "g_fox_out", "g_gla_out"]


def kernel(x, c, w_ada, b_ada, g_pre_mix, g_post_mix, w_in, b_fgate, w_gla_a2, b_gla_a2, g_fox_out, g_gla_out, w_out, g_pre_mlp, g_post_mlp, w_mlp_in, w_mlp_out, loss_target, m_w_ada, m_b_ada, m_g_pre_mix, m_g_post_mix, m_w_in, m_b_fgate, m_w_gla_a2, m_b_gla_a2, m_g_fox_out, m_g_gla_out, m_w_out, m_g_pre_mlp, m_g_post_mlp, m_w_mlp_in, m_w_mlp_out, v_w_ada, v_b_ada, v_g_pre_mix, v_g_post_mix, v_w_in, v_b_fgate, v_w_gla_a2, v_b_gla_a2, v_g_fox_out, v_g_gla_out, v_w_out, v_g_pre_mlp, v_g_post_mlp, v_w_mlp_in, v_w_mlp_out):
    ix, iy, ic = lax.axis_index("x"), lax.axis_index("y"), lax.axis_index("c")
    chip = 2 * ix + iy
    dev = 4 * ix + 2 * iy + ic
    d = D_MODEL

    c_act = _silu_rows(c, name="silu_c")
    pack1 = _pack([c_act, w_gla_a2[0], g_gla_out[0]])
    rows1 = pack1.shape[0]
    got1 = _gather8(pack1, name="gather_small_fwd").reshape(8, rows1, 128)
    per_dev = [_unpack(got1[q], [(d,), (GLA_RANK, GLA_KW // 4), (GLA_HEADS, GLA_DV // 4)]) for q in range(8)]
    c_all = jnp.stack([p[0] for p in per_dev])
    w_gla_a2_full = jnp.concatenate([per_dev[2 * j][1] for j in range(4)], axis=1)
    g_gla_full = jnp.concatenate([per_dev[2 * j][2] for j in range(4)], axis=1)
    cols = w_ada.shape[2]
    b_ada_shard = lax.dynamic_slice_in_dim(b_ada, chip * cols, cols, axis=1)
    mod_sh = _mod_shard(c_all, w_ada[0], b_ada_shard, name="ada_mod")
    got2 = _gather8(mod_sh, name="gather_mod").reshape(8, 8, cols)
    mod_all = jnp.concatenate([got2[2 * j] for j in range(4)], axis=1)
    mod = lax.dynamic_slice_in_dim(mod_all, dev, 1, axis=0)

    tr_in = lambda a: jnp.transpose(a[0])
    own_bf = [tr_in(w_in).astype(BF16), w_out[0].astype(BF16), w_mlp_in[0].astype(BF16), w_mlp_out[0].astype(BF16)]
    gw_in, = _finish_gather(_gather_weights(own_bf[:1], name="gather_w_in_ici"), own_bf[:1], chip,
                            name="gather_w_in_d2d")
    w_in_t = gw_in.reshape(-1, d)
    idx = jnp.stack([ic, chip]).astype(jnp.int32)

    loss_part, grad_x, bufs, small = _local_step(
        x[0], loss_target[0], mod, g_pre_mix, g_post_mix, g_pre_mlp, g_post_mlp, w_in_t, b_fgate,
        w_gla_a2_full, b_gla_a2, g_fox_out[0], g_gla_full, own_bf[1], own_bf[2], own_bf[3], chip, idx)
    loss = lax.psum(loss_part[0, 0], ("x", "y", "c"))

    g_big = _half_exchange(bufs, name="grad_half_exchange")
    big_w = [(tr_in(w_in), tr_in(m_w_in), tr_in(v_w_in)), (w_out[0], m_w_out[0], v_w_out[0]),
             (w_mlp_in[0], m_w_mlp_in[0], v_w_mlp_in[0]), (w_mlp_out[0], m_w_mlp_out[0], v_w_mlp_out[0])]
    big_res = []
    for q, (g, (w, m, v)) in enumerate(zip(g_big, big_w)):
        res4 = (g,) + tuple(_adam(g, w, m, v, name=f"adam_big_{q}"))
        big_res.append(tuple((jnp.transpose(a) if q == 0 else a)[None] for a in res4))

    pack2 = _pack([small[k] for k in SMALL_GRAD_ORDER])
    rows2 = pack2.shape[0]
    got3 = _gather8(pack2, name="gather_small_grads").reshape(8, rows2, 128)
    dmod_all = got3[:, :6 * d // 128, :].reshape(8, 6 * d)
    sums = _stack_sum(got3, name="small_grad_sum")
    shapes = [(1, 6 * d), (1, d), (1, d), (1, d), (1, d), (1, FOX_HEADS), (1, GLA_RANK, GLA_KW), (1, GLA_KW),
              (1, FOX_HEADS, FOX_HD), (1, GLA_HEADS, GLA_DV)]
    sg = dict(zip(["b_ada"] + SMALL_GRAD_ORDER[1:], _unpack(sums, shapes)))
    sg["w_gla_a2"] = lax.dynamic_slice_in_dim(sg["w_gla_a2"], chip * (GLA_KW // 4), GLA_KW // 4, axis=2)
    sg["g_gla_out"] = lax.dynamic_slice_in_dim(sg["g_gla_out"], chip * (GLA_DV // 4), GLA_DV // 4, axis=2)
    small_names = ["b_ada", "g_pre_mix", "g_post_mix", "b_fgate", "w_gla_a2", "b_gla_a2", "g_fox_out", "g_gla_out",
                   "g_pre_mlp", "g_post_mlp"]
    small_w = dict(b_ada=(b_ada, m_b_ada, v_b_ada), g_pre_mix=(g_pre_mix, m_g_pre_mix, v_g_pre_mix),
                   g_post_mix=(g_post_mix, m_g_post_mix, v_g_post_mix), b_fgate=(b_fgate, m_b_fgate, v_b_fgate),
                   w_gla_a2=(w_gla_a2, m_w_gla_a2, v_w_gla_a2), b_gla_a2=(b_gla_a2, m_b_gla_a2, v_b_gla_a2),
                   g_fox_out=(g_fox_out, m_g_fox_out, v_g_fox_out), g_gla_out=(g_gla_out, m_g_gla_out, v_g_gla_out),
                   g_pre_mlp=(g_pre_mlp, m_g_pre_mlp, v_g_pre_mlp), g_post_mlp=(g_post_mlp, m_g_post_mlp, v_g_post_mlp))
    sshapes = [small_w[k][0].shape for k in small_names]
    pg = _pack([sg[k] for k in small_names])
    pw, pm, pv = [_pack([small_w[k][q] for k in small_names]) for q in range(3)]
    pd, pmn, pvn = _adam(pg, pw, pm, pv, name="adam_small")
    s_delta = dict(zip(small_names, _unpack(pd, sshapes)))
    s_m = dict(zip(small_names, _unpack(pmn, sshapes)))
    s_v = dict(zip(small_names, _unpack(pvn, sshapes)))

    dmod_cols = lax.dynamic_slice_in_dim(dmod_all, chip * cols, cols, axis=1)
    g_ada, d_ada, m_ada, v_ada = _ada_grad_adam(c_all.T, dmod_cols, w_ada[0], m_w_ada[0], v_w_ada[0], name="ada_grad_adam")

    order = ["w_ada", "b_ada", "g_pre_mix", "g_post_mix", "w_in", "b_fgate", "w_gla_a2", "b_gla_a2", "g_fox_out",
             "g_gla_out", "w_out", "g_pre_mlp", "g_post_mlp", "w_mlp_in", "w_mlp_out"]
    res = {"w_ada": (g_ada[None], d_ada[None], m_ada[None], v_ada[None]),
           "w_in": big_res[0], "w_out": big_res[1], "w_mlp_in": big_res[2], "w_mlp_out": big_res[3]}
    for k in small_names:
        res[k] = (sg[k], s_delta[k], s_m[k], s_v[k])
    return (loss, grad_x[None], *[res[k][0] for k in order], *[res[k][1] for k in order],
            *[res[k][2] for k in order], *[res[k][3] for k in order])
```

```python
import functools

import jax
import jax.numpy as jnp
from jax import lax
from jax.experimental import pallas as pl
from jax.experimental.pallas import tpu as pltpu

F32 = jnp.float32
BF16 = jnp.bfloat16
MESH = pl.DeviceIdType.MESH
HIGHEST = lax.Precision.HIGHEST

D_MODEL = 2048
FOX_HEADS = 8
FOX_HD = 128
FOX_W = FOX_HEADS * FOX_HD
GLA_HEADS = 4
GLA_DK = 128
GLA_DV = 256
GLA_KW = GLA_HEADS * GLA_DK
GLA_W = GLA_HEADS * GLA_DV
GLA_RANK = 16
GLA_TEMP = 16.0
CHUNK = 64
D_FF = 4 * D_MODEL
EPS = 1e-6
MAIN_W = 3 * FOX_W + 2 * GLA_KW + 2 * GLA_W
SMALL_W = 128
NEG = -1e30

ADAM_LR = 0.001
ADAM_B1 = 0.9
ADAM_B2 = 0.999
ADAM_EPS = 1e-08
ADAM_WD = 0.01
ADAM_STEP = 10

VMEM_LIMIT = 52 * 1024 * 1024
ROW_TILE = 256
FOX_TQ = 512
FOX_TK = 512
GLA_ROWS = 512
GATE_TS = 512
MM_T = 1024
MM_TK = 2048
MM_TM = 2048


def _cp(*sem):
    return pltpu.CompilerParams(dimension_semantics=sem, vmem_limit_bytes=VMEM_LIMIT)


def _dot_nn(a, b, precision=None):
    return jnp.dot(a, b, preferred_element_type=F32, precision=precision)


def _dot_nt(a, b, precision=None):
    return lax.dot_general(a, b, (((1,), (1,)), ((), ())), preferred_element_type=F32, precision=precision)


def _dot_tn(a, b, precision=None):
    return lax.dot_general(a, b, (((0,), (0,)), ((), ())), preferred_element_type=F32, precision=precision)


def _sigmoid(x):
    return 1.0 / (1.0 + jnp.exp(-x))


def _log_sigmoid(x):
    return jnp.minimum(x, 0.0) - jnp.log(1.0 + jnp.exp(-jnp.abs(x)))


class _Side:
    def __init__(self, inputs, out_shapes, plan_fn, n_copies):
        self.inputs, self.out_shapes, self.plan_fn, self.n_copies = list(inputs), list(out_shapes), plan_fn, n_copies

    def scratch(self):
        return [pltpu.SemaphoreType.DMA((self.n_copies,)), pltpu.SemaphoreType.DMA((self.n_copies,))]

    def run(self, in_refs, out_refs, sems, first, last):
        @pl.when(first)
        def _():
            _plan_start(self.plan_fn(in_refs, out_refs), *sems)

        @pl.when(last)
        def _():
            _plan_wait(self.plan_fn(in_refs, out_refs), *sems)


def _mm(a, b, *, mode, out_dtypes, name, tm=None, tn=None, tk=None, extras=(), epi=None,
        out_shapes=None, out_specs=None, side=None, b_slots=0):
    tm, tn, tk = tm or MM_T, tn or MM_T, tk or MM_TK
    b2 = (b.shape[1], b_slots * b.shape[2]) if b_slots else b.shape
    if mode == "nn":
        (m, k), n = a.shape, b2[1]
    elif mode == "nt":
        (m, k), n = a.shape, b2[0]
    else:
        (k, m), n = a.shape, b2[1]
    tm, tn, tk = min(tm, m), min(tn, n), min(tk, k)
    if b_slots:
        tn = min(tn, b.shape[2]) if mode == "nn" else tn
        tk = min(tk, b.shape[2]) if mode == "nt" else tk
    assert m % tm == 0 and n % tn == 0 and k % tk == 0, (name, m, n, k)
    nk = k // tk
    n_out, n_ex = len(out_dtypes), len(extras)
    if epi is None:
        epi = lambda acc: tuple(acc for _ in range(n_out))
    dot = {"nn": _dot_nn, "nt": _dot_nt, "tn": _dot_tn}[mode]

    n_si = len(side.inputs) if side else 0
    n_so = len(side.out_shapes) if side else 0
    grid = (m // tm, n // tn, nk)

    def body(*refs):
        a_ref, b_ref = refs[0], refs[1]
        ex_refs = refs[2:2 + n_ex]
        base = 2 + n_ex + n_si
        o_refs = refs[base:base + n_out]
        scratch = refs[base + n_out + n_so:]
        if side:
            pos = [pl.program_id(q) for q in range(3)]
            first = (pos[0] == 0) & (pos[1] == 0) & (pos[2] == 0)
            last = (pos[0] == grid[0] - 1) & (pos[1] == grid[1] - 1) & (pos[2] == grid[2] - 1)
            side.run(refs[2 + n_ex:base], refs[base + n_out:base + n_out + n_so], scratch[-2:], first, last)
        part = dot(a_ref[...], b_ref[...])

        def finish(acc):
            outs = epi(acc, *[e[...] for e in ex_refs])
            for o_ref, val in zip(o_refs, outs):
                o_ref[...] = val.reshape(o_ref.shape).astype(o_ref.dtype)

        if nk == 1:
            finish(part)
        else:
            acc_ref = scratch[0]
            kk = pl.program_id(2)

            @pl.when(kk == 0)
            def _():
                acc_ref[...] = part

            @pl.when(kk > 0)
            def _():
                acc_ref[...] += part

            @pl.when(kk == nk - 1)
            def _():
                finish(acc_ref[...])

    if mode == "nn":
        a_spec = pl.BlockSpec((tm, tk), lambda i, j, kk: (i, kk))
        b_spec = pl.BlockSpec((tk, tn), lambda i, j, kk: (kk, j))
        if b_slots:
            per = b.shape[2] // tn
            b_spec = pl.BlockSpec((None, tk, tn), lambda i, j, kk: (j // per, kk, j % per))
    elif mode == "nt":
        a_spec = pl.BlockSpec((tm, tk), lambda i, j, kk: (i, kk))
        b_spec = pl.BlockSpec((tn, tk), lambda i, j, kk: (j, kk))
        if b_slots:
            per = b.shape[2] // tk
            b_spec = pl.BlockSpec((None, tn, tk), lambda i, j, kk: (kk // per, j, kk % per))
    else:
        assert not b_slots
        a_spec = pl.BlockSpec((tk, tm), lambda i, j, kk: (kk, i))
        b_spec = pl.BlockSpec((tk, tn), lambda i, j, kk: (kk, j))
    tile_spec = pl.BlockSpec((tm, tn), lambda i, j, kk: (i, j))
    if out_shapes is None:
        out_shapes = [jax.ShapeDtypeStruct((m, n), dt) for dt in out_dtypes]
    if out_specs is None:
        out_specs = [tile_spec for _ in out_dtypes]
    any_spec = pl.BlockSpec(memory_space=pl.ANY)
    res = pl.pallas_call(
        body,
        grid=grid,
        in_specs=[a_spec, b_spec] + [tile_spec for _ in extras] + [any_spec] * n_si,
        out_specs=list(out_specs) + [any_spec] * n_so,
        out_shape=list(out_shapes) + (side.out_shapes if side else []),
        scratch_shapes=([pltpu.VMEM((tm, tn), F32)] if nk > 1 else []) + (side.scratch() if side else []),
        compiler_params=_cp("arbitrary", "arbitrary", "arbitrary") if side else _cp("parallel", "parallel", "arbitrary"),
        name=name,
    )(a, b, *extras, *(side.inputs if side else []))
    return res


def _row_spec(ts, d):
    return pl.BlockSpec((ts, d), lambda i: (i, 0))


def _vec_spec(d):
    return pl.BlockSpec((1, d), lambda i: (0, 0))


def _pre_fwd(x, avec, shift, *, name):
    s, d = x.shape
    ts = min(ROW_TILE, s)

    def body(x_ref, a_ref, s_ref, h_ref):
        xv = x_ref[...]
        r = lax.rsqrt(jnp.mean(xv * xv, axis=-1, keepdims=True) + EPS)
        h_ref[...] = (xv * r * a_ref[...] + s_ref[...]).astype(BF16)

    return pl.pallas_call(
        body, grid=(s // ts,),
        in_specs=[_row_spec(ts, d), _vec_spec(d), _vec_spec(d)],
        out_specs=_row_spec(ts, d),
        out_shape=jax.ShapeDtypeStruct((s, d), BF16),
        compiler_params=_cp("parallel"), name=name,
    )(x, avec, shift)


def _post_pre_fwd(x, y, gate, g, avec, shift, *, name):
    s, d = x.shape
    ts = min(ROW_TILE, s)

    def body(x_ref, y_ref, gate_ref, g_ref, a_ref, s_ref, o_ref, h_ref):
        yv = y_ref[...]
        r = lax.rsqrt(jnp.mean(yv * yv, axis=-1, keepdims=True) + EPS)
        x1 = x_ref[...] + gate_ref[...] * (yv * r * g_ref[...])
        o_ref[...] = x1
        r1 = lax.rsqrt(jnp.mean(x1 * x1, axis=-1, keepdims=True) + EPS)
        h_ref[...] = (x1 * r1 * a_ref[...] + s_ref[...]).astype(BF16)

    return pl.pallas_call(
        body, grid=(s // ts,),
        in_specs=[_row_spec(ts, d), _row_spec(ts, d)] + [_vec_spec(d)] * 4,
        out_specs=[_row_spec(ts, d), _row_spec(ts, d)],
        out_shape=[jax.ShapeDtypeStruct((s, d), F32), jax.ShapeDtypeStruct((s, d), BF16)],
        compiler_params=_cp("parallel"), name=name,
    )(x, y, gate, g, avec, shift)


def _post_bwd_math(dxv, yv, gatev, gv):
    r = lax.rsqrt(jnp.mean(yv * yv, axis=-1, keepdims=True) + EPS)
    yhat = yv * r
    dn = dxv * gatev
    dyhat = dn * gv
    dy = r * (dyhat - yhat * jnp.mean(dyhat * yhat, axis=-1, keepdims=True))
    return dy, dxv * (yhat * gv), dn * yhat


def _accumulate(first, pairs):
    @pl.when(first)
    def _():
        for ref, _ in pairs:
            ref[...] = jnp.zeros_like(ref)

    for ref, val in pairs:
        ref[...] += jnp.sum(val, axis=0, keepdims=True)


def _post_loss_bwd(x, y, gate, g, target, *, name):
    s, d = x.shape
    ts = min(ROW_TILE, s)

    def body(x_ref, y_ref, gate_ref, g_ref, t_ref, dx_ref, dy_ref, loss_ref, dgate_ref, dg_ref):
        yv, gatev, gv = y_ref[...], gate_ref[...], g_ref[...]
        r = lax.rsqrt(jnp.mean(yv * yv, axis=-1, keepdims=True) + EPS)
        diff = x_ref[...] + gatev * (yv * r * gv) - t_ref[...]
        dxv = diff * (1.0 / d)
        dx_ref[...] = dxv
        dy, dgate_rows, dg_rows = _post_bwd_math(dxv, yv, gatev, gv)
        dy_ref[...] = dy.astype(BF16)
        first = pl.program_id(0) == 0
        _accumulate(first, [(dgate_ref, dgate_rows), (dg_ref, dg_rows)])

        @pl.when(first)
        def _():
            loss_ref[...] = jnp.zeros_like(loss_ref)

        loss_ref[...] += jnp.sum(jnp.mean(diff * diff, axis=-1, keepdims=True)) * 0.5

    return pl.pallas_call(
        body, grid=(s // ts,),
        in_specs=[_row_spec(ts, d), _row_spec(ts, d), _vec_spec(d), _vec_spec(d), _row_spec(ts, d)],
        out_specs=[_row_spec(ts, d), _row_spec(ts, d), pl.BlockSpec((1, 128), lambda i: (0, 0)), _vec_spec(d),
                   _vec_spec(d)],
        out_shape=[jax.ShapeDtypeStruct((s, d), F32), jax.ShapeDtypeStruct((s, d), BF16),
                   jax.ShapeDtypeStruct((1, 128), F32), jax.ShapeDtypeStruct((1, d), F32),
                   jax.ShapeDtypeStruct((1, d), F32)],
        compiler_params=_cp("arbitrary"), name=name,
    )(x, y, gate, g, target)


def _pre_post_bwd(dh, xin, dres, avec, y, gate, g, *, name):
    s, d = xin.shape
    ts = min(ROW_TILE, s)

    def body(dh_ref, x_ref, dres_ref, a_ref, y_ref, gate_ref, g_ref, dx_ref, dshift_ref, da_ref, dy_ref,
             dgate_ref, dg_ref):
        xv, dhv = x_ref[...], dh_ref[...]
        r = lax.rsqrt(jnp.mean(xv * xv, axis=-1, keepdims=True) + EPS)
        xhat = xv * r
        dxhat = dhv * a_ref[...]
        dxv = dres_ref[...] + r * (dxhat - xhat * jnp.mean(dxhat * xhat, axis=-1, keepdims=True))
        dx_ref[...] = dxv
        dy, dgate_rows, dg_rows = _post_bwd_math(dxv, y_ref[...], gate_ref[...], g_ref[...])
        dy_ref[...] = dy.astype(BF16)
        _accumulate(pl.program_id(0) == 0, [(dshift_ref, dhv), (da_ref, dhv * xhat), (dgate_ref, dgate_rows),
                                            (dg_ref, dg_rows)])

    return pl.pallas_call(
        body, grid=(s // ts,),
        in_specs=[_row_spec(ts, d), _row_spec(ts, d), _row_spec(ts, d), _vec_spec(d), _row_spec(ts, d),
                  _vec_spec(d), _vec_spec(d)],
        out_specs=[_row_spec(ts, d), _vec_spec(d), _vec_spec(d), _row_spec(ts, d), _vec_spec(d), _vec_spec(d)],
        out_shape=[jax.ShapeDtypeStruct((s, d), F32), jax.ShapeDtypeStruct((1, d), F32),
                   jax.ShapeDtypeStruct((1, d), F32), jax.ShapeDtypeStruct((s, d), BF16),
                   jax.ShapeDtypeStruct((1, d), F32), jax.ShapeDtypeStruct((1, d), F32)],
        compiler_params=_cp("arbitrary"), name=name,
    )(dh, xin, dres, avec, y, gate, g)


def _pre_bwd(dh, xin, dres, avec, *, name):
    s, d = xin.shape
    ts = min(ROW_TILE, s)

    def body(dh_ref, x_ref, dres_ref, a_ref, dx_ref, dshift_ref, da_ref):
        xv, dhv = x_ref[...], dh_ref[...]
        r = lax.rsqrt(jnp.mean(xv * xv, axis=-1, keepdims=True) + EPS)
        xhat = xv * r
        dxhat = dhv * a_ref[...]
        dx_ref[...] = dres_ref[...] + r * (dxhat - xhat * jnp.mean(dxhat * xhat, axis=-1, keepdims=True))

        @pl.when(pl.program_id(0) == 0)
        def _():
            dshift_ref[...] = jnp.zeros_like(dshift_ref)
            da_ref[...] = jnp.zeros_like(da_ref)

        dshift_ref[...] += jnp.sum(dhv, axis=0, keepdims=True)
        da_ref[...] += jnp.sum(dhv * xhat, axis=0, keepdims=True)

    return pl.pallas_call(
        body, grid=(s // ts,),
        in_specs=[_row_spec(ts, d), _row_spec(ts, d), _row_spec(ts, d), _vec_spec(d)],
        out_specs=[_row_spec(ts, d), _vec_spec(d), _vec_spec(d)],
        out_shape=[jax.ShapeDtypeStruct((s, d), F32), jax.ShapeDtypeStruct((1, d), F32),
                   jax.ShapeDtypeStruct((1, d), F32)],
        compiler_params=_cp("arbitrary"), name=name,
    )(dh, xin, dres, avec)


def _tri(n, strict=False, upper=False):
    r = lax.broadcasted_iota(jnp.int32, (n, n), 0)
    c = lax.broadcasted_iota(jnp.int32, (n, n), 1)
    if upper:
        r, c = c, r
    return ((r > c) if strict else (r >= c)).astype(F32)


def _gates_fwd(ps, bf, w2p, b2, *, name):
    s = ps.shape[0]
    ts = min(GATE_TS, s)

    def body(ps_ref, bf_ref, w_ref, b2_ref, cum_ref, la_ref, carry_ref):
        @pl.when(pl.program_id(0) == 0)
        def _():
            carry_ref[...] = jnp.zeros_like(carry_ref)

        psv = ps_ref[...]
        lf = _log_sigmoid(psv + bf_ref[...])
        cum = _dot_nn(_tri(ts), lf, HIGHEST) + carry_ref[...]
        cum_ref[...] = cum
        carry_ref[...] = cum[ts - 1:ts, :]
        z = _dot_nn(psv, w_ref[...], HIGHEST) + b2_ref[...]
        la_ref[...] = _log_sigmoid(z) * (1.0 / GLA_TEMP)

    return pl.pallas_call(
        body, grid=(s // ts,),
        in_specs=[_row_spec(ts, SMALL_W), _vec_spec(SMALL_W),
                  pl.BlockSpec((SMALL_W, GLA_KW), lambda i: (0, 0)), _vec_spec(GLA_KW)],
        out_specs=[_row_spec(ts, SMALL_W), _row_spec(ts, GLA_KW)],
        out_shape=[jax.ShapeDtypeStruct((s, SMALL_W), F32), jax.ShapeDtypeStruct((s, GLA_KW), F32)],
        scratch_shapes=[pltpu.VMEM((1, SMALL_W), F32)],
        compiler_params=_cp("arbitrary"), name=name,
    )(ps, bf, w2p, b2)


def _gates_bwd(dck, ps, bf, w2p, b2, dla, *, name):
    s = ps.shape[0]
    ts = min(GATE_TS, s)
    nb = s // ts
    rev = lambda i: (nb - 1 - i, 0)

    def body(dck_ref, ps_ref, bf_ref, w_ref, b2_ref, dla_ref, dps_ref, dbf_ref, dw_ref, db2_ref, carry_ref):
        @pl.when(pl.program_id(0) == 0)
        def _():
            carry_ref[...] = jnp.zeros_like(carry_ref)
            dbf_ref[...] = jnp.zeros_like(dbf_ref)
            dw_ref[...] = jnp.zeros_like(dw_ref)
            db2_ref[...] = jnp.zeros_like(db2_ref)

        psv, dckv = ps_ref[...], dck_ref[...]
        dlf = _dot_nn(_tri(ts, upper=True), dckv, HIGHEST) + carry_ref[...]
        carry_ref[...] += jnp.sum(dckv, axis=0, keepdims=True)
        lane = lax.broadcasted_iota(jnp.int32, (ts, SMALL_W), 1)
        dff = jnp.where(lane < FOX_HEADS, dlf * _sigmoid(-(psv + bf_ref[...])), 0.0)
        z = _dot_nn(psv, w_ref[...], HIGHEST) + b2_ref[...]
        dz = dla_ref[...] * _sigmoid(-z) * (1.0 / GLA_TEMP)
        dps_ref[...] = (_dot_nt(dz, w_ref[...], HIGHEST) + dff).astype(BF16)
        dbf_ref[...] += jnp.sum(dff, axis=0, keepdims=True)
        dw_ref[...] += _dot_tn(psv, dz, HIGHEST)
        db2_ref[...] += jnp.sum(dz, axis=0, keepdims=True)

    return pl.pallas_call(
        body, grid=(nb,),
        in_specs=[pl.BlockSpec((ts, SMALL_W), rev), pl.BlockSpec((ts, SMALL_W), rev), _vec_spec(SMALL_W),
                  pl.BlockSpec((SMALL_W, GLA_KW), lambda i: (0, 0)), _vec_spec(GLA_KW),
                  pl.BlockSpec((ts, GLA_KW), rev)],
        out_specs=[pl.BlockSpec((ts, SMALL_W), rev), _vec_spec(SMALL_W),
                   pl.BlockSpec((SMALL_W, GLA_KW), lambda i: (0, 0)), _vec_spec(GLA_KW)],
        out_shape=[jax.ShapeDtypeStruct((s, SMALL_W), BF16), jax.ShapeDtypeStruct((1, SMALL_W), F32),
                   jax.ShapeDtypeStruct((SMALL_W, GLA_KW), F32), jax.ShapeDtypeStruct((1, GLA_KW), F32)],
        scratch_shapes=[pltpu.VMEM((1, SMALL_W), F32)],
        compiler_params=_cp("arbitrary"), name=name,
    )(dck, ps, bf, w2p, b2, dla)


def _hs(h, hd=FOX_HD):
    return slice(h * hd, (h + 1) * hd)


def _fox_fwd(proj, cum_t, g_fox, *, name, side=None):
    s = proj.shape[0]
    tq, tk = min(FOX_TQ, s), min(FOX_TK, s)
    scale = FOX_HD ** -0.5
    n_si = len(side.inputs) if side else 0
    n_so = len(side.out_shapes) if side else 0
    grid = (s // tq, s // tk)

    def body(*refs):
        q_ref, k_ref, v_ref, ck_ref, g_ref = refs[:5]
        o_ref, n_ref, lse_ref = refs[5 + n_si:8 + n_si]
        m_sc, acc_sc = refs[8 + n_si + n_so:10 + n_si + n_so]
        i, j = pl.program_id(0), pl.program_id(1)
        if side:
            side.run(refs[5:5 + n_si], refs[8 + n_si:8 + n_si + n_so], refs[10 + n_si + n_so:],
                     (i == 0) & (j == 0), (i == grid[0] - 1) & (j == grid[1] - 1))

        @pl.when(j == 0)
        def _():
            m_sc[...] = jnp.full_like(m_sc, NEG)
            acc_sc[...] = jnp.zeros_like(acc_sc)

        def block(masked):
            mask = _causal_mask(i, j, tq, tk) if masked else None
            ones = jnp.ones((tk, FOX_HD), BF16)
            for h in range(FOX_HEADS):
                sc = _fox_logits(_dot_nt(q_ref[:, _hs(h)], k_ref[:, _hs(h)]), ck_ref[h:h + 1, :], mask, scale)
                m_prev = m_sc[h]
                m_new = jnp.maximum(m_prev, jnp.max(sc, axis=-1, keepdims=True))
                alpha = jnp.exp(m_prev - m_new)
                p = jnp.exp(sc - m_new).astype(BF16)
                v_one = jnp.concatenate([v_ref[:, _hs(h)], ones], axis=1)
                acc_sc[:, _hs(h, 2 * FOX_HD)] = alpha * acc_sc[:, _hs(h, 2 * FOX_HD)] + _dot_nn(p, v_one)
                m_sc[h] = m_new

        pl.when(j < i)(functools.partial(block, False))

        @pl.when(j == i)
        def _():
            block(True)
            lane = lax.broadcasted_iota(jnp.int32, (tq, 128), 1)
            lse = jnp.zeros((tq, 128), F32)
            for h in range(FOX_HEADS):
                l_rep = acc_sc[:, 2 * h * FOX_HD + FOX_HD:2 * (h + 1) * FOX_HD]
                o = acc_sc[:, 2 * h * FOX_HD:2 * h * FOX_HD + FOX_HD] / l_rep
                o_ref[:, _hs(h)] = o
                r = lax.rsqrt(jnp.mean(o * o, axis=-1, keepdims=True) + EPS)
                n_ref[:, _hs(h)] = (o * r * g_ref[h:h + 1, :]).astype(BF16)
                lse = jnp.where(lane == h, m_sc[h] + jnp.log(l_rep), lse)
            lse_ref[...] = lse

    kv = lambda col: (lambda i, j: (jnp.minimum(j, i), col))
    any_spec = pl.BlockSpec(memory_space=pl.ANY)
    return pl.pallas_call(
        body, grid=grid,
        in_specs=[pl.BlockSpec((tq, FOX_W), lambda i, j: (i, 0)),
                  pl.BlockSpec((tk, FOX_W), kv(1)),
                  pl.BlockSpec((tk, FOX_W), kv(2)),
                  pl.BlockSpec((FOX_HEADS, tk), lambda i, j: (0, jnp.minimum(j, i))),
                  pl.BlockSpec((FOX_HEADS, FOX_HD), lambda i, j: (0, 0))] + [any_spec] * n_si,
        out_specs=[pl.BlockSpec((tq, FOX_W), lambda i, j: (i, 0)),
                   pl.BlockSpec((tq, FOX_W), lambda i, j: (i, 0)),
                   pl.BlockSpec((tq, 128), lambda i, j: (i, 0))] + [any_spec] * n_so,
        out_shape=[jax.ShapeDtypeStruct((s, FOX_W), F32), jax.ShapeDtypeStruct((s, FOX_W), BF16),
                   jax.ShapeDtypeStruct((s, 128), F32)] + (side.out_shapes if side else []),
        scratch_shapes=[pltpu.VMEM((FOX_HEADS, tq, 1), F32), pltpu.VMEM((tq, 2 * FOX_W), F32)]
        + (side.scratch() if side else []),
        compiler_params=_cp("arbitrary", "arbitrary"), name=name,
    )(proj, proj, proj, cum_t, g_fox, *(side.inputs if side else []))


def _causal_mask(i, j, tq, tk):
    rows = i * tq + lax.broadcasted_iota(jnp.int32, (tq, tk), 0)
    cols = j * tk + lax.broadcasted_iota(jnp.int32, (tq, tk), 1)
    return rows >= cols


def _fox_logits(qk, ck, mask, scale):
    sc = qk * scale - ck
    return sc if mask is None else jnp.where(mask, sc, NEG)


def _fox_bwd(proj, do, cum_t, lse, delta, *, name):
    s = proj.shape[0]
    tq, tk = min(FOX_TQ, s), min(FOX_TK, s)
    nk, nq = s // tk, s // tq
    scale = FOX_HD ** -0.5

    def body(q_ref, k_ref, v_ref, do_ref, ck_ref, lse_ref, dl_ref, dq_hbm, dk_ref, dv_ref, dcq_hbm, dck_ref,
             dq_sc, dcq_sc, dk_sc, dv_sc, dck_sc, out_sems):
        j, i = pl.program_id(0), pl.program_id(1)

        @pl.when((j == 0) & (i == 0))
        def _():
            dq_sc[...] = jnp.zeros_like(dq_sc)
            dcq_sc[...] = jnp.zeros_like(dcq_sc)

        @pl.when(i == 0)
        def _():
            dk_sc[...] = jnp.zeros_like(dk_sc)
            dv_sc[...] = jnp.zeros_like(dv_sc)
            dck_sc[...] = jnp.zeros_like(dck_sc)

        def block(masked):
            mask = _causal_mask(i, j, tq, tk) if masked else None
            qrows = pl.ds(pl.multiple_of(i * tq, tq), tq)
            for h in range(FOX_HEADS):
                sc = _fox_logits(_dot_nt(q_ref[:, _hs(h)], k_ref[:, _hs(h)]), ck_ref[h:h + 1, :], mask, scale)
                p = jnp.exp(sc - lse_ref[:, h:h + 1])
                ds = p * (_dot_nt(do_ref[:, _hs(h)], v_ref[:, _hs(h)]) - dl_ref[:, h:h + 1])
                dsb = ds.astype(BF16)
                dv_sc[:, _hs(h)] += _dot_tn(p.astype(BF16), do_ref[:, _hs(h)])
                dk_sc[:, _hs(h)] += _dot_tn(dsb, q_ref[:, _hs(h)])
                dq_sc[qrows, _hs(h)] += _dot_nn(dsb, k_ref[:, _hs(h)]) * scale
                dck_sc[h:h + 1, :] -= jnp.sum(ds, axis=0, keepdims=True)
                dcq_sc[qrows, h:h + 1] += jnp.sum(ds, axis=-1, keepdims=True)

        pl.when(i > j)(functools.partial(block, False))
        pl.when(i == j)(functools.partial(block, True))

        @pl.when(i == nq - 1)
        def _():
            dk_ref[...] = (dk_sc[...] * scale).astype(BF16)
            dv_ref[...] = dv_sc[...].astype(BF16)
            dck_ref[...] = dck_sc[...]

        @pl.when((j == nk - 1) & (i == nq - 1))
        def _():
            out_q = pltpu.make_async_copy(dq_sc, dq_hbm, out_sems.at[0])
            out_c = pltpu.make_async_copy(dcq_sc, dcq_hbm, out_sems.at[1])
            out_q.start()
            out_c.start()
            out_q.wait()
            out_c.wait()

    qrow = lambda j, i: (jnp.maximum(i, j), 0)
    krow = lambda col: (lambda j, i: (j, col))
    any_spec = pl.BlockSpec(memory_space=pl.ANY)
    return pl.pallas_call(
        body, grid=(nk, nq),
        in_specs=[pl.BlockSpec((tq, FOX_W), qrow), pl.BlockSpec((tk, FOX_W), krow(1)),
                  pl.BlockSpec((tk, FOX_W), krow(2)),
                  pl.BlockSpec((tq, FOX_W), qrow),
                  pl.BlockSpec((FOX_HEADS, tk), lambda j, i: (0, j)),
                  pl.BlockSpec((tq, 128), qrow), pl.BlockSpec((tq, 128), qrow)],
        out_specs=[any_spec, pl.BlockSpec((tk, FOX_W), lambda j, i: (j, 0)),
                   pl.BlockSpec((tk, FOX_W), lambda j, i: (j, 0)), any_spec,
                   pl.BlockSpec((FOX_HEADS, tk), lambda j, i: (0, j))],
        out_shape=[jax.ShapeDtypeStruct((s, FOX_W), F32), jax.ShapeDtypeStruct((s, FOX_W), BF16),
                   jax.ShapeDtypeStruct((s, FOX_W), BF16), jax.ShapeDtypeStruct((s, 128), F32),
                   jax.ShapeDtypeStruct((FOX_HEADS, s), F32)],
        scratch_shapes=[pltpu.VMEM((s, FOX_W), F32), pltpu.VMEM((s, 128), F32),
                        pltpu.VMEM((tk, FOX_W), F32), pltpu.VMEM((tk, FOX_W), F32), pltpu.VMEM((FOX_HEADS, tk), F32),
                        pltpu.SemaphoreType.DMA((2,))],
        compiler_params=_cp("arbitrary", "arbitrary"), name=name,
    )(proj, proj, proj, do, cum_t, lse, delta)


def _head_norm_bwd(dn_in, o, g, gr_src, *, nh, hd, dn_col, gr_col, name):
    s, w = o.shape
    ts = min(ROW_TILE, s)
    gated = gr_src is not None

    def body(*refs):
        if gated:
            dn_ref, o_ref, g_ref, gr_ref, do_ref, dgr_ref, dl_ref, dg_ref = refs
        else:
            dn_ref, o_ref, g_ref, do_ref, dl_ref, dg_ref = refs

        @pl.when(pl.program_id(0) == 0)
        def _():
            dg_ref[...] = jnp.zeros_like(dg_ref)

        lane = lax.broadcasted_iota(jnp.int32, (ts, 128), 1)
        delta = jnp.zeros((ts, 128), F32)
        for h in range(nh):
            sl = _hs(h, hd)
            ov = o_ref[:, sl]
            dnv = dn_ref[:, sl].astype(F32)
            gv = g_ref[h:h + 1, :]
            r = lax.rsqrt(jnp.mean(ov * ov, axis=-1, keepdims=True) + EPS)
            ohat = ov * r
            if gated:
                grv = gr_ref[:, sl].astype(F32)
                sig = _sigmoid(grv)
                dgr_ref[:, sl] = (dnv * (ohat * gv) * (sig * (1.0 + grv * (1.0 - sig)))).astype(BF16)
                dnv = dnv * (grv * sig)
            dg_ref[h:h + 1, :] += jnp.sum(dnv * ohat, axis=0, keepdims=True)
            dohat = dnv * gv
            do = r * (dohat - ohat * jnp.mean(dohat * ohat, axis=-1, keepdims=True))
            do_ref[:, sl] = do.astype(BF16)
            delta = jnp.where(lane == h, jnp.sum(do.astype(BF16).astype(F32) * ov, axis=-1, keepdims=True), delta)
        dl_ref[...] = delta

    in_specs = [pl.BlockSpec((ts, w), lambda i: (i, dn_col)), _row_spec(ts, w),
                pl.BlockSpec((nh, hd), lambda i: (0, 0))]
    args = [dn_in, o, g]
    out_specs = [_row_spec(ts, w)]
    out_shape = [jax.ShapeDtypeStruct((s, w), BF16)]
    if gated:
        in_specs.append(pl.BlockSpec((ts, w), lambda i: (i, gr_col)))
        args.append(gr_src)
        out_specs.append(_row_spec(ts, w))
        out_shape.append(jax.ShapeDtypeStruct((s, w), BF16))
    out_specs += [_row_spec(ts, 128), pl.BlockSpec((nh, hd), lambda i: (0, 0))]
    out_shape += [jax.ShapeDtypeStruct((s, 128), F32), jax.ShapeDtypeStruct((nh, hd), F32)]
    return pl.pallas_call(
        body, grid=(s // ts,), in_specs=in_specs, out_specs=out_specs, out_shape=out_shape,
        compiler_params=_cp("arbitrary"), name=name,
    )(*args)


GQ_BLK = 3 * FOX_W // GLA_DK
GK_BLK = GQ_BLK + GLA_HEADS
GV_BLK = (3 * FOX_W + 2 * GLA_KW) // GLA_DV
GR_BLK = GV_BLK + GLA_HEADS


def _gla_chunk_terms(la):
    cum = _dot_nn(_tri(CHUNK), la, HIGHEST)
    total = cum[CHUNK - 1:CHUNK, :]
    return jnp.exp(total - cum), jnp.exp(total)


def _gla_fwd(proj, log_a, g_gla, *, name):
    s = proj.shape[0]
    rows = min(GLA_ROWS, s)
    cb = rows // CHUNK
    nblk = s // rows
    scale = GLA_DK ** -0.5

    def body(q_ref, k_ref, v_ref, gr_ref, la_ref, g_ref, o_ref, n_ref, st_ref, st_sc):
        h = pl.program_id(0)

        @pl.when(pl.program_id(1) == 0)
        def _():
            st_sc[...] = jnp.zeros_like(st_sc)

        gv = g_ref[pl.ds(h, 1), :]
        for ci in range(cb):
            sl = slice(ci * CHUNK, (ci + 1) * CHUNK)
            e, dec = _gla_chunk_terms(la_ref[sl, :])
            k_dec = (k_ref[sl, :].astype(F32) * e).astype(BF16)
            st = st_sc[...] * dec + _dot_tn(v_ref[sl, :], k_dec)
            st_sc[...] = st
            st_ref[0, ci] = st
            qs = (q_ref[sl, :].astype(F32) * scale).astype(BF16)
            o = _dot_nt(qs, st.astype(BF16))
            o_ref[sl, :] = o
            r = lax.rsqrt(jnp.mean(o * o, axis=-1, keepdims=True) + EPS)
            grv = gr_ref[sl, :].astype(F32)
            n_ref[sl, :] = (o * r * gv * (grv * _sigmoid(grv))).astype(BF16)

    return pl.pallas_call(
        body, grid=(GLA_HEADS, nblk),
        in_specs=[pl.BlockSpec((rows, GLA_DK), lambda h, n: (n, GQ_BLK + h)),
                  pl.BlockSpec((rows, GLA_DK), lambda h, n: (n, GK_BLK + h)),
                  pl.BlockSpec((rows, GLA_DV), lambda h, n: (n, GV_BLK + h)),
                  pl.BlockSpec((rows, GLA_DV), lambda h, n: (n, GR_BLK + h)),
                  pl.BlockSpec((rows, GLA_DK), lambda h, n: (n, h)),
                  pl.BlockSpec((GLA_HEADS, GLA_DV), lambda h, n: (0, 0))],
        out_specs=[pl.BlockSpec((rows, GLA_DV), lambda h, n: (n, h)),
                   pl.BlockSpec((rows, GLA_DV), lambda h, n: (n, h)),
                   pl.BlockSpec((1, cb, GLA_DV, GLA_DK), lambda h, n: (h, n, 0, 0))],
        out_shape=[jax.ShapeDtypeStruct((s, GLA_W), F32), jax.ShapeDtypeStruct((s, GLA_W), BF16),
                   jax.ShapeDtypeStruct((GLA_HEADS, s // CHUNK, GLA_DV, GLA_DK), F32)],
        scratch_shapes=[pltpu.VMEM((GLA_DV, GLA_DK), F32)],
        compiler_params=_cp("parallel", "arbitrary"), name=name,
    )(proj, proj, proj, proj, log_a, g_gla)


def _gla_bwd(proj, log_a, do, states, *, name):
    s = proj.shape[0]
    rows = min(GLA_ROWS, s)
    cb = rows // CHUNK
    nblk = s // rows
    scale = GLA_DK ** -0.5

    def body(q_ref, k_ref, v_ref, la_ref, do_ref, st_ref, prev_ref, dq_ref, dk_ref, dv_ref, dla_ref, g_sc):
        nrev = pl.program_id(1)
        blk = nblk - 1 - nrev

        @pl.when(nrev == 0)
        def _():
            g_sc[...] = jnp.zeros_like(g_sc)

        for ci in reversed(range(cb)):
            sl = slice(ci * CHUNK, (ci + 1) * CHUNK)
            e, dec = _gla_chunk_terms(la_ref[sl, :])
            kd = k_ref[sl, :].astype(F32) * e
            qs = (q_ref[sl, :].astype(F32) * scale).astype(BF16)
            dov = do_ref[sl, :]
            st = st_ref[0, ci]
            if ci > 0:
                st_prev = st_ref[0, ci - 1]
            else:
                st_prev = prev_ref[0, 0] * (blk > 0).astype(F32)
            dq_ref[sl, :] = (_dot_nn(dov, st.astype(BF16)) * scale).astype(BF16)
            gt = g_sc[...] + _dot_tn(dov, qs)
            gtb = gt.astype(BF16)
            dkd = _dot_nn(v_ref[sl, :], gtb)
            dv_ref[sl, :] = _dot_nt(kd.astype(BF16), gtb).astype(BF16)
            dk_ref[sl, :] = (dkd * e).astype(BF16)
            ddec = jnp.sum(gt * st_prev, axis=0, keepdims=True) * dec
            dla_ref[sl, :] = _dot_nn(_tri(CHUNK, strict=True), dkd * kd, HIGHEST) + ddec
            g_sc[...] = gt * dec

    rev = lambda col0: (lambda h, n: (nblk - 1 - n, col0 + h))
    return pl.pallas_call(
        body, grid=(GLA_HEADS, nblk),
        in_specs=[pl.BlockSpec((rows, GLA_DK), rev(GQ_BLK)),
                  pl.BlockSpec((rows, GLA_DK), rev(GK_BLK)),
                  pl.BlockSpec((rows, GLA_DV), rev(GV_BLK)),
                  pl.BlockSpec((rows, GLA_DK), rev(0)),
                  pl.BlockSpec((rows, GLA_DV), rev(0)),
                  pl.BlockSpec((1, cb, GLA_DV, GLA_DK), lambda h, n: (h, nblk - 1 - n, 0, 0)),
                  pl.BlockSpec((1, 1, GLA_DV, GLA_DK),
                               lambda h, n: (h, jnp.maximum((nblk - 1 - n) * cb - 1, 0), 0, 0))],
        out_specs=[pl.BlockSpec((rows, GLA_DK), rev(0)), pl.BlockSpec((rows, GLA_DK), rev(0)),
                   pl.BlockSpec((rows, GLA_DV), rev(0)), pl.BlockSpec((rows, GLA_DK), rev(0))],
        out_shape=[jax.ShapeDtypeStruct((s, GLA_KW), BF16), jax.ShapeDtypeStruct((s, GLA_KW), BF16),
                   jax.ShapeDtypeStruct((s, GLA_W), BF16), jax.ShapeDtypeStruct((s, GLA_KW), F32)],
        scratch_shapes=[pltpu.VMEM((GLA_DV, GLA_DK), F32)],
        compiler_params=_cp("parallel", "arbitrary"), name=name,
    )(proj, proj, proj, log_a, do, states, states)


def _row_tile(r):
    tr = min(ROW_TILE, r)
    while r % tr or tr % 8:
        tr -= 1
    return tr


def _adamw_math(w, g, m, v):
    m = ADAM_B1 * m + (1.0 - ADAM_B1) * g
    v = ADAM_B2 * v + (1.0 - ADAM_B2) * (g * g)
    m_hat = m / (1.0 - ADAM_B1 ** ADAM_STEP)
    v_hat = v / (1.0 - ADAM_B2 ** ADAM_STEP)
    delta = -ADAM_LR * (m_hat / (jnp.sqrt(v_hat) + ADAM_EPS) + ADAM_WD * w)
    return delta, m, v


COL_TILE = 256


def _tile_2d(r, c):
    if r % 8 == 0:
        return _row_tile(r), c
    assert c % COL_TILE == 0, (r, c)
    return r, COL_TILE


def _half_shape(shape):
    r, c = shape[-2:]
    return tuple(shape[:-2]) + ((r // 2, c) if _half_axis(r) == 0 else (r, c // 2))


def _adam(g, w, m, v, *, name):
    r, c = w.shape
    tr, tc = _tile_2d(r, c)

    def body(g_ref, w_ref, m_ref, v_ref, d_ref, mo_ref, vo_ref):
        d, mn, vn = _adamw_math(w_ref[...], g_ref[...], m_ref[...], v_ref[...])
        d_ref[...] = d
        mo_ref[...] = mn
        vo_ref[...] = vn

    spec = pl.BlockSpec((tr, tc), lambda i, j: (i, j))
    return pl.pallas_call(
        body, grid=(r // tr, c // tc), in_specs=[spec] * 4, out_specs=[spec] * 3,
        out_shape=[jax.ShapeDtypeStruct((r, c), F32)] * 3,
        compiler_params=_cp("parallel", "parallel"), name=name,
    )(g, w, m, v)


def _ada_grad_adam(c_all_t, dmod_cols, w, m, v, *, name):
    r, c = w.shape
    tr, tc = min(512, r), min(1024, c)

    def body(ct_ref, dm_ref, w_ref, m_ref, v_ref, g_ref, d_ref, mo_ref, vo_ref):
        g = _dot_nn(ct_ref[...], dm_ref[...], HIGHEST)
        g_ref[...] = g
        d, mn, vn = _adamw_math(w_ref[...], g, m_ref[...], v_ref[...])
        d_ref[...] = d
        mo_ref[...] = mn
        vo_ref[...] = vn

    spec = pl.BlockSpec((tr, tc), lambda i, j: (i, j))
    nb = c_all_t.shape[1]
    return pl.pallas_call(
        body, grid=(r // tr, c // tc),
        in_specs=[pl.BlockSpec((tr, nb), lambda i, j: (i, 0)), pl.BlockSpec((nb, tc), lambda i, j: (0, j)),
                  spec, spec, spec],
        out_specs=[spec] * 4, out_shape=[jax.ShapeDtypeStruct((r, c), F32)] * 4,
        compiler_params=_cp("parallel", "parallel"), name=name,
    )(c_all_t, dmod_cols, w, m, v)


def _mod_shard(c_all, w, b, *, name):
    k, c = w.shape
    tc = min(512, c)
    nb = c_all.shape[0]

    def body(c_ref, w_ref, b_ref, o_ref):
        o_ref[...] = _dot_nn(c_ref[...], w_ref[...], HIGHEST) + b_ref[...]

    return pl.pallas_call(
        body, grid=(c // tc,),
        in_specs=[pl.BlockSpec((nb, k), lambda j: (0, 0)), pl.BlockSpec((k, tc), lambda j: (0, j)),
                  pl.BlockSpec((1, tc), lambda j: (0, j))],
        out_specs=pl.BlockSpec((nb, tc), lambda j: (0, j)),
        out_shape=jax.ShapeDtypeStruct((nb, c), F32),
        compiler_params=_cp("parallel"), name=name,
    )(c_all, w, b)


def _silu_rows(c, *, name):
    def body(c_ref, o_ref):
        cv = c_ref[...]
        o_ref[...] = cv * _sigmoid(cv)

    return pl.pallas_call(body, out_shape=jax.ShapeDtypeStruct(c.shape, F32), name=name)(c)


def _pair_sum(g, got, idx, *, name):
    p, r, c = g.shape
    ax = _half_axis(r)
    hr, hc = _half_shape((r, c))
    tr, tc = _tile_2d(hr, hc)
    nbr, nbc = hr // tr, hc // tc

    def body(idx_ref, a_ref, b_ref, o_ref):
        o_ref[...] = (a_ref[...].astype(F32) + b_ref[...].astype(F32)).astype(BF16)

    def own_map(i, j, k, idx_ref):
        return (i, j + (idx_ref[0] * nbr if ax == 0 else 0), k + (idx_ref[0] * nbc if ax == 1 else 0))

    half_spec = pl.BlockSpec((1, tr, tc), lambda i, j, k, idx_ref: (i, j, k))
    return pl.pallas_call(
        body,
        grid_spec=pltpu.PrefetchScalarGridSpec(
            num_scalar_prefetch=1, grid=(p, nbr, nbc),
            in_specs=[pl.BlockSpec((1, tr, tc), own_map), half_spec],
            out_specs=half_spec),
        out_shape=jax.ShapeDtypeStruct((p, hr, hc), BF16),
        compiler_params=_cp("parallel", "parallel", "parallel"), name=name,
    )(idx, g, got)


def _final_sum(own, parts, idx, shard_shape, *, name):
    ax = _half_axis(shard_shape[0])
    hr, hc = own.shape[1:]
    tr, tc = _tile_2d(hr, hc)
    nbr, nbc = hr // tr, hc // tc

    def body(idx_ref, own_ref, parts_ref, o_ref):
        acc = own_ref[0].astype(F32)
        for q in range(3):
            acc = acc + parts_ref[q].astype(F32)
        o_ref[...] = acc

    def out_map(j, k, idx_ref):
        return (j + (idx_ref[0] * nbr if ax == 0 else 0), k + (idx_ref[0] * nbc if ax == 1 else 0))

    return pl.pallas_call(
        body,
        grid_spec=pltpu.PrefetchScalarGridSpec(
            num_scalar_prefetch=1, grid=(nbr, nbc),
            in_specs=[pl.BlockSpec((1, tr, tc), lambda j, k, idx_ref: (idx_ref[1], j, k)),
                      pl.BlockSpec((3, tr, tc), lambda j, k, idx_ref: (0, j, k))],
            out_specs=pl.BlockSpec((tr, tc), out_map)),
        out_shape=jax.ShapeDtypeStruct(tuple(shard_shape), F32),
        compiler_params=_cp("parallel", "parallel"), name=name,
    )(idx, own, parts)


def _stack_sum(x, *, name):
    p, r, c = x.shape
    tr = _row_tile(r)

    def body(x_ref, o_ref):
        acc = x_ref[0].astype(F32)
        for q in range(1, p):
            acc = acc + x_ref[q].astype(F32)
        o_ref[...] = acc

    return pl.pallas_call(
        body, grid=(r // tr,),
        in_specs=[pl.BlockSpec((p, tr, c), lambda i: (0, i, 0))],
        out_specs=pl.BlockSpec((tr, c), lambda i: (i, 0)),
        out_shape=jax.ShapeDtypeStruct((r, c), F32),
        compiler_params=_cp("parallel"), name=name,
    )(x)


def _place():
    x, y, c = lax.axis_index("x"), lax.axis_index("y"), lax.axis_index("c")
    chips = [(1 - x, y), (x, 1 - y), (1 - x, 1 - y)]
    return x, y, c, chips


def _gather8(x_shard, *, name):
    m_per, n = x_shard.shape

    def body(x_ref, out_ref, send_sems, recv_sems, local_sem):
        x, y, c, chips = _place()
        me, sibling = (x, y, c), (x, y, 1 - c)

        def rows(px, py, pc):
            return out_ref.at[pl.ds((4 * px + 2 * py + pc) * m_per, m_per), :]

        def copy(k, block, to, src=None):
            return pltpu.make_async_remote_copy(
                src_ref=rows(*block) if src is None else src, dst_ref=rows(*block),
                send_sem=send_sems.at[k], recv_sem=recv_sems.at[k], device_id=to, device_id_type=MESH)

        mine = pltpu.make_async_copy(x_ref, rows(*me), local_sem)
        mine.start()
        first = [copy(0, me, sibling, src=x_ref)]
        first += [copy(1 + j, me, (*chip, c), src=x_ref) for j, chip in enumerate(chips)]
        for cp in first:
            cp.start()
        passed = [copy(4 + j, (*chip, c), sibling) for j, chip in enumerate(chips)]
        for j, chip in enumerate(chips):
            copy(1 + j, (*chip, c), me).wait_recv()
            passed[j].start()
        copy(0, sibling, me).wait_recv()
        for j, chip in enumerate(chips):
            copy(4 + j, (*chip, 1 - c), me).wait_recv()
        for cp in first + passed:
            cp.wait_send()
        mine.wait()

    return pl.pallas_call(
        body,
        out_shape=jax.ShapeDtypeStruct((8 * m_per, n), x_shard.dtype),
        in_specs=[pl.BlockSpec(memory_space=pltpu.VMEM)],
        out_specs=pl.BlockSpec(memory_space=pltpu.VMEM),
        scratch_shapes=[pltpu.SemaphoreType.DMA((7,)), pltpu.SemaphoreType.DMA((7,)), pltpu.SemaphoreType.DMA],
        name=name,
    )(x_shard)


def _gather_weights(shards, *, name):
    return _comm_call(lambda ins, outs: [cp for i, o in zip(ins, outs) for cp in _plan_gather_ici(i, o)],
                      shards, [jax.ShapeDtypeStruct((4,) + s.shape, s.dtype) for s in shards], name=name)


def _plan_start(plan, send_sems, recv_sems):
    for k, (src, dst, _, peer) in enumerate(plan):
        pltpu.make_async_remote_copy(src_ref=src, dst_ref=dst, send_sem=send_sems.at[k], recv_sem=recv_sems.at[k],
                                     device_id=peer, device_id_type=MESH).start()


def _plan_wait(plan, send_sems, recv_sems):
    for k, (src, _, land, peer) in enumerate(plan):
        pltpu.make_async_remote_copy(src_ref=src, dst_ref=land, send_sem=send_sems.at[k], recv_sem=recv_sems.at[k],
                                     device_id=peer, device_id_type=MESH).wait_recv()
    for k, (src, dst, _, peer) in enumerate(plan):
        pltpu.make_async_remote_copy(src_ref=src, dst_ref=dst, send_sem=send_sems.at[k], recv_sem=recv_sems.at[k],
                                     device_id=peer, device_id_type=MESH).wait_send()


def _half_axis(rows):
    return 0 if rows % 32 == 0 else 1


def _rows_half(ref, hc, axis):
    hr = ref.shape[axis] // 2
    idx = [slice(None)] * len(ref.shape)
    idx[axis] = pl.ds(hc * hr, hr)
    return ref.at[tuple(idx)]


def _plan_gather_ici(shard, full):
    x, y, c, chips = _place()
    ax = _half_axis(shard.shape[0])
    src = _rows_half(shard, c, ax)
    return [(src, _rows_half(full.at[2 * x + y], c, ax), _rows_half(full.at[2 * cx + cy], c, ax), (cx, cy, c))
            for cx, cy in chips]


def _plan_gather_d2d(full):
    x, y, c, chips = _place()
    ax = _half_axis(full.shape[1])
    plan = []
    for cx, cy in chips:
        slot = full.at[2 * cx + cy]
        plan.append((_rows_half(slot, c, ax), _rows_half(slot, c, ax), _rows_half(slot, 1 - c, ax), (x, y, 1 - c)))
    return plan


def _plan_pair(grad, got):
    x, y, c, _ = _place()
    return [(_rows_half(grad, 1 - c, 1 + _half_axis(grad.shape[1])), got, got, (x, y, 1 - c))]


def _plan_shard_ici(sums, parts):
    _, _, c, chips = _place()
    return [(sums.at[2 * cx + cy], parts.at[k], parts.at[k], (cx, cy, c)) for k, (cx, cy) in enumerate(chips)]


def _plan_half(buf):
    x, y, c, _ = _place()
    ax = _half_axis(buf.shape[0])
    mine = _rows_half(buf, c, ax)
    return [(mine, mine, _rows_half(buf, 1 - c, ax), (x, y, 1 - c))]


def _comm_call(plan_fn, inputs, out_shapes, *, name, aliases=None):
    ni, no = len(inputs), len(out_shapes)

    def body(*refs):
        plan = plan_fn(refs[:ni], refs[ni:ni + no])
        send_sems, recv_sems = refs[ni + no:]
        _plan_start(plan, send_sems, recv_sems)
        _plan_wait(plan, send_sems, recv_sems)

    any_spec = pl.BlockSpec(memory_space=pl.ANY)
    n_copies = 3 * max(ni, no)
    return pl.pallas_call(
        body, out_shape=list(out_shapes), in_specs=[any_spec] * ni, out_specs=[any_spec] * no,
        scratch_shapes=[pltpu.SemaphoreType.DMA((n_copies,)), pltpu.SemaphoreType.DMA((n_copies,))],
        input_output_aliases=aliases or {}, name=name,
    )(*inputs)


def _gather_forward(fulls, *, name):
    return _comm_call(lambda ins, outs: [cp for o in outs for cp in _plan_gather_d2d(o)],
                      fulls, [jax.ShapeDtypeStruct(f.shape, f.dtype) for f in fulls], name=name,
                      aliases={k: k for k in range(len(fulls))})


def _pair_exchange(grads, *, name):
    return _comm_call(lambda ins, outs: [cp for i, o in zip(ins, outs) for cp in _plan_pair(i, o)],
                      grads, [jax.ShapeDtypeStruct(_half_shape(g.shape), g.dtype) for g in grads], name=name)


def _half_exchange(bufs, *, name):
    return _comm_call(lambda ins, outs: [cp for o in outs for cp in _plan_half(o)],
                      bufs, [jax.ShapeDtypeStruct(b.shape, b.dtype) for b in bufs], name=name,
                      aliases={k: k for k in range(len(bufs))})


def _split_w_in(w_in_t):
    d = w_in_t.shape[1]
    main = jnp.concatenate([w_in_t[0:3072], w_in_t[3080:5128], w_in_t[5144:6168]], axis=0)
    small = jnp.concatenate([w_in_t[3072:3080], w_in_t[5128:5144], jnp.zeros((SMALL_W - 24, d), w_in_t.dtype)], axis=0)
    return main, small


def _merge_dw_in(dw_main, dw_small):
    return jnp.concatenate([dw_main[0:3072], dw_small[0:8], dw_main[3072:5120], dw_small[8:24], dw_main[5120:6144]],
                           axis=0)


def _gather_side(shards):
    return _Side(shards, [jax.ShapeDtypeStruct((4,) + w.shape, w.dtype) for w in shards],
                 lambda ins, outs: [cp for i, o in zip(ins, outs) for cp in _plan_gather_ici(i, o)], 3 * len(shards))


def _finish_gather(fulls, owns, chip, *, name):
    fulls = _gather_forward(list(fulls), name=name)
    return [lax.dynamic_update_index_in_dim(f, o, chip, 0) for f, o in zip(fulls, owns)]


def _shard_side(sums):
    return _Side([sums], [jax.ShapeDtypeStruct((3,) + sums.shape[1:], sums.dtype)],
                 lambda ins, outs: _plan_shard_ici(ins[0], outs[0]), 3)


def _chip_sum(grad, idx, tag):
    got, = _pair_exchange([grad], name=f"grad_pair_exchange_{tag}")
    return _pair_sum(grad, got, idx, name=f"grad_pair_sum_{tag}")


def _local_step(x, target, mod, g_pre_mix, g_post_mix, g_pre_mlp, g_post_mlp, w_in_t, b_fgate, w_gla_a2,
                b_gla_a2, g_fox, g_gla, own_w_out, own_w_mlp_in, own_w_mlp_out, chip, idx):
    s, d = x.shape
    shift_m, scale_m, gate_m, shift_f, scale_f, gate_f = [mod[:, i * d:(i + 1) * d] for i in range(6)]
    a1 = g_pre_mix * (1.0 + scale_m)
    a2 = g_pre_mlp * (1.0 + scale_f)
    w_main, w_small = _split_w_in(w_in_t)
    bf = jnp.concatenate([b_fgate, jnp.zeros((1, SMALL_W - FOX_HEADS), F32)], axis=1)
    w2p = jnp.zeros((SMALL_W, GLA_KW), F32).at[FOX_HEADS:FOX_HEADS + GLA_RANK].set(w_gla_a2)

    h1 = _pre_fwd(x, a1, shift_m, name="pre_mix_fwd")
    proj, gw_out = _mm(h1, w_main, mode="nt", out_dtypes=[BF16], name="in_proj_main",
                       side=_gather_side([own_w_out]))
    ps, = _mm(h1, w_small, mode="nt", out_dtypes=[F32], name="in_proj_small")
    gw_out, = _finish_gather([gw_out], [own_w_out], chip, name="gather_w_out_d2d")
    w_out_full = gw_out.reshape(-1, d)
    cum, log_a = _gates_fwd(ps, bf, w2p, b_gla_a2, name="gates_fwd")
    cum_t = cum[:, :FOX_HEADS].T
    o_fox, fox_n, lse, gw_mlp_in = _fox_fwd(proj, cum_t, g_fox, name="fox_fwd", side=_gather_side([own_w_mlp_in]))
    gw_mlp_in, = _finish_gather([gw_mlp_in], [own_w_mlp_in], chip, name="gather_w_mlp_in_d2d")
    o_gla, gla_n, states = _gla_fwd(proj, log_a, g_gla, name="gla_fwd")
    mixed = jnp.concatenate([fox_n, gla_n], axis=1)
    y1, = _mm(mixed, w_out_full, mode="nn", out_dtypes=[F32], name="out_proj")
    x1, h2 = _post_pre_fwd(x, y1, gate_m, g_post_mix, a2, shift_f, name="post_mix_pre_mlp_fwd")

    def mlp_act(acc):
        r = jnp.maximum(acc, 0.0)
        return acc, r * r

    u, act, gw_mlp_out = _mm(h2, gw_mlp_in, mode="nn", out_dtypes=[BF16, BF16], epi=mlp_act, name="mlp_in",
                             b_slots=4, tm=MM_TM, side=_gather_side([own_w_mlp_out]))
    gw_mlp_out, = _finish_gather([gw_mlp_out], [own_w_mlp_out], chip, name="gather_w_mlp_out_d2d")
    w_mlp_out_full = gw_mlp_out.reshape(-1, d)
    y2, = _mm(act, w_mlp_out_full, mode="nn", out_dtypes=[F32], name="mlp_out")
    dx2, dy2, loss_part, dgate_f, dg_post_mlp = _post_loss_bwd(x1, y2, gate_f, g_post_mlp, target,
                                                               name="post_mlp_loss_bwd")
    dw_mlp_out, = _mm(act, dy2, mode="tn", out_dtypes=[BF16], name="dw_mlp_out")
    sum_mlp_out = _chip_sum(dw_mlp_out.reshape(4, D_FF // 4, d), idx, "mlp_out")

    def act_bwd(acc, uv):
        return (acc * (2.0 * jnp.maximum(uv.astype(F32), 0.0)),)

    du, parts_mlp_out = _mm(dy2, w_mlp_out_full, mode="nt", out_dtypes=[BF16], extras=[u], epi=act_bwd,
                            name="d_mlp_hidden", tm=MM_TM, side=_shard_side(sum_mlp_out))
    nj = D_FF // 4 // min(MM_T, D_FF // 4)
    tmw = min(MM_T, d)
    dw_mlp_in, = _mm(h2, du, mode="tn", out_dtypes=[BF16], name="dw_mlp_in",
                     out_shapes=[jax.ShapeDtypeStruct((4, d, D_FF // 4), BF16)],
                     out_specs=[pl.BlockSpec((1, tmw, min(MM_T, D_FF // 4)), lambda i, j, kk: (j // nj, i, j % nj))])
    sum_mlp_in = _chip_sum(dw_mlp_in, idx, "mlp_in")
    dh2, parts_mlp_in = _mm(du, gw_mlp_in, mode="nt", out_dtypes=[F32], name="d_mlp_in", b_slots=4,
                            side=_shard_side(sum_mlp_in))
    dx1, dshift_f, da2, dy1, dgate_m, dg_post_mix = _pre_post_bwd(dh2, x1, dx2, a2, y1, gate_m, g_post_mix,
                                                                  name="pre_mlp_post_mix_bwd")
    dw_out, = _mm(mixed, dy1, mode="tn", out_dtypes=[BF16], name="dw_out")
    sum_out = _chip_sum(dw_out.reshape(4, d // 4, d), idx, "out")
    dmixed, parts_out = _mm(dy1, w_out_full, mode="nt", out_dtypes=[BF16], name="d_mixed", side=_shard_side(sum_out))
    do_fox, delta, dg_fox = _head_norm_bwd(dmixed, o_fox, g_fox, None, nh=FOX_HEADS, hd=FOX_HD, dn_col=0,
                                           gr_col=0, name="fox_norm_bwd")
    do_gla, dgr, _, dg_gla = _head_norm_bwd(dmixed, o_gla, g_gla, proj, nh=GLA_HEADS, hd=GLA_DV, dn_col=1,
                                            gr_col=(3 * FOX_W + 2 * GLA_KW + GLA_W) // GLA_W, name="gla_norm_bwd")
    dq_fox, dk_fox, dv_fox, dcq, dck_t = _fox_bwd(proj, do_fox, cum_t, lse, delta, name="fox_bwd")
    dgq, dgk, dgv, dla = _gla_bwd(proj, log_a, do_gla, states, name="gla_bwd")
    dck = dcq + jnp.concatenate([dck_t.T, jnp.zeros((s, SMALL_W - FOX_HEADS), F32)], axis=1)
    dps, dbf, dw2p, db2 = _gates_bwd(dck, ps, bf, w2p, b_gla_a2, dla, name="gates_bwd")
    dproj = jnp.concatenate([dq_fox.astype(BF16), dk_fox, dv_fox, dgq, dgk, dgv, dgr], axis=1)
    dw_main, = _mm(dproj, h1, mode="tn", out_dtypes=[BF16], name="dw_in_main")
    dw_small, = _mm(dps, h1, mode="tn", out_dtypes=[BF16], name="dw_in_small")
    rs_in = w_in_t.shape[0] // 4
    dw_in = _merge_dw_in(dw_main, dw_small).reshape(4, rs_in, d)
    sum_in = _chip_sum(dw_in, idx, "in")
    dh1_small, = _mm(dps, w_small, mode="nn", out_dtypes=[F32], name="d_h1_small")
    dh1, parts_in = _mm(dproj, w_main, mode="nn", out_dtypes=[F32], extras=[dh1_small],
                        epi=lambda acc, e: (acc + e,), name="d_h1", side=_shard_side(sum_in))
    grad_x, dshift_m, da1 = _pre_bwd(dh1, x, dx1, a1, name="pre_mix_bwd")
    bufs = [_final_sum(sm, pt, idx, shp, name=f"grad_final_sum_{tag}")
            for tag, sm, pt, shp in [("in", sum_in, parts_in, (rs_in, d)), ("out", sum_out, parts_out, (d // 4, d)),
                                     ("mlp_in", sum_mlp_in, parts_mlp_in, (d, D_FF // 4)),
                                     ("mlp_out", sum_mlp_out, parts_mlp_out, (D_FF // 4, d))]]

    dmod = jnp.concatenate([dshift_m, da1 * g_pre_mix, dgate_m, dshift_f, da2 * g_pre_mlp, dgate_f], axis=1)
    small = dict(
        dmod=dmod, g_pre_mix=da1 * (1.0 + scale_m), g_post_mix=dg_post_mix, g_pre_mlp=da2 * (1.0 + scale_f),
        g_post_mlp=dg_post_mlp, b_fgate=dbf[:, :FOX_HEADS], w_gla_a2=dw2p[FOX_HEADS:FOX_HEADS + GLA_RANK],
        b_gla_a2=db2, g_fox_out=dg_fox, g_gla_out=dg_gla)
    return loss_part, grad_x, bufs, small


def _pack(arrays):
    flat = jnp.concatenate([a.reshape(-1).astype(F32) for a in arrays])
    n = flat.shape[0]
    rows = -(-n // 128)
    rows = -(-rows // 8) * 8
    return jnp.pad(flat, (0, rows * 128 - n)).reshape(rows, 128)


def _unpack(buf, shapes):
    flat = buf.reshape(-1)
    out, off = [], 0
    for shp in shapes:
        n = 1
        for q in shp:
            n *= q
        out.append(flat[off:off + n].reshape(shp))
        off += n
    return out


SMALL_GRAD_ORDER = ["dmod", "g_pre_mix", "g_post_mix", "g_pre_mlp", "g_post_mlp", "b_fgate", "w_gla_a2", "b_gla_a2",
                    "g_fox_out", "g_gla_out"]


def kernel(x, c, w_ada, b_ada, g_pre_mix, g_post_mix, w_in, b_fgate, w_gla_a2, b_gla_a2, g_fox_out, g_gla_out, w_out, g_pre_mlp, g_post_mlp, w_mlp_in, w_mlp_out, loss_target, m_w_ada, m_b_ada, m_g_pre_mix, m_g_post_mix, m_w_in, m_b_fgate, m_w_gla_a2, m_b_gla_a2, m_g_fox_out, m_g_gla_out, m_w_out, m_g_pre_mlp, m_g_post_mlp, m_w_mlp_in, m_w_mlp_out, v_w_ada, v_b_ada, v_g_pre_mix, v_g_post_mix, v_w_in, v_b_fgate, v_w_gla_a2, v_b_gla_a2, v_g_fox_out, v_g_gla_out, v_w_out, v_g_pre_mlp, v_g_post_mlp, v_w_mlp_in, v_w_mlp_out):
    ix, iy, ic = lax.axis_index("x"), lax.axis_index("y"), lax.axis_index("c")
    chip = 2 * ix + iy
    dev = 4 * ix + 2 * iy + ic
    d = D_MODEL

    c_act = _silu_rows(c, name="silu_c")
    pack1 = _pack([c_act, w_gla_a2[0], g_gla_out[0]])
    rows1 = pack1.shape[0]
    got1 = _gather8(pack1, name="gather_small_fwd").reshape(8, rows1, 128)
    per_dev = [_unpack(got1[q], [(d,), (GLA_RANK, GLA_KW // 4), (GLA_HEADS, GLA_DV // 4)]) for q in range(8)]
    c_all = jnp.stack([p[0] for p in per_dev])
    w_gla_a2_full = jnp.concatenate([per_dev[2 * j][1] for j in range(4)], axis=1)
    g_gla_full = jnp.concatenate([per_dev[2 * j][2] for j in range(4)], axis=1)
    cols = w_ada.shape[2]
    b_ada_shard = lax.dynamic_slice_in_dim(b_ada, chip * cols, cols, axis=1)
    mod_sh = _mod_shard(c_all, w_ada[0], b_ada_shard, name="ada_mod")
    got2 = _gather8(mod_sh, name="gather_mod").reshape(8, 8, cols)
    mod_all = jnp.concatenate([got2[2 * j] for j in range(4)], axis=1)
    mod = lax.dynamic_slice_in_dim(mod_all, dev, 1, axis=0)

    tr_in = lambda a: jnp.transpose(a[0])
    own_bf = [tr_in(w_in).astype(BF16), w_out[0].astype(BF16), w_mlp_in[0].astype(BF16), w_mlp_out[0].astype(BF16)]
    gw_in, = _finish_gather(_gather_weights(own_bf[:1], name="gather_w_in_ici"), own_bf[:1], chip,
                            name="gather_w_in_d2d")
    w_in_t = gw_in.reshape(-1, d)
    idx = jnp.stack([ic, chip]).astype(jnp.int32)

    loss_part, grad_x, bufs, small = _local_step(
        x[0], loss_target[0], mod, g_pre_mix, g_post_mix, g_pre_mlp, g_post_mlp, w_in_t, b_fgate,
        w_gla_a2_full, b_gla_a2, g_fox_out[0], g_gla_full, own_bf[1], own_bf[2], own_bf[3], chip, idx)
    loss = lax.psum(loss_part[0, 0], ("x", "y", "c"))

    g_big = _half_exchange(bufs, name="grad_half_exchange")
    big_w = [(tr_in(w_in), tr_in(m_w_in), tr_in(v_w_in)), (w_out[0], m_w_out[0], v_w_out[0]),
             (w_mlp_in[0], m_w_mlp_in[0], v_w_mlp_in[0]), (w_mlp_out[0], m_w_mlp_out[0], v_w_mlp_out[0])]
    big_res = []
    for q, (g, (w, m, v)) in enumerate(zip(g_big, big_w)):
        res4 = (g,) + tuple(_adam(g, w, m, v, name=f"adam_big_{q}"))
        big_res.append(tuple((jnp.transpose(a) if q == 0 else a)[None] for a in res4))

    pack2 = _pack([small[k] for k in SMALL_GRAD_ORDER])
    rows2 = pack2.shape[0]
    got3 = _gather8(pack2, name="gather_small_grads").reshape(8, rows2, 128)
    dmod_all = got3[:, :6 * d // 128, :].reshape(8, 6 * d)
    sums = _stack_sum(got3, name="small_grad_sum")
    shapes = [(1, 6 * d), (1, d), (1, d), (1, d), (1, d), (1, FOX_HEADS), (1, GLA_RANK, GLA_KW), (1, GLA_KW),
              (1, FOX_HEADS, FOX_HD), (1, GLA_HEADS, GLA_DV)]
    sg = dict(zip(["b_ada"] + SMALL_GRAD_ORDER[1:], _unpack(sums, shapes)))
    sg["w_gla_a2"] = lax.dynamic_slice_in_dim(sg["w_gla_a2"], chip * (GLA_KW // 4), GLA_KW // 4, axis=2)
    sg["g_gla_out"] = lax.dynamic_slice_in_dim(sg["g_gla_out"], chip * (GLA_DV // 4), GLA_DV // 4, axis=2)
    small_names = ["b_ada", "g_pre_mix", "g_post_mix", "b_fgate", "w_gla_a2", "b_gla_a2", "g_fox_out", "g_gla_out",
                   "g_pre_mlp", "g_post_mlp"]
    small_w = dict(b_ada=(b_ada, m_b_ada, v_b_ada), g_pre_mix=(g_pre_mix, m_g_pre_mix, v_g_pre_mix),
                   g_post_mix=(g_post_mix, m_g_post_mix, v_g_post_mix), b_fgate=(b_fgate, m_b_fgate, v_b_fgate),
                   w_gla_a2=(w_gla_a2, m_w_gla_a2, v_w_gla_a2), b_gla_a2=(b_gla_a2, m_b_gla_a2, v_b_gla_a2),
                   g_fox_out=(g_fox_out, m_g_fox_out, v_g_fox_out), g_gla_out=(g_gla_out, m_g_gla_out, v_g_gla_out),
                   g_pre_mlp=(g_pre_mlp, m_g_pre_mlp, v_g_pre_mlp), g_post_mlp=(g_post_mlp, m_g_post_mlp, v_g_post_mlp))
    sshapes = [small_w[k][0].shape for k in small_names]
    pg = _pack([sg[k] for k in small_names])
    pw, pm, pv = [_pack([small_w[k][q] for k in small_names]) for q in range(3)]
    pd, pmn, pvn = _adam(pg, pw, pm, pv, name="adam_small")
    s_delta = dict(zip(small_names, _unpack(pd, sshapes)))
    s_m = dict(zip(small_names, _unpack(pmn, sshapes)))
    s_v = dict(zip(small_names, _unpack(pvn, sshapes)))

    dmod_cols = lax.dynamic_slice_in_dim(dmod_all, chip * cols, cols, axis=1)
    g_ada, d_ada, m_ada, v_ada = _ada_grad_adam(c_all.T, dmod_cols, w_ada[0], m_w_ada[0], v_w_ada[0], name="ada_grad_adam")

    order = ["w_ada", "b_ada", "g_pre_mix", "g_post_mix", "w_in", "b_fgate", "w_gla_a2", "b_gla_a2", "g_fox_out",
             "g_gla_out", "w_out", "g_pre_mlp", "g_post_mlp", "w_mlp_in", "w_mlp_out"]
    res = {"w_ada": (g_ada[None], d_ada[None], m_ada[None], v_ada[None]),
           "w_in": big_res[0], "w_out": big_res[1], "w_mlp_in": big_res[2], "w_mlp_out": big_res[3]}
    for k in small_names:
        res[k] = (sg[k], s_delta[k], s_m[k], s_v[k])
    return (loss, grad_x[None], *[res[k][0] for k in order], *[res[k][1] for k in order],
            *[res[k][2] for k in order], *[res[k][3] for k in order])
```

```python
import functools

import jax
import jax.numpy as jnp
from jax import lax
from jax.experimental import pallas as pl
from jax.experimental.pallas import tpu as pltpu

F32 = jnp.float32
BF16 = jnp.bfloat16
MESH = pl.DeviceIdType.MESH
HIGHEST = lax.Precision.HIGHEST

D_MODEL = 2048
FOX_HEADS = 8
FOX_HD = 128
FOX_W = FOX_HEADS * FOX_HD
GLA_HEADS = 4
GLA_DK = 128
GLA_DV = 256
GLA_KW = GLA_HEADS * GLA_DK
GLA_W = GLA_HEADS * GLA_DV
GLA_RANK = 16
GLA_TEMP = 16.0
CHUNK = 64
D_FF = 4 * D_MODEL
EPS = 1e-6
MAIN_W = 3 * FOX_W + 2 * GLA_KW + 2 * GLA_W
SMALL_W = 128
NEG = -1e30

ADAM_LR = 0.001
ADAM_B1 = 0.9
ADAM_B2 = 0.999
ADAM_EPS = 1e-08
ADAM_WD = 0.01
ADAM_STEP = 10

VMEM_LIMIT = 52 * 1024 * 1024
ROW_TILE = 256
FOX_TQ = 512
FOX_TK = 512
GLA_ROWS = 512
GATE_TS = 512
MM_T = 1024
MM_TK = 2048
MM_TM = 2048


def _cp(*sem):
    return pltpu.CompilerParams(dimension_semantics=sem, vmem_limit_bytes=VMEM_LIMIT)


def _dot_nn(a, b, precision=None):
    return jnp.dot(a, b, preferred_element_type=F32, precision=precision)


def _dot_nt(a, b, precision=None):
    return lax.dot_general(a, b, (((1,), (1,)), ((), ())), preferred_element_type=F32, precision=precision)


def _dot_tn(a, b, precision=None):
    return lax.dot_general(a, b, (((0,), (0,)), ((), ())), preferred_element_type=F32, precision=precision)


def _sigmoid(x):
    return 1.0 / (1.0 + jnp.exp(-x))


def _log_sigmoid(x):
    return jnp.minimum(x, 0.0) - jnp.log(1.0 + jnp.exp(-jnp.abs(x)))


class _Side:
    def __init__(self, inputs, out_shapes, plan_fn, n_copies, aliases=None):
        self.inputs, self.out_shapes, self.plan_fn, self.n_copies = list(inputs), list(out_shapes), plan_fn, n_copies
        self.aliases = dict(aliases or {})

    def scratch(self):
        return [pltpu.SemaphoreType.DMA((self.n_copies,)), pltpu.SemaphoreType.DMA((self.n_copies,))]

    def run(self, in_refs, out_refs, sems, first, last):
        @pl.when(first)
        def _():
            _plan_start(self.plan_fn(in_refs, out_refs), *sems)

        @pl.when(last)
        def _():
            _plan_wait(self.plan_fn(in_refs, out_refs), *sems)


def _mm(a, b, *, mode, out_dtypes, name, tm=None, tn=None, tk=None, extras=(), epi=None,
        out_shapes=None, out_specs=None, side=None, b_slots=0):
    tm, tn, tk = tm or MM_T, tn or MM_T, tk or MM_TK
    b2 = (b.shape[1], b_slots * b.shape[2]) if b_slots else b.shape
    if mode == "nn":
        (m, k), n = a.shape, b2[1]
    elif mode == "nt":
        (m, k), n = a.shape, b2[0]
    else:
        (k, m), n = a.shape, b2[1]
    tm, tn, tk = min(tm, m), min(tn, n), min(tk, k)
    if b_slots:
        tn = min(tn, b.shape[2]) if mode == "nn" else tn
        tk = min(tk, b.shape[2]) if mode == "nt" else tk
    assert m % tm == 0 and n % tn == 0 and k % tk == 0, (name, m, n, k)
    nk = k // tk
    n_out, n_ex = len(out_dtypes), len(extras)
    if epi is None:
        epi = lambda acc: tuple(acc for _ in range(n_out))
    dot = {"nn": _dot_nn, "nt": _dot_nt, "tn": _dot_tn}[mode]

    n_si = len(side.inputs) if side else 0
    n_so = len(side.out_shapes) if side else 0
    grid = (m // tm, n // tn, nk)

    def body(*refs):
        a_ref, b_ref = refs[0], refs[1]
        ex_refs = refs[2:2 + n_ex]
        base = 2 + n_ex + n_si
        o_refs = refs[base:base + n_out]
        scratch = refs[base + n_out + n_so:]
        if side:
            pos = [pl.program_id(q) for q in range(3)]
            first = (pos[0] == 0) & (pos[1] == 0) & (pos[2] == 0)
            last = (pos[0] == grid[0] - 1) & (pos[1] == grid[1] - 1) & (pos[2] == grid[2] - 1)
            side.run(refs[2 + n_ex:base], refs[base + n_out:base + n_out + n_so], scratch[-2:], first, last)
        part = dot(a_ref[...], b_ref[...])

        def finish(acc):
            outs = epi(acc, *[e[...] for e in ex_refs])
            for o_ref, val in zip(o_refs, outs):
                o_ref[...] = val.reshape(o_ref.shape).astype(o_ref.dtype)

        if nk == 1:
            finish(part)
        else:
            acc_ref = scratch[0]
            kk = pl.program_id(2)

            @pl.when(kk == 0)
            def _():
                acc_ref[...] = part

            @pl.when(kk > 0)
            def _():
                acc_ref[...] += part

            @pl.when(kk == nk - 1)
            def _():
                finish(acc_ref[...])

    if mode == "nn":
        a_spec = pl.BlockSpec((tm, tk), lambda i, j, kk: (i, kk))
        b_spec = pl.BlockSpec((tk, tn), lambda i, j, kk: (kk, j))
        if b_slots:
            per = b.shape[2] // tn
            b_spec = pl.BlockSpec((None, tk, tn), lambda i, j, kk: (j // per, kk, j % per))
    elif mode == "nt":
        a_spec = pl.BlockSpec((tm, tk), lambda i, j, kk: (i, kk))
        b_spec = pl.BlockSpec((tn, tk), lambda i, j, kk: (j, kk))
        if b_slots:
            per = b.shape[2] // tk
            b_spec = pl.BlockSpec((None, tn, tk), lambda i, j, kk: (kk // per, j, kk % per))
    else:
        assert not b_slots
        a_spec = pl.BlockSpec((tk, tm), lambda i, j, kk: (kk, i))
        b_spec = pl.BlockSpec((tk, tn), lambda i, j, kk: (kk, j))
    tile_spec = pl.BlockSpec((tm, tn), lambda i, j, kk: (i, j))
    if out_shapes is None:
        out_shapes = [jax.ShapeDtypeStruct((m, n), dt) for dt in out_dtypes]
    if out_specs is None:
        out_specs = [tile_spec for _ in out_dtypes]
    any_spec = pl.BlockSpec(memory_space=pl.ANY)
    res = pl.pallas_call(
        body,
        grid=grid,
        in_specs=[a_spec, b_spec] + [tile_spec for _ in extras] + [any_spec] * n_si,
        out_specs=list(out_specs) + [any_spec] * n_so,
        out_shape=list(out_shapes) + (side.out_shapes if side else []),
        scratch_shapes=([pltpu.VMEM((tm, tn), F32)] if nk > 1 else []) + (side.scratch() if side else []),
        compiler_params=_cp("arbitrary", "arbitrary", "arbitrary") if side else _cp("parallel", "parallel", "arbitrary"),
        input_output_aliases={2 + n_ex + si: n_out + so for si, so in side.aliases.items()} if side else {},
        name=name,
    )(a, b, *extras, *(side.inputs if side else []))
    return res


def _row_spec(ts, d):
    return pl.BlockSpec((ts, d), lambda i: (i, 0))


def _vec_spec(d):
    return pl.BlockSpec((1, d), lambda i: (0, 0))


def _pre_fwd(x, avec, shift, *, name):
    s, d = x.shape
    ts = min(ROW_TILE, s)

    def body(x_ref, a_ref, s_ref, h_ref):
        xv = x_ref[...]
        r = lax.rsqrt(jnp.mean(xv * xv, axis=-1, keepdims=True) + EPS)
        h_ref[...] = (xv * r * a_ref[...] + s_ref[...]).astype(BF16)

    return pl.pallas_call(
        body, grid=(s // ts,),
        in_specs=[_row_spec(ts, d), _vec_spec(d), _vec_spec(d)],
        out_specs=_row_spec(ts, d),
        out_shape=jax.ShapeDtypeStruct((s, d), BF16),
        compiler_params=_cp("parallel"), name=name,
    )(x, avec, shift)


def _post_pre_fwd(x, y, gate, g, avec, shift, *, name):
    s, d = x.shape
    ts = min(ROW_TILE, s)

    def body(x_ref, y_ref, gate_ref, g_ref, a_ref, s_ref, o_ref, h_ref):
        yv = y_ref[...]
        r = lax.rsqrt(jnp.mean(yv * yv, axis=-1, keepdims=True) + EPS)
        x1 = x_ref[...] + gate_ref[...] * (yv * r * g_ref[...])
        o_ref[...] = x1
        r1 = lax.rsqrt(jnp.mean(x1 * x1, axis=-1, keepdims=True) + EPS)
        h_ref[...] = (x1 * r1 * a_ref[...] + s_ref[...]).astype(BF16)

    return pl.pallas_call(
        body, grid=(s // ts,),
        in_specs=[_row_spec(ts, d), _row_spec(ts, d)] + [_vec_spec(d)] * 4,
        out_specs=[_row_spec(ts, d), _row_spec(ts, d)],
        out_shape=[jax.ShapeDtypeStruct((s, d), F32), jax.ShapeDtypeStruct((s, d), BF16)],
        compiler_params=_cp("parallel"), name=name,
    )(x, y, gate, g, avec, shift)


def _post_bwd_math(dxv, yv, gatev, gv):
    r = lax.rsqrt(jnp.mean(yv * yv, axis=-1, keepdims=True) + EPS)
    yhat = yv * r
    dn = dxv * gatev
    dyhat = dn * gv
    dy = r * (dyhat - yhat * jnp.mean(dyhat * yhat, axis=-1, keepdims=True))
    return dy, dxv * (yhat * gv), dn * yhat


def _accumulate(first, pairs):
    @pl.when(first)
    def _():
        for ref, _ in pairs:
            ref[...] = jnp.zeros_like(ref)

    for ref, val in pairs:
        ref[...] += jnp.sum(val, axis=0, keepdims=True)


def _post_loss_bwd(x, y, gate, g, target, *, name):
    s, d = x.shape
    ts = min(ROW_TILE, s)

    def body(x_ref, y_ref, gate_ref, g_ref, t_ref, dx_ref, dy_ref, loss_ref, dgate_ref, dg_ref):
        yv, gatev, gv = y_ref[...], gate_ref[...], g_ref[...]
        r = lax.rsqrt(jnp.mean(yv * yv, axis=-1, keepdims=True) + EPS)
        diff = x_ref[...] + gatev * (yv * r * gv) - t_ref[...]
        dxv = diff * (1.0 / d)
        dx_ref[...] = dxv
        dy, dgate_rows, dg_rows = _post_bwd_math(dxv, yv, gatev, gv)
        dy_ref[...] = dy.astype(BF16)
        first = pl.program_id(0) == 0
        _accumulate(first, [(dgate_ref, dgate_rows), (dg_ref, dg_rows)])

        @pl.when(first)
        def _():
            loss_ref[...] = jnp.zeros_like(loss_ref)

        loss_ref[...] += jnp.sum(jnp.mean(diff * diff, axis=-1, keepdims=True)) * 0.5

    return pl.pallas_call(
        body, grid=(s // ts,),
        in_specs=[_row_spec(ts, d), _row_spec(ts, d), _vec_spec(d), _vec_spec(d), _row_spec(ts, d)],
        out_specs=[_row_spec(ts, d), _row_spec(ts, d), pl.BlockSpec((1, 128), lambda i: (0, 0)), _vec_spec(d),
                   _vec_spec(d)],
        out_shape=[jax.ShapeDtypeStruct((s, d), F32), jax.ShapeDtypeStruct((s, d), BF16),
                   jax.ShapeDtypeStruct((1, 128), F32), jax.ShapeDtypeStruct((1, d), F32),
                   jax.ShapeDtypeStruct((1, d), F32)],
        compiler_params=_cp("arbitrary"), name=name,
    )(x, y, gate, g, target)


def _pre_post_bwd(dh, xin, dres, avec, y, gate, g, *, name):
    s, d = xin.shape
    ts = min(ROW_TILE, s)

    def body(dh_ref, x_ref, dres_ref, a_ref, y_ref, gate_ref, g_ref, dx_ref, dshift_ref, da_ref, dy_ref,
             dgate_ref, dg_ref):
        xv, dhv = x_ref[...], dh_ref[...]
        r = lax.rsqrt(jnp.mean(xv * xv, axis=-1, keepdims=True) + EPS)
        xhat = xv * r
        dxhat = dhv * a_ref[...]
        dxv = dres_ref[...] + r * (dxhat - xhat * jnp.mean(dxhat * xhat, axis=-1, keepdims=True))
        dx_ref[...] = dxv
        dy, dgate_rows, dg_rows = _post_bwd_math(dxv, y_ref[...], gate_ref[...], g_ref[...])
        dy_ref[...] = dy.astype(BF16)
        _accumulate(pl.program_id(0) == 0, [(dshift_ref, dhv), (da_ref, dhv * xhat), (dgate_ref, dgate_rows),
                                            (dg_ref, dg_rows)])

    return pl.pallas_call(
        body, grid=(s // ts,),
        in_specs=[_row_spec(ts, d), _row_spec(ts, d), _row_spec(ts, d), _vec_spec(d), _row_spec(ts, d),
                  _vec_spec(d), _vec_spec(d)],
        out_specs=[_row_spec(ts, d), _vec_spec(d), _vec_spec(d), _row_spec(ts, d), _vec_spec(d), _vec_spec(d)],
        out_shape=[jax.ShapeDtypeStruct((s, d), F32), jax.ShapeDtypeStruct((1, d), F32),
                   jax.ShapeDtypeStruct((1, d), F32), jax.ShapeDtypeStruct((s, d), BF16),
                   jax.ShapeDtypeStruct((1, d), F32), jax.ShapeDtypeStruct((1, d), F32)],
        compiler_params=_cp("arbitrary"), name=name,
    )(dh, xin, dres, avec, y, gate, g)


def _pre_bwd(dh, xin, dres, avec, *, name):
    s, d = xin.shape
    ts = min(ROW_TILE, s)

    def body(dh_ref, x_ref, dres_ref, a_ref, dx_ref, dshift_ref, da_ref):
        xv, dhv = x_ref[...], dh_ref[...]
        r = lax.rsqrt(jnp.mean(xv * xv, axis=-1, keepdims=True) + EPS)
        xhat = xv * r
        dxhat = dhv * a_ref[...]
        dx_ref[...] = dres_ref[...] + r * (dxhat - xhat * jnp.mean(dxhat * xhat, axis=-1, keepdims=True))

        @pl.when(pl.program_id(0) == 0)
        def _():
            dshift_ref[...] = jnp.zeros_like(dshift_ref)
            da_ref[...] = jnp.zeros_like(da_ref)

        dshift_ref[...] += jnp.sum(dhv, axis=0, keepdims=True)
        da_ref[...] += jnp.sum(dhv * xhat, axis=0, keepdims=True)

    return pl.pallas_call(
        body, grid=(s // ts,),
        in_specs=[_row_spec(ts, d), _row_spec(ts, d), _row_spec(ts, d), _vec_spec(d)],
        out_specs=[_row_spec(ts, d), _vec_spec(d), _vec_spec(d)],
        out_shape=[jax.ShapeDtypeStruct((s, d), F32), jax.ShapeDtypeStruct((1, d), F32),
                   jax.ShapeDtypeStruct((1, d), F32)],
        compiler_params=_cp("arbitrary"), name=name,
    )(dh, xin, dres, avec)


def _tri(n, strict=False, upper=False):
    r = lax.broadcasted_iota(jnp.int32, (n, n), 0)
    c = lax.broadcasted_iota(jnp.int32, (n, n), 1)
    if upper:
        r, c = c, r
    return ((r > c) if strict else (r >= c)).astype(F32)


def _gates_fwd(ps, bf, w2p, b2, *, name):
    s = ps.shape[0]
    ts = min(GATE_TS, s)

    def body(ps_ref, bf_ref, w_ref, b2_ref, cum_ref, la_ref, carry_ref):
        @pl.when(pl.program_id(0) == 0)
        def _():
            carry_ref[...] = jnp.zeros_like(carry_ref)

        psv = ps_ref[...]
        lf = _log_sigmoid(psv + bf_ref[...])
        cum = _dot_nn(_tri(ts), lf, HIGHEST) + carry_ref[...]
        cum_ref[...] = cum
        carry_ref[...] = cum[ts - 1:ts, :]
        z = _dot_nn(psv, w_ref[...], HIGHEST) + b2_ref[...]
        la_ref[...] = _log_sigmoid(z) * (1.0 / GLA_TEMP)

    return pl.pallas_call(
        body, grid=(s // ts,),
        in_specs=[_row_spec(ts, SMALL_W), _vec_spec(SMALL_W),
                  pl.BlockSpec((SMALL_W, GLA_KW), lambda i: (0, 0)), _vec_spec(GLA_KW)],
        out_specs=[_row_spec(ts, SMALL_W), _row_spec(ts, GLA_KW)],
        out_shape=[jax.ShapeDtypeStruct((s, SMALL_W), F32), jax.ShapeDtypeStruct((s, GLA_KW), F32)],
        scratch_shapes=[pltpu.VMEM((1, SMALL_W), F32)],
        compiler_params=_cp("arbitrary"), name=name,
    )(ps, bf, w2p, b2)


def _gates_bwd(dck, ps, bf, w2p, b2, dla, *, name):
    s = ps.shape[0]
    ts = min(GATE_TS, s)
    nb = s // ts
    rev = lambda i: (nb - 1 - i, 0)

    def body(dck_ref, ps_ref, bf_ref, w_ref, b2_ref, dla_ref, dps_ref, dbf_ref, dw_ref, db2_ref, carry_ref):
        @pl.when(pl.program_id(0) == 0)
        def _():
            carry_ref[...] = jnp.zeros_like(carry_ref)
            dbf_ref[...] = jnp.zeros_like(dbf_ref)
            dw_ref[...] = jnp.zeros_like(dw_ref)
            db2_ref[...] = jnp.zeros_like(db2_ref)

        psv, dckv = ps_ref[...], dck_ref[...]
        dlf = _dot_nn(_tri(ts, upper=True), dckv, HIGHEST) + carry_ref[...]
        carry_ref[...] += jnp.sum(dckv, axis=0, keepdims=True)
        lane = lax.broadcasted_iota(jnp.int32, (ts, SMALL_W), 1)
        dff = jnp.where(lane < FOX_HEADS, dlf * _sigmoid(-(psv + bf_ref[...])), 0.0)
        z = _dot_nn(psv, w_ref[...], HIGHEST) + b2_ref[...]
        dz = dla_ref[...] * _sigmoid(-z) * (1.0 / GLA_TEMP)
        dps_ref[...] = (_dot_nt(dz, w_ref[...], HIGHEST) + dff).astype(BF16)
        dbf_ref[...] += jnp.sum(dff, axis=0, keepdims=True)
        dw_ref[...] += _dot_tn(psv, dz, HIGHEST)
        db2_ref[...] += jnp.sum(dz, axis=0, keepdims=True)

    return pl.pallas_call(
        body, grid=(nb,),
        in_specs=[pl.BlockSpec((ts, SMALL_W), rev), pl.BlockSpec((ts, SMALL_W), rev), _vec_spec(SMALL_W),
                  pl.BlockSpec((SMALL_W, GLA_KW), lambda i: (0, 0)), _vec_spec(GLA_KW),
                  pl.BlockSpec((ts, GLA_KW), rev)],
        out_specs=[pl.BlockSpec((ts, SMALL_W), rev), _vec_spec(SMALL_W),
                   pl.BlockSpec((SMALL_W, GLA_KW), lambda i: (0, 0)), _vec_spec(GLA_KW)],
        out_shape=[jax.ShapeDtypeStruct((s, SMALL_W), BF16), jax.ShapeDtypeStruct((1, SMALL_W), F32),
                   jax.ShapeDtypeStruct((SMALL_W, GLA_KW), F32), jax.ShapeDtypeStruct((1, GLA_KW), F32)],
        scratch_shapes=[pltpu.VMEM((1, SMALL_W), F32)],
        compiler_params=_cp("arbitrary"), name=name,
    )(dck, ps, bf, w2p, b2, dla)


def _hs(h, hd=FOX_HD):
    return slice(h * hd, (h + 1) * hd)


def _fox_fwd(proj, cum_t, g_fox, *, name, side=None):
    s = proj.shape[0]
    tq, tk = min(FOX_TQ, s), min(FOX_TK, s)
    scale = FOX_HD ** -0.5
    n_si = len(side.inputs) if side else 0
    n_so = len(side.out_shapes) if side else 0
    grid = (s // tq, s // tk)

    def body(*refs):
        q_ref, k_ref, v_ref, ck_ref, g_ref = refs[:5]
        o_ref, n_ref, lse_ref = refs[5 + n_si:8 + n_si]
        m_sc, acc_sc = refs[8 + n_si + n_so:10 + n_si + n_so]
        i, j = pl.program_id(0), pl.program_id(1)
        if side:
            side.run(refs[5:5 + n_si], refs[8 + n_si:8 + n_si + n_so], refs[10 + n_si + n_so:],
                     (i == 0) & (j == 0), (i == grid[0] - 1) & (j == grid[1] - 1))

        @pl.when(j == 0)
        def _():
            m_sc[...] = jnp.full_like(m_sc, NEG)
            acc_sc[...] = jnp.zeros_like(acc_sc)

        def block(masked):
            mask = _causal_mask(i, j, tq, tk) if masked else None
            ones = jnp.ones((tk, FOX_HD), BF16)
            for h in range(FOX_HEADS):
                sc = _fox_logits(_dot_nt(q_ref[:, _hs(h)], k_ref[:, _hs(h)]), ck_ref[h:h + 1, :], mask, scale)
                m_prev = m_sc[h]
                m_new = jnp.maximum(m_prev, jnp.max(sc, axis=-1, keepdims=True))
                alpha = jnp.exp(m_prev - m_new)
                p = jnp.exp(sc - m_new).astype(BF16)
                v_one = jnp.concatenate([v_ref[:, _hs(h)], ones], axis=1)
                acc_sc[:, _hs(h, 2 * FOX_HD)] = alpha * acc_sc[:, _hs(h, 2 * FOX_HD)] + _dot_nn(p, v_one)
                m_sc[h] = m_new

        pl.when(j < i)(functools.partial(block, False))

        @pl.when(j == i)
        def _():
            block(True)
            lane = lax.broadcasted_iota(jnp.int32, (tq, 128), 1)
            lse = jnp.zeros((tq, 128), F32)
            for h in range(FOX_HEADS):
                l_rep = acc_sc[:, 2 * h * FOX_HD + FOX_HD:2 * (h + 1) * FOX_HD]
                o = acc_sc[:, 2 * h * FOX_HD:2 * h * FOX_HD + FOX_HD] / l_rep
                o_ref[:, _hs(h)] = o
                r = lax.rsqrt(jnp.mean(o * o, axis=-1, keepdims=True) + EPS)
                n_ref[:, _hs(h)] = (o * r * g_ref[h:h + 1, :]).astype(BF16)
                lse = jnp.where(lane == h, m_sc[h] + jnp.log(l_rep), lse)
            lse_ref[...] = lse

    kv = lambda col: (lambda i, j: (jnp.minimum(j, i), col))
    any_spec = pl.BlockSpec(memory_space=pl.ANY)
    return pl.pallas_call(
        body, grid=grid,
        in_specs=[pl.BlockSpec((tq, FOX_W), lambda i, j: (i, 0)),
                  pl.BlockSpec((tk, FOX_W), kv(1)),
                  pl.BlockSpec((tk, FOX_W), kv(2)),
                  pl.BlockSpec((FOX_HEADS, tk), lambda i, j: (0, jnp.minimum(j, i))),
                  pl.BlockSpec((FOX_HEADS, FOX_HD), lambda i, j: (0, 0))] + [any_spec] * n_si,
        out_specs=[pl.BlockSpec((tq, FOX_W), lambda i, j: (i, 0)),
                   pl.BlockSpec((tq, FOX_W), lambda i, j: (i, 0)),
                   pl.BlockSpec((tq, 128), lambda i, j: (i, 0))] + [any_spec] * n_so,
        out_shape=[jax.ShapeDtypeStruct((s, FOX_W), F32), jax.ShapeDtypeStruct((s, FOX_W), BF16),
                   jax.ShapeDtypeStruct((s, 128), F32)] + (side.out_shapes if side else []),
        scratch_shapes=[pltpu.VMEM((FOX_HEADS, tq, 1), F32), pltpu.VMEM((tq, 2 * FOX_W), F32)]
        + (side.scratch() if side else []),
        compiler_params=_cp("arbitrary", "arbitrary"), name=name,
    )(proj, proj, proj, cum_t, g_fox, *(side.inputs if side else []))


def _causal_mask(i, j, tq, tk):
    rows = i * tq + lax.broadcasted_iota(jnp.int32, (tq, tk), 0)
    cols = j * tk + lax.broadcasted_iota(jnp.int32, (tq, tk), 1)
    return rows >= cols


def _fox_logits(qk, ck, mask, scale):
    sc = qk * scale - ck
    return sc if mask is None else jnp.where(mask, sc, NEG)


def _fox_bwd(proj, do, cum_t, lse, delta, *, name, side=None):
    s = proj.shape[0]
    tq, tk = min(FOX_TQ, s), min(FOX_TK, s)
    nk, nq = s // tk, s // tq
    scale = FOX_HD ** -0.5
    n_si = len(side.inputs) if side else 0
    n_so = len(side.out_shapes) if side else 0

    def body(*refs):
        q_ref, k_ref, v_ref, do_ref, ck_ref, lse_ref, dl_ref = refs[:7]
        dq_hbm, dk_ref, dv_ref, dcq_hbm, dck_ref = refs[7 + n_si:12 + n_si]
        dq_sc, dcq_sc, dk_sc, dv_sc, dck_sc, out_sems = refs[12 + n_si + n_so:18 + n_si + n_so]
        j, i = pl.program_id(0), pl.program_id(1)
        if side:
            side.run(refs[7:7 + n_si], refs[12 + n_si:12 + n_si + n_so], refs[18 + n_si + n_so:],
                     (j == 0) & (i == 0), (j == nk - 1) & (i == nq - 1))

        @pl.when((j == 0) & (i == 0))
        def _():
            dq_sc[...] = jnp.zeros_like(dq_sc)
            dcq_sc[...] = jnp.zeros_like(dcq_sc)

        @pl.when(i == 0)
        def _():
            dk_sc[...] = jnp.zeros_like(dk_sc)
            dv_sc[...] = jnp.zeros_like(dv_sc)
            dck_sc[...] = jnp.zeros_like(dck_sc)

        def block(masked):
            mask = _causal_mask(i, j, tq, tk) if masked else None
            qrows = pl.ds(pl.multiple_of(i * tq, tq), tq)
            for h in range(FOX_HEADS):
                sc = _fox_logits(_dot_nt(q_ref[:, _hs(h)], k_ref[:, _hs(h)]), ck_ref[h:h + 1, :], mask, scale)
                p = jnp.exp(sc - lse_ref[:, h:h + 1])
                ds = p * (_dot_nt(do_ref[:, _hs(h)], v_ref[:, _hs(h)]) - dl_ref[:, h:h + 1])
                dsb = ds.astype(BF16)
                dv_sc[:, _hs(h)] += _dot_tn(p.astype(BF16), do_ref[:, _hs(h)])
                dk_sc[:, _hs(h)] += _dot_tn(dsb, q_ref[:, _hs(h)])
                dq_sc[qrows, _hs(h)] += _dot_nn(dsb, k_ref[:, _hs(h)]) * scale
                dck_sc[h:h + 1, :] -= jnp.sum(ds, axis=0, keepdims=True)
                dcq_sc[qrows, h:h + 1] += jnp.sum(ds, axis=-1, keepdims=True)

        pl.when(i > j)(functools.partial(block, False))
        pl.when(i == j)(functools.partial(block, True))

        @pl.when(i == nq - 1)
        def _():
            dk_ref[...] = (dk_sc[...] * scale).astype(BF16)
            dv_ref[...] = dv_sc[...].astype(BF16)
            dck_ref[...] = dck_sc[...]

        @pl.when((j == nk - 1) & (i == nq - 1))
        def _():
            out_q = pltpu.make_async_copy(dq_sc, dq_hbm, out_sems.at[0])
            out_c = pltpu.make_async_copy(dcq_sc, dcq_hbm, out_sems.at[1])
            out_q.start()
            out_c.start()
            out_q.wait()
            out_c.wait()

    qrow = lambda j, i: (jnp.maximum(i, j), 0)
    krow = lambda col: (lambda j, i: (j, col))
    any_spec = pl.BlockSpec(memory_space=pl.ANY)
    return pl.pallas_call(
        body, grid=(nk, nq),
        in_specs=[pl.BlockSpec((tq, FOX_W), qrow), pl.BlockSpec((tk, FOX_W), krow(1)),
                  pl.BlockSpec((tk, FOX_W), krow(2)),
                  pl.BlockSpec((tq, FOX_W), qrow),
                  pl.BlockSpec((FOX_HEADS, tk), lambda j, i: (0, j)),
                  pl.BlockSpec((tq, 128), qrow), pl.BlockSpec((tq, 128), qrow)] + [any_spec] * n_si,
        out_specs=[any_spec, pl.BlockSpec((tk, FOX_W), lambda j, i: (j, 0)),
                   pl.BlockSpec((tk, FOX_W), lambda j, i: (j, 0)), any_spec,
                   pl.BlockSpec((FOX_HEADS, tk), lambda j, i: (0, j))] + [any_spec] * n_so,
        out_shape=[jax.ShapeDtypeStruct((s, FOX_W), F32), jax.ShapeDtypeStruct((s, FOX_W), BF16),
                   jax.ShapeDtypeStruct((s, FOX_W), BF16), jax.ShapeDtypeStruct((s, 128), F32),
                   jax.ShapeDtypeStruct((FOX_HEADS, s), F32)] + (side.out_shapes if side else []),
        scratch_shapes=[pltpu.VMEM((s, FOX_W), F32), pltpu.VMEM((s, 128), F32),
                        pltpu.VMEM((tk, FOX_W), F32), pltpu.VMEM((tk, FOX_W), F32), pltpu.VMEM((FOX_HEADS, tk), F32),
                        pltpu.SemaphoreType.DMA((2,))] + (side.scratch() if side else []),
        compiler_params=_cp("arbitrary", "arbitrary"), name=name,
    )(proj, proj, proj, do, cum_t, lse, delta, *(side.inputs if side else []))


def _head_norm_bwd(dn_in, o, g, gr_src, *, nh, hd, dn_col, gr_col, name):
    s, w = o.shape
    ts = min(ROW_TILE, s)
    gated = gr_src is not None

    def body(*refs):
        if gated:
            dn_ref, o_ref, g_ref, gr_ref, do_ref, dgr_ref, dl_ref, dg_ref = refs
        else:
            dn_ref, o_ref, g_ref, do_ref, dl_ref, dg_ref = refs

        @pl.when(pl.program_id(0) == 0)
        def _():
            dg_ref[...] = jnp.zeros_like(dg_ref)

        lane = lax.broadcasted_iota(jnp.int32, (ts, 128), 1)
        delta = jnp.zeros((ts, 128), F32)
        for h in range(nh):
            sl = _hs(h, hd)
            ov = o_ref[:, sl]
            dnv = dn_ref[:, sl].astype(F32)
            gv = g_ref[h:h + 1, :]
            r = lax.rsqrt(jnp.mean(ov * ov, axis=-1, keepdims=True) + EPS)
            ohat = ov * r
            if gated:
                grv = gr_ref[:, sl].astype(F32)
                sig = _sigmoid(grv)
                dgr_ref[:, sl] = (dnv * (ohat * gv) * (sig * (1.0 + grv * (1.0 - sig)))).astype(BF16)
                dnv = dnv * (grv * sig)
            dg_ref[h:h + 1, :] += jnp.sum(dnv * ohat, axis=0, keepdims=True)
            dohat = dnv * gv
            do = r * (dohat - ohat * jnp.mean(dohat * ohat, axis=-1, keepdims=True))
            do_ref[:, sl] = do.astype(BF16)
            delta = jnp.where(lane == h, jnp.sum(do.astype(BF16).astype(F32) * ov, axis=-1, keepdims=True), delta)
        dl_ref[...] = delta

    in_specs = [pl.BlockSpec((ts, w), lambda i: (i, dn_col)), _row_spec(ts, w),
                pl.BlockSpec((nh, hd), lambda i: (0, 0))]
    args = [dn_in, o, g]
    out_specs = [_row_spec(ts, w)]
    out_shape = [jax.ShapeDtypeStruct((s, w), BF16)]
    if gated:
        in_specs.append(pl.BlockSpec((ts, w), lambda i: (i, gr_col)))
        args.append(gr_src)
        out_specs.append(_row_spec(ts, w))
        out_shape.append(jax.ShapeDtypeStruct((s, w), BF16))
    out_specs += [_row_spec(ts, 128), pl.BlockSpec((nh, hd), lambda i: (0, 0))]
    out_shape += [jax.ShapeDtypeStruct((s, 128), F32), jax.ShapeDtypeStruct((nh, hd), F32)]
    return pl.pallas_call(
        body, grid=(s // ts,), in_specs=in_specs, out_specs=out_specs, out_shape=out_shape,
        compiler_params=_cp("arbitrary"), name=name,
    )(*args)


GQ_BLK = 3 * FOX_W // GLA_DK
GK_BLK = GQ_BLK + GLA_HEADS
GV_BLK = (3 * FOX_W + 2 * GLA_KW) // GLA_DV
GR_BLK = GV_BLK + GLA_HEADS


def _gla_chunk_terms(la):
    cum = _dot_nn(_tri(CHUNK), la, HIGHEST)
    total = cum[CHUNK - 1:CHUNK, :]
    return jnp.exp(total - cum), jnp.exp(total)


def _gla_fwd(proj, log_a, g_gla, *, name):
    s = proj.shape[0]
    rows = min(GLA_ROWS, s)
    cb = rows // CHUNK
    nblk = s // rows
    scale = GLA_DK ** -0.5

    def body(q_ref, k_ref, v_ref, gr_ref, la_ref, g_ref, o_ref, n_ref, st_ref, st_sc):
        h = pl.program_id(0)

        @pl.when(pl.program_id(1) == 0)
        def _():
            st_sc[...] = jnp.zeros_like(st_sc)

        gv = g_ref[pl.ds(h, 1), :]
        for ci in range(cb):
            sl = slice(ci * CHUNK, (ci + 1) * CHUNK)
            e, dec = _gla_chunk_terms(la_ref[sl, :])
            k_dec = (k_ref[sl, :].astype(F32) * e).astype(BF16)
            st = st_sc[...] * dec + _dot_tn(v_ref[sl, :], k_dec)
            st_sc[...] = st
            st_ref[0, ci] = st
            qs = (q_ref[sl, :].astype(F32) * scale).astype(BF16)
            o = _dot_nt(qs, st.astype(BF16))
            o_ref[sl, :] = o
            r = lax.rsqrt(jnp.mean(o * o, axis=-1, keepdims=True) + EPS)
            grv = gr_ref[sl, :].astype(F32)
            n_ref[sl, :] = (o * r * gv * (grv * _sigmoid(grv))).astype(BF16)

    return pl.pallas_call(
        body, grid=(GLA_HEADS, nblk),
        in_specs=[pl.BlockSpec((rows, GLA_DK), lambda h, n: (n, GQ_BLK + h)),
                  pl.BlockSpec((rows, GLA_DK), lambda h, n: (n, GK_BLK + h)),
                  pl.BlockSpec((rows, GLA_DV), lambda h, n: (n, GV_BLK + h)),
                  pl.BlockSpec((rows, GLA_DV), lambda h, n: (n, GR_BLK + h)),
                  pl.BlockSpec((rows, GLA_DK), lambda h, n: (n, h)),
                  pl.BlockSpec((GLA_HEADS, GLA_DV), lambda h, n: (0, 0))],
        out_specs=[pl.BlockSpec((rows, GLA_DV), lambda h, n: (n, h)),
                   pl.BlockSpec((rows, GLA_DV), lambda h, n: (n, h)),
                   pl.BlockSpec((1, cb, GLA_DV, GLA_DK), lambda h, n: (h, n, 0, 0))],
        out_shape=[jax.ShapeDtypeStruct((s, GLA_W), F32), jax.ShapeDtypeStruct((s, GLA_W), BF16),
                   jax.ShapeDtypeStruct((GLA_HEADS, s // CHUNK, GLA_DV, GLA_DK), F32)],
        scratch_shapes=[pltpu.VMEM((GLA_DV, GLA_DK), F32)],
        compiler_params=_cp("parallel", "arbitrary"), name=name,
    )(proj, proj, proj, proj, log_a, g_gla)


def _gla_bwd(proj, log_a, do, states, *, name, side=None):
    s = proj.shape[0]
    rows = min(GLA_ROWS, s)
    cb = rows // CHUNK
    nblk = s // rows
    scale = GLA_DK ** -0.5
    n_si = len(side.inputs) if side else 0
    n_so = len(side.out_shapes) if side else 0

    def body(*refs):
        q_ref, k_ref, v_ref, la_ref, do_ref, st_ref, prev_ref = refs[:7]
        dq_ref, dk_ref, dv_ref, dla_ref = refs[7 + n_si:11 + n_si]
        g_sc = refs[11 + n_si + n_so]
        nrev = pl.program_id(1)
        blk = nblk - 1 - nrev
        if side:
            hh = pl.program_id(0)
            side.run(refs[7:7 + n_si], refs[11 + n_si:11 + n_si + n_so], refs[12 + n_si + n_so:],
                     (hh == 0) & (nrev == 0), (hh == GLA_HEADS - 1) & (nrev == nblk - 1))

        @pl.when(nrev == 0)
        def _():
            g_sc[...] = jnp.zeros_like(g_sc)

        for ci in reversed(range(cb)):
            sl = slice(ci * CHUNK, (ci + 1) * CHUNK)
            e, dec = _gla_chunk_terms(la_ref[sl, :])
            kd = k_ref[sl, :].astype(F32) * e
            qs = (q_ref[sl, :].astype(F32) * scale).astype(BF16)
            dov = do_ref[sl, :]
            st = st_ref[0, ci]
            if ci > 0:
                st_prev = st_ref[0, ci - 1]
            else:
                st_prev = prev_ref[0, 0] * (blk > 0).astype(F32)
            dq_ref[sl, :] = (_dot_nn(dov, st.astype(BF16)) * scale).astype(BF16)
            gt = g_sc[...] + _dot_tn(dov, qs)
            gtb = gt.astype(BF16)
            dkd = _dot_nn(v_ref[sl, :], gtb)
            dv_ref[sl, :] = _dot_nt(kd.astype(BF16), gtb).astype(BF16)
            dk_ref[sl, :] = (dkd * e).astype(BF16)
            ddec = jnp.sum(gt * st_prev, axis=0, keepdims=True) * dec
            dla_ref[sl, :] = _dot_nn(_tri(CHUNK, strict=True), dkd * kd, HIGHEST) + ddec
            g_sc[...] = gt * dec

    rev = lambda col0: (lambda h, n: (nblk - 1 - n, col0 + h))
    return pl.pallas_call(
        body, grid=(GLA_HEADS, nblk),
        in_specs=[pl.BlockSpec((rows, GLA_DK), rev(GQ_BLK)),
                  pl.BlockSpec((rows, GLA_DK), rev(GK_BLK)),
                  pl.BlockSpec((rows, GLA_DV), rev(GV_BLK)),
                  pl.BlockSpec((rows, GLA_DK), rev(0)),
                  pl.BlockSpec((rows, GLA_DV), rev(0)),
                  pl.BlockSpec((1, cb, GLA_DV, GLA_DK), lambda h, n: (h, nblk - 1 - n, 0, 0)),
                  pl.BlockSpec((1, 1, GLA_DV, GLA_DK),
                               lambda h, n: (h, jnp.maximum((nblk - 1 - n) * cb - 1, 0), 0, 0))]
        + [pl.BlockSpec(memory_space=pl.ANY)] * n_si,
        out_specs=[pl.BlockSpec((rows, GLA_DK), rev(0)), pl.BlockSpec((rows, GLA_DK), rev(0)),
                   pl.BlockSpec((rows, GLA_DV), rev(0)), pl.BlockSpec((rows, GLA_DK), rev(0))]
        + [pl.BlockSpec(memory_space=pl.ANY)] * n_so,
        out_shape=[jax.ShapeDtypeStruct((s, GLA_KW), BF16), jax.ShapeDtypeStruct((s, GLA_KW), BF16),
                   jax.ShapeDtypeStruct((s, GLA_W), BF16), jax.ShapeDtypeStruct((s, GLA_KW), F32)]
        + (side.out_shapes if side else []),
        scratch_shapes=[pltpu.VMEM((GLA_DV, GLA_DK), F32)] + (side.scratch() if side else []),
        compiler_params=_cp("arbitrary", "arbitrary"), name=name,
    )(proj, proj, proj, log_a, do, states, states, *(side.inputs if side else []))


def _row_tile(r):
    tr = min(ROW_TILE, r)
    while r % tr or tr % 8:
        tr -= 1
    return tr


def _adamw_math(w, g, m, v):
    m = ADAM_B1 * m + (1.0 - ADAM_B1) * g
    v = ADAM_B2 * v + (1.0 - ADAM_B2) * (g * g)
    m_hat = m / (1.0 - ADAM_B1 ** ADAM_STEP)
    v_hat = v / (1.0 - ADAM_B2 ** ADAM_STEP)
    delta = -ADAM_LR * (m_hat / (jnp.sqrt(v_hat) + ADAM_EPS) + ADAM_WD * w)
    return delta, m, v


COL_TILE = 256


def _tile_2d(r, c):
    if r % 8 == 0:
        return _row_tile(r), c
    assert c % COL_TILE == 0, (r, c)
    return r, COL_TILE


def _half_shape(shape):
    r, c = shape[-2:]
    return tuple(shape[:-2]) + ((r // 2, c) if _half_axis(r) == 0 else (r, c // 2))


def _adam(g, w, m, v, *, name):
    r, c = w.shape
    tr, tc = _tile_2d(r, c)

    def body(g_ref, w_ref, m_ref, v_ref, d_ref, mo_ref, vo_ref):
        d, mn, vn = _adamw_math(w_ref[...], g_ref[...], m_ref[...], v_ref[...])
        d_ref[...] = d
        mo_ref[...] = mn
        vo_ref[...] = vn

    spec = pl.BlockSpec((tr, tc), lambda i, j: (i, j))
    return pl.pallas_call(
        body, grid=(r // tr, c // tc), in_specs=[spec] * 4, out_specs=[spec] * 3,
        out_shape=[jax.ShapeDtypeStruct((r, c), F32)] * 3,
        compiler_params=_cp("parallel", "parallel"), name=name,
    )(g, w, m, v)


def _ada_grad_adam(c_all_t, dmod_cols, w, m, v, *, name):
    r, c = w.shape
    tr, tc = min(512, r), min(1024, c)

    def body(ct_ref, dm_ref, w_ref, m_ref, v_ref, g_ref, d_ref, mo_ref, vo_ref):
        g = _dot_nn(ct_ref[...], dm_ref[...], HIGHEST)
        g_ref[...] = g
        d, mn, vn = _adamw_math(w_ref[...], g, m_ref[...], v_ref[...])
        d_ref[...] = d
        mo_ref[...] = mn
        vo_ref[...] = vn

    spec = pl.BlockSpec((tr, tc), lambda i, j: (i, j))
    nb = c_all_t.shape[1]
    return pl.pallas_call(
        body, grid=(r // tr, c // tc),
        in_specs=[pl.BlockSpec((tr, nb), lambda i, j: (i, 0)), pl.BlockSpec((nb, tc), lambda i, j: (0, j)),
                  spec, spec, spec],
        out_specs=[spec] * 4, out_shape=[jax.ShapeDtypeStruct((r, c), F32)] * 4,
        compiler_params=_cp("parallel", "parallel"), name=name,
    )(c_all_t, dmod_cols, w, m, v)


def _mod_shard(c_all, w, b, *, name):
    k, c = w.shape
    tc = min(512, c)
    nb = c_all.shape[0]

    def body(c_ref, w_ref, b_ref, o_ref):
        o_ref[...] = _dot_nn(c_ref[...], w_ref[...], HIGHEST) + b_ref[...]

    return pl.pallas_call(
        body, grid=(c // tc,),
        in_specs=[pl.BlockSpec((nb, k), lambda j: (0, 0)), pl.BlockSpec((k, tc), lambda j: (0, j)),
                  pl.BlockSpec((1, tc), lambda j: (0, j))],
        out_specs=pl.BlockSpec((nb, tc), lambda j: (0, j)),
        out_shape=jax.ShapeDtypeStruct((nb, c), F32),
        compiler_params=_cp("parallel"), name=name,
    )(c_all, w, b)


def _silu_rows(c, *, name):
    def body(c_ref, o_ref):
        cv = c_ref[...]
        o_ref[...] = cv * _sigmoid(cv)

    return pl.pallas_call(body, out_shape=jax.ShapeDtypeStruct(c.shape, F32), name=name)(c)


def _pair_sum(g, got, idx, *, name):
    p, r, c = g.shape
    ax = _half_axis(r)
    hr, hc = _half_shape((r, c))
    tr, tc = _tile_2d(hr, hc)
    nbr, nbc = hr // tr, hc // tc

    def body(idx_ref, a_ref, b_ref, o_ref):
        o_ref[...] = (a_ref[...].astype(F32) + b_ref[...].astype(F32)).astype(BF16)

    def own_map(i, j, k, idx_ref):
        return (i, j + (idx_ref[0] * nbr if ax == 0 else 0), k + (idx_ref[0] * nbc if ax == 1 else 0))

    half_spec = pl.BlockSpec((1, tr, tc), lambda i, j, k, idx_ref: (i, j, k))
    return pl.pallas_call(
        body,
        grid_spec=pltpu.PrefetchScalarGridSpec(
            num_scalar_prefetch=1, grid=(p, nbr, nbc),
            in_specs=[pl.BlockSpec((1, tr, tc), own_map), half_spec],
            out_specs=half_spec),
        out_shape=jax.ShapeDtypeStruct((p, hr, hc), BF16),
        compiler_params=_cp("parallel", "parallel", "parallel"), name=name,
    )(idx, g, got)


def _final_sum(own, parts, idx, shard_shape, *, name):
    ax = _half_axis(shard_shape[0])
    hr, hc = own.shape[1:]
    tr, tc = _tile_2d(hr, hc)
    nbr, nbc = hr // tr, hc // tc

    def body(idx_ref, own_ref, parts_ref, o_ref):
        acc = own_ref[0].astype(F32)
        for q in range(3):
            acc = acc + parts_ref[q].astype(F32)
        o_ref[...] = acc

    def out_map(j, k, idx_ref):
        return (j + (idx_ref[0] * nbr if ax == 0 else 0), k + (idx_ref[0] * nbc if ax == 1 else 0))

    return pl.pallas_call(
        body,
        grid_spec=pltpu.PrefetchScalarGridSpec(
            num_scalar_prefetch=1, grid=(nbr, nbc),
            in_specs=[pl.BlockSpec((1, tr, tc), lambda j, k, idx_ref: (idx_ref[1], j, k)),
                      pl.BlockSpec((3, tr, tc), lambda j, k, idx_ref: (0, j, k))],
            out_specs=pl.BlockSpec((tr, tc), out_map)),
        out_shape=jax.ShapeDtypeStruct(tuple(shard_shape), F32),
        compiler_params=_cp("parallel", "parallel"), name=name,
    )(idx, own, parts)


def _stack_sum(x, *, name):
    p, r, c = x.shape
    tr = _row_tile(r)

    def body(x_ref, o_ref):
        acc = x_ref[0].astype(F32)
        for q in range(1, p):
            acc = acc + x_ref[q].astype(F32)
        o_ref[...] = acc

    return pl.pallas_call(
        body, grid=(r // tr,),
        in_specs=[pl.BlockSpec((p, tr, c), lambda i: (0, i, 0))],
        out_specs=pl.BlockSpec((tr, c), lambda i: (i, 0)),
        out_shape=jax.ShapeDtypeStruct((r, c), F32),
        compiler_params=_cp("parallel"), name=name,
    )(x)


def _place():
    x, y, c = lax.axis_index("x"), lax.axis_index("y"), lax.axis_index("c")
    chips = [(1 - x, y), (x, 1 - y), (1 - x, 1 - y)]
    return x, y, c, chips


def _gather8(x_shard, *, name):
    m_per, n = x_shard.shape

    def body(x_ref, out_ref, send_sems, recv_sems, local_sem):
        x, y, c, chips = _place()
        me, sibling = (x, y, c), (x, y, 1 - c)

        def rows(px, py, pc):
            return out_ref.at[pl.ds((4 * px + 2 * py + pc) * m_per, m_per), :]

        def copy(k, block, to, src=None):
            return pltpu.make_async_remote_copy(
                src_ref=rows(*block) if src is None else src, dst_ref=rows(*block),
                send_sem=send_sems.at[k], recv_sem=recv_sems.at[k], device_id=to, device_id_type=MESH)

        mine = pltpu.make_async_copy(x_ref, rows(*me), local_sem)
        mine.start()
        first = [copy(0, me, sibling, src=x_ref)]
        first += [copy(1 + j, me, (*chip, c), src=x_ref) for j, chip in enumerate(chips)]
        for cp in first:
            cp.start()
        passed = [copy(4 + j, (*chip, c), sibling) for j, chip in enumerate(chips)]
        for j, chip in enumerate(chips):
            copy(1 + j, (*chip, c), me).wait_recv()
            passed[j].start()
        copy(0, sibling, me).wait_recv()
        for j, chip in enumerate(chips):
            copy(4 + j, (*chip, 1 - c), me).wait_recv()
        for cp in first + passed:
            cp.wait_send()
        mine.wait()

    return pl.pallas_call(
        body,
        out_shape=jax.ShapeDtypeStruct((8 * m_per, n), x_shard.dtype),
        in_specs=[pl.BlockSpec(memory_space=pltpu.VMEM)],
        out_specs=pl.BlockSpec(memory_space=pltpu.VMEM),
        scratch_shapes=[pltpu.SemaphoreType.DMA((7,)), pltpu.SemaphoreType.DMA((7,)), pltpu.SemaphoreType.DMA],
        name=name,
    )(x_shard)


def _gather_weights(shards, *, name):
    return _comm_call(lambda ins, outs: [cp for i, o in zip(ins, outs) for cp in _plan_gather_ici(i, o)],
                      shards, [jax.ShapeDtypeStruct((4,) + s.shape, s.dtype) for s in shards], name=name)


def _plan_start(plan, send_sems, recv_sems):
    for k, (src, dst, _, peer) in enumerate(plan):
        pltpu.make_async_remote_copy(src_ref=src, dst_ref=dst, send_sem=send_sems.at[k], recv_sem=recv_sems.at[k],
                                     device_id=peer, device_id_type=MESH).start()


def _plan_wait(plan, send_sems, recv_sems):
    for k, (src, _, land, peer) in enumerate(plan):
        pltpu.make_async_remote_copy(src_ref=src, dst_ref=land, send_sem=send_sems.at[k], recv_sem=recv_sems.at[k],
                                     device_id=peer, device_id_type=MESH).wait_recv()
    for k, (src, dst, _, peer) in enumerate(plan):
        pltpu.make_async_remote_copy(src_ref=src, dst_ref=dst, send_sem=send_sems.at[k], recv_sem=recv_sems.at[k],
                                     device_id=peer, device_id_type=MESH).wait_send()


def _half_axis(rows):
    return 0 if rows % 32 == 0 else 1


def _rows_half(ref, hc, axis, part=None):
    size = ref.shape[axis] // 2
    start = hc * size
    if part is not None:
        size //= part[1]
        start = start + part[0] * size
    idx = [slice(None)] * len(ref.shape)
    idx[axis] = pl.ds(start, size)
    return ref.at[tuple(idx)]


def _plan_gather_ici(shard, full, part=None):
    x, y, c, chips = _place()
    ax = _half_axis(shard.shape[0])
    src = _rows_half(shard, c, ax, part)
    return [(src, _rows_half(full.at[2 * x + y], c, ax, part), _rows_half(full.at[2 * cx + cy], c, ax, part),
             (cx, cy, c)) for cx, cy in chips]


def _plan_gather_d2d(full):
    x, y, c, chips = _place()
    ax = _half_axis(full.shape[1])
    plan = []
    for cx, cy in chips:
        slot = full.at[2 * cx + cy]
        plan.append((_rows_half(slot, c, ax), _rows_half(slot, c, ax), _rows_half(slot, 1 - c, ax), (x, y, 1 - c)))
    return plan


def _plan_pair(grad, got):
    x, y, c, _ = _place()
    return [(_rows_half(grad, 1 - c, 1 + _half_axis(grad.shape[1])), got, got, (x, y, 1 - c))]


def _plan_shard_ici(sums, parts):
    _, _, c, chips = _place()
    return [(sums.at[2 * cx + cy], parts.at[k], parts.at[k], (cx, cy, c)) for k, (cx, cy) in enumerate(chips)]


def _plan_half(buf):
    x, y, c, _ = _place()
    ax = _half_axis(buf.shape[0])
    mine = _rows_half(buf, c, ax)
    return [(mine, mine, _rows_half(buf, 1 - c, ax), (x, y, 1 - c))]


def _comm_call(plan_fn, inputs, out_shapes, *, name, aliases=None):
    ni, no = len(inputs), len(out_shapes)

    def body(*refs):
        plan = plan_fn(refs[:ni], refs[ni:ni + no])
        send_sems, recv_sems = refs[ni + no:]
        _plan_start(plan, send_sems, recv_sems)
        _plan_wait(plan, send_sems, recv_sems)

    any_spec = pl.BlockSpec(memory_space=pl.ANY)
    n_copies = 3 * max(ni, no)
    return pl.pallas_call(
        body, out_shape=list(out_shapes), in_specs=[any_spec] * ni, out_specs=[any_spec] * no,
        scratch_shapes=[pltpu.SemaphoreType.DMA((n_copies,)), pltpu.SemaphoreType.DMA((n_copies,))],
        input_output_aliases=aliases or {}, name=name,
    )(*inputs)


def _gather_forward(fulls, *, name):
    return _comm_call(lambda ins, outs: [cp for o in outs for cp in _plan_gather_d2d(o)],
                      fulls, [jax.ShapeDtypeStruct(f.shape, f.dtype) for f in fulls], name=name,
                      aliases={k: k for k in range(len(fulls))})


def _pair_exchange(grads, *, name):
    return _comm_call(lambda ins, outs: [cp for i, o in zip(ins, outs) for cp in _plan_pair(i, o)],
                      grads, [jax.ShapeDtypeStruct(_half_shape(g.shape), g.dtype) for g in grads], name=name)


def _half_exchange(bufs, *, name):
    return _comm_call(lambda ins, outs: [cp for o in outs for cp in _plan_half(o)],
                      bufs, [jax.ShapeDtypeStruct(b.shape, b.dtype) for b in bufs], name=name,
                      aliases={k: k for k in range(len(bufs))})


def _split_w_in(w_in_t):
    d = w_in_t.shape[1]
    main = jnp.concatenate([w_in_t[0:3072], w_in_t[3080:5128], w_in_t[5144:6168]], axis=0)
    small = jnp.concatenate([w_in_t[3072:3080], w_in_t[5128:5144], jnp.zeros((SMALL_W - 24, d), w_in_t.dtype)], axis=0)
    return main, small


def _merge_dw_in(dw_main, dw_small):
    return jnp.concatenate([dw_main[0:3072], dw_small[0:8], dw_main[3072:5120], dw_small[8:24], dw_main[5120:6144]],
                           axis=0)


def _gather_side(shards):
    return _Side(shards, [jax.ShapeDtypeStruct((4,) + w.shape, w.dtype) for w in shards],
                 lambda ins, outs: [cp for i, o in zip(ins, outs) for cp in _plan_gather_ici(i, o)], 3 * len(shards))


def _finish_gather(fulls, owns, chip, *, name):
    fulls = _gather_forward(list(fulls), name=name)
    return [lax.dynamic_update_index_in_dim(f, o, chip, 0) for f, o in zip(fulls, owns)]


def _shard_side(sums):
    return _Side([sums], [jax.ShapeDtypeStruct((3,) + sums.shape[1:], sums.dtype)],
                 lambda ins, outs: _plan_shard_ici(ins[0], outs[0]), 3)


def _chip_sum(grad, idx, tag):
    got, = _pair_exchange([grad], name=f"grad_pair_exchange_{tag}")
    return _pair_sum(grad, got, idx, name=f"grad_pair_sum_{tag}")


def _local_step(x, target, mod, g_pre_mix, g_post_mix, g_pre_mlp, g_post_mlp, w_in_t, b_fgate, w_gla_a2,
                b_gla_a2, g_fox, g_gla, own_w_out, own_w_mlp_in, own_w_mlp_out, chip, idx):
    s, d = x.shape
    shift_m, scale_m, gate_m, shift_f, scale_f, gate_f = [mod[:, i * d:(i + 1) * d] for i in range(6)]
    a1 = g_pre_mix * (1.0 + scale_m)
    a2 = g_pre_mlp * (1.0 + scale_f)
    w_main, w_small = _split_w_in(w_in_t)
    bf = jnp.concatenate([b_fgate, jnp.zeros((1, SMALL_W - FOX_HEADS), F32)], axis=1)
    w2p = jnp.zeros((SMALL_W, GLA_KW), F32).at[FOX_HEADS:FOX_HEADS + GLA_RANK].set(w_gla_a2)

    h1 = _pre_fwd(x, a1, shift_m, name="pre_mix_fwd")
    full_shape = lambda w: jax.ShapeDtypeStruct((4,) + w.shape, w.dtype)
    first_side = _Side(
        [own_w_out, own_w_mlp_out], [full_shape(own_w_out), full_shape(own_w_mlp_out)],
        lambda ins, outs: _plan_gather_ici(ins[0], outs[0]) + _plan_gather_ici(ins[1], outs[1], part=(0, 2)), 6)
    proj, gw_out, gw_mlp_out = _mm(h1, w_main, mode="nt", out_dtypes=[BF16], name="in_proj_main", side=first_side)
    ps, = _mm(h1, w_small, mode="nt", out_dtypes=[F32], name="in_proj_small")
    gw_out, = _finish_gather([gw_out], [own_w_out], chip, name="gather_w_out_d2d")
    w_out_full = gw_out.reshape(-1, d)
    cum, log_a = _gates_fwd(ps, bf, w2p, b_gla_a2, name="gates_fwd")
    cum_t = cum[:, :FOX_HEADS].T
    o_fox, fox_n, lse, gw_mlp_in = _fox_fwd(proj, cum_t, g_fox, name="fox_fwd", side=_gather_side([own_w_mlp_in]))
    gw_mlp_in, = _finish_gather([gw_mlp_in], [own_w_mlp_in], chip, name="gather_w_mlp_in_d2d")
    o_gla, gla_n, states = _gla_fwd(proj, log_a, g_gla, name="gla_fwd")
    mixed = jnp.concatenate([fox_n, gla_n], axis=1)
    y1, = _mm(mixed, w_out_full, mode="nn", out_dtypes=[F32], name="out_proj")
    x1, h2 = _post_pre_fwd(x, y1, gate_m, g_post_mix, a2, shift_f, name="post_mix_pre_mlp_fwd")

    def mlp_act(acc):
        r = jnp.maximum(acc, 0.0)
        return acc, r * r

    rest_side = _Side([own_w_mlp_out, gw_mlp_out], [full_shape(own_w_mlp_out)],
                      lambda ins, outs: _plan_gather_ici(ins[0], outs[0], part=(1, 2)), 3, aliases={1: 0})
    u, act, gw_mlp_out = _mm(h2, gw_mlp_in, mode="nn", out_dtypes=[BF16, BF16], epi=mlp_act, name="mlp_in",
                             b_slots=4, tm=MM_TM, side=rest_side)
    gw_mlp_out, = _finish_gather([gw_mlp_out], [own_w_mlp_out], chip, name="gather_w_mlp_out_d2d")
    w_mlp_out_full = gw_mlp_out.reshape(-1, d)
    y2, = _mm(act, w_mlp_out_full, mode="nn", out_dtypes=[F32], name="mlp_out")
    dx2, dy2, loss_part, dgate_f, dg_post_mlp = _post_loss_bwd(x1, y2, gate_f, g_post_mlp, target,
                                                               name="post_mlp_loss_bwd")
    dw_mlp_out, = _mm(act, dy2, mode="tn", out_dtypes=[BF16], name="dw_mlp_out")
    sum_mlp_out = _chip_sum(dw_mlp_out.reshape(4, D_FF // 4, d), idx, "mlp_out")

    def act_bwd(acc, uv):
        return (acc * (2.0 * jnp.maximum(uv.astype(F32), 0.0)),)

    du, = _mm(dy2, w_mlp_out_full, mode="nt", out_dtypes=[BF16], extras=[u], epi=act_bwd, name="d_mlp_hidden",
              tm=MM_TM)
    nj = D_FF // 4 // min(MM_T, D_FF // 4)
    tmw = min(MM_T, d)
    dw_mlp_in, parts_mlp_out = _mm(
        h2, du, mode="tn", out_dtypes=[BF16], name="dw_mlp_in",
        out_shapes=[jax.ShapeDtypeStruct((4, d, D_FF // 4), BF16)],
        out_specs=[pl.BlockSpec((1, tmw, min(MM_T, D_FF // 4)), lambda i, j, kk: (j // nj, i, j % nj))],
        side=_shard_side(sum_mlp_out))
    sum_mlp_in = _chip_sum(dw_mlp_in, idx, "mlp_in")
    dh2, = _mm(du, gw_mlp_in, mode="nt", out_dtypes=[F32], name="d_mlp_in", b_slots=4)
    dx1, dshift_f, da2, dy1, dgate_m, dg_post_mix = _pre_post_bwd(dh2, x1, dx2, a2, y1, gate_m, g_post_mix,
                                                                  name="pre_mlp_post_mix_bwd")
    dw_out, = _mm(mixed, dy1, mode="tn", out_dtypes=[BF16], name="dw_out")
    sum_out = _chip_sum(dw_out.reshape(4, d // 4, d), idx, "out")
    dmixed, = _mm(dy1, w_out_full, mode="nt", out_dtypes=[BF16], name="d_mixed")
    do_fox, delta, dg_fox = _head_norm_bwd(dmixed, o_fox, g_fox, None, nh=FOX_HEADS, hd=FOX_HD, dn_col=0,
                                           gr_col=0, name="fox_norm_bwd")
    do_gla, dgr, _, dg_gla = _head_norm_bwd(dmixed, o_gla, g_gla, proj, nh=GLA_HEADS, hd=GLA_DV, dn_col=1,
                                            gr_col=(3 * FOX_W + 2 * GLA_KW + GLA_W) // GLA_W, name="gla_norm_bwd")
    dq_fox, dk_fox, dv_fox, dcq, dck_t, parts_mlp_in = _fox_bwd(proj, do_fox, cum_t, lse, delta, name="fox_bwd",
                                                                side=_shard_side(sum_mlp_in))
    dgq, dgk, dgv, dla, parts_out = _gla_bwd(proj, log_a, do_gla, states, name="gla_bwd", side=_shard_side(sum_out))
    dck = dcq + jnp.concatenate([dck_t.T, jnp.zeros((s, SMALL_W - FOX_HEADS), F32)], axis=1)
    dps, dbf, dw2p, db2 = _gates_bwd(dck, ps, bf, w2p, b_gla_a2, dla, name="gates_bwd")
    dproj = jnp.concatenate([dq_fox.astype(BF16), dk_fox, dv_fox, dgq, dgk, dgv, dgr], axis=1)
    dw_main, = _mm(dproj, h1, mode="tn", out_dtypes=[BF16], name="dw_in_main")
    dw_small, = _mm(dps, h1, mode="tn", out_dtypes=[BF16], name="dw_in_small")
    rs_in = w_in_t.shape[0] // 4
    dw_in = _merge_dw_in(dw_main, dw_small).reshape(4, rs_in, d)
    sum_in = _chip_sum(dw_in, idx, "in")
    dh1_small, = _mm(dps, w_small, mode="nn", out_dtypes=[F32], name="d_h1_small")
    dh1, parts_in = _mm(dproj, w_main, mode="nn", out_dtypes=[F32], extras=[dh1_small],
                        epi=lambda acc, e: (acc + e,), name="d_h1", side=_shard_side(sum_in))
    grad_x, dshift_m, da1 = _pre_bwd(dh1, x, dx1, a1, name="pre_mix_bwd")
    bufs = [_final_sum(sm, pt, idx, shp, name=f"grad_final_sum_{tag}")
            for tag, sm, pt, shp in [("in", sum_in, parts_in, (rs_in, d)), ("out", sum_out, parts_out, (d // 4, d)),
                                     ("mlp_in", sum_mlp_in, parts_mlp_in, (d, D_FF // 4)),
                                     ("mlp_out", sum_mlp_out, parts_mlp_out, (D_FF // 4, d))]]

    dmod = jnp.concatenate([dshift_m, da1 * g_pre_mix, dgate_m, dshift_f, da2 * g_pre_mlp, dgate_f], axis=1)
    small = dict(
        dmod=dmod, g_pre_mix=da1 * (1.0 + scale_m), g_post_mix=dg_post_mix, g_pre_mlp=da2 * (1.0 + scale_f),
        g_post_mlp=dg_post_mlp, b_fgate=dbf[:, :FOX_HEADS], w_gla_a2=dw2p[FOX_HEADS:FOX_HEADS + GLA_RANK],
        b_gla_a2=db2, g_fox_out=dg_fox, g_gla_out=dg_gla)
    return loss_part, grad_x, bufs, small


def _pack(arrays):
    flat = jnp.concatenate([a.reshape(-1).astype(F32) for a in arrays])
    n = flat.shape[0]
    rows = -(-n // 128)
    rows = -(-rows // 8) * 8
    return jnp.pad(flat, (0, rows * 128 - n)).reshape(rows, 128)


def _unpack(buf, shapes):
    flat = buf.reshape(-1)
    out, off = [], 0
    for shp in shapes:
        n = 1
        for q in shp:
            n *= q
        out.append(flat[off:off + n].reshape(shp))
        off += n
    return out


SMALL_GRAD_ORDER = ["dmod", "g_pre_mix", "g_post_mix", "g_pre_mlp", "g_post_mlp", "b_fgate", "w_gla_a2", "b_gla_a2",
                    "g_fox_out", "g_gla_out"]


def kernel(x, c, w_ada, b_ada, g_pre_mix, g_post_mix, w_in, b_fgate, w_gla_a2, b_gla_a2, g_fox_out, g_gla_out, w_out, g_pre_mlp, g_post_mlp, w_mlp_in, w_mlp_out, loss_target, m_w_ada, m_b_ada, m_g_pre_mix, m_g_post_mix, m_w_in, m_b_fgate, m_w_gla_a2, m_b_gla_a2, m_g_fox_out, m_g_gla_out, m_w_out, m_g_pre_mlp, m_g_post_mlp, m_w_mlp_in, m_w_mlp_out, v_w_ada, v_b_ada, v_g_pre_mix, v_g_post_mix, v_w_in, v_b_fgate, v_w_gla_a2, v_b_gla_a2, v_g_fox_out, v_g_gla_out, v_w_out, v_g_pre_mlp, v_g_post_mlp, v_w_mlp_in, v_w_mlp_out):
    ix, iy, ic = lax.axis_index("x"), lax.axis_index("y"), lax.axis_index("c")
    chip = 2 * ix + iy
    dev = 4 * ix + 2 * iy + ic
    d = D_MODEL

    c_act = _silu_rows(c, name="silu_c")
    pack1 = _pack([c_act, w_gla_a2[0], g_gla_out[0]])
    rows1 = pack1.shape[0]
    got1 = _gather8(pack1, name="gather_small_fwd").reshape(8, rows1, 128)
    per_dev = [_unpack(got1[q], [(d,), (GLA_RANK, GLA_KW // 4), (GLA_HEADS, GLA_DV // 4)]) for q in range(8)]
    c_all = jnp.stack([p[0] for p in per_dev])
    w_gla_a2_full = jnp.concatenate([per_dev[2 * j][1] for j in range(4)], axis=1)
    g_gla_full = jnp.concatenate([per_dev[2 * j][2] for j in range(4)], axis=1)
    cols = w_ada.shape[2]
    b_ada_shard = lax.dynamic_slice_in_dim(b_ada, chip * cols, cols, axis=1)
    mod_sh = _mod_shard(c_all, w_ada[0], b_ada_shard, name="ada_mod")
    got2 = _gather8(mod_sh, name="gather_mod").reshape(8, 8, cols)
    mod_all = jnp.concatenate([got2[2 * j] for j in range(4)], axis=1)
    mod = lax.dynamic_slice_in_dim(mod_all, dev, 1, axis=0)

    tr_in = lambda a: jnp.transpose(a[0])
    own_bf = [tr_in(w_in).astype(BF16), w_out[0].astype(BF16), w_mlp_in[0].astype(BF16), w_mlp_out[0].astype(BF16)]
    gw_in, = _finish_gather(_gather_weights(own_bf[:1], name="gather_w_in_ici"), own_bf[:1], chip,
                            name="gather_w_in_d2d")
    w_in_t = gw_in.reshape(-1, d)
    idx = jnp.stack([ic, chip]).astype(jnp.int32)

    loss_part, grad_x, bufs, small = _local_step(
        x[0], loss_target[0], mod, g_pre_mix, g_post_mix, g_pre_mlp, g_post_mlp, w_in_t, b_fgate,
        w_gla_a2_full, b_gla_a2, g_fox_out[0], g_gla_full, own_bf[1], own_bf[2], own_bf[3], chip, idx)
    loss = lax.psum(loss_part[0, 0], ("x", "y", "c"))

    g_big = _half_exchange(bufs, name="grad_half_exchange")
    big_w = [(tr_in(w_in), tr_in(m_w_in), tr_in(v_w_in)), (w_out[0], m_w_out[0], v_w_out[0]),
             (w_mlp_in[0], m_w_mlp_in[0], v_w_mlp_in[0]), (w_mlp_out[0], m_w_mlp_out[0], v_w_mlp_out[0])]
    big_res = []
    for q, (g, (w, m, v)) in enumerate(zip(g_big, big_w)):
        res4 = (g,) + tuple(_adam(g, w, m, v, name=f"adam_big_{q}"))
        big_res.append(tuple((jnp.transpose(a) if q == 0 else a)[None] for a in res4))

    pack2 = _pack([small[k] for k in SMALL_GRAD_ORDER])
    rows2 = pack2.shape[0]
    got3 = _gather8(pack2, name="gather_small_grads").reshape(8, rows2, 128)
    dmod_all = got3[:, :6 * d // 128, :].reshape(8, 6 * d)
    sums = _stack_sum(got3, name="small_grad_sum")
    shapes = [(1, 6 * d), (1, d), (1, d), (1, d), (1, d), (1, FOX_HEADS), (1, GLA_RANK, GLA_KW), (1, GLA_KW),
              (1, FOX_HEADS, FOX_HD), (1, GLA_HEADS, GLA_DV)]
    sg = dict(zip(["b_ada"] + SMALL_GRAD_ORDER[1:], _unpack(sums, shapes)))
    sg["w_gla_a2"] = lax.dynamic_slice_in_dim(sg["w_gla_a2"], chip * (GLA_KW // 4), GLA_KW // 4, axis=2)
    sg["g_gla_out"] = lax.dynamic_slice_in_dim(sg["g_gla_out"], chip * (GLA_DV // 4), GLA_DV // 4, axis=2)
    small_names = ["b_ada", "g_pre_mix", "g_post_mix", "b_fgate", "w_gla_a2", "b_gla_a2", "g_fox_out", "g_gla_out",
                   "g_pre_mlp", "g_post_mlp"]
    small_w = dict(b_ada=(b_ada, m_b_ada, v_b_ada), g_pre_mix=(g_pre_mix, m_g_pre_mix, v_g_pre_mix),
                   g_post_mix=(g_post_mix, m_g_post_mix, v_g_post_mix), b_fgate=(b_fgate, m_b_fgate, v_b_fgate),
                   w_gla_a2=(w_gla_a2, m_w_gla_a2, v_w_gla_a2), b_gla_a2=(b_gla_a2, m_b_gla_a2, v_b_gla_a2),
                   g_fox_out=(g_fox_out, m_g_fox_out, v_g_fox_out), g_gla_out=(g_gla_out, m_g_gla_out, v_g_gla_out),
                   g_pre_mlp=(g_pre_mlp, m_g_pre_mlp, v_g_pre_mlp), g_post_mlp=(g_post_mlp, m_g_post_mlp, v_g_post_mlp))
    sshapes = [small_w[k][0].shape for k in small_names]
    pg = _pack([sg[k] for k in small_names])
    pw, pm, pv = [_pack([small_w[k][q] for k in small_names]) for q in range(3)]
    pd, pmn, pvn = _adam(pg, pw, pm, pv, name="adam_small")
    s_delta = dict(zip(small_names, _unpack(pd, sshapes)))
    s_m = dict(zip(small_names, _unpack(pmn, sshapes)))
    s_v = dict(zip(small_names, _unpack(pvn, sshapes)))

    dmod_cols = lax.dynamic_slice_in_dim(dmod_all, chip * cols, cols, axis=1)
    g_ada, d_ada, m_ada, v_ada = _ada_grad_adam(c_all.T, dmod_cols, w_ada[0], m_w_ada[0], v_w_ada[0], name="ada_grad_adam")

    order = ["w_ada", "b_ada", "g_pre_mix", "g_post_mix", "w_in", "b_fgate", "w_gla_a2", "b_gla_a2", "g_fox_out",
             "g_gla_out", "w_out", "g_pre_mlp", "g_post_mlp", "w_mlp_in", "w_mlp_out"]
    res = {"w_ada": (g_ada[None], d_ada[None], m_ada[None], v_ada[None]),
           "w_in": big_res[0], "w_out": big_res[1], "w_mlp_in": big_res[2], "w_mlp_out": big_res[3]}
    for k in small_names:
        res[k] = (sg[k], s_delta[k], s_m[k], s_v[k])
    return (loss, grad_x[None], *[res[k][0] for k in order], *[res[k][1] for k in order],
            *[res[k][2] for k in order], *[res[k][3] for k in order])
```

```python
import functools

import jax
import jax.numpy as jnp
from jax import lax
from jax.experimental import pallas as pl
from jax.experimental.pallas import tpu as pltpu

F32 = jnp.float32
BF16 = jnp.bfloat16
MESH = pl.DeviceIdType.MESH
HIGHEST = lax.Precision.HIGHEST

D_MODEL = 2048
FOX_HEADS = 8
FOX_HD = 128
FOX_W = FOX_HEADS * FOX_HD
GLA_HEADS = 4
GLA_DK = 128
GLA_DV = 256
GLA_KW = GLA_HEADS * GLA_DK
GLA_W = GLA_HEADS * GLA_DV
GLA_RANK = 16
GLA_TEMP = 16.0
CHUNK = 64
D_FF = 4 * D_MODEL
EPS = 1e-6
MAIN_W = 3 * FOX_W + 2 * GLA_KW + 2 * GLA_W
SMALL_W = 128
NEG = -1e30

ADAM_LR = 0.001
ADAM_B1 = 0.9
ADAM_B2 = 0.999
ADAM_EPS = 1e-08
ADAM_WD = 0.01
ADAM_STEP = 10

VMEM_LIMIT = 52 * 1024 * 1024
ROW_TILE = 256
FOX_TQ = 512
FOX_TK = 512
GLA_ROWS = 512
GATE_TS = 512
MM_T = 1024
MM_TK = 2048
MM_TM = 2048


def _cp(*sem):
    return pltpu.CompilerParams(dimension_semantics=sem, vmem_limit_bytes=VMEM_LIMIT)


def _dot_nn(a, b, precision=None):
    return jnp.dot(a, b, preferred_element_type=F32, precision=precision)


def _dot_nt(a, b, precision=None):
    return lax.dot_general(a, b, (((1,), (1,)), ((), ())), preferred_element_type=F32, precision=precision)


def _dot_tn(a, b, precision=None):
    return lax.dot_general(a, b, (((0,), (0,)), ((), ())), preferred_element_type=F32, precision=precision)


def _sigmoid(x):
    return 1.0 / (1.0 + jnp.exp(-x))


def _log_sigmoid(x):
    return jnp.minimum(x, 0.0) - jnp.log(1.0 + jnp.exp(-jnp.abs(x)))


class _Side:
    def __init__(self, inputs, out_shapes, plan_fn, n_copies, aliases=None):
        self.inputs, self.out_shapes, self.plan_fn, self.n_copies = list(inputs), list(out_shapes), plan_fn, n_copies
        self.aliases = dict(aliases or {})

    def scratch(self):
        return [pltpu.SemaphoreType.DMA((self.n_copies,)), pltpu.SemaphoreType.DMA((self.n_copies,))]

    def run(self, in_refs, out_refs, sems, first, last):
        @pl.when(first)
        def _():
            _plan_start(self.plan_fn(in_refs, out_refs), *sems)

        @pl.when(last)
        def _():
            _plan_wait(self.plan_fn(in_refs, out_refs), *sems)


def _mm(a, b, *, mode, out_dtypes, name, tm=None, tn=None, tk=None, extras=(), epi=None,
        out_shapes=None, out_specs=None, side=None, b_slots=0):
    tm, tn, tk = tm or MM_T, tn or MM_T, tk or MM_TK
    b2 = (b.shape[1], b_slots * b.shape[2]) if b_slots else b.shape
    if mode == "nn":
        (m, k), n = a.shape, b2[1]
    elif mode == "nt":
        (m, k), n = a.shape, b2[0]
    else:
        (k, m), n = a.shape, b2[1]
    tm, tn, tk = min(tm, m), min(tn, n), min(tk, k)
    if b_slots:
        tn = min(tn, b.shape[2]) if mode == "nn" else tn
        tk = min(tk, b.shape[2]) if mode == "nt" else tk
    assert m % tm == 0 and n % tn == 0 and k % tk == 0, (name, m, n, k)
    nk = k // tk
    n_out, n_ex = len(out_dtypes), len(extras)
    if epi is None:
        epi = lambda acc: tuple(acc for _ in range(n_out))
    dot = {"nn": _dot_nn, "nt": _dot_nt, "tn": _dot_tn}[mode]

    n_si = len(side.inputs) if side else 0
    n_so = len(side.out_shapes) if side else 0
    grid = (m // tm, n // tn, nk)

    def body(*refs):
        a_ref, b_ref = refs[0], refs[1]
        ex_refs = refs[2:2 + n_ex]
        base = 2 + n_ex + n_si
        o_refs = refs[base:base + n_out]
        scratch = refs[base + n_out + n_so:]
        if side:
            pos = [pl.program_id(q) for q in range(3)]
            first = (pos[0] == 0) & (pos[1] == 0) & (pos[2] == 0)
            last = (pos[0] == grid[0] - 1) & (pos[1] == grid[1] - 1) & (pos[2] == grid[2] - 1)
            side.run(refs[2 + n_ex:base], refs[base + n_out:base + n_out + n_so], scratch[-2:], first, last)
        part = dot(a_ref[...], b_ref[...])

        def finish(acc):
            outs = epi(acc, *[e[...] for e in ex_refs])
            for o_ref, val in zip(o_refs, outs):
                o_ref[...] = val.reshape(o_ref.shape).astype(o_ref.dtype)

        if nk == 1:
            finish(part)
        else:
            acc_ref = scratch[0]
            kk = pl.program_id(2)

            @pl.when(kk == 0)
            def _():
                acc_ref[...] = part

            @pl.when(kk > 0)
            def _():
                acc_ref[...] += part

            @pl.when(kk == nk - 1)
            def _():
                finish(acc_ref[...])

    if mode == "nn":
        a_spec = pl.BlockSpec((tm, tk), lambda i, j, kk: (i, kk))
        b_spec = pl.BlockSpec((tk, tn), lambda i, j, kk: (kk, j))
        if b_slots:
            per = b.shape[2] // tn
            b_spec = pl.BlockSpec((None, tk, tn), lambda i, j, kk: (j // per, kk, j % per))
    elif mode == "nt":
        a_spec = pl.BlockSpec((tm, tk), lambda i, j, kk: (i, kk))
        b_spec = pl.BlockSpec((tn, tk), lambda i, j, kk: (j, kk))
        if b_slots:
            per = b.shape[2] // tk
            b_spec = pl.BlockSpec((None, tn, tk), lambda i, j, kk: (kk // per, j, kk % per))
    else:
        assert not b_slots
        a_spec = pl.BlockSpec((tk, tm), lambda i, j, kk: (kk, i))
        b_spec = pl.BlockSpec((tk, tn), lambda i, j, kk: (kk, j))
    tile_spec = pl.BlockSpec((tm, tn), lambda i, j, kk: (i, j))
    if out_shapes is None:
        out_shapes = [jax.ShapeDtypeStruct((m, n), dt) for dt in out_dtypes]
    if out_specs is None:
        out_specs = [tile_spec for _ in out_dtypes]
    any_spec = pl.BlockSpec(memory_space=pl.ANY)
    res = pl.pallas_call(
        body,
        grid=grid,
        in_specs=[a_spec, b_spec] + [tile_spec for _ in extras] + [any_spec] * n_si,
        out_specs=list(out_specs) + [any_spec] * n_so,
        out_shape=list(out_shapes) + (side.out_shapes if side else []),
        scratch_shapes=([pltpu.VMEM((tm, tn), F32)] if nk > 1 else []) + (side.scratch() if side else []),
        compiler_params=_cp("arbitrary", "arbitrary", "arbitrary") if side else _cp("parallel", "parallel", "arbitrary"),
        input_output_aliases={2 + n_ex + si: n_out + so for si, so in side.aliases.items()} if side else {},
        name=name,
    )(a, b, *extras, *(side.inputs if side else []))
    return res


def _row_spec(ts, d):
    return pl.BlockSpec((ts, d), lambda i: (i, 0))


def _vec_spec(d):
    return pl.BlockSpec((1, d), lambda i: (0, 0))


def _pre_fwd(x, avec, shift, *, name):
    s, d = x.shape
    ts = min(ROW_TILE, s)

    def body(x_ref, a_ref, s_ref, h_ref):
        xv = x_ref[...]
        r = lax.rsqrt(jnp.mean(xv * xv, axis=-1, keepdims=True) + EPS)
        h_ref[...] = (xv * r * a_ref[...] + s_ref[...]).astype(BF16)

    return pl.pallas_call(
        body, grid=(s // ts,),
        in_specs=[_row_spec(ts, d), _vec_spec(d), _vec_spec(d)],
        out_specs=_row_spec(ts, d),
        out_shape=jax.ShapeDtypeStruct((s, d), BF16),
        compiler_params=_cp("parallel"), name=name,
    )(x, avec, shift)


def _post_pre_fwd(x, y, gate, g, avec, shift, *, name):
    s, d = x.shape
    ts = min(ROW_TILE, s)

    def body(x_ref, y_ref, gate_ref, g_ref, a_ref, s_ref, o_ref, h_ref):
        yv = y_ref[...]
        r = lax.rsqrt(jnp.mean(yv * yv, axis=-1, keepdims=True) + EPS)
        x1 = x_ref[...] + gate_ref[...] * (yv * r * g_ref[...])
        o_ref[...] = x1
        r1 = lax.rsqrt(jnp.mean(x1 * x1, axis=-1, keepdims=True) + EPS)
        h_ref[...] = (x1 * r1 * a_ref[...] + s_ref[...]).astype(BF16)

    return pl.pallas_call(
        body, grid=(s // ts,),
        in_specs=[_row_spec(ts, d), _row_spec(ts, d)] + [_vec_spec(d)] * 4,
        out_specs=[_row_spec(ts, d), _row_spec(ts, d)],
        out_shape=[jax.ShapeDtypeStruct((s, d), F32), jax.ShapeDtypeStruct((s, d), BF16)],
        compiler_params=_cp("parallel"), name=name,
    )(x, y, gate, g, avec, shift)


def _post_bwd_math(dxv, yv, gatev, gv):
    r = lax.rsqrt(jnp.mean(yv * yv, axis=-1, keepdims=True) + EPS)
    yhat = yv * r
    dn = dxv * gatev
    dyhat = dn * gv
    dy = r * (dyhat - yhat * jnp.mean(dyhat * yhat, axis=-1, keepdims=True))
    return dy, dxv * (yhat * gv), dn * yhat


def _accumulate(first, pairs):
    @pl.when(first)
    def _():
        for ref, _ in pairs:
            ref[...] = jnp.zeros_like(ref)

    for ref, val in pairs:
        ref[...] += jnp.sum(val, axis=0, keepdims=True)


def _post_loss_bwd(x, y, gate, g, target, *, name):
    s, d = x.shape
    ts = min(ROW_TILE, s)

    def body(x_ref, y_ref, gate_ref, g_ref, t_ref, dx_ref, dy_ref, loss_ref, dgate_ref, dg_ref):
        yv, gatev, gv = y_ref[...], gate_ref[...], g_ref[...]
        r = lax.rsqrt(jnp.mean(yv * yv, axis=-1, keepdims=True) + EPS)
        diff = x_ref[...] + gatev * (yv * r * gv) - t_ref[...]
        dxv = diff * (1.0 / d)
        dx_ref[...] = dxv
        dy, dgate_rows, dg_rows = _post_bwd_math(dxv, yv, gatev, gv)
        dy_ref[...] = dy.astype(BF16)
        first = pl.program_id(0) == 0
        _accumulate(first, [(dgate_ref, dgate_rows), (dg_ref, dg_rows)])

        @pl.when(first)
        def _():
            loss_ref[...] = jnp.zeros_like(loss_ref)

        loss_ref[...] += jnp.sum(jnp.mean(diff * diff, axis=-1, keepdims=True)) * 0.5

    return pl.pallas_call(
        body, grid=(s // ts,),
        in_specs=[_row_spec(ts, d), _row_spec(ts, d), _vec_spec(d), _vec_spec(d), _row_spec(ts, d)],
        out_specs=[_row_spec(ts, d), _row_spec(ts, d), pl.BlockSpec((1, 128), lambda i: (0, 0)), _vec_spec(d),
                   _vec_spec(d)],
        out_shape=[jax.ShapeDtypeStruct((s, d), F32), jax.ShapeDtypeStruct((s, d), BF16),
                   jax.ShapeDtypeStruct((1, 128), F32), jax.ShapeDtypeStruct((1, d), F32),
                   jax.ShapeDtypeStruct((1, d), F32)],
        compiler_params=_cp("arbitrary"), name=name,
    )(x, y, gate, g, target)


def _pre_post_bwd(dh, xin, dres, avec, y, gate, g, *, name):
    s, d = xin.shape
    ts = min(ROW_TILE, s)

    def body(dh_ref, x_ref, dres_ref, a_ref, y_ref, gate_ref, g_ref, dx_ref, dshift_ref, da_ref, dy_ref,
             dgate_ref, dg_ref):
        xv, dhv = x_ref[...], dh_ref[...]
        r = lax.rsqrt(jnp.mean(xv * xv, axis=-1, keepdims=True) + EPS)
        xhat = xv * r
        dxhat = dhv * a_ref[...]
        dxv = dres_ref[...] + r * (dxhat - xhat * jnp.mean(dxhat * xhat, axis=-1, keepdims=True))
        dx_ref[...] = dxv
        dy, dgate_rows, dg_rows = _post_bwd_math(dxv, y_ref[...], gate_ref[...], g_ref[...])
        dy_ref[...] = dy.astype(BF16)
        _accumulate(pl.program_id(0) == 0, [(dshift_ref, dhv), (da_ref, dhv * xhat), (dgate_ref, dgate_rows),
                                            (dg_ref, dg_rows)])

    return pl.pallas_call(
        body, grid=(s // ts,),
        in_specs=[_row_spec(ts, d), _row_spec(ts, d), _row_spec(ts, d), _vec_spec(d), _row_spec(ts, d),
                  _vec_spec(d), _vec_spec(d)],
        out_specs=[_row_spec(ts, d), _vec_spec(d), _vec_spec(d), _row_spec(ts, d), _vec_spec(d), _vec_spec(d)],
        out_shape=[jax.ShapeDtypeStruct((s, d), F32), jax.ShapeDtypeStruct((1, d), F32),
                   jax.ShapeDtypeStruct((1, d), F32), jax.ShapeDtypeStruct((s, d), BF16),
                   jax.ShapeDtypeStruct((1, d), F32), jax.ShapeDtypeStruct((1, d), F32)],
        compiler_params=_cp("arbitrary"), name=name,
    )(dh, xin, dres, avec, y, gate, g)


def _pre_bwd(dh, xin, dres, avec, *, name):
    s, d = xin.shape
    ts = min(ROW_TILE, s)

    def body(dh_ref, x_ref, dres_ref, a_ref, dx_ref, dshift_ref, da_ref):
        xv, dhv = x_ref[...], dh_ref[...]
        r = lax.rsqrt(jnp.mean(xv * xv, axis=-1, keepdims=True) + EPS)
        xhat = xv * r
        dxhat = dhv * a_ref[...]
        dx_ref[...] = dres_ref[...] + r * (dxhat - xhat * jnp.mean(dxhat * xhat, axis=-1, keepdims=True))

        @pl.when(pl.program_id(0) == 0)
        def _():
            dshift_ref[...] = jnp.zeros_like(dshift_ref)
            da_ref[...] = jnp.zeros_like(da_ref)

        dshift_ref[...] += jnp.sum(dhv, axis=0, keepdims=True)
        da_ref[...] += jnp.sum(dhv * xhat, axis=0, keepdims=True)

    return pl.pallas_call(
        body, grid=(s // ts,),
        in_specs=[_row_spec(ts, d), _row_spec(ts, d), _row_spec(ts, d), _vec_spec(d)],
        out_specs=[_row_spec(ts, d), _vec_spec(d), _vec_spec(d)],
        out_shape=[jax.ShapeDtypeStruct((s, d), F32), jax.ShapeDtypeStruct((1, d), F32),
                   jax.ShapeDtypeStruct((1, d), F32)],
        compiler_params=_cp("arbitrary"), name=name,
    )(dh, xin, dres, avec)


def _tri(n, strict=False, upper=False):
    r = lax.broadcasted_iota(jnp.int32, (n, n), 0)
    c = lax.broadcasted_iota(jnp.int32, (n, n), 1)
    if upper:
        r, c = c, r
    return ((r > c) if strict else (r >= c)).astype(F32)


def _gates_fwd(ps, bf, w2p, b2, *, name):
    s = ps.shape[0]
    ts = min(GATE_TS, s)

    def body(ps_ref, bf_ref, w_ref, b2_ref, cum_ref, la_ref, carry_ref):
        @pl.when(pl.program_id(0) == 0)
        def _():
            carry_ref[...] = jnp.zeros_like(carry_ref)

        psv = ps_ref[...]
        lf = _log_sigmoid(psv + bf_ref[...])
        cum = _dot_nn(_tri(ts), lf, HIGHEST) + carry_ref[...]
        cum_ref[...] = cum
        carry_ref[...] = cum[ts - 1:ts, :]
        z = _dot_nn(psv, w_ref[...], HIGHEST) + b2_ref[...]
        la_ref[...] = _log_sigmoid(z) * (1.0 / GLA_TEMP)

    return pl.pallas_call(
        body, grid=(s // ts,),
        in_specs=[_row_spec(ts, SMALL_W), _vec_spec(SMALL_W),
                  pl.BlockSpec((SMALL_W, GLA_KW), lambda i: (0, 0)), _vec_spec(GLA_KW)],
        out_specs=[_row_spec(ts, SMALL_W), _row_spec(ts, GLA_KW)],
        out_shape=[jax.ShapeDtypeStruct((s, SMALL_W), F32), jax.ShapeDtypeStruct((s, GLA_KW), F32)],
        scratch_shapes=[pltpu.VMEM((1, SMALL_W), F32)],
        compiler_params=_cp("arbitrary"), name=name,
    )(ps, bf, w2p, b2)


def _gates_bwd(dck, ps, bf, w2p, b2, dla, *, name):
    s = ps.shape[0]
    ts = min(GATE_TS, s)
    nb = s // ts
    rev = lambda i: (nb - 1 - i, 0)

    def body(dck_ref, ps_ref, bf_ref, w_ref, b2_ref, dla_ref, dps_ref, dbf_ref, dw_ref, db2_ref, carry_ref):
        @pl.when(pl.program_id(0) == 0)
        def _():
            carry_ref[...] = jnp.zeros_like(carry_ref)
            dbf_ref[...] = jnp.zeros_like(dbf_ref)
            dw_ref[...] = jnp.zeros_like(dw_ref)
            db2_ref[...] = jnp.zeros_like(db2_ref)

        psv, dckv = ps_ref[...], dck_ref[...]
        dlf = _dot_nn(_tri(ts, upper=True), dckv, HIGHEST) + carry_ref[...]
        carry_ref[...] += jnp.sum(dckv, axis=0, keepdims=True)
        lane = lax.broadcasted_iota(jnp.int32, (ts, SMALL_W), 1)
        dff = jnp.where(lane < FOX_HEADS, dlf * _sigmoid(-(psv + bf_ref[...])), 0.0)
        z = _dot_nn(psv, w_ref[...], HIGHEST) + b2_ref[...]
        dz = dla_ref[...] * _sigmoid(-z) * (1.0 / GLA_TEMP)
        dps_ref[...] = (_dot_nt(dz, w_ref[...], HIGHEST) + dff).astype(BF16)
        dbf_ref[...] += jnp.sum(dff, axis=0, keepdims=True)
        dw_ref[...] += _dot_tn(psv, dz, HIGHEST)
        db2_ref[...] += jnp.sum(dz, axis=0, keepdims=True)

    return pl.pallas_call(
        body, grid=(nb,),
        in_specs=[pl.BlockSpec((ts, SMALL_W), rev), pl.BlockSpec((ts, SMALL_W), rev), _vec_spec(SMALL_W),
                  pl.BlockSpec((SMALL_W, GLA_KW), lambda i: (0, 0)), _vec_spec(GLA_KW),
                  pl.BlockSpec((ts, GLA_KW), rev)],
        out_specs=[pl.BlockSpec((ts, SMALL_W), rev), _vec_spec(SMALL_W),
                   pl.BlockSpec((SMALL_W, GLA_KW), lambda i: (0, 0)), _vec_spec(GLA_KW)],
        out_shape=[jax.ShapeDtypeStruct((s, SMALL_W), BF16), jax.ShapeDtypeStruct((1, SMALL_W), F32),
                   jax.ShapeDtypeStruct((SMALL_W, GLA_KW), F32), jax.ShapeDtypeStruct((1, GLA_KW), F32)],
        scratch_shapes=[pltpu.VMEM((1, SMALL_W), F32)],
        compiler_params=_cp("arbitrary"), name=name,
    )(dck, ps, bf, w2p, b2, dla)


def _hs(h, hd=FOX_HD):
    return slice(h * hd, (h + 1) * hd)


def _fox_fwd(proj, cum_t, g_fox, *, name, side=None):
    s = proj.shape[0]
    tq, tk = min(FOX_TQ, s), min(FOX_TK, s)
    scale = FOX_HD ** -0.5
    n_si = len(side.inputs) if side else 0
    n_so = len(side.out_shapes) if side else 0
    grid = (s // tq, s // tk)

    def body(*refs):
        q_ref, k_ref, v_ref, ck_ref, g_ref = refs[:5]
        o_ref, n_ref, lse_ref = refs[5 + n_si:8 + n_si]
        m_sc, acc_sc = refs[8 + n_si + n_so:10 + n_si + n_so]
        i, j = pl.program_id(0), pl.program_id(1)
        if side:
            side.run(refs[5:5 + n_si], refs[8 + n_si:8 + n_si + n_so], refs[10 + n_si + n_so:],
                     (i == 0) & (j == 0), (i == grid[0] - 1) & (j == grid[1] - 1))

        @pl.when(j == 0)
        def _():
            m_sc[...] = jnp.full_like(m_sc, NEG)
            acc_sc[...] = jnp.zeros_like(acc_sc)

        def block(masked):
            mask = _causal_mask(i, j, tq, tk) if masked else None
            ones = jnp.ones((tk, FOX_HD), BF16)
            for h in range(FOX_HEADS):
                sc = _fox_logits(_dot_nt(q_ref[:, _hs(h)], k_ref[:, _hs(h)]), ck_ref[h:h + 1, :], mask, scale)
                m_prev = m_sc[h]
                m_new = jnp.maximum(m_prev, jnp.max(sc, axis=-1, keepdims=True))
                alpha = jnp.exp(m_prev - m_new)
                p = jnp.exp(sc - m_new).astype(BF16)
                v_one = jnp.concatenate([v_ref[:, _hs(h)], ones], axis=1)
                acc_sc[:, _hs(h, 2 * FOX_HD)] = alpha * acc_sc[:, _hs(h, 2 * FOX_HD)] + _dot_nn(p, v_one)
                m_sc[h] = m_new

        pl.when(j < i)(functools.partial(block, False))

        @pl.when(j == i)
        def _():
            block(True)
            lane = lax.broadcasted_iota(jnp.int32, (tq, 128), 1)
            lse = jnp.zeros((tq, 128), F32)
            for h in range(FOX_HEADS):
                l_rep = acc_sc[:, 2 * h * FOX_HD + FOX_HD:2 * (h + 1) * FOX_HD]
                o = acc_sc[:, 2 * h * FOX_HD:2 * h * FOX_HD + FOX_HD] / l_rep
                o_ref[:, _hs(h)] = o
                r = lax.rsqrt(jnp.mean(o * o, axis=-1, keepdims=True) + EPS)
                n_ref[:, _hs(h)] = (o * r * g_ref[h:h + 1, :]).astype(BF16)
                lse = jnp.where(lane == h, m_sc[h] + jnp.log(l_rep), lse)
            lse_ref[...] = lse

    kv = lambda col: (lambda i, j: (jnp.minimum(j, i), col))
    any_spec = pl.BlockSpec(memory_space=pl.ANY)
    return pl.pallas_call(
        body, grid=grid,
        in_specs=[pl.BlockSpec((tq, FOX_W), lambda i, j: (i, 0)),
                  pl.BlockSpec((tk, FOX_W), kv(1)),
                  pl.BlockSpec((tk, FOX_W), kv(2)),
                  pl.BlockSpec((FOX_HEADS, tk), lambda i, j: (0, jnp.minimum(j, i))),
                  pl.BlockSpec((FOX_HEADS, FOX_HD), lambda i, j: (0, 0))] + [any_spec] * n_si,
        out_specs=[pl.BlockSpec((tq, FOX_W), lambda i, j: (i, 0)),
                   pl.BlockSpec((tq, FOX_W), lambda i, j: (i, 0)),
                   pl.BlockSpec((tq, 128), lambda i, j: (i, 0))] + [any_spec] * n_so,
        out_shape=[jax.ShapeDtypeStruct((s, FOX_W), F32), jax.ShapeDtypeStruct((s, FOX_W), BF16),
                   jax.ShapeDtypeStruct((s, 128), F32)] + (side.out_shapes if side else []),
        scratch_shapes=[pltpu.VMEM((FOX_HEADS, tq, 1), F32), pltpu.VMEM((tq, 2 * FOX_W), F32)]
        + (side.scratch() if side else []),
        compiler_params=_cp("arbitrary", "arbitrary"), name=name,
    )(proj, proj, proj, cum_t, g_fox, *(side.inputs if side else []))


def _causal_mask(i, j, tq, tk):
    rows = i * tq + lax.broadcasted_iota(jnp.int32, (tq, tk), 0)
    cols = j * tk + lax.broadcasted_iota(jnp.int32, (tq, tk), 1)
    return rows >= cols


def _fox_logits(qk, ck, mask, scale):
    sc = qk * scale - ck
    return sc if mask is None else jnp.where(mask, sc, NEG)


def _fox_bwd(proj, do, cum_t, lse, delta, *, name, side=None):
    s = proj.shape[0]
    tq, tk = min(FOX_TQ, s), min(FOX_TK, s)
    nk, nq = s // tk, s // tq
    scale = FOX_HD ** -0.5
    n_si = len(side.inputs) if side else 0
    n_so = len(side.out_shapes) if side else 0

    def body(*refs):
        q_ref, k_ref, v_ref, do_ref, ck_ref, lse_ref, dl_ref = refs[:7]
        dq_hbm, dk_ref, dv_ref, dcq_hbm, dck_ref = refs[7 + n_si:12 + n_si]
        dq_sc, dcq_sc, dk_sc, dv_sc, dck_sc, out_sems = refs[12 + n_si + n_so:18 + n_si + n_so]
        j, i = pl.program_id(0), pl.program_id(1)
        if side:
            side.run(refs[7:7 + n_si], refs[12 + n_si:12 + n_si + n_so], refs[18 + n_si + n_so:],
                     (j == 0) & (i == 0), (j == nk - 1) & (i == nq - 1))

        @pl.when((j == 0) & (i == 0))
        def _():
            dq_sc[...] = jnp.zeros_like(dq_sc)
            dcq_sc[...] = jnp.zeros_like(dcq_sc)

        @pl.when(i == 0)
        def _():
            dk_sc[...] = jnp.zeros_like(dk_sc)
            dv_sc[...] = jnp.zeros_like(dv_sc)
            dck_sc[...] = jnp.zeros_like(dck_sc)

        def block(masked):
            mask = _causal_mask(i, j, tq, tk) if masked else None
            qrows = pl.ds(pl.multiple_of(i * tq, tq), tq)
            for h in range(FOX_HEADS):
                sc = _fox_logits(_dot_nt(q_ref[:, _hs(h)], k_ref[:, _hs(h)]), ck_ref[h:h + 1, :], mask, scale)
                p = jnp.exp(sc - lse_ref[:, h:h + 1])
                ds = p * (_dot_nt(do_ref[:, _hs(h)], v_ref[:, _hs(h)]) - dl_ref[:, h:h + 1])
                dsb = ds.astype(BF16)
                dv_sc[:, _hs(h)] += _dot_tn(p.astype(BF16), do_ref[:, _hs(h)])
                dk_sc[:, _hs(h)] += _dot_tn(dsb, q_ref[:, _hs(h)])
                dq_sc[qrows, _hs(h)] += _dot_nn(dsb, k_ref[:, _hs(h)]) * scale
                dck_sc[h:h + 1, :] -= jnp.sum(ds, axis=0, keepdims=True)
                dcq_sc[qrows, h:h + 1] += jnp.sum(ds, axis=-1, keepdims=True)

        pl.when(i > j)(functools.partial(block, False))
        pl.when(i == j)(functools.partial(block, True))

        @pl.when(i == nq - 1)
        def _():
            dk_ref[...] = (dk_sc[...] * scale).astype(BF16)
            dv_ref[...] = dv_sc[...].astype(BF16)
            dck_ref[...] = dck_sc[...]

        @pl.when((j == nk - 1) & (i == nq - 1))
        def _():
            out_q = pltpu.make_async_copy(dq_sc, dq_hbm, out_sems.at[0])
            out_c = pltpu.make_async_copy(dcq_sc, dcq_hbm, out_sems.at[1])
            out_q.start()
            out_c.start()
            out_q.wait()
            out_c.wait()

    qrow = lambda j, i: (jnp.maximum(i, j), 0)
    krow = lambda col: (lambda j, i: (j, col))
    any_spec = pl.BlockSpec(memory_space=pl.ANY)
    return pl.pallas_call(
        body, grid=(nk, nq),
        in_specs=[pl.BlockSpec((tq, FOX_W), qrow), pl.BlockSpec((tk, FOX_W), krow(1)),
                  pl.BlockSpec((tk, FOX_W), krow(2)),
                  pl.BlockSpec((tq, FOX_W), qrow),
                  pl.BlockSpec((FOX_HEADS, tk), lambda j, i: (0, j)),
                  pl.BlockSpec((tq, 128), qrow), pl.BlockSpec((tq, 128), qrow)] + [any_spec] * n_si,
        out_specs=[any_spec, pl.BlockSpec((tk, FOX_W), lambda j, i: (j, 0)),
                   pl.BlockSpec((tk, FOX_W), lambda j, i: (j, 0)), any_spec,
                   pl.BlockSpec((FOX_HEADS, tk), lambda j, i: (0, j))] + [any_spec] * n_so,
        out_shape=[jax.ShapeDtypeStruct((s, FOX_W), F32), jax.ShapeDtypeStruct((s, FOX_W), BF16),
                   jax.ShapeDtypeStruct((s, FOX_W), BF16), jax.ShapeDtypeStruct((s, 128), F32),
                   jax.ShapeDtypeStruct((FOX_HEADS, s), F32)] + (side.out_shapes if side else []),
        scratch_shapes=[pltpu.VMEM((s, FOX_W), F32), pltpu.VMEM((s, 128), F32),
                        pltpu.VMEM((tk, FOX_W), F32), pltpu.VMEM((tk, FOX_W), F32), pltpu.VMEM((FOX_HEADS, tk), F32),
                        pltpu.SemaphoreType.DMA((2,))] + (side.scratch() if side else []),
        compiler_params=_cp("arbitrary", "arbitrary"), name=name,
    )(proj, proj, proj, do, cum_t, lse, delta, *(side.inputs if side else []))


def _head_norm_bwd(dn_in, o, g, gr_src, *, nh, hd, dn_col, gr_col, name):
    s, w = o.shape
    ts = min(ROW_TILE, s)
    gated = gr_src is not None

    def body(*refs):
        if gated:
            dn_ref, o_ref, g_ref, gr_ref, do_ref, dgr_ref, dl_ref, dg_ref = refs
        else:
            dn_ref, o_ref, g_ref, do_ref, dl_ref, dg_ref = refs

        @pl.when(pl.program_id(0) == 0)
        def _():
            dg_ref[...] = jnp.zeros_like(dg_ref)

        lane = lax.broadcasted_iota(jnp.int32, (ts, 128), 1)
        delta = jnp.zeros((ts, 128), F32)
        for h in range(nh):
            sl = _hs(h, hd)
            ov = o_ref[:, sl]
            dnv = dn_ref[:, sl].astype(F32)
            gv = g_ref[h:h + 1, :]
            r = lax.rsqrt(jnp.mean(ov * ov, axis=-1, keepdims=True) + EPS)
            ohat = ov * r
            if gated:
                grv = gr_ref[:, sl].astype(F32)
                sig = _sigmoid(grv)
                dgr_ref[:, sl] = (dnv * (ohat * gv) * (sig * (1.0 + grv * (1.0 - sig)))).astype(BF16)
                dnv = dnv * (grv * sig)
            dg_ref[h:h + 1, :] += jnp.sum(dnv * ohat, axis=0, keepdims=True)
            dohat = dnv * gv
            do = r * (dohat - ohat * jnp.mean(dohat * ohat, axis=-1, keepdims=True))
            do_ref[:, sl] = do.astype(BF16)
            delta = jnp.where(lane == h, jnp.sum(do.astype(BF16).astype(F32) * ov, axis=-1, keepdims=True), delta)
        dl_ref[...] = delta

    in_specs = [pl.BlockSpec((ts, w), lambda i: (i, dn_col)), _row_spec(ts, w),
                pl.BlockSpec((nh, hd), lambda i: (0, 0))]
    args = [dn_in, o, g]
    out_specs = [_row_spec(ts, w)]
    out_shape = [jax.ShapeDtypeStruct((s, w), BF16)]
    if gated:
        in_specs.append(pl.BlockSpec((ts, w), lambda i: (i, gr_col)))
        args.append(gr_src)
        out_specs.append(_row_spec(ts, w))
        out_shape.append(jax.ShapeDtypeStruct((s, w), BF16))
    out_specs += [_row_spec(ts, 128), pl.BlockSpec((nh, hd), lambda i: (0, 0))]
    out_shape += [jax.ShapeDtypeStruct((s, 128), F32), jax.ShapeDtypeStruct((nh, hd), F32)]
    return pl.pallas_call(
        body, grid=(s // ts,), in_specs=in_specs, out_specs=out_specs, out_shape=out_shape,
        compiler_params=_cp("arbitrary"), name=name,
    )(*args)


GQ_BLK = 3 * FOX_W // GLA_DK
GK_BLK = GQ_BLK + GLA_HEADS
GV_BLK = (3 * FOX_W + 2 * GLA_KW) // GLA_DV
GR_BLK = GV_BLK + GLA_HEADS


def _gla_chunk_terms(la):
    cum = _dot_nn(_tri(CHUNK), la, HIGHEST)
    total = cum[CHUNK - 1:CHUNK, :]
    return jnp.exp(total - cum), jnp.exp(total)


def _gla_fwd(proj, log_a, g_gla, *, name):
    s = proj.shape[0]
    rows = min(GLA_ROWS, s)
    cb = rows // CHUNK
    nblk = s // rows
    scale = GLA_DK ** -0.5

    def body(q_ref, k_ref, v_ref, gr_ref, la_ref, g_ref, o_ref, n_ref, st_ref, st_sc):
        h = pl.program_id(0)

        @pl.when(pl.program_id(1) == 0)
        def _():
            st_sc[...] = jnp.zeros_like(st_sc)

        gv = g_ref[pl.ds(h, 1), :]
        for ci in range(cb):
            sl = slice(ci * CHUNK, (ci + 1) * CHUNK)
            e, dec = _gla_chunk_terms(la_ref[sl, :])
            k_dec = (k_ref[sl, :].astype(F32) * e).astype(BF16)
            st = st_sc[...] * dec + _dot_tn(v_ref[sl, :], k_dec)
            st_sc[...] = st
            st_ref[0, ci] = st
            qs = (q_ref[sl, :].astype(F32) * scale).astype(BF16)
            o = _dot_nt(qs, st.astype(BF16))
            o_ref[sl, :] = o
            r = lax.rsqrt(jnp.mean(o * o, axis=-1, keepdims=True) + EPS)
            grv = gr_ref[sl, :].astype(F32)
            n_ref[sl, :] = (o * r * gv * (grv * _sigmoid(grv))).astype(BF16)

    return pl.pallas_call(
        body, grid=(GLA_HEADS, nblk),
        in_specs=[pl.BlockSpec((rows, GLA_DK), lambda h, n: (n, GQ_BLK + h)),
                  pl.BlockSpec((rows, GLA_DK), lambda h, n: (n, GK_BLK + h)),
                  pl.BlockSpec((rows, GLA_DV), lambda h, n: (n, GV_BLK + h)),
                  pl.BlockSpec((rows, GLA_DV), lambda h, n: (n, GR_BLK + h)),
                  pl.BlockSpec((rows, GLA_DK), lambda h, n: (n, h)),
                  pl.BlockSpec((GLA_HEADS, GLA_DV), lambda h, n: (0, 0))],
        out_specs=[pl.BlockSpec((rows, GLA_DV), lambda h, n: (n, h)),
                   pl.BlockSpec((rows, GLA_DV), lambda h, n: (n, h)),
                   pl.BlockSpec((1, cb, GLA_DV, GLA_DK), lambda h, n: (h, n, 0, 0))],
        out_shape=[jax.ShapeDtypeStruct((s, GLA_W), F32), jax.ShapeDtypeStruct((s, GLA_W), BF16),
                   jax.ShapeDtypeStruct((GLA_HEADS, s // CHUNK, GLA_DV, GLA_DK), F32)],
        scratch_shapes=[pltpu.VMEM((GLA_DV, GLA_DK), F32)],
        compiler_params=_cp("parallel", "arbitrary"), name=name,
    )(proj, proj, proj, proj, log_a, g_gla)


def _gla_bwd(proj, log_a, do, states, *, name, side=None):
    s = proj.shape[0]
    rows = min(GLA_ROWS, s)
    cb = rows // CHUNK
    nblk = s // rows
    scale = GLA_DK ** -0.5
    n_si = len(side.inputs) if side else 0
    n_so = len(side.out_shapes) if side else 0

    def body(*refs):
        q_ref, k_ref, v_ref, la_ref, do_ref, st_ref, prev_ref = refs[:7]
        dq_ref, dk_ref, dv_ref, dla_ref = refs[7 + n_si:11 + n_si]
        g_sc = refs[11 + n_si + n_so]
        nrev = pl.program_id(1)
        blk = nblk - 1 - nrev
        if side:
            hh = pl.program_id(0)
            side.run(refs[7:7 + n_si], refs[11 + n_si:11 + n_si + n_so], refs[12 + n_si + n_so:],
                     (hh == 0) & (nrev == 0), (hh == GLA_HEADS - 1) & (nrev == nblk - 1))

        @pl.when(nrev == 0)
        def _():
            g_sc[...] = jnp.zeros_like(g_sc)

        for ci in reversed(range(cb)):
            sl = slice(ci * CHUNK, (ci + 1) * CHUNK)
            e, dec = _gla_chunk_terms(la_ref[sl, :])
            kd = k_ref[sl, :].astype(F32) * e
            qs = (q_ref[sl, :].astype(F32) * scale).astype(BF16)
            dov = do_ref[sl, :]
            st = st_ref[0, ci]
            if ci > 0:
                st_prev = st_ref[0, ci - 1]
            else:
                st_prev = prev_ref[0, 0] * (blk > 0).astype(F32)
            dq_ref[sl, :] = (_dot_nn(dov, st.astype(BF16)) * scale).astype(BF16)
            gt = g_sc[...] + _dot_tn(dov, qs)
            gtb = gt.astype(BF16)
            dkd = _dot_nn(v_ref[sl, :], gtb)
            dv_ref[sl, :] = _dot_nt(kd.astype(BF16), gtb).astype(BF16)
            dk_ref[sl, :] = (dkd * e).astype(BF16)
            ddec = jnp.sum(gt * st_prev, axis=0, keepdims=True) * dec
            dla_ref[sl, :] = _dot_nn(_tri(CHUNK, strict=True), dkd * kd, HIGHEST) + ddec
            g_sc[...] = gt * dec

    rev = lambda col0: (lambda h, n: (nblk - 1 - n, col0 + h))
    return pl.pallas_call(
        body, grid=(GLA_HEADS, nblk),
        in_specs=[pl.BlockSpec((rows, GLA_DK), rev(GQ_BLK)),
                  pl.BlockSpec((rows, GLA_DK), rev(GK_BLK)),
                  pl.BlockSpec((rows, GLA_DV), rev(GV_BLK)),
                  pl.BlockSpec((rows, GLA_DK), rev(0)),
                  pl.BlockSpec((rows, GLA_DV), rev(0)),
                  pl.BlockSpec((1, cb, GLA_DV, GLA_DK), lambda h, n: (h, nblk - 1 - n, 0, 0)),
                  pl.BlockSpec((1, 1, GLA_DV, GLA_DK),
                               lambda h, n: (h, jnp.maximum((nblk - 1 - n) * cb - 1, 0), 0, 0))]
        + [pl.BlockSpec(memory_space=pl.ANY)] * n_si,
        out_specs=[pl.BlockSpec((rows, GLA_DK), rev(0)), pl.BlockSpec((rows, GLA_DK), rev(0)),
                   pl.BlockSpec((rows, GLA_DV), rev(0)), pl.BlockSpec((rows, GLA_DK), rev(0))]
        + [pl.BlockSpec(memory_space=pl.ANY)] * n_so,
        out_shape=[jax.ShapeDtypeStruct((s, GLA_KW), BF16), jax.ShapeDtypeStruct((s, GLA_KW), BF16),
                   jax.ShapeDtypeStruct((s, GLA_W), BF16), jax.ShapeDtypeStruct((s, GLA_KW), F32)]
        + (side.out_shapes if side else []),
        scratch_shapes=[pltpu.VMEM((GLA_DV, GLA_DK), F32)] + (side.scratch() if side else []),
        compiler_params=_cp("arbitrary", "arbitrary"), name=name,
    )(proj, proj, proj, log_a, do, states, states, *(side.inputs if side else []))


def _row_tile(r):
    tr = min(ROW_TILE, r)
    while r % tr or tr % 8:
        tr -= 1
    return tr


def _adamw_math(w, g, m, v):
    m = ADAM_B1 * m + (1.0 - ADAM_B1) * g
    v = ADAM_B2 * v + (1.0 - ADAM_B2) * (g * g)
    m_hat = m / (1.0 - ADAM_B1 ** ADAM_STEP)
    v_hat = v / (1.0 - ADAM_B2 ** ADAM_STEP)
    delta = -ADAM_LR * (m_hat / (jnp.sqrt(v_hat) + ADAM_EPS) + ADAM_WD * w)
    return delta, m, v


COL_TILE = 256


def _tile_2d(r, c):
    if r % 8 == 0:
        return _row_tile(r), c
    assert c % COL_TILE == 0, (r, c)
    return r, COL_TILE


def _half_shape(shape):
    r, c = shape[-2:]
    return tuple(shape[:-2]) + ((r // 2, c) if _half_axis(r) == 0 else (r, c // 2))


def _adam(g, w, m, v, *, name):
    r, c = w.shape
    tr, tc = _tile_2d(r, c)

    def body(g_ref, w_ref, m_ref, v_ref, d_ref, mo_ref, vo_ref):
        d, mn, vn = _adamw_math(w_ref[...], g_ref[...], m_ref[...], v_ref[...])
        d_ref[...] = d
        mo_ref[...] = mn
        vo_ref[...] = vn

    spec = pl.BlockSpec((tr, tc), lambda i, j: (i, j))
    return pl.pallas_call(
        body, grid=(r // tr, c // tc), in_specs=[spec] * 4, out_specs=[spec] * 3,
        out_shape=[jax.ShapeDtypeStruct((r, c), F32)] * 3,
        compiler_params=_cp("parallel", "parallel"), name=name,
    )(g, w, m, v)


def _ada_grad_adam(c_all_t, dmod_cols, w, m, v, *, name):
    r, c = w.shape
    tr, tc = min(512, r), min(1024, c)

    def body(ct_ref, dm_ref, w_ref, m_ref, v_ref, g_ref, d_ref, mo_ref, vo_ref):
        g = _dot_nn(ct_ref[...], dm_ref[...], HIGHEST)
        g_ref[...] = g
        d, mn, vn = _adamw_math(w_ref[...], g, m_ref[...], v_ref[...])
        d_ref[...] = d
        mo_ref[...] = mn
        vo_ref[...] = vn

    spec = pl.BlockSpec((tr, tc), lambda i, j: (i, j))
    nb = c_all_t.shape[1]
    return pl.pallas_call(
        body, grid=(r // tr, c // tc),
        in_specs=[pl.BlockSpec((tr, nb), lambda i, j: (i, 0)), pl.BlockSpec((nb, tc), lambda i, j: (0, j)),
                  spec, spec, spec],
        out_specs=[spec] * 4, out_shape=[jax.ShapeDtypeStruct((r, c), F32)] * 4,
        compiler_params=_cp("parallel", "parallel"), name=name,
    )(c_all_t, dmod_cols, w, m, v)


def _mod_shard(c_all, w, b, *, name):
    k, c = w.shape
    tc = min(512, c)
    nb = c_all.shape[0]

    def body(c_ref, w_ref, b_ref, o_ref):
        o_ref[...] = _dot_nn(c_ref[...], w_ref[...], HIGHEST) + b_ref[...]

    return pl.pallas_call(
        body, grid=(c // tc,),
        in_specs=[pl.BlockSpec((nb, k), lambda j: (0, 0)), pl.BlockSpec((k, tc), lambda j: (0, j)),
                  pl.BlockSpec((1, tc), lambda j: (0, j))],
        out_specs=pl.BlockSpec((nb, tc), lambda j: (0, j)),
        out_shape=jax.ShapeDtypeStruct((nb, c), F32),
        compiler_params=_cp("parallel"), name=name,
    )(c_all, w, b)


def _silu_rows(c, *, name):
    def body(c_ref, o_ref):
        cv = c_ref[...]
        o_ref[...] = cv * _sigmoid(cv)

    return pl.pallas_call(body, out_shape=jax.ShapeDtypeStruct(c.shape, F32), name=name)(c)


def _pair_sum(g, got, idx, *, name):
    p, r, c = g.shape
    ax = _half_axis(r)
    hr, hc = _half_shape((r, c))
    tr, tc = _tile_2d(hr, hc)
    nbr, nbc = hr // tr, hc // tc

    def body(idx_ref, a_ref, b_ref, o_ref):
        o_ref[...] = (a_ref[...].astype(F32) + b_ref[...].astype(F32)).astype(BF16)

    def own_map(i, j, k, idx_ref):
        return (i, j + (idx_ref[0] * nbr if ax == 0 else 0), k + (idx_ref[0] * nbc if ax == 1 else 0))

    half_spec = pl.BlockSpec((1, tr, tc), lambda i, j, k, idx_ref: (i, j, k))
    return pl.pallas_call(
        body,
        grid_spec=pltpu.PrefetchScalarGridSpec(
            num_scalar_prefetch=1, grid=(p, nbr, nbc),
            in_specs=[pl.BlockSpec((1, tr, tc), own_map), half_spec],
            out_specs=half_spec),
        out_shape=jax.ShapeDtypeStruct((p, hr, hc), BF16),
        compiler_params=_cp("parallel", "parallel", "parallel"), name=name,
    )(idx, g, got)


def _final_sum(own, parts, idx, shard_shape, *, name):
    ax = _half_axis(shard_shape[0])
    hr, hc = own.shape[1:]
    tr, tc = _tile_2d(hr, hc)
    nbr, nbc = hr // tr, hc // tc

    def body(idx_ref, own_ref, parts_ref, o_ref):
        acc = own_ref[0].astype(F32)
        for q in range(3):
            acc = acc + parts_ref[q].astype(F32)
        o_ref[...] = acc

    def out_map(j, k, idx_ref):
        return (j + (idx_ref[0] * nbr if ax == 0 else 0), k + (idx_ref[0] * nbc if ax == 1 else 0))

    return pl.pallas_call(
        body,
        grid_spec=pltpu.PrefetchScalarGridSpec(
            num_scalar_prefetch=1, grid=(nbr, nbc),
            in_specs=[pl.BlockSpec((1, tr, tc), lambda j, k, idx_ref: (idx_ref[1], j, k)),
                      pl.BlockSpec((3, tr, tc), lambda j, k, idx_ref: (0, j, k))],
            out_specs=pl.BlockSpec((tr, tc), out_map)),
        out_shape=jax.ShapeDtypeStruct(tuple(shard_shape), F32),
        compiler_params=_cp("parallel", "parallel"), name=name,
    )(idx, own, parts)


def _stack_sum(x, *, name):
    p, r, c = x.shape
    tr = _row_tile(r)

    def body(x_ref, o_ref):
        acc = x_ref[0].astype(F32)
        for q in range(1, p):
            acc = acc + x_ref[q].astype(F32)
        o_ref[...] = acc

    return pl.pallas_call(
        body, grid=(r // tr,),
        in_specs=[pl.BlockSpec((p, tr, c), lambda i: (0, i, 0))],
        out_specs=pl.BlockSpec((tr, c), lambda i: (i, 0)),
        out_shape=jax.ShapeDtypeStruct((r, c), F32),
        compiler_params=_cp("parallel"), name=name,
    )(x)


def _place():
    x, y, c = lax.axis_index("x"), lax.axis_index("y"), lax.axis_index("c")
    chips = [(1 - x, y), (x, 1 - y), (1 - x, 1 - y)]
    return x, y, c, chips


def _gather8(x_shard, *, name):
    m_per, n = x_shard.shape

    def body(x_ref, out_ref, send_sems, recv_sems, local_sem):
        x, y, c, chips = _place()
        me, sibling = (x, y, c), (x, y, 1 - c)

        def rows(px, py, pc):
            return out_ref.at[pl.ds((4 * px + 2 * py + pc) * m_per, m_per), :]

        def copy(k, block, to, src=None):
            return pltpu.make_async_remote_copy(
                src_ref=rows(*block) if src is None else src, dst_ref=rows(*block),
                send_sem=send_sems.at[k], recv_sem=recv_sems.at[k], device_id=to, device_id_type=MESH)

        mine = pltpu.make_async_copy(x_ref, rows(*me), local_sem)
        mine.start()
        first = [copy(0, me, sibling, src=x_ref)]
        first += [copy(1 + j, me, (*chip, c), src=x_ref) for j, chip in enumerate(chips)]
        for cp in first:
            cp.start()
        passed = [copy(4 + j, (*chip, c), sibling) for j, chip in enumerate(chips)]
        for j, chip in enumerate(chips):
            copy(1 + j, (*chip, c), me).wait_recv()
            passed[j].start()
        copy(0, sibling, me).wait_recv()
        for j, chip in enumerate(chips):
            copy(4 + j, (*chip, 1 - c), me).wait_recv()
        for cp in first + passed:
            cp.wait_send()
        mine.wait()

    return pl.pallas_call(
        body,
        out_shape=jax.ShapeDtypeStruct((8 * m_per, n), x_shard.dtype),
        in_specs=[pl.BlockSpec(memory_space=pltpu.VMEM)],
        out_specs=pl.BlockSpec(memory_space=pltpu.VMEM),
        scratch_shapes=[pltpu.SemaphoreType.DMA((7,)), pltpu.SemaphoreType.DMA((7,)), pltpu.SemaphoreType.DMA],
        name=name,
    )(x_shard)


def _gather_weights(shards, *, name):
    return _comm_call(lambda ins, outs: [cp for i, o in zip(ins, outs) for cp in _plan_gather_ici(i, o)],
                      shards, [jax.ShapeDtypeStruct((4,) + s.shape, s.dtype) for s in shards], name=name)


def _plan_start(plan, send_sems, recv_sems):
    for k, (src, dst, _, peer) in enumerate(plan):
        pltpu.make_async_remote_copy(src_ref=src, dst_ref=dst, send_sem=send_sems.at[k], recv_sem=recv_sems.at[k],
                                     device_id=peer, device_id_type=MESH).start()


def _plan_wait(plan, send_sems, recv_sems):
    for k, (src, _, land, peer) in enumerate(plan):
        pltpu.make_async_remote_copy(src_ref=src, dst_ref=land, send_sem=send_sems.at[k], recv_sem=recv_sems.at[k],
                                     device_id=peer, device_id_type=MESH).wait_recv()
    for k, (src, dst, _, peer) in enumerate(plan):
        pltpu.make_async_remote_copy(src_ref=src, dst_ref=dst, send_sem=send_sems.at[k], recv_sem=recv_sems.at[k],
                                     device_id=peer, device_id_type=MESH).wait_send()


def _half_axis(rows):
    return 0 if rows % 32 == 0 else 1


def _rows_half(ref, hc, axis, part=None):
    size = ref.shape[axis] // 2
    start = hc * size
    if part is not None:
        size //= part[1]
        start = start + part[0] * size
    idx = [slice(None)] * len(ref.shape)
    idx[axis] = pl.ds(start, size)
    return ref.at[tuple(idx)]


def _plan_gather_ici(shard, full, part=None):
    x, y, c, chips = _place()
    ax = _half_axis(shard.shape[0])
    src = _rows_half(shard, c, ax, part)
    return [(src, _rows_half(full.at[2 * x + y], c, ax, part), _rows_half(full.at[2 * cx + cy], c, ax, part),
             (cx, cy, c)) for cx, cy in chips]


def _plan_gather_d2d(full):
    x, y, c, chips = _place()
    ax = _half_axis(full.shape[1])
    plan = []
    for cx, cy in chips:
        slot = full.at[2 * cx + cy]
        plan.append((_rows_half(slot, c, ax), _rows_half(slot, c, ax), _rows_half(slot, 1 - c, ax), (x, y, 1 - c)))
    return plan


def _plan_pair(grad, got):
    x, y, c, _ = _place()
    return [(_rows_half(grad, 1 - c, 1 + _half_axis(grad.shape[1])), got, got, (x, y, 1 - c))]


def _plan_shard_ici(sums, parts, piece=None):
    _, _, c, chips = _place()

    def rows(ref):
        if piece is None:
            return ref
        size = ref.shape[0] // piece[1]
        return ref.at[pl.ds(piece[0] * size, size), :]

    return [(rows(sums.at[2 * cx + cy]), rows(parts.at[k]), rows(parts.at[k]), (cx, cy, c))
            for k, (cx, cy) in enumerate(chips)]


def _plan_half(buf):
    x, y, c, _ = _place()
    ax = _half_axis(buf.shape[0])
    mine = _rows_half(buf, c, ax)
    return [(mine, mine, _rows_half(buf, 1 - c, ax), (x, y, 1 - c))]


def _comm_call(plan_fn, inputs, out_shapes, *, name, aliases=None):
    ni, no = len(inputs), len(out_shapes)

    def body(*refs):
        plan = plan_fn(refs[:ni], refs[ni:ni + no])
        send_sems, recv_sems = refs[ni + no:]
        _plan_start(plan, send_sems, recv_sems)
        _plan_wait(plan, send_sems, recv_sems)

    any_spec = pl.BlockSpec(memory_space=pl.ANY)
    n_copies = 3 * max(ni, no)
    return pl.pallas_call(
        body, out_shape=list(out_shapes), in_specs=[any_spec] * ni, out_specs=[any_spec] * no,
        scratch_shapes=[pltpu.SemaphoreType.DMA((n_copies,)), pltpu.SemaphoreType.DMA((n_copies,))],
        input_output_aliases=aliases or {}, name=name,
    )(*inputs)


def _gather_forward(fulls, *, name):
    return _comm_call(lambda ins, outs: [cp for o in outs for cp in _plan_gather_d2d(o)],
                      fulls, [jax.ShapeDtypeStruct(f.shape, f.dtype) for f in fulls], name=name,
                      aliases={k: k for k in range(len(fulls))})


def _pair_exchange(grads, *, name):
    return _comm_call(lambda ins, outs: [cp for i, o in zip(ins, outs) for cp in _plan_pair(i, o)],
                      grads, [jax.ShapeDtypeStruct(_half_shape(g.shape), g.dtype) for g in grads], name=name)


def _half_exchange(bufs, *, name):
    return _comm_call(lambda ins, outs: [cp for o in outs for cp in _plan_half(o)],
                      bufs, [jax.ShapeDtypeStruct(b.shape, b.dtype) for b in bufs], name=name,
                      aliases={k: k for k in range(len(bufs))})


def _split_w_in(w_in_t):
    d = w_in_t.shape[1]
    main = jnp.concatenate([w_in_t[0:3072], w_in_t[3080:5128], w_in_t[5144:6168]], axis=0)
    small = jnp.concatenate([w_in_t[3072:3080], w_in_t[5128:5144], jnp.zeros((SMALL_W - 24, d), w_in_t.dtype)], axis=0)
    return main, small


def _merge_dw_in(dw_main, dw_small):
    return jnp.concatenate([dw_main[0:3072], dw_small[0:8], dw_main[3072:5120], dw_small[8:24], dw_main[5120:6144]],
                           axis=0)


def _gather_side(shards):
    return _Side(shards, [jax.ShapeDtypeStruct((4,) + w.shape, w.dtype) for w in shards],
                 lambda ins, outs: [cp for i, o in zip(ins, outs) for cp in _plan_gather_ici(i, o)], 3 * len(shards))


def _finish_gather(fulls, owns, chip, *, name):
    fulls = _gather_forward(list(fulls), name=name)
    return [lax.dynamic_update_index_in_dim(f, o, chip, 0) for f, o in zip(fulls, owns)]


def _parts_shape(sums):
    return jax.ShapeDtypeStruct((3,) + sums.shape[1:], sums.dtype)


def _shard_side(sums):
    return _Side([sums], [_parts_shape(sums)], lambda ins, outs: _plan_shard_ici(ins[0], outs[0]), 3)


def _got_shape(grad):
    return jax.ShapeDtypeStruct(_half_shape(grad.shape), grad.dtype)


def _pair_side(grad):
    return _Side([grad], [_got_shape(grad)], lambda ins, outs: _plan_pair(ins[0], outs[0]), 1)


def _chip_sum(grad, idx, tag):
    got, = _pair_exchange([grad], name=f"grad_pair_exchange_{tag}")
    return _pair_sum(grad, got, idx, name=f"grad_pair_sum_{tag}")


def _local_step(x, target, mod, g_pre_mix, g_post_mix, g_pre_mlp, g_post_mlp, w_in_t, b_fgate, w_gla_a2,
                b_gla_a2, g_fox, g_gla, own_w_out, own_w_mlp_in, own_w_mlp_out, chip, idx):
    s, d = x.shape
    shift_m, scale_m, gate_m, shift_f, scale_f, gate_f = [mod[:, i * d:(i + 1) * d] for i in range(6)]
    a1 = g_pre_mix * (1.0 + scale_m)
    a2 = g_pre_mlp * (1.0 + scale_f)
    w_main, w_small = _split_w_in(w_in_t)
    bf = jnp.concatenate([b_fgate, jnp.zeros((1, SMALL_W - FOX_HEADS), F32)], axis=1)
    w2p = jnp.zeros((SMALL_W, GLA_KW), F32).at[FOX_HEADS:FOX_HEADS + GLA_RANK].set(w_gla_a2)

    h1 = _pre_fwd(x, a1, shift_m, name="pre_mix_fwd")
    full_shape = lambda w: jax.ShapeDtypeStruct((4,) + w.shape, w.dtype)
    first_side = _Side(
        [own_w_out, own_w_mlp_out], [full_shape(own_w_out), full_shape(own_w_mlp_out)],
        lambda ins, outs: _plan_gather_ici(ins[0], outs[0]) + _plan_gather_ici(ins[1], outs[1], part=(0, 4)), 6)
    proj, gw_out, gw_mlp_out = _mm(h1, w_main, mode="nt", out_dtypes=[BF16], name="in_proj_main", side=first_side)
    ps, = _mm(h1, w_small, mode="nt", out_dtypes=[F32], name="in_proj_small")
    gw_out, = _finish_gather([gw_out], [own_w_out], chip, name="gather_w_out_d2d")
    w_out_full = gw_out.reshape(-1, d)
    cum, log_a = _gates_fwd(ps, bf, w2p, b_gla_a2, name="gates_fwd")
    cum_t = cum[:, :FOX_HEADS].T
    o_fox, fox_n, lse, gw_mlp_in = _fox_fwd(proj, cum_t, g_fox, name="fox_fwd", side=_gather_side([own_w_mlp_in]))
    gw_mlp_in, = _finish_gather([gw_mlp_in], [own_w_mlp_in], chip, name="gather_w_mlp_in_d2d")
    o_gla, gla_n, states = _gla_fwd(proj, log_a, g_gla, name="gla_fwd")
    mixed = jnp.concatenate([fox_n, gla_n], axis=1)
    y1, = _mm(mixed, w_out_full, mode="nn", out_dtypes=[F32], name="out_proj")
    x1, h2 = _post_pre_fwd(x, y1, gate_m, g_post_mix, a2, shift_f, name="post_mix_pre_mlp_fwd")

    def mlp_act(acc):
        r = jnp.maximum(acc, 0.0)
        return acc, r * r

    rest_side = _Side([own_w_mlp_out, gw_mlp_out], [full_shape(own_w_mlp_out)],
                      lambda ins, outs: [cp for q in (1, 2, 3) for cp in _plan_gather_ici(ins[0], outs[0], part=(q, 4))],
                      9, aliases={1: 0})
    u, act, gw_mlp_out = _mm(h2, gw_mlp_in, mode="nn", out_dtypes=[BF16, BF16], epi=mlp_act, name="mlp_in",
                             b_slots=4, tm=MM_TM, side=rest_side)
    gw_mlp_out, = _finish_gather([gw_mlp_out], [own_w_mlp_out], chip, name="gather_w_mlp_out_d2d")
    w_mlp_out_full = gw_mlp_out.reshape(-1, d)
    y2, = _mm(act, w_mlp_out_full, mode="nn", out_dtypes=[F32], name="mlp_out")
    dx2, dy2, loss_part, dgate_f, dg_post_mlp = _post_loss_bwd(x1, y2, gate_f, g_post_mlp, target,
                                                               name="post_mlp_loss_bwd")
    dw_mlp_out, = _mm(act, dy2, mode="tn", out_dtypes=[BF16], name="dw_mlp_out")
    dw_mlp_out = dw_mlp_out.reshape(4, D_FF // 4, d)

    def act_bwd(acc, uv):
        return (acc * (2.0 * jnp.maximum(uv.astype(F32), 0.0)),)

    du, got_mlp_out = _mm(dy2, w_mlp_out_full, mode="nt", out_dtypes=[BF16], extras=[u], epi=act_bwd,
                          name="d_mlp_hidden", tm=MM_TM, side=_pair_side(dw_mlp_out))
    sum_mlp_out = _pair_sum(dw_mlp_out, got_mlp_out, idx, name="grad_pair_sum_mlp_out")
    nj = D_FF // 4 // min(MM_T, D_FF // 4)
    tmw = min(MM_T, d)
    dw_mlp_in, parts_mlp_out = _mm(
        h2, du, mode="tn", out_dtypes=[BF16], name="dw_mlp_in",
        out_shapes=[jax.ShapeDtypeStruct((4, d, D_FF // 4), BF16)],
        out_specs=[pl.BlockSpec((1, tmw, min(MM_T, D_FF // 4)), lambda i, j, kk: (j // nj, i, j % nj))],
        side=_Side([sum_mlp_out], [_parts_shape(sum_mlp_out)],
                   lambda ins, outs: _plan_shard_ici(ins[0], outs[0], piece=(0, 2)), 3))
    dh2, got_mlp_in, parts_mlp_out = _mm(
        du, gw_mlp_in, mode="nt", out_dtypes=[F32], name="d_mlp_in", b_slots=4,
        side=_Side([dw_mlp_in, sum_mlp_out, parts_mlp_out], [_got_shape(dw_mlp_in), _parts_shape(sum_mlp_out)],
                   lambda ins, outs: _plan_pair(ins[0], outs[0]) + _plan_shard_ici(ins[1], outs[1], piece=(1, 2)),
                   4, aliases={2: 1}))
    sum_mlp_in = _pair_sum(dw_mlp_in, got_mlp_in, idx, name="grad_pair_sum_mlp_in")
    dx1, dshift_f, da2, dy1, dgate_m, dg_post_mix = _pre_post_bwd(dh2, x1, dx2, a2, y1, gate_m, g_post_mix,
                                                                  name="pre_mlp_post_mix_bwd")
    dw_out, = _mm(mixed, dy1, mode="tn", out_dtypes=[BF16], name="dw_out")
    dw_out = dw_out.reshape(4, d // 4, d)
    dmixed, got_out = _mm(dy1, w_out_full, mode="nt", out_dtypes=[BF16], name="d_mixed", side=_pair_side(dw_out))
    sum_out = _pair_sum(dw_out, got_out, idx, name="grad_pair_sum_out")
    do_fox, delta, dg_fox = _head_norm_bwd(dmixed, o_fox, g_fox, None, nh=FOX_HEADS, hd=FOX_HD, dn_col=0,
                                           gr_col=0, name="fox_norm_bwd")
    do_gla, dgr, _, dg_gla = _head_norm_bwd(dmixed, o_gla, g_gla, proj, nh=GLA_HEADS, hd=GLA_DV, dn_col=1,
                                            gr_col=(3 * FOX_W + 2 * GLA_KW + GLA_W) // GLA_W, name="gla_norm_bwd")
    dq_fox, dk_fox, dv_fox, dcq, dck_t, parts_mlp_in, parts_out = _fox_bwd(
        proj, do_fox, cum_t, lse, delta, name="fox_bwd",
        side=_Side([sum_mlp_in, sum_out], [_parts_shape(sum_mlp_in), _parts_shape(sum_out)],
                   lambda ins, outs: _plan_shard_ici(ins[0], outs[0]) + _plan_shard_ici(ins[1], outs[1]), 6))
    dgq, dgk, dgv, dla = _gla_bwd(proj, log_a, do_gla, states, name="gla_bwd")
    dck = dcq + jnp.concatenate([dck_t.T, jnp.zeros((s, SMALL_W - FOX_HEADS), F32)], axis=1)
    dps, dbf, dw2p, db2 = _gates_bwd(dck, ps, bf, w2p, b_gla_a2, dla, name="gates_bwd")
    dproj = jnp.concatenate([dq_fox.astype(BF16), dk_fox, dv_fox, dgq, dgk, dgv, dgr], axis=1)
    dw_main, = _mm(dproj, h1, mode="tn", out_dtypes=[BF16], name="dw_in_main")
    dw_small, = _mm(dps, h1, mode="tn", out_dtypes=[BF16], name="dw_in_small")
    rs_in = w_in_t.shape[0] // 4
    dw_in = _merge_dw_in(dw_main, dw_small).reshape(4, rs_in, d)
    sum_in = _chip_sum(dw_in, idx, "in")
    dh1_small, = _mm(dps, w_small, mode="nn", out_dtypes=[F32], name="d_h1_small")
    dh1, parts_in = _mm(dproj, w_main, mode="nn", out_dtypes=[F32], extras=[dh1_small],
                        epi=lambda acc, e: (acc + e,), name="d_h1", side=_shard_side(sum_in))
    grad_x, dshift_m, da1 = _pre_bwd(dh1, x, dx1, a1, name="pre_mix_bwd")
    bufs = [_final_sum(sm, pt, idx, shp, name=f"grad_final_sum_{tag}")
            for tag, sm, pt, shp in [("in", sum_in, parts_in, (rs_in, d)), ("out", sum_out, parts_out, (d // 4, d)),
                                     ("mlp_in", sum_mlp_in, parts_mlp_in, (d, D_FF // 4)),
                                     ("mlp_out", sum_mlp_out, parts_mlp_out, (D_FF // 4, d))]]

    dmod = jnp.concatenate([dshift_m, da1 * g_pre_mix, dgate_m, dshift_f, da2 * g_pre_mlp, dgate_f], axis=1)
    small = dict(
        dmod=dmod, g_pre_mix=da1 * (1.0 + scale_m), g_post_mix=dg_post_mix, g_pre_mlp=da2 * (1.0 + scale_f),
        g_post_mlp=dg_post_mlp, b_fgate=dbf[:, :FOX_HEADS], w_gla_a2=dw2p[FOX_HEADS:FOX_HEADS + GLA_RANK],
        b_gla_a2=db2, g_fox_out=dg_fox, g_gla_out=dg_gla)
    return loss_part, grad_x, bufs, small


def _pack(arrays):
    flat = jnp.concatenate([a.reshape(-1).astype(F32) for a in arrays])
    n = flat.shape[0]
    rows = -(-n // 128)
    rows = -(-rows // 8) * 8
    return jnp.pad(flat, (0, rows * 128 - n)).reshape(rows, 128)


def _unpack(buf, shapes):
    flat = buf.reshape(-1)
    out, off = [], 0
    for shp in shapes:
        n = 1
        for q in shp:
            n *= q
        out.append(flat[off:off + n].reshape(shp))
        off += n
    return out


SMALL_GRAD_ORDER = ["dmod", "g_pre_mix", "g_post_mix", "g_pre_mlp", "g_post_mlp", "b_fgate", "w_gla_a2", "b_gla_a2",
                    "g_fox_out", "g_gla_out"]


def kernel(x, c, w_ada, b_ada, g_pre_mix, g_post_mix, w_in, b_fgate, w_gla_a2, b_gla_a2, g_fox_out, g_gla_out, w_out, g_pre_mlp, g_post_mlp, w_mlp_in, w_mlp_out, loss_target, m_w_ada, m_b_ada, m_g_pre_mix, m_g_post_mix, m_w_in, m_b_fgate, m_w_gla_a2, m_b_gla_a2, m_g_fox_out, m_g_gla_out, m_w_out, m_g_pre_mlp, m_g_post_mlp, m_w_mlp_in, m_w_mlp_out, v_w_ada, v_b_ada, v_g_pre_mix, v_g_post_mix, v_w_in, v_b_fgate, v_w_gla_a2, v_b_gla_a2, v_g_fox_out, v_g_gla_out, v_w_out, v_g_pre_mlp, v_g_post_mlp, v_w_mlp_in, v_w_mlp_out):
    ix, iy, ic = lax.axis_index("x"), lax.axis_index("y"), lax.axis_index("c")
    chip = 2 * ix + iy
    dev = 4 * ix + 2 * iy + ic
    d = D_MODEL

    c_act = _silu_rows(c, name="silu_c")
    pack1 = _pack([c_act, w_gla_a2[0], g_gla_out[0]])
    rows1 = pack1.shape[0]
    got1 = _gather8(pack1, name="gather_small_fwd").reshape(8, rows1, 128)
    per_dev = [_unpack(got1[q], [(d,), (GLA_RANK, GLA_KW // 4), (GLA_HEADS, GLA_DV // 4)]) for q in range(8)]
    c_all = jnp.stack([p[0] for p in per_dev])
    w_gla_a2_full = jnp.concatenate([per_dev[2 * j][1] for j in range(4)], axis=1)
    g_gla_full = jnp.concatenate([per_dev[2 * j][2] for j in range(4)], axis=1)
    cols = w_ada.shape[2]
    b_ada_shard = lax.dynamic_slice_in_dim(b_ada, chip * cols, cols, axis=1)
    mod_sh = _mod_shard(c_all, w_ada[0], b_ada_shard, name="ada_mod")
    got2 = _gather8(mod_sh, name="gather_mod").reshape(8, 8, cols)
    mod_all = jnp.concatenate([got2[2 * j] for j in range(4)], axis=1)
    mod = lax.dynamic_slice_in_dim(mod_all, dev, 1, axis=0)

    tr_in = lambda a: jnp.transpose(a[0])
    own_bf = [tr_in(w_in).astype(BF16), w_out[0].astype(BF16), w_mlp_in[0].astype(BF16), w_mlp_out[0].astype(BF16)]
    gw_in, = _finish_gather(_gather_weights(own_bf[:1], name="gather_w_in_ici"), own_bf[:1], chip,
                            name="gather_w_in_d2d")
    w_in_t = gw_in.reshape(-1, d)
    idx = jnp.stack([ic, chip]).astype(jnp.int32)

    loss_part, grad_x, bufs, small = _local_step(
        x[0], loss_target[0], mod, g_pre_mix, g_post_mix, g_pre_mlp, g_post_mlp, w_in_t, b_fgate,
        w_gla_a2_full, b_gla_a2, g_fox_out[0], g_gla_full, own_bf[1], own_bf[2], own_bf[3], chip, idx)
    loss = lax.psum(loss_part[0, 0], ("x", "y", "c"))

    g_big = _half_exchange(bufs, name="grad_half_exchange")
    big_w = [(tr_in(w_in), tr_in(m_w_in), tr_in(v_w_in)), (w_out[0], m_w_out[0], v_w_out[0]),
             (w_mlp_in[0], m_w_mlp_in[0], v_w_mlp_in[0]), (w_mlp_out[0], m_w_mlp_out[0], v_w_mlp_out[0])]
    big_res = []
    for q, (g, (w, m, v)) in enumerate(zip(g_big, big_w)):
        res4 = (g,) + tuple(_adam(g, w, m, v, name=f"adam_big_{q}"))
        big_res.append(tuple((jnp.transpose(a) if q == 0 else a)[None] for a in res4))

    pack2 = _pack([small[k] for k in SMALL_GRAD_ORDER])
    rows2 = pack2.shape[0]
    got3 = _gather8(pack2, name="gather_small_grads").reshape(8, rows2, 128)
    dmod_all = got3[:, :6 * d // 128, :].reshape(8, 6 * d)
    sums = _stack_sum(got3, name="small_grad_sum")
    shapes = [(1, 6 * d), (1, d), (1, d), (1, d), (1, d), (1, FOX_HEADS), (1, GLA_RANK, GLA_KW), (1, GLA_KW),
              (1, FOX_HEADS, FOX_HD), (1, GLA_HEADS, GLA_DV)]
    sg = dict(zip(["b_ada"] + SMALL_GRAD_ORDER[1:], _unpack(sums, shapes)))
    sg["w_gla_a2"] = lax.dynamic_slice_in_dim(sg["w_gla_a2"], chip * (GLA_KW // 4), GLA_KW // 4, axis=2)
    sg["g_gla_out"] = lax.dynamic_slice_in_dim(sg["g_gla_out"], chip * (GLA_DV // 4), GLA_DV // 4, axis=2)
    small_names = ["b_ada", "g_pre_mix", "g_post_mix", "b_fgate", "w_gla_a2", "b_gla_a2", "g_fox_out", "g_gla_out",
                   "g_pre_mlp", "g_post_mlp"]
    small_w = dict(b_ada=(b_ada, m_b_ada, v_b_ada), g_pre_mix=(g_pre_mix, m_g_pre_mix, v_g_pre_mix),
                   g_post_mix=(g_post_mix, m_g_post_mix, v_g_post_mix), b_fgate=(b_fgate, m_b_fgate, v_b_fgate),
                   w_gla_a2=(w_gla_a2, m_w_gla_a2, v_w_gla_a2), b_gla_a2=(b_gla_a2, m_b_gla_a2, v_b_gla_a2),
                   g_fox_out=(g_fox_out, m_g_fox_out, v_g_fox_out), g_gla_out=(g_gla_out, m_g_gla_out, v_g_gla_out),
                   g_pre_mlp=(g_pre_mlp, m_g_pre_mlp, v_g_pre_mlp), g_post_mlp=(g_post_mlp, m_g_post_mlp, v_g_post_mlp))
    sshapes = [small_w[k][0].shape for k in small_names]
    pg = _pack([sg[k] for k in small_names])
    pw, pm, pv = [_pack([small_w[k][q] for k in small_names]) for q in range(3)]
    pd, pmn, pvn = _adam(pg, pw, pm, pv, name="adam_small")
    s_delta = dict(zip(small_names, _unpack(pd, sshapes)))
    s_m = dict(zip(small_names, _unpack(pmn, sshapes)))
    s_v = dict(zip(small_names, _unpack(pvn, sshapes)))

    dmod_cols = lax.dynamic_slice_in_dim(dmod_all, chip * cols, cols, axis=1)
    g_ada, d_ada, m_ada, v_ada = _ada_grad_adam(c_all.T, dmod_cols, w_ada[0], m_w_ada[0], v_w_ada[0], name="ada_grad_adam")

    order = ["w_ada", "b_ada", "g_pre_mix", "g_post_mix", "w_in", "b_fgate", "w_gla_a2", "b_gla_a2", "g_fox_out",
             "g_gla_out", "w_out", "g_pre_mlp", "g_post_mlp", "w_mlp_in", "w_mlp_out"]
    res = {"w_ada": (g_ada[None], d_ada[None], m_ada[None], v_ada[None]),
           "w_in": big_res[0], "w_out": big_res[1], "w_mlp_in": big_res[2], "w_mlp_out": big_res[3]}
    for k in small_names:
        res[k] = (sg[k], s_delta[k], s_m[k], s_v[k])
    return (loss, grad_x[None], *[res[k][0] for k in order], *[res[k][1] for k in order],
            *[res[k][2] for k in order], *[res[k][3] for k in order])
```

```python
import functools

import jax
import jax.numpy as jnp
from jax import lax
from jax.experimental import pallas as pl
from jax.experimental.pallas import tpu as pltpu

F32 = jnp.float32
BF16 = jnp.bfloat16
MESH = pl.DeviceIdType.MESH
HIGHEST = lax.Precision.HIGHEST

D_MODEL = 2048
FOX_HEADS = 8
FOX_HD = 128
FOX_W = FOX_HEADS * FOX_HD
GLA_HEADS = 4
GLA_DK = 128
GLA_DV = 256
GLA_KW = GLA_HEADS * GLA_DK
GLA_W = GLA_HEADS * GLA_DV
GLA_RANK = 16
GLA_TEMP = 16.0
CHUNK = 64
D_FF = 4 * D_MODEL
EPS = 1e-6
MAIN_W = 3 * FOX_W + 2 * GLA_KW + 2 * GLA_W
SMALL_W = 128
NEG = -1e30

ADAM_LR = 0.001
ADAM_B1 = 0.9
ADAM_B2 = 0.999
ADAM_EPS = 1e-08
ADAM_WD = 0.01
ADAM_STEP = 10

VMEM_LIMIT = 52 * 1024 * 1024
ROW_TILE = 256
FOX_TQ = 512
FOX_TK = 512
GLA_ROWS = 512
GATE_TS = 512
MM_T = 1024
MM_TK = 2048
MM_TM = 2048


def _cp(*sem):
    return pltpu.CompilerParams(dimension_semantics=sem, vmem_limit_bytes=VMEM_LIMIT)


def _dot_nn(a, b, precision=None):
    return jnp.dot(a, b, preferred_element_type=F32, precision=precision)


def _dot_nt(a, b, precision=None):
    return lax.dot_general(a, b, (((1,), (1,)), ((), ())), preferred_element_type=F32, precision=precision)


def _dot_tn(a, b, precision=None):
    return lax.dot_general(a, b, (((0,), (0,)), ((), ())), preferred_element_type=F32, precision=precision)


def _sigmoid(x):
    return 1.0 / (1.0 + jnp.exp(-x))


def _log_sigmoid(x):
    return jnp.minimum(x, 0.0) - jnp.log(1.0 + jnp.exp(-jnp.abs(x)))


class _Side:
    def __init__(self, inputs, out_shapes, plan_fn, n_copies, aliases=None):
        self.inputs, self.out_shapes, self.plan_fn, self.n_copies = list(inputs), list(out_shapes), plan_fn, n_copies
        self.aliases = dict(aliases or {})

    def scratch(self):
        return [pltpu.SemaphoreType.DMA((self.n_copies,)), pltpu.SemaphoreType.DMA((self.n_copies,))]

    def run(self, in_refs, out_refs, sems, first, last):
        @pl.when(first)
        def _():
            _plan_start(self.plan_fn(in_refs, out_refs), *sems)

        @pl.when(last)
        def _():
            _plan_wait(self.plan_fn(in_refs, out_refs), *sems)


def _mm(a, b, *, mode, out_dtypes, name, tm=None, tn=None, tk=None, extras=(), epi=None,
        out_shapes=None, out_specs=None, side=None, b_slots=0):
    tm, tn, tk = tm or MM_T, tn or MM_T, tk or MM_TK
    b2 = (b.shape[1], b_slots * b.shape[2]) if b_slots else b.shape
    if mode == "nn":
        (m, k), n = a.shape, b2[1]
    elif mode == "nt":
        (m, k), n = a.shape, b2[0]
    else:
        (k, m), n = a.shape, b2[1]
    tm, tn, tk = min(tm, m), min(tn, n), min(tk, k)
    if b_slots:
        tn = min(tn, b.shape[2]) if mode == "nn" else tn
        tk = min(tk, b.shape[2]) if mode == "nt" else tk
    assert m % tm == 0 and n % tn == 0 and k % tk == 0, (name, m, n, k)
    nk = k // tk
    n_out, n_ex = len(out_dtypes), len(extras)
    if epi is None:
        epi = lambda acc: tuple(acc for _ in range(n_out))
    dot = {"nn": _dot_nn, "nt": _dot_nt, "tn": _dot_tn}[mode]

    n_si = len(side.inputs) if side else 0
    n_so = len(side.out_shapes) if side else 0
    grid = (m // tm, n // tn, nk)

    def body(*refs):
        a_ref, b_ref = refs[0], refs[1]
        ex_refs = refs[2:2 + n_ex]
        base = 2 + n_ex + n_si
        o_refs = refs[base:base + n_out]
        scratch = refs[base + n_out + n_so:]
        if side:
            pos = [pl.program_id(q) for q in range(3)]
            first = (pos[0] == 0) & (pos[1] == 0) & (pos[2] == 0)
            last = (pos[0] == grid[0] - 1) & (pos[1] == grid[1] - 1) & (pos[2] == grid[2] - 1)
            side.run(refs[2 + n_ex:base], refs[base + n_out:base + n_out + n_so], scratch[-2:], first, last)
        part = dot(a_ref[...], b_ref[...])

        def finish(acc):
            outs = epi(acc, *[e[...] for e in ex_refs])
            for o_ref, val in zip(o_refs, outs):
                o_ref[...] = val.reshape(o_ref.shape).astype(o_ref.dtype)

        if nk == 1:
            finish(part)
        else:
            acc_ref = scratch[0]
            kk = pl.program_id(2)

            @pl.when(kk == 0)
            def _():
                acc_ref[...] = part

            @pl.when(kk > 0)
            def _():
                acc_ref[...] += part

            @pl.when(kk == nk - 1)
            def _():
                finish(acc_ref[...])

    if mode == "nn":
        a_spec = pl.BlockSpec((tm, tk), lambda i, j, kk: (i, kk))
        b_spec = pl.BlockSpec((tk, tn), lambda i, j, kk: (kk, j))
        if b_slots:
            per = b.shape[2] // tn
            b_spec = pl.BlockSpec((None, tk, tn), lambda i, j, kk: (j // per, kk, j % per))
    elif mode == "nt":
        a_spec = pl.BlockSpec((tm, tk), lambda i, j, kk: (i, kk))
        b_spec = pl.BlockSpec((tn, tk), lambda i, j, kk: (j, kk))
        if b_slots:
            per = b.shape[2] // tk
            b_spec = pl.BlockSpec((None, tn, tk), lambda i, j, kk: (kk // per, j, kk % per))
    else:
        assert not b_slots
        a_spec = pl.BlockSpec((tk, tm), lambda i, j, kk: (kk, i))
        b_spec = pl.BlockSpec((tk, tn), lambda i, j, kk: (kk, j))
    tile_spec = pl.BlockSpec((tm, tn), lambda i, j, kk: (i, j))
    if out_shapes is None:
        out_shapes = [jax.ShapeDtypeStruct((m, n), dt) for dt in out_dtypes]
    if out_specs is None:
        out_specs = [tile_spec for _ in out_dtypes]
    any_spec = pl.BlockSpec(memory_space=pl.ANY)
    res = pl.pallas_call(
        body,
        grid=grid,
        in_specs=[a_spec, b_spec] + [tile_spec for _ in extras] + [any_spec] * n_si,
        out_specs=list(out_specs) + [any_spec] * n_so,
        out_shape=list(out_shapes) + (side.out_shapes if side else []),
        scratch_shapes=([pltpu.VMEM((tm, tn), F32)] if nk > 1 else []) + (side.scratch() if side else []),
        compiler_params=_cp("arbitrary", "arbitrary", "arbitrary") if side else _cp("parallel", "parallel", "arbitrary"),
        input_output_aliases={2 + n_ex + si: n_out + so for si, so in side.aliases.items()} if side else {},
        name=name,
    )(a, b, *extras, *(side.inputs if side else []))
    return res


def _row_spec(ts, d):
    return pl.BlockSpec((ts, d), lambda i: (i, 0))


def _vec_spec(d):
    return pl.BlockSpec((1, d), lambda i: (0, 0))


def _pre_fwd(x, avec, shift, *, name):
    s, d = x.shape
    ts = min(ROW_TILE, s)

    def body(x_ref, a_ref, s_ref, h_ref):
        xv = x_ref[...]
        r = lax.rsqrt(jnp.mean(xv * xv, axis=-1, keepdims=True) + EPS)
        h_ref[...] = (xv * r * a_ref[...] + s_ref[...]).astype(BF16)

    return pl.pallas_call(
        body, grid=(s // ts,),
        in_specs=[_row_spec(ts, d), _vec_spec(d), _vec_spec(d)],
        out_specs=_row_spec(ts, d),
        out_shape=jax.ShapeDtypeStruct((s, d), BF16),
        compiler_params=_cp("parallel"), name=name,
    )(x, avec, shift)


def _post_pre_fwd(x, y, gate, g, avec, shift, *, name):
    s, d = x.shape
    ts = min(ROW_TILE, s)

    def body(x_ref, y_ref, gate_ref, g_ref, a_ref, s_ref, o_ref, h_ref):
        yv = y_ref[...]
        r = lax.rsqrt(jnp.mean(yv * yv, axis=-1, keepdims=True) + EPS)
        x1 = x_ref[...] + gate_ref[...] * (yv * r * g_ref[...])
        o_ref[...] = x1
        r1 = lax.rsqrt(jnp.mean(x1 * x1, axis=-1, keepdims=True) + EPS)
        h_ref[...] = (x1 * r1 * a_ref[...] + s_ref[...]).astype(BF16)

    return pl.pallas_call(
        body, grid=(s // ts,),
        in_specs=[_row_spec(ts, d), _row_spec(ts, d)] + [_vec_spec(d)] * 4,
        out_specs=[_row_spec(ts, d), _row_spec(ts, d)],
        out_shape=[jax.ShapeDtypeStruct((s, d), F32), jax.ShapeDtypeStruct((s, d), BF16)],
        compiler_params=_cp("parallel"), name=name,
    )(x, y, gate, g, avec, shift)


def _post_bwd_math(dxv, yv, gatev, gv):
    r = lax.rsqrt(jnp.mean(yv * yv, axis=-1, keepdims=True) + EPS)
    yhat = yv * r
    dn = dxv * gatev
    dyhat = dn * gv
    dy = r * (dyhat - yhat * jnp.mean(dyhat * yhat, axis=-1, keepdims=True))
    return dy, dxv * (yhat * gv), dn * yhat


def _accumulate(first, pairs):
    @pl.when(first)
    def _():
        for ref, _ in pairs:
            ref[...] = jnp.zeros_like(ref)

    for ref, val in pairs:
        ref[...] += jnp.sum(val, axis=0, keepdims=True)


def _post_loss_bwd(x, y, gate, g, target, *, name):
    s, d = x.shape
    ts = min(ROW_TILE, s)

    def body(x_ref, y_ref, gate_ref, g_ref, t_ref, dx_ref, dy_ref, loss_ref, dgate_ref, dg_ref):
        yv, gatev, gv = y_ref[...], gate_ref[...], g_ref[...]
        r = lax.rsqrt(jnp.mean(yv * yv, axis=-1, keepdims=True) + EPS)
        diff = x_ref[...] + gatev * (yv * r * gv) - t_ref[...]
        dxv = diff * (1.0 / d)
        dx_ref[...] = dxv
        dy, dgate_rows, dg_rows = _post_bwd_math(dxv, yv, gatev, gv)
        dy_ref[...] = dy.astype(BF16)
        first = pl.program_id(0) == 0
        _accumulate(first, [(dgate_ref, dgate_rows), (dg_ref, dg_rows)])

        @pl.when(first)
        def _():
            loss_ref[...] = jnp.zeros_like(loss_ref)

        loss_ref[...] += jnp.sum(jnp.mean(diff * diff, axis=-1, keepdims=True)) * 0.5

    return pl.pallas_call(
        body, grid=(s // ts,),
        in_specs=[_row_spec(ts, d), _row_spec(ts, d), _vec_spec(d), _vec_spec(d), _row_spec(ts, d)],
        out_specs=[_row_spec(ts, d), _row_spec(ts, d), pl.BlockSpec((1, 128), lambda i: (0, 0)), _vec_spec(d),
                   _vec_spec(d)],
        out_shape=[jax.ShapeDtypeStruct((s, d), F32), jax.ShapeDtypeStruct((s, d), BF16),
                   jax.ShapeDtypeStruct((1, 128), F32), jax.ShapeDtypeStruct((1, d), F32),
                   jax.ShapeDtypeStruct((1, d), F32)],
        compiler_params=_cp("arbitrary"), name=name,
    )(x, y, gate, g, target)


def _pre_post_bwd(dh, xin, dres, avec, y, gate, g, *, name):
    s, d = xin.shape
    ts = min(ROW_TILE, s)

    def body(dh_ref, x_ref, dres_ref, a_ref, y_ref, gate_ref, g_ref, dx_ref, dshift_ref, da_ref, dy_ref,
             dgate_ref, dg_ref):
        xv, dhv = x_ref[...], dh_ref[...]
        r = lax.rsqrt(jnp.mean(xv * xv, axis=-1, keepdims=True) + EPS)
        xhat = xv * r
        dxhat = dhv * a_ref[...]
        dxv = dres_ref[...] + r * (dxhat - xhat * jnp.mean(dxhat * xhat, axis=-1, keepdims=True))
        dx_ref[...] = dxv
        dy, dgate_rows, dg_rows = _post_bwd_math(dxv, y_ref[...], gate_ref[...], g_ref[...])
        dy_ref[...] = dy.astype(BF16)
        _accumulate(pl.program_id(0) == 0, [(dshift_ref, dhv), (da_ref, dhv * xhat), (dgate_ref, dgate_rows),
                                            (dg_ref, dg_rows)])

    return pl.pallas_call(
        body, grid=(s // ts,),
        in_specs=[_row_spec(ts, d), _row_spec(ts, d), _row_spec(ts, d), _vec_spec(d), _row_spec(ts, d),
                  _vec_spec(d), _vec_spec(d)],
        out_specs=[_row_spec(ts, d), _vec_spec(d), _vec_spec(d), _row_spec(ts, d), _vec_spec(d), _vec_spec(d)],
        out_shape=[jax.ShapeDtypeStruct((s, d), F32), jax.ShapeDtypeStruct((1, d), F32),
                   jax.ShapeDtypeStruct((1, d), F32), jax.ShapeDtypeStruct((s, d), BF16),
                   jax.ShapeDtypeStruct((1, d), F32), jax.ShapeDtypeStruct((1, d), F32)],
        compiler_params=_cp("arbitrary"), name=name,
    )(dh, xin, dres, avec, y, gate, g)


def _pre_bwd(dh, xin, dres, avec, *, name):
    s, d = xin.shape
    ts = min(ROW_TILE, s)

    def body(dh_ref, x_ref, dres_ref, a_ref, dx_ref, dshift_ref, da_ref):
        xv, dhv = x_ref[...], dh_ref[...]
        r = lax.rsqrt(jnp.mean(xv * xv, axis=-1, keepdims=True) + EPS)
        xhat = xv * r
        dxhat = dhv * a_ref[...]
        dx_ref[...] = dres_ref[...] + r * (dxhat - xhat * jnp.mean(dxhat * xhat, axis=-1, keepdims=True))

        @pl.when(pl.program_id(0) == 0)
        def _():
            dshift_ref[...] = jnp.zeros_like(dshift_ref)
            da_ref[...] = jnp.zeros_like(da_ref)

        dshift_ref[...] += jnp.sum(dhv, axis=0, keepdims=True)
        da_ref[...] += jnp.sum(dhv * xhat, axis=0, keepdims=True)

    return pl.pallas_call(
        body, grid=(s // ts,),
        in_specs=[_row_spec(ts, d), _row_spec(ts, d), _row_spec(ts, d), _vec_spec(d)],
        out_specs=[_row_spec(ts, d), _vec_spec(d), _vec_spec(d)],
        out_shape=[jax.ShapeDtypeStruct((s, d), F32), jax.ShapeDtypeStruct((1, d), F32),
                   jax.ShapeDtypeStruct((1, d), F32)],
        compiler_params=_cp("arbitrary"), name=name,
    )(dh, xin, dres, avec)


def _tri(n, strict=False, upper=False):
    r = lax.broadcasted_iota(jnp.int32, (n, n), 0)
    c = lax.broadcasted_iota(jnp.int32, (n, n), 1)
    if upper:
        r, c = c, r
    return ((r > c) if strict else (r >= c)).astype(F32)


def _gates_fwd(ps, bf, w2p, b2, *, name):
    s = ps.shape[0]
    ts = min(GATE_TS, s)

    def body(ps_ref, bf_ref, w_ref, b2_ref, cum_ref, la_ref, carry_ref):
        @pl.when(pl.program_id(0) == 0)
        def _():
            carry_ref[...] = jnp.zeros_like(carry_ref)

        psv = ps_ref[...]
        lf = _log_sigmoid(psv + bf_ref[...])
        cum = _dot_nn(_tri(ts), lf, HIGHEST) + carry_ref[...]
        cum_ref[...] = cum
        carry_ref[...] = cum[ts - 1:ts, :]
        z = _dot_nn(psv, w_ref[...], HIGHEST) + b2_ref[...]
        la_ref[...] = _log_sigmoid(z) * (1.0 / GLA_TEMP)

    return pl.pallas_call(
        body, grid=(s // ts,),
        in_specs=[_row_spec(ts, SMALL_W), _vec_spec(SMALL_W),
                  pl.BlockSpec((SMALL_W, GLA_KW), lambda i: (0, 0)), _vec_spec(GLA_KW)],
        out_specs=[_row_spec(ts, SMALL_W), _row_spec(ts, GLA_KW)],
        out_shape=[jax.ShapeDtypeStruct((s, SMALL_W), F32), jax.ShapeDtypeStruct((s, GLA_KW), F32)],
        scratch_shapes=[pltpu.VMEM((1, SMALL_W), F32)],
        compiler_params=_cp("arbitrary"), name=name,
    )(ps, bf, w2p, b2)


def _gates_bwd(dck, ps, bf, w2p, b2, dla, *, name):
    s = ps.shape[0]
    ts = min(GATE_TS, s)
    nb = s // ts
    rev = lambda i: (nb - 1 - i, 0)

    def body(dck_ref, ps_ref, bf_ref, w_ref, b2_ref, dla_ref, dps_ref, dbf_ref, dw_ref, db2_ref, carry_ref):
        @pl.when(pl.program_id(0) == 0)
        def _():
            carry_ref[...] = jnp.zeros_like(carry_ref)
            dbf_ref[...] = jnp.zeros_like(dbf_ref)
            dw_ref[...] = jnp.zeros_like(dw_ref)
            db2_ref[...] = jnp.zeros_like(db2_ref)

        psv, dckv = ps_ref[...], dck_ref[...]
        dlf = _dot_nn(_tri(ts, upper=True), dckv, HIGHEST) + carry_ref[...]
        carry_ref[...] += jnp.sum(dckv, axis=0, keepdims=True)
        lane = lax.broadcasted_iota(jnp.int32, (ts, SMALL_W), 1)
        dff = jnp.where(lane < FOX_HEADS, dlf * _sigmoid(-(psv + bf_ref[...])), 0.0)
        z = _dot_nn(psv, w_ref[...], HIGHEST) + b2_ref[...]
        dz = dla_ref[...] * _sigmoid(-z) * (1.0 / GLA_TEMP)
        dps_ref[...] = (_dot_nt(dz, w_ref[...], HIGHEST) + dff).astype(BF16)
        dbf_ref[...] += jnp.sum(dff, axis=0, keepdims=True)
        dw_ref[...] += _dot_tn(psv, dz, HIGHEST)
        db2_ref[...] += jnp.sum(dz, axis=0, keepdims=True)

    return pl.pallas_call(
        body, grid=(nb,),
        in_specs=[pl.BlockSpec((ts, SMALL_W), rev), pl.BlockSpec((ts, SMALL_W), rev), _vec_spec(SMALL_W),
                  pl.BlockSpec((SMALL_W, GLA_KW), lambda i: (0, 0)), _vec_spec(GLA_KW),
                  pl.BlockSpec((ts, GLA_KW), rev)],
        out_specs=[pl.BlockSpec((ts, SMALL_W), rev), _vec_spec(SMALL_W),
                   pl.BlockSpec((SMALL_W, GLA_KW), lambda i: (0, 0)), _vec_spec(GLA_KW)],
        out_shape=[jax.ShapeDtypeStruct((s, SMALL_W), BF16), jax.ShapeDtypeStruct((1, SMALL_W), F32),
                   jax.ShapeDtypeStruct((SMALL_W, GLA_KW), F32), jax.ShapeDtypeStruct((1, GLA_KW), F32)],
        scratch_shapes=[pltpu.VMEM((1, SMALL_W), F32)],
        compiler_params=_cp("arbitrary"), name=name,
    )(dck, ps, bf, w2p, b2, dla)


def _hs(h, hd=FOX_HD):
    return slice(h * hd, (h + 1) * hd)


def _fox_fwd(proj, cum_t, g_fox, *, name, side=None):
    s = proj.shape[0]
    tq, tk = min(FOX_TQ, s), min(FOX_TK, s)
    scale = FOX_HD ** -0.5
    n_si = len(side.inputs) if side else 0
    n_so = len(side.out_shapes) if side else 0
    grid = (s // tq, s // tk)

    def body(*refs):
        q_ref, k_ref, v_ref, ck_ref, g_ref = refs[:5]
        o_ref, n_ref, lse_ref = refs[5 + n_si:8 + n_si]
        m_sc, acc_sc = refs[8 + n_si + n_so:10 + n_si + n_so]
        i, j = pl.program_id(0), pl.program_id(1)
        if side:
            side.run(refs[5:5 + n_si], refs[8 + n_si:8 + n_si + n_so], refs[10 + n_si + n_so:],
                     (i == 0) & (j == 0), (i == grid[0] - 1) & (j == grid[1] - 1))

        @pl.when(j == 0)
        def _():
            m_sc[...] = jnp.full_like(m_sc, NEG)
            acc_sc[...] = jnp.zeros_like(acc_sc)

        def block(masked):
            mask = _causal_mask(i, j, tq, tk) if masked else None
            ones = jnp.ones((tk, FOX_HD), BF16)
            for h in range(FOX_HEADS):
                sc = _fox_logits(_dot_nt(q_ref[:, _hs(h)], k_ref[:, _hs(h)]), ck_ref[h:h + 1, :], mask, scale)
                m_prev = m_sc[h]
                m_new = jnp.maximum(m_prev, jnp.max(sc, axis=-1, keepdims=True))
                alpha = jnp.exp(m_prev - m_new)
                p = jnp.exp(sc - m_new).astype(BF16)
                v_one = jnp.concatenate([v_ref[:, _hs(h)], ones], axis=1)
                acc_sc[:, _hs(h, 2 * FOX_HD)] = alpha * acc_sc[:, _hs(h, 2 * FOX_HD)] + _dot_nn(p, v_one)
                m_sc[h] = m_new

        pl.when(j < i)(functools.partial(block, False))

        @pl.when(j == i)
        def _():
            block(True)
            lane = lax.broadcasted_iota(jnp.int32, (tq, 128), 1)
            lse = jnp.zeros((tq, 128), F32)
            for h in range(FOX_HEADS):
                l_rep = acc_sc[:, 2 * h * FOX_HD + FOX_HD:2 * (h + 1) * FOX_HD]
                o = acc_sc[:, 2 * h * FOX_HD:2 * h * FOX_HD + FOX_HD] / l_rep
                o_ref[:, _hs(h)] = o
                r = lax.rsqrt(jnp.mean(o * o, axis=-1, keepdims=True) + EPS)
                n_ref[:, _hs(h)] = (o * r * g_ref[h:h + 1, :]).astype(BF16)
                lse = jnp.where(lane == h, m_sc[h] + jnp.log(l_rep), lse)
            lse_ref[...] = lse

    kv = lambda col: (lambda i, j: (jnp.minimum(j, i), col))
    any_spec = pl.BlockSpec(memory_space=pl.ANY)
    return pl.pallas_call(
        body, grid=grid,
        in_specs=[pl.BlockSpec((tq, FOX_W), lambda i, j: (i, 0)),
                  pl.BlockSpec((tk, FOX_W), kv(1)),
                  pl.BlockSpec((tk, FOX_W), kv(2)),
                  pl.BlockSpec((FOX_HEADS, tk), lambda i, j: (0, jnp.minimum(j, i))),
                  pl.BlockSpec((FOX_HEADS, FOX_HD), lambda i, j: (0, 0))] + [any_spec] * n_si,
        out_specs=[pl.BlockSpec((tq, FOX_W), lambda i, j: (i, 0)),
                   pl.BlockSpec((tq, FOX_W), lambda i, j: (i, 0)),
                   pl.BlockSpec((tq, 128), lambda i, j: (i, 0))] + [any_spec] * n_so,
        out_shape=[jax.ShapeDtypeStruct((s, FOX_W), F32), jax.ShapeDtypeStruct((s, FOX_W), BF16),
                   jax.ShapeDtypeStruct((s, 128), F32)] + (side.out_shapes if side else []),
        scratch_shapes=[pltpu.VMEM((FOX_HEADS, tq, 1), F32), pltpu.VMEM((tq, 2 * FOX_W), F32)]
        + (side.scratch() if side else []),
        compiler_params=_cp("arbitrary", "arbitrary"), name=name,
    )(proj, proj, proj, cum_t, g_fox, *(side.inputs if side else []))


def _causal_mask(i, j, tq, tk):
    rows = i * tq + lax.broadcasted_iota(jnp.int32, (tq, tk), 0)
    cols = j * tk + lax.broadcasted_iota(jnp.int32, (tq, tk), 1)
    return rows >= cols


def _fox_logits(qk, ck, mask, scale):
    sc = qk * scale - ck
    return sc if mask is None else jnp.where(mask, sc, NEG)


def _fox_bwd(proj, do, cum_t, lse, delta, *, name, side=None):
    s = proj.shape[0]
    tq, tk = min(FOX_TQ, s), min(FOX_TK, s)
    nk, nq = s // tk, s // tq
    scale = FOX_HD ** -0.5
    n_si = len(side.inputs) if side else 0
    n_so = len(side.out_shapes) if side else 0

    def body(*refs):
        q_ref, k_ref, v_ref, do_ref, ck_ref, lse_ref, dl_ref = refs[:7]
        dq_hbm, dk_ref, dv_ref, dcq_hbm, dck_ref = refs[7 + n_si:12 + n_si]
        dq_sc, dcq_sc, dk_sc, dv_sc, dck_sc, out_sems = refs[12 + n_si + n_so:18 + n_si + n_so]
        j, i = pl.program_id(0), pl.program_id(1)
        if side:
            side.run(refs[7:7 + n_si], refs[12 + n_si:12 + n_si + n_so], refs[18 + n_si + n_so:],
                     (j == 0) & (i == 0), (j == nk - 1) & (i == nq - 1))

        @pl.when((j == 0) & (i == 0))
        def _():
            dq_sc[...] = jnp.zeros_like(dq_sc)
            dcq_sc[...] = jnp.zeros_like(dcq_sc)

        @pl.when(i == 0)
        def _():
            dk_sc[...] = jnp.zeros_like(dk_sc)
            dv_sc[...] = jnp.zeros_like(dv_sc)
            dck_sc[...] = jnp.zeros_like(dck_sc)

        def block(masked):
            mask = _causal_mask(i, j, tq, tk) if masked else None
            qrows = pl.ds(pl.multiple_of(i * tq, tq), tq)
            for h in range(FOX_HEADS):
                sc = _fox_logits(_dot_nt(q_ref[:, _hs(h)], k_ref[:, _hs(h)]), ck_ref[h:h + 1, :], mask, scale)
                p = jnp.exp(sc - lse_ref[:, h:h + 1])
                ds = p * (_dot_nt(do_ref[:, _hs(h)], v_ref[:, _hs(h)]) - dl_ref[:, h:h + 1])
                dsb = ds.astype(BF16)
                dv_sc[:, _hs(h)] += _dot_tn(p.astype(BF16), do_ref[:, _hs(h)])
                dk_sc[:, _hs(h)] += _dot_tn(dsb, q_ref[:, _hs(h)])
                dq_sc[qrows, _hs(h)] += _dot_nn(dsb, k_ref[:, _hs(h)]) * scale
                dck_sc[h:h + 1, :] -= jnp.sum(ds, axis=0, keepdims=True)
                dcq_sc[qrows, h:h + 1] += jnp.sum(ds, axis=-1, keepdims=True)

        pl.when(i > j)(functools.partial(block, False))
        pl.when(i == j)(functools.partial(block, True))

        @pl.when(i == nq - 1)
        def _():
            dk_ref[...] = (dk_sc[...] * scale).astype(BF16)
            dv_ref[...] = dv_sc[...].astype(BF16)
            dck_ref[...] = dck_sc[...]

        @pl.when((j == nk - 1) & (i == nq - 1))
        def _():
            out_q = pltpu.make_async_copy(dq_sc, dq_hbm, out_sems.at[0])
            out_c = pltpu.make_async_copy(dcq_sc, dcq_hbm, out_sems.at[1])
            out_q.start()
            out_c.start()
            out_q.wait()
            out_c.wait()

    qrow = lambda j, i: (jnp.maximum(i, j), 0)
    krow = lambda col: (lambda j, i: (j, col))
    any_spec = pl.BlockSpec(memory_space=pl.ANY)
    return pl.pallas_call(
        body, grid=(nk, nq),
        in_specs=[pl.BlockSpec((tq, FOX_W), qrow), pl.BlockSpec((tk, FOX_W), krow(1)),
                  pl.BlockSpec((tk, FOX_W), krow(2)),
                  pl.BlockSpec((tq, FOX_W), qrow),
                  pl.BlockSpec((FOX_HEADS, tk), lambda j, i: (0, j)),
                  pl.BlockSpec((tq, 128), qrow), pl.BlockSpec((tq, 128), qrow)] + [any_spec] * n_si,
        out_specs=[any_spec, pl.BlockSpec((tk, FOX_W), lambda j, i: (j, 0)),
                   pl.BlockSpec((tk, FOX_W), lambda j, i: (j, 0)), any_spec,
                   pl.BlockSpec((FOX_HEADS, tk), lambda j, i: (0, j))] + [any_spec] * n_so,
        out_shape=[jax.ShapeDtypeStruct((s, FOX_W), F32), jax.ShapeDtypeStruct((s, FOX_W), BF16),
                   jax.ShapeDtypeStruct((s, FOX_W), BF16), jax.ShapeDtypeStruct((s, 128), F32),
                   jax.ShapeDtypeStruct((FOX_HEADS, s), F32)] + (side.out_shapes if side else []),
        scratch_shapes=[pltpu.VMEM((s, FOX_W), F32), pltpu.VMEM((s, 128), F32),
                        pltpu.VMEM((tk, FOX_W), F32), pltpu.VMEM((tk, FOX_W), F32), pltpu.VMEM((FOX_HEADS, tk), F32),
                        pltpu.SemaphoreType.DMA((2,))] + (side.scratch() if side else []),
        compiler_params=_cp("arbitrary", "arbitrary"), name=name,
    )(proj, proj, proj, do, cum_t, lse, delta, *(side.inputs if side else []))


def _head_norm_bwd(dn_in, o, g, gr_src, *, nh, hd, dn_col, gr_col, name):
    s, w = o.shape
    ts = min(ROW_TILE, s)
    gated = gr_src is not None

    def body(*refs):
        if gated:
            dn_ref, o_ref, g_ref, gr_ref, do_ref, dgr_ref, dl_ref, dg_ref = refs
        else:
            dn_ref, o_ref, g_ref, do_ref, dl_ref, dg_ref = refs

        @pl.when(pl.program_id(0) == 0)
        def _():
            dg_ref[...] = jnp.zeros_like(dg_ref)

        lane = lax.broadcasted_iota(jnp.int32, (ts, 128), 1)
        delta = jnp.zeros((ts, 128), F32)
        for h in range(nh):
            sl = _hs(h, hd)
            ov = o_ref[:, sl]
            dnv = dn_ref[:, sl].astype(F32)
            gv = g_ref[h:h + 1, :]
            r = lax.rsqrt(jnp.mean(ov * ov, axis=-1, keepdims=True) + EPS)
            ohat = ov * r
            if gated:
                grv = gr_ref[:, sl].astype(F32)
                sig = _sigmoid(grv)
                dgr_ref[:, sl] = (dnv * (ohat * gv) * (sig * (1.0 + grv * (1.0 - sig)))).astype(BF16)
                dnv = dnv * (grv * sig)
            dg_ref[h:h + 1, :] += jnp.sum(dnv * ohat, axis=0, keepdims=True)
            dohat = dnv * gv
            do = r * (dohat - ohat * jnp.mean(dohat * ohat, axis=-1, keepdims=True))
            do_ref[:, sl] = do.astype(BF16)
            delta = jnp.where(lane == h, jnp.sum(do.astype(BF16).astype(F32) * ov, axis=-1, keepdims=True), delta)
        dl_ref[...] = delta

    in_specs = [pl.BlockSpec((ts, w), lambda i: (i, dn_col)), _row_spec(ts, w),
                pl.BlockSpec((nh, hd), lambda i: (0, 0))]
    args = [dn_in, o, g]
    out_specs = [_row_spec(ts, w)]
    out_shape = [jax.ShapeDtypeStruct((s, w), BF16)]
    if gated:
        in_specs.append(pl.BlockSpec((ts, w), lambda i: (i, gr_col)))
        args.append(gr_src)
        out_specs.append(_row_spec(ts, w))
        out_shape.append(jax.ShapeDtypeStruct((s, w), BF16))
    out_specs += [_row_spec(ts, 128), pl.BlockSpec((nh, hd), lambda i: (0, 0))]
    out_shape += [jax.ShapeDtypeStruct((s, 128), F32), jax.ShapeDtypeStruct((nh, hd), F32)]
    return pl.pallas_call(
        body, grid=(s // ts,), in_specs=in_specs, out_specs=out_specs, out_shape=out_shape,
        compiler_params=_cp("arbitrary"), name=name,
    )(*args)


GQ_BLK = 3 * FOX_W // GLA_DK
GK_BLK = GQ_BLK + GLA_HEADS
GV_BLK = (3 * FOX_W + 2 * GLA_KW) // GLA_DV
GR_BLK = GV_BLK + GLA_HEADS


def _gla_chunk_terms(la):
    cum = _dot_nn(_tri(CHUNK), la, HIGHEST)
    total = cum[CHUNK - 1:CHUNK, :]
    return jnp.exp(total - cum), jnp.exp(total)


def _gla_fwd(proj, log_a, g_gla, *, name):
    s = proj.shape[0]
    rows = min(GLA_ROWS, s)
    cb = rows // CHUNK
    nblk = s // rows
    scale = GLA_DK ** -0.5

    def body(q_ref, k_ref, v_ref, gr_ref, la_ref, g_ref, o_ref, n_ref, st_ref, st_sc):
        h = pl.program_id(0)

        @pl.when(pl.program_id(1) == 0)
        def _():
            st_sc[...] = jnp.zeros_like(st_sc)

        gv = g_ref[pl.ds(h, 1), :]
        for ci in range(cb):
            sl = slice(ci * CHUNK, (ci + 1) * CHUNK)
            e, dec = _gla_chunk_terms(la_ref[sl, :])
            k_dec = (k_ref[sl, :].astype(F32) * e).astype(BF16)
            st = st_sc[...] * dec + _dot_tn(v_ref[sl, :], k_dec)
            st_sc[...] = st
            st_ref[0, ci] = st
            qs = (q_ref[sl, :].astype(F32) * scale).astype(BF16)
            o = _dot_nt(qs, st.astype(BF16))
            o_ref[sl, :] = o
            r = lax.rsqrt(jnp.mean(o * o, axis=-1, keepdims=True) + EPS)
            grv = gr_ref[sl, :].astype(F32)
            n_ref[sl, :] = (o * r * gv * (grv * _sigmoid(grv))).astype(BF16)

    return pl.pallas_call(
        body, grid=(GLA_HEADS, nblk),
        in_specs=[pl.BlockSpec((rows, GLA_DK), lambda h, n: (n, GQ_BLK + h)),
                  pl.BlockSpec((rows, GLA_DK), lambda h, n: (n, GK_BLK + h)),
                  pl.BlockSpec((rows, GLA_DV), lambda h, n: (n, GV_BLK + h)),
                  pl.BlockSpec((rows, GLA_DV), lambda h, n: (n, GR_BLK + h)),
                  pl.BlockSpec((rows, GLA_DK), lambda h, n: (n, h)),
                  pl.BlockSpec((GLA_HEADS, GLA_DV), lambda h, n: (0, 0))],
        out_specs=[pl.BlockSpec((rows, GLA_DV), lambda h, n: (n, h)),
                   pl.BlockSpec((rows, GLA_DV), lambda h, n: (n, h)),
                   pl.BlockSpec((1, cb, GLA_DV, GLA_DK), lambda h, n: (h, n, 0, 0))],
        out_shape=[jax.ShapeDtypeStruct((s, GLA_W), F32), jax.ShapeDtypeStruct((s, GLA_W), BF16),
                   jax.ShapeDtypeStruct((GLA_HEADS, s // CHUNK, GLA_DV, GLA_DK), F32)],
        scratch_shapes=[pltpu.VMEM((GLA_DV, GLA_DK), F32)],
        compiler_params=_cp("parallel", "arbitrary"), name=name,
    )(proj, proj, proj, proj, log_a, g_gla)


def _gla_bwd(proj, log_a, do, states, *, name, side=None):
    s = proj.shape[0]
    rows = min(GLA_ROWS, s)
    cb = rows // CHUNK
    nblk = s // rows
    scale = GLA_DK ** -0.5
    n_si = len(side.inputs) if side else 0
    n_so = len(side.out_shapes) if side else 0

    def body(*refs):
        q_ref, k_ref, v_ref, la_ref, do_ref, st_ref, prev_ref = refs[:7]
        dq_ref, dk_ref, dv_ref, dla_ref = refs[7 + n_si:11 + n_si]
        g_sc = refs[11 + n_si + n_so]
        nrev = pl.program_id(1)
        blk = nblk - 1 - nrev
        if side:
            hh = pl.program_id(0)
            side.run(refs[7:7 + n_si], refs[11 + n_si:11 + n_si + n_so], refs[12 + n_si + n_so:],
                     (hh == 0) & (nrev == 0), (hh == GLA_HEADS - 1) & (nrev == nblk - 1))

        @pl.when(nrev == 0)
        def _():
            g_sc[...] = jnp.zeros_like(g_sc)

        for ci in reversed(range(cb)):
            sl = slice(ci * CHUNK, (ci + 1) * CHUNK)
            e, dec = _gla_chunk_terms(la_ref[sl, :])
            kd = k_ref[sl, :].astype(F32) * e
            qs = (q_ref[sl, :].astype(F32) * scale).astype(BF16)
            dov = do_ref[sl, :]
            st = st_ref[0, ci]
            if ci > 0:
                st_prev = st_ref[0, ci - 1]
            else:
                st_prev = prev_ref[0, 0] * (blk > 0).astype(F32)
            dq_ref[sl, :] = (_dot_nn(dov, st.astype(BF16)) * scale).astype(BF16)
            gt = g_sc[...] + _dot_tn(dov, qs)
            gtb = gt.astype(BF16)
            dkd = _dot_nn(v_ref[sl, :], gtb)
            dv_ref[sl, :] = _dot_nt(kd.astype(BF16), gtb).astype(BF16)
            dk_ref[sl, :] = (dkd * e).astype(BF16)
            ddec = jnp.sum(gt * st_prev, axis=0, keepdims=True) * dec
            dla_ref[sl, :] = _dot_nn(_tri(CHUNK, strict=True), dkd * kd, HIGHEST) + ddec
            g_sc[...] = gt * dec

    rev = lambda col0: (lambda h, n: (nblk - 1 - n, col0 + h))
    return pl.pallas_call(
        body, grid=(GLA_HEADS, nblk),
        in_specs=[pl.BlockSpec((rows, GLA_DK), rev(GQ_BLK)),
                  pl.BlockSpec((rows, GLA_DK), rev(GK_BLK)),
                  pl.BlockSpec((rows, GLA_DV), rev(GV_BLK)),
                  pl.BlockSpec((rows, GLA_DK), rev(0)),
                  pl.BlockSpec((rows, GLA_DV), rev(0)),
                  pl.BlockSpec((1, cb, GLA_DV, GLA_DK), lambda h, n: (h, nblk - 1 - n, 0, 0)),
                  pl.BlockSpec((1, 1, GLA_DV, GLA_DK),
                               lambda h, n: (h, jnp.maximum((nblk - 1 - n) * cb - 1, 0), 0, 0))]
        + [pl.BlockSpec(memory_space=pl.ANY)] * n_si,
        out_specs=[pl.BlockSpec((rows, GLA_DK), rev(0)), pl.BlockSpec((rows, GLA_DK), rev(0)),
                   pl.BlockSpec((rows, GLA_DV), rev(0)), pl.BlockSpec((rows, GLA_DK), rev(0))]
        + [pl.BlockSpec(memory_space=pl.ANY)] * n_so,
        out_shape=[jax.ShapeDtypeStruct((s, GLA_KW), BF16), jax.ShapeDtypeStruct((s, GLA_KW), BF16),
                   jax.ShapeDtypeStruct((s, GLA_W), BF16), jax.ShapeDtypeStruct((s, GLA_KW), F32)]
        + (side.out_shapes if side else []),
        scratch_shapes=[pltpu.VMEM((GLA_DV, GLA_DK), F32)] + (side.scratch() if side else []),
        compiler_params=_cp("arbitrary", "arbitrary"), name=name,
    )(proj, proj, proj, log_a, do, states, states, *(side.inputs if side else []))


def _row_tile(r):
    tr = min(ROW_TILE, r)
    while r % tr or tr % 8:
        tr -= 1
    return tr


def _adamw_math(w, g, m, v):
    m = ADAM_B1 * m + (1.0 - ADAM_B1) * g
    v = ADAM_B2 * v + (1.0 - ADAM_B2) * (g * g)
    m_hat = m / (1.0 - ADAM_B1 ** ADAM_STEP)
    v_hat = v / (1.0 - ADAM_B2 ** ADAM_STEP)
    delta = -ADAM_LR * (m_hat / (jnp.sqrt(v_hat) + ADAM_EPS) + ADAM_WD * w)
    return delta, m, v


COL_TILE = 256


def _tile_2d(r, c):
    if r % 8 == 0:
        return _row_tile(r), c
    assert c % COL_TILE == 0, (r, c)
    return r, COL_TILE


def _half_shape(shape):
    r, c = shape[-2:]
    return tuple(shape[:-2]) + ((r // 2, c) if _half_axis(r) == 0 else (r, c // 2))


def _adam(g, w, m, v, *, name):
    r, c = w.shape
    tr, tc = _tile_2d(r, c)

    def body(g_ref, w_ref, m_ref, v_ref, d_ref, mo_ref, vo_ref):
        d, mn, vn = _adamw_math(w_ref[...], g_ref[...], m_ref[...], v_ref[...])
        d_ref[...] = d
        mo_ref[...] = mn
        vo_ref[...] = vn

    spec = pl.BlockSpec((tr, tc), lambda i, j: (i, j))
    return pl.pallas_call(
        body, grid=(r // tr, c // tc), in_specs=[spec] * 4, out_specs=[spec] * 3,
        out_shape=[jax.ShapeDtypeStruct((r, c), F32)] * 3,
        compiler_params=_cp("parallel", "parallel"), name=name,
    )(g, w, m, v)


def _ada_grad_adam(c_all_t, dmod_cols, w, m, v, *, name):
    r, c = w.shape
    tr, tc = min(512, r), min(1024, c)

    def body(ct_ref, dm_ref, w_ref, m_ref, v_ref, g_ref, d_ref, mo_ref, vo_ref):
        g = _dot_nn(ct_ref[...], dm_ref[...], HIGHEST)
        g_ref[...] = g
        d, mn, vn = _adamw_math(w_ref[...], g, m_ref[...], v_ref[...])
        d_ref[...] = d
        mo_ref[...] = mn
        vo_ref[...] = vn

    spec = pl.BlockSpec((tr, tc), lambda i, j: (i, j))
    nb = c_all_t.shape[1]
    return pl.pallas_call(
        body, grid=(r // tr, c // tc),
        in_specs=[pl.BlockSpec((tr, nb), lambda i, j: (i, 0)), pl.BlockSpec((nb, tc), lambda i, j: (0, j)),
                  spec, spec, spec],
        out_specs=[spec] * 4, out_shape=[jax.ShapeDtypeStruct((r, c), F32)] * 4,
        compiler_params=_cp("parallel", "parallel"), name=name,
    )(c_all_t, dmod_cols, w, m, v)


def _mod_shard(c_all, w, b, *, name):
    k, c = w.shape
    tc = min(512, c)
    nb = c_all.shape[0]

    def body(c_ref, w_ref, b_ref, o_ref):
        o_ref[...] = _dot_nn(c_ref[...], w_ref[...], HIGHEST) + b_ref[...]

    return pl.pallas_call(
        body, grid=(c // tc,),
        in_specs=[pl.BlockSpec((nb, k), lambda j: (0, 0)), pl.BlockSpec((k, tc), lambda j: (0, j)),
                  pl.BlockSpec((1, tc), lambda j: (0, j))],
        out_specs=pl.BlockSpec((nb, tc), lambda j: (0, j)),
        out_shape=jax.ShapeDtypeStruct((nb, c), F32),
        compiler_params=_cp("parallel"), name=name,
    )(c_all, w, b)


def _silu_rows(c, *, name):
    def body(c_ref, o_ref):
        cv = c_ref[...]
        o_ref[...] = cv * _sigmoid(cv)

    return pl.pallas_call(body, out_shape=jax.ShapeDtypeStruct(c.shape, F32), name=name)(c)


def _pair_sum(g, got, idx, *, name):
    p, r, c = g.shape
    ax = _half_axis(r)
    hr, hc = _half_shape((r, c))
    tr, tc = _tile_2d(hr, hc)
    nbr, nbc = hr // tr, hc // tc

    def body(idx_ref, a_ref, b_ref, o_ref):
        o_ref[...] = (a_ref[...].astype(F32) + b_ref[...].astype(F32)).astype(BF16)

    def own_map(i, j, k, idx_ref):
        return (i, j + (idx_ref[0] * nbr if ax == 0 else 0), k + (idx_ref[0] * nbc if ax == 1 else 0))

    half_spec = pl.BlockSpec((1, tr, tc), lambda i, j, k, idx_ref: (i, j, k))
    return pl.pallas_call(
        body,
        grid_spec=pltpu.PrefetchScalarGridSpec(
            num_scalar_prefetch=1, grid=(p, nbr, nbc),
            in_specs=[pl.BlockSpec((1, tr, tc), own_map), half_spec],
            out_specs=half_spec),
        out_shape=jax.ShapeDtypeStruct((p, hr, hc), BF16),
        compiler_params=_cp("parallel", "parallel", "parallel"), name=name,
    )(idx, g, got)


def _final_sum(own, parts, idx, shard_shape, *, name):
    ax = _half_axis(shard_shape[0])
    hr, hc = own.shape[1:]
    tr, tc = _tile_2d(hr, hc)
    nbr, nbc = hr // tr, hc // tc

    def body(idx_ref, own_ref, parts_ref, o_ref):
        acc = own_ref[0].astype(F32)
        for q in range(3):
            acc = acc + parts_ref[q].astype(F32)
        o_ref[...] = acc

    def out_map(j, k, idx_ref):
        return (j + (idx_ref[0] * nbr if ax == 0 else 0), k + (idx_ref[0] * nbc if ax == 1 else 0))

    return pl.pallas_call(
        body,
        grid_spec=pltpu.PrefetchScalarGridSpec(
            num_scalar_prefetch=1, grid=(nbr, nbc),
            in_specs=[pl.BlockSpec((1, tr, tc), lambda j, k, idx_ref: (idx_ref[1], j, k)),
                      pl.BlockSpec((3, tr, tc), lambda j, k, idx_ref: (0, j, k))],
            out_specs=pl.BlockSpec((tr, tc), out_map)),
        out_shape=jax.ShapeDtypeStruct(tuple(shard_shape), F32),
        compiler_params=_cp("parallel", "parallel"), name=name,
    )(idx, own, parts)


def _stack_sum(x, *, name):
    p, r, c = x.shape
    tr = _row_tile(r)

    def body(x_ref, o_ref):
        acc = x_ref[0].astype(F32)
        for q in range(1, p):
            acc = acc + x_ref[q].astype(F32)
        o_ref[...] = acc

    return pl.pallas_call(
        body, grid=(r // tr,),
        in_specs=[pl.BlockSpec((p, tr, c), lambda i: (0, i, 0))],
        out_specs=pl.BlockSpec((tr, c), lambda i: (i, 0)),
        out_shape=jax.ShapeDtypeStruct((r, c), F32),
        compiler_params=_cp("parallel"), name=name,
    )(x)


def _place():
    x, y, c = lax.axis_index("x"), lax.axis_index("y"), lax.axis_index("c")
    chips = [(1 - x, y), (x, 1 - y), (1 - x, 1 - y)]
    return x, y, c, chips


def _gather8(x_shard, *, name):
    m_per, n = x_shard.shape

    def body(x_ref, out_ref, send_sems, recv_sems, local_sem):
        x, y, c, chips = _place()
        me, sibling = (x, y, c), (x, y, 1 - c)

        def rows(px, py, pc):
            return out_ref.at[pl.ds((4 * px + 2 * py + pc) * m_per, m_per), :]

        def copy(k, block, to, src=None):
            return pltpu.make_async_remote_copy(
                src_ref=rows(*block) if src is None else src, dst_ref=rows(*block),
                send_sem=send_sems.at[k], recv_sem=recv_sems.at[k], device_id=to, device_id_type=MESH)

        mine = pltpu.make_async_copy(x_ref, rows(*me), local_sem)
        mine.start()
        first = [copy(0, me, sibling, src=x_ref)]
        first += [copy(1 + j, me, (*chip, c), src=x_ref) for j, chip in enumerate(chips)]
        for cp in first:
            cp.start()
        passed = [copy(4 + j, (*chip, c), sibling) for j, chip in enumerate(chips)]
        for j, chip in enumerate(chips):
            copy(1 + j, (*chip, c), me).wait_recv()
            passed[j].start()
        copy(0, sibling, me).wait_recv()
        for j, chip in enumerate(chips):
            copy(4 + j, (*chip, 1 - c), me).wait_recv()
        for cp in first + passed:
            cp.wait_send()
        mine.wait()

    return pl.pallas_call(
        body,
        out_shape=jax.ShapeDtypeStruct((8 * m_per, n), x_shard.dtype),
        in_specs=[pl.BlockSpec(memory_space=pltpu.VMEM)],
        out_specs=pl.BlockSpec(memory_space=pltpu.VMEM),
        scratch_shapes=[pltpu.SemaphoreType.DMA((7,)), pltpu.SemaphoreType.DMA((7,)), pltpu.SemaphoreType.DMA],
        name=name,
    )(x_shard)


N_GATHER_COPIES = 3


def _gather_start(shard, *, name):
    hbm, sem = pl.BlockSpec(memory_space=pltpu.HBM), pl.BlockSpec(memory_space=pltpu.SEMAPHORE)
    n = N_GATHER_COPIES

    def body(shard_ref, land_ref, *outs):
        send_sems, recv_sems, token = outs[:n], outs[n:2 * n], outs[2 * n + 2]
        for k, (src, dst, _, peer) in enumerate(_plan_gather_ici(shard_ref, land_ref)):
            pltpu.make_async_remote_copy(src_ref=src, dst_ref=dst, send_sem=send_sems[k], recv_sem=recv_sems[k],
                                         device_id=peer, device_id_type=MESH).start()
        token[...] = jnp.zeros_like(token)

    land = lax.empty((4,) + shard.shape, shard.dtype)
    res = pl.pallas_call(
        body, name=name,
        out_shape=[pltpu.SemaphoreType.DMA(())] * (2 * n)
        + [pltpu.HBM(shard.shape, shard.dtype), pltpu.HBM(land.shape, land.dtype),
           jax.ShapeDtypeStruct((8, 128), F32)],
        in_specs=[hbm, hbm], out_specs=[sem] * (2 * n) + [hbm, hbm, pl.BlockSpec(memory_space=pltpu.VMEM)],
        input_output_aliases={0: 2 * n, 1: 2 * n + 1},
        compiler_params=pltpu.CompilerParams(has_side_effects=pltpu.SideEffectType.DATAFLOW_SIDE_EFFECTING),
    )(pltpu.with_memory_space_constraint(shard, pltpu.HBM), pltpu.with_memory_space_constraint(land, pltpu.HBM))
    return res[:n], res[n:2 * n], res[2 * n], res[2 * n + 1], res[2 * n + 2]


def _gather_wait(send_sems, recv_sems, shard_thru, land_thru, after, *, name):
    hbm, sem = pl.BlockSpec(memory_space=pltpu.HBM), pl.BlockSpec(memory_space=pltpu.SEMAPHORE)
    n = N_GATHER_COPIES

    def body(shard_ref, land_ref, *rest):
        send, recv = rest[:n], rest[n:2 * n]
        for k, (src, dst, landing, peer) in enumerate(_plan_gather_ici(shard_ref, land_ref)):
            pltpu.make_async_remote_copy(src_ref=src, dst_ref=dst, send_sem=send[k], recv_sem=recv[k],
                                         device_id=peer, device_id_type=MESH).wait_send()
            pltpu.make_async_remote_copy(src_ref=src, dst_ref=landing, send_sem=send[k], recv_sem=recv[k],
                                         device_id=peer, device_id_type=MESH).wait_recv()

    return pl.pallas_call(
        body, name=name,
        out_shape=[pltpu.HBM(shard_thru.shape, shard_thru.dtype), pltpu.HBM(land_thru.shape, land_thru.dtype)],
        in_specs=[hbm, hbm] + [sem] * (2 * n) + [pl.BlockSpec(memory_space=pl.ANY)], out_specs=[hbm, hbm],
        input_output_aliases={0: 0, 1: 1},
        compiler_params=pltpu.CompilerParams(has_side_effects=pltpu.SideEffectType.DATAFLOW_SIDE_EFFECTING),
    )(shard_thru, land_thru, *send_sems, *recv_sems, after)[1]


def _plan_start(plan, send_sems, recv_sems):
    for k, (src, dst, _, peer) in enumerate(plan):
        pltpu.make_async_remote_copy(src_ref=src, dst_ref=dst, send_sem=send_sems.at[k], recv_sem=recv_sems.at[k],
                                     device_id=peer, device_id_type=MESH).start()


def _plan_wait(plan, send_sems, recv_sems):
    for k, (src, _, land, peer) in enumerate(plan):
        pltpu.make_async_remote_copy(src_ref=src, dst_ref=land, send_sem=send_sems.at[k], recv_sem=recv_sems.at[k],
                                     device_id=peer, device_id_type=MESH).wait_recv()
    for k, (src, dst, _, peer) in enumerate(plan):
        pltpu.make_async_remote_copy(src_ref=src, dst_ref=dst, send_sem=send_sems.at[k], recv_sem=recv_sems.at[k],
                                     device_id=peer, device_id_type=MESH).wait_send()


def _half_axis(rows):
    return 0 if rows % 32 == 0 else 1


def _rows_half(ref, hc, axis, part=None):
    size = ref.shape[axis] // 2
    start = hc * size
    if part is not None:
        size //= part[1]
        start = start + part[0] * size
    idx = [slice(None)] * len(ref.shape)
    idx[axis] = pl.ds(start, size)
    return ref.at[tuple(idx)]


def _plan_gather_ici(shard, full, part=None):
    x, y, c, chips = _place()
    ax = _half_axis(shard.shape[0])
    src = _rows_half(shard, c, ax, part)
    return [(src, _rows_half(full.at[2 * x + y], c, ax, part), _rows_half(full.at[2 * cx + cy], c, ax, part),
             (cx, cy, c)) for cx, cy in chips]


def _plan_gather_d2d(full):
    x, y, c, chips = _place()
    ax = _half_axis(full.shape[1])
    plan = []
    for cx, cy in chips:
        slot = full.at[2 * cx + cy]
        plan.append((_rows_half(slot, c, ax), _rows_half(slot, c, ax), _rows_half(slot, 1 - c, ax), (x, y, 1 - c)))
    return plan


def _plan_pair(grad, got):
    x, y, c, _ = _place()
    return [(_rows_half(grad, 1 - c, 1 + _half_axis(grad.shape[1])), got, got, (x, y, 1 - c))]


def _plan_shard_ici(sums, parts, piece=None):
    _, _, c, chips = _place()

    def rows(ref):
        if piece is None:
            return ref
        size = ref.shape[0] // piece[1]
        return ref.at[pl.ds(piece[0] * size, size), :]

    return [(rows(sums.at[2 * cx + cy]), rows(parts.at[k]), rows(parts.at[k]), (cx, cy, c))
            for k, (cx, cy) in enumerate(chips)]


def _plan_half(buf):
    x, y, c, _ = _place()
    ax = _half_axis(buf.shape[0])
    mine = _rows_half(buf, c, ax)
    return [(mine, mine, _rows_half(buf, 1 - c, ax), (x, y, 1 - c))]


def _comm_call(plan_fn, inputs, out_shapes, *, name, aliases=None):
    ni, no = len(inputs), len(out_shapes)

    def body(*refs):
        plan = plan_fn(refs[:ni], refs[ni:ni + no])
        send_sems, recv_sems = refs[ni + no:]
        _plan_start(plan, send_sems, recv_sems)
        _plan_wait(plan, send_sems, recv_sems)

    any_spec = pl.BlockSpec(memory_space=pl.ANY)
    n_copies = 3 * max(ni, no)
    return pl.pallas_call(
        body, out_shape=list(out_shapes), in_specs=[any_spec] * ni, out_specs=[any_spec] * no,
        scratch_shapes=[pltpu.SemaphoreType.DMA((n_copies,)), pltpu.SemaphoreType.DMA((n_copies,))],
        input_output_aliases=aliases or {}, name=name,
    )(*inputs)


def _gather_forward(fulls, *, name):
    return _comm_call(lambda ins, outs: [cp for o in outs for cp in _plan_gather_d2d(o)],
                      fulls, [jax.ShapeDtypeStruct(f.shape, f.dtype) for f in fulls], name=name,
                      aliases={k: k for k in range(len(fulls))})


def _pair_exchange(grads, *, name):
    return _comm_call(lambda ins, outs: [cp for i, o in zip(ins, outs) for cp in _plan_pair(i, o)],
                      grads, [jax.ShapeDtypeStruct(_half_shape(g.shape), g.dtype) for g in grads], name=name)


def _half_exchange(bufs, *, name):
    return _comm_call(lambda ins, outs: [cp for o in outs for cp in _plan_half(o)],
                      bufs, [jax.ShapeDtypeStruct(b.shape, b.dtype) for b in bufs], name=name,
                      aliases={k: k for k in range(len(bufs))})


def _split_w_in(w_in_t):
    d = w_in_t.shape[1]
    main = jnp.concatenate([w_in_t[0:3072], w_in_t[3080:5128], w_in_t[5144:6168]], axis=0)
    small = jnp.concatenate([w_in_t[3072:3080], w_in_t[5128:5144], jnp.zeros((SMALL_W - 24, d), w_in_t.dtype)], axis=0)
    return main, small


def _merge_dw_in(dw_main, dw_small):
    return jnp.concatenate([dw_main[0:3072], dw_small[0:8], dw_main[3072:5120], dw_small[8:24], dw_main[5120:6144]],
                           axis=0)


def _gather_side(shards):
    return _Side(shards, [jax.ShapeDtypeStruct((4,) + w.shape, w.dtype) for w in shards],
                 lambda ins, outs: [cp for i, o in zip(ins, outs) for cp in _plan_gather_ici(i, o)], 3 * len(shards))


def _finish_gather(fulls, owns, chip, *, name):
    fulls = _gather_forward(list(fulls), name=name)
    return [lax.dynamic_update_index_in_dim(f, o, chip, 0) for f, o in zip(fulls, owns)]


def _parts_shape(sums):
    return jax.ShapeDtypeStruct((3,) + sums.shape[1:], sums.dtype)


def _shard_side(sums):
    return _Side([sums], [_parts_shape(sums)], lambda ins, outs: _plan_shard_ici(ins[0], outs[0]), 3)


def _got_shape(grad):
    return jax.ShapeDtypeStruct(_half_shape(grad.shape), grad.dtype)


def _pair_side(grad):
    return _Side([grad], [_got_shape(grad)], lambda ins, outs: _plan_pair(ins[0], outs[0]), 1)


def _chip_sum(grad, idx, tag):
    got, = _pair_exchange([grad], name=f"grad_pair_exchange_{tag}")
    return _pair_sum(grad, got, idx, name=f"grad_pair_sum_{tag}")


def _local_step(x, target, mod, g_pre_mix, g_post_mix, g_pre_mlp, g_post_mlp, w_in_pending, b_fgate, w_gla_a2,
                b_gla_a2, g_fox, g_gla, own_w_in, own_w_out, own_w_mlp_in, own_w_mlp_out, chip, idx):
    s, d = x.shape
    shift_m, scale_m, gate_m, shift_f, scale_f, gate_f = [mod[:, i * d:(i + 1) * d] for i in range(6)]
    a1 = g_pre_mix * (1.0 + scale_m)
    a2 = g_pre_mlp * (1.0 + scale_f)
    bf = jnp.concatenate([b_fgate, jnp.zeros((1, SMALL_W - FOX_HEADS), F32)], axis=1)
    w2p = jnp.zeros((SMALL_W, GLA_KW), F32).at[FOX_HEADS:FOX_HEADS + GLA_RANK].set(w_gla_a2)

    h1 = _pre_fwd(x, a1, shift_m, name="pre_mix_fwd")
    gw_in = _gather_wait(*w_in_pending[:4], h1, name="gather_w_in_wait")
    gw_in, = _finish_gather([gw_in], [own_w_in], chip, name="gather_w_in_d2d")
    w_in_t = gw_in.reshape(-1, d)
    w_main, w_small = _split_w_in(w_in_t)
    full_shape = lambda w: jax.ShapeDtypeStruct((4,) + w.shape, w.dtype)
    first_side = _Side(
        [own_w_out, own_w_mlp_out], [full_shape(own_w_out), full_shape(own_w_mlp_out)],
        lambda ins, outs: _plan_gather_ici(ins[0], outs[0]) + _plan_gather_ici(ins[1], outs[1], part=(0, 4)), 6)
    proj, gw_out, gw_mlp_out = _mm(h1, w_main, mode="nt", out_dtypes=[BF16], name="in_proj_main", side=first_side)
    ps, = _mm(h1, w_small, mode="nt", out_dtypes=[F32], name="in_proj_small")
    gw_out, = _finish_gather([gw_out], [own_w_out], chip, name="gather_w_out_d2d")
    w_out_full = gw_out.reshape(-1, d)
    cum, log_a = _gates_fwd(ps, bf, w2p, b_gla_a2, name="gates_fwd")
    cum_t = cum[:, :FOX_HEADS].T
    o_fox, fox_n, lse, gw_mlp_in = _fox_fwd(proj, cum_t, g_fox, name="fox_fwd", side=_gather_side([own_w_mlp_in]))
    gw_mlp_in, = _finish_gather([gw_mlp_in], [own_w_mlp_in], chip, name="gather_w_mlp_in_d2d")
    o_gla, gla_n, states = _gla_fwd(proj, log_a, g_gla, name="gla_fwd")
    mixed = jnp.concatenate([fox_n, gla_n], axis=1)
    y1, = _mm(mixed, w_out_full, mode="nn", out_dtypes=[F32], name="out_proj")
    x1, h2 = _post_pre_fwd(x, y1, gate_m, g_post_mix, a2, shift_f, name="post_mix_pre_mlp_fwd")

    def mlp_act(acc):
        r = jnp.maximum(acc, 0.0)
        return acc, r * r

    rest_side = _Side([own_w_mlp_out, gw_mlp_out], [full_shape(own_w_mlp_out)],
                      lambda ins, outs: [cp for q in (1, 2, 3) for cp in _plan_gather_ici(ins[0], outs[0], part=(q, 4))],
                      9, aliases={1: 0})
    u, act, gw_mlp_out = _mm(h2, gw_mlp_in, mode="nn", out_dtypes=[BF16, BF16], epi=mlp_act, name="mlp_in",
                             b_slots=4, tm=MM_TM, side=rest_side)
    gw_mlp_out, = _finish_gather([gw_mlp_out], [own_w_mlp_out], chip, name="gather_w_mlp_out_d2d")
    w_mlp_out_full = gw_mlp_out.reshape(-1, d)
    y2, = _mm(act, w_mlp_out_full, mode="nn", out_dtypes=[F32], name="mlp_out")
    dx2, dy2, loss_part, dgate_f, dg_post_mlp = _post_loss_bwd(x1, y2, gate_f, g_post_mlp, target,
                                                               name="post_mlp_loss_bwd")
    dw_mlp_out, = _mm(act, dy2, mode="tn", out_dtypes=[BF16], name="dw_mlp_out")
    dw_mlp_out = dw_mlp_out.reshape(4, D_FF // 4, d)

    def act_bwd(acc, uv):
        return (acc * (2.0 * jnp.maximum(uv.astype(F32), 0.0)),)

    du, got_mlp_out = _mm(dy2, w_mlp_out_full, mode="nt", out_dtypes=[BF16], extras=[u], epi=act_bwd,
                          name="d_mlp_hidden", tm=MM_TM, side=_pair_side(dw_mlp_out))
    sum_mlp_out = _pair_sum(dw_mlp_out, got_mlp_out, idx, name="grad_pair_sum_mlp_out")
    nj = D_FF // 4 // min(MM_T, D_FF // 4)
    tmw = min(MM_T, d)
    dw_mlp_in, parts_mlp_out = _mm(
        h2, du, mode="tn", out_dtypes=[BF16], name="dw_mlp_in",
        out_shapes=[jax.ShapeDtypeStruct((4, d, D_FF // 4), BF16)],
        out_specs=[pl.BlockSpec((1, tmw, min(MM_T, D_FF // 4)), lambda i, j, kk: (j // nj, i, j % nj))],
        side=_Side([sum_mlp_out], [_parts_shape(sum_mlp_out)],
                   lambda ins, outs: _plan_shard_ici(ins[0], outs[0], piece=(0, 2)), 3))
    dh2, got_mlp_in, parts_mlp_out = _mm(
        du, gw_mlp_in, mode="nt", out_dtypes=[F32], name="d_mlp_in", b_slots=4,
        side=_Side([dw_mlp_in, sum_mlp_out, parts_mlp_out], [_got_shape(dw_mlp_in), _parts_shape(sum_mlp_out)],
                   lambda ins, outs: _plan_pair(ins[0], outs[0]) + _plan_shard_ici(ins[1], outs[1], piece=(1, 2)),
                   4, aliases={2: 1}))
    sum_mlp_in = _pair_sum(dw_mlp_in, got_mlp_in, idx, name="grad_pair_sum_mlp_in")
    dx1, dshift_f, da2, dy1, dgate_m, dg_post_mix = _pre_post_bwd(dh2, x1, dx2, a2, y1, gate_m, g_post_mix,
                                                                  name="pre_mlp_post_mix_bwd")
    dw_out, = _mm(mixed, dy1, mode="tn", out_dtypes=[BF16], name="dw_out")
    dw_out = dw_out.reshape(4, d // 4, d)
    dmixed, got_out = _mm(dy1, w_out_full, mode="nt", out_dtypes=[BF16], name="d_mixed", side=_pair_side(dw_out))
    sum_out = _pair_sum(dw_out, got_out, idx, name="grad_pair_sum_out")
    do_fox, delta, dg_fox = _head_norm_bwd(dmixed, o_fox, g_fox, None, nh=FOX_HEADS, hd=FOX_HD, dn_col=0,
                                           gr_col=0, name="fox_norm_bwd")
    do_gla, dgr, _, dg_gla = _head_norm_bwd(dmixed, o_gla, g_gla, proj, nh=GLA_HEADS, hd=GLA_DV, dn_col=1,
                                            gr_col=(3 * FOX_W + 2 * GLA_KW + GLA_W) // GLA_W, name="gla_norm_bwd")
    dq_fox, dk_fox, dv_fox, dcq, dck_t, parts_mlp_in, parts_out = _fox_bwd(
        proj, do_fox, cum_t, lse, delta, name="fox_bwd",
        side=_Side([sum_mlp_in, sum_out], [_parts_shape(sum_mlp_in), _parts_shape(sum_out)],
                   lambda ins, outs: _plan_shard_ici(ins[0], outs[0]) + _plan_shard_ici(ins[1], outs[1]), 6))
    dgq, dgk, dgv, dla = _gla_bwd(proj, log_a, do_gla, states, name="gla_bwd")
    dck = dcq + jnp.concatenate([dck_t.T, jnp.zeros((s, SMALL_W - FOX_HEADS), F32)], axis=1)
    dps, dbf, dw2p, db2 = _gates_bwd(dck, ps, bf, w2p, b_gla_a2, dla, name="gates_bwd")
    dproj = jnp.concatenate([dq_fox.astype(BF16), dk_fox, dv_fox, dgq, dgk, dgv, dgr], axis=1)
    dw_main, = _mm(dproj, h1, mode="tn", out_dtypes=[BF16], name="dw_in_main")
    dw_small, = _mm(dps, h1, mode="tn", out_dtypes=[BF16], name="dw_in_small")
    rs_in = w_in_t.shape[0] // 4
    dw_in = _merge_dw_in(dw_main, dw_small).reshape(4, rs_in, d)
    sum_in = _chip_sum(dw_in, idx, "in")
    dh1_small, = _mm(dps, w_small, mode="nn", out_dtypes=[F32], name="d_h1_small")
    dh1, parts_in = _mm(dproj, w_main, mode="nn", out_dtypes=[F32], extras=[dh1_small],
                        epi=lambda acc, e: (acc + e,), name="d_h1", side=_shard_side(sum_in))
    grad_x, dshift_m, da1 = _pre_bwd(dh1, x, dx1, a1, name="pre_mix_bwd")
    bufs = [_final_sum(sm, pt, idx, shp, name=f"grad_final_sum_{tag}")
            for tag, sm, pt, shp in [("in", sum_in, parts_in, (rs_in, d)), ("out", sum_out, parts_out, (d // 4, d)),
                                     ("mlp_in", sum_mlp_in, parts_mlp_in, (d, D_FF // 4)),
                                     ("mlp_out", sum_mlp_out, parts_mlp_out, (D_FF // 4, d))]]

    dmod = jnp.concatenate([dshift_m, da1 * g_pre_mix, dgate_m, dshift_f, da2 * g_pre_mlp, dgate_f], axis=1)
    small = dict(
        dmod=dmod, g_pre_mix=da1 * (1.0 + scale_m), g_post_mix=dg_post_mix, g_pre_mlp=da2 * (1.0 + scale_f),
        g_post_mlp=dg_post_mlp, b_fgate=dbf[:, :FOX_HEADS], w_gla_a2=dw2p[FOX_HEADS:FOX_HEADS + GLA_RANK],
        b_gla_a2=db2, g_fox_out=dg_fox, g_gla_out=dg_gla)
    return loss_part, grad_x, bufs, small


def _pack(arrays):
    flat = jnp.concatenate([a.reshape(-1).astype(F32) for a in arrays])
    n = flat.shape[0]
    rows = -(-n // 128)
    rows = -(-rows // 8) * 8
    return jnp.pad(flat, (0, rows * 128 - n)).reshape(rows, 128)


def _unpack(buf, shapes):
    flat = buf.reshape(-1)
    out, off = [], 0
    for shp in shapes:
        n = 1
        for q in shp:
            n *= q
        out.append(flat[off:off + n].reshape(shp))
        off += n
    return out


SMALL_GRAD_ORDER = ["dmod", "g_pre_mix", "g_post_mix", "g_pre_mlp", "g_post_mlp", "b_fgate", "w_gla_a2", "b_gla_a2",
                    "g_fox_out", "g_gla_out"]


def kernel(x, c, w_ada, b_ada, g_pre_mix, g_post_mix, w_in, b_fgate, w_gla_a2, b_gla_a2, g_fox_out, g_gla_out, w_out, g_pre_mlp, g_post_mlp, w_mlp_in, w_mlp_out, loss_target, m_w_ada, m_b_ada, m_g_pre_mix, m_g_post_mix, m_w_in, m_b_fgate, m_w_gla_a2, m_b_gla_a2, m_g_fox_out, m_g_gla_out, m_w_out, m_g_pre_mlp, m_g_post_mlp, m_w_mlp_in, m_w_mlp_out, v_w_ada, v_b_ada, v_g_pre_mix, v_g_post_mix, v_w_in, v_b_fgate, v_w_gla_a2, v_b_gla_a2, v_g_fox_out, v_g_gla_out, v_w_out, v_g_pre_mlp, v_g_post_mlp, v_w_mlp_in, v_w_mlp_out):
    ix, iy, ic = lax.axis_index("x"), lax.axis_index("y"), lax.axis_index("c")
    chip = 2 * ix + iy
    dev = 4 * ix + 2 * iy + ic
    d = D_MODEL

    tr_in = lambda a: jnp.transpose(a[0])
    own_bf = [tr_in(w_in).astype(BF16), w_out[0].astype(BF16), w_mlp_in[0].astype(BF16), w_mlp_out[0].astype(BF16)]
    w_in_pending = _gather_start(own_bf[0], name="gather_w_in_start")

    c_act = _silu_rows(c + w_in_pending[4][:1, :1], name="silu_c")
    pack1 = _pack([c_act, w_gla_a2[0], g_gla_out[0]])
    rows1 = pack1.shape[0]
    got1 = _gather8(pack1, name="gather_small_fwd").reshape(8, rows1, 128)
    per_dev = [_unpack(got1[q], [(d,), (GLA_RANK, GLA_KW // 4), (GLA_HEADS, GLA_DV // 4)]) for q in range(8)]
    c_all = jnp.stack([p[0] for p in per_dev])
    w_gla_a2_full = jnp.concatenate([per_dev[2 * j][1] for j in range(4)], axis=1)
    g_gla_full = jnp.concatenate([per_dev[2 * j][2] for j in range(4)], axis=1)
    cols = w_ada.shape[2]
    b_ada_shard = lax.dynamic_slice_in_dim(b_ada, chip * cols, cols, axis=1)
    mod_sh = _mod_shard(c_all, w_ada[0], b_ada_shard, name="ada_mod")
    got2 = _gather8(mod_sh, name="gather_mod").reshape(8, 8, cols)
    mod_all = jnp.concatenate([got2[2 * j] for j in range(4)], axis=1)
    mod = lax.dynamic_slice_in_dim(mod_all, dev, 1, axis=0)

    idx = jnp.stack([ic, chip]).astype(jnp.int32)
    loss_part, grad_x, bufs, small = _local_step(
        x[0], loss_target[0], mod, g_pre_mix, g_post_mix, g_pre_mlp, g_post_mlp, w_in_pending, b_fgate,
        w_gla_a2_full, b_gla_a2, g_fox_out[0], g_gla_full, own_bf[0], own_bf[1], own_bf[2], own_bf[3], chip, idx)
    loss = lax.psum(loss_part[0, 0], ("x", "y", "c"))

    g_big = _half_exchange(bufs, name="grad_half_exchange")
    big_w = [(tr_in(w_in), tr_in(m_w_in), tr_in(v_w_in)), (w_out[0], m_w_out[0], v_w_out[0]),
             (w_mlp_in[0], m_w_mlp_in[0], v_w_mlp_in[0]), (w_mlp_out[0], m_w_mlp_out[0], v_w_mlp_out[0])]
    big_res = []
    for q, (g, (w, m, v)) in enumerate(zip(g_big, big_w)):
        res4 = (g,) + tuple(_adam(g, w, m, v, name=f"adam_big_{q}"))
        big_res.append(tuple((jnp.transpose(a) if q == 0 else a)[None] for a in res4))

    pack2 = _pack([small[k] for k in SMALL_GRAD_ORDER])
    rows2 = pack2.shape[0]
    got3 = _gather8(pack2, name="gather_small_grads").reshape(8, rows2, 128)
    dmod_all = got3[:, :6 * d // 128, :].reshape(8, 6 * d)
    sums = _stack_sum(got3, name="small_grad_sum")
    shapes = [(1, 6 * d), (1, d), (1, d), (1, d), (1, d), (1, FOX_HEADS), (1, GLA_RANK, GLA_KW), (1, GLA_KW),
              (1, FOX_HEADS, FOX_HD), (1, GLA_HEADS, GLA_DV)]
    sg = dict(zip(["b_ada"] + SMALL_GRAD_ORDER[1:], _unpack(sums, shapes)))
    sg["w_gla_a2"] = lax.dynamic_slice_in_dim(sg["w_gla_a2"], chip * (GLA_KW // 4), GLA_KW // 4, axis=2)
    sg["g_gla_out"] = lax.dynamic_slice_in_dim(sg["g_gla_out"], chip * (GLA_DV // 4), GLA_DV // 4, axis=2)
    small_names = ["b_ada", "g_pre_mix", "g_post_mix", "b_fgate", "w_gla_a2", "b_gla_a2", "g_fox_out", "g_gla_out",
                   "g_pre_mlp", "g_post_mlp"]
    small_w = dict(b_ada=(b_ada, m_b_ada, v_b_ada), g_pre_mix=(g_pre_mix, m_g_pre_mix, v_g_pre_mix),
                   g_post_mix=(g_post_mix, m_g_post_mix, v_g_post_mix), b_fgate=(b_fgate, m_b_fgate, v_b_fgate),
                   w_gla_a2=(w_gla_a2, m_w_gla_a2, v_w_gla_a2), b_gla_a2=(b_gla_a2, m_b_gla_a2, v_b_gla_a2),
                   g_fox_out=(g_fox_out, m_g_fox_out, v_g_fox_out), g_gla_out=(g_gla_out, m_g_gla_out, v_g_gla_out),
                   g_pre_mlp=(g_pre_mlp, m_g_pre_mlp, v_g_pre_mlp), g_post_mlp=(g_post_mlp, m_g_post_mlp, v_g_post_mlp))
    sshapes = [small_w[k][0].shape for k in small_names]
    pg = _pack([sg[k] for k in small_names])
    pw, pm, pv = [_pack([small_w[k][q] for k in small_names]) for q in range(3)]
    pd, pmn, pvn = _adam(pg, pw, pm, pv, name="adam_small")
    s_delta = dict(zip(small_names, _unpack(pd, sshapes)))
    s_m = dict(zip(small_names, _unpack(pmn, sshapes)))
    s_v = dict(zip(small_names, _unpack(pvn, sshapes)))

    dmod_cols = lax.dynamic_slice_in_dim(dmod_all, chip * cols, cols, axis=1)
    g_ada, d_ada, m_ada, v_ada = _ada_grad_adam(c_all.T, dmod_cols, w_ada[0], m_w_ada[0], v_w_ada[0], name="ada_grad_adam")

    order = ["w_ada", "b_ada", "g_pre_mix", "g_post_mix", "w_in", "b_fgate", "w_gla_a2", "b_gla_a2", "g_fox_out",
             "g_gla_out", "w_out", "g_pre_mlp", "g_post_mlp", "w_mlp_in", "w_mlp_out"]
    res = {"w_ada": (g_ada[None], d_ada[None], m_ada[None], v_ada[None]),
           "w_in": big_res[0], "w_out": big_res[1], "w_mlp_in": big_res[2], "w_mlp_out": big_res[3]}
    for k in small_names:
        res[k] = (sg[k], s_delta[k], s_m[k], s_v[k])
    return (loss, grad_x[None], *[res[k][0] for k in order], *[res[k][1] for k in order],
            *[res[k][2] for k in order], *[res[k][3] for k in order])
```

```python
import functools

import jax
import jax.numpy as jnp
from jax import lax
from jax.experimental import pallas as pl
from jax.experimental.pallas import tpu as pltpu

F32 = jnp.float32
BF16 = jnp.bfloat16
MESH = pl.DeviceIdType.MESH
HIGHEST = lax.Precision.HIGHEST

D_MODEL = 2048
FOX_HEADS = 8
FOX_HD = 128
FOX_W = FOX_HEADS * FOX_HD
GLA_HEADS = 4
GLA_DK = 128
GLA_DV = 256
GLA_KW = GLA_HEADS * GLA_DK
GLA_W = GLA_HEADS * GLA_DV
GLA_RANK = 16
GLA_TEMP = 16.0
CHUNK = 64
D_FF = 4 * D_MODEL
EPS = 1e-6
MAIN_W = 3 * FOX_W + 2 * GLA_KW + 2 * GLA_W
SMALL_W = 128
NEG = -1e30

ADAM_LR = 0.001
ADAM_B1 = 0.9
ADAM_B2 = 0.999
ADAM_EPS = 1e-08
ADAM_WD = 0.01
ADAM_STEP = 10

VMEM_LIMIT = 52 * 1024 * 1024
ROW_TILE = 256
FOX_TQ = 512
FOX_TK = 512
GLA_ROWS = 512
GATE_TS = 512
MM_T = 1024
MM_TK = 2048
MM_TM = 2048


def _cp(*sem):
    return pltpu.CompilerParams(dimension_semantics=sem, vmem_limit_bytes=VMEM_LIMIT)


def _dot_nn(a, b, precision=None):
    return jnp.dot(a, b, preferred_element_type=F32, precision=precision)


def _dot_nt(a, b, precision=None):
    return lax.dot_general(a, b, (((1,), (1,)), ((), ())), preferred_element_type=F32, precision=precision)


def _dot_tn(a, b, precision=None):
    return lax.dot_general(a, b, (((0,), (0,)), ((), ())), preferred_element_type=F32, precision=precision)


def _sigmoid(x):
    return 1.0 / (1.0 + jnp.exp(-x))


def _log_sigmoid(x):
    return jnp.minimum(x, 0.0) - jnp.log(1.0 + jnp.exp(-jnp.abs(x)))


class _Side:
    def __init__(self, inputs, out_shapes, plan_fn, n_copies, aliases=None):
        self.inputs, self.out_shapes, self.plan_fn, self.n_copies = list(inputs), list(out_shapes), plan_fn, n_copies
        self.aliases = dict(aliases or {})

    def scratch(self):
        return [pltpu.SemaphoreType.DMA((self.n_copies,)), pltpu.SemaphoreType.DMA((self.n_copies,))]

    def run(self, in_refs, out_refs, sems, first, last):
        @pl.when(first)
        def _():
            _plan_start(self.plan_fn(in_refs, out_refs), *sems)

        @pl.when(last)
        def _():
            _plan_wait(self.plan_fn(in_refs, out_refs), *sems)


def _mm(a, b, *, mode, out_dtypes, name, tm=None, tn=None, tk=None, extras=(), epi=None,
        out_shapes=None, out_specs=None, side=None, b_slots=0):
    tm, tn, tk = tm or MM_T, tn or MM_T, tk or MM_TK
    b2 = (b.shape[1], b_slots * b.shape[2]) if b_slots else b.shape
    if mode == "nn":
        (m, k), n = a.shape, b2[1]
    elif mode == "nt":
        (m, k), n = a.shape, b2[0]
    else:
        (k, m), n = a.shape, b2[1]
    tm, tn, tk = min(tm, m), min(tn, n), min(tk, k)
    if b_slots:
        tn = min(tn, b.shape[2]) if mode == "nn" else tn
        tk = min(tk, b.shape[2]) if mode == "nt" else tk
    assert m % tm == 0 and n % tn == 0 and k % tk == 0, (name, m, n, k)
    nk = k // tk
    n_out, n_ex = len(out_dtypes), len(extras)
    if epi is None:
        epi = lambda acc: tuple(acc for _ in range(n_out))
    dot = {"nn": _dot_nn, "nt": _dot_nt, "tn": _dot_tn}[mode]

    n_si = len(side.inputs) if side else 0
    n_so = len(side.out_shapes) if side else 0
    grid = (m // tm, n // tn, nk)

    def body(*refs):
        a_ref, b_ref = refs[0], refs[1]
        ex_refs = refs[2:2 + n_ex]
        base = 2 + n_ex + n_si
        o_refs = refs[base:base + n_out]
        scratch = refs[base + n_out + n_so:]
        if side:
            pos = [pl.program_id(q) for q in range(3)]
            first = (pos[0] == 0) & (pos[1] == 0) & (pos[2] == 0)
            last = (pos[0] == grid[0] - 1) & (pos[1] == grid[1] - 1) & (pos[2] == grid[2] - 1)
            side.run(refs[2 + n_ex:base], refs[base + n_out:base + n_out + n_so], scratch[-2:], first, last)
        part = dot(a_ref[...], b_ref[...])

        def finish(acc):
            outs = epi(acc, *[e[...] for e in ex_refs])
            for o_ref, val in zip(o_refs, outs):
                o_ref[...] = val.reshape(o_ref.shape).astype(o_ref.dtype)

        if nk == 1:
            finish(part)
        else:
            acc_ref = scratch[0]
            kk = pl.program_id(2)

            @pl.when(kk == 0)
            def _():
                acc_ref[...] = part

            @pl.when(kk > 0)
            def _():
                acc_ref[...] += part

            @pl.when(kk == nk - 1)
            def _():
                finish(acc_ref[...])

    if mode == "nn":
        a_spec = pl.BlockSpec((tm, tk), lambda i, j, kk: (i, kk))
        b_spec = pl.BlockSpec((tk, tn), lambda i, j, kk: (kk, j))
        if b_slots:
            per = b.shape[2] // tn
            b_spec = pl.BlockSpec((None, tk, tn), lambda i, j, kk: (j // per, kk, j % per))
    elif mode == "nt":
        a_spec = pl.BlockSpec((tm, tk), lambda i, j, kk: (i, kk))
        b_spec = pl.BlockSpec((tn, tk), lambda i, j, kk: (j, kk))
        if b_slots:
            per = b.shape[2] // tk
            b_spec = pl.BlockSpec((None, tn, tk), lambda i, j, kk: (kk // per, j, kk % per))
    else:
        assert not b_slots
        a_spec = pl.BlockSpec((tk, tm), lambda i, j, kk: (kk, i))
        b_spec = pl.BlockSpec((tk, tn), lambda i, j, kk: (kk, j))
    tile_spec = pl.BlockSpec((tm, tn), lambda i, j, kk: (i, j))
    if out_shapes is None:
        out_shapes = [jax.ShapeDtypeStruct((m, n), dt) for dt in out_dtypes]
    if out_specs is None:
        out_specs = [tile_spec for _ in out_dtypes]
    any_spec = pl.BlockSpec(memory_space=pl.ANY)
    res = pl.pallas_call(
        body,
        grid=grid,
        in_specs=[a_spec, b_spec] + [tile_spec for _ in extras] + [any_spec] * n_si,
        out_specs=list(out_specs) + [any_spec] * n_so,
        out_shape=list(out_shapes) + (side.out_shapes if side else []),
        scratch_shapes=([pltpu.VMEM((tm, tn), F32)] if nk > 1 else []) + (side.scratch() if side else []),
        compiler_params=_cp("arbitrary", "arbitrary", "arbitrary") if side else _cp("parallel", "parallel", "arbitrary"),
        input_output_aliases={2 + n_ex + si: n_out + so for si, so in side.aliases.items()} if side else {},
        name=name,
    )(a, b, *extras, *(side.inputs if side else []))
    return res


def _row_spec(ts, d):
    return pl.BlockSpec((ts, d), lambda i: (i, 0))


def _vec_spec(d):
    return pl.BlockSpec((1, d), lambda i: (0, 0))


def _pre_fwd(x, avec, shift, *, name):
    s, d = x.shape
    ts = min(ROW_TILE, s)

    def body(x_ref, a_ref, s_ref, h_ref):
        xv = x_ref[...]
        r = lax.rsqrt(jnp.mean(xv * xv, axis=-1, keepdims=True) + EPS)
        h_ref[...] = (xv * r * a_ref[...] + s_ref[...]).astype(BF16)

    return pl.pallas_call(
        body, grid=(s // ts,),
        in_specs=[_row_spec(ts, d), _vec_spec(d), _vec_spec(d)],
        out_specs=_row_spec(ts, d),
        out_shape=jax.ShapeDtypeStruct((s, d), BF16),
        compiler_params=_cp("parallel"), name=name,
    )(x, avec, shift)


def _post_pre_fwd(x, y, gate, g, avec, shift, *, name):
    s, d = x.shape
    ts = min(ROW_TILE, s)

    def body(x_ref, y_ref, gate_ref, g_ref, a_ref, s_ref, o_ref, h_ref):
        yv = y_ref[...]
        r = lax.rsqrt(jnp.mean(yv * yv, axis=-1, keepdims=True) + EPS)
        x1 = x_ref[...] + gate_ref[...] * (yv * r * g_ref[...])
        o_ref[...] = x1
        r1 = lax.rsqrt(jnp.mean(x1 * x1, axis=-1, keepdims=True) + EPS)
        h_ref[...] = (x1 * r1 * a_ref[...] + s_ref[...]).astype(BF16)

    return pl.pallas_call(
        body, grid=(s // ts,),
        in_specs=[_row_spec(ts, d), _row_spec(ts, d)] + [_vec_spec(d)] * 4,
        out_specs=[_row_spec(ts, d), _row_spec(ts, d)],
        out_shape=[jax.ShapeDtypeStruct((s, d), F32), jax.ShapeDtypeStruct((s, d), BF16)],
        compiler_params=_cp("parallel"), name=name,
    )(x, y, gate, g, avec, shift)


def _post_bwd_math(dxv, yv, gatev, gv):
    r = lax.rsqrt(jnp.mean(yv * yv, axis=-1, keepdims=True) + EPS)
    yhat = yv * r
    dn = dxv * gatev
    dyhat = dn * gv
    dy = r * (dyhat - yhat * jnp.mean(dyhat * yhat, axis=-1, keepdims=True))
    return dy, dxv * (yhat * gv), dn * yhat


def _accumulate(first, pairs):
    @pl.when(first)
    def _():
        for ref, _ in pairs:
            ref[...] = jnp.zeros_like(ref)

    for ref, val in pairs:
        ref[...] += jnp.sum(val, axis=0, keepdims=True)


def _post_loss_bwd(x, y, gate, g, target, *, name):
    s, d = x.shape
    ts = min(ROW_TILE, s)

    def body(x_ref, y_ref, gate_ref, g_ref, t_ref, dx_ref, dy_ref, loss_ref, dgate_ref, dg_ref):
        yv, gatev, gv = y_ref[...], gate_ref[...], g_ref[...]
        r = lax.rsqrt(jnp.mean(yv * yv, axis=-1, keepdims=True) + EPS)
        diff = x_ref[...] + gatev * (yv * r * gv) - t_ref[...]
        dxv = diff * (1.0 / d)
        dx_ref[...] = dxv
        dy, dgate_rows, dg_rows = _post_bwd_math(dxv, yv, gatev, gv)
        dy_ref[...] = dy.astype(BF16)
        first = pl.program_id(0) == 0
        _accumulate(first, [(dgate_ref, dgate_rows), (dg_ref, dg_rows)])

        @pl.when(first)
        def _():
            loss_ref[...] = jnp.zeros_like(loss_ref)

        loss_ref[...] += jnp.sum(jnp.mean(diff * diff, axis=-1, keepdims=True)) * 0.5

    return pl.pallas_call(
        body, grid=(s // ts,),
        in_specs=[_row_spec(ts, d), _row_spec(ts, d), _vec_spec(d), _vec_spec(d), _row_spec(ts, d)],
        out_specs=[_row_spec(ts, d), _row_spec(ts, d), pl.BlockSpec((1, 128), lambda i: (0, 0)), _vec_spec(d),
                   _vec_spec(d)],
        out_shape=[jax.ShapeDtypeStruct((s, d), F32), jax.ShapeDtypeStruct((s, d), BF16),
                   jax.ShapeDtypeStruct((1, 128), F32), jax.ShapeDtypeStruct((1, d), F32),
                   jax.ShapeDtypeStruct((1, d), F32)],
        compiler_params=_cp("arbitrary"), name=name,
    )(x, y, gate, g, target)


def _pre_post_bwd(dh, xin, dres, avec, y, gate, g, *, name):
    s, d = xin.shape
    ts = min(ROW_TILE, s)

    def body(dh_ref, x_ref, dres_ref, a_ref, y_ref, gate_ref, g_ref, dx_ref, dshift_ref, da_ref, dy_ref,
             dgate_ref, dg_ref):
        xv, dhv = x_ref[...], dh_ref[...]
        r = lax.rsqrt(jnp.mean(xv * xv, axis=-1, keepdims=True) + EPS)
        xhat = xv * r
        dxhat = dhv * a_ref[...]
        dxv = dres_ref[...] + r * (dxhat - xhat * jnp.mean(dxhat * xhat, axis=-1, keepdims=True))
        dx_ref[...] = dxv
        dy, dgate_rows, dg_rows = _post_bwd_math(dxv, y_ref[...], gate_ref[...], g_ref[...])
        dy_ref[...] = dy.astype(BF16)
        _accumulate(pl.program_id(0) == 0, [(dshift_ref, dhv), (da_ref, dhv * xhat), (dgate_ref, dgate_rows),
                                            (dg_ref, dg_rows)])

    return pl.pallas_call(
        body, grid=(s // ts,),
        in_specs=[_row_spec(ts, d), _row_spec(ts, d), _row_spec(ts, d), _vec_spec(d), _row_spec(ts, d),
                  _vec_spec(d), _vec_spec(d)],
        out_specs=[_row_spec(ts, d), _vec_spec(d), _vec_spec(d), _row_spec(ts, d), _vec_spec(d), _vec_spec(d)],
        out_shape=[jax.ShapeDtypeStruct((s, d), F32), jax.ShapeDtypeStruct((1, d), F32),
                   jax.ShapeDtypeStruct((1, d), F32), jax.ShapeDtypeStruct((s, d), BF16),
                   jax.ShapeDtypeStruct((1, d), F32), jax.ShapeDtypeStruct((1, d), F32)],
        compiler_params=_cp("arbitrary"), name=name,
    )(dh, xin, dres, avec, y, gate, g)


def _pre_bwd(dh, xin, dres, avec, *, name):
    s, d = xin.shape
    ts = min(ROW_TILE, s)

    def body(dh_ref, x_ref, dres_ref, a_ref, dx_ref, dshift_ref, da_ref):
        xv, dhv = x_ref[...], dh_ref[...]
        r = lax.rsqrt(jnp.mean(xv * xv, axis=-1, keepdims=True) + EPS)
        xhat = xv * r
        dxhat = dhv * a_ref[...]
        dx_ref[...] = dres_ref[...] + r * (dxhat - xhat * jnp.mean(dxhat * xhat, axis=-1, keepdims=True))

        @pl.when(pl.program_id(0) == 0)
        def _():
            dshift_ref[...] = jnp.zeros_like(dshift_ref)
            da_ref[...] = jnp.zeros_like(da_ref)

        dshift_ref[...] += jnp.sum(dhv, axis=0, keepdims=True)
        da_ref[...] += jnp.sum(dhv * xhat, axis=0, keepdims=True)

    return pl.pallas_call(
        body, grid=(s // ts,),
        in_specs=[_row_spec(ts, d), _row_spec(ts, d), _row_spec(ts, d), _vec_spec(d)],
        out_specs=[_row_spec(ts, d), _vec_spec(d), _vec_spec(d)],
        out_shape=[jax.ShapeDtypeStruct((s, d), F32), jax.ShapeDtypeStruct((1, d), F32),
                   jax.ShapeDtypeStruct((1, d), F32)],
        compiler_params=_cp("arbitrary"), name=name,
    )(dh, xin, dres, avec)


def _tri(n, strict=False, upper=False):
    r = lax.broadcasted_iota(jnp.int32, (n, n), 0)
    c = lax.broadcasted_iota(jnp.int32, (n, n), 1)
    if upper:
        r, c = c, r
    return ((r > c) if strict else (r >= c)).astype(F32)


def _gates_fwd(ps, bf, w2p, b2, *, name):
    s = ps.shape[0]
    ts = min(GATE_TS, s)

    def body(ps_ref, bf_ref, w_ref, b2_ref, cum_ref, la_ref, carry_ref):
        @pl.when(pl.program_id(0) == 0)
        def _():
            carry_ref[...] = jnp.zeros_like(carry_ref)

        psv = ps_ref[...]
        lf = _log_sigmoid(psv + bf_ref[...])
        cum = _dot_nn(_tri(ts), lf, HIGHEST) + carry_ref[...]
        cum_ref[...] = cum
        carry_ref[...] = cum[ts - 1:ts, :]
        z = _dot_nn(psv, w_ref[...], HIGHEST) + b2_ref[...]
        la_ref[...] = _log_sigmoid(z) * (1.0 / GLA_TEMP)

    return pl.pallas_call(
        body, grid=(s // ts,),
        in_specs=[_row_spec(ts, SMALL_W), _vec_spec(SMALL_W),
                  pl.BlockSpec((SMALL_W, GLA_KW), lambda i: (0, 0)), _vec_spec(GLA_KW)],
        out_specs=[_row_spec(ts, SMALL_W), _row_spec(ts, GLA_KW)],
        out_shape=[jax.ShapeDtypeStruct((s, SMALL_W), F32), jax.ShapeDtypeStruct((s, GLA_KW), F32)],
        scratch_shapes=[pltpu.VMEM((1, SMALL_W), F32)],
        compiler_params=_cp("arbitrary"), name=name,
    )(ps, bf, w2p, b2)


def _gates_bwd(dck, ps, bf, w2p, b2, dla, *, name):
    s = ps.shape[0]
    ts = min(GATE_TS, s)
    nb = s // ts
    rev = lambda i: (nb - 1 - i, 0)

    def body(dck_ref, ps_ref, bf_ref, w_ref, b2_ref, dla_ref, dps_ref, dbf_ref, dw_ref, db2_ref, carry_ref):
        @pl.when(pl.program_id(0) == 0)
        def _():
            carry_ref[...] = jnp.zeros_like(carry_ref)
            dbf_ref[...] = jnp.zeros_like(dbf_ref)
            dw_ref[...] = jnp.zeros_like(dw_ref)
            db2_ref[...] = jnp.zeros_like(db2_ref)

        psv, dckv = ps_ref[...], dck_ref[...]
        dlf = _dot_nn(_tri(ts, upper=True), dckv, HIGHEST) + carry_ref[...]
        carry_ref[...] += jnp.sum(dckv, axis=0, keepdims=True)
        lane = lax.broadcasted_iota(jnp.int32, (ts, SMALL_W), 1)
        dff = jnp.where(lane < FOX_HEADS, dlf * _sigmoid(-(psv + bf_ref[...])), 0.0)
        z = _dot_nn(psv, w_ref[...], HIGHEST) + b2_ref[...]
        dz = dla_ref[...] * _sigmoid(-z) * (1.0 / GLA_TEMP)
        dps_ref[...] = (_dot_nt(dz, w_ref[...], HIGHEST) + dff).astype(BF16)
        dbf_ref[...] += jnp.sum(dff, axis=0, keepdims=True)
        dw_ref[...] += _dot_tn(psv, dz, HIGHEST)
        db2_ref[...] += jnp.sum(dz, axis=0, keepdims=True)

    return pl.pallas_call(
        body, grid=(nb,),
        in_specs=[pl.BlockSpec((ts, SMALL_W), rev), pl.BlockSpec((ts, SMALL_W), rev), _vec_spec(SMALL_W),
                  pl.BlockSpec((SMALL_W, GLA_KW), lambda i: (0, 0)), _vec_spec(GLA_KW),
                  pl.BlockSpec((ts, GLA_KW), rev)],
        out_specs=[pl.BlockSpec((ts, SMALL_W), rev), _vec_spec(SMALL_W),
                   pl.BlockSpec((SMALL_W, GLA_KW), lambda i: (0, 0)), _vec_spec(GLA_KW)],
        out_shape=[jax.ShapeDtypeStruct((s, SMALL_W), BF16), jax.ShapeDtypeStruct((1, SMALL_W), F32),
                   jax.ShapeDtypeStruct((SMALL_W, GLA_KW), F32), jax.ShapeDtypeStruct((1, GLA_KW), F32)],
        scratch_shapes=[pltpu.VMEM((1, SMALL_W), F32)],
        compiler_params=_cp("arbitrary"), name=name,
    )(dck, ps, bf, w2p, b2, dla)


def _hs(h, hd=FOX_HD):
    return slice(h * hd, (h + 1) * hd)


def _fox_fwd(proj, cum_t, g_fox, *, name, side=None):
    s = proj.shape[0]
    tq, tk = min(FOX_TQ, s), min(FOX_TK, s)
    scale = FOX_HD ** -0.5
    n_si = len(side.inputs) if side else 0
    n_so = len(side.out_shapes) if side else 0
    grid = (s // tq, s // tk)

    def body(*refs):
        q_ref, k_ref, v_ref, ck_ref, g_ref = refs[:5]
        o_ref, n_ref, lse_ref = refs[5 + n_si:8 + n_si]
        m_sc, acc_sc = refs[8 + n_si + n_so:10 + n_si + n_so]
        i, j = pl.program_id(0), pl.program_id(1)
        if side:
            side.run(refs[5:5 + n_si], refs[8 + n_si:8 + n_si + n_so], refs[10 + n_si + n_so:],
                     (i == 0) & (j == 0), (i == grid[0] - 1) & (j == grid[1] - 1))

        @pl.when(j == 0)
        def _():
            m_sc[...] = jnp.full_like(m_sc, NEG)
            acc_sc[...] = jnp.zeros_like(acc_sc)

        def block(masked):
            mask = _causal_mask(i, j, tq, tk) if masked else None
            ones = jnp.ones((tk, FOX_HD), BF16)
            for h in range(FOX_HEADS):
                sc = _fox_logits(_dot_nt(q_ref[:, _hs(h)], k_ref[:, _hs(h)]), ck_ref[h:h + 1, :], mask, scale)
                m_prev = m_sc[h]
                m_new = jnp.maximum(m_prev, jnp.max(sc, axis=-1, keepdims=True))
                alpha = jnp.exp(m_prev - m_new)
                p = jnp.exp(sc - m_new).astype(BF16)
                v_one = jnp.concatenate([v_ref[:, _hs(h)], ones], axis=1)
                acc_sc[:, _hs(h, 2 * FOX_HD)] = alpha * acc_sc[:, _hs(h, 2 * FOX_HD)] + _dot_nn(p, v_one)
                m_sc[h] = m_new

        pl.when(j < i)(functools.partial(block, False))

        @pl.when(j == i)
        def _():
            block(True)
            lane = lax.broadcasted_iota(jnp.int32, (tq, 128), 1)
            lse = jnp.zeros((tq, 128), F32)
            for h in range(FOX_HEADS):
                l_rep = acc_sc[:, 2 * h * FOX_HD + FOX_HD:2 * (h + 1) * FOX_HD]
                o = acc_sc[:, 2 * h * FOX_HD:2 * h * FOX_HD + FOX_HD] / l_rep
                o_ref[:, _hs(h)] = o
                r = lax.rsqrt(jnp.mean(o * o, axis=-1, keepdims=True) + EPS)
                n_ref[:, _hs(h)] = (o * r * g_ref[h:h + 1, :]).astype(BF16)
                lse = jnp.where(lane == h, m_sc[h] + jnp.log(l_rep), lse)
            lse_ref[...] = lse

    kv = lambda col: (lambda i, j: (jnp.minimum(j, i), col))
    any_spec = pl.BlockSpec(memory_space=pl.ANY)
    return pl.pallas_call(
        body, grid=grid,
        in_specs=[pl.BlockSpec((tq, FOX_W), lambda i, j: (i, 0)),
                  pl.BlockSpec((tk, FOX_W), kv(1)),
                  pl.BlockSpec((tk, FOX_W), kv(2)),
                  pl.BlockSpec((FOX_HEADS, tk), lambda i, j: (0, jnp.minimum(j, i))),
                  pl.BlockSpec((FOX_HEADS, FOX_HD), lambda i, j: (0, 0))] + [any_spec] * n_si,
        out_specs=[pl.BlockSpec((tq, FOX_W), lambda i, j: (i, 0)),
                   pl.BlockSpec((tq, FOX_W), lambda i, j: (i, 0)),
                   pl.BlockSpec((tq, 128), lambda i, j: (i, 0))] + [any_spec] * n_so,
        out_shape=[jax.ShapeDtypeStruct((s, FOX_W), F32), jax.ShapeDtypeStruct((s, FOX_W), BF16),
                   jax.ShapeDtypeStruct((s, 128), F32)] + (side.out_shapes if side else []),
        scratch_shapes=[pltpu.VMEM((FOX_HEADS, tq, 1), F32), pltpu.VMEM((tq, 2 * FOX_W), F32)]
        + (side.scratch() if side else []),
        compiler_params=_cp("arbitrary", "arbitrary"), name=name,
    )(proj, proj, proj, cum_t, g_fox, *(side.inputs if side else []))


def _causal_mask(i, j, tq, tk):
    rows = i * tq + lax.broadcasted_iota(jnp.int32, (tq, tk), 0)
    cols = j * tk + lax.broadcasted_iota(jnp.int32, (tq, tk), 1)
    return rows >= cols


def _fox_logits(qk, ck, mask, scale):
    sc = qk * scale - ck
    return sc if mask is None else jnp.where(mask, sc, NEG)


def _fox_bwd(proj, do, cum_t, lse, delta, *, name, side=None):
    s = proj.shape[0]
    tq, tk = min(FOX_TQ, s), min(FOX_TK, s)
    nk, nq = s // tk, s // tq
    scale = FOX_HD ** -0.5
    n_si = len(side.inputs) if side else 0
    n_so = len(side.out_shapes) if side else 0

    def body(*refs):
        q_ref, k_ref, v_ref, do_ref, ck_ref, lse_ref, dl_ref = refs[:7]
        dq_hbm, dk_ref, dv_ref, dcq_hbm, dck_ref = refs[7 + n_si:12 + n_si]
        dq_sc, dcq_sc, dk_sc, dv_sc, dck_sc, out_sems = refs[12 + n_si + n_so:18 + n_si + n_so]
        j, i = pl.program_id(0), pl.program_id(1)
        if side:
            side.run(refs[7:7 + n_si], refs[12 + n_si:12 + n_si + n_so], refs[18 + n_si + n_so:],
                     (j == 0) & (i == 0), (j == nk - 1) & (i == nq - 1))

        @pl.when((j == 0) & (i == 0))
        def _():
            dq_sc[...] = jnp.zeros_like(dq_sc)
            dcq_sc[...] = jnp.zeros_like(dcq_sc)

        @pl.when(i == 0)
        def _():
            dk_sc[...] = jnp.zeros_like(dk_sc)
            dv_sc[...] = jnp.zeros_like(dv_sc)
            dck_sc[...] = jnp.zeros_like(dck_sc)

        def block(masked):
            mask = _causal_mask(i, j, tq, tk) if masked else None
            qrows = pl.ds(pl.multiple_of(i * tq, tq), tq)
            for h in range(FOX_HEADS):
                sc = _fox_logits(_dot_nt(q_ref[:, _hs(h)], k_ref[:, _hs(h)]), ck_ref[h:h + 1, :], mask, scale)
                p = jnp.exp(sc - lse_ref[:, h:h + 1])
                ds = p * (_dot_nt(do_ref[:, _hs(h)], v_ref[:, _hs(h)]) - dl_ref[:, h:h + 1])
                dsb = ds.astype(BF16)
                dv_sc[:, _hs(h)] += _dot_tn(p.astype(BF16), do_ref[:, _hs(h)])
                dk_sc[:, _hs(h)] += _dot_tn(dsb, q_ref[:, _hs(h)])
                dq_sc[qrows, _hs(h)] += _dot_nn(dsb, k_ref[:, _hs(h)]) * scale
                dck_sc[h:h + 1, :] -= jnp.sum(ds, axis=0, keepdims=True)
                dcq_sc[qrows, h:h + 1] += jnp.sum(ds, axis=-1, keepdims=True)

        pl.when(i > j)(functools.partial(block, False))
        pl.when(i == j)(functools.partial(block, True))

        @pl.when(i == nq - 1)
        def _():
            dk_ref[...] = (dk_sc[...] * scale).astype(BF16)
            dv_ref[...] = dv_sc[...].astype(BF16)
            dck_ref[...] = dck_sc[...]

        @pl.when((j == nk - 1) & (i == nq - 1))
        def _():
            out_q = pltpu.make_async_copy(dq_sc, dq_hbm, out_sems.at[0])
            out_c = pltpu.make_async_copy(dcq_sc, dcq_hbm, out_sems.at[1])
            out_q.start()
            out_c.start()
            out_q.wait()
            out_c.wait()

    qrow = lambda j, i: (jnp.maximum(i, j), 0)
    krow = lambda col: (lambda j, i: (j, col))
    any_spec = pl.BlockSpec(memory_space=pl.ANY)
    return pl.pallas_call(
        body, grid=(nk, nq),
        in_specs=[pl.BlockSpec((tq, FOX_W), qrow), pl.BlockSpec((tk, FOX_W), krow(1)),
                  pl.BlockSpec((tk, FOX_W), krow(2)),
                  pl.BlockSpec((tq, FOX_W), qrow),
                  pl.BlockSpec((FOX_HEADS, tk), lambda j, i: (0, j)),
                  pl.BlockSpec((tq, 128), qrow), pl.BlockSpec((tq, 128), qrow)] + [any_spec] * n_si,
        out_specs=[any_spec, pl.BlockSpec((tk, FOX_W), lambda j, i: (j, 0)),
                   pl.BlockSpec((tk, FOX_W), lambda j, i: (j, 0)), any_spec,
                   pl.BlockSpec((FOX_HEADS, tk), lambda j, i: (0, j))] + [any_spec] * n_so,
        out_shape=[jax.ShapeDtypeStruct((s, FOX_W), F32), jax.ShapeDtypeStruct((s, FOX_W), BF16),
                   jax.ShapeDtypeStruct((s, FOX_W), BF16), jax.ShapeDtypeStruct((s, 128), F32),
                   jax.ShapeDtypeStruct((FOX_HEADS, s), F32)] + (side.out_shapes if side else []),
        scratch_shapes=[pltpu.VMEM((s, FOX_W), F32), pltpu.VMEM((s, 128), F32),
                        pltpu.VMEM((tk, FOX_W), F32), pltpu.VMEM((tk, FOX_W), F32), pltpu.VMEM((FOX_HEADS, tk), F32),
                        pltpu.SemaphoreType.DMA((2,))] + (side.scratch() if side else []),
        compiler_params=_cp("arbitrary", "arbitrary"), name=name,
    )(proj, proj, proj, do, cum_t, lse, delta, *(side.inputs if side else []))


def _head_norm_bwd(dn_in, o, g, gr_src, *, nh, hd, dn_col, gr_col, name):
    s, w = o.shape
    ts = min(ROW_TILE, s)
    gated = gr_src is not None

    def body(*refs):
        if gated:
            dn_ref, o_ref, g_ref, gr_ref, do_ref, dgr_ref, dl_ref, dg_ref = refs
        else:
            dn_ref, o_ref, g_ref, do_ref, dl_ref, dg_ref = refs

        @pl.when(pl.program_id(0) == 0)
        def _():
            dg_ref[...] = jnp.zeros_like(dg_ref)

        lane = lax.broadcasted_iota(jnp.int32, (ts, 128), 1)
        delta = jnp.zeros((ts, 128), F32)
        for h in range(nh):
            sl = _hs(h, hd)
            ov = o_ref[:, sl]
            dnv = dn_ref[:, sl].astype(F32)
            gv = g_ref[h:h + 1, :]
            r = lax.rsqrt(jnp.mean(ov * ov, axis=-1, keepdims=True) + EPS)
            ohat = ov * r
            if gated:
                grv = gr_ref[:, sl].astype(F32)
                sig = _sigmoid(grv)
                dgr_ref[:, sl] = (dnv * (ohat * gv) * (sig * (1.0 + grv * (1.0 - sig)))).astype(BF16)
                dnv = dnv * (grv * sig)
            dg_ref[h:h + 1, :] += jnp.sum(dnv * ohat, axis=0, keepdims=True)
            dohat = dnv * gv
            do = r * (dohat - ohat * jnp.mean(dohat * ohat, axis=-1, keepdims=True))
            do_ref[:, sl] = do.astype(BF16)
            delta = jnp.where(lane == h, jnp.sum(do.astype(BF16).astype(F32) * ov, axis=-1, keepdims=True), delta)
        dl_ref[...] = delta

    in_specs = [pl.BlockSpec((ts, w), lambda i: (i, dn_col)), _row_spec(ts, w),
                pl.BlockSpec((nh, hd), lambda i: (0, 0))]
    args = [dn_in, o, g]
    out_specs = [_row_spec(ts, w)]
    out_shape = [jax.ShapeDtypeStruct((s, w), BF16)]
    if gated:
        in_specs.append(pl.BlockSpec((ts, w), lambda i: (i, gr_col)))
        args.append(gr_src)
        out_specs.append(_row_spec(ts, w))
        out_shape.append(jax.ShapeDtypeStruct((s, w), BF16))
    out_specs += [_row_spec(ts, 128), pl.BlockSpec((nh, hd), lambda i: (0, 0))]
    out_shape += [jax.ShapeDtypeStruct((s, 128), F32), jax.ShapeDtypeStruct((nh, hd), F32)]
    return pl.pallas_call(
        body, grid=(s // ts,), in_specs=in_specs, out_specs=out_specs, out_shape=out_shape,
        compiler_params=_cp("arbitrary"), name=name,
    )(*args)


GQ_BLK = 3 * FOX_W // GLA_DK
GK_BLK = GQ_BLK + GLA_HEADS
GV_BLK = (3 * FOX_W + 2 * GLA_KW) // GLA_DV
GR_BLK = GV_BLK + GLA_HEADS


def _gla_chunk_terms(la):
    cum = _dot_nn(_tri(CHUNK), la, HIGHEST)
    total = cum[CHUNK - 1:CHUNK, :]
    return jnp.exp(total - cum), jnp.exp(total)


def _gla_fwd(proj, log_a, g_gla, *, name):
    s = proj.shape[0]
    rows = min(GLA_ROWS, s)
    cb = rows // CHUNK
    nblk = s // rows
    scale = GLA_DK ** -0.5

    def body(q_ref, k_ref, v_ref, gr_ref, la_ref, g_ref, o_ref, n_ref, st_ref, st_sc):
        h = pl.program_id(0)

        @pl.when(pl.program_id(1) == 0)
        def _():
            st_sc[...] = jnp.zeros_like(st_sc)

        gv = g_ref[pl.ds(h, 1), :]
        for ci in range(cb):
            sl = slice(ci * CHUNK, (ci + 1) * CHUNK)
            e, dec = _gla_chunk_terms(la_ref[sl, :])
            k_dec = (k_ref[sl, :].astype(F32) * e).astype(BF16)
            st = st_sc[...] * dec + _dot_tn(v_ref[sl, :], k_dec)
            st_sc[...] = st
            st_ref[0, ci] = st
            qs = (q_ref[sl, :].astype(F32) * scale).astype(BF16)
            o = _dot_nt(qs, st.astype(BF16))
            o_ref[sl, :] = o
            r = lax.rsqrt(jnp.mean(o * o, axis=-1, keepdims=True) + EPS)
            grv = gr_ref[sl, :].astype(F32)
            n_ref[sl, :] = (o * r * gv * (grv * _sigmoid(grv))).astype(BF16)

    return pl.pallas_call(
        body, grid=(GLA_HEADS, nblk),
        in_specs=[pl.BlockSpec((rows, GLA_DK), lambda h, n: (n, GQ_BLK + h)),
                  pl.BlockSpec((rows, GLA_DK), lambda h, n: (n, GK_BLK + h)),
                  pl.BlockSpec((rows, GLA_DV), lambda h, n: (n, GV_BLK + h)),
                  pl.BlockSpec((rows, GLA_DV), lambda h, n: (n, GR_BLK + h)),
                  pl.BlockSpec((rows, GLA_DK), lambda h, n: (n, h)),
                  pl.BlockSpec((GLA_HEADS, GLA_DV), lambda h, n: (0, 0))],
        out_specs=[pl.BlockSpec((rows, GLA_DV), lambda h, n: (n, h)),
                   pl.BlockSpec((rows, GLA_DV), lambda h, n: (n, h)),
                   pl.BlockSpec((1, cb, GLA_DV, GLA_DK), lambda h, n: (h, n, 0, 0))],
        out_shape=[jax.ShapeDtypeStruct((s, GLA_W), F32), jax.ShapeDtypeStruct((s, GLA_W), BF16),
                   jax.ShapeDtypeStruct((GLA_HEADS, s // CHUNK, GLA_DV, GLA_DK), F32)],
        scratch_shapes=[pltpu.VMEM((GLA_DV, GLA_DK), F32)],
        compiler_params=_cp("parallel", "arbitrary"), name=name,
    )(proj, proj, proj, proj, log_a, g_gla)


def _gla_bwd(proj, log_a, do, states, *, name, side=None):
    s = proj.shape[0]
    rows = min(GLA_ROWS, s)
    cb = rows // CHUNK
    nblk = s // rows
    scale = GLA_DK ** -0.5
    n_si = len(side.inputs) if side else 0
    n_so = len(side.out_shapes) if side else 0

    def body(*refs):
        q_ref, k_ref, v_ref, la_ref, do_ref, st_ref, prev_ref = refs[:7]
        dq_ref, dk_ref, dv_ref, dla_ref = refs[7 + n_si:11 + n_si]
        g_sc = refs[11 + n_si + n_so]
        nrev = pl.program_id(1)
        blk = nblk - 1 - nrev
        if side:
            hh = pl.program_id(0)
            side.run(refs[7:7 + n_si], refs[11 + n_si:11 + n_si + n_so], refs[12 + n_si + n_so:],
                     (hh == 0) & (nrev == 0), (hh == GLA_HEADS - 1) & (nrev == nblk - 1))

        @pl.when(nrev == 0)
        def _():
            g_sc[...] = jnp.zeros_like(g_sc)

        for ci in reversed(range(cb)):
            sl = slice(ci * CHUNK, (ci + 1) * CHUNK)
            e, dec = _gla_chunk_terms(la_ref[sl, :])
            kd = k_ref[sl, :].astype(F32) * e
            qs = (q_ref[sl, :].astype(F32) * scale).astype(BF16)
            dov = do_ref[sl, :]
            st = st_ref[0, ci]
            if ci > 0:
                st_prev = st_ref[0, ci - 1]
            else:
                st_prev = prev_ref[0, 0] * (blk > 0).astype(F32)
            dq_ref[sl, :] = (_dot_nn(dov, st.astype(BF16)) * scale).astype(BF16)
            gt = g_sc[...] + _dot_tn(dov, qs)
            gtb = gt.astype(BF16)
            dkd = _dot_nn(v_ref[sl, :], gtb)
            dv_ref[sl, :] = _dot_nt(kd.astype(BF16), gtb).astype(BF16)
            dk_ref[sl, :] = (dkd * e).astype(BF16)
            ddec = jnp.sum(gt * st_prev, axis=0, keepdims=True) * dec
            dla_ref[sl, :] = _dot_nn(_tri(CHUNK, strict=True), dkd * kd, HIGHEST) + ddec
            g_sc[...] = gt * dec

    rev = lambda col0: (lambda h, n: (nblk - 1 - n, col0 + h))
    return pl.pallas_call(
        body, grid=(GLA_HEADS, nblk),
        in_specs=[pl.BlockSpec((rows, GLA_DK), rev(GQ_BLK)),
                  pl.BlockSpec((rows, GLA_DK), rev(GK_BLK)),
                  pl.BlockSpec((rows, GLA_DV), rev(GV_BLK)),
                  pl.BlockSpec((rows, GLA_DK), rev(0)),
                  pl.BlockSpec((rows, GLA_DV), rev(0)),
                  pl.BlockSpec((1, cb, GLA_DV, GLA_DK), lambda h, n: (h, nblk - 1 - n, 0, 0)),
                  pl.BlockSpec((1, 1, GLA_DV, GLA_DK),
                               lambda h, n: (h, jnp.maximum((nblk - 1 - n) * cb - 1, 0), 0, 0))]
        + [pl.BlockSpec(memory_space=pl.ANY)] * n_si,
        out_specs=[pl.BlockSpec((rows, GLA_DK), rev(0)), pl.BlockSpec((rows, GLA_DK), rev(0)),
                   pl.BlockSpec((rows, GLA_DV), rev(0)), pl.BlockSpec((rows, GLA_DK), rev(0))]
        + [pl.BlockSpec(memory_space=pl.ANY)] * n_so,
        out_shape=[jax.ShapeDtypeStruct((s, GLA_KW), BF16), jax.ShapeDtypeStruct((s, GLA_KW), BF16),
                   jax.ShapeDtypeStruct((s, GLA_W), BF16), jax.ShapeDtypeStruct((s, GLA_KW), F32)]
        + (side.out_shapes if side else []),
        scratch_shapes=[pltpu.VMEM((GLA_DV, GLA_DK), F32)] + (side.scratch() if side else []),
        compiler_params=_cp("arbitrary", "arbitrary"), name=name,
    )(proj, proj, proj, log_a, do, states, states, *(side.inputs if side else []))


def _row_tile(r):
    tr = min(ROW_TILE, r)
    while r % tr or tr % 8:
        tr -= 1
    return tr


def _adamw_math(w, g, m, v):
    m = ADAM_B1 * m + (1.0 - ADAM_B1) * g
    v = ADAM_B2 * v + (1.0 - ADAM_B2) * (g * g)
    m_hat = m / (1.0 - ADAM_B1 ** ADAM_STEP)
    v_hat = v / (1.0 - ADAM_B2 ** ADAM_STEP)
    delta = -ADAM_LR * (m_hat / (jnp.sqrt(v_hat) + ADAM_EPS) + ADAM_WD * w)
    return delta, m, v


COL_TILE = 256


def _tile_2d(r, c):
    if r % 8 == 0 and _row_tile(r) >= 64:
        return _row_tile(r), c
    assert c % COL_TILE == 0, (r, c)
    return r, COL_TILE


def _half_shape(shape):
    r, c = shape[-2:]
    return tuple(shape[:-2]) + ((r // 2, c) if _half_axis(r) == 0 else (r, c // 2))


def _adam(g, w, m, v, *, name):
    r, c = w.shape
    tr, tc = _tile_2d(r, c)

    def body(g_ref, w_ref, m_ref, v_ref, d_ref, mo_ref, vo_ref):
        d, mn, vn = _adamw_math(w_ref[...], g_ref[...], m_ref[...], v_ref[...])
        d_ref[...] = d
        mo_ref[...] = mn
        vo_ref[...] = vn

    spec = pl.BlockSpec((tr, tc), lambda i, j: (i, j))
    return pl.pallas_call(
        body, grid=(r // tr, c // tc), in_specs=[spec] * 4, out_specs=[spec] * 3,
        out_shape=[jax.ShapeDtypeStruct((r, c), F32)] * 3,
        compiler_params=_cp("parallel", "parallel"), name=name,
    )(g, w, m, v)


def _ada_grad_adam(c_all_t, dmod_cols, w, m, v, *, name):
    r, c = w.shape
    tr, tc = min(512, r), min(1024, c)

    def body(ct_ref, dm_ref, w_ref, m_ref, v_ref, g_ref, d_ref, mo_ref, vo_ref):
        g = _dot_nn(ct_ref[...], dm_ref[...], HIGHEST)
        g_ref[...] = g
        d, mn, vn = _adamw_math(w_ref[...], g, m_ref[...], v_ref[...])
        d_ref[...] = d
        mo_ref[...] = mn
        vo_ref[...] = vn

    spec = pl.BlockSpec((tr, tc), lambda i, j: (i, j))
    nb = c_all_t.shape[1]
    return pl.pallas_call(
        body, grid=(r // tr, c // tc),
        in_specs=[pl.BlockSpec((tr, nb), lambda i, j: (i, 0)), pl.BlockSpec((nb, tc), lambda i, j: (0, j)),
                  spec, spec, spec],
        out_specs=[spec] * 4, out_shape=[jax.ShapeDtypeStruct((r, c), F32)] * 4,
        compiler_params=_cp("parallel", "parallel"), name=name,
    )(c_all_t, dmod_cols, w, m, v)


def _mod_shard(c_all, w, b, *, name):
    k, c = w.shape
    tc = min(512, c)
    nb = c_all.shape[0]

    def body(c_ref, w_ref, b_ref, o_ref):
        o_ref[...] = _dot_nn(c_ref[...], w_ref[...], HIGHEST) + b_ref[...]

    return pl.pallas_call(
        body, grid=(c // tc,),
        in_specs=[pl.BlockSpec((nb, k), lambda j: (0, 0)), pl.BlockSpec((k, tc), lambda j: (0, j)),
                  pl.BlockSpec((1, tc), lambda j: (0, j))],
        out_specs=pl.BlockSpec((nb, tc), lambda j: (0, j)),
        out_shape=jax.ShapeDtypeStruct((nb, c), F32),
        compiler_params=_cp("parallel"), name=name,
    )(c_all, w, b)


def _silu_rows(c, *, name):
    def body(c_ref, o_ref):
        cv = c_ref[...]
        o_ref[...] = cv * _sigmoid(cv)

    return pl.pallas_call(body, out_shape=jax.ShapeDtypeStruct(c.shape, F32), name=name)(c)


def _pair_sum(g, got, idx, *, name):
    p, r, c = g.shape
    ax = _half_axis(r)
    hr, hc = _half_shape((r, c))
    tr, tc = _tile_2d(hr, hc)
    nbr, nbc = hr // tr, hc // tc

    def body(idx_ref, a_ref, b_ref, o_ref):
        o_ref[...] = (a_ref[...].astype(F32) + b_ref[...].astype(F32)).astype(BF16)

    def own_map(i, j, k, idx_ref):
        return (i, j + (idx_ref[0] * nbr if ax == 0 else 0), k + (idx_ref[0] * nbc if ax == 1 else 0))

    half_spec = pl.BlockSpec((1, tr, tc), lambda i, j, k, idx_ref: (i, j, k))
    return pl.pallas_call(
        body,
        grid_spec=pltpu.PrefetchScalarGridSpec(
            num_scalar_prefetch=1, grid=(p, nbr, nbc),
            in_specs=[pl.BlockSpec((1, tr, tc), own_map), half_spec],
            out_specs=half_spec),
        out_shape=jax.ShapeDtypeStruct((p, hr, hc), BF16),
        compiler_params=_cp("parallel", "parallel", "parallel"), name=name,
    )(idx, g, got)


def _final_sum(own, parts, idx, shard_shape, *, name):
    ax = _half_axis(shard_shape[0])
    hr, hc = own.shape[1:]
    tr, tc = _tile_2d(hr, hc)
    nbr, nbc = hr // tr, hc // tc

    def body(idx_ref, own_ref, parts_ref, o_ref):
        acc = own_ref[0].astype(F32)
        for q in range(3):
            acc = acc + parts_ref[q].astype(F32)
        o_ref[...] = acc

    def out_map(j, k, idx_ref):
        return (j + (idx_ref[0] * nbr if ax == 0 else 0), k + (idx_ref[0] * nbc if ax == 1 else 0))

    return pl.pallas_call(
        body,
        grid_spec=pltpu.PrefetchScalarGridSpec(
            num_scalar_prefetch=1, grid=(nbr, nbc),
            in_specs=[pl.BlockSpec((1, tr, tc), lambda j, k, idx_ref: (idx_ref[1], j, k)),
                      pl.BlockSpec((3, tr, tc), lambda j, k, idx_ref: (0, j, k))],
            out_specs=pl.BlockSpec((tr, tc), out_map)),
        out_shape=jax.ShapeDtypeStruct(tuple(shard_shape), F32),
        compiler_params=_cp("parallel", "parallel"), name=name,
    )(idx, own, parts)


def _stack_sum(x, *, name):
    p, r, c = x.shape
    tr = _row_tile(r)

    def body(x_ref, o_ref):
        acc = x_ref[0].astype(F32)
        for q in range(1, p):
            acc = acc + x_ref[q].astype(F32)
        o_ref[...] = acc

    return pl.pallas_call(
        body, grid=(r // tr,),
        in_specs=[pl.BlockSpec((p, tr, c), lambda i: (0, i, 0))],
        out_specs=pl.BlockSpec((tr, c), lambda i: (i, 0)),
        out_shape=jax.ShapeDtypeStruct((r, c), F32),
        compiler_params=_cp("parallel"), name=name,
    )(x)


def _place():
    x, y, c = lax.axis_index("x"), lax.axis_index("y"), lax.axis_index("c")
    chips = [(1 - x, y), (x, 1 - y), (1 - x, 1 - y)]
    return x, y, c, chips


def _gather8(x_shard, *, name):
    m_per, n = x_shard.shape

    def body(x_ref, out_ref, send_sems, recv_sems, local_sem):
        x, y, c, chips = _place()
        me, sibling = (x, y, c), (x, y, 1 - c)

        def rows(px, py, pc):
            return out_ref.at[pl.ds((4 * px + 2 * py + pc) * m_per, m_per), :]

        def copy(k, block, to, src=None):
            return pltpu.make_async_remote_copy(
                src_ref=rows(*block) if src is None else src, dst_ref=rows(*block),
                send_sem=send_sems.at[k], recv_sem=recv_sems.at[k], device_id=to, device_id_type=MESH)

        mine = pltpu.make_async_copy(x_ref, rows(*me), local_sem)
        mine.start()
        first = [copy(0, me, sibling, src=x_ref)]
        first += [copy(1 + j, me, (*chip, c), src=x_ref) for j, chip in enumerate(chips)]
        for cp in first:
            cp.start()
        passed = [copy(4 + j, (*chip, c), sibling) for j, chip in enumerate(chips)]
        for j, chip in enumerate(chips):
            copy(1 + j, (*chip, c), me).wait_recv()
            passed[j].start()
        copy(0, sibling, me).wait_recv()
        for j, chip in enumerate(chips):
            copy(4 + j, (*chip, 1 - c), me).wait_recv()
        for cp in first + passed:
            cp.wait_send()
        mine.wait()

    return pl.pallas_call(
        body,
        out_shape=jax.ShapeDtypeStruct((8 * m_per, n), x_shard.dtype),
        in_specs=[pl.BlockSpec(memory_space=pltpu.VMEM)],
        out_specs=pl.BlockSpec(memory_space=pltpu.VMEM),
        scratch_shapes=[pltpu.SemaphoreType.DMA((7,)), pltpu.SemaphoreType.DMA((7,)), pltpu.SemaphoreType.DMA],
        name=name,
    )(x_shard)


def _gather_weights(shards, *, name):
    return _comm_call(lambda ins, outs: [cp for i, o in zip(ins, outs) for cp in _plan_gather_ici(i, o)],
                      shards, [jax.ShapeDtypeStruct((4,) + s.shape, s.dtype) for s in shards], name=name)


def _plan_start(plan, send_sems, recv_sems):
    for k, (src, dst, _, peer) in enumerate(plan):
        pltpu.make_async_remote_copy(src_ref=src, dst_ref=dst, send_sem=send_sems.at[k], recv_sem=recv_sems.at[k],
                                     device_id=peer, device_id_type=MESH).start()


def _plan_wait(plan, send_sems, recv_sems):
    for k, (src, _, land, peer) in enumerate(plan):
        pltpu.make_async_remote_copy(src_ref=src, dst_ref=land, send_sem=send_sems.at[k], recv_sem=recv_sems.at[k],
                                     device_id=peer, device_id_type=MESH).wait_recv()
    for k, (src, dst, _, peer) in enumerate(plan):
        pltpu.make_async_remote_copy(src_ref=src, dst_ref=dst, send_sem=send_sems.at[k], recv_sem=recv_sems.at[k],
                                     device_id=peer, device_id_type=MESH).wait_send()


def _half_axis(rows):
    return 0 if rows % 32 == 0 else 1


def _rows_half(ref, hc, axis, part=None):
    size = ref.shape[axis] // 2
    start = hc * size
    if part is not None:
        size //= part[1]
        start = start + part[0] * size
    idx = [slice(None)] * len(ref.shape)
    idx[axis] = pl.ds(start, size)
    return ref.at[tuple(idx)]


def _plan_gather_ici(shard, full, part=None):
    x, y, c, chips = _place()
    ax = _half_axis(shard.shape[0])
    src = _rows_half(shard, c, ax, part)
    return [(src, _rows_half(full.at[2 * x + y], c, ax, part), _rows_half(full.at[2 * cx + cy], c, ax, part),
             (cx, cy, c)) for cx, cy in chips]


def _plan_gather_d2d(full):
    x, y, c, chips = _place()
    ax = _half_axis(full.shape[1])
    plan = []
    for cx, cy in chips:
        slot = full.at[2 * cx + cy]
        plan.append((_rows_half(slot, c, ax), _rows_half(slot, c, ax), _rows_half(slot, 1 - c, ax), (x, y, 1 - c)))
    return plan


def _plan_pair(grad, got):
    x, y, c, _ = _place()
    return [(_rows_half(grad, 1 - c, 1 + _half_axis(grad.shape[1])), got, got, (x, y, 1 - c))]


def _plan_shard_ici(sums, parts, piece=None):
    _, _, c, chips = _place()

    def rows(ref):
        if piece is None:
            return ref
        size = ref.shape[0] // piece[1]
        return ref.at[pl.ds(piece[0] * size, size), :]

    return [(rows(sums.at[2 * cx + cy]), rows(parts.at[k]), rows(parts.at[k]), (cx, cy, c))
            for k, (cx, cy) in enumerate(chips)]


def _plan_half(buf):
    x, y, c, _ = _place()
    ax = _half_axis(buf.shape[0])
    mine = _rows_half(buf, c, ax)
    return [(mine, mine, _rows_half(buf, 1 - c, ax), (x, y, 1 - c))]


def _comm_call(plan_fn, inputs, out_shapes, *, name, aliases=None):
    ni, no = len(inputs), len(out_shapes)

    def body(*refs):
        plan = plan_fn(refs[:ni], refs[ni:ni + no])
        send_sems, recv_sems = refs[ni + no:]
        _plan_start(plan, send_sems, recv_sems)
        _plan_wait(plan, send_sems, recv_sems)

    any_spec = pl.BlockSpec(memory_space=pl.ANY)
    n_copies = 3 * max(ni, no)
    return pl.pallas_call(
        body, out_shape=list(out_shapes), in_specs=[any_spec] * ni, out_specs=[any_spec] * no,
        scratch_shapes=[pltpu.SemaphoreType.DMA((n_copies,)), pltpu.SemaphoreType.DMA((n_copies,))],
        input_output_aliases=aliases or {}, name=name,
    )(*inputs)


def _gather_forward(fulls, *, name):
    return _comm_call(lambda ins, outs: [cp for o in outs for cp in _plan_gather_d2d(o)],
                      fulls, [jax.ShapeDtypeStruct(f.shape, f.dtype) for f in fulls], name=name,
                      aliases={k: k for k in range(len(fulls))})


def _pair_exchange(grads, *, name):
    return _comm_call(lambda ins, outs: [cp for i, o in zip(ins, outs) for cp in _plan_pair(i, o)],
                      grads, [jax.ShapeDtypeStruct(_half_shape(g.shape), g.dtype) for g in grads], name=name)


def _half_exchange(bufs, *, name):
    return _comm_call(lambda ins, outs: [cp for o in outs for cp in _plan_half(o)],
                      bufs, [jax.ShapeDtypeStruct(b.shape, b.dtype) for b in bufs], name=name,
                      aliases={k: k for k in range(len(bufs))})


IN_SHARD_ROWS = 1542
IN_SLOT_ROWS = 1552
IN_SECTIONS = ((0, 3072, "main", 0), (3072, 3080, "small", 0), (3080, 5128, "main", 3072),
               (5128, 5144, "small", 8), (5144, 6168, "main", 5120))


def _split_w_in(w_slots):
    d = w_slots.shape[1]
    out = {"main": [], "small": []}
    for lo, hi, which, _ in IN_SECTIONS:
        while lo < hi:
            j = lo // IN_SHARD_ROWS
            end = min(hi, (j + 1) * IN_SHARD_ROWS)
            base = j * IN_SLOT_ROWS - j * IN_SHARD_ROWS
            out[which].append(w_slots[base + lo:base + end])
            lo = end
    out["small"].append(jnp.zeros((SMALL_W - 24, d), w_slots.dtype))
    return jnp.concatenate(out["main"], axis=0), jnp.concatenate(out["small"], axis=0)


def _stack_dw_in(dw_main, dw_small):
    src = {"main": dw_main, "small": dw_small}
    slots = []
    for j in range(4):
        lo, hi = j * IN_SHARD_ROWS, (j + 1) * IN_SHARD_ROWS
        rows = []
        for a, b, which, off in IN_SECTIONS:
            s0, s1 = max(lo, a), min(hi, b)
            if s0 < s1:
                rows.append(src[which][off + s0 - a:off + s1 - a])
        rows.append(jnp.zeros((IN_SLOT_ROWS - IN_SHARD_ROWS, dw_main.shape[1]), dw_main.dtype))
        slots.append(jnp.concatenate(rows, axis=0))
    return jnp.stack(slots)


def _gather_side(shards):
    return _Side(shards, [jax.ShapeDtypeStruct((4,) + w.shape, w.dtype) for w in shards],
                 lambda ins, outs: [cp for i, o in zip(ins, outs) for cp in _plan_gather_ici(i, o)], 3 * len(shards))


def _finish_gather(fulls, owns, chip, *, name):
    fulls = _gather_forward(list(fulls), name=name)
    return [lax.dynamic_update_index_in_dim(f, o, chip, 0) for f, o in zip(fulls, owns)]


def _parts_shape(sums):
    return jax.ShapeDtypeStruct((3,) + sums.shape[1:], sums.dtype)


def _shard_side(sums):
    return _Side([sums], [_parts_shape(sums)], lambda ins, outs: _plan_shard_ici(ins[0], outs[0]), 3)


def _got_shape(grad):
    return jax.ShapeDtypeStruct(_half_shape(grad.shape), grad.dtype)


def _pair_side(grad):
    return _Side([grad], [_got_shape(grad)], lambda ins, outs: _plan_pair(ins[0], outs[0]), 1)


def _chip_sum(grad, idx, tag):
    got, = _pair_exchange([grad], name=f"grad_pair_exchange_{tag}")
    return _pair_sum(grad, got, idx, name=f"grad_pair_sum_{tag}")


def _local_step(x, target, mod, g_pre_mix, g_post_mix, g_pre_mlp, g_post_mlp, w_in_t, b_fgate, w_gla_a2,
                b_gla_a2, g_fox, g_gla, own_w_out, own_w_mlp_in, own_w_mlp_out, chip, idx):
    s, d = x.shape
    shift_m, scale_m, gate_m, shift_f, scale_f, gate_f = [mod[:, i * d:(i + 1) * d] for i in range(6)]
    a1 = g_pre_mix * (1.0 + scale_m)
    a2 = g_pre_mlp * (1.0 + scale_f)
    bf = jnp.concatenate([b_fgate, jnp.zeros((1, SMALL_W - FOX_HEADS), F32)], axis=1)
    w2p = jnp.zeros((SMALL_W, GLA_KW), F32).at[FOX_HEADS:FOX_HEADS + GLA_RANK].set(w_gla_a2)

    w_main, w_small = _split_w_in(w_in_t)
    h1 = _pre_fwd(x, a1, shift_m, name="pre_mix_fwd")
    full_shape = lambda w: jax.ShapeDtypeStruct((4,) + w.shape, w.dtype)
    first_side = _Side(
        [own_w_out, own_w_mlp_out], [full_shape(own_w_out), full_shape(own_w_mlp_out)],
        lambda ins, outs: _plan_gather_ici(ins[0], outs[0]) + _plan_gather_ici(ins[1], outs[1], part=(0, 4)), 6)
    proj, gw_out, gw_mlp_out = _mm(h1, w_main, mode="nt", out_dtypes=[BF16], name="in_proj_main", side=first_side)
    ps, = _mm(h1, w_small, mode="nt", out_dtypes=[F32], name="in_proj_small")
    gw_out, = _finish_gather([gw_out], [own_w_out], chip, name="gather_w_out_d2d")
    w_out_full = gw_out.reshape(-1, d)
    cum, log_a = _gates_fwd(ps, bf, w2p, b_gla_a2, name="gates_fwd")
    cum_t = cum[:, :FOX_HEADS].T
    o_fox, fox_n, lse, gw_mlp_in = _fox_fwd(proj, cum_t, g_fox, name="fox_fwd", side=_gather_side([own_w_mlp_in]))
    gw_mlp_in, = _finish_gather([gw_mlp_in], [own_w_mlp_in], chip, name="gather_w_mlp_in_d2d")
    o_gla, gla_n, states = _gla_fwd(proj, log_a, g_gla, name="gla_fwd")
    mixed = jnp.concatenate([fox_n, gla_n], axis=1)
    y1, = _mm(mixed, w_out_full, mode="nn", out_dtypes=[F32], name="out_proj")
    x1, h2 = _post_pre_fwd(x, y1, gate_m, g_post_mix, a2, shift_f, name="post_mix_pre_mlp_fwd")

    def mlp_act(acc):
        r = jnp.maximum(acc, 0.0)
        return acc, r * r

    rest_side = _Side([own_w_mlp_out, gw_mlp_out], [full_shape(own_w_mlp_out)],
                      lambda ins, outs: [cp for q in (1, 2, 3) for cp in _plan_gather_ici(ins[0], outs[0], part=(q, 4))],
                      9, aliases={1: 0})
    u, act, gw_mlp_out = _mm(h2, gw_mlp_in, mode="nn", out_dtypes=[BF16, BF16], epi=mlp_act, name="mlp_in",
                             b_slots=4, tm=MM_TM, side=rest_side)
    gw_mlp_out, = _finish_gather([gw_mlp_out], [own_w_mlp_out], chip, name="gather_w_mlp_out_d2d")
    w_mlp_out_full = gw_mlp_out.reshape(-1, d)
    y2, = _mm(act, w_mlp_out_full, mode="nn", out_dtypes=[F32], name="mlp_out")
    dx2, dy2, loss_part, dgate_f, dg_post_mlp = _post_loss_bwd(x1, y2, gate_f, g_post_mlp, target,
                                                               name="post_mlp_loss_bwd")
    dw_mlp_out, = _mm(act, dy2, mode="tn", out_dtypes=[BF16], name="dw_mlp_out")
    dw_mlp_out = dw_mlp_out.reshape(4, D_FF // 4, d)

    def act_bwd(acc, uv):
        return (acc * (2.0 * jnp.maximum(uv.astype(F32), 0.0)),)

    du, got_mlp_out = _mm(dy2, w_mlp_out_full, mode="nt", out_dtypes=[BF16], extras=[u], epi=act_bwd,
                          name="d_mlp_hidden", tm=MM_TM, side=_pair_side(dw_mlp_out))
    sum_mlp_out = _pair_sum(dw_mlp_out, got_mlp_out, idx, name="grad_pair_sum_mlp_out")
    nj = D_FF // 4 // min(MM_T, D_FF // 4)
    tmw = min(MM_T, d)
    dw_mlp_in, parts_mlp_out = _mm(
        h2, du, mode="tn", out_dtypes=[BF16], name="dw_mlp_in",
        out_shapes=[jax.ShapeDtypeStruct((4, d, D_FF // 4), BF16)],
        out_specs=[pl.BlockSpec((1, tmw, min(MM_T, D_FF // 4)), lambda i, j, kk: (j // nj, i, j % nj))],
        side=_Side([sum_mlp_out], [_parts_shape(sum_mlp_out)],
                   lambda ins, outs: _plan_shard_ici(ins[0], outs[0], piece=(0, 2)), 3))
    dh2, got_mlp_in, parts_mlp_out = _mm(
        du, gw_mlp_in, mode="nt", out_dtypes=[F32], name="d_mlp_in", b_slots=4,
        side=_Side([dw_mlp_in, sum_mlp_out, parts_mlp_out], [_got_shape(dw_mlp_in), _parts_shape(sum_mlp_out)],
                   lambda ins, outs: _plan_pair(ins[0], outs[0]) + _plan_shard_ici(ins[1], outs[1], piece=(1, 2)),
                   4, aliases={2: 1}))
    sum_mlp_in = _pair_sum(dw_mlp_in, got_mlp_in, idx, name="grad_pair_sum_mlp_in")
    dx1, dshift_f, da2, dy1, dgate_m, dg_post_mix = _pre_post_bwd(dh2, x1, dx2, a2, y1, gate_m, g_post_mix,
                                                                  name="pre_mlp_post_mix_bwd")
    dw_out, = _mm(mixed, dy1, mode="tn", out_dtypes=[BF16], name="dw_out")
    dw_out = dw_out.reshape(4, d // 4, d)
    dmixed, got_out = _mm(dy1, w_out_full, mode="nt", out_dtypes=[BF16], name="d_mixed", side=_pair_side(dw_out))
    sum_out = _pair_sum(dw_out, got_out, idx, name="grad_pair_sum_out")
    do_fox, delta, dg_fox = _head_norm_bwd(dmixed, o_fox, g_fox, None, nh=FOX_HEADS, hd=FOX_HD, dn_col=0,
                                           gr_col=0, name="fox_norm_bwd")
    do_gla, dgr, _, dg_gla = _head_norm_bwd(dmixed, o_gla, g_gla, proj, nh=GLA_HEADS, hd=GLA_DV, dn_col=1,
                                            gr_col=(3 * FOX_W + 2 * GLA_KW + GLA_W) // GLA_W, name="gla_norm_bwd")
    dq_fox, dk_fox, dv_fox, dcq, dck_t, parts_mlp_in, parts_out = _fox_bwd(
        proj, do_fox, cum_t, lse, delta, name="fox_bwd",
        side=_Side([sum_mlp_in, sum_out], [_parts_shape(sum_mlp_in), _parts_shape(sum_out)],
                   lambda ins, outs: _plan_shard_ici(ins[0], outs[0]) + _plan_shard_ici(ins[1], outs[1]), 6))
    dgq, dgk, dgv, dla = _gla_bwd(proj, log_a, do_gla, states, name="gla_bwd")
    dck = dcq + jnp.concatenate([dck_t.T, jnp.zeros((s, SMALL_W - FOX_HEADS), F32)], axis=1)
    dps, dbf, dw2p, db2 = _gates_bwd(dck, ps, bf, w2p, b_gla_a2, dla, name="gates_bwd")
    dproj = jnp.concatenate([dq_fox.astype(BF16), dk_fox, dv_fox, dgq, dgk, dgv, dgr], axis=1)
    dw_main, = _mm(dproj, h1, mode="tn", out_dtypes=[BF16], name="dw_in_main")
    dw_small, = _mm(dps, h1, mode="tn", out_dtypes=[BF16], name="dw_in_small")
    rs_in = IN_SLOT_ROWS
    sum_in = _chip_sum(_stack_dw_in(dw_main, dw_small), idx, "in")
    dh1_small, = _mm(dps, w_small, mode="nn", out_dtypes=[F32], name="d_h1_small")
    dh1, parts_in = _mm(dproj, w_main, mode="nn", out_dtypes=[F32], extras=[dh1_small],
                        epi=lambda acc, e: (acc + e,), name="d_h1", side=_shard_side(sum_in))
    grad_x, dshift_m, da1 = _pre_bwd(dh1, x, dx1, a1, name="pre_mix_bwd")
    bufs = [_final_sum(sm, pt, idx, shp, name=f"grad_final_sum_{tag}")
            for tag, sm, pt, shp in [("in", sum_in, parts_in, (rs_in, d)), ("out", sum_out, parts_out, (d // 4, d)),
                                     ("mlp_in", sum_mlp_in, parts_mlp_in, (d, D_FF // 4)),
                                     ("mlp_out", sum_mlp_out, parts_mlp_out, (D_FF // 4, d))]]

    dmod = jnp.concatenate([dshift_m, da1 * g_pre_mix, dgate_m, dshift_f, da2 * g_pre_mlp, dgate_f], axis=1)
    small = dict(
        dmod=dmod, g_pre_mix=da1 * (1.0 + scale_m), g_post_mix=dg_post_mix, g_pre_mlp=da2 * (1.0 + scale_f),
        g_post_mlp=dg_post_mlp, b_fgate=dbf[:, :FOX_HEADS], w_gla_a2=dw2p[FOX_HEADS:FOX_HEADS + GLA_RANK],
        b_gla_a2=db2, g_fox_out=dg_fox, g_gla_out=dg_gla)
    return loss_part, grad_x, bufs, small


def _pack(arrays):
    flat = jnp.concatenate([a.reshape(-1).astype(F32) for a in arrays])
    n = flat.shape[0]
    rows = -(-n // 128)
    rows = -(-rows // 8) * 8
    return jnp.pad(flat, (0, rows * 128 - n)).reshape(rows, 128)


def _unpack(buf, shapes):
    flat = buf.reshape(-1)
    out, off = [], 0
    for shp in shapes:
        n = 1
        for q in shp:
            n *= q
        out.append(flat[off:off + n].reshape(shp))
        off += n
    return out


SMALL_GRAD_ORDER = ["dmod", "g_pre_mix", "g_post_mix", "g_pre_mlp", "g_post_mlp", "b_fgate", "w_gla_a2", "b_gla_a2",
                    "g_fox_out", "g_gla_out"]


def kernel(x, c, w_ada, b_ada, g_pre_mix, g_post_mix, w_in, b_fgate, w_gla_a2, b_gla_a2, g_fox_out, g_gla_out, w_out, g_pre_mlp, g_post_mlp, w_mlp_in, w_mlp_out, loss_target, m_w_ada, m_b_ada, m_g_pre_mix, m_g_post_mix, m_w_in, m_b_fgate, m_w_gla_a2, m_b_gla_a2, m_g_fox_out, m_g_gla_out, m_w_out, m_g_pre_mlp, m_g_post_mlp, m_w_mlp_in, m_w_mlp_out, v_w_ada, v_b_ada, v_g_pre_mix, v_g_post_mix, v_w_in, v_b_fgate, v_w_gla_a2, v_b_gla_a2, v_g_fox_out, v_g_gla_out, v_w_out, v_g_pre_mlp, v_g_post_mlp, v_w_mlp_in, v_w_mlp_out):
    ix, iy, ic = lax.axis_index("x"), lax.axis_index("y"), lax.axis_index("c")
    chip = 2 * ix + iy
    dev = 4 * ix + 2 * iy + ic
    d = D_MODEL

    c_act = _silu_rows(c, name="silu_c")
    pack1 = _pack([c_act, w_gla_a2[0], g_gla_out[0]])
    rows1 = pack1.shape[0]
    got1 = _gather8(pack1, name="gather_small_fwd").reshape(8, rows1, 128)
    per_dev = [_unpack(got1[q], [(d,), (GLA_RANK, GLA_KW // 4), (GLA_HEADS, GLA_DV // 4)]) for q in range(8)]
    c_all = jnp.stack([p[0] for p in per_dev])
    w_gla_a2_full = jnp.concatenate([per_dev[2 * j][1] for j in range(4)], axis=1)
    g_gla_full = jnp.concatenate([per_dev[2 * j][2] for j in range(4)], axis=1)
    cols = w_ada.shape[2]
    b_ada_shard = lax.dynamic_slice_in_dim(b_ada, chip * cols, cols, axis=1)
    mod_sh = _mod_shard(c_all, w_ada[0], b_ada_shard, name="ada_mod")
    got2 = _gather8(mod_sh, name="gather_mod").reshape(8, 8, cols)
    mod_all = jnp.concatenate([got2[2 * j] for j in range(4)], axis=1)
    mod = lax.dynamic_slice_in_dim(mod_all, dev, 1, axis=0)

    tr_in = lambda a: jnp.transpose(a[0])
    own_in = jnp.pad(tr_in(w_in).astype(BF16), ((0, IN_SLOT_ROWS - IN_SHARD_ROWS), (0, 0)))
    own_bf = [own_in, w_out[0].astype(BF16), w_mlp_in[0].astype(BF16), w_mlp_out[0].astype(BF16)]
    gw_in, = _finish_gather(_gather_weights(own_bf[:1], name="gather_w_in_ici"), own_bf[:1], chip,
                            name="gather_w_in_d2d")
    w_in_t = gw_in.reshape(-1, d)
    idx = jnp.stack([ic, chip]).astype(jnp.int32)
    loss_part, grad_x, bufs, small = _local_step(
        x[0], loss_target[0], mod, g_pre_mix, g_post_mix, g_pre_mlp, g_post_mlp, w_in_t, b_fgate,
        w_gla_a2_full, b_gla_a2, g_fox_out[0], g_gla_full, own_bf[1], own_bf[2], own_bf[3], chip, idx)
    loss = lax.psum(loss_part[0, 0], ("x", "y", "c"))

    g_big = list(_half_exchange(bufs, name="grad_half_exchange"))
    g_big[0] = g_big[0][:IN_SHARD_ROWS]
    big_w = [(tr_in(w_in), tr_in(m_w_in), tr_in(v_w_in)), (w_out[0], m_w_out[0], v_w_out[0]),
             (w_mlp_in[0], m_w_mlp_in[0], v_w_mlp_in[0]), (w_mlp_out[0], m_w_mlp_out[0], v_w_mlp_out[0])]
    big_res = []
    for q, (g, (w, m, v)) in enumerate(zip(g_big, big_w)):
        res4 = (g,) + tuple(_adam(g, w, m, v, name=f"adam_big_{q}"))
        big_res.append(tuple((jnp.transpose(a) if q == 0 else a)[None] for a in res4))

    pack2 = _pack([small[k] for k in SMALL_GRAD_ORDER])
    rows2 = pack2.shape[0]
    got3 = _gather8(pack2, name="gather_small_grads").reshape(8, rows2, 128)
    dmod_all = got3[:, :6 * d // 128, :].reshape(8, 6 * d)
    sums = _stack_sum(got3, name="small_grad_sum")
    shapes = [(1, 6 * d), (1, d), (1, d), (1, d), (1, d), (1, FOX_HEADS), (1, GLA_RANK, GLA_KW), (1, GLA_KW),
              (1, FOX_HEADS, FOX_HD), (1, GLA_HEADS, GLA_DV)]
    sg = dict(zip(["b_ada"] + SMALL_GRAD_ORDER[1:], _unpack(sums, shapes)))
    sg["w_gla_a2"] = lax.dynamic_slice_in_dim(sg["w_gla_a2"], chip * (GLA_KW // 4), GLA_KW // 4, axis=2)
    sg["g_gla_out"] = lax.dynamic_slice_in_dim(sg["g_gla_out"], chip * (GLA_DV // 4), GLA_DV // 4, axis=2)
    small_names = ["b_ada", "g_pre_mix", "g_post_mix", "b_fgate", "w_gla_a2", "b_gla_a2", "g_fox_out", "g_gla_out",
                   "g_pre_mlp", "g_post_mlp"]
    small_w = dict(b_ada=(b_ada, m_b_ada, v_b_ada), g_pre_mix=(g_pre_mix, m_g_pre_mix, v_g_pre_mix),
                   g_post_mix=(g_post_mix, m_g_post_mix, v_g_post_mix), b_fgate=(b_fgate, m_b_fgate, v_b_fgate),
                   w_gla_a2=(w_gla_a2, m_w_gla_a2, v_w_gla_a2), b_gla_a2=(b_gla_a2, m_b_gla_a2, v_b_gla_a2),
                   g_fox_out=(g_fox_out, m_g_fox_out, v_g_fox_out), g_gla_out=(g_gla_out, m_g_gla_out, v_g_gla_out),
                   g_pre_mlp=(g_pre_mlp, m_g_pre_mlp, v_g_pre_mlp), g_post_mlp=(g_post_mlp, m_g_post_mlp, v_g_post_mlp))
    sshapes = [small_w[k][0].shape for k in small_names]
    pg = _pack([sg[k] for k in small_names])
    pw, pm, pv = [_pack([small_w[k][q] for k in small_names]) for q in range(3)]
    pd, pmn, pvn = _adam(pg, pw, pm, pv, name="adam_small")
    s_delta = dict(zip(small_names, _unpack(pd, sshapes)))
    s_m = dict(zip(small_names, _unpack(pmn, sshapes)))
    s_v = dict(zip(small_names, _unpack(pvn, sshapes)))

    dmod_cols = lax.dynamic_slice_in_dim(dmod_all, chip * cols, cols, axis=1)
    g_ada, d_ada, m_ada, v_ada = _ada_grad_adam(c_all.T, dmod_cols, w_ada[0], m_w_ada[0], v_w_ada[0], name="ada_grad_adam")

    order = ["w_ada", "b_ada", "g_pre_mix", "g_post_mix", "w_in", "b_fgate", "w_gla_a2", "b_gla_a2", "g_fox_out",
             "g_gla_out", "w_out", "g_pre_mlp", "g_post_mlp", "w_mlp_in", "w_mlp_out"]
    res = {"w_ada": (g_ada[None], d_ada[None], m_ada[None], v_ada[None]),
           "w_in": big_res[0], "w_out": big_res[1], "w_mlp_in": big_res[2], "w_mlp_out": big_res[3]}
    for k in small_names:
        res[k] = (sg[k], s_delta[k], s_m[k], s_v[k])
    return (loss, grad_x[None], *[res[k][0] for k in order], *[res[k][1] for k in order],
            *[res[k][2] for k in order], *[res[k][3] for k in order])
```

```python
import functools

import jax
import jax.numpy as jnp
from jax import lax
from jax.experimental import pallas as pl
from jax.experimental.pallas import tpu as pltpu

F32 = jnp.float32
BF16 = jnp.bfloat16
MESH = pl.DeviceIdType.MESH
HIGHEST = lax.Precision.HIGHEST

D_MODEL = 2048
FOX_HEADS = 8
FOX_HD = 128
FOX_W = FOX_HEADS * FOX_HD
GLA_HEADS = 4
GLA_DK = 128
GLA_DV = 256
GLA_KW = GLA_HEADS * GLA_DK
GLA_W = GLA_HEADS * GLA_DV
GLA_RANK = 16
GLA_TEMP = 16.0
CHUNK = 64
D_FF = 4 * D_MODEL
EPS = 1e-6
MAIN_W = 3 * FOX_W + 2 * GLA_KW + 2 * GLA_W
SMALL_W = 128
NEG = -1e30

ADAM_LR = 0.001
ADAM_B1 = 0.9
ADAM_B2 = 0.999
ADAM_EPS = 1e-08
ADAM_WD = 0.01
ADAM_STEP = 10

VMEM_LIMIT = 52 * 1024 * 1024
ROW_TILE = 256
FOX_TQ = 512
FOX_TK = 512
GLA_ROWS = 512
GATE_TS = 512
MM_T = 1024
MM_TK = 2048
MM_TM = 2048


def _cp(*sem):
    return pltpu.CompilerParams(dimension_semantics=sem, vmem_limit_bytes=VMEM_LIMIT)


def _dot_nn(a, b, precision=None):
    return jnp.dot(a, b, preferred_element_type=F32, precision=precision)


def _dot_nt(a, b, precision=None):
    return lax.dot_general(a, b, (((1,), (1,)), ((), ())), preferred_element_type=F32, precision=precision)


def _dot_tn(a, b, precision=None):
    return lax.dot_general(a, b, (((0,), (0,)), ((), ())), preferred_element_type=F32, precision=precision)


def _sigmoid(x):
    return 1.0 / (1.0 + jnp.exp(-x))


def _log_sigmoid(x):
    return jnp.minimum(x, 0.0) - jnp.log(1.0 + jnp.exp(-jnp.abs(x)))


class _Side:
    def __init__(self, inputs, out_shapes, plan_fn, n_copies, aliases=None):
        self.inputs, self.out_shapes, self.plan_fn, self.n_copies = list(inputs), list(out_shapes), plan_fn, n_copies
        self.aliases = dict(aliases or {})

    def scratch(self):
        return [pltpu.SemaphoreType.DMA((self.n_copies,)), pltpu.SemaphoreType.DMA((self.n_copies,))]

    def run(self, in_refs, out_refs, sems, first, last):
        @pl.when(first)
        def _():
            _plan_start(self.plan_fn(in_refs, out_refs), *sems)

        @pl.when(last)
        def _():
            _plan_wait(self.plan_fn(in_refs, out_refs), *sems)


def _mm(a, b, *, mode, out_dtypes, name, tm=None, tn=None, tk=None, extras=(), epi=None,
        out_shapes=None, out_specs=None, side=None, b_slots=0):
    tm, tn, tk = tm or MM_T, tn or MM_T, tk or MM_TK
    b2 = (b.shape[1], b_slots * b.shape[2]) if b_slots else b.shape
    if mode == "nn":
        (m, k), n = a.shape, b2[1]
    elif mode == "nt":
        (m, k), n = a.shape, b2[0]
    else:
        (k, m), n = a.shape, b2[1]
    tm, tn, tk = min(tm, m), min(tn, n), min(tk, k)
    if b_slots:
        tn = min(tn, b.shape[2]) if mode == "nn" else tn
        tk = min(tk, b.shape[2]) if mode == "nt" else tk
    assert m % tm == 0 and n % tn == 0 and k % tk == 0, (name, m, n, k)
    nk = k // tk
    n_out, n_ex = len(out_dtypes), len(extras)
    if epi is None:
        epi = lambda acc: tuple(acc for _ in range(n_out))
    dot = {"nn": _dot_nn, "nt": _dot_nt, "tn": _dot_tn}[mode]

    n_si = len(side.inputs) if side else 0
    n_so = len(side.out_shapes) if side else 0
    grid = (m // tm, n // tn, nk)

    def body(*refs):
        a_ref, b_ref = refs[0], refs[1]
        ex_refs = refs[2:2 + n_ex]
        base = 2 + n_ex + n_si
        o_refs = refs[base:base + n_out]
        scratch = refs[base + n_out + n_so:]
        if side:
            pos = [pl.program_id(q) for q in range(3)]
            first = (pos[0] == 0) & (pos[1] == 0) & (pos[2] == 0)
            last = (pos[0] == grid[0] - 1) & (pos[1] == grid[1] - 1) & (pos[2] == grid[2] - 1)
            side.run(refs[2 + n_ex:base], refs[base + n_out:base + n_out + n_so], scratch[-2:], first, last)
        part = dot(a_ref[...], b_ref[...])

        def finish(acc):
            outs = epi(acc, *[e[...] for e in ex_refs])
            for o_ref, val in zip(o_refs, outs):
                o_ref[...] = val.reshape(o_ref.shape).astype(o_ref.dtype)

        if nk == 1:
            finish(part)
        else:
            acc_ref = scratch[0]
            kk = pl.program_id(2)

            @pl.when(kk == 0)
            def _():
                acc_ref[...] = part

            @pl.when(kk > 0)
            def _():
                acc_ref[...] += part

            @pl.when(kk == nk - 1)
            def _():
                finish(acc_ref[...])

    if mode == "nn":
        a_spec = pl.BlockSpec((tm, tk), lambda i, j, kk: (i, kk))
        b_spec = pl.BlockSpec((tk, tn), lambda i, j, kk: (kk, j))
        if b_slots:
            per = b.shape[2] // tn
            b_spec = pl.BlockSpec((None, tk, tn), lambda i, j, kk: (j // per, kk, j % per))
    elif mode == "nt":
        a_spec = pl.BlockSpec((tm, tk), lambda i, j, kk: (i, kk))
        b_spec = pl.BlockSpec((tn, tk), lambda i, j, kk: (j, kk))
        if b_slots:
            per = b.shape[2] // tk
            b_spec = pl.BlockSpec((None, tn, tk), lambda i, j, kk: (kk // per, j, kk % per))
    else:
        assert not b_slots
        a_spec = pl.BlockSpec((tk, tm), lambda i, j, kk: (kk, i))
        b_spec = pl.BlockSpec((tk, tn), lambda i, j, kk: (kk, j))
    tile_spec = pl.BlockSpec((tm, tn), lambda i, j, kk: (i, j))
    if out_shapes is None:
        out_shapes = [jax.ShapeDtypeStruct((m, n), dt) for dt in out_dtypes]
    if out_specs is None:
        out_specs = [tile_spec for _ in out_dtypes]
    any_spec = pl.BlockSpec(memory_space=pl.ANY)
    res = pl.pallas_call(
        body,
        grid=grid,
        in_specs=[a_spec, b_spec] + [tile_spec for _ in extras] + [any_spec] * n_si,
        out_specs=list(out_specs) + [any_spec] * n_so,
        out_shape=list(out_shapes) + (side.out_shapes if side else []),
        scratch_shapes=([pltpu.VMEM((tm, tn), F32)] if nk > 1 else []) + (side.scratch() if side else []),
        compiler_params=_cp("arbitrary", "arbitrary", "arbitrary") if side else _cp("parallel", "parallel", "arbitrary"),
        input_output_aliases={2 + n_ex + si: n_out + so for si, so in side.aliases.items()} if side else {},
        name=name,
    )(a, b, *extras, *(side.inputs if side else []))
    return res


def _row_spec(ts, d):
    return pl.BlockSpec((ts, d), lambda i: (i, 0))


def _vec_spec(d):
    return pl.BlockSpec((1, d), lambda i: (0, 0))


def _pre_fwd(x, avec, shift, *, name):
    s, d = x.shape
    ts = min(ROW_TILE, s)

    def body(x_ref, a_ref, s_ref, h_ref):
        xv = x_ref[...]
        r = lax.rsqrt(jnp.mean(xv * xv, axis=-1, keepdims=True) + EPS)
        h_ref[...] = (xv * r * a_ref[...] + s_ref[...]).astype(BF16)

    return pl.pallas_call(
        body, grid=(s // ts,),
        in_specs=[_row_spec(ts, d), _vec_spec(d), _vec_spec(d)],
        out_specs=_row_spec(ts, d),
        out_shape=jax.ShapeDtypeStruct((s, d), BF16),
        compiler_params=_cp("parallel"), name=name,
    )(x, avec, shift)


def _post_pre_fwd(x, y, gate, g, avec, shift, *, name):
    s, d = x.shape
    ts = min(ROW_TILE, s)

    def body(x_ref, y_ref, gate_ref, g_ref, a_ref, s_ref, o_ref, h_ref):
        yv = y_ref[...]
        r = lax.rsqrt(jnp.mean(yv * yv, axis=-1, keepdims=True) + EPS)
        x1 = x_ref[...] + gate_ref[...] * (yv * r * g_ref[...])
        o_ref[...] = x1
        r1 = lax.rsqrt(jnp.mean(x1 * x1, axis=-1, keepdims=True) + EPS)
        h_ref[...] = (x1 * r1 * a_ref[...] + s_ref[...]).astype(BF16)

    return pl.pallas_call(
        body, grid=(s // ts,),
        in_specs=[_row_spec(ts, d), _row_spec(ts, d)] + [_vec_spec(d)] * 4,
        out_specs=[_row_spec(ts, d), _row_spec(ts, d)],
        out_shape=[jax.ShapeDtypeStruct((s, d), F32), jax.ShapeDtypeStruct((s, d), BF16)],
        compiler_params=_cp("parallel"), name=name,
    )(x, y, gate, g, avec, shift)


def _post_bwd_math(dxv, yv, gatev, gv):
    r = lax.rsqrt(jnp.mean(yv * yv, axis=-1, keepdims=True) + EPS)
    yhat = yv * r
    dn = dxv * gatev
    dyhat = dn * gv
    dy = r * (dyhat - yhat * jnp.mean(dyhat * yhat, axis=-1, keepdims=True))
    return dy, dxv * (yhat * gv), dn * yhat


def _accumulate(first, pairs):
    @pl.when(first)
    def _():
        for ref, _ in pairs:
            ref[...] = jnp.zeros_like(ref)

    for ref, val in pairs:
        ref[...] += jnp.sum(val, axis=0, keepdims=True)


def _post_loss_bwd(x, y, gate, g, target, *, name):
    s, d = x.shape
    ts = min(ROW_TILE, s)

    def body(x_ref, y_ref, gate_ref, g_ref, t_ref, dx_ref, dy_ref, loss_ref, dgate_ref, dg_ref):
        yv, gatev, gv = y_ref[...], gate_ref[...], g_ref[...]
        r = lax.rsqrt(jnp.mean(yv * yv, axis=-1, keepdims=True) + EPS)
        diff = x_ref[...] + gatev * (yv * r * gv) - t_ref[...]
        dxv = diff * (1.0 / d)
        dx_ref[...] = dxv
        dy, dgate_rows, dg_rows = _post_bwd_math(dxv, yv, gatev, gv)
        dy_ref[...] = dy.astype(BF16)
        first = pl.program_id(0) == 0
        _accumulate(first, [(dgate_ref, dgate_rows), (dg_ref, dg_rows)])

        @pl.when(first)
        def _():
            loss_ref[...] = jnp.zeros_like(loss_ref)

        loss_ref[...] += jnp.sum(jnp.mean(diff * diff, axis=-1, keepdims=True)) * 0.5

    return pl.pallas_call(
        body, grid=(s // ts,),
        in_specs=[_row_spec(ts, d), _row_spec(ts, d), _vec_spec(d), _vec_spec(d), _row_spec(ts, d)],
        out_specs=[_row_spec(ts, d), _row_spec(ts, d), pl.BlockSpec((1, 128), lambda i: (0, 0)), _vec_spec(d),
                   _vec_spec(d)],
        out_shape=[jax.ShapeDtypeStruct((s, d), F32), jax.ShapeDtypeStruct((s, d), BF16),
                   jax.ShapeDtypeStruct((1, 128), F32), jax.ShapeDtypeStruct((1, d), F32),
                   jax.ShapeDtypeStruct((1, d), F32)],
        compiler_params=_cp("arbitrary"), name=name,
    )(x, y, gate, g, target)


def _pre_post_bwd(dh, xin, dres, avec, y, gate, g, *, name):
    s, d = xin.shape
    ts = min(ROW_TILE, s)

    def body(dh_ref, x_ref, dres_ref, a_ref, y_ref, gate_ref, g_ref, dx_ref, dshift_ref, da_ref, dy_ref,
             dgate_ref, dg_ref):
        xv, dhv = x_ref[...], dh_ref[...]
        r = lax.rsqrt(jnp.mean(xv * xv, axis=-1, keepdims=True) + EPS)
        xhat = xv * r
        dxhat = dhv * a_ref[...]
        dxv = dres_ref[...] + r * (dxhat - xhat * jnp.mean(dxhat * xhat, axis=-1, keepdims=True))
        dx_ref[...] = dxv
        dy, dgate_rows, dg_rows = _post_bwd_math(dxv, y_ref[...], gate_ref[...], g_ref[...])
        dy_ref[...] = dy.astype(BF16)
        _accumulate(pl.program_id(0) == 0, [(dshift_ref, dhv), (da_ref, dhv * xhat), (dgate_ref, dgate_rows),
                                            (dg_ref, dg_rows)])

    return pl.pallas_call(
        body, grid=(s // ts,),
        in_specs=[_row_spec(ts, d), _row_spec(ts, d), _row_spec(ts, d), _vec_spec(d), _row_spec(ts, d),
                  _vec_spec(d), _vec_spec(d)],
        out_specs=[_row_spec(ts, d), _vec_spec(d), _vec_spec(d), _row_spec(ts, d), _vec_spec(d), _vec_spec(d)],
        out_shape=[jax.ShapeDtypeStruct((s, d), F32), jax.ShapeDtypeStruct((1, d), F32),
                   jax.ShapeDtypeStruct((1, d), F32), jax.ShapeDtypeStruct((s, d), BF16),
                   jax.ShapeDtypeStruct((1, d), F32), jax.ShapeDtypeStruct((1, d), F32)],
        compiler_params=_cp("arbitrary"), name=name,
    )(dh, xin, dres, avec, y, gate, g)


def _pre_bwd(dh, xin, dres, avec, *, name):
    s, d = xin.shape
    ts = min(ROW_TILE, s)

    def body(dh_ref, x_ref, dres_ref, a_ref, dx_ref, dshift_ref, da_ref):
        xv, dhv = x_ref[...], dh_ref[...]
        r = lax.rsqrt(jnp.mean(xv * xv, axis=-1, keepdims=True) + EPS)
        xhat = xv * r
        dxhat = dhv * a_ref[...]
        dx_ref[...] = dres_ref[...] + r * (dxhat - xhat * jnp.mean(dxhat * xhat, axis=-1, keepdims=True))

        @pl.when(pl.program_id(0) == 0)
        def _():
            dshift_ref[...] = jnp.zeros_like(dshift_ref)
            da_ref[...] = jnp.zeros_like(da_ref)

        dshift_ref[...] += jnp.sum(dhv, axis=0, keepdims=True)
        da_ref[...] += jnp.sum(dhv * xhat, axis=0, keepdims=True)

    return pl.pallas_call(
        body, grid=(s // ts,),
        in_specs=[_row_spec(ts, d), _row_spec(ts, d), _row_spec(ts, d), _vec_spec(d)],
        out_specs=[_row_spec(ts, d), _vec_spec(d), _vec_spec(d)],
        out_shape=[jax.ShapeDtypeStruct((s, d), F32), jax.ShapeDtypeStruct((1, d), F32),
                   jax.ShapeDtypeStruct((1, d), F32)],
        compiler_params=_cp("arbitrary"), name=name,
    )(dh, xin, dres, avec)


def _tri(n, strict=False, upper=False):
    r = lax.broadcasted_iota(jnp.int32, (n, n), 0)
    c = lax.broadcasted_iota(jnp.int32, (n, n), 1)
    if upper:
        r, c = c, r
    return ((r > c) if strict else (r >= c)).astype(F32)


def _gates_fwd(ps, bf, w2p, b2, *, name):
    s = ps.shape[0]
    ts = min(GATE_TS, s)

    def body(ps_ref, bf_ref, w_ref, b2_ref, cum_ref, la_ref, carry_ref):
        @pl.when(pl.program_id(0) == 0)
        def _():
            carry_ref[...] = jnp.zeros_like(carry_ref)

        psv = ps_ref[...]
        lf = _log_sigmoid(psv + bf_ref[...])
        cum = _dot_nn(_tri(ts), lf, HIGHEST) + carry_ref[...]
        cum_ref[...] = cum
        carry_ref[...] = cum[ts - 1:ts, :]
        z = _dot_nn(psv, w_ref[...], HIGHEST) + b2_ref[...]
        la_ref[...] = _log_sigmoid(z) * (1.0 / GLA_TEMP)

    return pl.pallas_call(
        body, grid=(s // ts,),
        in_specs=[_row_spec(ts, SMALL_W), _vec_spec(SMALL_W),
                  pl.BlockSpec((SMALL_W, GLA_KW), lambda i: (0, 0)), _vec_spec(GLA_KW)],
        out_specs=[_row_spec(ts, SMALL_W), _row_spec(ts, GLA_KW)],
        out_shape=[jax.ShapeDtypeStruct((s, SMALL_W), F32), jax.ShapeDtypeStruct((s, GLA_KW), F32)],
        scratch_shapes=[pltpu.VMEM((1, SMALL_W), F32)],
        compiler_params=_cp("arbitrary"), name=name,
    )(ps, bf, w2p, b2)


def _gates_bwd(dck, ps, bf, w2p, b2, dla, *, name):
    s = ps.shape[0]
    ts = min(GATE_TS, s)
    nb = s // ts
    rev = lambda i: (nb - 1 - i, 0)

    def body(dck_ref, ps_ref, bf_ref, w_ref, b2_ref, dla_ref, dps_ref, dbf_ref, dw_ref, db2_ref, carry_ref):
        @pl.when(pl.program_id(0) == 0)
        def _():
            carry_ref[...] = jnp.zeros_like(carry_ref)
            dbf_ref[...] = jnp.zeros_like(dbf_ref)
            dw_ref[...] = jnp.zeros_like(dw_ref)
            db2_ref[...] = jnp.zeros_like(db2_ref)

        psv, dckv = ps_ref[...], dck_ref[...]
        dlf = _dot_nn(_tri(ts, upper=True), dckv, HIGHEST) + carry_ref[...]
        carry_ref[...] += jnp.sum(dckv, axis=0, keepdims=True)
        lane = lax.broadcasted_iota(jnp.int32, (ts, SMALL_W), 1)
        dff = jnp.where(lane < FOX_HEADS, dlf * _sigmoid(-(psv + bf_ref[...])), 0.0)
        z = _dot_nn(psv, w_ref[...], HIGHEST) + b2_ref[...]
        dz = dla_ref[...] * _sigmoid(-z) * (1.0 / GLA_TEMP)
        dps_ref[...] = (_dot_nt(dz, w_ref[...], HIGHEST) + dff).astype(BF16)
        dbf_ref[...] += jnp.sum(dff, axis=0, keepdims=True)
        dw_ref[...] += _dot_tn(psv, dz, HIGHEST)
        db2_ref[...] += jnp.sum(dz, axis=0, keepdims=True)

    return pl.pallas_call(
        body, grid=(nb,),
        in_specs=[pl.BlockSpec((ts, SMALL_W), rev), pl.BlockSpec((ts, SMALL_W), rev), _vec_spec(SMALL_W),
                  pl.BlockSpec((SMALL_W, GLA_KW), lambda i: (0, 0)), _vec_spec(GLA_KW),
                  pl.BlockSpec((ts, GLA_KW), rev)],
        out_specs=[pl.BlockSpec((ts, SMALL_W), rev), _vec_spec(SMALL_W),
                   pl.BlockSpec((SMALL_W, GLA_KW), lambda i: (0, 0)), _vec_spec(GLA_KW)],
        out_shape=[jax.ShapeDtypeStruct((s, SMALL_W), BF16), jax.ShapeDtypeStruct((1, SMALL_W), F32),
                   jax.ShapeDtypeStruct((SMALL_W, GLA_KW), F32), jax.ShapeDtypeStruct((1, GLA_KW), F32)],
        scratch_shapes=[pltpu.VMEM((1, SMALL_W), F32)],
        compiler_params=_cp("arbitrary"), name=name,
    )(dck, ps, bf, w2p, b2, dla)


def _hs(h, hd=FOX_HD):
    return slice(h * hd, (h + 1) * hd)


def _fox_fwd(proj, cum_t, g_fox, *, name, side=None):
    s = proj.shape[0]
    tq, tk = min(FOX_TQ, s), min(FOX_TK, s)
    scale = FOX_HD ** -0.5
    n_si = len(side.inputs) if side else 0
    n_so = len(side.out_shapes) if side else 0
    grid = (s // tq, s // tk)

    def body(*refs):
        q_ref, k_ref, v_ref, ck_ref, g_ref = refs[:5]
        o_ref, n_ref, lse_ref = refs[5 + n_si:8 + n_si]
        m_sc, acc_sc = refs[8 + n_si + n_so:10 + n_si + n_so]
        i, j = pl.program_id(0), pl.program_id(1)
        if side:
            side.run(refs[5:5 + n_si], refs[8 + n_si:8 + n_si + n_so], refs[10 + n_si + n_so:],
                     (i == 0) & (j == 0), (i == grid[0] - 1) & (j == grid[1] - 1))

        @pl.when(j == 0)
        def _():
            m_sc[...] = jnp.full_like(m_sc, NEG)
            acc_sc[...] = jnp.zeros_like(acc_sc)

        def block(masked):
            mask = _causal_mask(i, j, tq, tk) if masked else None
            ones = jnp.ones((tk, FOX_HD), BF16)
            for h in range(FOX_HEADS):
                sc = _fox_logits(_dot_nt(q_ref[:, _hs(h)], k_ref[:, _hs(h)]), ck_ref[h:h + 1, :], mask, scale)
                m_prev = m_sc[h]
                m_new = jnp.maximum(m_prev, jnp.max(sc, axis=-1, keepdims=True))
                alpha = jnp.exp(m_prev - m_new)
                p = jnp.exp(sc - m_new).astype(BF16)
                v_one = jnp.concatenate([v_ref[:, _hs(h)], ones], axis=1)
                acc_sc[:, _hs(h, 2 * FOX_HD)] = alpha * acc_sc[:, _hs(h, 2 * FOX_HD)] + _dot_nn(p, v_one)
                m_sc[h] = m_new

        pl.when(j < i)(functools.partial(block, False))

        @pl.when(j == i)
        def _():
            block(True)
            lane = lax.broadcasted_iota(jnp.int32, (tq, 128), 1)
            lse = jnp.zeros((tq, 128), F32)
            for h in range(FOX_HEADS):
                l_rep = acc_sc[:, 2 * h * FOX_HD + FOX_HD:2 * (h + 1) * FOX_HD]
                o = acc_sc[:, 2 * h * FOX_HD:2 * h * FOX_HD + FOX_HD] / l_rep
                o_ref[:, _hs(h)] = o
                r = lax.rsqrt(jnp.mean(o * o, axis=-1, keepdims=True) + EPS)
                n_ref[:, _hs(h)] = (o * r * g_ref[h:h + 1, :]).astype(BF16)
                lse = jnp.where(lane == h, m_sc[h] + jnp.log(l_rep), lse)
            lse_ref[...] = lse

    kv = lambda col: (lambda i, j: (jnp.minimum(j, i), col))
    any_spec = pl.BlockSpec(memory_space=pl.ANY)
    return pl.pallas_call(
        body, grid=grid,
        in_specs=[pl.BlockSpec((tq, FOX_W), lambda i, j: (i, 0)),
                  pl.BlockSpec((tk, FOX_W), kv(1)),
                  pl.BlockSpec((tk, FOX_W), kv(2)),
                  pl.BlockSpec((FOX_HEADS, tk), lambda i, j: (0, jnp.minimum(j, i))),
                  pl.BlockSpec((FOX_HEADS, FOX_HD), lambda i, j: (0, 0))] + [any_spec] * n_si,
        out_specs=[pl.BlockSpec((tq, FOX_W), lambda i, j: (i, 0)),
                   pl.BlockSpec((tq, FOX_W), lambda i, j: (i, 0)),
                   pl.BlockSpec((tq, 128), lambda i, j: (i, 0))] + [any_spec] * n_so,
        out_shape=[jax.ShapeDtypeStruct((s, FOX_W), F32), jax.ShapeDtypeStruct((s, FOX_W), BF16),
                   jax.ShapeDtypeStruct((s, 128), F32)] + (side.out_shapes if side else []),
        scratch_shapes=[pltpu.VMEM((FOX_HEADS, tq, 1), F32), pltpu.VMEM((tq, 2 * FOX_W), F32)]
        + (side.scratch() if side else []),
        compiler_params=_cp("arbitrary", "arbitrary"), name=name,
    )(proj, proj, proj, cum_t, g_fox, *(side.inputs if side else []))


def _causal_mask(i, j, tq, tk):
    rows = i * tq + lax.broadcasted_iota(jnp.int32, (tq, tk), 0)
    cols = j * tk + lax.broadcasted_iota(jnp.int32, (tq, tk), 1)
    return rows >= cols


def _fox_logits(qk, ck, mask, scale):
    sc = qk * scale - ck
    return sc if mask is None else jnp.where(mask, sc, NEG)


def _fox_bwd(proj, do, cum_t, lse, delta, *, name, side=None):
    s = proj.shape[0]
    tq, tk = min(FOX_TQ, s), min(FOX_TK, s)
    nk, nq = s // tk, s // tq
    scale = FOX_HD ** -0.5
    n_si = len(side.inputs) if side else 0
    n_so = len(side.out_shapes) if side else 0

    def body(*refs):
        q_ref, k_ref, v_ref, do_ref, ck_ref, lse_ref, dl_ref = refs[:7]
        dq_hbm, dk_ref, dv_ref, dcq_hbm, dck_ref = refs[7 + n_si:12 + n_si]
        dq_sc, dcq_sc, dk_sc, dv_sc, dck_sc, out_sems = refs[12 + n_si + n_so:18 + n_si + n_so]
        j, i = pl.program_id(0), pl.program_id(1)
        if side:
            side.run(refs[7:7 + n_si], refs[12 + n_si:12 + n_si + n_so], refs[18 + n_si + n_so:],
                     (j == 0) & (i == 0), (j == nk - 1) & (i == nq - 1))

        @pl.when((j == 0) & (i == 0))
        def _():
            dq_sc[...] = jnp.zeros_like(dq_sc)
            dcq_sc[...] = jnp.zeros_like(dcq_sc)

        @pl.when(i == 0)
        def _():
            dk_sc[...] = jnp.zeros_like(dk_sc)
            dv_sc[...] = jnp.zeros_like(dv_sc)
            dck_sc[...] = jnp.zeros_like(dck_sc)

        def block(masked):
            mask = _causal_mask(i, j, tq, tk) if masked else None
            qrows = pl.ds(pl.multiple_of(i * tq, tq), tq)
            for h in range(FOX_HEADS):
                sc = _fox_logits(_dot_nt(q_ref[:, _hs(h)], k_ref[:, _hs(h)]), ck_ref[h:h + 1, :], mask, scale)
                p = jnp.exp(sc - lse_ref[:, h:h + 1])
                ds = p * (_dot_nt(do_ref[:, _hs(h)], v_ref[:, _hs(h)]) - dl_ref[:, h:h + 1])
                dsb = ds.astype(BF16)
                dv_sc[:, _hs(h)] += _dot_tn(p.astype(BF16), do_ref[:, _hs(h)])
                dk_sc[:, _hs(h)] += _dot_tn(dsb, q_ref[:, _hs(h)])
                dq_sc[qrows, _hs(h)] += _dot_nn(dsb, k_ref[:, _hs(h)]) * scale
                dck_sc[h:h + 1, :] -= jnp.sum(ds, axis=0, keepdims=True)
                dcq_sc[qrows, h:h + 1] += jnp.sum(ds, axis=-1, keepdims=True)

        pl.when(i > j)(functools.partial(block, False))
        pl.when(i == j)(functools.partial(block, True))

        @pl.when(i == nq - 1)
        def _():
            dk_ref[...] = (dk_sc[...] * scale).astype(BF16)
            dv_ref[...] = dv_sc[...].astype(BF16)
            dck_ref[...] = dck_sc[...]

        @pl.when((j == nk - 1) & (i == nq - 1))
        def _():
            out_q = pltpu.make_async_copy(dq_sc, dq_hbm, out_sems.at[0])
            out_c = pltpu.make_async_copy(dcq_sc, dcq_hbm, out_sems.at[1])
            out_q.start()
            out_c.start()
            out_q.wait()
            out_c.wait()

    qrow = lambda j, i: (jnp.maximum(i, j), 0)
    krow = lambda col: (lambda j, i: (j, col))
    any_spec = pl.BlockSpec(memory_space=pl.ANY)
    return pl.pallas_call(
        body, grid=(nk, nq),
        in_specs=[pl.BlockSpec((tq, FOX_W), qrow), pl.BlockSpec((tk, FOX_W), krow(1)),
                  pl.BlockSpec((tk, FOX_W), krow(2)),
                  pl.BlockSpec((tq, FOX_W), qrow),
                  pl.BlockSpec((FOX_HEADS, tk), lambda j, i: (0, j)),
                  pl.BlockSpec((tq, 128), qrow), pl.BlockSpec((tq, 128), qrow)] + [any_spec] * n_si,
        out_specs=[any_spec, pl.BlockSpec((tk, FOX_W), lambda j, i: (j, 0)),
                   pl.BlockSpec((tk, FOX_W), lambda j, i: (j, 0)), any_spec,
                   pl.BlockSpec((FOX_HEADS, tk), lambda j, i: (0, j))] + [any_spec] * n_so,
        out_shape=[jax.ShapeDtypeStruct((s, FOX_W), F32), jax.ShapeDtypeStruct((s, FOX_W), BF16),
                   jax.ShapeDtypeStruct((s, FOX_W), BF16), jax.ShapeDtypeStruct((s, 128), F32),
                   jax.ShapeDtypeStruct((FOX_HEADS, s), F32)] + (side.out_shapes if side else []),
        scratch_shapes=[pltpu.VMEM((s, FOX_W), F32), pltpu.VMEM((s, 128), F32),
                        pltpu.VMEM((tk, FOX_W), F32), pltpu.VMEM((tk, FOX_W), F32), pltpu.VMEM((FOX_HEADS, tk), F32),
                        pltpu.SemaphoreType.DMA((2,))] + (side.scratch() if side else []),
        compiler_params=_cp("arbitrary", "arbitrary"), name=name,
    )(proj, proj, proj, do, cum_t, lse, delta, *(side.inputs if side else []))


def _head_norm_bwd(dn_in, o, g, gr_src, *, nh, hd, dn_col, gr_col, name):
    s, w = o.shape
    ts = min(ROW_TILE, s)
    gated = gr_src is not None

    def body(*refs):
        if gated:
            dn_ref, o_ref, g_ref, gr_ref, do_ref, dgr_ref, dl_ref, dg_ref = refs
        else:
            dn_ref, o_ref, g_ref, do_ref, dl_ref, dg_ref = refs

        @pl.when(pl.program_id(0) == 0)
        def _():
            dg_ref[...] = jnp.zeros_like(dg_ref)

        lane = lax.broadcasted_iota(jnp.int32, (ts, 128), 1)
        delta = jnp.zeros((ts, 128), F32)
        for h in range(nh):
            sl = _hs(h, hd)
            ov = o_ref[:, sl]
            dnv = dn_ref[:, sl].astype(F32)
            gv = g_ref[h:h + 1, :]
            r = lax.rsqrt(jnp.mean(ov * ov, axis=-1, keepdims=True) + EPS)
            ohat = ov * r
            if gated:
                grv = gr_ref[:, sl].astype(F32)
                sig = _sigmoid(grv)
                dgr_ref[:, sl] = (dnv * (ohat * gv) * (sig * (1.0 + grv * (1.0 - sig)))).astype(BF16)
                dnv = dnv * (grv * sig)
            dg_ref[h:h + 1, :] += jnp.sum(dnv * ohat, axis=0, keepdims=True)
            dohat = dnv * gv
            do = r * (dohat - ohat * jnp.mean(dohat * ohat, axis=-1, keepdims=True))
            do_ref[:, sl] = do.astype(BF16)
            delta = jnp.where(lane == h, jnp.sum(do.astype(BF16).astype(F32) * ov, axis=-1, keepdims=True), delta)
        dl_ref[...] = delta

    in_specs = [pl.BlockSpec((ts, w), lambda i: (i, dn_col)), _row_spec(ts, w),
                pl.BlockSpec((nh, hd), lambda i: (0, 0))]
    args = [dn_in, o, g]
    out_specs = [_row_spec(ts, w)]
    out_shape = [jax.ShapeDtypeStruct((s, w), BF16)]
    if gated:
        in_specs.append(pl.BlockSpec((ts, w), lambda i: (i, gr_col)))
        args.append(gr_src)
        out_specs.append(_row_spec(ts, w))
        out_shape.append(jax.ShapeDtypeStruct((s, w), BF16))
    out_specs += [_row_spec(ts, 128), pl.BlockSpec((nh, hd), lambda i: (0, 0))]
    out_shape += [jax.ShapeDtypeStruct((s, 128), F32), jax.ShapeDtypeStruct((nh, hd), F32)]
    return pl.pallas_call(
        body, grid=(s // ts,), in_specs=in_specs, out_specs=out_specs, out_shape=out_shape,
        compiler_params=_cp("arbitrary"), name=name,
    )(*args)


GQ_BLK = 3 * FOX_W // GLA_DK
GK_BLK = GQ_BLK + GLA_HEADS
GV_BLK = (3 * FOX_W + 2 * GLA_KW) // GLA_DV
GR_BLK = GV_BLK + GLA_HEADS


def _gla_chunk_terms(la):
    cum = _dot_nn(_tri(CHUNK), la, HIGHEST)
    total = cum[CHUNK - 1:CHUNK, :]
    return jnp.exp(total - cum), jnp.exp(total)


def _gla_fwd(proj, log_a, g_gla, *, name):
    s = proj.shape[0]
    rows = min(GLA_ROWS, s)
    cb = rows // CHUNK
    nblk = s // rows
    scale = GLA_DK ** -0.5

    def body(q_ref, k_ref, v_ref, gr_ref, la_ref, g_ref, o_ref, n_ref, st_ref, st_sc):
        h = pl.program_id(0)

        @pl.when(pl.program_id(1) == 0)
        def _():
            st_sc[...] = jnp.zeros_like(st_sc)

        gv = g_ref[pl.ds(h, 1), :]
        for ci in range(cb):
            sl = slice(ci * CHUNK, (ci + 1) * CHUNK)
            e, dec = _gla_chunk_terms(la_ref[sl, :])
            k_dec = (k_ref[sl, :].astype(F32) * e).astype(BF16)
            st = st_sc[...] * dec + _dot_tn(v_ref[sl, :], k_dec)
            st_sc[...] = st
            st_ref[0, ci] = st
            qs = (q_ref[sl, :].astype(F32) * scale).astype(BF16)
            o = _dot_nt(qs, st.astype(BF16))
            o_ref[sl, :] = o
            r = lax.rsqrt(jnp.mean(o * o, axis=-1, keepdims=True) + EPS)
            grv = gr_ref[sl, :].astype(F32)
            n_ref[sl, :] = (o * r * gv * (grv * _sigmoid(grv))).astype(BF16)

    return pl.pallas_call(
        body, grid=(GLA_HEADS, nblk),
        in_specs=[pl.BlockSpec((rows, GLA_DK), lambda h, n: (n, GQ_BLK + h)),
                  pl.BlockSpec((rows, GLA_DK), lambda h, n: (n, GK_BLK + h)),
                  pl.BlockSpec((rows, GLA_DV), lambda h, n: (n, GV_BLK + h)),
                  pl.BlockSpec((rows, GLA_DV), lambda h, n: (n, GR_BLK + h)),
                  pl.BlockSpec((rows, GLA_DK), lambda h, n: (n, h)),
                  pl.BlockSpec((GLA_HEADS, GLA_DV), lambda h, n: (0, 0))],
        out_specs=[pl.BlockSpec((rows, GLA_DV), lambda h, n: (n, h)),
                   pl.BlockSpec((rows, GLA_DV), lambda h, n: (n, h)),
                   pl.BlockSpec((1, cb, GLA_DV, GLA_DK), lambda h, n: (h, n, 0, 0))],
        out_shape=[jax.ShapeDtypeStruct((s, GLA_W), F32), jax.ShapeDtypeStruct((s, GLA_W), BF16),
                   jax.ShapeDtypeStruct((GLA_HEADS, s // CHUNK, GLA_DV, GLA_DK), F32)],
        scratch_shapes=[pltpu.VMEM((GLA_DV, GLA_DK), F32)],
        compiler_params=_cp("parallel", "arbitrary"), name=name,
    )(proj, proj, proj, proj, log_a, g_gla)


def _gla_bwd(proj, log_a, do, states, *, name, side=None):
    s = proj.shape[0]
    rows = min(GLA_ROWS, s)
    cb = rows // CHUNK
    nblk = s // rows
    scale = GLA_DK ** -0.5
    n_si = len(side.inputs) if side else 0
    n_so = len(side.out_shapes) if side else 0

    def body(*refs):
        q_ref, k_ref, v_ref, la_ref, do_ref, st_ref, prev_ref = refs[:7]
        dq_ref, dk_ref, dv_ref, dla_ref = refs[7 + n_si:11 + n_si]
        g_sc = refs[11 + n_si + n_so]
        nrev = pl.program_id(1)
        blk = nblk - 1 - nrev
        if side:
            hh = pl.program_id(0)
            side.run(refs[7:7 + n_si], refs[11 + n_si:11 + n_si + n_so], refs[12 + n_si + n_so:],
                     (hh == 0) & (nrev == 0), (hh == GLA_HEADS - 1) & (nrev == nblk - 1))

        @pl.when(nrev == 0)
        def _():
            g_sc[...] = jnp.zeros_like(g_sc)

        for ci in reversed(range(cb)):
            sl = slice(ci * CHUNK, (ci + 1) * CHUNK)
            e, dec = _gla_chunk_terms(la_ref[sl, :])
            kd = k_ref[sl, :].astype(F32) * e
            qs = (q_ref[sl, :].astype(F32) * scale).astype(BF16)
            dov = do_ref[sl, :]
            st = st_ref[0, ci]
            if ci > 0:
                st_prev = st_ref[0, ci - 1]
            else:
                st_prev = prev_ref[0, 0] * (blk > 0).astype(F32)
            dq_ref[sl, :] = (_dot_nn(dov, st.astype(BF16)) * scale).astype(BF16)
            gt = g_sc[...] + _dot_tn(dov, qs)
            gtb = gt.astype(BF16)
            dkd = _dot_nn(v_ref[sl, :], gtb)
            dv_ref[sl, :] = _dot_nt(kd.astype(BF16), gtb).astype(BF16)
            dk_ref[sl, :] = (dkd * e).astype(BF16)
            ddec = jnp.sum(gt * st_prev, axis=0, keepdims=True) * dec
            dla_ref[sl, :] = _dot_nn(_tri(CHUNK, strict=True), dkd * kd, HIGHEST) + ddec
            g_sc[...] = gt * dec

    rev = lambda col0: (lambda h, n: (nblk - 1 - n, col0 + h))
    return pl.pallas_call(
        body, grid=(GLA_HEADS, nblk),
        in_specs=[pl.BlockSpec((rows, GLA_DK), rev(GQ_BLK)),
                  pl.BlockSpec((rows, GLA_DK), rev(GK_BLK)),
                  pl.BlockSpec((rows, GLA_DV), rev(GV_BLK)),
                  pl.BlockSpec((rows, GLA_DK), rev(0)),
                  pl.BlockSpec((rows, GLA_DV), rev(0)),
                  pl.BlockSpec((1, cb, GLA_DV, GLA_DK), lambda h, n: (h, nblk - 1 - n, 0, 0)),
                  pl.BlockSpec((1, 1, GLA_DV, GLA_DK),
                               lambda h, n: (h, jnp.maximum((nblk - 1 - n) * cb - 1, 0), 0, 0))]
        + [pl.BlockSpec(memory_space=pl.ANY)] * n_si,
        out_specs=[pl.BlockSpec((rows, GLA_DK), rev(0)), pl.BlockSpec((rows, GLA_DK), rev(0)),
                   pl.BlockSpec((rows, GLA_DV), rev(0)), pl.BlockSpec((rows, GLA_DK), rev(0))]
        + [pl.BlockSpec(memory_space=pl.ANY)] * n_so,
        out_shape=[jax.ShapeDtypeStruct((s, GLA_KW), BF16), jax.ShapeDtypeStruct((s, GLA_KW), BF16),
                   jax.ShapeDtypeStruct((s, GLA_W), BF16), jax.ShapeDtypeStruct((s, GLA_KW), F32)]
        + (side.out_shapes if side else []),
        scratch_shapes=[pltpu.VMEM((GLA_DV, GLA_DK), F32)] + (side.scratch() if side else []),
        compiler_params=_cp("arbitrary", "arbitrary"), name=name,
    )(proj, proj, proj, log_a, do, states, states, *(side.inputs if side else []))


def _row_tile(r):
    tr = min(ROW_TILE, r)
    while r % tr or tr % 8:
        tr -= 1
    return tr


def _adamw_math(w, g, m, v):
    m = ADAM_B1 * m + (1.0 - ADAM_B1) * g
    v = ADAM_B2 * v + (1.0 - ADAM_B2) * (g * g)
    m_hat = m / (1.0 - ADAM_B1 ** ADAM_STEP)
    v_hat = v / (1.0 - ADAM_B2 ** ADAM_STEP)
    delta = -ADAM_LR * (m_hat / (jnp.sqrt(v_hat) + ADAM_EPS) + ADAM_WD * w)
    return delta, m, v


COL_TILE = 256


def _tile_2d(r, c):
    if r % 8 == 0 and _row_tile(r) >= 64:
        return _row_tile(r), c
    assert c % COL_TILE == 0, (r, c)
    return r, COL_TILE


def _half_shape(shape):
    r, c = shape[-2:]
    return tuple(shape[:-2]) + ((r // 2, c) if _half_axis(r) == 0 else (r, c // 2))


def _adam(g, w, m, v, *, name):
    r, c = w.shape
    tr, tc = _tile_2d(r, c)

    def body(g_ref, w_ref, m_ref, v_ref, d_ref, mo_ref, vo_ref):
        d, mn, vn = _adamw_math(w_ref[...], g_ref[...], m_ref[...], v_ref[...])
        d_ref[...] = d
        mo_ref[...] = mn
        vo_ref[...] = vn

    spec = pl.BlockSpec((tr, tc), lambda i, j: (i, j))
    return pl.pallas_call(
        body, grid=(r // tr, c // tc), in_specs=[spec] * 4, out_specs=[spec] * 3,
        out_shape=[jax.ShapeDtypeStruct((r, c), F32)] * 3,
        compiler_params=_cp("parallel", "parallel"), name=name,
    )(g, w, m, v)


def _ada_grad_adam(c_all_t, dmod_cols, w, m, v, *, name):
    r, c = w.shape
    tr, tc = min(512, r), min(1024, c)

    def body(ct_ref, dm_ref, w_ref, m_ref, v_ref, g_ref, d_ref, mo_ref, vo_ref):
        g = _dot_nn(ct_ref[...], dm_ref[...], HIGHEST)
        g_ref[...] = g
        d, mn, vn = _adamw_math(w_ref[...], g, m_ref[...], v_ref[...])
        d_ref[...] = d
        mo_ref[...] = mn
        vo_ref[...] = vn

    spec = pl.BlockSpec((tr, tc), lambda i, j: (i, j))
    nb = c_all_t.shape[1]
    return pl.pallas_call(
        body, grid=(r // tr, c // tc),
        in_specs=[pl.BlockSpec((tr, nb), lambda i, j: (i, 0)), pl.BlockSpec((nb, tc), lambda i, j: (0, j)),
                  spec, spec, spec],
        out_specs=[spec] * 4, out_shape=[jax.ShapeDtypeStruct((r, c), F32)] * 4,
        compiler_params=_cp("parallel", "parallel"), name=name,
    )(c_all_t, dmod_cols, w, m, v)


def _mod_shard(c_all, w, b, *, name):
    k, c = w.shape
    tc = min(512, c)
    nb = c_all.shape[0]

    def body(c_ref, w_ref, b_ref, o_ref):
        o_ref[...] = _dot_nn(c_ref[...], w_ref[...], HIGHEST) + b_ref[...]

    return pl.pallas_call(
        body, grid=(c // tc,),
        in_specs=[pl.BlockSpec((nb, k), lambda j: (0, 0)), pl.BlockSpec((k, tc), lambda j: (0, j)),
                  pl.BlockSpec((1, tc), lambda j: (0, j))],
        out_specs=pl.BlockSpec((nb, tc), lambda j: (0, j)),
        out_shape=jax.ShapeDtypeStruct((nb, c), F32),
        compiler_params=_cp("parallel"), name=name,
    )(c_all, w, b)


def _silu_rows(c, *, name):
    def body(c_ref, o_ref):
        cv = c_ref[...]
        o_ref[...] = cv * _sigmoid(cv)

    return pl.pallas_call(body, out_shape=jax.ShapeDtypeStruct(c.shape, F32), name=name)(c)


def _pair_sum(g, got, idx, *, name):
    p, r, c = g.shape
    ax = _half_axis(r)
    hr, hc = _half_shape((r, c))
    tr, tc = _tile_2d(hr, hc)
    nbr, nbc = hr // tr, hc // tc

    def body(idx_ref, a_ref, b_ref, o_ref):
        o_ref[...] = (a_ref[...].astype(F32) + b_ref[...].astype(F32)).astype(BF16)

    def own_map(i, j, k, idx_ref):
        return (i, j + (idx_ref[0] * nbr if ax == 0 else 0), k + (idx_ref[0] * nbc if ax == 1 else 0))

    half_spec = pl.BlockSpec((1, tr, tc), lambda i, j, k, idx_ref: (i, j, k))
    return pl.pallas_call(
        body,
        grid_spec=pltpu.PrefetchScalarGridSpec(
            num_scalar_prefetch=1, grid=(p, nbr, nbc),
            in_specs=[pl.BlockSpec((1, tr, tc), own_map), half_spec],
            out_specs=half_spec),
        out_shape=jax.ShapeDtypeStruct((p, hr, hc), BF16),
        compiler_params=_cp("parallel", "parallel", "parallel"), name=name,
    )(idx, g, got)


def _final_sum(own, parts, idx, shard_shape, *, name):
    ax = _half_axis(shard_shape[0])
    hr, hc = own.shape[1:]
    tr, tc = _tile_2d(hr, hc)
    nbr, nbc = hr // tr, hc // tc

    def body(idx_ref, own_ref, parts_ref, o_ref):
        acc = own_ref[0].astype(F32)
        for q in range(3):
            acc = acc + parts_ref[q].astype(F32)
        o_ref[...] = acc

    def out_map(j, k, idx_ref):
        return (j + (idx_ref[0] * nbr if ax == 0 else 0), k + (idx_ref[0] * nbc if ax == 1 else 0))

    return pl.pallas_call(
        body,
        grid_spec=pltpu.PrefetchScalarGridSpec(
            num_scalar_prefetch=1, grid=(nbr, nbc),
            in_specs=[pl.BlockSpec((1, tr, tc), lambda j, k, idx_ref: (idx_ref[1], j, k)),
                      pl.BlockSpec((3, tr, tc), lambda j, k, idx_ref: (0, j, k))],
            out_specs=pl.BlockSpec((tr, tc), out_map)),
        out_shape=jax.ShapeDtypeStruct(tuple(shard_shape), F32),
        compiler_params=_cp("parallel", "parallel"), name=name,
    )(idx, own, parts)


def _stack_sum(x, *, name):
    p, r, c = x.shape
    tr = _row_tile(r)

    def body(x_ref, o_ref):
        acc = x_ref[0].astype(F32)
        for q in range(1, p):
            acc = acc + x_ref[q].astype(F32)
        o_ref[...] = acc

    return pl.pallas_call(
        body, grid=(r // tr,),
        in_specs=[pl.BlockSpec((p, tr, c), lambda i: (0, i, 0))],
        out_specs=pl.BlockSpec((tr, c), lambda i: (i, 0)),
        out_shape=jax.ShapeDtypeStruct((r, c), F32),
        compiler_params=_cp("parallel"), name=name,
    )(x)


def _place():
    x, y, c = lax.axis_index("x"), lax.axis_index("y"), lax.axis_index("c")
    chips = [(1 - x, y), (x, 1 - y), (1 - x, 1 - y)]
    return x, y, c, chips


def _gather8(x_shard, *, name):
    m_per, n = x_shard.shape

    def body(x_ref, out_ref, send_sems, recv_sems, local_sem):
        x, y, c, chips = _place()
        me, sibling = (x, y, c), (x, y, 1 - c)

        def rows(px, py, pc):
            return out_ref.at[pl.ds((4 * px + 2 * py + pc) * m_per, m_per), :]

        def copy(k, block, to, src=None):
            return pltpu.make_async_remote_copy(
                src_ref=rows(*block) if src is None else src, dst_ref=rows(*block),
                send_sem=send_sems.at[k], recv_sem=recv_sems.at[k], device_id=to, device_id_type=MESH)

        mine = pltpu.make_async_copy(x_ref, rows(*me), local_sem)
        mine.start()
        first = [copy(0, me, sibling, src=x_ref)]
        first += [copy(1 + j, me, (*chip, c), src=x_ref) for j, chip in enumerate(chips)]
        for cp in first:
            cp.start()
        passed = [copy(4 + j, (*chip, c), sibling) for j, chip in enumerate(chips)]
        for j, chip in enumerate(chips):
            copy(1 + j, (*chip, c), me).wait_recv()
            passed[j].start()
        copy(0, sibling, me).wait_recv()
        for j, chip in enumerate(chips):
            copy(4 + j, (*chip, 1 - c), me).wait_recv()
        for cp in first + passed:
            cp.wait_send()
        mine.wait()

    return pl.pallas_call(
        body,
        out_shape=jax.ShapeDtypeStruct((8 * m_per, n), x_shard.dtype),
        in_specs=[pl.BlockSpec(memory_space=pltpu.VMEM)],
        out_specs=pl.BlockSpec(memory_space=pltpu.VMEM),
        scratch_shapes=[pltpu.SemaphoreType.DMA((7,)), pltpu.SemaphoreType.DMA((7,)), pltpu.SemaphoreType.DMA],
        name=name,
    )(x_shard)


def _gather_weights(shards, *, name):
    return _comm_call(lambda ins, outs: [cp for i, o in zip(ins, outs) for cp in _plan_gather_ici(i, o)],
                      shards, [jax.ShapeDtypeStruct((4,) + s.shape, s.dtype) for s in shards], name=name)


def _plan_start(plan, send_sems, recv_sems):
    for k, (src, dst, _, peer) in enumerate(plan):
        pltpu.make_async_remote_copy(src_ref=src, dst_ref=dst, send_sem=send_sems.at[k], recv_sem=recv_sems.at[k],
                                     device_id=peer, device_id_type=MESH).start()


def _plan_wait(plan, send_sems, recv_sems):
    for k, (src, _, land, peer) in enumerate(plan):
        pltpu.make_async_remote_copy(src_ref=src, dst_ref=land, send_sem=send_sems.at[k], recv_sem=recv_sems.at[k],
                                     device_id=peer, device_id_type=MESH).wait_recv()
    for k, (src, dst, _, peer) in enumerate(plan):
        pltpu.make_async_remote_copy(src_ref=src, dst_ref=dst, send_sem=send_sems.at[k], recv_sem=recv_sems.at[k],
                                     device_id=peer, device_id_type=MESH).wait_send()


def _half_axis(rows):
    return 0 if rows % 32 == 0 else 1


def _rows_half(ref, hc, axis, part=None):
    size = ref.shape[axis] // 2
    start = hc * size
    if part is not None:
        size //= part[1]
        start = start + part[0] * size
    idx = [slice(None)] * len(ref.shape)
    idx[axis] = pl.ds(start, size)
    return ref.at[tuple(idx)]


def _plan_gather_ici(shard, full, part=None):
    x, y, c, chips = _place()
    ax = _half_axis(shard.shape[0])
    src = _rows_half(shard, c, ax, part)
    return [(src, _rows_half(full.at[2 * x + y], c, ax, part), _rows_half(full.at[2 * cx + cy], c, ax, part),
             (cx, cy, c)) for cx, cy in chips]


def _plan_gather_d2d(full):
    x, y, c, chips = _place()
    ax = _half_axis(full.shape[1])
    plan = []
    for cx, cy in chips:
        slot = full.at[2 * cx + cy]
        plan.append((_rows_half(slot, c, ax), _rows_half(slot, c, ax), _rows_half(slot, 1 - c, ax), (x, y, 1 - c)))
    return plan


def _plan_pair(grad, got):
    x, y, c, _ = _place()
    return [(_rows_half(grad, 1 - c, 1 + _half_axis(grad.shape[1])), got, got, (x, y, 1 - c))]


def _plan_shard_ici(sums, parts, piece=None):
    _, _, c, chips = _place()

    def rows(ref):
        if piece is None:
            return ref
        size = ref.shape[0] // piece[1]
        return ref.at[pl.ds(piece[0] * size, size), :]

    return [(rows(sums.at[2 * cx + cy]), rows(parts.at[k]), rows(parts.at[k]), (cx, cy, c))
            for k, (cx, cy) in enumerate(chips)]


def _plan_half(buf):
    x, y, c, _ = _place()
    ax = _half_axis(buf.shape[0])
    mine = _rows_half(buf, c, ax)
    return [(mine, mine, _rows_half(buf, 1 - c, ax), (x, y, 1 - c))]


def _comm_call(plan_fn, inputs, out_shapes, *, name, aliases=None):
    ni, no = len(inputs), len(out_shapes)

    def body(*refs):
        plan = plan_fn(refs[:ni], refs[ni:ni + no])
        send_sems, recv_sems = refs[ni + no:]
        _plan_start(plan, send_sems, recv_sems)
        _plan_wait(plan, send_sems, recv_sems)

    any_spec = pl.BlockSpec(memory_space=pl.ANY)
    n_copies = 3 * max(ni, no)
    return pl.pallas_call(
        body, out_shape=list(out_shapes), in_specs=[any_spec] * ni, out_specs=[any_spec] * no,
        scratch_shapes=[pltpu.SemaphoreType.DMA((n_copies,)), pltpu.SemaphoreType.DMA((n_copies,))],
        input_output_aliases=aliases or {}, name=name,
    )(*inputs)


def _gather_forward(fulls, *, name):
    return _comm_call(lambda ins, outs: [cp for o in outs for cp in _plan_gather_d2d(o)],
                      fulls, [jax.ShapeDtypeStruct(f.shape, f.dtype) for f in fulls], name=name,
                      aliases={k: k for k in range(len(fulls))})


def _pair_exchange(grads, *, name):
    return _comm_call(lambda ins, outs: [cp for i, o in zip(ins, outs) for cp in _plan_pair(i, o)],
                      grads, [jax.ShapeDtypeStruct(_half_shape(g.shape), g.dtype) for g in grads], name=name)


def _half_exchange(bufs, *, name):
    return _comm_call(lambda ins, outs: [cp for o in outs for cp in _plan_half(o)],
                      bufs, [jax.ShapeDtypeStruct(b.shape, b.dtype) for b in bufs], name=name,
                      aliases={k: k for k in range(len(bufs))})


def _split_w_in(w_in_t):
    d = w_in_t.shape[1]
    main = jnp.concatenate([w_in_t[0:3072], w_in_t[3080:5128], w_in_t[5144:6168]], axis=0)
    small = jnp.concatenate([w_in_t[3072:3080], w_in_t[5128:5144], jnp.zeros((SMALL_W - 24, d), w_in_t.dtype)], axis=0)
    return main, small


def _merge_dw_in(dw_main, dw_small):
    return jnp.concatenate([dw_main[0:3072], dw_small[0:8], dw_main[3072:5120], dw_small[8:24], dw_main[5120:6144]],
                           axis=0)


def _gather_side(shards):
    return _Side(shards, [jax.ShapeDtypeStruct((4,) + w.shape, w.dtype) for w in shards],
                 lambda ins, outs: [cp for i, o in zip(ins, outs) for cp in _plan_gather_ici(i, o)], 3 * len(shards))


def _finish_gather(fulls, owns, chip, *, name):
    fulls = _gather_forward(list(fulls), name=name)
    return [lax.dynamic_update_index_in_dim(f, o, chip, 0) for f, o in zip(fulls, owns)]


def _half_side(bufs):
    return _Side(bufs, [jax.ShapeDtypeStruct(b.shape, b.dtype) for b in bufs],
                 lambda ins, outs: [cp for o in outs for cp in _plan_half(o)], len(bufs),
                 aliases={k: k for k in range(len(bufs))})


def _parts_shape(sums):
    return jax.ShapeDtypeStruct((3,) + sums.shape[1:], sums.dtype)


def _shard_side(sums):
    return _Side([sums], [_parts_shape(sums)], lambda ins, outs: _plan_shard_ici(ins[0], outs[0]), 3)


def _got_shape(grad):
    return jax.ShapeDtypeStruct(_half_shape(grad.shape), grad.dtype)


def _pair_side(grad):
    return _Side([grad], [_got_shape(grad)], lambda ins, outs: _plan_pair(ins[0], outs[0]), 1)


def _chip_sum(grad, idx, tag):
    got, = _pair_exchange([grad], name=f"grad_pair_exchange_{tag}")
    return _pair_sum(grad, got, idx, name=f"grad_pair_sum_{tag}")


def _local_step(x, target, mod, g_pre_mix, g_post_mix, g_pre_mlp, g_post_mlp, w_in_t, b_fgate, w_gla_a2,
                b_gla_a2, g_fox, g_gla, own_w_out, own_w_mlp_in, own_w_mlp_out, chip, idx):
    s, d = x.shape
    shift_m, scale_m, gate_m, shift_f, scale_f, gate_f = [mod[:, i * d:(i + 1) * d] for i in range(6)]
    a1 = g_pre_mix * (1.0 + scale_m)
    a2 = g_pre_mlp * (1.0 + scale_f)
    bf = jnp.concatenate([b_fgate, jnp.zeros((1, SMALL_W - FOX_HEADS), F32)], axis=1)
    w2p = jnp.zeros((SMALL_W, GLA_KW), F32).at[FOX_HEADS:FOX_HEADS + GLA_RANK].set(w_gla_a2)

    w_main, w_small = _split_w_in(w_in_t)
    h1 = _pre_fwd(x, a1, shift_m, name="pre_mix_fwd")
    full_shape = lambda w: jax.ShapeDtypeStruct((4,) + w.shape, w.dtype)
    first_side = _Side(
        [own_w_out, own_w_mlp_out], [full_shape(own_w_out), full_shape(own_w_mlp_out)],
        lambda ins, outs: _plan_gather_ici(ins[0], outs[0]) + _plan_gather_ici(ins[1], outs[1], part=(0, 4)), 6)
    proj, gw_out, gw_mlp_out = _mm(h1, w_main, mode="nt", out_dtypes=[BF16], name="in_proj_main", side=first_side)
    ps, = _mm(h1, w_small, mode="nt", out_dtypes=[F32], name="in_proj_small")
    gw_out, = _finish_gather([gw_out], [own_w_out], chip, name="gather_w_out_d2d")
    w_out_full = gw_out.reshape(-1, d)
    cum, log_a = _gates_fwd(ps, bf, w2p, b_gla_a2, name="gates_fwd")
    cum_t = cum[:, :FOX_HEADS].T
    o_fox, fox_n, lse, gw_mlp_in = _fox_fwd(proj, cum_t, g_fox, name="fox_fwd", side=_gather_side([own_w_mlp_in]))
    o_gla, gla_n, states = _gla_fwd(proj, log_a, g_gla, name="gla_fwd")
    mixed = jnp.concatenate([fox_n, gla_n], axis=1)
    y1, gw_mlp_in = _mm(mixed, w_out_full, mode="nn", out_dtypes=[F32], name="out_proj",
                        side=_Side([gw_mlp_in], [jax.ShapeDtypeStruct(gw_mlp_in.shape, gw_mlp_in.dtype)],
                                   lambda ins, outs: _plan_gather_d2d(outs[0]), 3, aliases={0: 0}))
    gw_mlp_in = lax.dynamic_update_index_in_dim(gw_mlp_in, own_w_mlp_in, chip, 0)
    x1, h2 = _post_pre_fwd(x, y1, gate_m, g_post_mix, a2, shift_f, name="post_mix_pre_mlp_fwd")

    def mlp_act(acc):
        r = jnp.maximum(acc, 0.0)
        return acc, r * r

    rest_side = _Side([own_w_mlp_out, gw_mlp_out], [full_shape(own_w_mlp_out)],
                      lambda ins, outs: [cp for q in (1, 2, 3) for cp in _plan_gather_ici(ins[0], outs[0], part=(q, 4))],
                      9, aliases={1: 0})
    u, act, gw_mlp_out = _mm(h2, gw_mlp_in, mode="nn", out_dtypes=[BF16, BF16], epi=mlp_act, name="mlp_in",
                             b_slots=4, tm=MM_TM, side=rest_side)
    gw_mlp_out, = _finish_gather([gw_mlp_out], [own_w_mlp_out], chip, name="gather_w_mlp_out_d2d")
    w_mlp_out_full = gw_mlp_out.reshape(-1, d)
    y2, = _mm(act, w_mlp_out_full, mode="nn", out_dtypes=[F32], name="mlp_out")
    dx2, dy2, loss_part, dgate_f, dg_post_mlp = _post_loss_bwd(x1, y2, gate_f, g_post_mlp, target,
                                                               name="post_mlp_loss_bwd")
    dw_mlp_out, = _mm(act, dy2, mode="tn", out_dtypes=[BF16], name="dw_mlp_out")
    dw_mlp_out = dw_mlp_out.reshape(4, D_FF // 4, d)

    def act_bwd(acc, uv):
        return (acc * (2.0 * jnp.maximum(uv.astype(F32), 0.0)),)

    du, got_mlp_out = _mm(dy2, w_mlp_out_full, mode="nt", out_dtypes=[BF16], extras=[u], epi=act_bwd,
                          name="d_mlp_hidden", tm=MM_TM, side=_pair_side(dw_mlp_out))
    sum_mlp_out = _pair_sum(dw_mlp_out, got_mlp_out, idx, name="grad_pair_sum_mlp_out")
    nj = D_FF // 4 // min(MM_T, D_FF // 4)
    tmw = min(MM_T, d)
    dw_mlp_in, parts_mlp_out = _mm(
        h2, du, mode="tn", out_dtypes=[BF16], name="dw_mlp_in",
        out_shapes=[jax.ShapeDtypeStruct((4, d, D_FF // 4), BF16)],
        out_specs=[pl.BlockSpec((1, tmw, min(MM_T, D_FF // 4)), lambda i, j, kk: (j // nj, i, j % nj))],
        side=_Side([sum_mlp_out], [_parts_shape(sum_mlp_out)],
                   lambda ins, outs: _plan_shard_ici(ins[0], outs[0], piece=(0, 2)), 3))
    dh2, got_mlp_in, parts_mlp_out = _mm(
        du, gw_mlp_in, mode="nt", out_dtypes=[F32], name="d_mlp_in", b_slots=4,
        side=_Side([dw_mlp_in, sum_mlp_out, parts_mlp_out], [_got_shape(dw_mlp_in), _parts_shape(sum_mlp_out)],
                   lambda ins, outs: _plan_pair(ins[0], outs[0]) + _plan_shard_ici(ins[1], outs[1], piece=(1, 2)),
                   4, aliases={2: 1}))
    sum_mlp_in = _pair_sum(dw_mlp_in, got_mlp_in, idx, name="grad_pair_sum_mlp_in")
    dx1, dshift_f, da2, dy1, dgate_m, dg_post_mix = _pre_post_bwd(dh2, x1, dx2, a2, y1, gate_m, g_post_mix,
                                                                  name="pre_mlp_post_mix_bwd")
    buf_mlp_out = _final_sum(sum_mlp_out, parts_mlp_out, idx, (D_FF // 4, d), name="grad_final_sum_mlp_out")
    dw_out, g_mlp_out = _mm(mixed, dy1, mode="tn", out_dtypes=[BF16], name="dw_out", side=_half_side([buf_mlp_out]))
    dw_out = dw_out.reshape(4, d // 4, d)
    dmixed, got_out = _mm(dy1, w_out_full, mode="nt", out_dtypes=[BF16], name="d_mixed", side=_pair_side(dw_out))
    sum_out = _pair_sum(dw_out, got_out, idx, name="grad_pair_sum_out")
    do_fox, delta, dg_fox = _head_norm_bwd(dmixed, o_fox, g_fox, None, nh=FOX_HEADS, hd=FOX_HD, dn_col=0,
                                           gr_col=0, name="fox_norm_bwd")
    do_gla, dgr, _, dg_gla = _head_norm_bwd(dmixed, o_gla, g_gla, proj, nh=GLA_HEADS, hd=GLA_DV, dn_col=1,
                                            gr_col=(3 * FOX_W + 2 * GLA_KW + GLA_W) // GLA_W, name="gla_norm_bwd")
    dq_fox, dk_fox, dv_fox, dcq, dck_t, parts_mlp_in, parts_out = _fox_bwd(
        proj, do_fox, cum_t, lse, delta, name="fox_bwd",
        side=_Side([sum_mlp_in, sum_out], [_parts_shape(sum_mlp_in), _parts_shape(sum_out)],
                   lambda ins, outs: _plan_shard_ici(ins[0], outs[0]) + _plan_shard_ici(ins[1], outs[1]), 6))
    dgq, dgk, dgv, dla = _gla_bwd(proj, log_a, do_gla, states, name="gla_bwd")
    dck = dcq + jnp.concatenate([dck_t.T, jnp.zeros((s, SMALL_W - FOX_HEADS), F32)], axis=1)
    dps, dbf, dw2p, db2 = _gates_bwd(dck, ps, bf, w2p, b_gla_a2, dla, name="gates_bwd")
    dproj = jnp.concatenate([dq_fox.astype(BF16), dk_fox, dv_fox, dgq, dgk, dgv, dgr], axis=1)
    buf_mlp_in = _final_sum(sum_mlp_in, parts_mlp_in, idx, (d, D_FF // 4), name="grad_final_sum_mlp_in")
    buf_out = _final_sum(sum_out, parts_out, idx, (d // 4, d), name="grad_final_sum_out")
    dw_main, g_mlp_in, g_out = _mm(dproj, h1, mode="tn", out_dtypes=[BF16], name="dw_in_main",
                                   side=_half_side([buf_mlp_in, buf_out]))
    dw_small, = _mm(dps, h1, mode="tn", out_dtypes=[BF16], name="dw_in_small")
    rs_in = w_in_t.shape[0] // 4
    dw_in = _merge_dw_in(dw_main, dw_small).reshape(4, rs_in, d)
    sum_in = _chip_sum(dw_in, idx, "in")
    dh1_small, = _mm(dps, w_small, mode="nn", out_dtypes=[F32], name="d_h1_small")
    dh1, parts_in = _mm(dproj, w_main, mode="nn", out_dtypes=[F32], extras=[dh1_small],
                        epi=lambda acc, e: (acc + e,), name="d_h1", side=_shard_side(sum_in))
    grad_x, dshift_m, da1 = _pre_bwd(dh1, x, dx1, a1, name="pre_mix_bwd")
    buf_in = _final_sum(sum_in, parts_in, idx, (rs_in, d), name="grad_final_sum_in")
    g_in, = _half_exchange([buf_in], name="grad_half_exchange_in")
    g_big = [g_in, g_out, g_mlp_in, g_mlp_out]

    dmod = jnp.concatenate([dshift_m, da1 * g_pre_mix, dgate_m, dshift_f, da2 * g_pre_mlp, dgate_f], axis=1)
    small = dict(
        dmod=dmod, g_pre_mix=da1 * (1.0 + scale_m), g_post_mix=dg_post_mix, g_pre_mlp=da2 * (1.0 + scale_f),
        g_post_mlp=dg_post_mlp, b_fgate=dbf[:, :FOX_HEADS], w_gla_a2=dw2p[FOX_HEADS:FOX_HEADS + GLA_RANK],
        b_gla_a2=db2, g_fox_out=dg_fox, g_gla_out=dg_gla)
    return loss_part, grad_x, g_big, small


def _pack(arrays):
    flat = jnp.concatenate([a.reshape(-1).astype(F32) for a in arrays])
    n = flat.shape[0]
    rows = -(-n // 128)
    rows = -(-rows // 8) * 8
    return jnp.pad(flat, (0, rows * 128 - n)).reshape(rows, 128)


def _unpack(buf, shapes):
    flat = buf.reshape(-1)
    out, off = [], 0
    for shp in shapes:
        n = 1
        for q in shp:
            n *= q
        out.append(flat[off:off + n].reshape(shp))
        off += n
    return out


SMALL_GRAD_ORDER = ["dmod", "g_pre_mix", "g_post_mix", "g_pre_mlp", "g_post_mlp", "b_fgate", "w_gla_a2", "b_gla_a2",
                    "g_fox_out", "g_gla_out"]


def kernel(x, c, w_ada, b_ada, g_pre_mix, g_post_mix, w_in, b_fgate, w_gla_a2, b_gla_a2, g_fox_out, g_gla_out, w_out, g_pre_mlp, g_post_mlp, w_mlp_in, w_mlp_out, loss_target, m_w_ada, m_b_ada, m_g_pre_mix, m_g_post_mix, m_w_in, m_b_fgate, m_w_gla_a2, m_b_gla_a2, m_g_fox_out, m_g_gla_out, m_w_out, m_g_pre_mlp, m_g_post_mlp, m_w_mlp_in, m_w_mlp_out, v_w_ada, v_b_ada, v_g_pre_mix, v_g_post_mix, v_w_in, v_b_fgate, v_w_gla_a2, v_b_gla_a2, v_g_fox_out, v_g_gla_out, v_w_out, v_g_pre_mlp, v_g_post_mlp, v_w_mlp_in, v_w_mlp_out):
    ix, iy, ic = lax.axis_index("x"), lax.axis_index("y"), lax.axis_index("c")
    chip = 2 * ix + iy
    dev = 4 * ix + 2 * iy + ic
    d = D_MODEL

    c_act = _silu_rows(c, name="silu_c")
    pack1 = _pack([c_act, w_gla_a2[0], g_gla_out[0]])
    rows1 = pack1.shape[0]
    got1 = _gather8(pack1, name="gather_small_fwd").reshape(8, rows1, 128)
    per_dev = [_unpack(got1[q], [(d,), (GLA_RANK, GLA_KW // 4), (GLA_HEADS, GLA_DV // 4)]) for q in range(8)]
    c_all = jnp.stack([p[0] for p in per_dev])
    w_gla_a2_full = jnp.concatenate([per_dev[2 * j][1] for j in range(4)], axis=1)
    g_gla_full = jnp.concatenate([per_dev[2 * j][2] for j in range(4)], axis=1)
    cols = w_ada.shape[2]
    b_ada_shard = lax.dynamic_slice_in_dim(b_ada, chip * cols, cols, axis=1)
    mod_sh = _mod_shard(c_all, w_ada[0], b_ada_shard, name="ada_mod")
    got2 = _gather8(mod_sh, name="gather_mod").reshape(8, 8, cols)
    mod_all = jnp.concatenate([got2[2 * j] for j in range(4)], axis=1)
    mod = lax.dynamic_slice_in_dim(mod_all, dev, 1, axis=0)

    tr_in = lambda a: jnp.transpose(a[0])
    own_bf = [tr_in(w_in).astype(BF16), w_out[0].astype(BF16), w_mlp_in[0].astype(BF16), w_mlp_out[0].astype(BF16)]
    gw_in, = _finish_gather(_gather_weights(own_bf[:1], name="gather_w_in_ici"), own_bf[:1], chip,
                            name="gather_w_in_d2d")
    w_in_t = gw_in.reshape(-1, d)
    idx = jnp.stack([ic, chip]).astype(jnp.int32)
    loss_part, grad_x, g_big, small = _local_step(
        x[0], loss_target[0], mod, g_pre_mix, g_post_mix, g_pre_mlp, g_post_mlp, w_in_t, b_fgate,
        w_gla_a2_full, b_gla_a2, g_fox_out[0], g_gla_full, own_bf[1], own_bf[2], own_bf[3], chip, idx)
    loss = lax.psum(loss_part[0, 0], ("x", "y", "c"))

    big_w = [(tr_in(w_in), tr_in(m_w_in), tr_in(v_w_in)), (w_out[0], m_w_out[0], v_w_out[0]),
             (w_mlp_in[0], m_w_mlp_in[0], v_w_mlp_in[0]), (w_mlp_out[0], m_w_mlp_out[0], v_w_mlp_out[0])]
    big_res = []
    for q, (g, (w, m, v)) in enumerate(zip(g_big, big_w)):
        res4 = (g,) + tuple(_adam(g, w, m, v, name=f"adam_big_{q}"))
        big_res.append(tuple((jnp.transpose(a) if q == 0 else a)[None] for a in res4))

    pack2 = _pack([small[k] for k in SMALL_GRAD_ORDER])
    rows2 = pack2.shape[0]
    got3 = _gather8(pack2, name="gather_small_grads").reshape(8, rows2, 128)
    dmod_all = got3[:, :6 * d // 128, :].reshape(8, 6 * d)
    sums = _stack_sum(got3, name="small_grad_sum")
    shapes = [(1, 6 * d), (1, d), (1, d), (1, d), (1, d), (1, FOX_HEADS), (1, GLA_RANK, GLA_KW), (1, GLA_KW),
              (1, FOX_HEADS, FOX_HD), (1, GLA_HEADS, GLA_DV)]
    sg = dict(zip(["b_ada"] + SMALL_GRAD_ORDER[1:], _unpack(sums, shapes)))
    sg["w_gla_a2"] = lax.dynamic_slice_in_dim(sg["w_gla_a2"], chip * (GLA_KW // 4), GLA_KW // 4, axis=2)
    sg["g_gla_out"] = lax.dynamic_slice_in_dim(sg["g_gla_out"], chip * (GLA_DV // 4), GLA_DV // 4, axis=2)
    small_names = ["b_ada", "g_pre_mix", "g_post_mix", "b_fgate", "w_gla_a2", "b_gla_a2", "g_fox_out", "g_gla_out",
                   "g_pre_mlp", "g_post_mlp"]
    small_w = dict(b_ada=(b_ada, m_b_ada, v_b_ada), g_pre_mix=(g_pre_mix, m_g_pre_mix, v_g_pre_mix),
                   g_post_mix=(g_post_mix, m_g_post_mix, v_g_post_mix), b_fgate=(b_fgate, m_b_fgate, v_b_fgate),
                   w_gla_a2=(w_gla_a2, m_w_gla_a2, v_w_gla_a2), b_gla_a2=(b_gla_a2, m_b_gla_a2, v_b_gla_a2),
                   g_fox_out=(g_fox_out, m_g_fox_out, v_g_fox_out), g_gla_out=(g_gla_out, m_g_gla_out, v_g_gla_out),
                   g_pre_mlp=(g_pre_mlp, m_g_pre_mlp, v_g_pre_mlp), g_post_mlp=(g_post_mlp, m_g_post_mlp, v_g_post_mlp))
    sshapes = [small_w[k][0].shape for k in small_names]
    pg = _pack([sg[k] for k in small_names])
    pw, pm, pv = [_pack([small_w[k][q] for k in small_names]) for q in range(3)]
    pd, pmn, pvn = _adam(pg, pw, pm, pv, name="adam_small")
    s_delta = dict(zip(small_names, _unpack(pd, sshapes)))
    s_m = dict(zip(small_names, _unpack(pmn, sshapes)))
    s_v = dict(zip(small_names, _unpack(pvn, sshapes)))

    dmod_cols = lax.dynamic_slice_in_dim(dmod_all, chip * cols, cols, axis=1)
    g_ada, d_ada, m_ada, v_ada = _ada_grad_adam(c_all.T, dmod_cols, w_ada[0], m_w_ada[0], v_w_ada[0], name="ada_grad_adam")

    order = ["w_ada", "b_ada", "g_pre_mix", "g_post_mix", "w_in", "b_fgate", "w_gla_a2", "b_gla_a2", "g_fox_out",
             "g_gla_out", "w_out", "g_pre_mlp", "g_post_mlp", "w_mlp_in", "w_mlp_out"]
    res = {"w_ada": (g_ada[None], d_ada[None], m_ada[None], v_ada[None]),
           "w_in": big_res[0], "w_out": big_res[1], "w_mlp_in": big_res[2], "w_mlp_out": big_res[3]}
    for k in small_names:
        res[k] = (sg[k], s_delta[k], s_m[k], s_v[k])
    return (loss, grad_x[None], *[res[k][0] for k in order], *[res[k][1] for k in order],
            *[res[k][2] for k in order], *[res[k][3] for k in order])
```

```python
import functools

import jax
import jax.numpy as jnp
from jax import lax
from jax.experimental import pallas as pl
from jax.experimental.pallas import tpu as pltpu

F32 = jnp.float32
BF16 = jnp.bfloat16
MESH = pl.DeviceIdType.MESH
HIGHEST = lax.Precision.HIGHEST

D_MODEL = 2048
FOX_HEADS = 8
FOX_HD = 128
FOX_W = FOX_HEADS * FOX_HD
GLA_HEADS = 4
GLA_DK = 128
GLA_DV = 256
GLA_KW = GLA_HEADS * GLA_DK
GLA_W = GLA_HEADS * GLA_DV
GLA_RANK = 16
GLA_TEMP = 16.0
CHUNK = 64
D_FF = 4 * D_MODEL
EPS = 1e-6
MAIN_W = 3 * FOX_W + 2 * GLA_KW + 2 * GLA_W
SMALL_W = 128
NEG = -1e30

ADAM_LR = 0.001
ADAM_B1 = 0.9
ADAM_B2 = 0.999
ADAM_EPS = 1e-08
ADAM_WD = 0.01
ADAM_STEP = 10

VMEM_LIMIT = 52 * 1024 * 1024
ROW_TILE = 256
FOX_TQ = 512
FOX_TK = 512
GLA_ROWS = 512
GATE_TS = 512
MM_T = 1024
MM_TK = 2048
MM_TM = 2048


def _cp(*sem):
    return pltpu.CompilerParams(dimension_semantics=sem, vmem_limit_bytes=VMEM_LIMIT)


def _dot_nn(a, b, precision=None):
    return jnp.dot(a, b, preferred_element_type=F32, precision=precision)


def _dot_nt(a, b, precision=None):
    return lax.dot_general(a, b, (((1,), (1,)), ((), ())), preferred_element_type=F32, precision=precision)


def _dot_tn(a, b, precision=None):
    return lax.dot_general(a, b, (((0,), (0,)), ((), ())), preferred_element_type=F32, precision=precision)


def _sigmoid(x):
    return 1.0 / (1.0 + jnp.exp(-x))


def _log_sigmoid(x):
    return jnp.minimum(x, 0.0) - jnp.log(1.0 + jnp.exp(-jnp.abs(x)))


class _Side:
    def __init__(self, inputs, out_shapes, plan_fn, n_copies, aliases=None):
        self.inputs, self.out_shapes, self.plan_fn, self.n_copies = list(inputs), list(out_shapes), plan_fn, n_copies
        self.aliases = dict(aliases or {})

    def scratch(self):
        return [pltpu.SemaphoreType.DMA((self.n_copies,)), pltpu.SemaphoreType.DMA((self.n_copies,))]

    def run(self, in_refs, out_refs, sems, first, last):
        @pl.when(first)
        def _():
            _plan_start(self.plan_fn(in_refs, out_refs), *sems)

        @pl.when(last)
        def _():
            _plan_wait(self.plan_fn(in_refs, out_refs), *sems)


def _mm(a, b, *, mode, out_dtypes, name, tm=None, tn=None, tk=None, extras=(), epi=None,
        out_shapes=None, out_specs=None, side=None, b_slots=0):
    tm, tn, tk = tm or MM_T, tn or MM_T, tk or MM_TK
    b2 = (b.shape[1], b_slots * b.shape[2]) if b_slots else b.shape
    if mode == "nn":
        (m, k), n = a.shape, b2[1]
    elif mode == "nt":
        (m, k), n = a.shape, b2[0]
    else:
        (k, m), n = a.shape, b2[1]
    tm, tn, tk = min(tm, m), min(tn, n), min(tk, k)
    if b_slots:
        tn = min(tn, b.shape[2]) if mode == "nn" else tn
        tk = min(tk, b.shape[2]) if mode == "nt" else tk
    assert m % tm == 0 and n % tn == 0 and k % tk == 0, (name, m, n, k)
    nk = k // tk
    n_out, n_ex = len(out_dtypes), len(extras)
    if epi is None:
        epi = lambda acc: tuple(acc for _ in range(n_out))
    dot = {"nn": _dot_nn, "nt": _dot_nt, "tn": _dot_tn}[mode]

    n_si = len(side.inputs) if side else 0
    n_so = len(side.out_shapes) if side else 0
    grid = (m // tm, n // tn, nk)

    def body(*refs):
        a_ref, b_ref = refs[0], refs[1]
        ex_refs = refs[2:2 + n_ex]
        base = 2 + n_ex + n_si
        o_refs = refs[base:base + n_out]
        scratch = refs[base + n_out + n_so:]
        if side:
            pos = [pl.program_id(q) for q in range(3)]
            first = (pos[0] == 0) & (pos[1] == 0) & (pos[2] == 0)
            last = (pos[0] == grid[0] - 1) & (pos[1] == grid[1] - 1) & (pos[2] == grid[2] - 1)
            side.run(refs[2 + n_ex:base], refs[base + n_out:base + n_out + n_so], scratch[-2:], first, last)
        part = dot(a_ref[...], b_ref[...])

        def finish(acc):
            outs = epi(acc, *[e[...] for e in ex_refs])
            for o_ref, val in zip(o_refs, outs):
                o_ref[...] = val.reshape(o_ref.shape).astype(o_ref.dtype)

        if nk == 1:
            finish(part)
        else:
            acc_ref = scratch[0]
            kk = pl.program_id(2)

            @pl.when(kk == 0)
            def _():
                acc_ref[...] = part

            @pl.when(kk > 0)
            def _():
                acc_ref[...] += part

            @pl.when(kk == nk - 1)
            def _():
                finish(acc_ref[...])

    if mode == "nn":
        a_spec = pl.BlockSpec((tm, tk), lambda i, j, kk: (i, kk))
        b_spec = pl.BlockSpec((tk, tn), lambda i, j, kk: (kk, j))
        if b_slots:
            per = b.shape[2] // tn
            b_spec = pl.BlockSpec((None, tk, tn), lambda i, j, kk: (j // per, kk, j % per))
    elif mode == "nt":
        a_spec = pl.BlockSpec((tm, tk), lambda i, j, kk: (i, kk))
        b_spec = pl.BlockSpec((tn, tk), lambda i, j, kk: (j, kk))
        if b_slots:
            per = b.shape[2] // tk
            b_spec = pl.BlockSpec((None, tn, tk), lambda i, j, kk: (kk // per, j, kk % per))
    else:
        assert not b_slots
        a_spec = pl.BlockSpec((tk, tm), lambda i, j, kk: (kk, i))
        b_spec = pl.BlockSpec((tk, tn), lambda i, j, kk: (kk, j))
    tile_spec = pl.BlockSpec((tm, tn), lambda i, j, kk: (i, j))
    if out_shapes is None:
        out_shapes = [jax.ShapeDtypeStruct((m, n), dt) for dt in out_dtypes]
    if out_specs is None:
        out_specs = [tile_spec for _ in out_dtypes]
    any_spec = pl.BlockSpec(memory_space=pl.ANY)
    res = pl.pallas_call(
        body,
        grid=grid,
        in_specs=[a_spec, b_spec] + [tile_spec for _ in extras] + [any_spec] * n_si,
        out_specs=list(out_specs) + [any_spec] * n_so,
        out_shape=list(out_shapes) + (side.out_shapes if side else []),
        scratch_shapes=([pltpu.VMEM((tm, tn), F32)] if nk > 1 else []) + (side.scratch() if side else []),
        compiler_params=_cp("arbitrary", "arbitrary", "arbitrary") if side else _cp("parallel", "parallel", "arbitrary"),
        input_output_aliases={2 + n_ex + si: n_out + so for si, so in side.aliases.items()} if side else {},
        name=name,
    )(a, b, *extras, *(side.inputs if side else []))
    return res


def _row_spec(ts, d):
    return pl.BlockSpec((ts, d), lambda i: (i, 0))


def _vec_spec(d):
    return pl.BlockSpec((1, d), lambda i: (0, 0))


def _side_args(side, n_in, n_out):
    if side is None:
        return [], [], [], [], [], {}
    any_spec = pl.BlockSpec(memory_space=pl.ANY)
    return ([any_spec] * len(side.inputs), [any_spec] * len(side.out_shapes), side.out_shapes, side.scratch(),
            side.inputs, {n_in + si: n_out + so for si, so in side.aliases.items()})


def _pre_fwd(x, avec, shift, *, name, side=None):
    s, d = x.shape
    ts = min(ROW_TILE, s)
    nb = s // ts
    s_in, s_out, s_shapes, s_scratch, s_ops, s_alias = _side_args(side, 3, 1)

    def body(x_ref, a_ref, s_ref, *rest):
        h_ref = rest[len(s_in)]
        if side:
            step = pl.program_id(0)
            side.run(rest[:len(s_in)], rest[len(s_in) + 1:len(s_in) + 1 + len(s_out)],
                     rest[len(s_in) + 1 + len(s_out):], step == 0, step == nb - 1)
        xv = x_ref[...]
        r = lax.rsqrt(jnp.mean(xv * xv, axis=-1, keepdims=True) + EPS)
        h_ref[...] = (xv * r * a_ref[...] + s_ref[...]).astype(BF16)

    res = pl.pallas_call(
        body, grid=(nb,),
        in_specs=[_row_spec(ts, d), _vec_spec(d), _vec_spec(d)] + s_in,
        out_specs=[_row_spec(ts, d)] + s_out,
        out_shape=[jax.ShapeDtypeStruct((s, d), BF16)] + s_shapes,
        scratch_shapes=s_scratch, input_output_aliases=s_alias,
        compiler_params=_cp("arbitrary" if side else "parallel"), name=name,
    )(x, avec, shift, *s_ops)
    return res if side else res[0]


def _post_pre_fwd(x, y, gate, g, avec, shift, *, name):
    s, d = x.shape
    ts = min(ROW_TILE, s)

    def body(x_ref, y_ref, gate_ref, g_ref, a_ref, s_ref, o_ref, h_ref):
        yv = y_ref[...]
        r = lax.rsqrt(jnp.mean(yv * yv, axis=-1, keepdims=True) + EPS)
        x1 = x_ref[...] + gate_ref[...] * (yv * r * g_ref[...])
        o_ref[...] = x1
        r1 = lax.rsqrt(jnp.mean(x1 * x1, axis=-1, keepdims=True) + EPS)
        h_ref[...] = (x1 * r1 * a_ref[...] + s_ref[...]).astype(BF16)

    return pl.pallas_call(
        body, grid=(s // ts,),
        in_specs=[_row_spec(ts, d), _row_spec(ts, d)] + [_vec_spec(d)] * 4,
        out_specs=[_row_spec(ts, d), _row_spec(ts, d)],
        out_shape=[jax.ShapeDtypeStruct((s, d), F32), jax.ShapeDtypeStruct((s, d), BF16)],
        compiler_params=_cp("parallel"), name=name,
    )(x, y, gate, g, avec, shift)


def _post_bwd_math(dxv, yv, gatev, gv):
    r = lax.rsqrt(jnp.mean(yv * yv, axis=-1, keepdims=True) + EPS)
    yhat = yv * r
    dn = dxv * gatev
    dyhat = dn * gv
    dy = r * (dyhat - yhat * jnp.mean(dyhat * yhat, axis=-1, keepdims=True))
    return dy, dxv * (yhat * gv), dn * yhat


def _accumulate(first, pairs):
    @pl.when(first)
    def _():
        for ref, _ in pairs:
            ref[...] = jnp.zeros_like(ref)

    for ref, val in pairs:
        ref[...] += jnp.sum(val, axis=0, keepdims=True)


def _post_loss_bwd(x, y, gate, g, target, *, name):
    s, d = x.shape
    ts = min(ROW_TILE, s)

    def body(x_ref, y_ref, gate_ref, g_ref, t_ref, dx_ref, dy_ref, loss_ref, dgate_ref, dg_ref):
        yv, gatev, gv = y_ref[...], gate_ref[...], g_ref[...]
        r = lax.rsqrt(jnp.mean(yv * yv, axis=-1, keepdims=True) + EPS)
        diff = x_ref[...] + gatev * (yv * r * gv) - t_ref[...]
        dxv = diff * (1.0 / d)
        dx_ref[...] = dxv
        dy, dgate_rows, dg_rows = _post_bwd_math(dxv, yv, gatev, gv)
        dy_ref[...] = dy.astype(BF16)
        first = pl.program_id(0) == 0
        _accumulate(first, [(dgate_ref, dgate_rows), (dg_ref, dg_rows)])

        @pl.when(first)
        def _():
            loss_ref[...] = jnp.zeros_like(loss_ref)

        loss_ref[...] += jnp.sum(jnp.mean(diff * diff, axis=-1, keepdims=True)) * 0.5

    return pl.pallas_call(
        body, grid=(s // ts,),
        in_specs=[_row_spec(ts, d), _row_spec(ts, d), _vec_spec(d), _vec_spec(d), _row_spec(ts, d)],
        out_specs=[_row_spec(ts, d), _row_spec(ts, d), pl.BlockSpec((1, 128), lambda i: (0, 0)), _vec_spec(d),
                   _vec_spec(d)],
        out_shape=[jax.ShapeDtypeStruct((s, d), F32), jax.ShapeDtypeStruct((s, d), BF16),
                   jax.ShapeDtypeStruct((1, 128), F32), jax.ShapeDtypeStruct((1, d), F32),
                   jax.ShapeDtypeStruct((1, d), F32)],
        compiler_params=_cp("arbitrary"), name=name,
    )(x, y, gate, g, target)


def _pre_post_bwd(dh, xin, dres, avec, y, gate, g, *, name):
    s, d = xin.shape
    ts = min(ROW_TILE, s)

    def body(dh_ref, x_ref, dres_ref, a_ref, y_ref, gate_ref, g_ref, dx_ref, dshift_ref, da_ref, dy_ref,
             dgate_ref, dg_ref):
        xv, dhv = x_ref[...], dh_ref[...]
        r = lax.rsqrt(jnp.mean(xv * xv, axis=-1, keepdims=True) + EPS)
        xhat = xv * r
        dxhat = dhv * a_ref[...]
        dxv = dres_ref[...] + r * (dxhat - xhat * jnp.mean(dxhat * xhat, axis=-1, keepdims=True))
        dx_ref[...] = dxv
        dy, dgate_rows, dg_rows = _post_bwd_math(dxv, y_ref[...], gate_ref[...], g_ref[...])
        dy_ref[...] = dy.astype(BF16)
        _accumulate(pl.program_id(0) == 0, [(dshift_ref, dhv), (da_ref, dhv * xhat), (dgate_ref, dgate_rows),
                                            (dg_ref, dg_rows)])

    return pl.pallas_call(
        body, grid=(s // ts,),
        in_specs=[_row_spec(ts, d), _row_spec(ts, d), _row_spec(ts, d), _vec_spec(d), _row_spec(ts, d),
                  _vec_spec(d), _vec_spec(d)],
        out_specs=[_row_spec(ts, d), _vec_spec(d), _vec_spec(d), _row_spec(ts, d), _vec_spec(d), _vec_spec(d)],
        out_shape=[jax.ShapeDtypeStruct((s, d), F32), jax.ShapeDtypeStruct((1, d), F32),
                   jax.ShapeDtypeStruct((1, d), F32), jax.ShapeDtypeStruct((s, d), BF16),
                   jax.ShapeDtypeStruct((1, d), F32), jax.ShapeDtypeStruct((1, d), F32)],
        compiler_params=_cp("arbitrary"), name=name,
    )(dh, xin, dres, avec, y, gate, g)


def _pre_bwd(dh, xin, dres, avec, *, name, side=None):
    s, d = xin.shape
    ts = min(ROW_TILE, s)
    nb = s // ts
    s_in, s_out, s_shapes, s_scratch, s_ops, s_alias = _side_args(side, 4, 3)

    def body(dh_ref, x_ref, dres_ref, a_ref, *rest):
        dx_ref, dshift_ref, da_ref = rest[len(s_in):len(s_in) + 3]
        if side:
            step = pl.program_id(0)
            side.run(rest[:len(s_in)], rest[len(s_in) + 3:len(s_in) + 3 + len(s_out)],
                     rest[len(s_in) + 3 + len(s_out):], step == 0, step == nb - 1)
        xv, dhv = x_ref[...], dh_ref[...]
        r = lax.rsqrt(jnp.mean(xv * xv, axis=-1, keepdims=True) + EPS)
        xhat = xv * r
        dxhat = dhv * a_ref[...]
        dx_ref[...] = dres_ref[...] + r * (dxhat - xhat * jnp.mean(dxhat * xhat, axis=-1, keepdims=True))

        @pl.when(pl.program_id(0) == 0)
        def _():
            dshift_ref[...] = jnp.zeros_like(dshift_ref)
            da_ref[...] = jnp.zeros_like(da_ref)

        dshift_ref[...] += jnp.sum(dhv, axis=0, keepdims=True)
        da_ref[...] += jnp.sum(dhv * xhat, axis=0, keepdims=True)

    return pl.pallas_call(
        body, grid=(nb,),
        in_specs=[_row_spec(ts, d), _row_spec(ts, d), _row_spec(ts, d), _vec_spec(d)] + s_in,
        out_specs=[_row_spec(ts, d), _vec_spec(d), _vec_spec(d)] + s_out,
        out_shape=[jax.ShapeDtypeStruct((s, d), F32), jax.ShapeDtypeStruct((1, d), F32),
                   jax.ShapeDtypeStruct((1, d), F32)] + s_shapes,
        scratch_shapes=s_scratch, input_output_aliases=s_alias,
        compiler_params=_cp("arbitrary"), name=name,
    )(dh, xin, dres, avec, *s_ops)


def _tri(n, strict=False, upper=False):
    r = lax.broadcasted_iota(jnp.int32, (n, n), 0)
    c = lax.broadcasted_iota(jnp.int32, (n, n), 1)
    if upper:
        r, c = c, r
    return ((r > c) if strict else (r >= c)).astype(F32)


def _gates_fwd(ps, bf, w2p, b2, *, name):
    s = ps.shape[0]
    ts = min(GATE_TS, s)

    def body(ps_ref, bf_ref, w_ref, b2_ref, cum_ref, la_ref, carry_ref):
        @pl.when(pl.program_id(0) == 0)
        def _():
            carry_ref[...] = jnp.zeros_like(carry_ref)

        psv = ps_ref[...]
        lf = _log_sigmoid(psv + bf_ref[...])
        cum = _dot_nn(_tri(ts), lf, HIGHEST) + carry_ref[...]
        cum_ref[...] = cum
        carry_ref[...] = cum[ts - 1:ts, :]
        z = _dot_nn(psv, w_ref[...], HIGHEST) + b2_ref[...]
        la_ref[...] = _log_sigmoid(z) * (1.0 / GLA_TEMP)

    return pl.pallas_call(
        body, grid=(s // ts,),
        in_specs=[_row_spec(ts, SMALL_W), _vec_spec(SMALL_W),
                  pl.BlockSpec((SMALL_W, GLA_KW), lambda i: (0, 0)), _vec_spec(GLA_KW)],
        out_specs=[_row_spec(ts, SMALL_W), _row_spec(ts, GLA_KW)],
        out_shape=[jax.ShapeDtypeStruct((s, SMALL_W), F32), jax.ShapeDtypeStruct((s, GLA_KW), F32)],
        scratch_shapes=[pltpu.VMEM((1, SMALL_W), F32)],
        compiler_params=_cp("arbitrary"), name=name,
    )(ps, bf, w2p, b2)


def _gates_bwd(dck, ps, bf, w2p, b2, dla, *, name):
    s = ps.shape[0]
    ts = min(GATE_TS, s)
    nb = s // ts
    rev = lambda i: (nb - 1 - i, 0)

    def body(dck_ref, ps_ref, bf_ref, w_ref, b2_ref, dla_ref, dps_ref, dbf_ref, dw_ref, db2_ref, carry_ref):
        @pl.when(pl.program_id(0) == 0)
        def _():
            carry_ref[...] = jnp.zeros_like(carry_ref)
            dbf_ref[...] = jnp.zeros_like(dbf_ref)
            dw_ref[...] = jnp.zeros_like(dw_ref)
            db2_ref[...] = jnp.zeros_like(db2_ref)

        psv, dckv = ps_ref[...], dck_ref[...]
        dlf = _dot_nn(_tri(ts, upper=True), dckv, HIGHEST) + carry_ref[...]
        carry_ref[...] += jnp.sum(dckv, axis=0, keepdims=True)
        lane = lax.broadcasted_iota(jnp.int32, (ts, SMALL_W), 1)
        dff = jnp.where(lane < FOX_HEADS, dlf * _sigmoid(-(psv + bf_ref[...])), 0.0)
        z = _dot_nn(psv, w_ref[...], HIGHEST) + b2_ref[...]
        dz = dla_ref[...] * _sigmoid(-z) * (1.0 / GLA_TEMP)
        dps_ref[...] = (_dot_nt(dz, w_ref[...], HIGHEST) + dff).astype(BF16)
        dbf_ref[...] += jnp.sum(dff, axis=0, keepdims=True)
        dw_ref[...] += _dot_tn(psv, dz, HIGHEST)
        db2_ref[...] += jnp.sum(dz, axis=0, keepdims=True)

    return pl.pallas_call(
        body, grid=(nb,),
        in_specs=[pl.BlockSpec((ts, SMALL_W), rev), pl.BlockSpec((ts, SMALL_W), rev), _vec_spec(SMALL_W),
                  pl.BlockSpec((SMALL_W, GLA_KW), lambda i: (0, 0)), _vec_spec(GLA_KW),
                  pl.BlockSpec((ts, GLA_KW), rev)],
        out_specs=[pl.BlockSpec((ts, SMALL_W), rev), _vec_spec(SMALL_W),
                   pl.BlockSpec((SMALL_W, GLA_KW), lambda i: (0, 0)), _vec_spec(GLA_KW)],
        out_shape=[jax.ShapeDtypeStruct((s, SMALL_W), BF16), jax.ShapeDtypeStruct((1, SMALL_W), F32),
                   jax.ShapeDtypeStruct((SMALL_W, GLA_KW), F32), jax.ShapeDtypeStruct((1, GLA_KW), F32)],
        scratch_shapes=[pltpu.VMEM((1, SMALL_W), F32)],
        compiler_params=_cp("arbitrary"), name=name,
    )(dck, ps, bf, w2p, b2, dla)


def _hs(h, hd=FOX_HD):
    return slice(h * hd, (h + 1) * hd)


def _fox_fwd(proj, cum_t, g_fox, *, name, side=None):
    s = proj.shape[0]
    tq, tk = min(FOX_TQ, s), min(FOX_TK, s)
    scale = FOX_HD ** -0.5
    n_si = len(side.inputs) if side else 0
    n_so = len(side.out_shapes) if side else 0
    grid = (s // tq, s // tk)

    def body(*refs):
        q_ref, k_ref, v_ref, ck_ref, g_ref = refs[:5]
        o_ref, n_ref, lse_ref = refs[5 + n_si:8 + n_si]
        m_sc, acc_sc = refs[8 + n_si + n_so:10 + n_si + n_so]
        i, j = pl.program_id(0), pl.program_id(1)
        if side:
            side.run(refs[5:5 + n_si], refs[8 + n_si:8 + n_si + n_so], refs[10 + n_si + n_so:],
                     (i == 0) & (j == 0), (i == grid[0] - 1) & (j == grid[1] - 1))

        @pl.when(j == 0)
        def _():
            m_sc[...] = jnp.full_like(m_sc, NEG)
            acc_sc[...] = jnp.zeros_like(acc_sc)

        def block(masked):
            mask = _causal_mask(i, j, tq, tk) if masked else None
            ones = jnp.ones((tk, FOX_HD), BF16)
            for h in range(FOX_HEADS):
                sc = _fox_logits(_dot_nt(q_ref[:, _hs(h)], k_ref[:, _hs(h)]), ck_ref[h:h + 1, :], mask, scale)
                m_prev = m_sc[h]
                m_new = jnp.maximum(m_prev, jnp.max(sc, axis=-1, keepdims=True))
                alpha = jnp.exp(m_prev - m_new)
                p = jnp.exp(sc - m_new).astype(BF16)
                v_one = jnp.concatenate([v_ref[:, _hs(h)], ones], axis=1)
                acc_sc[:, _hs(h, 2 * FOX_HD)] = alpha * acc_sc[:, _hs(h, 2 * FOX_HD)] + _dot_nn(p, v_one)
                m_sc[h] = m_new

        pl.when(j < i)(functools.partial(block, False))

        @pl.when(j == i)
        def _():
            block(True)
            lane = lax.broadcasted_iota(jnp.int32, (tq, 128), 1)
            lse = jnp.zeros((tq, 128), F32)
            for h in range(FOX_HEADS):
                l_rep = acc_sc[:, 2 * h * FOX_HD + FOX_HD:2 * (h + 1) * FOX_HD]
                o = acc_sc[:, 2 * h * FOX_HD:2 * h * FOX_HD + FOX_HD] / l_rep
                o_ref[:, _hs(h)] = o
                r = lax.rsqrt(jnp.mean(o * o, axis=-1, keepdims=True) + EPS)
                n_ref[:, _hs(h)] = (o * r * g_ref[h:h + 1, :]).astype(BF16)
                lse = jnp.where(lane == h, m_sc[h] + jnp.log(l_rep), lse)
            lse_ref[...] = lse

    kv = lambda col: (lambda i, j: (jnp.minimum(j, i), col))
    any_spec = pl.BlockSpec(memory_space=pl.ANY)
    return pl.pallas_call(
        body, grid=grid,
        in_specs=[pl.BlockSpec((tq, FOX_W), lambda i, j: (i, 0)),
                  pl.BlockSpec((tk, FOX_W), kv(1)),
                  pl.BlockSpec((tk, FOX_W), kv(2)),
                  pl.BlockSpec((FOX_HEADS, tk), lambda i, j: (0, jnp.minimum(j, i))),
                  pl.BlockSpec((FOX_HEADS, FOX_HD), lambda i, j: (0, 0))] + [any_spec] * n_si,
        out_specs=[pl.BlockSpec((tq, FOX_W), lambda i, j: (i, 0)),
                   pl.BlockSpec((tq, FOX_W), lambda i, j: (i, 0)),
                   pl.BlockSpec((tq, 128), lambda i, j: (i, 0))] + [any_spec] * n_so,
        out_shape=[jax.ShapeDtypeStruct((s, FOX_W), F32), jax.ShapeDtypeStruct((s, FOX_W), BF16),
                   jax.ShapeDtypeStruct((s, 128), F32)] + (side.out_shapes if side else []),
        scratch_shapes=[pltpu.VMEM((FOX_HEADS, tq, 1), F32), pltpu.VMEM((tq, 2 * FOX_W), F32)]
        + (side.scratch() if side else []),
        compiler_params=_cp("arbitrary", "arbitrary"), name=name,
    )(proj, proj, proj, cum_t, g_fox, *(side.inputs if side else []))


def _causal_mask(i, j, tq, tk):
    rows = i * tq + lax.broadcasted_iota(jnp.int32, (tq, tk), 0)
    cols = j * tk + lax.broadcasted_iota(jnp.int32, (tq, tk), 1)
    return rows >= cols


def _fox_logits(qk, ck, mask, scale):
    sc = qk * scale - ck
    return sc if mask is None else jnp.where(mask, sc, NEG)


def _fox_bwd(proj, do, cum_t, lse, delta, *, name, side=None):
    s = proj.shape[0]
    tq, tk = min(FOX_TQ, s), min(FOX_TK, s)
    nk, nq = s // tk, s // tq
    scale = FOX_HD ** -0.5
    n_si = len(side.inputs) if side else 0
    n_so = len(side.out_shapes) if side else 0

    def body(*refs):
        q_ref, k_ref, v_ref, do_ref, ck_ref, lse_ref, dl_ref = refs[:7]
        dq_hbm, dk_ref, dv_ref, dcq_hbm, dck_ref = refs[7 + n_si:12 + n_si]
        dq_sc, dcq_sc, dk_sc, dv_sc, dck_sc, out_sems = refs[12 + n_si + n_so:18 + n_si + n_so]
        j, i = pl.program_id(0), pl.program_id(1)
        if side:
            side.run(refs[7:7 + n_si], refs[12 + n_si:12 + n_si + n_so], refs[18 + n_si + n_so:],
                     (j == 0) & (i == 0), (j == nk - 1) & (i == nq - 1))

        @pl.when((j == 0) & (i == 0))
        def _():
            dq_sc[...] = jnp.zeros_like(dq_sc)
            dcq_sc[...] = jnp.zeros_like(dcq_sc)

        @pl.when(i == 0)
        def _():
            dk_sc[...] = jnp.zeros_like(dk_sc)
            dv_sc[...] = jnp.zeros_like(dv_sc)
            dck_sc[...] = jnp.zeros_like(dck_sc)

        def block(masked):
            mask = _causal_mask(i, j, tq, tk) if masked else None
            qrows = pl.ds(pl.multiple_of(i * tq, tq), tq)
            for h in range(FOX_HEADS):
                sc = _fox_logits(_dot_nt(q_ref[:, _hs(h)], k_ref[:, _hs(h)]), ck_ref[h:h + 1, :], mask, scale)
                p = jnp.exp(sc - lse_ref[:, h:h + 1])
                ds = p * (_dot_nt(do_ref[:, _hs(h)], v_ref[:, _hs(h)]) - dl_ref[:, h:h + 1])
                dsb = ds.astype(BF16)
                dv_sc[:, _hs(h)] += _dot_tn(p.astype(BF16), do_ref[:, _hs(h)])
                dk_sc[:, _hs(h)] += _dot_tn(dsb, q_ref[:, _hs(h)])
                dq_sc[qrows, _hs(h)] += _dot_nn(dsb, k_ref[:, _hs(h)]) * scale
                dck_sc[h:h + 1, :] -= jnp.sum(ds, axis=0, keepdims=True)
                dcq_sc[qrows, h:h + 1] += jnp.sum(ds, axis=-1, keepdims=True)

        pl.when(i > j)(functools.partial(block, False))
        pl.when(i == j)(functools.partial(block, True))

        @pl.when(i == nq - 1)
        def _():
            dk_ref[...] = (dk_sc[...] * scale).astype(BF16)
            dv_ref[...] = dv_sc[...].astype(BF16)
            dck_ref[...] = dck_sc[...]

        @pl.when((j == nk - 1) & (i == nq - 1))
        def _():
            out_q = pltpu.make_async_copy(dq_sc, dq_hbm, out_sems.at[0])
            out_c = pltpu.make_async_copy(dcq_sc, dcq_hbm, out_sems.at[1])
            out_q.start()
            out_c.start()
            out_q.wait()
            out_c.wait()

    qrow = lambda j, i: (jnp.maximum(i, j), 0)
    krow = lambda col: (lambda j, i: (j, col))
    any_spec = pl.BlockSpec(memory_space=pl.ANY)
    return pl.pallas_call(
        body, grid=(nk, nq),
        in_specs=[pl.BlockSpec((tq, FOX_W), qrow), pl.BlockSpec((tk, FOX_W), krow(1)),
                  pl.BlockSpec((tk, FOX_W), krow(2)),
                  pl.BlockSpec((tq, FOX_W), qrow),
                  pl.BlockSpec((FOX_HEADS, tk), lambda j, i: (0, j)),
                  pl.BlockSpec((tq, 128), qrow), pl.BlockSpec((tq, 128), qrow)] + [any_spec] * n_si,
        out_specs=[any_spec, pl.BlockSpec((tk, FOX_W), lambda j, i: (j, 0)),
                   pl.BlockSpec((tk, FOX_W), lambda j, i: (j, 0)), any_spec,
                   pl.BlockSpec((FOX_HEADS, tk), lambda j, i: (0, j))] + [any_spec] * n_so,
        out_shape=[jax.ShapeDtypeStruct((s, FOX_W), F32), jax.ShapeDtypeStruct((s, FOX_W), BF16),
                   jax.ShapeDtypeStruct((s, FOX_W), BF16), jax.ShapeDtypeStruct((s, 128), F32),
                   jax.ShapeDtypeStruct((FOX_HEADS, s), F32)] + (side.out_shapes if side else []),
        scratch_shapes=[pltpu.VMEM((s, FOX_W), F32), pltpu.VMEM((s, 128), F32),
                        pltpu.VMEM((tk, FOX_W), F32), pltpu.VMEM((tk, FOX_W), F32), pltpu.VMEM((FOX_HEADS, tk), F32),
                        pltpu.SemaphoreType.DMA((2,))] + (side.scratch() if side else []),
        compiler_params=_cp("arbitrary", "arbitrary"), name=name,
    )(proj, proj, proj, do, cum_t, lse, delta, *(side.inputs if side else []))


def _head_norm_bwd(dn_in, o, g, gr_src, *, nh, hd, dn_col, gr_col, name):
    s, w = o.shape
    ts = min(ROW_TILE, s)
    gated = gr_src is not None

    def body(*refs):
        if gated:
            dn_ref, o_ref, g_ref, gr_ref, do_ref, dgr_ref, dl_ref, dg_ref = refs
        else:
            dn_ref, o_ref, g_ref, do_ref, dl_ref, dg_ref = refs

        @pl.when(pl.program_id(0) == 0)
        def _():
            dg_ref[...] = jnp.zeros_like(dg_ref)

        lane = lax.broadcasted_iota(jnp.int32, (ts, 128), 1)
        delta = jnp.zeros((ts, 128), F32)
        for h in range(nh):
            sl = _hs(h, hd)
            ov = o_ref[:, sl]
            dnv = dn_ref[:, sl].astype(F32)
            gv = g_ref[h:h + 1, :]
            r = lax.rsqrt(jnp.mean(ov * ov, axis=-1, keepdims=True) + EPS)
            ohat = ov * r
            if gated:
                grv = gr_ref[:, sl].astype(F32)
                sig = _sigmoid(grv)
                dgr_ref[:, sl] = (dnv * (ohat * gv) * (sig * (1.0 + grv * (1.0 - sig)))).astype(BF16)
                dnv = dnv * (grv * sig)
            dg_ref[h:h + 1, :] += jnp.sum(dnv * ohat, axis=0, keepdims=True)
            dohat = dnv * gv
            do = r * (dohat - ohat * jnp.mean(dohat * ohat, axis=-1, keepdims=True))
            do_ref[:, sl] = do.astype(BF16)
            delta = jnp.where(lane == h, jnp.sum(do.astype(BF16).astype(F32) * ov, axis=-1, keepdims=True), delta)
        dl_ref[...] = delta

    in_specs = [pl.BlockSpec((ts, w), lambda i: (i, dn_col)), _row_spec(ts, w),
                pl.BlockSpec((nh, hd), lambda i: (0, 0))]
    args = [dn_in, o, g]
    out_specs = [_row_spec(ts, w)]
    out_shape = [jax.ShapeDtypeStruct((s, w), BF16)]
    if gated:
        in_specs.append(pl.BlockSpec((ts, w), lambda i: (i, gr_col)))
        args.append(gr_src)
        out_specs.append(_row_spec(ts, w))
        out_shape.append(jax.ShapeDtypeStruct((s, w), BF16))
    out_specs += [_row_spec(ts, 128), pl.BlockSpec((nh, hd), lambda i: (0, 0))]
    out_shape += [jax.ShapeDtypeStruct((s, 128), F32), jax.ShapeDtypeStruct((nh, hd), F32)]
    return pl.pallas_call(
        body, grid=(s // ts,), in_specs=in_specs, out_specs=out_specs, out_shape=out_shape,
        compiler_params=_cp("arbitrary"), name=name,
    )(*args)


GQ_BLK = 3 * FOX_W // GLA_DK
GK_BLK = GQ_BLK + GLA_HEADS
GV_BLK = (3 * FOX_W + 2 * GLA_KW) // GLA_DV
GR_BLK = GV_BLK + GLA_HEADS


def _gla_chunk_terms(la):
    cum = _dot_nn(_tri(CHUNK), la, HIGHEST)
    total = cum[CHUNK - 1:CHUNK, :]
    return jnp.exp(total - cum), jnp.exp(total)


def _gla_fwd(proj, log_a, g_gla, *, name):
    s = proj.shape[0]
    rows = min(GLA_ROWS, s)
    cb = rows // CHUNK
    nblk = s // rows
    scale = GLA_DK ** -0.5

    def body(q_ref, k_ref, v_ref, gr_ref, la_ref, g_ref, o_ref, n_ref, st_ref, st_sc):
        h = pl.program_id(0)

        @pl.when(pl.program_id(1) == 0)
        def _():
            st_sc[...] = jnp.zeros_like(st_sc)

        gv = g_ref[pl.ds(h, 1), :]
        for ci in range(cb):
            sl = slice(ci * CHUNK, (ci + 1) * CHUNK)
            e, dec = _gla_chunk_terms(la_ref[sl, :])
            k_dec = (k_ref[sl, :].astype(F32) * e).astype(BF16)
            st = st_sc[...] * dec + _dot_tn(v_ref[sl, :], k_dec)
            st_sc[...] = st
            st_ref[0, ci] = st
            qs = (q_ref[sl, :].astype(F32) * scale).astype(BF16)
            o = _dot_nt(qs, st.astype(BF16))
            o_ref[sl, :] = o
            r = lax.rsqrt(jnp.mean(o * o, axis=-1, keepdims=True) + EPS)
            grv = gr_ref[sl, :].astype(F32)
            n_ref[sl, :] = (o * r * gv * (grv * _sigmoid(grv))).astype(BF16)

    return pl.pallas_call(
        body, grid=(GLA_HEADS, nblk),
        in_specs=[pl.BlockSpec((rows, GLA_DK), lambda h, n: (n, GQ_BLK + h)),
                  pl.BlockSpec((rows, GLA_DK), lambda h, n: (n, GK_BLK + h)),
                  pl.BlockSpec((rows, GLA_DV), lambda h, n: (n, GV_BLK + h)),
                  pl.BlockSpec((rows, GLA_DV), lambda h, n: (n, GR_BLK + h)),
                  pl.BlockSpec((rows, GLA_DK), lambda h, n: (n, h)),
                  pl.BlockSpec((GLA_HEADS, GLA_DV), lambda h, n: (0, 0))],
        out_specs=[pl.BlockSpec((rows, GLA_DV), lambda h, n: (n, h)),
                   pl.BlockSpec((rows, GLA_DV), lambda h, n: (n, h)),
                   pl.BlockSpec((1, cb, GLA_DV, GLA_DK), lambda h, n: (h, n, 0, 0))],
        out_shape=[jax.ShapeDtypeStruct((s, GLA_W), F32), jax.ShapeDtypeStruct((s, GLA_W), BF16),
                   jax.ShapeDtypeStruct((GLA_HEADS, s // CHUNK, GLA_DV, GLA_DK), F32)],
        scratch_shapes=[pltpu.VMEM((GLA_DV, GLA_DK), F32)],
        compiler_params=_cp("parallel", "arbitrary"), name=name,
    )(proj, proj, proj, proj, log_a, g_gla)


def _gla_bwd(proj, log_a, do, states, *, name, side=None):
    s = proj.shape[0]
    rows = min(GLA_ROWS, s)
    cb = rows // CHUNK
    nblk = s // rows
    scale = GLA_DK ** -0.5
    n_si = len(side.inputs) if side else 0
    n_so = len(side.out_shapes) if side else 0

    def body(*refs):
        q_ref, k_ref, v_ref, la_ref, do_ref, st_ref, prev_ref = refs[:7]
        dq_ref, dk_ref, dv_ref, dla_ref = refs[7 + n_si:11 + n_si]
        g_sc = refs[11 + n_si + n_so]
        nrev = pl.program_id(1)
        blk = nblk - 1 - nrev
        if side:
            hh = pl.program_id(0)
            side.run(refs[7:7 + n_si], refs[11 + n_si:11 + n_si + n_so], refs[12 + n_si + n_so:],
                     (hh == 0) & (nrev == 0), (hh == GLA_HEADS - 1) & (nrev == nblk - 1))

        @pl.when(nrev == 0)
        def _():
            g_sc[...] = jnp.zeros_like(g_sc)

        for ci in reversed(range(cb)):
            sl = slice(ci * CHUNK, (ci + 1) * CHUNK)
            e, dec = _gla_chunk_terms(la_ref[sl, :])
            kd = k_ref[sl, :].astype(F32) * e
            qs = (q_ref[sl, :].astype(F32) * scale).astype(BF16)
            dov = do_ref[sl, :]
            st = st_ref[0, ci]
            if ci > 0:
                st_prev = st_ref[0, ci - 1]
            else:
                st_prev = prev_ref[0, 0] * (blk > 0).astype(F32)
            dq_ref[sl, :] = (_dot_nn(dov, st.astype(BF16)) * scale).astype(BF16)
            gt = g_sc[...] + _dot_tn(dov, qs)
            gtb = gt.astype(BF16)
            dkd = _dot_nn(v_ref[sl, :], gtb)
            dv_ref[sl, :] = _dot_nt(kd.astype(BF16), gtb).astype(BF16)
            dk_ref[sl, :] = (dkd * e).astype(BF16)
            ddec = jnp.sum(gt * st_prev, axis=0, keepdims=True) * dec
            dla_ref[sl, :] = _dot_nn(_tri(CHUNK, strict=True), dkd * kd, HIGHEST) + ddec
            g_sc[...] = gt * dec

    rev = lambda col0: (lambda h, n: (nblk - 1 - n, col0 + h))
    return pl.pallas_call(
        body, grid=(GLA_HEADS, nblk),
        in_specs=[pl.BlockSpec((rows, GLA_DK), rev(GQ_BLK)),
                  pl.BlockSpec((rows, GLA_DK), rev(GK_BLK)),
                  pl.BlockSpec((rows, GLA_DV), rev(GV_BLK)),
                  pl.BlockSpec((rows, GLA_DK), rev(0)),
                  pl.BlockSpec((rows, GLA_DV), rev(0)),
                  pl.BlockSpec((1, cb, GLA_DV, GLA_DK), lambda h, n: (h, nblk - 1 - n, 0, 0)),
                  pl.BlockSpec((1, 1, GLA_DV, GLA_DK),
                               lambda h, n: (h, jnp.maximum((nblk - 1 - n) * cb - 1, 0), 0, 0))]
        + [pl.BlockSpec(memory_space=pl.ANY)] * n_si,
        out_specs=[pl.BlockSpec((rows, GLA_DK), rev(0)), pl.BlockSpec((rows, GLA_DK), rev(0)),
                   pl.BlockSpec((rows, GLA_DV), rev(0)), pl.BlockSpec((rows, GLA_DK), rev(0))]
        + [pl.BlockSpec(memory_space=pl.ANY)] * n_so,
        out_shape=[jax.ShapeDtypeStruct((s, GLA_KW), BF16), jax.ShapeDtypeStruct((s, GLA_KW), BF16),
                   jax.ShapeDtypeStruct((s, GLA_W), BF16), jax.ShapeDtypeStruct((s, GLA_KW), F32)]
        + (side.out_shapes if side else []),
        scratch_shapes=[pltpu.VMEM((GLA_DV, GLA_DK), F32)] + (side.scratch() if side else []),
        compiler_params=_cp("arbitrary", "arbitrary"), name=name,
    )(proj, proj, proj, log_a, do, states, states, *(side.inputs if side else []))


def _row_tile(r):
    tr = min(ROW_TILE, r)
    while r % tr or tr % 8:
        tr -= 1
    return tr


def _adamw_math(w, g, m, v):
    m = ADAM_B1 * m + (1.0 - ADAM_B1) * g
    v = ADAM_B2 * v + (1.0 - ADAM_B2) * (g * g)
    m_hat = m / (1.0 - ADAM_B1 ** ADAM_STEP)
    v_hat = v / (1.0 - ADAM_B2 ** ADAM_STEP)
    delta = -ADAM_LR * (m_hat / (jnp.sqrt(v_hat) + ADAM_EPS) + ADAM_WD * w)
    return delta, m, v


COL_TILE = 256


def _tile_2d(r, c):
    if r % 8 == 0 and _row_tile(r) >= 64:
        return _row_tile(r), c
    assert c % COL_TILE == 0, (r, c)
    return r, COL_TILE


def _half_shape(shape):
    r, c = shape[-2:]
    return tuple(shape[:-2]) + ((r // 2, c) if _half_axis(r) == 0 else (r, c // 2))


def _adam(g, w, m, v, *, name):
    r, c = w.shape
    tr, tc = _tile_2d(r, c)

    def body(g_ref, w_ref, m_ref, v_ref, d_ref, mo_ref, vo_ref):
        d, mn, vn = _adamw_math(w_ref[...], g_ref[...], m_ref[...], v_ref[...])
        d_ref[...] = d
        mo_ref[...] = mn
        vo_ref[...] = vn

    spec = pl.BlockSpec((tr, tc), lambda i, j: (i, j))
    return pl.pallas_call(
        body, grid=(r // tr, c // tc), in_specs=[spec] * 4, out_specs=[spec] * 3,
        out_shape=[jax.ShapeDtypeStruct((r, c), F32)] * 3,
        compiler_params=_cp("parallel", "parallel"), name=name,
    )(g, w, m, v)


def _ada_grad_adam(c_all_t, dmod_cols, w, m, v, *, name):
    r, c = w.shape
    tr, tc = min(512, r), min(1024, c)

    def body(ct_ref, dm_ref, w_ref, m_ref, v_ref, g_ref, d_ref, mo_ref, vo_ref):
        g = _dot_nn(ct_ref[...], dm_ref[...], HIGHEST)
        g_ref[...] = g
        d, mn, vn = _adamw_math(w_ref[...], g, m_ref[...], v_ref[...])
        d_ref[...] = d
        mo_ref[...] = mn
        vo_ref[...] = vn

    spec = pl.BlockSpec((tr, tc), lambda i, j: (i, j))
    nb = c_all_t.shape[1]
    return pl.pallas_call(
        body, grid=(r // tr, c // tc),
        in_specs=[pl.BlockSpec((tr, nb), lambda i, j: (i, 0)), pl.BlockSpec((nb, tc), lambda i, j: (0, j)),
                  spec, spec, spec],
        out_specs=[spec] * 4, out_shape=[jax.ShapeDtypeStruct((r, c), F32)] * 4,
        compiler_params=_cp("parallel", "parallel"), name=name,
    )(c_all_t, dmod_cols, w, m, v)


def _mod_shard(c_all, w, b, *, name):
    k, c = w.shape
    tc = min(512, c)
    nb = c_all.shape[0]

    def body(c_ref, w_ref, b_ref, o_ref):
        o_ref[...] = _dot_nn(c_ref[...], w_ref[...], HIGHEST) + b_ref[...]

    return pl.pallas_call(
        body, grid=(c // tc,),
        in_specs=[pl.BlockSpec((nb, k), lambda j: (0, 0)), pl.BlockSpec((k, tc), lambda j: (0, j)),
                  pl.BlockSpec((1, tc), lambda j: (0, j))],
        out_specs=pl.BlockSpec((nb, tc), lambda j: (0, j)),
        out_shape=jax.ShapeDtypeStruct((nb, c), F32),
        compiler_params=_cp("parallel"), name=name,
    )(c_all, w, b)


def _silu_rows(c, *, name):
    def body(c_ref, o_ref):
        cv = c_ref[...]
        o_ref[...] = cv * _sigmoid(cv)

    return pl.pallas_call(body, out_shape=jax.ShapeDtypeStruct(c.shape, F32), name=name)(c)


def _pair_sum(g, got, idx, *, name):
    p, r, c = g.shape
    ax = _half_axis(r)
    hr, hc = _half_shape((r, c))
    tr, tc = _tile_2d(hr, hc)
    nbr, nbc = hr // tr, hc // tc

    def body(idx_ref, a_ref, b_ref, o_ref):
        o_ref[...] = (a_ref[...].astype(F32) + b_ref[...].astype(F32)).astype(BF16)

    def own_map(i, j, k, idx_ref):
        return (i, j + (idx_ref[0] * nbr if ax == 0 else 0), k + (idx_ref[0] * nbc if ax == 1 else 0))

    half_spec = pl.BlockSpec((1, tr, tc), lambda i, j, k, idx_ref: (i, j, k))
    return pl.pallas_call(
        body,
        grid_spec=pltpu.PrefetchScalarGridSpec(
            num_scalar_prefetch=1, grid=(p, nbr, nbc),
            in_specs=[pl.BlockSpec((1, tr, tc), own_map), half_spec],
            out_specs=half_spec),
        out_shape=jax.ShapeDtypeStruct((p, hr, hc), BF16),
        compiler_params=_cp("parallel", "parallel", "parallel"), name=name,
    )(idx, g, got)


def _final_sum(own, parts, idx, shard_shape, *, name):
    ax = _half_axis(shard_shape[0])
    hr, hc = own.shape[1:]
    tr, tc = _tile_2d(hr, hc)
    nbr, nbc = hr // tr, hc // tc

    def body(idx_ref, own_ref, parts_ref, o_ref):
        acc = own_ref[0].astype(F32)
        for q in range(3):
            acc = acc + parts_ref[q].astype(F32)
        o_ref[...] = acc

    def out_map(j, k, idx_ref):
        return (j + (idx_ref[0] * nbr if ax == 0 else 0), k + (idx_ref[0] * nbc if ax == 1 else 0))

    return pl.pallas_call(
        body,
        grid_spec=pltpu.PrefetchScalarGridSpec(
            num_scalar_prefetch=1, grid=(nbr, nbc),
            in_specs=[pl.BlockSpec((1, tr, tc), lambda j, k, idx_ref: (idx_ref[1], j, k)),
                      pl.BlockSpec((3, tr, tc), lambda j, k, idx_ref: (0, j, k))],
            out_specs=pl.BlockSpec((tr, tc), out_map)),
        out_shape=jax.ShapeDtypeStruct(tuple(shard_shape), F32),
        compiler_params=_cp("parallel", "parallel"), name=name,
    )(idx, own, parts)


def _stack_sum(x, *, name):
    p, r, c = x.shape
    tr = _row_tile(r)

    def body(x_ref, o_ref):
        acc = x_ref[0].astype(F32)
        for q in range(1, p):
            acc = acc + x_ref[q].astype(F32)
        o_ref[...] = acc

    return pl.pallas_call(
        body, grid=(r // tr,),
        in_specs=[pl.BlockSpec((p, tr, c), lambda i: (0, i, 0))],
        out_specs=pl.BlockSpec((tr, c), lambda i: (i, 0)),
        out_shape=jax.ShapeDtypeStruct((r, c), F32),
        compiler_params=_cp("parallel"), name=name,
    )(x)


def _place():
    x, y, c = lax.axis_index("x"), lax.axis_index("y"), lax.axis_index("c")
    chips = [(1 - x, y), (x, 1 - y), (1 - x, 1 - y)]
    return x, y, c, chips


def _gather8(x_shard, *, name):
    m_per, n = x_shard.shape

    def body(x_ref, out_ref, send_sems, recv_sems, local_sem):
        x, y, c, chips = _place()
        me, sibling = (x, y, c), (x, y, 1 - c)

        def rows(px, py, pc):
            return out_ref.at[pl.ds((4 * px + 2 * py + pc) * m_per, m_per), :]

        def copy(k, block, to, src=None):
            return pltpu.make_async_remote_copy(
                src_ref=rows(*block) if src is None else src, dst_ref=rows(*block),
                send_sem=send_sems.at[k], recv_sem=recv_sems.at[k], device_id=to, device_id_type=MESH)

        mine = pltpu.make_async_copy(x_ref, rows(*me), local_sem)
        mine.start()
        first = [copy(0, me, sibling, src=x_ref)]
        first += [copy(1 + j, me, (*chip, c), src=x_ref) for j, chip in enumerate(chips)]
        for cp in first:
            cp.start()
        passed = [copy(4 + j, (*chip, c), sibling) for j, chip in enumerate(chips)]
        for j, chip in enumerate(chips):
            copy(1 + j, (*chip, c), me).wait_recv()
            passed[j].start()
        copy(0, sibling, me).wait_recv()
        for j, chip in enumerate(chips):
            copy(4 + j, (*chip, 1 - c), me).wait_recv()
        for cp in first + passed:
            cp.wait_send()
        mine.wait()

    return pl.pallas_call(
        body,
        out_shape=jax.ShapeDtypeStruct((8 * m_per, n), x_shard.dtype),
        in_specs=[pl.BlockSpec(memory_space=pltpu.VMEM)],
        out_specs=pl.BlockSpec(memory_space=pltpu.VMEM),
        scratch_shapes=[pltpu.SemaphoreType.DMA((7,)), pltpu.SemaphoreType.DMA((7,)), pltpu.SemaphoreType.DMA],
        name=name,
    )(x_shard)


def _gather_weights(shards, *, name):
    return _comm_call(lambda ins, outs: [cp for i, o in zip(ins, outs) for cp in _plan_gather_ici(i, o)],
                      shards, [jax.ShapeDtypeStruct((4,) + s.shape, s.dtype) for s in shards], name=name)


def _plan_start(plan, send_sems, recv_sems):
    for k, (src, dst, _, peer) in enumerate(plan):
        pltpu.make_async_remote_copy(src_ref=src, dst_ref=dst, send_sem=send_sems.at[k], recv_sem=recv_sems.at[k],
                                     device_id=peer, device_id_type=MESH).start()


def _plan_wait(plan, send_sems, recv_sems):
    for k, (src, _, land, peer) in enumerate(plan):
        pltpu.make_async_remote_copy(src_ref=src, dst_ref=land, send_sem=send_sems.at[k], recv_sem=recv_sems.at[k],
                                     device_id=peer, device_id_type=MESH).wait_recv()
    for k, (src, dst, _, peer) in enumerate(plan):
        pltpu.make_async_remote_copy(src_ref=src, dst_ref=dst, send_sem=send_sems.at[k], recv_sem=recv_sems.at[k],
                                     device_id=peer, device_id_type=MESH).wait_send()


def _half_axis(rows):
    return 0 if rows % 32 == 0 else 1


def _rows_half(ref, hc, axis, part=None):
    size = ref.shape[axis] // 2
    start = hc * size
    if part is not None:
        size //= part[1]
        start = start + part[0] * size
    idx = [slice(None)] * len(ref.shape)
    idx[axis] = pl.ds(start, size)
    return ref.at[tuple(idx)]


def _plan_gather_ici(shard, full, part=None):
    x, y, c, chips = _place()
    ax = _half_axis(shard.shape[0])
    src = _rows_half(shard, c, ax, part)
    return [(src, _rows_half(full.at[2 * x + y], c, ax, part), _rows_half(full.at[2 * cx + cy], c, ax, part),
             (cx, cy, c)) for cx, cy in chips]


def _plan_gather_d2d(full):
    x, y, c, chips = _place()
    ax = _half_axis(full.shape[1])
    plan = []
    for cx, cy in chips:
        slot = full.at[2 * cx + cy]
        plan.append((_rows_half(slot, c, ax), _rows_half(slot, c, ax), _rows_half(slot, 1 - c, ax), (x, y, 1 - c)))
    return plan


def _plan_pair(grad, got):
    x, y, c, _ = _place()
    return [(_rows_half(grad, 1 - c, 1 + _half_axis(grad.shape[1])), got, got, (x, y, 1 - c))]


def _plan_shard_ici(sums, parts, piece=None):
    _, _, c, chips = _place()

    def rows(ref):
        if piece is None:
            return ref
        k, n = piece
        if ref.shape[0] % (16 * n) == 0:
            size = ref.shape[0] // n
            return ref.at[pl.ds(k * size, size), :]
        size = ref.shape[1] // n
        return ref.at[:, pl.ds(k * size, size)]

    return [(rows(sums.at[2 * cx + cy]), rows(parts.at[k]), rows(parts.at[k]), (cx, cy, c))
            for k, (cx, cy) in enumerate(chips)]


def _plan_half(buf):
    x, y, c, _ = _place()
    ax = _half_axis(buf.shape[0])
    mine = _rows_half(buf, c, ax)
    return [(mine, mine, _rows_half(buf, 1 - c, ax), (x, y, 1 - c))]


def _comm_call(plan_fn, inputs, out_shapes, *, name, aliases=None):
    ni, no = len(inputs), len(out_shapes)

    def body(*refs):
        plan = plan_fn(refs[:ni], refs[ni:ni + no])
        send_sems, recv_sems = refs[ni + no:]
        _plan_start(plan, send_sems, recv_sems)
        _plan_wait(plan, send_sems, recv_sems)

    any_spec = pl.BlockSpec(memory_space=pl.ANY)
    n_copies = 3 * max(ni, no)
    return pl.pallas_call(
        body, out_shape=list(out_shapes), in_specs=[any_spec] * ni, out_specs=[any_spec] * no,
        scratch_shapes=[pltpu.SemaphoreType.DMA((n_copies,)), pltpu.SemaphoreType.DMA((n_copies,))],
        input_output_aliases=aliases or {}, name=name,
    )(*inputs)


def _gather_forward(fulls, *, name):
    return _comm_call(lambda ins, outs: [cp for o in outs for cp in _plan_gather_d2d(o)],
                      fulls, [jax.ShapeDtypeStruct(f.shape, f.dtype) for f in fulls], name=name,
                      aliases={k: k for k in range(len(fulls))})


def _half_exchange(bufs, *, name):
    return _comm_call(lambda ins, outs: [cp for o in outs for cp in _plan_half(o)],
                      bufs, [jax.ShapeDtypeStruct(b.shape, b.dtype) for b in bufs], name=name,
                      aliases={k: k for k in range(len(bufs))})


def _split_w_in(w_in_t):
    d = w_in_t.shape[1]
    main = jnp.concatenate([w_in_t[0:3072], w_in_t[3080:5128], w_in_t[5144:6168]], axis=0)
    small = jnp.concatenate([w_in_t[3072:3080], w_in_t[5128:5144], jnp.zeros((SMALL_W - 24, d), w_in_t.dtype)], axis=0)
    return main, small


def _merge_dw_in(dw_main, dw_small):
    return jnp.concatenate([dw_main[0:3072], dw_small[0:8], dw_main[3072:5120], dw_small[8:24], dw_main[5120:6144]],
                           axis=0)


def _gather_side(shards):
    return _Side(shards, [jax.ShapeDtypeStruct((4,) + w.shape, w.dtype) for w in shards],
                 lambda ins, outs: [cp for i, o in zip(ins, outs) for cp in _plan_gather_ici(i, o)], 3 * len(shards))


def _finish_gather(fulls, owns, chip, *, name):
    fulls = _gather_forward(list(fulls), name=name)
    return [lax.dynamic_update_index_in_dim(f, o, chip, 0) for f, o in zip(fulls, owns)]


def _forward_side(full):
    return _Side([full], [jax.ShapeDtypeStruct(full.shape, full.dtype)],
                 lambda ins, outs: _plan_gather_d2d(outs[0]), 3, aliases={0: 0})


def _half_side(bufs):
    return _Side(bufs, [jax.ShapeDtypeStruct(b.shape, b.dtype) for b in bufs],
                 lambda ins, outs: [cp for o in outs for cp in _plan_half(o)], len(bufs),
                 aliases={k: k for k in range(len(bufs))})


def _parts_shape(sums):
    return jax.ShapeDtypeStruct((3,) + sums.shape[1:], sums.dtype)


def _got_shape(grad):
    return jax.ShapeDtypeStruct(_half_shape(grad.shape), grad.dtype)


def _pair_side(grad):
    return _Side([grad], [_got_shape(grad)], lambda ins, outs: _plan_pair(ins[0], outs[0]), 1)


def _local_step(x, target, mod, g_pre_mix, g_post_mix, g_pre_mlp, g_post_mlp, gw_in, b_fgate, w_gla_a2,
                b_gla_a2, g_fox, g_gla, own_w_in, own_w_out, own_w_mlp_in, own_w_mlp_out, chip, idx):
    s, d = x.shape
    shift_m, scale_m, gate_m, shift_f, scale_f, gate_f = [mod[:, i * d:(i + 1) * d] for i in range(6)]
    a1 = g_pre_mix * (1.0 + scale_m)
    a2 = g_pre_mlp * (1.0 + scale_f)
    bf = jnp.concatenate([b_fgate, jnp.zeros((1, SMALL_W - FOX_HEADS), F32)], axis=1)
    w2p = jnp.zeros((SMALL_W, GLA_KW), F32).at[FOX_HEADS:FOX_HEADS + GLA_RANK].set(w_gla_a2)

    h1, gw_in = _pre_fwd(x, a1, shift_m, name="pre_mix_fwd", side=_forward_side(gw_in))
    w_in_t = lax.dynamic_update_index_in_dim(gw_in, own_w_in, chip, 0).reshape(-1, d)
    w_main, w_small = _split_w_in(w_in_t)
    full_shape = lambda w: jax.ShapeDtypeStruct((4,) + w.shape, w.dtype)
    first_side = _Side(
        [own_w_out, own_w_mlp_out], [full_shape(own_w_out), full_shape(own_w_mlp_out)],
        lambda ins, outs: _plan_gather_ici(ins[0], outs[0]) + _plan_gather_ici(ins[1], outs[1], part=(0, 4)), 6)
    proj, gw_out, gw_mlp_out = _mm(h1, w_main, mode="nt", out_dtypes=[BF16], name="in_proj_main", side=first_side)
    ps, gw_out = _mm(h1, w_small, mode="nt", out_dtypes=[F32], name="in_proj_small", side=_forward_side(gw_out))
    w_out_full = lax.dynamic_update_index_in_dim(gw_out, own_w_out, chip, 0).reshape(-1, d)
    cum, log_a = _gates_fwd(ps, bf, w2p, b_gla_a2, name="gates_fwd")
    cum_t = cum[:, :FOX_HEADS].T
    o_fox, fox_n, lse, gw_mlp_in = _fox_fwd(proj, cum_t, g_fox, name="fox_fwd", side=_gather_side([own_w_mlp_in]))
    o_gla, gla_n, states = _gla_fwd(proj, log_a, g_gla, name="gla_fwd")
    mixed = jnp.concatenate([fox_n, gla_n], axis=1)
    y1, gw_mlp_in = _mm(mixed, w_out_full, mode="nn", out_dtypes=[F32], name="out_proj",
                        side=_forward_side(gw_mlp_in))
    gw_mlp_in = lax.dynamic_update_index_in_dim(gw_mlp_in, own_w_mlp_in, chip, 0)
    x1, h2 = _post_pre_fwd(x, y1, gate_m, g_post_mix, a2, shift_f, name="post_mix_pre_mlp_fwd")

    def mlp_act(acc):
        r = jnp.maximum(acc, 0.0)
        return acc, r * r

    rest_side = _Side([own_w_mlp_out, gw_mlp_out], [full_shape(own_w_mlp_out)],
                      lambda ins, outs: [cp for q in (1, 2, 3) for cp in _plan_gather_ici(ins[0], outs[0], part=(q, 4))],
                      9, aliases={1: 0})
    u, act, gw_mlp_out = _mm(h2, gw_mlp_in, mode="nn", out_dtypes=[BF16, BF16], epi=mlp_act, name="mlp_in",
                             b_slots=4, tm=MM_TM, side=rest_side)
    gw_mlp_out, = _finish_gather([gw_mlp_out], [own_w_mlp_out], chip, name="gather_w_mlp_out_d2d")
    w_mlp_out_full = gw_mlp_out.reshape(-1, d)
    y2, = _mm(act, w_mlp_out_full, mode="nn", out_dtypes=[F32], name="mlp_out")
    dx2, dy2, loss_part, dgate_f, dg_post_mlp = _post_loss_bwd(x1, y2, gate_f, g_post_mlp, target,
                                                               name="post_mlp_loss_bwd")
    dw_mlp_out, = _mm(act, dy2, mode="tn", out_dtypes=[BF16], name="dw_mlp_out")
    dw_mlp_out = dw_mlp_out.reshape(4, D_FF // 4, d)

    def act_bwd(acc, uv):
        return (acc * (2.0 * jnp.maximum(uv.astype(F32), 0.0)),)

    du, got_mlp_out = _mm(dy2, w_mlp_out_full, mode="nt", out_dtypes=[BF16], extras=[u], epi=act_bwd,
                          name="d_mlp_hidden", tm=MM_TM, side=_pair_side(dw_mlp_out))
    sum_mlp_out = _pair_sum(dw_mlp_out, got_mlp_out, idx, name="grad_pair_sum_mlp_out")
    nj = D_FF // 4 // min(MM_T, D_FF // 4)
    tmw = min(MM_T, d)
    dw_mlp_in, parts_mlp_out = _mm(
        h2, du, mode="tn", out_dtypes=[BF16], name="dw_mlp_in",
        out_shapes=[jax.ShapeDtypeStruct((4, d, D_FF // 4), BF16)],
        out_specs=[pl.BlockSpec((1, tmw, min(MM_T, D_FF // 4)), lambda i, j, kk: (j // nj, i, j % nj))],
        side=_Side([sum_mlp_out], [_parts_shape(sum_mlp_out)],
                   lambda ins, outs: _plan_shard_ici(ins[0], outs[0], piece=(0, 2)), 3))
    dh2, got_mlp_in, parts_mlp_out = _mm(
        du, gw_mlp_in, mode="nt", out_dtypes=[F32], name="d_mlp_in", b_slots=4,
        side=_Side([dw_mlp_in, sum_mlp_out, parts_mlp_out], [_got_shape(dw_mlp_in), _parts_shape(sum_mlp_out)],
                   lambda ins, outs: _plan_pair(ins[0], outs[0]) + _plan_shard_ici(ins[1], outs[1], piece=(1, 2)),
                   4, aliases={2: 1}))
    sum_mlp_in = _pair_sum(dw_mlp_in, got_mlp_in, idx, name="grad_pair_sum_mlp_in")
    dx1, dshift_f, da2, dy1, dgate_m, dg_post_mix = _pre_post_bwd(dh2, x1, dx2, a2, y1, gate_m, g_post_mix,
                                                                  name="pre_mlp_post_mix_bwd")
    buf_mlp_out = _final_sum(sum_mlp_out, parts_mlp_out, idx, (D_FF // 4, d), name="grad_final_sum_mlp_out")
    dw_out, g_mlp_out = _mm(mixed, dy1, mode="tn", out_dtypes=[BF16], name="dw_out", side=_half_side([buf_mlp_out]))
    dw_out = dw_out.reshape(4, d // 4, d)
    dmixed, got_out = _mm(dy1, w_out_full, mode="nt", out_dtypes=[BF16], name="d_mixed", side=_pair_side(dw_out))
    sum_out = _pair_sum(dw_out, got_out, idx, name="grad_pair_sum_out")
    do_fox, delta, dg_fox = _head_norm_bwd(dmixed, o_fox, g_fox, None, nh=FOX_HEADS, hd=FOX_HD, dn_col=0,
                                           gr_col=0, name="fox_norm_bwd")
    do_gla, dgr, _, dg_gla = _head_norm_bwd(dmixed, o_gla, g_gla, proj, nh=GLA_HEADS, hd=GLA_DV, dn_col=1,
                                            gr_col=(3 * FOX_W + 2 * GLA_KW + GLA_W) // GLA_W, name="gla_norm_bwd")
    dq_fox, dk_fox, dv_fox, dcq, dck_t, parts_mlp_in, parts_out = _fox_bwd(
        proj, do_fox, cum_t, lse, delta, name="fox_bwd",
        side=_Side([sum_mlp_in, sum_out], [_parts_shape(sum_mlp_in), _parts_shape(sum_out)],
                   lambda ins, outs: _plan_shard_ici(ins[0], outs[0]) + _plan_shard_ici(ins[1], outs[1]), 6))
    dgq, dgk, dgv, dla = _gla_bwd(proj, log_a, do_gla, states, name="gla_bwd")
    dck = dcq + jnp.concatenate([dck_t.T, jnp.zeros((s, SMALL_W - FOX_HEADS), F32)], axis=1)
    dps, dbf, dw2p, db2 = _gates_bwd(dck, ps, bf, w2p, b_gla_a2, dla, name="gates_bwd")
    dproj = jnp.concatenate([dq_fox.astype(BF16), dk_fox, dv_fox, dgq, dgk, dgv, dgr], axis=1)
    buf_mlp_in = _final_sum(sum_mlp_in, parts_mlp_in, idx, (d, D_FF // 4), name="grad_final_sum_mlp_in")
    buf_out = _final_sum(sum_out, parts_out, idx, (d // 4, d), name="grad_final_sum_out")
    dw_main, g_mlp_in, g_out = _mm(dproj, h1, mode="tn", out_dtypes=[BF16], name="dw_in_main",
                                   side=_half_side([buf_mlp_in, buf_out]))
    dw_small, = _mm(dps, h1, mode="tn", out_dtypes=[BF16], name="dw_in_small")
    rs_in = w_in_t.shape[0] // 4
    dw_in = _merge_dw_in(dw_main, dw_small).reshape(4, rs_in, d)
    dh1_small, got_in = _mm(dps, w_small, mode="nn", out_dtypes=[F32], name="d_h1_small", side=_pair_side(dw_in))
    sum_in = _pair_sum(dw_in, got_in, idx, name="grad_pair_sum_in")
    dh1, parts_in = _mm(
        dproj, w_main, mode="nn", out_dtypes=[F32], extras=[dh1_small], epi=lambda acc, e: (acc + e,), name="d_h1",
        side=_Side([sum_in], [_parts_shape(sum_in)],
                   lambda ins, outs: [cp for q in range(3) for cp in _plan_shard_ici(ins[0], outs[0], piece=(q, 4))],
                   9))
    grad_x, dshift_m, da1, parts_in = _pre_bwd(
        dh1, x, dx1, a1, name="pre_mix_bwd",
        side=_Side([sum_in, parts_in], [_parts_shape(sum_in)],
                   lambda ins, outs: _plan_shard_ici(ins[0], outs[0], piece=(3, 4)), 3, aliases={1: 0}))
    buf_in = _final_sum(sum_in, parts_in, idx, (rs_in, d), name="grad_final_sum_in")
    g_in, = _half_exchange([buf_in], name="grad_half_exchange_in")
    g_big = [g_in, g_out, g_mlp_in, g_mlp_out]

    dmod = jnp.concatenate([dshift_m, da1 * g_pre_mix, dgate_m, dshift_f, da2 * g_pre_mlp, dgate_f], axis=1)
    small = dict(
        dmod=dmod, g_pre_mix=da1 * (1.0 + scale_m), g_post_mix=dg_post_mix, g_pre_mlp=da2 * (1.0 + scale_f),
        g_post_mlp=dg_post_mlp, b_fgate=dbf[:, :FOX_HEADS], w_gla_a2=dw2p[FOX_HEADS:FOX_HEADS + GLA_RANK],
        b_gla_a2=db2, g_fox_out=dg_fox, g_gla_out=dg_gla)
    return loss_part, grad_x, g_big, small


def _pack(arrays):
    flat = jnp.concatenate([a.reshape(-1).astype(F32) for a in arrays])
    n = flat.shape[0]
    rows = -(-n // 128)
    rows = -(-rows // 8) * 8
    return jnp.pad(flat, (0, rows * 128 - n)).reshape(rows, 128)


def _unpack(buf, shapes):
    flat = buf.reshape(-1)
    out, off = [], 0
    for shp in shapes:
        n = 1
        for q in shp:
            n *= q
        out.append(flat[off:off + n].reshape(shp))
        off += n
    return out


SMALL_GRAD_ORDER = ["dmod", "g_pre_mix", "g_post_mix", "g_pre_mlp", "g_post_mlp", "b_fgate", "w_gla_a2", "b_gla_a2",
                    "g_fox_out", "g_gla_out"]


def kernel(x, c, w_ada, b_ada, g_pre_mix, g_post_mix, w_in, b_fgate, w_gla_a2, b_gla_a2, g_fox_out, g_gla_out, w_out, g_pre_mlp, g_post_mlp, w_mlp_in, w_mlp_out, loss_target, m_w_ada, m_b_ada, m_g_pre_mix, m_g_post_mix, m_w_in, m_b_fgate, m_w_gla_a2, m_b_gla_a2, m_g_fox_out, m_g_gla_out, m_w_out, m_g_pre_mlp, m_g_post_mlp, m_w_mlp_in, m_w_mlp_out, v_w_ada, v_b_ada, v_g_pre_mix, v_g_post_mix, v_w_in, v_b_fgate, v_w_gla_a2, v_b_gla_a2, v_g_fox_out, v_g_gla_out, v_w_out, v_g_pre_mlp, v_g_post_mlp, v_w_mlp_in, v_w_mlp_out):
    ix, iy, ic = lax.axis_index("x"), lax.axis_index("y"), lax.axis_index("c")
    chip = 2 * ix + iy
    dev = 4 * ix + 2 * iy + ic
    d = D_MODEL

    c_act = _silu_rows(c, name="silu_c")
    pack1 = _pack([c_act, w_gla_a2[0], g_gla_out[0]])
    rows1 = pack1.shape[0]
    got1 = _gather8(pack1, name="gather_small_fwd").reshape(8, rows1, 128)
    per_dev = [_unpack(got1[q], [(d,), (GLA_RANK, GLA_KW // 4), (GLA_HEADS, GLA_DV // 4)]) for q in range(8)]
    c_all = jnp.stack([p[0] for p in per_dev])
    w_gla_a2_full = jnp.concatenate([per_dev[2 * j][1] for j in range(4)], axis=1)
    g_gla_full = jnp.concatenate([per_dev[2 * j][2] for j in range(4)], axis=1)
    cols = w_ada.shape[2]
    b_ada_shard = lax.dynamic_slice_in_dim(b_ada, chip * cols, cols, axis=1)
    mod_sh = _mod_shard(c_all, w_ada[0], b_ada_shard, name="ada_mod")
    got2 = _gather8(mod_sh, name="gather_mod").reshape(8, 8, cols)
    mod_all = jnp.concatenate([got2[2 * j] for j in range(4)], axis=1)
    mod = lax.dynamic_slice_in_dim(mod_all, dev, 1, axis=0)

    tr_in = lambda a: jnp.transpose(a[0])
    own_bf = [tr_in(w_in).astype(BF16), w_out[0].astype(BF16), w_mlp_in[0].astype(BF16), w_mlp_out[0].astype(BF16)]
    gw_in, = _gather_weights(own_bf[:1], name="gather_w_in_ici")
    idx = jnp.stack([ic, chip]).astype(jnp.int32)
    loss_part, grad_x, g_big, small = _local_step(
        x[0], loss_target[0], mod, g_pre_mix, g_post_mix, g_pre_mlp, g_post_mlp, gw_in, b_fgate,
        w_gla_a2_full, b_gla_a2, g_fox_out[0], g_gla_full, own_bf[0], own_bf[1], own_bf[2], own_bf[3], chip, idx)
    loss = lax.psum(loss_part[0, 0], ("x", "y", "c"))

    big_w = [(tr_in(w_in), tr_in(m_w_in), tr_in(v_w_in)), (w_out[0], m_w_out[0], v_w_out[0]),
             (w_mlp_in[0], m_w_mlp_in[0], v_w_mlp_in[0]), (w_mlp_out[0], m_w_mlp_out[0], v_w_mlp_out[0])]
    big_res = []
    for q, (g, (w, m, v)) in enumerate(zip(g_big, big_w)):
        res4 = (g,) + tuple(_adam(g, w, m, v, name=f"adam_big_{q}"))
        big_res.append(tuple((jnp.transpose(a) if q == 0 else a)[None] for a in res4))

    pack2 = _pack([small[k] for k in SMALL_GRAD_ORDER])
    rows2 = pack2.shape[0]
    got3 = _gather8(pack2, name="gather_small_grads").reshape(8, rows2, 128)
    dmod_all = got3[:, :6 * d // 128, :].reshape(8, 6 * d)
    sums = _stack_sum(got3, name="small_grad_sum")
    shapes = [(1, 6 * d), (1, d), (1, d), (1, d), (1, d), (1, FOX_HEADS), (1, GLA_RANK, GLA_KW), (1, GLA_KW),
              (1, FOX_HEADS, FOX_HD), (1, GLA_HEADS, GLA_DV)]
    sg = dict(zip(["b_ada"] + SMALL_GRAD_ORDER[1:], _unpack(sums, shapes)))
    sg["w_gla_a2"] = lax.dynamic_slice_in_dim(sg["w_gla_a2"], chip * (GLA_KW // 4), GLA_KW // 4, axis=2)
    sg["g_gla_out"] = lax.dynamic_slice_in_dim(sg["g_gla_out"], chip * (GLA_DV // 4), GLA_DV // 4, axis=2)
    small_names = ["b_ada", "g_pre_mix", "g_post_mix", "b_fgate", "w_gla_a2", "b_gla_a2", "g_fox_out", "g_gla_out",
                   "g_pre_mlp", "g_post_mlp"]
    small_w = dict(b_ada=(b_ada, m_b_ada, v_b_ada), g_pre_mix=(g_pre_mix, m_g_pre_mix, v_g_pre_mix),
                   g_post_mix=(g_post_mix, m_g_post_mix, v_g_post_mix), b_fgate=(b_fgate, m_b_fgate, v_b_fgate),
                   w_gla_a2=(w_gla_a2, m_w_gla_a2, v_w_gla_a2), b_gla_a2=(b_gla_a2, m_b_gla_a2, v_b_gla_a2),
                   g_fox_out=(g_fox_out, m_g_fox_out, v_g_fox_out), g_gla_out=(g_gla_out, m_g_gla_out, v_g_gla_out),
                   g_pre_mlp=(g_pre_mlp, m_g_pre_mlp, v_g_pre_mlp), g_post_mlp=(g_post_mlp, m_g_post_mlp, v_g_post_mlp))
    sshapes = [small_w[k][0].shape for k in small_names]
    pg = _pack([sg[k] for k in small_names])
    pw, pm, pv = [_pack([small_w[k][q] for k in small_names]) for q in range(3)]
    pd, pmn, pvn = _adam(pg, pw, pm, pv, name="adam_small")
    s_delta = dict(zip(small_names, _unpack(pd, sshapes)))
    s_m = dict(zip(small_names, _unpack(pmn, sshapes)))
    s_v = dict(zip(small_names, _unpack(pvn, sshapes)))

    dmod_cols = lax.dynamic_slice_in_dim(dmod_all, chip * cols, cols, axis=1)
    g_ada, d_ada, m_ada, v_ada = _ada_grad_adam(c_all.T, dmod_cols, w_ada[0], m_w_ada[0], v_w_ada[0], name="ada_grad_adam")

    order = ["w_ada", "b_ada", "g_pre_mix", "g_post_mix", "w_in", "b_fgate", "w_gla_a2", "b_gla_a2", "g_fox_out",
             "g_gla_out", "w_out", "g_pre_mlp", "g_post_mlp", "w_mlp_in", "w_mlp_out"]
    res = {"w_ada": (g_ada[None], d_ada[None], m_ada[None], v_ada[None]),
           "w_in": big_res[0], "w_out": big_res[1], "w_mlp_in": big_res[2], "w_mlp_out": big_res[3]}
    for k in small_names:
        res[k] = (sg[k], s_delta[k], s_m[k], s_v[k])
    return (loss, grad_x[None], *[res[k][0] for k in order], *[res[k][1] for k in order],
            *[res[k][2] for k in order], *[res[k][3] for k in order])
```

```python
import functools

import jax
import jax.numpy as jnp
from jax import lax
from jax.experimental import pallas as pl
from jax.experimental.pallas import tpu as pltpu

F32 = jnp.float32
BF16 = jnp.bfloat16
MESH = pl.DeviceIdType.MESH
HIGHEST = lax.Precision.HIGHEST

D_MODEL = 2048
FOX_HEADS = 8
FOX_HD = 128
FOX_W = FOX_HEADS * FOX_HD
GLA_HEADS = 4
GLA_DK = 128
GLA_DV = 256
GLA_KW = GLA_HEADS * GLA_DK
GLA_W = GLA_HEADS * GLA_DV
GLA_RANK = 16
GLA_TEMP = 16.0
CHUNK = 64
D_FF = 4 * D_MODEL
EPS = 1e-6
MAIN_W = 3 * FOX_W + 2 * GLA_KW + 2 * GLA_W
SMALL_W = 128
NEG = -1e30

ADAM_LR = 0.001
ADAM_B1 = 0.9
ADAM_B2 = 0.999
ADAM_EPS = 1e-08
ADAM_WD = 0.01
ADAM_STEP = 10

VMEM_LIMIT = 52 * 1024 * 1024
ROW_TILE = 256
FOX_TQ = 512
FOX_TK = 512
GLA_ROWS = 512
GATE_TS = 512
MM_T = 1024
MM_TK = 2048
MM_TM = 2048


def _cp(*sem):
    return pltpu.CompilerParams(dimension_semantics=sem, vmem_limit_bytes=VMEM_LIMIT)


def _dot_nn(a, b, precision=None):
    return jnp.dot(a, b, preferred_element_type=F32, precision=precision)


def _dot_nt(a, b, precision=None):
    return lax.dot_general(a, b, (((1,), (1,)), ((), ())), preferred_element_type=F32, precision=precision)


def _dot_tn(a, b, precision=None):
    return lax.dot_general(a, b, (((0,), (0,)), ((), ())), preferred_element_type=F32, precision=precision)


def _sigmoid(x):
    return 1.0 / (1.0 + jnp.exp(-x))


def _log_sigmoid(x):
    return jnp.minimum(x, 0.0) - jnp.log(1.0 + jnp.exp(-jnp.abs(x)))


class _Side:
    def __init__(self, inputs, out_shapes, plan_fn, n_copies, aliases=None):
        self.inputs, self.out_shapes, self.plan_fn, self.n_copies = list(inputs), list(out_shapes), plan_fn, n_copies
        self.aliases = dict(aliases or {})

    def scratch(self):
        return [pltpu.SemaphoreType.DMA((self.n_copies,)), pltpu.SemaphoreType.DMA((self.n_copies,))]

    def run(self, in_refs, out_refs, sems, first, last):
        @pl.when(first)
        def _():
            _plan_start(self.plan_fn(in_refs, out_refs), *sems)

        @pl.when(last)
        def _():
            _plan_wait(self.plan_fn(in_refs, out_refs), *sems)


def _mm(a, b, *, mode, out_dtypes, name, tm=None, tn=None, tk=None, extras=(), epi=None,
        out_shapes=None, out_specs=None, side=None, b_slots=0):
    tm, tn, tk = tm or MM_T, tn or MM_T, tk or MM_TK
    b2 = (b.shape[1], b_slots * b.shape[2]) if b_slots else b.shape
    if mode == "nn":
        (m, k), n = a.shape, b2[1]
    elif mode == "nt":
        (m, k), n = a.shape, b2[0]
    else:
        (k, m), n = a.shape, b2[1]
    tm, tn, tk = min(tm, m), min(tn, n), min(tk, k)
    if b_slots:
        tn = min(tn, b.shape[2]) if mode == "nn" else tn
        tk = min(tk, b.shape[2]) if mode == "nt" else tk
    assert m % tm == 0 and n % tn == 0 and k % tk == 0, (name, m, n, k)
    nk = k // tk
    n_out, n_ex = len(out_dtypes), len(extras)
    if epi is None:
        epi = lambda acc: tuple(acc for _ in range(n_out))
    dot = {"nn": _dot_nn, "nt": _dot_nt, "tn": _dot_tn}[mode]

    n_si = len(side.inputs) if side else 0
    n_so = len(side.out_shapes) if side else 0
    grid = (m // tm, n // tn, nk)

    def body(*refs):
        a_ref, b_ref = refs[0], refs[1]
        ex_refs = refs[2:2 + n_ex]
        base = 2 + n_ex + n_si
        o_refs = refs[base:base + n_out]
        scratch = refs[base + n_out + n_so:]
        if side:
            pos = [pl.program_id(q) for q in range(3)]
            first = (pos[0] == 0) & (pos[1] == 0) & (pos[2] == 0)
            last = (pos[0] == grid[0] - 1) & (pos[1] == grid[1] - 1) & (pos[2] == grid[2] - 1)
            side.run(refs[2 + n_ex:base], refs[base + n_out:base + n_out + n_so], scratch[-2:], first, last)
        part = dot(a_ref[...], b_ref[...])

        def finish(acc):
            outs = epi(acc, *[e[...] for e in ex_refs])
            for o_ref, val in zip(o_refs, outs):
                o_ref[...] = val.reshape(o_ref.shape).astype(o_ref.dtype)

        if nk == 1:
            finish(part)
        else:
            acc_ref = scratch[0]
            kk = pl.program_id(2)

            @pl.when(kk == 0)
            def _():
                acc_ref[...] = part

            @pl.when(kk > 0)
            def _():
                acc_ref[...] += part

            @pl.when(kk == nk - 1)
            def _():
                finish(acc_ref[...])

    if mode == "nn":
        a_spec = pl.BlockSpec((tm, tk), lambda i, j, kk: (i, kk))
        b_spec = pl.BlockSpec((tk, tn), lambda i, j, kk: (kk, j))
        if b_slots:
            per = b.shape[2] // tn
            b_spec = pl.BlockSpec((None, tk, tn), lambda i, j, kk: (j // per, kk, j % per))
    elif mode == "nt":
        a_spec = pl.BlockSpec((tm, tk), lambda i, j, kk: (i, kk))
        b_spec = pl.BlockSpec((tn, tk), lambda i, j, kk: (j, kk))
        if b_slots:
            per = b.shape[2] // tk
            b_spec = pl.BlockSpec((None, tn, tk), lambda i, j, kk: (kk // per, j, kk % per))
    else:
        assert not b_slots
        a_spec = pl.BlockSpec((tk, tm), lambda i, j, kk: (kk, i))
        b_spec = pl.BlockSpec((tk, tn), lambda i, j, kk: (kk, j))
    tile_spec = pl.BlockSpec((tm, tn), lambda i, j, kk: (i, j))
    if out_shapes is None:
        out_shapes = [jax.ShapeDtypeStruct((m, n), dt) for dt in out_dtypes]
    if out_specs is None:
        out_specs = [tile_spec for _ in out_dtypes]
    any_spec = pl.BlockSpec(memory_space=pl.ANY)
    res = pl.pallas_call(
        body,
        grid=grid,
        in_specs=[a_spec, b_spec] + [tile_spec for _ in extras] + [any_spec] * n_si,
        out_specs=list(out_specs) + [any_spec] * n_so,
        out_shape=list(out_shapes) + (side.out_shapes if side else []),
        scratch_shapes=([pltpu.VMEM((tm, tn), F32)] if nk > 1 else []) + (side.scratch() if side else []),
        compiler_params=_cp("arbitrary", "arbitrary", "arbitrary") if side else _cp("parallel", "parallel", "arbitrary"),
        input_output_aliases={2 + n_ex + si: n_out + so for si, so in side.aliases.items()} if side else {},
        name=name,
    )(a, b, *extras, *(side.inputs if side else []))
    return res


def _row_spec(ts, d):
    return pl.BlockSpec((ts, d), lambda i: (i, 0))


def _vec_spec(d):
    return pl.BlockSpec((1, d), lambda i: (0, 0))


def _side_args(side, n_in, n_out):
    if side is None:
        return [], [], [], [], [], {}
    any_spec = pl.BlockSpec(memory_space=pl.ANY)
    return ([any_spec] * len(side.inputs), [any_spec] * len(side.out_shapes), side.out_shapes, side.scratch(),
            side.inputs, {n_in + si: n_out + so for si, so in side.aliases.items()})


def _pre_fwd(x, avec, shift, *, name, side=None):
    s, d = x.shape
    ts = min(ROW_TILE, s)
    nb = s // ts
    s_in, s_out, s_shapes, s_scratch, s_ops, s_alias = _side_args(side, 3, 1)

    def body(x_ref, a_ref, s_ref, *rest):
        h_ref = rest[len(s_in)]
        if side:
            step = pl.program_id(0)
            side.run(rest[:len(s_in)], rest[len(s_in) + 1:len(s_in) + 1 + len(s_out)],
                     rest[len(s_in) + 1 + len(s_out):], step == 0, step == nb - 1)
        xv = x_ref[...]
        r = lax.rsqrt(jnp.mean(xv * xv, axis=-1, keepdims=True) + EPS)
        h_ref[...] = (xv * r * a_ref[...] + s_ref[...]).astype(BF16)

    res = pl.pallas_call(
        body, grid=(nb,),
        in_specs=[_row_spec(ts, d), _vec_spec(d), _vec_spec(d)] + s_in,
        out_specs=[_row_spec(ts, d)] + s_out,
        out_shape=[jax.ShapeDtypeStruct((s, d), BF16)] + s_shapes,
        scratch_shapes=s_scratch, input_output_aliases=s_alias,
        compiler_params=_cp("arbitrary" if side else "parallel"), name=name,
    )(x, avec, shift, *s_ops)
    return res if side else res[0]


def _post_pre_fwd(x, y, gate, g, avec, shift, *, name):
    s, d = x.shape
    ts = min(ROW_TILE, s)

    def body(x_ref, y_ref, gate_ref, g_ref, a_ref, s_ref, o_ref, h_ref):
        yv = y_ref[...]
        r = lax.rsqrt(jnp.mean(yv * yv, axis=-1, keepdims=True) + EPS)
        x1 = x_ref[...] + gate_ref[...] * (yv * r * g_ref[...])
        o_ref[...] = x1
        r1 = lax.rsqrt(jnp.mean(x1 * x1, axis=-1, keepdims=True) + EPS)
        h_ref[...] = (x1 * r1 * a_ref[...] + s_ref[...]).astype(BF16)

    return pl.pallas_call(
        body, grid=(s // ts,),
        in_specs=[_row_spec(ts, d), _row_spec(ts, d)] + [_vec_spec(d)] * 4,
        out_specs=[_row_spec(ts, d), _row_spec(ts, d)],
        out_shape=[jax.ShapeDtypeStruct((s, d), F32), jax.ShapeDtypeStruct((s, d), BF16)],
        compiler_params=_cp("parallel"), name=name,
    )(x, y, gate, g, avec, shift)


def _post_bwd_math(dxv, yv, gatev, gv):
    r = lax.rsqrt(jnp.mean(yv * yv, axis=-1, keepdims=True) + EPS)
    yhat = yv * r
    dn = dxv * gatev
    dyhat = dn * gv
    dy = r * (dyhat - yhat * jnp.mean(dyhat * yhat, axis=-1, keepdims=True))
    return dy, dxv * (yhat * gv), dn * yhat


def _accumulate(first, pairs):
    @pl.when(first)
    def _():
        for ref, _ in pairs:
            ref[...] = jnp.zeros_like(ref)

    for ref, val in pairs:
        ref[...] += jnp.sum(val, axis=0, keepdims=True)


def _post_loss_bwd(x, y, gate, g, target, *, name):
    s, d = x.shape
    ts = min(ROW_TILE, s)

    def body(x_ref, y_ref, gate_ref, g_ref, t_ref, dx_ref, dy_ref, loss_ref, dgate_ref, dg_ref):
        yv, gatev, gv = y_ref[...], gate_ref[...], g_ref[...]
        r = lax.rsqrt(jnp.mean(yv * yv, axis=-1, keepdims=True) + EPS)
        diff = x_ref[...] + gatev * (yv * r * gv) - t_ref[...]
        dxv = diff * (1.0 / d)
        dx_ref[...] = dxv
        dy, dgate_rows, dg_rows = _post_bwd_math(dxv, yv, gatev, gv)
        dy_ref[...] = dy.astype(BF16)
        first = pl.program_id(0) == 0
        _accumulate(first, [(dgate_ref, dgate_rows), (dg_ref, dg_rows)])

        @pl.when(first)
        def _():
            loss_ref[...] = jnp.zeros_like(loss_ref)

        loss_ref[...] += jnp.sum(jnp.mean(diff * diff, axis=-1, keepdims=True)) * 0.5

    return pl.pallas_call(
        body, grid=(s // ts,),
        in_specs=[_row_spec(ts, d), _row_spec(ts, d), _vec_spec(d), _vec_spec(d), _row_spec(ts, d)],
        out_specs=[_row_spec(ts, d), _row_spec(ts, d), pl.BlockSpec((1, 128), lambda i: (0, 0)), _vec_spec(d),
                   _vec_spec(d)],
        out_shape=[jax.ShapeDtypeStruct((s, d), F32), jax.ShapeDtypeStruct((s, d), BF16),
                   jax.ShapeDtypeStruct((1, 128), F32), jax.ShapeDtypeStruct((1, d), F32),
                   jax.ShapeDtypeStruct((1, d), F32)],
        compiler_params=_cp("arbitrary"), name=name,
    )(x, y, gate, g, target)


def _pre_post_bwd(dh, xin, dres, avec, y, gate, g, *, name):
    s, d = xin.shape
    ts = min(ROW_TILE, s)

    def body(dh_ref, x_ref, dres_ref, a_ref, y_ref, gate_ref, g_ref, dx_ref, dshift_ref, da_ref, dy_ref,
             dgate_ref, dg_ref):
        xv, dhv = x_ref[...], dh_ref[...]
        r = lax.rsqrt(jnp.mean(xv * xv, axis=-1, keepdims=True) + EPS)
        xhat = xv * r
        dxhat = dhv * a_ref[...]
        dxv = dres_ref[...] + r * (dxhat - xhat * jnp.mean(dxhat * xhat, axis=-1, keepdims=True))
        dx_ref[...] = dxv
        dy, dgate_rows, dg_rows = _post_bwd_math(dxv, y_ref[...], gate_ref[...], g_ref[...])
        dy_ref[...] = dy.astype(BF16)
        _accumulate(pl.program_id(0) == 0, [(dshift_ref, dhv), (da_ref, dhv * xhat), (dgate_ref, dgate_rows),
                                            (dg_ref, dg_rows)])

    return pl.pallas_call(
        body, grid=(s // ts,),
        in_specs=[_row_spec(ts, d), _row_spec(ts, d), _row_spec(ts, d), _vec_spec(d), _row_spec(ts, d),
                  _vec_spec(d), _vec_spec(d)],
        out_specs=[_row_spec(ts, d), _vec_spec(d), _vec_spec(d), _row_spec(ts, d), _vec_spec(d), _vec_spec(d)],
        out_shape=[jax.ShapeDtypeStruct((s, d), F32), jax.ShapeDtypeStruct((1, d), F32),
                   jax.ShapeDtypeStruct((1, d), F32), jax.ShapeDtypeStruct((s, d), BF16),
                   jax.ShapeDtypeStruct((1, d), F32), jax.ShapeDtypeStruct((1, d), F32)],
        compiler_params=_cp("arbitrary"), name=name,
    )(dh, xin, dres, avec, y, gate, g)


def _pre_bwd(dh, xin, dres, avec, *, name, side=None):
    s, d = xin.shape
    ts = min(ROW_TILE, s)
    nb = s // ts
    s_in, s_out, s_shapes, s_scratch, s_ops, s_alias = _side_args(side, 4, 3)

    def body(dh_ref, x_ref, dres_ref, a_ref, *rest):
        dx_ref, dshift_ref, da_ref = rest[len(s_in):len(s_in) + 3]
        if side:
            step = pl.program_id(0)
            side.run(rest[:len(s_in)], rest[len(s_in) + 3:len(s_in) + 3 + len(s_out)],
                     rest[len(s_in) + 3 + len(s_out):], step == 0, step == nb - 1)
        xv, dhv = x_ref[...], dh_ref[...]
        r = lax.rsqrt(jnp.mean(xv * xv, axis=-1, keepdims=True) + EPS)
        xhat = xv * r
        dxhat = dhv * a_ref[...]
        dx_ref[...] = dres_ref[...] + r * (dxhat - xhat * jnp.mean(dxhat * xhat, axis=-1, keepdims=True))

        @pl.when(pl.program_id(0) == 0)
        def _():
            dshift_ref[...] = jnp.zeros_like(dshift_ref)
            da_ref[...] = jnp.zeros_like(da_ref)

        dshift_ref[...] += jnp.sum(dhv, axis=0, keepdims=True)
        da_ref[...] += jnp.sum(dhv * xhat, axis=0, keepdims=True)

    return pl.pallas_call(
        body, grid=(nb,),
        in_specs=[_row_spec(ts, d), _row_spec(ts, d), _row_spec(ts, d), _vec_spec(d)] + s_in,
        out_specs=[_row_spec(ts, d), _vec_spec(d), _vec_spec(d)] + s_out,
        out_shape=[jax.ShapeDtypeStruct((s, d), F32), jax.ShapeDtypeStruct((1, d), F32),
                   jax.ShapeDtypeStruct((1, d), F32)] + s_shapes,
        scratch_shapes=s_scratch, input_output_aliases=s_alias,
        compiler_params=_cp("arbitrary"), name=name,
    )(dh, xin, dres, avec, *s_ops)


def _tri(n, strict=False, upper=False):
    r = lax.broadcasted_iota(jnp.int32, (n, n), 0)
    c = lax.broadcasted_iota(jnp.int32, (n, n), 1)
    if upper:
        r, c = c, r
    return ((r > c) if strict else (r >= c)).astype(F32)


def _gates_fwd(ps, bf, w2p, b2, *, name):
    s = ps.shape[0]
    ts = min(GATE_TS, s)

    def body(ps_ref, bf_ref, w_ref, b2_ref, cum_ref, la_ref, carry_ref):
        @pl.when(pl.program_id(0) == 0)
        def _():
            carry_ref[...] = jnp.zeros_like(carry_ref)

        psv = ps_ref[...]
        lf = _log_sigmoid(psv + bf_ref[...])
        cum = _dot_nn(_tri(ts), lf, HIGHEST) + carry_ref[...]
        cum_ref[...] = cum
        carry_ref[...] = cum[ts - 1:ts, :]
        z = _dot_nn(psv, w_ref[...], HIGHEST) + b2_ref[...]
        la_ref[...] = _log_sigmoid(z) * (1.0 / GLA_TEMP)

    return pl.pallas_call(
        body, grid=(s // ts,),
        in_specs=[_row_spec(ts, SMALL_W), _vec_spec(SMALL_W),
                  pl.BlockSpec((SMALL_W, GLA_KW), lambda i: (0, 0)), _vec_spec(GLA_KW)],
        out_specs=[_row_spec(ts, SMALL_W), _row_spec(ts, GLA_KW)],
        out_shape=[jax.ShapeDtypeStruct((s, SMALL_W), F32), jax.ShapeDtypeStruct((s, GLA_KW), F32)],
        scratch_shapes=[pltpu.VMEM((1, SMALL_W), F32)],
        compiler_params=_cp("arbitrary"), name=name,
    )(ps, bf, w2p, b2)


def _gates_bwd(dck, ps, bf, w2p, b2, dla, *, name):
    s = ps.shape[0]
    ts = min(GATE_TS, s)
    nb = s // ts
    rev = lambda i: (nb - 1 - i, 0)

    def body(dck_ref, ps_ref, bf_ref, w_ref, b2_ref, dla_ref, dps_ref, dbf_ref, dw_ref, db2_ref, carry_ref):
        @pl.when(pl.program_id(0) == 0)
        def _():
            carry_ref[...] = jnp.zeros_like(carry_ref)
            dbf_ref[...] = jnp.zeros_like(dbf_ref)
            dw_ref[...] = jnp.zeros_like(dw_ref)
            db2_ref[...] = jnp.zeros_like(db2_ref)

        psv, dckv = ps_ref[...], dck_ref[...]
        dlf = _dot_nn(_tri(ts, upper=True), dckv, HIGHEST) + carry_ref[...]
        carry_ref[...] += jnp.sum(dckv, axis=0, keepdims=True)
        lane = lax.broadcasted_iota(jnp.int32, (ts, SMALL_W), 1)
        dff = jnp.where(lane < FOX_HEADS, dlf * _sigmoid(-(psv + bf_ref[...])), 0.0)
        z = _dot_nn(psv, w_ref[...], HIGHEST) + b2_ref[...]
        dz = dla_ref[...] * _sigmoid(-z) * (1.0 / GLA_TEMP)
        dps_ref[...] = (_dot_nt(dz, w_ref[...], HIGHEST) + dff).astype(BF16)
        dbf_ref[...] += jnp.sum(dff, axis=0, keepdims=True)
        dw_ref[...] += _dot_tn(psv, dz, HIGHEST)
        db2_ref[...] += jnp.sum(dz, axis=0, keepdims=True)

    return pl.pallas_call(
        body, grid=(nb,),
        in_specs=[pl.BlockSpec((ts, SMALL_W), rev), pl.BlockSpec((ts, SMALL_W), rev), _vec_spec(SMALL_W),
                  pl.BlockSpec((SMALL_W, GLA_KW), lambda i: (0, 0)), _vec_spec(GLA_KW),
                  pl.BlockSpec((ts, GLA_KW), rev)],
        out_specs=[pl.BlockSpec((ts, SMALL_W), rev), _vec_spec(SMALL_W),
                   pl.BlockSpec((SMALL_W, GLA_KW), lambda i: (0, 0)), _vec_spec(GLA_KW)],
        out_shape=[jax.ShapeDtypeStruct((s, SMALL_W), BF16), jax.ShapeDtypeStruct((1, SMALL_W), F32),
                   jax.ShapeDtypeStruct((SMALL_W, GLA_KW), F32), jax.ShapeDtypeStruct((1, GLA_KW), F32)],
        scratch_shapes=[pltpu.VMEM((1, SMALL_W), F32)],
        compiler_params=_cp("arbitrary"), name=name,
    )(dck, ps, bf, w2p, b2, dla)


def _hs(h, hd=FOX_HD):
    return slice(h * hd, (h + 1) * hd)


def _fox_fwd(proj, cum_t, g_fox, *, name, side=None):
    s = proj.shape[0]
    tq, tk = min(FOX_TQ, s), min(FOX_TK, s)
    scale = FOX_HD ** -0.5
    n_si = len(side.inputs) if side else 0
    n_so = len(side.out_shapes) if side else 0
    grid = (s // tq, s // tk)

    def body(*refs):
        q_ref, k_ref, v_ref, ck_ref, g_ref = refs[:5]
        o_ref, n_ref, lse_ref = refs[5 + n_si:8 + n_si]
        m_sc, acc_sc = refs[8 + n_si + n_so:10 + n_si + n_so]
        i, j = pl.program_id(0), pl.program_id(1)
        if side:
            side.run(refs[5:5 + n_si], refs[8 + n_si:8 + n_si + n_so], refs[10 + n_si + n_so:],
                     (i == 0) & (j == 0), (i == grid[0] - 1) & (j == grid[1] - 1))

        @pl.when(j == 0)
        def _():
            m_sc[...] = jnp.full_like(m_sc, NEG)
            acc_sc[...] = jnp.zeros_like(acc_sc)

        def block(masked):
            mask = _causal_mask(i, j, tq, tk) if masked else None
            ones = jnp.ones((tk, FOX_HD), BF16)
            for h in range(FOX_HEADS):
                sc = _fox_logits(_dot_nt(q_ref[:, _hs(h)], k_ref[:, _hs(h)]), ck_ref[h:h + 1, :], mask, scale)
                m_prev = m_sc[h]
                m_new = jnp.maximum(m_prev, jnp.max(sc, axis=-1, keepdims=True))
                alpha = jnp.exp(m_prev - m_new)
                p = jnp.exp(sc - m_new).astype(BF16)
                v_one = jnp.concatenate([v_ref[:, _hs(h)], ones], axis=1)
                acc_sc[:, _hs(h, 2 * FOX_HD)] = alpha * acc_sc[:, _hs(h, 2 * FOX_HD)] + _dot_nn(p, v_one)
                m_sc[h] = m_new

        pl.when(j < i)(functools.partial(block, False))

        @pl.when(j == i)
        def _():
            block(True)
            lane = lax.broadcasted_iota(jnp.int32, (tq, 128), 1)
            lse = jnp.zeros((tq, 128), F32)
            for h in range(FOX_HEADS):
                l_rep = acc_sc[:, 2 * h * FOX_HD + FOX_HD:2 * (h + 1) * FOX_HD]
                o = acc_sc[:, 2 * h * FOX_HD:2 * h * FOX_HD + FOX_HD] / l_rep
                o_ref[:, _hs(h)] = o
                r = lax.rsqrt(jnp.mean(o * o, axis=-1, keepdims=True) + EPS)
                n_ref[:, _hs(h)] = (o * r * g_ref[h:h + 1, :]).astype(BF16)
                lse = jnp.where(lane == h, m_sc[h] + jnp.log(l_rep), lse)
            lse_ref[...] = lse

    kv = lambda col: (lambda i, j: (jnp.minimum(j, i), col))
    any_spec = pl.BlockSpec(memory_space=pl.ANY)
    return pl.pallas_call(
        body, grid=grid,
        in_specs=[pl.BlockSpec((tq, FOX_W), lambda i, j: (i, 0)),
                  pl.BlockSpec((tk, FOX_W), kv(1)),
                  pl.BlockSpec((tk, FOX_W), kv(2)),
                  pl.BlockSpec((FOX_HEADS, tk), lambda i, j: (0, jnp.minimum(j, i))),
                  pl.BlockSpec((FOX_HEADS, FOX_HD), lambda i, j: (0, 0))] + [any_spec] * n_si,
        out_specs=[pl.BlockSpec((tq, FOX_W), lambda i, j: (i, 0)),
                   pl.BlockSpec((tq, FOX_W), lambda i, j: (i, 0)),
                   pl.BlockSpec((tq, 128), lambda i, j: (i, 0))] + [any_spec] * n_so,
        out_shape=[jax.ShapeDtypeStruct((s, FOX_W), F32), jax.ShapeDtypeStruct((s, FOX_W), BF16),
                   jax.ShapeDtypeStruct((s, 128), F32)] + (side.out_shapes if side else []),
        scratch_shapes=[pltpu.VMEM((FOX_HEADS, tq, 1), F32), pltpu.VMEM((tq, 2 * FOX_W), F32)]
        + (side.scratch() if side else []),
        compiler_params=_cp("arbitrary", "arbitrary"), name=name,
    )(proj, proj, proj, cum_t, g_fox, *(side.inputs if side else []))


def _causal_mask(i, j, tq, tk):
    rows = i * tq + lax.broadcasted_iota(jnp.int32, (tq, tk), 0)
    cols = j * tk + lax.broadcasted_iota(jnp.int32, (tq, tk), 1)
    return rows >= cols


def _fox_logits(qk, ck, mask, scale):
    sc = qk * scale - ck
    return sc if mask is None else jnp.where(mask, sc, NEG)


def _fox_bwd(proj, do, cum_t, lse, delta, *, name, side=None):
    s = proj.shape[0]
    tq, tk = min(FOX_TQ, s), min(FOX_TK, s)
    nk, nq = s // tk, s // tq
    scale = FOX_HD ** -0.5
    n_si = len(side.inputs) if side else 0
    n_so = len(side.out_shapes) if side else 0

    def body(*refs):
        q_ref, k_ref, v_ref, do_ref, ck_ref, lse_ref, dl_ref = refs[:7]
        dq_hbm, dk_ref, dv_ref, dcq_hbm, dck_ref = refs[7 + n_si:12 + n_si]
        dq_sc, dcq_sc, dk_sc, dv_sc, dck_sc, out_sems = refs[12 + n_si + n_so:18 + n_si + n_so]
        j, i = pl.program_id(0), pl.program_id(1)
        if side:
            side.run(refs[7:7 + n_si], refs[12 + n_si:12 + n_si + n_so], refs[18 + n_si + n_so:],
                     (j == 0) & (i == 0), (j == nk - 1) & (i == nq - 1))

        @pl.when((j == 0) & (i == 0))
        def _():
            dq_sc[...] = jnp.zeros_like(dq_sc)
            dcq_sc[...] = jnp.zeros_like(dcq_sc)

        @pl.when(i == 0)
        def _():
            dk_sc[...] = jnp.zeros_like(dk_sc)
            dv_sc[...] = jnp.zeros_like(dv_sc)
            dck_sc[...] = jnp.zeros_like(dck_sc)

        def block(masked):
            mask = _causal_mask(i, j, tq, tk) if masked else None
            qrows = pl.ds(pl.multiple_of(i * tq, tq), tq)
            for h in range(FOX_HEADS):
                sc = _fox_logits(_dot_nt(q_ref[:, _hs(h)], k_ref[:, _hs(h)]), ck_ref[h:h + 1, :], mask, scale)
                p = jnp.exp(sc - lse_ref[:, h:h + 1])
                ds = p * (_dot_nt(do_ref[:, _hs(h)], v_ref[:, _hs(h)]) - dl_ref[:, h:h + 1])
                dsb = ds.astype(BF16)
                dv_sc[:, _hs(h)] += _dot_tn(p.astype(BF16), do_ref[:, _hs(h)])
                dk_sc[:, _hs(h)] += _dot_tn(dsb, q_ref[:, _hs(h)])
                dq_sc[qrows, _hs(h)] += _dot_nn(dsb, k_ref[:, _hs(h)]) * scale
                dck_sc[h:h + 1, :] -= jnp.sum(ds, axis=0, keepdims=True)
                dcq_sc[qrows, h:h + 1] += jnp.sum(ds, axis=-1, keepdims=True)

        pl.when(i > j)(functools.partial(block, False))
        pl.when(i == j)(functools.partial(block, True))

        @pl.when(i == nq - 1)
        def _():
            dk_ref[...] = (dk_sc[...] * scale).astype(BF16)
            dv_ref[...] = dv_sc[...].astype(BF16)
            dck_ref[...] = dck_sc[...]

        @pl.when((j == nk - 1) & (i == nq - 1))
        def _():
            out_q = pltpu.make_async_copy(dq_sc, dq_hbm, out_sems.at[0])
            out_c = pltpu.make_async_copy(dcq_sc, dcq_hbm, out_sems.at[1])
            out_q.start()
            out_c.start()
            out_q.wait()
            out_c.wait()

    qrow = lambda j, i: (jnp.maximum(i, j), 0)
    krow = lambda col: (lambda j, i: (j, col))
    any_spec = pl.BlockSpec(memory_space=pl.ANY)
    return pl.pallas_call(
        body, grid=(nk, nq),
        in_specs=[pl.BlockSpec((tq, FOX_W), qrow), pl.BlockSpec((tk, FOX_W), krow(1)),
                  pl.BlockSpec((tk, FOX_W), krow(2)),
                  pl.BlockSpec((tq, FOX_W), qrow),
                  pl.BlockSpec((FOX_HEADS, tk), lambda j, i: (0, j)),
                  pl.BlockSpec((tq, 128), qrow), pl.BlockSpec((tq, 128), qrow)] + [any_spec] * n_si,
        out_specs=[any_spec, pl.BlockSpec((tk, FOX_W), lambda j, i: (j, 0)),
                   pl.BlockSpec((tk, FOX_W), lambda j, i: (j, 0)), any_spec,
                   pl.BlockSpec((FOX_HEADS, tk), lambda j, i: (0, j))] + [any_spec] * n_so,
        out_shape=[jax.ShapeDtypeStruct((s, FOX_W), F32), jax.ShapeDtypeStruct((s, FOX_W), BF16),
                   jax.ShapeDtypeStruct((s, FOX_W), BF16), jax.ShapeDtypeStruct((s, 128), F32),
                   jax.ShapeDtypeStruct((FOX_HEADS, s), F32)] + (side.out_shapes if side else []),
        scratch_shapes=[pltpu.VMEM((s, FOX_W), F32), pltpu.VMEM((s, 128), F32),
                        pltpu.VMEM((tk, FOX_W), F32), pltpu.VMEM((tk, FOX_W), F32), pltpu.VMEM((FOX_HEADS, tk), F32),
                        pltpu.SemaphoreType.DMA((2,))] + (side.scratch() if side else []),
        compiler_params=_cp("arbitrary", "arbitrary"), name=name,
    )(proj, proj, proj, do, cum_t, lse, delta, *(side.inputs if side else []))


def _head_norm_bwd(dn_in, o, g, gr_src, *, nh, hd, dn_col, gr_col, name):
    s, w = o.shape
    ts = min(ROW_TILE, s)
    gated = gr_src is not None

    def body(*refs):
        if gated:
            dn_ref, o_ref, g_ref, gr_ref, do_ref, dgr_ref, dl_ref, dg_ref = refs
        else:
            dn_ref, o_ref, g_ref, do_ref, dl_ref, dg_ref = refs

        @pl.when(pl.program_id(0) == 0)
        def _():
            dg_ref[...] = jnp.zeros_like(dg_ref)

        lane = lax.broadcasted_iota(jnp.int32, (ts, 128), 1)
        delta = jnp.zeros((ts, 128), F32)
        for h in range(nh):
            sl = _hs(h, hd)
            ov = o_ref[:, sl]
            dnv = dn_ref[:, sl].astype(F32)
            gv = g_ref[h:h + 1, :]
            r = lax.rsqrt(jnp.mean(ov * ov, axis=-1, keepdims=True) + EPS)
            ohat = ov * r
            if gated:
                grv = gr_ref[:, sl].astype(F32)
                sig = _sigmoid(grv)
                dgr_ref[:, sl] = (dnv * (ohat * gv) * (sig * (1.0 + grv * (1.0 - sig)))).astype(BF16)
                dnv = dnv * (grv * sig)
            dg_ref[h:h + 1, :] += jnp.sum(dnv * ohat, axis=0, keepdims=True)
            dohat = dnv * gv
            do = r * (dohat - ohat * jnp.mean(dohat * ohat, axis=-1, keepdims=True))
            do_ref[:, sl] = do.astype(BF16)
            delta = jnp.where(lane == h, jnp.sum(do.astype(BF16).astype(F32) * ov, axis=-1, keepdims=True), delta)
        dl_ref[...] = delta

    in_specs = [pl.BlockSpec((ts, w), lambda i: (i, dn_col)), _row_spec(ts, w),
                pl.BlockSpec((nh, hd), lambda i: (0, 0))]
    args = [dn_in, o, g]
    out_specs = [_row_spec(ts, w)]
    out_shape = [jax.ShapeDtypeStruct((s, w), BF16)]
    if gated:
        in_specs.append(pl.BlockSpec((ts, w), lambda i: (i, gr_col)))
        args.append(gr_src)
        out_specs.append(_row_spec(ts, w))
        out_shape.append(jax.ShapeDtypeStruct((s, w), BF16))
    out_specs += [_row_spec(ts, 128), pl.BlockSpec((nh, hd), lambda i: (0, 0))]
    out_shape += [jax.ShapeDtypeStruct((s, 128), F32), jax.ShapeDtypeStruct((nh, hd), F32)]
    return pl.pallas_call(
        body, grid=(s // ts,), in_specs=in_specs, out_specs=out_specs, out_shape=out_shape,
        compiler_params=_cp("arbitrary"), name=name,
    )(*args)


GQ_BLK = 3 * FOX_W // GLA_DK
GK_BLK = GQ_BLK + GLA_HEADS
GV_BLK = (3 * FOX_W + 2 * GLA_KW) // GLA_DV
GR_BLK = GV_BLK + GLA_HEADS


def _gla_chunk_terms(la):
    cum = _dot_nn(_tri(CHUNK), la, HIGHEST)
    total = cum[CHUNK - 1:CHUNK, :]
    return jnp.exp(total - cum), jnp.exp(total)


def _gla_fwd(proj, log_a, g_gla, *, name):
    s = proj.shape[0]
    rows = min(GLA_ROWS, s)
    cb = rows // CHUNK
    nblk = s // rows
    scale = GLA_DK ** -0.5

    def body(q_ref, k_ref, v_ref, gr_ref, la_ref, g_ref, o_ref, n_ref, st_ref, st_sc):
        h = pl.program_id(0)

        @pl.when(pl.program_id(1) == 0)
        def _():
            st_sc[...] = jnp.zeros_like(st_sc)

        gv = g_ref[pl.ds(h, 1), :]
        for ci in range(cb):
            sl = slice(ci * CHUNK, (ci + 1) * CHUNK)
            e, dec = _gla_chunk_terms(la_ref[sl, :])
            k_dec = (k_ref[sl, :].astype(F32) * e).astype(BF16)
            st = st_sc[...] * dec + _dot_tn(v_ref[sl, :], k_dec)
            st_sc[...] = st
            st_ref[0, ci] = st
            qs = (q_ref[sl, :].astype(F32) * scale).astype(BF16)
            o = _dot_nt(qs, st.astype(BF16))
            o_ref[sl, :] = o
            r = lax.rsqrt(jnp.mean(o * o, axis=-1, keepdims=True) + EPS)
            grv = gr_ref[sl, :].astype(F32)
            n_ref[sl, :] = (o * r * gv * (grv * _sigmoid(grv))).astype(BF16)

    return pl.pallas_call(
        body, grid=(GLA_HEADS, nblk),
        in_specs=[pl.BlockSpec((rows, GLA_DK), lambda h, n: (n, GQ_BLK + h)),
                  pl.BlockSpec((rows, GLA_DK), lambda h, n: (n, GK_BLK + h)),
                  pl.BlockSpec((rows, GLA_DV), lambda h, n: (n, GV_BLK + h)),
                  pl.BlockSpec((rows, GLA_DV), lambda h, n: (n, GR_BLK + h)),
                  pl.BlockSpec((rows, GLA_DK), lambda h, n: (n, h)),
                  pl.BlockSpec((GLA_HEADS, GLA_DV), lambda h, n: (0, 0))],
        out_specs=[pl.BlockSpec((rows, GLA_DV), lambda h, n: (n, h)),
                   pl.BlockSpec((rows, GLA_DV), lambda h, n: (n, h)),
                   pl.BlockSpec((1, cb, GLA_DV, GLA_DK), lambda h, n: (h, n, 0, 0))],
        out_shape=[jax.ShapeDtypeStruct((s, GLA_W), F32), jax.ShapeDtypeStruct((s, GLA_W), BF16),
                   jax.ShapeDtypeStruct((GLA_HEADS, s // CHUNK, GLA_DV, GLA_DK), F32)],
        scratch_shapes=[pltpu.VMEM((GLA_DV, GLA_DK), F32)],
        compiler_params=_cp("parallel", "arbitrary"), name=name,
    )(proj, proj, proj, proj, log_a, g_gla)


def _gla_bwd(proj, log_a, do, states, *, name, side=None):
    s = proj.shape[0]
    rows = min(GLA_ROWS, s)
    cb = rows // CHUNK
    nblk = s // rows
    scale = GLA_DK ** -0.5
    n_si = len(side.inputs) if side else 0
    n_so = len(side.out_shapes) if side else 0

    def body(*refs):
        q_ref, k_ref, v_ref, la_ref, do_ref, st_ref, prev_ref = refs[:7]
        dq_ref, dk_ref, dv_ref, dla_ref = refs[7 + n_si:11 + n_si]
        g_sc = refs[11 + n_si + n_so]
        nrev = pl.program_id(1)
        blk = nblk - 1 - nrev
        if side:
            hh = pl.program_id(0)
            side.run(refs[7:7 + n_si], refs[11 + n_si:11 + n_si + n_so], refs[12 + n_si + n_so:],
                     (hh == 0) & (nrev == 0), (hh == GLA_HEADS - 1) & (nrev == nblk - 1))

        @pl.when(nrev == 0)
        def _():
            g_sc[...] = jnp.zeros_like(g_sc)

        for ci in reversed(range(cb)):
            sl = slice(ci * CHUNK, (ci + 1) * CHUNK)
            e, dec = _gla_chunk_terms(la_ref[sl, :])
            kd = k_ref[sl, :].astype(F32) * e
            qs = (q_ref[sl, :].astype(F32) * scale).astype(BF16)
            dov = do_ref[sl, :]
            st = st_ref[0, ci]
            if ci > 0:
                st_prev = st_ref[0, ci - 1]
            else:
                st_prev = prev_ref[0, 0] * (blk > 0).astype(F32)
            dq_ref[sl, :] = (_dot_nn(dov, st.astype(BF16)) * scale).astype(BF16)
            gt = g_sc[...] + _dot_tn(dov, qs)
            gtb = gt.astype(BF16)
            dkd = _dot_nn(v_ref[sl, :], gtb)
            dv_ref[sl, :] = _dot_nt(kd.astype(BF16), gtb).astype(BF16)
            dk_ref[sl, :] = (dkd * e).astype(BF16)
            ddec = jnp.sum(gt * st_prev, axis=0, keepdims=True) * dec
            dla_ref[sl, :] = _dot_nn(_tri(CHUNK, strict=True), dkd * kd, HIGHEST) + ddec
            g_sc[...] = gt * dec

    rev = lambda col0: (lambda h, n: (nblk - 1 - n, col0 + h))
    return pl.pallas_call(
        body, grid=(GLA_HEADS, nblk),
        in_specs=[pl.BlockSpec((rows, GLA_DK), rev(GQ_BLK)),
                  pl.BlockSpec((rows, GLA_DK), rev(GK_BLK)),
                  pl.BlockSpec((rows, GLA_DV), rev(GV_BLK)),
                  pl.BlockSpec((rows, GLA_DK), rev(0)),
                  pl.BlockSpec((rows, GLA_DV), rev(0)),
                  pl.BlockSpec((1, cb, GLA_DV, GLA_DK), lambda h, n: (h, nblk - 1 - n, 0, 0)),
                  pl.BlockSpec((1, 1, GLA_DV, GLA_DK),
                               lambda h, n: (h, jnp.maximum((nblk - 1 - n) * cb - 1, 0), 0, 0))]
        + [pl.BlockSpec(memory_space=pl.ANY)] * n_si,
        out_specs=[pl.BlockSpec((rows, GLA_DK), rev(0)), pl.BlockSpec((rows, GLA_DK), rev(0)),
                   pl.BlockSpec((rows, GLA_DV), rev(0)), pl.BlockSpec((rows, GLA_DK), rev(0))]
        + [pl.BlockSpec(memory_space=pl.ANY)] * n_so,
        out_shape=[jax.ShapeDtypeStruct((s, GLA_KW), BF16), jax.ShapeDtypeStruct((s, GLA_KW), BF16),
                   jax.ShapeDtypeStruct((s, GLA_W), BF16), jax.ShapeDtypeStruct((s, GLA_KW), F32)]
        + (side.out_shapes if side else []),
        scratch_shapes=[pltpu.VMEM((GLA_DV, GLA_DK), F32)] + (side.scratch() if side else []),
        compiler_params=_cp("arbitrary", "arbitrary"), name=name,
    )(proj, proj, proj, log_a, do, states, states, *(side.inputs if side else []))


def _row_tile(r):
    tr = min(ROW_TILE, r)
    while r % tr or tr % 8:
        tr -= 1
    return tr


def _adamw_math(w, g, m, v):
    m = ADAM_B1 * m + (1.0 - ADAM_B1) * g
    v = ADAM_B2 * v + (1.0 - ADAM_B2) * (g * g)
    m_hat = m / (1.0 - ADAM_B1 ** ADAM_STEP)
    v_hat = v / (1.0 - ADAM_B2 ** ADAM_STEP)
    delta = -ADAM_LR * (m_hat / (jnp.sqrt(v_hat) + ADAM_EPS) + ADAM_WD * w)
    return delta, m, v


COL_TILE = 256


def _tile_2d(r, c):
    if r % 8 == 0 and _row_tile(r) >= 64:
        return _row_tile(r), c
    assert c % COL_TILE == 0, (r, c)
    return r, COL_TILE


def _half_shape(shape):
    r, c = shape[-2:]
    return tuple(shape[:-2]) + ((r // 2, c) if _half_axis(r) == 0 else (r, c // 2))


def _adam(g, w, m, v, *, name):
    r, c = w.shape
    tr, tc = _tile_2d(r, c)

    def body(g_ref, w_ref, m_ref, v_ref, d_ref, mo_ref, vo_ref):
        d, mn, vn = _adamw_math(w_ref[...], g_ref[...], m_ref[...], v_ref[...])
        d_ref[...] = d
        mo_ref[...] = mn
        vo_ref[...] = vn

    spec = pl.BlockSpec((tr, tc), lambda i, j: (i, j))
    return pl.pallas_call(
        body, grid=(r // tr, c // tc), in_specs=[spec] * 4, out_specs=[spec] * 3,
        out_shape=[jax.ShapeDtypeStruct((r, c), F32)] * 3,
        compiler_params=_cp("parallel", "parallel"), name=name,
    )(g, w, m, v)


def _ada_grad_adam(c_all_t, dmod_cols, w, m, v, *, name):
    r, c = w.shape
    tr, tc = min(512, r), min(1024, c)

    def body(ct_ref, dm_ref, w_ref, m_ref, v_ref, g_ref, d_ref, mo_ref, vo_ref):
        g = _dot_nn(ct_ref[...], dm_ref[...], HIGHEST)
        g_ref[...] = g
        d, mn, vn = _adamw_math(w_ref[...], g, m_ref[...], v_ref[...])
        d_ref[...] = d
        mo_ref[...] = mn
        vo_ref[...] = vn

    spec = pl.BlockSpec((tr, tc), lambda i, j: (i, j))
    nb = c_all_t.shape[1]
    return pl.pallas_call(
        body, grid=(r // tr, c // tc),
        in_specs=[pl.BlockSpec((tr, nb), lambda i, j: (i, 0)), pl.BlockSpec((nb, tc), lambda i, j: (0, j)),
                  spec, spec, spec],
        out_specs=[spec] * 4, out_shape=[jax.ShapeDtypeStruct((r, c), F32)] * 4,
        compiler_params=_cp("parallel", "parallel"), name=name,
    )(c_all_t, dmod_cols, w, m, v)


def _mod_shard(c_all, w, b, *, name):
    k, c = w.shape
    tc = min(512, c)
    nb = c_all.shape[0]

    def body(c_ref, w_ref, b_ref, o_ref):
        o_ref[...] = _dot_nn(c_ref[...], w_ref[...], HIGHEST) + b_ref[...]

    return pl.pallas_call(
        body, grid=(c // tc,),
        in_specs=[pl.BlockSpec((nb, k), lambda j: (0, 0)), pl.BlockSpec((k, tc), lambda j: (0, j)),
                  pl.BlockSpec((1, tc), lambda j: (0, j))],
        out_specs=pl.BlockSpec((nb, tc), lambda j: (0, j)),
        out_shape=jax.ShapeDtypeStruct((nb, c), F32),
        compiler_params=_cp("parallel"), name=name,
    )(c_all, w, b)


def _silu_rows(c, *, name):
    def body(c_ref, o_ref):
        cv = c_ref[...]
        o_ref[...] = cv * _sigmoid(cv)

    return pl.pallas_call(body, out_shape=jax.ShapeDtypeStruct(c.shape, F32), name=name)(c)


def _pair_sum(g, got, idx, *, name):
    p, r, c = g.shape
    ax = _half_axis(r)
    hr, hc = _half_shape((r, c))
    tr, tc = _tile_2d(hr, hc)
    nbr, nbc = hr // tr, hc // tc

    def body(idx_ref, a_ref, b_ref, o_ref):
        o_ref[...] = (a_ref[...].astype(F32) + b_ref[...].astype(F32)).astype(BF16)

    def own_map(i, j, k, idx_ref):
        return (i, j + (idx_ref[0] * nbr if ax == 0 else 0), k + (idx_ref[0] * nbc if ax == 1 else 0))

    half_spec = pl.BlockSpec((1, tr, tc), lambda i, j, k, idx_ref: (i, j, k))
    return pl.pallas_call(
        body,
        grid_spec=pltpu.PrefetchScalarGridSpec(
            num_scalar_prefetch=1, grid=(p, nbr, nbc),
            in_specs=[pl.BlockSpec((1, tr, tc), own_map), half_spec],
            out_specs=half_spec),
        out_shape=jax.ShapeDtypeStruct((p, hr, hc), BF16),
        compiler_params=_cp("parallel", "parallel", "parallel"), name=name,
    )(idx, g, got)


def _final_sum(own, parts, idx, shard_shape, *, name):
    ax = _half_axis(shard_shape[0])
    hr, hc = own.shape[1:]
    tr, tc = _tile_2d(hr, hc)
    nbr, nbc = hr // tr, hc // tc

    def body(idx_ref, own_ref, parts_ref, o_ref):
        acc = own_ref[0].astype(F32)
        for q in range(3):
            acc = acc + parts_ref[q].astype(F32)
        o_ref[...] = acc

    def out_map(j, k, idx_ref):
        return (j + (idx_ref[0] * nbr if ax == 0 else 0), k + (idx_ref[0] * nbc if ax == 1 else 0))

    return pl.pallas_call(
        body,
        grid_spec=pltpu.PrefetchScalarGridSpec(
            num_scalar_prefetch=1, grid=(nbr, nbc),
            in_specs=[pl.BlockSpec((1, tr, tc), lambda j, k, idx_ref: (idx_ref[1], j, k)),
                      pl.BlockSpec((3, tr, tc), lambda j, k, idx_ref: (0, j, k))],
            out_specs=pl.BlockSpec((tr, tc), out_map)),
        out_shape=jax.ShapeDtypeStruct(tuple(shard_shape), F32),
        compiler_params=_cp("parallel", "parallel"), name=name,
    )(idx, own, parts)


def _stack_sum(x, *, name):
    p, r, c = x.shape
    tr = _row_tile(r)

    def body(x_ref, o_ref):
        acc = x_ref[0].astype(F32)
        for q in range(1, p):
            acc = acc + x_ref[q].astype(F32)
        o_ref[...] = acc

    return pl.pallas_call(
        body, grid=(r // tr,),
        in_specs=[pl.BlockSpec((p, tr, c), lambda i: (0, i, 0))],
        out_specs=pl.BlockSpec((tr, c), lambda i: (i, 0)),
        out_shape=jax.ShapeDtypeStruct((r, c), F32),
        compiler_params=_cp("parallel"), name=name,
    )(x)


def _place():
    x, y, c = lax.axis_index("x"), lax.axis_index("y"), lax.axis_index("c")
    chips = [(1 - x, y), (x, 1 - y), (1 - x, 1 - y)]
    return x, y, c, chips


def _gather8(x_shard, *, name):
    m_per, n = x_shard.shape

    def body(x_ref, out_ref, send_sems, recv_sems, local_sem):
        x, y, c, chips = _place()
        me, sibling = (x, y, c), (x, y, 1 - c)

        def rows(px, py, pc):
            return out_ref.at[pl.ds((4 * px + 2 * py + pc) * m_per, m_per), :]

        def copy(k, block, to, src=None):
            return pltpu.make_async_remote_copy(
                src_ref=rows(*block) if src is None else src, dst_ref=rows(*block),
                send_sem=send_sems.at[k], recv_sem=recv_sems.at[k], device_id=to, device_id_type=MESH)

        mine = pltpu.make_async_copy(x_ref, rows(*me), local_sem)
        mine.start()
        first = [copy(0, me, sibling, src=x_ref)]
        first += [copy(1 + j, me, (*chip, c), src=x_ref) for j, chip in enumerate(chips)]
        for cp in first:
            cp.start()
        passed = [copy(4 + j, (*chip, c), sibling) for j, chip in enumerate(chips)]
        for j, chip in enumerate(chips):
            copy(1 + j, (*chip, c), me).wait_recv()
            passed[j].start()
        copy(0, sibling, me).wait_recv()
        for j, chip in enumerate(chips):
            copy(4 + j, (*chip, 1 - c), me).wait_recv()
        for cp in first + passed:
            cp.wait_send()
        mine.wait()

    return pl.pallas_call(
        body,
        out_shape=jax.ShapeDtypeStruct((8 * m_per, n), x_shard.dtype),
        in_specs=[pl.BlockSpec(memory_space=pltpu.VMEM)],
        out_specs=pl.BlockSpec(memory_space=pltpu.VMEM),
        scratch_shapes=[pltpu.SemaphoreType.DMA((7,)), pltpu.SemaphoreType.DMA((7,)), pltpu.SemaphoreType.DMA],
        name=name,
    )(x_shard)


def _gather_weights(shards, *, name):
    return _comm_call(lambda ins, outs: [cp for i, o in zip(ins, outs) for cp in _plan_gather_ici(i, o)],
                      shards, [jax.ShapeDtypeStruct((4,) + s.shape, s.dtype) for s in shards], name=name)


def _plan_start(plan, send_sems, recv_sems):
    for k, (src, dst, _, peer) in enumerate(plan):
        pltpu.make_async_remote_copy(src_ref=src, dst_ref=dst, send_sem=send_sems.at[k], recv_sem=recv_sems.at[k],
                                     device_id=peer, device_id_type=MESH).start()


def _plan_wait(plan, send_sems, recv_sems):
    for k, (src, _, land, peer) in enumerate(plan):
        pltpu.make_async_remote_copy(src_ref=src, dst_ref=land, send_sem=send_sems.at[k], recv_sem=recv_sems.at[k],
                                     device_id=peer, device_id_type=MESH).wait_recv()
    for k, (src, dst, _, peer) in enumerate(plan):
        pltpu.make_async_remote_copy(src_ref=src, dst_ref=dst, send_sem=send_sems.at[k], recv_sem=recv_sems.at[k],
                                     device_id=peer, device_id_type=MESH).wait_send()


def _half_axis(rows):
    return 0 if rows % 32 == 0 else 1


def _rows_half(ref, hc, axis, part=None):
    size = ref.shape[axis] // 2
    start = hc * size
    if part is not None:
        size //= part[1]
        start = start + part[0] * size
    idx = [slice(None)] * len(ref.shape)
    idx[axis] = pl.ds(start, size)
    return ref.at[tuple(idx)]


def _plan_gather_ici(shard, full, part=None):
    x, y, c, chips = _place()
    ax = _half_axis(shard.shape[0])
    src = _rows_half(shard, c, ax, part)
    return [(src, _rows_half(full.at[2 * x + y], c, ax, part), _rows_half(full.at[2 * cx + cy], c, ax, part),
             (cx, cy, c)) for cx, cy in chips]


def _plan_gather_d2d(full, own):
    x, y, c, chips = _place()
    ax = _half_axis(full.shape[1])
    plan = []
    for cx, cy in chips:
        slot = full.at[2 * cx + cy]
        plan.append((_rows_half(slot, c, ax), _rows_half(slot, c, ax), _rows_half(slot, 1 - c, ax), (x, y, 1 - c)))
    mine = full.at[2 * x + y]
    plan.append((own, mine, mine, (x, y, 1 - c)))
    return plan


def _plan_pair(grad, got):
    x, y, c, _ = _place()
    return [(_rows_half(grad, 1 - c, 1 + _half_axis(grad.shape[1])), got, got, (x, y, 1 - c))]


def _plan_shard_ici(sums, parts, piece=None):
    _, _, c, chips = _place()

    def rows(ref):
        if piece is None:
            return ref
        k, n = piece
        if ref.shape[0] % (16 * n) == 0:
            size = ref.shape[0] // n
            return ref.at[pl.ds(k * size, size), :]
        size = ref.shape[1] // n
        return ref.at[:, pl.ds(k * size, size)]

    return [(rows(sums.at[2 * cx + cy]), rows(parts.at[k]), rows(parts.at[k]), (cx, cy, c))
            for k, (cx, cy) in enumerate(chips)]


def _plan_half(buf):
    x, y, c, _ = _place()
    ax = _half_axis(buf.shape[0])
    mine = _rows_half(buf, c, ax)
    return [(mine, mine, _rows_half(buf, 1 - c, ax), (x, y, 1 - c))]


def _comm_call(plan_fn, inputs, out_shapes, *, name, aliases=None):
    ni, no = len(inputs), len(out_shapes)

    def body(*refs):
        plan = plan_fn(refs[:ni], refs[ni:ni + no])
        send_sems, recv_sems = refs[ni + no:]
        _plan_start(plan, send_sems, recv_sems)
        _plan_wait(plan, send_sems, recv_sems)

    any_spec = pl.BlockSpec(memory_space=pl.ANY)
    n_copies = 3 * max(ni, no)
    return pl.pallas_call(
        body, out_shape=list(out_shapes), in_specs=[any_spec] * ni, out_specs=[any_spec] * no,
        scratch_shapes=[pltpu.SemaphoreType.DMA((n_copies,)), pltpu.SemaphoreType.DMA((n_copies,))],
        input_output_aliases=aliases or {}, name=name,
    )(*inputs)


def _gather_forward(full, own, *, name):
    return _comm_call(lambda ins, outs: _plan_gather_d2d(outs[0], ins[1]), [full, own],
                      [jax.ShapeDtypeStruct(full.shape, full.dtype)], name=name, aliases={0: 0})[0]


def _half_exchange(bufs, *, name):
    return _comm_call(lambda ins, outs: [cp for o in outs for cp in _plan_half(o)],
                      bufs, [jax.ShapeDtypeStruct(b.shape, b.dtype) for b in bufs], name=name,
                      aliases={k: k for k in range(len(bufs))})


def _split_w_in(w_in_t):
    d = w_in_t.shape[1]
    main = jnp.concatenate([w_in_t[0:3072], w_in_t[3080:5128], w_in_t[5144:6168]], axis=0)
    small = jnp.concatenate([w_in_t[3072:3080], w_in_t[5128:5144], jnp.zeros((SMALL_W - 24, d), w_in_t.dtype)], axis=0)
    return main, small


def _merge_dw_in(dw_main, dw_small):
    return jnp.concatenate([dw_main[0:3072], dw_small[0:8], dw_main[3072:5120], dw_small[8:24], dw_main[5120:6144]],
                           axis=0)


def _gather_side(shards):
    return _Side(shards, [jax.ShapeDtypeStruct((4,) + w.shape, w.dtype) for w in shards],
                 lambda ins, outs: [cp for i, o in zip(ins, outs) for cp in _plan_gather_ici(i, o)], 3 * len(shards))


def _forward_side(full, own):
    return _Side([full, own], [jax.ShapeDtypeStruct(full.shape, full.dtype)],
                 lambda ins, outs: _plan_gather_d2d(outs[0], ins[1]), 4, aliases={0: 0})


def _half_side(bufs):
    return _Side(bufs, [jax.ShapeDtypeStruct(b.shape, b.dtype) for b in bufs],
                 lambda ins, outs: [cp for o in outs for cp in _plan_half(o)], len(bufs),
                 aliases={k: k for k in range(len(bufs))})


def _parts_shape(sums):
    return jax.ShapeDtypeStruct((3,) + sums.shape[1:], sums.dtype)


def _got_shape(grad):
    return jax.ShapeDtypeStruct(_half_shape(grad.shape), grad.dtype)


def _pair_side(grad):
    return _Side([grad], [_got_shape(grad)], lambda ins, outs: _plan_pair(ins[0], outs[0]), 1)


def _local_step(x, target, mod, g_pre_mix, g_post_mix, g_pre_mlp, g_post_mlp, gw_in, b_fgate, w_gla_a2,
                b_gla_a2, g_fox, g_gla, own_w_in, own_w_out, own_w_mlp_in, own_w_mlp_out, chip, idx):
    s, d = x.shape
    shift_m, scale_m, gate_m, shift_f, scale_f, gate_f = [mod[:, i * d:(i + 1) * d] for i in range(6)]
    a1 = g_pre_mix * (1.0 + scale_m)
    a2 = g_pre_mlp * (1.0 + scale_f)
    bf = jnp.concatenate([b_fgate, jnp.zeros((1, SMALL_W - FOX_HEADS), F32)], axis=1)
    w2p = jnp.zeros((SMALL_W, GLA_KW), F32).at[FOX_HEADS:FOX_HEADS + GLA_RANK].set(w_gla_a2)

    h1, gw_in = _pre_fwd(x, a1, shift_m, name="pre_mix_fwd", side=_forward_side(gw_in, own_w_in))
    w_in_t = gw_in.reshape(-1, d)
    w_main, w_small = _split_w_in(w_in_t)
    full_shape = lambda w: jax.ShapeDtypeStruct((4,) + w.shape, w.dtype)
    first_side = _Side(
        [own_w_out, own_w_mlp_out], [full_shape(own_w_out), full_shape(own_w_mlp_out)],
        lambda ins, outs: _plan_gather_ici(ins[0], outs[0]) + _plan_gather_ici(ins[1], outs[1], part=(0, 4)), 6)
    proj, gw_out, gw_mlp_out = _mm(h1, w_main, mode="nt", out_dtypes=[BF16], name="in_proj_main", side=first_side)
    ps, gw_out = _mm(h1, w_small, mode="nt", out_dtypes=[F32], name="in_proj_small",
                     side=_forward_side(gw_out, own_w_out))
    w_out_full = gw_out.reshape(-1, d)
    cum, log_a = _gates_fwd(ps, bf, w2p, b_gla_a2, name="gates_fwd")
    cum_t = cum[:, :FOX_HEADS].T
    o_fox, fox_n, lse, gw_mlp_in = _fox_fwd(proj, cum_t, g_fox, name="fox_fwd", side=_gather_side([own_w_mlp_in]))
    o_gla, gla_n, states = _gla_fwd(proj, log_a, g_gla, name="gla_fwd")
    mixed = jnp.concatenate([fox_n, gla_n], axis=1)
    y1, gw_mlp_in = _mm(mixed, w_out_full, mode="nn", out_dtypes=[F32], name="out_proj",
                        side=_forward_side(gw_mlp_in, own_w_mlp_in))
    x1, h2 = _post_pre_fwd(x, y1, gate_m, g_post_mix, a2, shift_f, name="post_mix_pre_mlp_fwd")

    def mlp_act(acc):
        r = jnp.maximum(acc, 0.0)
        return acc, r * r

    rest_side = _Side([own_w_mlp_out, gw_mlp_out], [full_shape(own_w_mlp_out)],
                      lambda ins, outs: [cp for q in (1, 2, 3) for cp in _plan_gather_ici(ins[0], outs[0], part=(q, 4))],
                      9, aliases={1: 0})
    u, act, gw_mlp_out = _mm(h2, gw_mlp_in, mode="nn", out_dtypes=[BF16, BF16], epi=mlp_act, name="mlp_in",
                             b_slots=4, tm=MM_TM, side=rest_side)
    gw_mlp_out = _gather_forward(gw_mlp_out, own_w_mlp_out, name="gather_w_mlp_out_d2d")
    w_mlp_out_full = gw_mlp_out.reshape(-1, d)
    y2, = _mm(act, w_mlp_out_full, mode="nn", out_dtypes=[F32], name="mlp_out")
    dx2, dy2, loss_part, dgate_f, dg_post_mlp = _post_loss_bwd(x1, y2, gate_f, g_post_mlp, target,
                                                               name="post_mlp_loss_bwd")
    dw_mlp_out, = _mm(act, dy2, mode="tn", out_dtypes=[BF16], name="dw_mlp_out")
    dw_mlp_out = dw_mlp_out.reshape(4, D_FF // 4, d)

    def act_bwd(acc, uv):
        return (acc * (2.0 * jnp.maximum(uv.astype(F32), 0.0)),)

    du, got_mlp_out = _mm(dy2, w_mlp_out_full, mode="nt", out_dtypes=[BF16], extras=[u], epi=act_bwd,
                          name="d_mlp_hidden", tm=MM_TM, side=_pair_side(dw_mlp_out))
    sum_mlp_out = _pair_sum(dw_mlp_out, got_mlp_out, idx, name="grad_pair_sum_mlp_out")
    nj = D_FF // 4 // min(MM_T, D_FF // 4)
    tmw = min(MM_T, d)
    dw_mlp_in, parts_mlp_out = _mm(
        h2, du, mode="tn", out_dtypes=[BF16], name="dw_mlp_in",
        out_shapes=[jax.ShapeDtypeStruct((4, d, D_FF // 4), BF16)],
        out_specs=[pl.BlockSpec((1, tmw, min(MM_T, D_FF // 4)), lambda i, j, kk: (j // nj, i, j % nj))],
        side=_Side([sum_mlp_out], [_parts_shape(sum_mlp_out)],
                   lambda ins, outs: _plan_shard_ici(ins[0], outs[0], piece=(0, 2)), 3))
    dh2, got_mlp_in, parts_mlp_out = _mm(
        du, gw_mlp_in, mode="nt", out_dtypes=[F32], name="d_mlp_in", b_slots=4,
        side=_Side([dw_mlp_in, sum_mlp_out, parts_mlp_out], [_got_shape(dw_mlp_in), _parts_shape(sum_mlp_out)],
                   lambda ins, outs: _plan_pair(ins[0], outs[0]) + _plan_shard_ici(ins[1], outs[1], piece=(1, 2)),
                   4, aliases={2: 1}))
    sum_mlp_in = _pair_sum(dw_mlp_in, got_mlp_in, idx, name="grad_pair_sum_mlp_in")
    dx1, dshift_f, da2, dy1, dgate_m, dg_post_mix = _pre_post_bwd(dh2, x1, dx2, a2, y1, gate_m, g_post_mix,
                                                                  name="pre_mlp_post_mix_bwd")
    buf_mlp_out = _final_sum(sum_mlp_out, parts_mlp_out, idx, (D_FF // 4, d), name="grad_final_sum_mlp_out")
    dw_out, g_mlp_out = _mm(mixed, dy1, mode="tn", out_dtypes=[BF16], name="dw_out", side=_half_side([buf_mlp_out]))
    dw_out = dw_out.reshape(4, d // 4, d)
    dmixed, got_out = _mm(dy1, w_out_full, mode="nt", out_dtypes=[BF16], name="d_mixed", side=_pair_side(dw_out))
    sum_out = _pair_sum(dw_out, got_out, idx, name="grad_pair_sum_out")
    do_fox, delta, dg_fox = _head_norm_bwd(dmixed, o_fox, g_fox, None, nh=FOX_HEADS, hd=FOX_HD, dn_col=0,
                                           gr_col=0, name="fox_norm_bwd")
    do_gla, dgr, _, dg_gla = _head_norm_bwd(dmixed, o_gla, g_gla, proj, nh=GLA_HEADS, hd=GLA_DV, dn_col=1,
                                            gr_col=(3 * FOX_W + 2 * GLA_KW + GLA_W) // GLA_W, name="gla_norm_bwd")
    dq_fox, dk_fox, dv_fox, dcq, dck_t, parts_mlp_in, parts_out = _fox_bwd(
        proj, do_fox, cum_t, lse, delta, name="fox_bwd",
        side=_Side([sum_mlp_in, sum_out], [_parts_shape(sum_mlp_in), _parts_shape(sum_out)],
                   lambda ins, outs: _plan_shard_ici(ins[0], outs[0]) + _plan_shard_ici(ins[1], outs[1]), 6))
    dgq, dgk, dgv, dla = _gla_bwd(proj, log_a, do_gla, states, name="gla_bwd")
    dck = dcq + jnp.concatenate([dck_t.T, jnp.zeros((s, SMALL_W - FOX_HEADS), F32)], axis=1)
    dps, dbf, dw2p, db2 = _gates_bwd(dck, ps, bf, w2p, b_gla_a2, dla, name="gates_bwd")
    dproj = jnp.concatenate([dq_fox.astype(BF16), dk_fox, dv_fox, dgq, dgk, dgv, dgr], axis=1)
    buf_mlp_in = _final_sum(sum_mlp_in, parts_mlp_in, idx, (d, D_FF // 4), name="grad_final_sum_mlp_in")
    buf_out = _final_sum(sum_out, parts_out, idx, (d // 4, d), name="grad_final_sum_out")
    dw_main, g_mlp_in, g_out = _mm(dproj, h1, mode="tn", out_dtypes=[BF16], name="dw_in_main",
                                   side=_half_side([buf_mlp_in, buf_out]))
    dw_small, = _mm(dps, h1, mode="tn", out_dtypes=[BF16], name="dw_in_small")
    rs_in = w_in_t.shape[0] // 4
    dw_in = _merge_dw_in(dw_main, dw_small).reshape(4, rs_in, d)
    dh1_small, got_in = _mm(dps, w_small, mode="nn", out_dtypes=[F32], name="d_h1_small", side=_pair_side(dw_in))
    sum_in = _pair_sum(dw_in, got_in, idx, name="grad_pair_sum_in")
    dh1, parts_in = _mm(
        dproj, w_main, mode="nn", out_dtypes=[F32], extras=[dh1_small], epi=lambda acc, e: (acc + e,), name="d_h1",
        side=_Side([sum_in], [_parts_shape(sum_in)],
                   lambda ins, outs: [cp for q in range(3) for cp in _plan_shard_ici(ins[0], outs[0], piece=(q, 4))],
                   9))
    grad_x, dshift_m, da1, parts_in = _pre_bwd(
        dh1, x, dx1, a1, name="pre_mix_bwd",
        side=_Side([sum_in, parts_in], [_parts_shape(sum_in)],
                   lambda ins, outs: _plan_shard_ici(ins[0], outs[0], piece=(3, 4)), 3, aliases={1: 0}))
    buf_in = _final_sum(sum_in, parts_in, idx, (rs_in, d), name="grad_final_sum_in")
    g_in, = _half_exchange([buf_in], name="grad_half_exchange_in")
    g_big = [g_in, g_out, g_mlp_in, g_mlp_out]

    dmod = jnp.concatenate([dshift_m, da1 * g_pre_mix, dgate_m, dshift_f, da2 * g_pre_mlp, dgate_f], axis=1)
    small = dict(
        dmod=dmod, g_pre_mix=da1 * (1.0 + scale_m), g_post_mix=dg_post_mix, g_pre_mlp=da2 * (1.0 + scale_f),
        g_post_mlp=dg_post_mlp, b_fgate=dbf[:, :FOX_HEADS], w_gla_a2=dw2p[FOX_HEADS:FOX_HEADS + GLA_RANK],
        b_gla_a2=db2, g_fox_out=dg_fox, g_gla_out=dg_gla)
    return loss_part, grad_x, g_big, small


def _pack(arrays):
    flat = jnp.concatenate([a.reshape(-1).astype(F32) for a in arrays])
    n = flat.shape[0]
    rows = -(-n // 128)
    rows = -(-rows // 8) * 8
    return jnp.pad(flat, (0, rows * 128 - n)).reshape(rows, 128)


def _unpack(buf, shapes):
    flat = buf.reshape(-1)
    out, off = [], 0
    for shp in shapes:
        n = 1
        for q in shp:
            n *= q
        out.append(flat[off:off + n].reshape(shp))
        off += n
    return out


SMALL_GRAD_ORDER = ["dmod", "g_pre_mix", "g_post_mix", "g_pre_mlp", "g_post_mlp", "b_fgate", "w_gla_a2", "b_gla_a2",
                    "g_fox_out", "g_gla_out"]


def kernel(x, c, w_ada, b_ada, g_pre_mix, g_post_mix, w_in, b_fgate, w_gla_a2, b_gla_a2, g_fox_out, g_gla_out, w_out, g_pre_mlp, g_post_mlp, w_mlp_in, w_mlp_out, loss_target, m_w_ada, m_b_ada, m_g_pre_mix, m_g_post_mix, m_w_in, m_b_fgate, m_w_gla_a2, m_b_gla_a2, m_g_fox_out, m_g_gla_out, m_w_out, m_g_pre_mlp, m_g_post_mlp, m_w_mlp_in, m_w_mlp_out, v_w_ada, v_b_ada, v_g_pre_mix, v_g_post_mix, v_w_in, v_b_fgate, v_w_gla_a2, v_b_gla_a2, v_g_fox_out, v_g_gla_out, v_w_out, v_g_pre_mlp, v_g_post_mlp, v_w_mlp_in, v_w_mlp_out):
    ix, iy, ic = lax.axis_index("x"), lax.axis_index("y"), lax.axis_index("c")
    chip = 2 * ix + iy
    dev = 4 * ix + 2 * iy + ic
    d = D_MODEL

    c_act = _silu_rows(c, name="silu_c")
    pack1 = _pack([c_act, w_gla_a2[0], g_gla_out[0]])
    rows1 = pack1.shape[0]
    got1 = _gather8(pack1, name="gather_small_fwd").reshape(8, rows1, 128)
    per_dev = [_unpack(got1[q], [(d,), (GLA_RANK, GLA_KW // 4), (GLA_HEADS, GLA_DV // 4)]) for q in range(8)]
    c_all = jnp.stack([p[0] for p in per_dev])
    w_gla_a2_full = jnp.concatenate([per_dev[2 * j][1] for j in range(4)], axis=1)
    g_gla_full = jnp.concatenate([per_dev[2 * j][2] for j in range(4)], axis=1)
    cols = w_ada.shape[2]
    b_ada_shard = lax.dynamic_slice_in_dim(b_ada, chip * cols, cols, axis=1)
    mod_sh = _mod_shard(c_all, w_ada[0], b_ada_shard, name="ada_mod")
    got2 = _gather8(mod_sh, name="gather_mod").reshape(8, 8, cols)
    mod_all = jnp.concatenate([got2[2 * j] for j in range(4)], axis=1)
    mod = lax.dynamic_slice_in_dim(mod_all, dev, 1, axis=0)

    tr_in = lambda a: jnp.transpose(a[0])
    own_bf = [tr_in(w_in).astype(BF16), w_out[0].astype(BF16), w_mlp_in[0].astype(BF16), w_mlp_out[0].astype(BF16)]
    gw_in, = _gather_weights(own_bf[:1], name="gather_w_in_ici")
    idx = jnp.stack([ic, chip]).astype(jnp.int32)
    loss_part, grad_x, g_big, small = _local_step(
        x[0], loss_target[0], mod, g_pre_mix, g_post_mix, g_pre_mlp, g_post_mlp, gw_in, b_fgate,
        w_gla_a2_full, b_gla_a2, g_fox_out[0], g_gla_full, own_bf[0], own_bf[1], own_bf[2], own_bf[3], chip, idx)
    loss = lax.psum(loss_part[0, 0], ("x", "y", "c"))

    big_w = [(tr_in(w_in), tr_in(m_w_in), tr_in(v_w_in)), (w_out[0], m_w_out[0], v_w_out[0]),
             (w_mlp_in[0], m_w_mlp_in[0], v_w_mlp_in[0]), (w_mlp_out[0], m_w_mlp_out[0], v_w_mlp_out[0])]
    big_res = []
    for q, (g, (w, m, v)) in enumerate(zip(g_big, big_w)):
        res4 = (g,) + tuple(_adam(g, w, m, v, name=f"adam_big_{q}"))
        big_res.append(tuple((jnp.transpose(a) if q == 0 else a)[None] for a in res4))

    pack2 = _pack([small[k] for k in SMALL_GRAD_ORDER])
    rows2 = pack2.shape[0]
    got3 = _gather8(pack2, name="gather_small_grads").reshape(8, rows2, 128)
    dmod_all = got3[:, :6 * d // 128, :].reshape(8, 6 * d)
    sums = _stack_sum(got3, name="small_grad_sum")
    shapes = [(1, 6 * d), (1, d), (1, d), (1, d), (1, d), (1, FOX_HEADS), (1, GLA_RANK, GLA_KW), (1, GLA_KW),
              (1, FOX_HEADS, FOX_HD), (1, GLA_HEADS, GLA_DV)]
    sg = dict(zip(["b_ada"] + SMALL_GRAD_ORDER[1:], _unpack(sums, shapes)))
    sg["w_gla_a2"] = lax.dynamic_slice_in_dim(sg["w_gla_a2"], chip * (GLA_KW // 4), GLA_KW // 4, axis=2)
    sg["g_gla_out"] = lax.dynamic_slice_in_dim(sg["g_gla_out"], chip * (GLA_DV // 4), GLA_DV // 4, axis=2)
    small_names = ["b_ada", "g_pre_mix", "g_post_mix", "b_fgate", "w_gla_a2", "b_gla_a2", "g_fox_out", "g_gla_out",
                   "g_pre_mlp", "g_post_mlp"]
    small_w = dict(b_ada=(b_ada, m_b_ada, v_b_ada), g_pre_mix=(g_pre_mix, m_g_pre_mix, v_g_pre_mix),
                   g_post_mix=(g_post_mix, m_g_post_mix, v_g_post_mix), b_fgate=(b_fgate, m_b_fgate, v_b_fgate),
                   w_gla_a2=(w_gla_a2, m_w_gla_a2, v_w_gla_a2), b_gla_a2=(b_gla_a2, m_b_gla_a2, v_b_gla_a2),
                   g_fox_out=(g_fox_out, m_g_fox_out, v_g_fox_out), g_gla_out=(g_gla_out, m_g_gla_out, v_g_gla_out),
                   g_pre_mlp=(g_pre_mlp, m_g_pre_mlp, v_g_pre_mlp), g_post_mlp=(g_post_mlp, m_g_post_mlp, v_g_post_mlp))
    sshapes = [small_w[k][0].shape for k in small_names]
    pg = _pack([sg[k] for k in small_names])
    pw, pm, pv = [_pack([small_w[k][q] for k in small_names]) for q in range(3)]
    pd, pmn, pvn = _adam(pg, pw, pm, pv, name="adam_small")
    s_delta = dict(zip(small_names, _unpack(pd, sshapes)))
    s_m = dict(zip(small_names, _unpack(pmn, sshapes)))
    s_v = dict(zip(small_names, _unpack(pvn, sshapes)))

    dmod_cols = lax.dynamic_slice_in_dim(dmod_all, chip * cols, cols, axis=1)
    g_ada, d_ada, m_ada, v_ada = _ada_grad_adam(c_all.T, dmod_cols, w_ada[0], m_w_ada[0], v_w_ada[0], name="ada_grad_adam")

    order = ["w_ada", "b_ada", "g_pre_mix", "g_post_mix", "w_in", "b_fgate", "w_gla_a2", "b_gla_a2", "g_fox_out",
             "g_gla_out", "w_out", "g_pre_mlp", "g_post_mlp", "w_mlp_in", "w_mlp_out"]
    res = {"w_ada": (g_ada[None], d_ada[None], m_ada[None], v_ada[None]),
           "w_in": big_res[0], "w_out": big_res[1], "w_mlp_in": big_res[2], "w_mlp_out": big_res[3]}
    for k in small_names:
        res[k] = (sg[k], s_delta[k], s_m[k], s_v[k])
    return (loss, grad_x[None], *[res[k][0] for k in order], *[res[k][1] for k in order],
            *[res[k][2] for k in order], *[res[k][3] for k in order])
```

```python
import functools

import jax
import jax.numpy as jnp
from jax import lax
from jax.experimental import pallas as pl
from jax.experimental.pallas import tpu as pltpu

F32 = jnp.float32
BF16 = jnp.bfloat16
MESH = pl.DeviceIdType.MESH
HIGHEST = lax.Precision.HIGHEST

D_MODEL = 2048
FOX_HEADS = 8
FOX_HD = 128
FOX_W = FOX_HEADS * FOX_HD
GLA_HEADS = 4
GLA_DK = 128
GLA_DV = 256
GLA_KW = GLA_HEADS * GLA_DK
GLA_W = GLA_HEADS * GLA_DV
GLA_RANK = 16
GLA_TEMP = 16.0
CHUNK = 64
D_FF = 4 * D_MODEL
EPS = 1e-6
MAIN_W = 3 * FOX_W + 2 * GLA_KW + 2 * GLA_W
SMALL_W = 128
NEG = -1e30

ADAM_LR = 0.001
ADAM_B1 = 0.9
ADAM_B2 = 0.999
ADAM_EPS = 1e-08
ADAM_WD = 0.01
ADAM_STEP = 10

VMEM_LIMIT = 52 * 1024 * 1024
ROW_TILE = 256
FOX_TQ = 512
FOX_TK = 512
GLA_ROWS = 512
GATE_TS = 512
MM_T = 1024
MM_TK = 2048
MM_TM = 2048


def _cp(*sem):
    return pltpu.CompilerParams(dimension_semantics=sem, vmem_limit_bytes=VMEM_LIMIT)


def _dot_nn(a, b, precision=None):
    return jnp.dot(a, b, preferred_element_type=F32, precision=precision)


def _dot_nt(a, b, precision=None):
    return lax.dot_general(a, b, (((1,), (1,)), ((), ())), preferred_element_type=F32, precision=precision)


def _dot_tn(a, b, precision=None):
    return lax.dot_general(a, b, (((0,), (0,)), ((), ())), preferred_element_type=F32, precision=precision)


def _sigmoid(x):
    return 1.0 / (1.0 + jnp.exp(-x))


def _log_sigmoid(x):
    return jnp.minimum(x, 0.0) - jnp.log(1.0 + jnp.exp(-jnp.abs(x)))


class _Side:
    def __init__(self, inputs, out_shapes, plan_fn, n_copies, aliases=None):
        self.inputs, self.out_shapes, self.plan_fn, self.n_copies = list(inputs), list(out_shapes), plan_fn, n_copies
        self.aliases = dict(aliases or {})

    def scratch(self):
        return [pltpu.SemaphoreType.DMA((self.n_copies,)), pltpu.SemaphoreType.DMA((self.n_copies,))]

    def run(self, in_refs, out_refs, sems, first, last):
        @pl.when(first)
        def _():
            _plan_start(self.plan_fn(in_refs, out_refs), *sems)

        @pl.when(last)
        def _():
            _plan_wait(self.plan_fn(in_refs, out_refs), *sems)


def _mm(a, b, *, mode, out_dtypes, name, tm=None, tn=None, tk=None, extras=(), epi=None,
        out_shapes=None, out_specs=None, side=None, b_slots=0):
    tm, tn, tk = tm or MM_T, tn or MM_T, tk or MM_TK
    b2 = (b.shape[1], b_slots * b.shape[2]) if b_slots else b.shape
    if mode == "nn":
        (m, k), n = a.shape, b2[1]
    elif mode == "nt":
        (m, k), n = a.shape, b2[0]
    else:
        (k, m), n = a.shape, b2[1]
    tm, tn, tk = min(tm, m), min(tn, n), min(tk, k)
    if b_slots:
        tn = min(tn, b.shape[2]) if mode == "nn" else tn
        tk = min(tk, b.shape[2]) if mode == "nt" else tk
    assert m % tm == 0 and n % tn == 0 and k % tk == 0, (name, m, n, k)
    nk = k // tk
    n_out, n_ex = len(out_dtypes), len(extras)
    if epi is None:
        epi = lambda acc: tuple(acc for _ in range(n_out))
    dot = {"nn": _dot_nn, "nt": _dot_nt, "tn": _dot_tn}[mode]

    n_si = len(side.inputs) if side else 0
    n_so = len(side.out_shapes) if side else 0
    grid = (m // tm, n // tn, nk)

    def body(*refs):
        a_ref, b_ref = refs[0], refs[1]
        ex_refs = refs[2:2 + n_ex]
        base = 2 + n_ex + n_si
        o_refs = refs[base:base + n_out]
        scratch = refs[base + n_out + n_so:]
        if side:
            pos = [pl.program_id(q) for q in range(3)]
            first = (pos[0] == 0) & (pos[1] == 0) & (pos[2] == 0)
            last = (pos[0] == grid[0] - 1) & (pos[1] == grid[1] - 1) & (pos[2] == grid[2] - 1)
            side.run(refs[2 + n_ex:base], refs[base + n_out:base + n_out + n_so], scratch[-2:], first, last)
        part = dot(a_ref[...], b_ref[...])

        def finish(acc):
            outs = epi(acc, *[e[...] for e in ex_refs])
            for o_ref, val in zip(o_refs, outs):
                o_ref[...] = val.reshape(o_ref.shape).astype(o_ref.dtype)

        if nk == 1:
            finish(part)
        else:
            acc_ref = scratch[0]
            kk = pl.program_id(2)

            @pl.when(kk == 0)
            def _():
                acc_ref[...] = part

            @pl.when(kk > 0)
            def _():
                acc_ref[...] += part

            @pl.when(kk == nk - 1)
            def _():
                finish(acc_ref[...])

    if mode == "nn":
        a_spec = pl.BlockSpec((tm, tk), lambda i, j, kk: (i, kk))
        b_spec = pl.BlockSpec((tk, tn), lambda i, j, kk: (kk, j))
        if b_slots:
            per = b.shape[2] // tn
            b_spec = pl.BlockSpec((None, tk, tn), lambda i, j, kk: (j // per, kk, j % per))
    elif mode == "nt":
        a_spec = pl.BlockSpec((tm, tk), lambda i, j, kk: (i, kk))
        b_spec = pl.BlockSpec((tn, tk), lambda i, j, kk: (j, kk))
        if b_slots:
            per = b.shape[2] // tk
            b_spec = pl.BlockSpec((None, tn, tk), lambda i, j, kk: (kk // per, j, kk % per))
    else:
        assert not b_slots
        a_spec = pl.BlockSpec((tk, tm), lambda i, j, kk: (kk, i))
        b_spec = pl.BlockSpec((tk, tn), lambda i, j, kk: (kk, j))
    tile_spec = pl.BlockSpec((tm, tn), lambda i, j, kk: (i, j))
    if out_shapes is None:
        out_shapes = [jax.ShapeDtypeStruct((m, n), dt) for dt in out_dtypes]
    if out_specs is None:
        out_specs = [tile_spec for _ in out_dtypes]
    any_spec = pl.BlockSpec(memory_space=pl.ANY)
    res = pl.pallas_call(
        body,
        grid=grid,
        in_specs=[a_spec, b_spec] + [tile_spec for _ in extras] + [any_spec] * n_si,
        out_specs=list(out_specs) + [any_spec] * n_so,
        out_shape=list(out_shapes) + (side.out_shapes if side else []),
        scratch_shapes=([pltpu.VMEM((tm, tn), F32)] if nk > 1 else []) + (side.scratch() if side else []),
        compiler_params=_cp("arbitrary", "arbitrary", "arbitrary") if side else _cp("parallel", "parallel", "arbitrary"),
        input_output_aliases={2 + n_ex + si: n_out + so for si, so in side.aliases.items()} if side else {},
        name=name,
    )(a, b, *extras, *(side.inputs if side else []))
    return res


def _row_spec(ts, d):
    return pl.BlockSpec((ts, d), lambda i: (i, 0))


def _vec_spec(d):
    return pl.BlockSpec((1, d), lambda i: (0, 0))


def _side_args(side, n_in, n_out):
    if side is None:
        return [], [], [], [], [], {}
    any_spec = pl.BlockSpec(memory_space=pl.ANY)
    return ([any_spec] * len(side.inputs), [any_spec] * len(side.out_shapes), side.out_shapes, side.scratch(),
            side.inputs, {n_in + si: n_out + so for si, so in side.aliases.items()})


def _pre_fwd(x, avec, shift, *, name, side=None):
    s, d = x.shape
    ts = min(ROW_TILE, s)
    nb = s // ts
    s_in, s_out, s_shapes, s_scratch, s_ops, s_alias = _side_args(side, 3, 1)

    def body(x_ref, a_ref, s_ref, *rest):
        h_ref = rest[len(s_in)]
        if side:
            step = pl.program_id(0)
            side.run(rest[:len(s_in)], rest[len(s_in) + 1:len(s_in) + 1 + len(s_out)],
                     rest[len(s_in) + 1 + len(s_out):], step == 0, step == nb - 1)
        xv = x_ref[...]
        r = lax.rsqrt(jnp.mean(xv * xv, axis=-1, keepdims=True) + EPS)
        h_ref[...] = (xv * r * a_ref[...] + s_ref[...]).astype(BF16)

    res = pl.pallas_call(
        body, grid=(nb,),
        in_specs=[_row_spec(ts, d), _vec_spec(d), _vec_spec(d)] + s_in,
        out_specs=[_row_spec(ts, d)] + s_out,
        out_shape=[jax.ShapeDtypeStruct((s, d), BF16)] + s_shapes,
        scratch_shapes=s_scratch, input_output_aliases=s_alias,
        compiler_params=_cp("arbitrary" if side else "parallel"), name=name,
    )(x, avec, shift, *s_ops)
    return res if side else res[0]


def _post_pre_fwd(x, y, gate, g, avec, shift, *, name):
    s, d = x.shape
    ts = min(ROW_TILE, s)

    def body(x_ref, y_ref, gate_ref, g_ref, a_ref, s_ref, o_ref, h_ref):
        yv = y_ref[...]
        r = lax.rsqrt(jnp.mean(yv * yv, axis=-1, keepdims=True) + EPS)
        x1 = x_ref[...] + gate_ref[...] * (yv * r * g_ref[...])
        o_ref[...] = x1
        r1 = lax.rsqrt(jnp.mean(x1 * x1, axis=-1, keepdims=True) + EPS)
        h_ref[...] = (x1 * r1 * a_ref[...] + s_ref[...]).astype(BF16)

    return pl.pallas_call(
        body, grid=(s // ts,),
        in_specs=[_row_spec(ts, d), _row_spec(ts, d)] + [_vec_spec(d)] * 4,
        out_specs=[_row_spec(ts, d), _row_spec(ts, d)],
        out_shape=[jax.ShapeDtypeStruct((s, d), F32), jax.ShapeDtypeStruct((s, d), BF16)],
        compiler_params=_cp("parallel"), name=name,
    )(x, y, gate, g, avec, shift)


def _post_bwd_math(dxv, yv, gatev, gv):
    r = lax.rsqrt(jnp.mean(yv * yv, axis=-1, keepdims=True) + EPS)
    yhat = yv * r
    dn = dxv * gatev
    dyhat = dn * gv
    dy = r * (dyhat - yhat * jnp.mean(dyhat * yhat, axis=-1, keepdims=True))
    return dy, dxv * (yhat * gv), dn * yhat


def _accumulate(first, pairs):
    @pl.when(first)
    def _():
        for ref, _ in pairs:
            ref[...] = jnp.zeros_like(ref)

    for ref, val in pairs:
        ref[...] += jnp.sum(val, axis=0, keepdims=True)


def _post_loss_bwd(x, y, gate, g, target, *, name):
    s, d = x.shape
    ts = min(ROW_TILE, s)

    def body(x_ref, y_ref, gate_ref, g_ref, t_ref, dx_ref, dy_ref, loss_ref, dgate_ref, dg_ref):
        yv, gatev, gv = y_ref[...], gate_ref[...], g_ref[...]
        r = lax.rsqrt(jnp.mean(yv * yv, axis=-1, keepdims=True) + EPS)
        diff = x_ref[...] + gatev * (yv * r * gv) - t_ref[...]
        dxv = diff * (1.0 / d)
        dx_ref[...] = dxv
        dy, dgate_rows, dg_rows = _post_bwd_math(dxv, yv, gatev, gv)
        dy_ref[...] = dy.astype(BF16)
        first = pl.program_id(0) == 0
        _accumulate(first, [(dgate_ref, dgate_rows), (dg_ref, dg_rows)])

        @pl.when(first)
        def _():
            loss_ref[...] = jnp.zeros_like(loss_ref)

        loss_ref[...] += jnp.sum(jnp.mean(diff * diff, axis=-1, keepdims=True)) * 0.5

    return pl.pallas_call(
        body, grid=(s // ts,),
        in_specs=[_row_spec(ts, d), _row_spec(ts, d), _vec_spec(d), _vec_spec(d), _row_spec(ts, d)],
        out_specs=[_row_spec(ts, d), _row_spec(ts, d), pl.BlockSpec((1, 128), lambda i: (0, 0)), _vec_spec(d),
                   _vec_spec(d)],
        out_shape=[jax.ShapeDtypeStruct((s, d), F32), jax.ShapeDtypeStruct((s, d), BF16),
                   jax.ShapeDtypeStruct((1, 128), F32), jax.ShapeDtypeStruct((1, d), F32),
                   jax.ShapeDtypeStruct((1, d), F32)],
        compiler_params=_cp("arbitrary"), name=name,
    )(x, y, gate, g, target)


def _pre_post_bwd(dh, xin, dres, avec, y, gate, g, *, name):
    s, d = xin.shape
    ts = min(ROW_TILE, s)

    def body(dh_ref, x_ref, dres_ref, a_ref, y_ref, gate_ref, g_ref, dx_ref, dshift_ref, da_ref, dy_ref,
             dgate_ref, dg_ref):
        xv, dhv = x_ref[...], dh_ref[...]
        r = lax.rsqrt(jnp.mean(xv * xv, axis=-1, keepdims=True) + EPS)
        xhat = xv * r
        dxhat = dhv * a_ref[...]
        dxv = dres_ref[...] + r * (dxhat - xhat * jnp.mean(dxhat * xhat, axis=-1, keepdims=True))
        dx_ref[...] = dxv
        dy, dgate_rows, dg_rows = _post_bwd_math(dxv, y_ref[...], gate_ref[...], g_ref[...])
        dy_ref[...] = dy.astype(BF16)
        _accumulate(pl.program_id(0) == 0, [(dshift_ref, dhv), (da_ref, dhv * xhat), (dgate_ref, dgate_rows),
                                            (dg_ref, dg_rows)])

    return pl.pallas_call(
        body, grid=(s // ts,),
        in_specs=[_row_spec(ts, d), _row_spec(ts, d), _row_spec(ts, d), _vec_spec(d), _row_spec(ts, d),
                  _vec_spec(d), _vec_spec(d)],
        out_specs=[_row_spec(ts, d), _vec_spec(d), _vec_spec(d), _row_spec(ts, d), _vec_spec(d), _vec_spec(d)],
        out_shape=[jax.ShapeDtypeStruct((s, d), F32), jax.ShapeDtypeStruct((1, d), F32),
                   jax.ShapeDtypeStruct((1, d), F32), jax.ShapeDtypeStruct((s, d), BF16),
                   jax.ShapeDtypeStruct((1, d), F32), jax.ShapeDtypeStruct((1, d), F32)],
        compiler_params=_cp("arbitrary"), name=name,
    )(dh, xin, dres, avec, y, gate, g)


def _pre_bwd(dh, xin, dres, avec, *, name, side=None):
    s, d = xin.shape
    ts = min(ROW_TILE, s)
    nb = s // ts
    s_in, s_out, s_shapes, s_scratch, s_ops, s_alias = _side_args(side, 4, 3)

    def body(dh_ref, x_ref, dres_ref, a_ref, *rest):
        dx_ref, dshift_ref, da_ref = rest[len(s_in):len(s_in) + 3]
        if side:
            step = pl.program_id(0)
            side.run(rest[:len(s_in)], rest[len(s_in) + 3:len(s_in) + 3 + len(s_out)],
                     rest[len(s_in) + 3 + len(s_out):], step == 0, step == nb - 1)
        xv, dhv = x_ref[...], dh_ref[...]
        r = lax.rsqrt(jnp.mean(xv * xv, axis=-1, keepdims=True) + EPS)
        xhat = xv * r
        dxhat = dhv * a_ref[...]
        dx_ref[...] = dres_ref[...] + r * (dxhat - xhat * jnp.mean(dxhat * xhat, axis=-1, keepdims=True))

        @pl.when(pl.program_id(0) == 0)
        def _():
            dshift_ref[...] = jnp.zeros_like(dshift_ref)
            da_ref[...] = jnp.zeros_like(da_ref)

        dshift_ref[...] += jnp.sum(dhv, axis=0, keepdims=True)
        da_ref[...] += jnp.sum(dhv * xhat, axis=0, keepdims=True)

    return pl.pallas_call(
        body, grid=(nb,),
        in_specs=[_row_spec(ts, d), _row_spec(ts, d), _row_spec(ts, d), _vec_spec(d)] + s_in,
        out_specs=[_row_spec(ts, d), _vec_spec(d), _vec_spec(d)] + s_out,
        out_shape=[jax.ShapeDtypeStruct((s, d), F32), jax.ShapeDtypeStruct((1, d), F32),
                   jax.ShapeDtypeStruct((1, d), F32)] + s_shapes,
        scratch_shapes=s_scratch, input_output_aliases=s_alias,
        compiler_params=_cp("arbitrary"), name=name,
    )(dh, xin, dres, avec, *s_ops)


def _tri(n, strict=False, upper=False):
    r = lax.broadcasted_iota(jnp.int32, (n, n), 0)
    c = lax.broadcasted_iota(jnp.int32, (n, n), 1)
    if upper:
        r, c = c, r
    return ((r > c) if strict else (r >= c)).astype(F32)


def _gates_fwd(ps, bf, w2p, b2, *, name):
    s = ps.shape[0]
    ts = min(GATE_TS, s)

    def body(ps_ref, bf_ref, w_ref, b2_ref, cum_ref, la_ref, carry_ref):
        @pl.when(pl.program_id(0) == 0)
        def _():
            carry_ref[...] = jnp.zeros_like(carry_ref)

        psv = ps_ref[...]
        lf = _log_sigmoid(psv + bf_ref[...])
        cum = _dot_nn(_tri(ts), lf, HIGHEST) + carry_ref[...]
        cum_ref[...] = cum
        carry_ref[...] = cum[ts - 1:ts, :]
        z = _dot_nn(psv, w_ref[...], HIGHEST) + b2_ref[...]
        la_ref[...] = _log_sigmoid(z) * (1.0 / GLA_TEMP)

    return pl.pallas_call(
        body, grid=(s // ts,),
        in_specs=[_row_spec(ts, SMALL_W), _vec_spec(SMALL_W),
                  pl.BlockSpec((SMALL_W, GLA_KW), lambda i: (0, 0)), _vec_spec(GLA_KW)],
        out_specs=[_row_spec(ts, SMALL_W), _row_spec(ts, GLA_KW)],
        out_shape=[jax.ShapeDtypeStruct((s, SMALL_W), F32), jax.ShapeDtypeStruct((s, GLA_KW), F32)],
        scratch_shapes=[pltpu.VMEM((1, SMALL_W), F32)],
        compiler_params=_cp("arbitrary"), name=name,
    )(ps, bf, w2p, b2)


def _gates_bwd(dck, ps, bf, w2p, b2, dla, *, name):
    s = ps.shape[0]
    ts = min(GATE_TS, s)
    nb = s // ts
    rev = lambda i: (nb - 1 - i, 0)

    def body(dck_ref, ps_ref, bf_ref, w_ref, b2_ref, dla_ref, dps_ref, dbf_ref, dw_ref, db2_ref, carry_ref):
        @pl.when(pl.program_id(0) == 0)
        def _():
            carry_ref[...] = jnp.zeros_like(carry_ref)
            dbf_ref[...] = jnp.zeros_like(dbf_ref)
            dw_ref[...] = jnp.zeros_like(dw_ref)
            db2_ref[...] = jnp.zeros_like(db2_ref)

        psv, dckv = ps_ref[...], dck_ref[...]
        dlf = _dot_nn(_tri(ts, upper=True), dckv, HIGHEST) + carry_ref[...]
        carry_ref[...] += jnp.sum(dckv, axis=0, keepdims=True)
        lane = lax.broadcasted_iota(jnp.int32, (ts, SMALL_W), 1)
        dff = jnp.where(lane < FOX_HEADS, dlf * _sigmoid(-(psv + bf_ref[...])), 0.0)
        z = _dot_nn(psv, w_ref[...], HIGHEST) + b2_ref[...]
        dz = dla_ref[...] * _sigmoid(-z) * (1.0 / GLA_TEMP)
        dps_ref[...] = (_dot_nt(dz, w_ref[...], HIGHEST) + dff).astype(BF16)
        dbf_ref[...] += jnp.sum(dff, axis=0, keepdims=True)
        dw_ref[...] += _dot_tn(psv, dz, HIGHEST)
        db2_ref[...] += jnp.sum(dz, axis=0, keepdims=True)

    return pl.pallas_call(
        body, grid=(nb,),
        in_specs=[pl.BlockSpec((ts, SMALL_W), rev), pl.BlockSpec((ts, SMALL_W), rev), _vec_spec(SMALL_W),
                  pl.BlockSpec((SMALL_W, GLA_KW), lambda i: (0, 0)), _vec_spec(GLA_KW),
                  pl.BlockSpec((ts, GLA_KW), rev)],
        out_specs=[pl.BlockSpec((ts, SMALL_W), rev), _vec_spec(SMALL_W),
                   pl.BlockSpec((SMALL_W, GLA_KW), lambda i: (0, 0)), _vec_spec(GLA_KW)],
        out_shape=[jax.ShapeDtypeStruct((s, SMALL_W), BF16), jax.ShapeDtypeStruct((1, SMALL_W), F32),
                   jax.ShapeDtypeStruct((SMALL_W, GLA_KW), F32), jax.ShapeDtypeStruct((1, GLA_KW), F32)],
        scratch_shapes=[pltpu.VMEM((1, SMALL_W), F32)],
        compiler_params=_cp("arbitrary"), name=name,
    )(dck, ps, bf, w2p, b2, dla)


def _hs(h, hd=FOX_HD):
    return slice(h * hd, (h + 1) * hd)


def _fox_fwd(proj, cum_t, g_fox, *, name, side=None):
    s = proj.shape[0]
    tq, tk = min(FOX_TQ, s), min(FOX_TK, s)
    scale = FOX_HD ** -0.5
    n_si = len(side.inputs) if side else 0
    n_so = len(side.out_shapes) if side else 0
    grid = (s // tq, s // tk)

    def body(*refs):
        q_ref, k_ref, v_ref, ck_ref, g_ref = refs[:5]
        o_ref, n_ref, lse_ref = refs[5 + n_si:8 + n_si]
        m_sc, acc_sc = refs[8 + n_si + n_so:10 + n_si + n_so]
        i, j = pl.program_id(0), pl.program_id(1)
        if side:
            side.run(refs[5:5 + n_si], refs[8 + n_si:8 + n_si + n_so], refs[10 + n_si + n_so:],
                     (i == 0) & (j == 0), (i == grid[0] - 1) & (j == grid[1] - 1))

        @pl.when(j == 0)
        def _():
            m_sc[...] = jnp.full_like(m_sc, NEG)
            acc_sc[...] = jnp.zeros_like(acc_sc)

        def block(masked):
            mask = _causal_mask(i, j, tq, tk) if masked else None
            ones = jnp.ones((tk, FOX_HD), BF16)
            for h in range(FOX_HEADS):
                sc = _fox_logits(_dot_nt(q_ref[:, _hs(h)], k_ref[:, _hs(h)]), ck_ref[h:h + 1, :], mask, scale)
                m_prev = m_sc[h]
                m_new = jnp.maximum(m_prev, jnp.max(sc, axis=-1, keepdims=True))
                alpha = jnp.exp(m_prev - m_new)
                p = jnp.exp(sc - m_new).astype(BF16)
                v_one = jnp.concatenate([v_ref[:, _hs(h)], ones], axis=1)
                acc_sc[:, _hs(h, 2 * FOX_HD)] = alpha * acc_sc[:, _hs(h, 2 * FOX_HD)] + _dot_nn(p, v_one)
                m_sc[h] = m_new

        pl.when(j < i)(functools.partial(block, False))

        @pl.when(j == i)
        def _():
            block(True)
            lane = lax.broadcasted_iota(jnp.int32, (tq, 128), 1)
            lse = jnp.zeros((tq, 128), F32)
            for h in range(FOX_HEADS):
                l_rep = acc_sc[:, 2 * h * FOX_HD + FOX_HD:2 * (h + 1) * FOX_HD]
                o = acc_sc[:, 2 * h * FOX_HD:2 * h * FOX_HD + FOX_HD] / l_rep
                o_ref[:, _hs(h)] = o
                r = lax.rsqrt(jnp.mean(o * o, axis=-1, keepdims=True) + EPS)
                n_ref[:, _hs(h)] = (o * r * g_ref[h:h + 1, :]).astype(BF16)
                lse = jnp.where(lane == h, m_sc[h] + jnp.log(l_rep), lse)
            lse_ref[...] = lse

    kv = lambda col: (lambda i, j: (jnp.minimum(j, i), col))
    any_spec = pl.BlockSpec(memory_space=pl.ANY)
    return pl.pallas_call(
        body, grid=grid,
        in_specs=[pl.BlockSpec((tq, FOX_W), lambda i, j: (i, 0)),
                  pl.BlockSpec((tk, FOX_W), kv(1)),
                  pl.BlockSpec((tk, FOX_W), kv(2)),
                  pl.BlockSpec((FOX_HEADS, tk), lambda i, j: (0, jnp.minimum(j, i))),
                  pl.BlockSpec((FOX_HEADS, FOX_HD), lambda i, j: (0, 0))] + [any_spec] * n_si,
        out_specs=[pl.BlockSpec((tq, FOX_W), lambda i, j: (i, 0)),
                   pl.BlockSpec((tq, FOX_W), lambda i, j: (i, 0)),
                   pl.BlockSpec((tq, 128), lambda i, j: (i, 0))] + [any_spec] * n_so,
        out_shape=[jax.ShapeDtypeStruct((s, FOX_W), F32), jax.ShapeDtypeStruct((s, FOX_W), BF16),
                   jax.ShapeDtypeStruct((s, 128), F32)] + (side.out_shapes if side else []),
        scratch_shapes=[pltpu.VMEM((FOX_HEADS, tq, 1), F32), pltpu.VMEM((tq, 2 * FOX_W), F32)]
        + (side.scratch() if side else []),
        compiler_params=_cp("arbitrary", "arbitrary"), name=name,
    )(proj, proj, proj, cum_t, g_fox, *(side.inputs if side else []))


def _causal_mask(i, j, tq, tk):
    rows = i * tq + lax.broadcasted_iota(jnp.int32, (tq, tk), 0)
    cols = j * tk + lax.broadcasted_iota(jnp.int32, (tq, tk), 1)
    return rows >= cols


def _fox_logits(qk, ck, mask, scale):
    sc = qk * scale - ck
    return sc if mask is None else jnp.where(mask, sc, NEG)


def _fox_bwd(proj, do, cum_t, lse, delta, *, name, side=None):
    s = proj.shape[0]
    tq, tk = min(FOX_TQ, s), min(FOX_TK, s)
    nk, nq = s // tk, s // tq
    scale = FOX_HD ** -0.5
    n_si = len(side.inputs) if side else 0
    n_so = len(side.out_shapes) if side else 0

    def body(*refs):
        q_ref, k_ref, v_ref, do_ref, ck_ref, lse_ref, dl_ref = refs[:7]
        dq_hbm, dk_ref, dv_ref, dcq_hbm, dck_ref = refs[7 + n_si:12 + n_si]
        dq_sc, dcq_sc, dk_sc, dv_sc, dck_sc, out_sems = refs[12 + n_si + n_so:18 + n_si + n_so]
        j, i = pl.program_id(0), pl.program_id(1)
        if side:
            side.run(refs[7:7 + n_si], refs[12 + n_si:12 + n_si + n_so], refs[18 + n_si + n_so:],
                     (j == 0) & (i == 0), (j == nk - 1) & (i == nq - 1))

        @pl.when((j == 0) & (i == 0))
        def _():
            dq_sc[...] = jnp.zeros_like(dq_sc)
            dcq_sc[...] = jnp.zeros_like(dcq_sc)

        @pl.when(i == 0)
        def _():
            dk_sc[...] = jnp.zeros_like(dk_sc)
            dv_sc[...] = jnp.zeros_like(dv_sc)
            dck_sc[...] = jnp.zeros_like(dck_sc)

        def block(masked):
            mask = _causal_mask(i, j, tq, tk) if masked else None
            qrows = pl.ds(pl.multiple_of(i * tq, tq), tq)
            for h in range(FOX_HEADS):
                sc = _fox_logits(_dot_nt(q_ref[:, _hs(h)], k_ref[:, _hs(h)]), ck_ref[h:h + 1, :], mask, scale)
                p = jnp.exp(sc - lse_ref[:, h:h + 1])
                ds = p * (_dot_nt(do_ref[:, _hs(h)], v_ref[:, _hs(h)]) - dl_ref[:, h:h + 1])
                dsb = ds.astype(BF16)
                dv_sc[:, _hs(h)] += _dot_tn(p.astype(BF16), do_ref[:, _hs(h)])
                dk_sc[:, _hs(h)] += _dot_tn(dsb, q_ref[:, _hs(h)])
                dq_sc[qrows, _hs(h)] += _dot_nn(dsb, k_ref[:, _hs(h)]) * scale
                dck_sc[h:h + 1, :] -= jnp.sum(ds, axis=0, keepdims=True)
                dcq_sc[qrows, h:h + 1] += jnp.sum(ds, axis=-1, keepdims=True)

        pl.when(i > j)(functools.partial(block, False))
        pl.when(i == j)(functools.partial(block, True))

        @pl.when(i == nq - 1)
        def _():
            dk_ref[...] = (dk_sc[...] * scale).astype(BF16)
            dv_ref[...] = dv_sc[...].astype(BF16)
            dck_ref[...] = dck_sc[...]

        @pl.when((j == nk - 1) & (i == nq - 1))
        def _():
            out_q = pltpu.make_async_copy(dq_sc, dq_hbm, out_sems.at[0])
            out_c = pltpu.make_async_copy(dcq_sc, dcq_hbm, out_sems.at[1])
            out_q.start()
            out_c.start()
            out_q.wait()
            out_c.wait()

    qrow = lambda j, i: (jnp.maximum(i, j), 0)
    krow = lambda col: (lambda j, i: (j, col))
    any_spec = pl.BlockSpec(memory_space=pl.ANY)
    return pl.pallas_call(
        body, grid=(nk, nq),
        in_specs=[pl.BlockSpec((tq, FOX_W), qrow), pl.BlockSpec((tk, FOX_W), krow(1)),
                  pl.BlockSpec((tk, FOX_W), krow(2)),
                  pl.BlockSpec((tq, FOX_W), qrow),
                  pl.BlockSpec((FOX_HEADS, tk), lambda j, i: (0, j)),
                  pl.BlockSpec((tq, 128), qrow), pl.BlockSpec((tq, 128), qrow)] + [any_spec] * n_si,
        out_specs=[any_spec, pl.BlockSpec((tk, FOX_W), lambda j, i: (j, 0)),
                   pl.BlockSpec((tk, FOX_W), lambda j, i: (j, 0)), any_spec,
                   pl.BlockSpec((FOX_HEADS, tk), lambda j, i: (0, j))] + [any_spec] * n_so,
        out_shape=[jax.ShapeDtypeStruct((s, FOX_W), F32), jax.ShapeDtypeStruct((s, FOX_W), BF16),
                   jax.ShapeDtypeStruct((s, FOX_W), BF16), jax.ShapeDtypeStruct((s, 128), F32),
                   jax.ShapeDtypeStruct((FOX_HEADS, s), F32)] + (side.out_shapes if side else []),
        scratch_shapes=[pltpu.VMEM((s, FOX_W), F32), pltpu.VMEM((s, 128), F32),
                        pltpu.VMEM((tk, FOX_W), F32), pltpu.VMEM((tk, FOX_W), F32), pltpu.VMEM((FOX_HEADS, tk), F32),
                        pltpu.SemaphoreType.DMA((2,))] + (side.scratch() if side else []),
        compiler_params=_cp("arbitrary", "arbitrary"), name=name,
    )(proj, proj, proj, do, cum_t, lse, delta, *(side.inputs if side else []))


def _head_norm_bwd(dn_in, o, g, gr_src, *, nh, hd, dn_col, gr_col, name):
    s, w = o.shape
    ts = min(ROW_TILE, s)
    gated = gr_src is not None

    def body(*refs):
        if gated:
            dn_ref, o_ref, g_ref, gr_ref, do_ref, dgr_ref, dl_ref, dg_ref = refs
        else:
            dn_ref, o_ref, g_ref, do_ref, dl_ref, dg_ref = refs

        @pl.when(pl.program_id(0) == 0)
        def _():
            dg_ref[...] = jnp.zeros_like(dg_ref)

        lane = lax.broadcasted_iota(jnp.int32, (ts, 128), 1)
        delta = jnp.zeros((ts, 128), F32)
        for h in range(nh):
            sl = _hs(h, hd)
            ov = o_ref[:, sl]
            dnv = dn_ref[:, sl].astype(F32)
            gv = g_ref[h:h + 1, :]
            r = lax.rsqrt(jnp.mean(ov * ov, axis=-1, keepdims=True) + EPS)
            ohat = ov * r
            if gated:
                grv = gr_ref[:, sl].astype(F32)
                sig = _sigmoid(grv)
                dgr_ref[:, sl] = (dnv * (ohat * gv) * (sig * (1.0 + grv * (1.0 - sig)))).astype(BF16)
                dnv = dnv * (grv * sig)
            dg_ref[h:h + 1, :] += jnp.sum(dnv * ohat, axis=0, keepdims=True)
            dohat = dnv * gv
            do = r * (dohat - ohat * jnp.mean(dohat * ohat, axis=-1, keepdims=True))
            do_ref[:, sl] = do.astype(BF16)
            delta = jnp.where(lane == h, jnp.sum(do.astype(BF16).astype(F32) * ov, axis=-1, keepdims=True), delta)
        dl_ref[...] = delta

    in_specs = [pl.BlockSpec((ts, w), lambda i: (i, dn_col)), _row_spec(ts, w),
                pl.BlockSpec((nh, hd), lambda i: (0, 0))]
    args = [dn_in, o, g]
    out_specs = [_row_spec(ts, w)]
    out_shape = [jax.ShapeDtypeStruct((s, w), BF16)]
    if gated:
        in_specs.append(pl.BlockSpec((ts, w), lambda i: (i, gr_col)))
        args.append(gr_src)
        out_specs.append(_row_spec(ts, w))
        out_shape.append(jax.ShapeDtypeStruct((s, w), BF16))
    out_specs += [_row_spec(ts, 128), pl.BlockSpec((nh, hd), lambda i: (0, 0))]
    out_shape += [jax.ShapeDtypeStruct((s, 128), F32), jax.ShapeDtypeStruct((nh, hd), F32)]
    return pl.pallas_call(
        body, grid=(s // ts,), in_specs=in_specs, out_specs=out_specs, out_shape=out_shape,
        compiler_params=_cp("arbitrary"), name=name,
    )(*args)


GQ_BLK = 3 * FOX_W // GLA_DK
GK_BLK = GQ_BLK + GLA_HEADS
GV_BLK = (3 * FOX_W + 2 * GLA_KW) // GLA_DV
GR_BLK = GV_BLK + GLA_HEADS


def _gla_chunk_terms(la):
    cum = _dot_nn(_tri(CHUNK), la, HIGHEST)
    total = cum[CHUNK - 1:CHUNK, :]
    return jnp.exp(total - cum), jnp.exp(total)


def _gla_fwd(proj, log_a, g_gla, *, name):
    s = proj.shape[0]
    rows = min(GLA_ROWS, s)
    cb = rows // CHUNK
    nblk = s // rows
    scale = GLA_DK ** -0.5

    def body(q_ref, k_ref, v_ref, gr_ref, la_ref, g_ref, o_ref, n_ref, st_ref, st_sc):
        h = pl.program_id(0)

        @pl.when(pl.program_id(1) == 0)
        def _():
            st_sc[...] = jnp.zeros_like(st_sc)

        gv = g_ref[pl.ds(h, 1), :]
        for ci in range(cb):
            sl = slice(ci * CHUNK, (ci + 1) * CHUNK)
            e, dec = _gla_chunk_terms(la_ref[sl, :])
            k_dec = (k_ref[sl, :].astype(F32) * e).astype(BF16)
            st = st_sc[...] * dec + _dot_tn(v_ref[sl, :], k_dec)
            st_sc[...] = st
            st_ref[0, ci] = st
            qs = (q_ref[sl, :].astype(F32) * scale).astype(BF16)
            o = _dot_nt(qs, st.astype(BF16))
            o_ref[sl, :] = o
            r = lax.rsqrt(jnp.mean(o * o, axis=-1, keepdims=True) + EPS)
            grv = gr_ref[sl, :].astype(F32)
            n_ref[sl, :] = (o * r * gv * (grv * _sigmoid(grv))).astype(BF16)

    return pl.pallas_call(
        body, grid=(GLA_HEADS, nblk),
        in_specs=[pl.BlockSpec((rows, GLA_DK), lambda h, n: (n, GQ_BLK + h)),
                  pl.BlockSpec((rows, GLA_DK), lambda h, n: (n, GK_BLK + h)),
                  pl.BlockSpec((rows, GLA_DV), lambda h, n: (n, GV_BLK + h)),
                  pl.BlockSpec((rows, GLA_DV), lambda h, n: (n, GR_BLK + h)),
                  pl.BlockSpec((rows, GLA_DK), lambda h, n: (n, h)),
                  pl.BlockSpec((GLA_HEADS, GLA_DV), lambda h, n: (0, 0))],
        out_specs=[pl.BlockSpec((rows, GLA_DV), lambda h, n: (n, h)),
                   pl.BlockSpec((rows, GLA_DV), lambda h, n: (n, h)),
                   pl.BlockSpec((1, cb, GLA_DV, GLA_DK), lambda h, n: (h, n, 0, 0))],
        out_shape=[jax.ShapeDtypeStruct((s, GLA_W), F32), jax.ShapeDtypeStruct((s, GLA_W), BF16),
                   jax.ShapeDtypeStruct((GLA_HEADS, s // CHUNK, GLA_DV, GLA_DK), F32)],
        scratch_shapes=[pltpu.VMEM((GLA_DV, GLA_DK), F32)],
        compiler_params=_cp("parallel", "arbitrary"), name=name,
    )(proj, proj, proj, proj, log_a, g_gla)


def _gla_bwd(proj, log_a, do, states, *, name):
    s = proj.shape[0]
    rows = min(GLA_ROWS, s)
    cb = rows // CHUNK
    nblk = s // rows
    scale = GLA_DK ** -0.5

    def body(q_ref, k_ref, v_ref, la_ref, do_ref, st_ref, prev_ref, dq_ref, dk_ref, dv_ref, dla_ref, g_sc):
        nrev = pl.program_id(1)
        blk = nblk - 1 - nrev

        @pl.when(nrev == 0)
        def _():
            g_sc[...] = jnp.zeros_like(g_sc)

        for ci in reversed(range(cb)):
            sl = slice(ci * CHUNK, (ci + 1) * CHUNK)
            e, dec = _gla_chunk_terms(la_ref[sl, :])
            kd = k_ref[sl, :].astype(F32) * e
            qs = (q_ref[sl, :].astype(F32) * scale).astype(BF16)
            dov = do_ref[sl, :]
            st = st_ref[0, ci]
            if ci > 0:
                st_prev = st_ref[0, ci - 1]
            else:
                st_prev = prev_ref[0, 0] * (blk > 0).astype(F32)
            dq_ref[sl, :] = (_dot_nn(dov, st.astype(BF16)) * scale).astype(BF16)
            gt = g_sc[...] + _dot_tn(dov, qs)
            gtb = gt.astype(BF16)
            dkd = _dot_nn(v_ref[sl, :], gtb)
            dv_ref[sl, :] = _dot_nt(kd.astype(BF16), gtb).astype(BF16)
            dk_ref[sl, :] = (dkd * e).astype(BF16)
            ddec = jnp.sum(gt * st_prev, axis=0, keepdims=True) * dec
            dla_ref[sl, :] = _dot_nn(_tri(CHUNK, strict=True), dkd * kd, HIGHEST) + ddec
            g_sc[...] = gt * dec

    rev = lambda col0: (lambda h, n: (nblk - 1 - n, col0 + h))
    return pl.pallas_call(
        body, grid=(GLA_HEADS, nblk),
        in_specs=[pl.BlockSpec((rows, GLA_DK), rev(GQ_BLK)),
                  pl.BlockSpec((rows, GLA_DK), rev(GK_BLK)),
                  pl.BlockSpec((rows, GLA_DV), rev(GV_BLK)),
                  pl.BlockSpec((rows, GLA_DK), rev(0)),
                  pl.BlockSpec((rows, GLA_DV), rev(0)),
                  pl.BlockSpec((1, cb, GLA_DV, GLA_DK), lambda h, n: (h, nblk - 1 - n, 0, 0)),
                  pl.BlockSpec((1, 1, GLA_DV, GLA_DK),
                               lambda h, n: (h, jnp.maximum((nblk - 1 - n) * cb - 1, 0), 0, 0))],
        out_specs=[pl.BlockSpec((rows, GLA_DK), rev(0)), pl.BlockSpec((rows, GLA_DK), rev(0)),
                   pl.BlockSpec((rows, GLA_DV), rev(0)), pl.BlockSpec((rows, GLA_DK), rev(0))],
        out_shape=[jax.ShapeDtypeStruct((s, GLA_KW), BF16), jax.ShapeDtypeStruct((s, GLA_KW), BF16),
                   jax.ShapeDtypeStruct((s, GLA_W), BF16), jax.ShapeDtypeStruct((s, GLA_KW), F32)],
        scratch_shapes=[pltpu.VMEM((GLA_DV, GLA_DK), F32)],
        compiler_params=_cp("parallel", "arbitrary"), name=name,
    )(proj, proj, proj, log_a, do, states, states)


def _row_tile(r):
    tr = min(ROW_TILE, r)
    while r % tr or tr % 8:
        tr -= 1
    return tr


def _adamw_math(w, g, m, v):
    m = ADAM_B1 * m + (1.0 - ADAM_B1) * g
    v = ADAM_B2 * v + (1.0 - ADAM_B2) * (g * g)
    m_hat = m / (1.0 - ADAM_B1 ** ADAM_STEP)
    v_hat = v / (1.0 - ADAM_B2 ** ADAM_STEP)
    delta = -ADAM_LR * (m_hat / (jnp.sqrt(v_hat) + ADAM_EPS) + ADAM_WD * w)
    return delta, m, v


COL_TILE = 256


def _tile_2d(r, c):
    if r % 8 == 0 and _row_tile(r) >= 64:
        return _row_tile(r), c
    assert c % COL_TILE == 0, (r, c)
    return r, COL_TILE


def _half_shape(shape):
    r, c = shape[-2:]
    return tuple(shape[:-2]) + ((r // 2, c) if _half_axis(r) == 0 else (r, c // 2))


def _adam(g, w, m, v, *, name):
    r, c = w.shape
    tr, tc = _tile_2d(r, c)

    def body(g_ref, w_ref, m_ref, v_ref, d_ref, mo_ref, vo_ref):
        d, mn, vn = _adamw_math(w_ref[...], g_ref[...], m_ref[...], v_ref[...])
        d_ref[...] = d
        mo_ref[...] = mn
        vo_ref[...] = vn

    spec = pl.BlockSpec((tr, tc), lambda i, j: (i, j))
    return pl.pallas_call(
        body, grid=(r // tr, c // tc), in_specs=[spec] * 4, out_specs=[spec] * 3,
        out_shape=[jax.ShapeDtypeStruct((r, c), F32)] * 3,
        compiler_params=_cp("parallel", "parallel"), name=name,
    )(g, w, m, v)


def _ada_grad_adam(c_all_t, dmod_cols, w, m, v, *, name):
    r, c = w.shape
    tr, tc = min(512, r), min(1024, c)

    def body(ct_ref, dm_ref, w_ref, m_ref, v_ref, g_ref, d_ref, mo_ref, vo_ref):
        g = _dot_nn(ct_ref[...], dm_ref[...], HIGHEST)
        g_ref[...] = g
        d, mn, vn = _adamw_math(w_ref[...], g, m_ref[...], v_ref[...])
        d_ref[...] = d
        mo_ref[...] = mn
        vo_ref[...] = vn

    spec = pl.BlockSpec((tr, tc), lambda i, j: (i, j))
    nb = c_all_t.shape[1]
    return pl.pallas_call(
        body, grid=(r // tr, c // tc),
        in_specs=[pl.BlockSpec((tr, nb), lambda i, j: (i, 0)), pl.BlockSpec((nb, tc), lambda i, j: (0, j)),
                  spec, spec, spec],
        out_specs=[spec] * 4, out_shape=[jax.ShapeDtypeStruct((r, c), F32)] * 4,
        compiler_params=_cp("parallel", "parallel"), name=name,
    )(c_all_t, dmod_cols, w, m, v)


def _mod_shard(c_all, w, b, *, name):
    k, c = w.shape
    tc = min(512, c)
    nb = c_all.shape[0]

    def body(c_ref, w_ref, b_ref, o_ref):
        o_ref[...] = _dot_nn(c_ref[...], w_ref[...], HIGHEST) + b_ref[...]

    return pl.pallas_call(
        body, grid=(c // tc,),
        in_specs=[pl.BlockSpec((nb, k), lambda j: (0, 0)), pl.BlockSpec((k, tc), lambda j: (0, j)),
                  pl.BlockSpec((1, tc), lambda j: (0, j))],
        out_specs=pl.BlockSpec((nb, tc), lambda j: (0, j)),
        out_shape=jax.ShapeDtypeStruct((nb, c), F32),
        compiler_params=_cp("parallel"), name=name,
    )(c_all, w, b)


def _silu_rows(c, *, name):
    def body(c_ref, o_ref):
        cv = c_ref[...]
        o_ref[...] = cv * _sigmoid(cv)

    return pl.pallas_call(body, out_shape=jax.ShapeDtypeStruct(c.shape, F32), name=name)(c)


def _pair_sum(g, got, idx, *, name):
    p, r, c = g.shape
    ax = _half_axis(r)
    hr, hc = _half_shape((r, c))
    tr, tc = _tile_2d(hr, hc)
    nbr, nbc = hr // tr, hc // tc

    def body(idx_ref, a_ref, b_ref, o_ref):
        o_ref[...] = (a_ref[...].astype(F32) + b_ref[...].astype(F32)).astype(BF16)

    def own_map(i, j, k, idx_ref):
        return (i, j + (idx_ref[0] * nbr if ax == 0 else 0), k + (idx_ref[0] * nbc if ax == 1 else 0))

    half_spec = pl.BlockSpec((1, tr, tc), lambda i, j, k, idx_ref: (i, j, k))
    return pl.pallas_call(
        body,
        grid_spec=pltpu.PrefetchScalarGridSpec(
            num_scalar_prefetch=1, grid=(p, nbr, nbc),
            in_specs=[pl.BlockSpec((1, tr, tc), own_map), half_spec],
            out_specs=half_spec),
        out_shape=jax.ShapeDtypeStruct((p, hr, hc), BF16),
        compiler_params=_cp("parallel", "parallel", "parallel"), name=name,
    )(idx, g, got)


def _final_sum(own, parts, idx, shard_shape, *, name):
    ax = _half_axis(shard_shape[0])
    hr, hc = own.shape[1:]
    tr, tc = _tile_2d(hr, hc)
    nbr, nbc = hr // tr, hc // tc

    def body(idx_ref, own_ref, parts_ref, o_ref):
        acc = own_ref[0].astype(F32)
        for q in range(3):
            acc = acc + parts_ref[q].astype(F32)
        o_ref[...] = acc

    def out_map(j, k, idx_ref):
        return (j + (idx_ref[0] * nbr if ax == 0 else 0), k + (idx_ref[0] * nbc if ax == 1 else 0))

    return pl.pallas_call(
        body,
        grid_spec=pltpu.PrefetchScalarGridSpec(
            num_scalar_prefetch=1, grid=(nbr, nbc),
            in_specs=[pl.BlockSpec((1, tr, tc), lambda j, k, idx_ref: (idx_ref[1], j, k)),
                      pl.BlockSpec((3, tr, tc), lambda j, k, idx_ref: (0, j, k))],
            out_specs=pl.BlockSpec((tr, tc), out_map)),
        out_shape=jax.ShapeDtypeStruct(tuple(shard_shape), F32),
        compiler_params=_cp("parallel", "parallel"), name=name,
    )(idx, own, parts)


def _stack_sum(x, *, name):
    p, r, c = x.shape
    tr = _row_tile(r)

    def body(x_ref, o_ref):
        acc = x_ref[0].astype(F32)
        for q in range(1, p):
            acc = acc + x_ref[q].astype(F32)
        o_ref[...] = acc

    return pl.pallas_call(
        body, grid=(r // tr,),
        in_specs=[pl.BlockSpec((p, tr, c), lambda i: (0, i, 0))],
        out_specs=pl.BlockSpec((tr, c), lambda i: (i, 0)),
        out_shape=jax.ShapeDtypeStruct((r, c), F32),
        compiler_params=_cp("parallel"), name=name,
    )(x)


def _place():
    x, y, c = lax.axis_index("x"), lax.axis_index("y"), lax.axis_index("c")
    chips = [(1 - x, y), (x, 1 - y), (1 - x, 1 - y)]
    return x, y, c, chips


def _gather8(x_shard, *, name):
    m_per, n = x_shard.shape

    def body(x_ref, out_ref, send_sems, recv_sems, local_sem):
        x, y, c, chips = _place()
        me, sibling = (x, y, c), (x, y, 1 - c)

        def rows(px, py, pc):
            return out_ref.at[pl.ds((4 * px + 2 * py + pc) * m_per, m_per), :]

        def copy(k, block, to, src=None):
            return pltpu.make_async_remote_copy(
                src_ref=rows(*block) if src is None else src, dst_ref=rows(*block),
                send_sem=send_sems.at[k], recv_sem=recv_sems.at[k], device_id=to, device_id_type=MESH)

        mine = pltpu.make_async_copy(x_ref, rows(*me), local_sem)
        mine.start()
        first = [copy(0, me, sibling, src=x_ref)]
        first += [copy(1 + j, me, (*chip, c), src=x_ref) for j, chip in enumerate(chips)]
        for cp in first:
            cp.start()
        passed = [copy(4 + j, (*chip, c), sibling) for j, chip in enumerate(chips)]
        for j, chip in enumerate(chips):
            copy(1 + j, (*chip, c), me).wait_recv()
            passed[j].start()
        copy(0, sibling, me).wait_recv()
        for j, chip in enumerate(chips):
            copy(4 + j, (*chip, 1 - c), me).wait_recv()
        for cp in first + passed:
            cp.wait_send()
        mine.wait()

    return pl.pallas_call(
        body,
        out_shape=jax.ShapeDtypeStruct((8 * m_per, n), x_shard.dtype),
        in_specs=[pl.BlockSpec(memory_space=pltpu.VMEM)],
        out_specs=pl.BlockSpec(memory_space=pltpu.VMEM),
        scratch_shapes=[pltpu.SemaphoreType.DMA((7,)), pltpu.SemaphoreType.DMA((7,)), pltpu.SemaphoreType.DMA],
        name=name,
    )(x_shard)


def _gather_weights(shards, *, name):
    return _comm_call(lambda ins, outs: [cp for i, o in zip(ins, outs) for cp in _plan_gather_ici(i, o)],
                      shards, [jax.ShapeDtypeStruct((4,) + s.shape, s.dtype) for s in shards], name=name)


def _plan_start(plan, send_sems, recv_sems):
    for k, (src, dst, _, peer) in enumerate(plan):
        pltpu.make_async_remote_copy(src_ref=src, dst_ref=dst, send_sem=send_sems.at[k], recv_sem=recv_sems.at[k],
                                     device_id=peer, device_id_type=MESH).start()


def _plan_wait(plan, send_sems, recv_sems):
    for k, (src, _, land, peer) in enumerate(plan):
        pltpu.make_async_remote_copy(src_ref=src, dst_ref=land, send_sem=send_sems.at[k], recv_sem=recv_sems.at[k],
                                     device_id=peer, device_id_type=MESH).wait_recv()
    for k, (src, dst, _, peer) in enumerate(plan):
        pltpu.make_async_remote_copy(src_ref=src, dst_ref=dst, send_sem=send_sems.at[k], recv_sem=recv_sems.at[k],
                                     device_id=peer, device_id_type=MESH).wait_send()


def _half_axis(rows):
    return 0 if rows % 32 == 0 else 1


def _rows_half(ref, hc, axis, part=None):
    size = ref.shape[axis] // 2
    start = hc * size
    if part is not None:
        size //= part[1]
        start = start + part[0] * size
    idx = [slice(None)] * len(ref.shape)
    idx[axis] = pl.ds(start, size)
    return ref.at[tuple(idx)]


def _plan_gather_ici(shard, full, part=None):
    x, y, c, chips = _place()
    ax = _half_axis(shard.shape[0])
    src = _rows_half(shard, c, ax, part)
    return [(src, _rows_half(full.at[2 * x + y], c, ax, part), _rows_half(full.at[2 * cx + cy], c, ax, part),
             (cx, cy, c)) for cx, cy in chips]


def _plan_gather_d2d(full, own):
    x, y, c, chips = _place()
    ax = _half_axis(full.shape[1])
    plan = []
    for cx, cy in chips:
        slot = full.at[2 * cx + cy]
        plan.append((_rows_half(slot, c, ax), _rows_half(slot, c, ax), _rows_half(slot, 1 - c, ax), (x, y, 1 - c)))
    mine = full.at[2 * x + y]
    plan.append((own, mine, mine, (x, y, 1 - c)))
    return plan


def _plan_pair(grad, got):
    x, y, c, _ = _place()
    return [(_rows_half(grad, 1 - c, 1 + _half_axis(grad.shape[1])), got, got, (x, y, 1 - c))]


def _plan_shard_ici(sums, parts, piece=None):
    _, _, c, chips = _place()

    def rows(ref):
        if piece is None:
            return ref
        k, n = piece
        if ref.shape[0] % (16 * n) == 0:
            size = ref.shape[0] // n
            return ref.at[pl.ds(k * size, size), :]
        size = ref.shape[1] // n
        return ref.at[:, pl.ds(k * size, size)]

    return [(rows(sums.at[2 * cx + cy]), rows(parts.at[k]), rows(parts.at[k]), (cx, cy, c))
            for k, (cx, cy) in enumerate(chips)]


def _plan_half(buf):
    x, y, c, _ = _place()
    ax = _half_axis(buf.shape[0])
    mine = _rows_half(buf, c, ax)
    return [(mine, mine, _rows_half(buf, 1 - c, ax), (x, y, 1 - c))]


def _comm_call(plan_fn, inputs, out_shapes, *, name, aliases=None):
    ni, no = len(inputs), len(out_shapes)

    def body(*refs):
        plan = plan_fn(refs[:ni], refs[ni:ni + no])
        send_sems, recv_sems = refs[ni + no:]
        _plan_start(plan, send_sems, recv_sems)
        _plan_wait(plan, send_sems, recv_sems)

    any_spec = pl.BlockSpec(memory_space=pl.ANY)
    n_copies = 3 * max(ni, no)
    return pl.pallas_call(
        body, out_shape=list(out_shapes), in_specs=[any_spec] * ni, out_specs=[any_spec] * no,
        scratch_shapes=[pltpu.SemaphoreType.DMA((n_copies,)), pltpu.SemaphoreType.DMA((n_copies,))],
        input_output_aliases=aliases or {}, name=name,
    )(*inputs)


def _gather_forward(full, own, *, name):
    return _comm_call(lambda ins, outs: _plan_gather_d2d(outs[0], ins[1]), [full, own],
                      [jax.ShapeDtypeStruct(full.shape, full.dtype)], name=name, aliases={0: 0})[0]


def _half_exchange(bufs, *, name):
    return _comm_call(lambda ins, outs: [cp for o in outs for cp in _plan_half(o)],
                      bufs, [jax.ShapeDtypeStruct(b.shape, b.dtype) for b in bufs], name=name,
                      aliases={k: k for k in range(len(bufs))})


def _split_w_in(w_in_t):
    d = w_in_t.shape[1]
    main = jnp.concatenate([w_in_t[0:3072], w_in_t[3080:5128], w_in_t[5144:6168]], axis=0)
    small = jnp.concatenate([w_in_t[3072:3080], w_in_t[5128:5144], jnp.zeros((SMALL_W - 24, d), w_in_t.dtype)], axis=0)
    return main, small


def _merge_dw_in(dw_main, dw_small):
    return jnp.concatenate([dw_main[0:3072], dw_small[0:8], dw_main[3072:5120], dw_small[8:24], dw_main[5120:6144]],
                           axis=0)


def _gather_side(shards):
    return _Side(shards, [jax.ShapeDtypeStruct((4,) + w.shape, w.dtype) for w in shards],
                 lambda ins, outs: [cp for i, o in zip(ins, outs) for cp in _plan_gather_ici(i, o)], 3 * len(shards))


def _forward_side(full, own):
    return _Side([full, own], [jax.ShapeDtypeStruct(full.shape, full.dtype)],
                 lambda ins, outs: _plan_gather_d2d(outs[0], ins[1]), 4, aliases={0: 0})


def _half_side(bufs):
    return _Side(bufs, [jax.ShapeDtypeStruct(b.shape, b.dtype) for b in bufs],
                 lambda ins, outs: [cp for o in outs for cp in _plan_half(o)], len(bufs),
                 aliases={k: k for k in range(len(bufs))})


def _parts_shape(sums):
    return jax.ShapeDtypeStruct((3,) + sums.shape[1:], sums.dtype)


def _got_shape(grad):
    return jax.ShapeDtypeStruct(_half_shape(grad.shape), grad.dtype)


def _pair_side(grad):
    return _Side([grad], [_got_shape(grad)], lambda ins, outs: _plan_pair(ins[0], outs[0]), 1)


def _local_step(x, target, mod, g_pre_mix, g_post_mix, g_pre_mlp, g_post_mlp, gw_in, b_fgate, w_gla_a2,
                b_gla_a2, g_fox, g_gla, own_w_in, own_w_out, own_w_mlp_in, own_w_mlp_out, idx):
    s, d = x.shape
    shift_m, scale_m, gate_m, shift_f, scale_f, gate_f = [mod[:, i * d:(i + 1) * d] for i in range(6)]
    a1 = g_pre_mix * (1.0 + scale_m)
    a2 = g_pre_mlp * (1.0 + scale_f)
    bf = jnp.concatenate([b_fgate, jnp.zeros((1, SMALL_W - FOX_HEADS), F32)], axis=1)
    w2p = jnp.zeros((SMALL_W, GLA_KW), F32).at[FOX_HEADS:FOX_HEADS + GLA_RANK].set(w_gla_a2)

    h1, gw_in = _pre_fwd(x, a1, shift_m, name="pre_mix_fwd", side=_forward_side(gw_in, own_w_in))
    w_in_t = gw_in.reshape(-1, d)
    w_main, w_small = _split_w_in(w_in_t)
    full_shape = lambda w: jax.ShapeDtypeStruct((4,) + w.shape, w.dtype)
    first_side = _Side(
        [own_w_out, own_w_mlp_out], [full_shape(own_w_out), full_shape(own_w_mlp_out)],
        lambda ins, outs: _plan_gather_ici(ins[0], outs[0]) + _plan_gather_ici(ins[1], outs[1], part=(0, 4)), 6)
    proj, gw_out, gw_mlp_out = _mm(h1, w_main, mode="nt", out_dtypes=[BF16], name="in_proj_main", side=first_side)
    ps, gw_out = _mm(h1, w_small, mode="nt", out_dtypes=[F32], name="in_proj_small",
                     side=_forward_side(gw_out, own_w_out))
    w_out_full = gw_out.reshape(-1, d)
    cum, log_a = _gates_fwd(ps, bf, w2p, b_gla_a2, name="gates_fwd")
    cum_t = cum[:, :FOX_HEADS].T
    o_fox, fox_n, lse, gw_mlp_in = _fox_fwd(proj, cum_t, g_fox, name="fox_fwd", side=_gather_side([own_w_mlp_in]))
    o_gla, gla_n, states = _gla_fwd(proj, log_a, g_gla, name="gla_fwd")
    mixed = jnp.concatenate([fox_n, gla_n], axis=1)
    y1, gw_mlp_in = _mm(mixed, w_out_full, mode="nn", out_dtypes=[F32], name="out_proj",
                        side=_forward_side(gw_mlp_in, own_w_mlp_in))
    x1, h2 = _post_pre_fwd(x, y1, gate_m, g_post_mix, a2, shift_f, name="post_mix_pre_mlp_fwd")

    def mlp_act(acc):
        r = jnp.maximum(acc, 0.0)
        return acc, r * r

    rest_side = _Side([own_w_mlp_out, gw_mlp_out], [full_shape(own_w_mlp_out)],
                      lambda ins, outs: [cp for q in (1, 2, 3) for cp in _plan_gather_ici(ins[0], outs[0], part=(q, 4))],
                      9, aliases={1: 0})
    u, act, gw_mlp_out = _mm(h2, gw_mlp_in, mode="nn", out_dtypes=[BF16, BF16], epi=mlp_act, name="mlp_in",
                             b_slots=4, tm=MM_TM, side=rest_side)
    gw_mlp_out = _gather_forward(gw_mlp_out, own_w_mlp_out, name="gather_w_mlp_out_d2d")
    w_mlp_out_full = gw_mlp_out.reshape(-1, d)
    y2, = _mm(act, w_mlp_out_full, mode="nn", out_dtypes=[F32], name="mlp_out")
    dx2, dy2, loss_part, dgate_f, dg_post_mlp = _post_loss_bwd(x1, y2, gate_f, g_post_mlp, target,
                                                               name="post_mlp_loss_bwd")
    dw_mlp_out, = _mm(act, dy2, mode="tn", out_dtypes=[BF16], name="dw_mlp_out")
    dw_mlp_out = dw_mlp_out.reshape(4, D_FF // 4, d)

    def act_bwd(acc, uv):
        return (acc * (2.0 * jnp.maximum(uv.astype(F32), 0.0)),)

    du, got_mlp_out = _mm(dy2, w_mlp_out_full, mode="nt", out_dtypes=[BF16], extras=[u], epi=act_bwd,
                          name="d_mlp_hidden", tm=MM_TM, side=_pair_side(dw_mlp_out))
    sum_mlp_out = _pair_sum(dw_mlp_out, got_mlp_out, idx, name="grad_pair_sum_mlp_out")
    nj = D_FF // 4 // min(MM_T, D_FF // 4)
    tmw = min(MM_T, d)
    dw_mlp_in, parts_mlp_out = _mm(
        h2, du, mode="tn", out_dtypes=[BF16], name="dw_mlp_in",
        out_shapes=[jax.ShapeDtypeStruct((4, d, D_FF // 4), BF16)],
        out_specs=[pl.BlockSpec((1, tmw, min(MM_T, D_FF // 4)), lambda i, j, kk: (j // nj, i, j % nj))],
        side=_Side([sum_mlp_out], [_parts_shape(sum_mlp_out)],
                   lambda ins, outs: _plan_shard_ici(ins[0], outs[0], piece=(0, 2)), 3))
    dh2, got_mlp_in, parts_mlp_out = _mm(
        du, gw_mlp_in, mode="nt", out_dtypes=[F32], name="d_mlp_in", b_slots=4,
        side=_Side([dw_mlp_in, sum_mlp_out, parts_mlp_out], [_got_shape(dw_mlp_in), _parts_shape(sum_mlp_out)],
                   lambda ins, outs: _plan_pair(ins[0], outs[0]) + _plan_shard_ici(ins[1], outs[1], piece=(1, 2)),
                   4, aliases={2: 1}))
    sum_mlp_in = _pair_sum(dw_mlp_in, got_mlp_in, idx, name="grad_pair_sum_mlp_in")
    dx1, dshift_f, da2, dy1, dgate_m, dg_post_mix = _pre_post_bwd(dh2, x1, dx2, a2, y1, gate_m, g_post_mix,
                                                                  name="pre_mlp_post_mix_bwd")
    buf_mlp_out = _final_sum(sum_mlp_out, parts_mlp_out, idx, (D_FF // 4, d), name="grad_final_sum_mlp_out")
    dw_out, g_mlp_out = _mm(mixed, dy1, mode="tn", out_dtypes=[BF16], name="dw_out", side=_half_side([buf_mlp_out]))
    dw_out = dw_out.reshape(4, d // 4, d)
    dmixed, got_out = _mm(dy1, w_out_full, mode="nt", out_dtypes=[BF16], name="d_mixed", side=_pair_side(dw_out))
    sum_out = _pair_sum(dw_out, got_out, idx, name="grad_pair_sum_out")
    do_fox, delta, dg_fox = _head_norm_bwd(dmixed, o_fox, g_fox, None, nh=FOX_HEADS, hd=FOX_HD, dn_col=0,
                                           gr_col=0, name="fox_norm_bwd")
    do_gla, dgr, _, dg_gla = _head_norm_bwd(dmixed, o_gla, g_gla, proj, nh=GLA_HEADS, hd=GLA_DV, dn_col=1,
                                            gr_col=(3 * FOX_W + 2 * GLA_KW + GLA_W) // GLA_W, name="gla_norm_bwd")
    dq_fox, dk_fox, dv_fox, dcq, dck_t, parts_mlp_in, parts_out = _fox_bwd(
        proj, do_fox, cum_t, lse, delta, name="fox_bwd",
        side=_Side([sum_mlp_in, sum_out], [_parts_shape(sum_mlp_in), _parts_shape(sum_out)],
                   lambda ins, outs: _plan_shard_ici(ins[0], outs[0]) + _plan_shard_ici(ins[1], outs[1]), 6))
    dgq, dgk, dgv, dla = _gla_bwd(proj, log_a, do_gla, states, name="gla_bwd")
    dck = dcq + jnp.concatenate([dck_t.T, jnp.zeros((s, SMALL_W - FOX_HEADS), F32)], axis=1)
    dps, dbf, dw2p, db2 = _gates_bwd(dck, ps, bf, w2p, b_gla_a2, dla, name="gates_bwd")
    dproj = jnp.concatenate([dq_fox.astype(BF16), dk_fox, dv_fox, dgq, dgk, dgv, dgr], axis=1)
    buf_mlp_in = _final_sum(sum_mlp_in, parts_mlp_in, idx, (d, D_FF // 4), name="grad_final_sum_mlp_in")
    buf_out = _final_sum(sum_out, parts_out, idx, (d // 4, d), name="grad_final_sum_out")
    dw_main, g_mlp_in, g_out = _mm(dproj, h1, mode="tn", out_dtypes=[BF16], name="dw_in_main",
                                   side=_half_side([buf_mlp_in, buf_out]))
    dw_small, = _mm(dps, h1, mode="tn", out_dtypes=[BF16], name="dw_in_small")
    rs_in = w_in_t.shape[0] // 4
    dw_in = _merge_dw_in(dw_main, dw_small).reshape(4, rs_in, d)
    dh1_small, got_in = _mm(dps, w_small, mode="nn", out_dtypes=[F32], name="d_h1_small", side=_pair_side(dw_in))
    sum_in = _pair_sum(dw_in, got_in, idx, name="grad_pair_sum_in")
    dh1, parts_in = _mm(
        dproj, w_main, mode="nn", out_dtypes=[F32], extras=[dh1_small], epi=lambda acc, e: (acc + e,), name="d_h1",
        side=_Side([sum_in], [_parts_shape(sum_in)],
                   lambda ins, outs: [cp for q in range(3) for cp in _plan_shard_ici(ins[0], outs[0], piece=(q, 4))],
                   9))
    grad_x, dshift_m, da1, parts_in = _pre_bwd(
        dh1, x, dx1, a1, name="pre_mix_bwd",
        side=_Side([sum_in, parts_in], [_parts_shape(sum_in)],
                   lambda ins, outs: _plan_shard_ici(ins[0], outs[0], piece=(3, 4)), 3, aliases={1: 0}))
    buf_in = _final_sum(sum_in, parts_in, idx, (rs_in, d), name="grad_final_sum_in")
    g_in, = _half_exchange([buf_in], name="grad_half_exchange_in")
    g_big = [g_in, g_out, g_mlp_in, g_mlp_out]

    dmod = jnp.concatenate([dshift_m, da1 * g_pre_mix, dgate_m, dshift_f, da2 * g_pre_mlp, dgate_f], axis=1)
    small = dict(
        dmod=dmod, g_pre_mix=da1 * (1.0 + scale_m), g_post_mix=dg_post_mix, g_pre_mlp=da2 * (1.0 + scale_f),
        g_post_mlp=dg_post_mlp, b_fgate=dbf[:, :FOX_HEADS], w_gla_a2=dw2p[FOX_HEADS:FOX_HEADS + GLA_RANK],
        b_gla_a2=db2, g_fox_out=dg_fox, g_gla_out=dg_gla)
    return loss_part, grad_x, g_big, small


def _pack(arrays):
    flat = jnp.concatenate([a.reshape(-1).astype(F32) for a in arrays])
    n = flat.shape[0]
    rows = -(-n // 128)
    rows = -(-rows // 8) * 8
    return jnp.pad(flat, (0, rows * 128 - n)).reshape(rows, 128)


def _unpack(buf, shapes):
    flat = buf.reshape(-1)
    out, off = [], 0
    for shp in shapes:
        n = 1
        for q in shp:
            n *= q
        out.append(flat[off:off + n].reshape(shp))
        off += n
    return out


SMALL_GRAD_ORDER = ["dmod", "g_pre_mix", "g_post_mix", "g_pre_mlp", "g_post_mlp", "b_fgate", "w_gla_a2", "b_gla_a2",
                    "g_fox_out", "g_gla_out"]


def kernel(x, c, w_ada, b_ada, g_pre_mix, g_post_mix, w_in, b_fgate, w_gla_a2, b_gla_a2, g_fox_out, g_gla_out, w_out, g_pre_mlp, g_post_mlp, w_mlp_in, w_mlp_out, loss_target, m_w_ada, m_b_ada, m_g_pre_mix, m_g_post_mix, m_w_in, m_b_fgate, m_w_gla_a2, m_b_gla_a2, m_g_fox_out, m_g_gla_out, m_w_out, m_g_pre_mlp, m_g_post_mlp, m_w_mlp_in, m_w_mlp_out, v_w_ada, v_b_ada, v_g_pre_mix, v_g_post_mix, v_w_in, v_b_fgate, v_w_gla_a2, v_b_gla_a2, v_g_fox_out, v_g_gla_out, v_w_out, v_g_pre_mlp, v_g_post_mlp, v_w_mlp_in, v_w_mlp_out):
    ix, iy, ic = lax.axis_index("x"), lax.axis_index("y"), lax.axis_index("c")
    chip = 2 * ix + iy
    dev = 4 * ix + 2 * iy + ic
    d = D_MODEL

    c_act = _silu_rows(c, name="silu_c")
    pack1 = _pack([c_act, w_gla_a2[0], g_gla_out[0]])
    rows1 = pack1.shape[0]
    got1 = _gather8(pack1, name="gather_small_fwd").reshape(8, rows1, 128)
    per_dev = [_unpack(got1[q], [(d,), (GLA_RANK, GLA_KW // 4), (GLA_HEADS, GLA_DV // 4)]) for q in range(8)]
    c_all = jnp.stack([p[0] for p in per_dev])
    w_gla_a2_full = jnp.concatenate([per_dev[2 * j][1] for j in range(4)], axis=1)
    g_gla_full = jnp.concatenate([per_dev[2 * j][2] for j in range(4)], axis=1)
    cols = w_ada.shape[2]
    b_ada_shard = lax.dynamic_slice_in_dim(b_ada, chip * cols, cols, axis=1)
    mod_sh = _mod_shard(c_all, w_ada[0], b_ada_shard, name="ada_mod")
    got2 = _gather8(mod_sh, name="gather_mod").reshape(8, 8, cols)
    mod_all = jnp.concatenate([got2[2 * j] for j in range(4)], axis=1)
    mod = lax.dynamic_slice_in_dim(mod_all, dev, 1, axis=0)

    tr_in = lambda a: jnp.transpose(a[0])
    own_bf = [tr_in(w_in).astype(BF16), w_out[0].astype(BF16), w_mlp_in[0].astype(BF16), w_mlp_out[0].astype(BF16)]
    gw_in, = _gather_weights(own_bf[:1], name="gather_w_in_ici")
    idx = jnp.stack([ic, chip]).astype(jnp.int32)
    loss_part, grad_x, g_big, small = _local_step(
        x[0], loss_target[0], mod, g_pre_mix, g_post_mix, g_pre_mlp, g_post_mlp, gw_in, b_fgate,
        w_gla_a2_full, b_gla_a2, g_fox_out[0], g_gla_full, own_bf[0], own_bf[1], own_bf[2], own_bf[3], idx)
    loss = lax.psum(loss_part[0, 0], ("x", "y", "c"))

    big_w = [(tr_in(w_in), tr_in(m_w_in), tr_in(v_w_in)), (w_out[0], m_w_out[0], v_w_out[0]),
             (w_mlp_in[0], m_w_mlp_in[0], v_w_mlp_in[0]), (w_mlp_out[0], m_w_mlp_out[0], v_w_mlp_out[0])]
    big_res = []
    for q, (g, (w, m, v)) in enumerate(zip(g_big, big_w)):
        res4 = (g,) + tuple(_adam(g, w, m, v, name=f"adam_big_{q}"))
        big_res.append(tuple((jnp.transpose(a) if q == 0 else a)[None] for a in res4))

    pack2 = _pack([small[k] for k in SMALL_GRAD_ORDER])
    rows2 = pack2.shape[0]
    got3 = _gather8(pack2, name="gather_small_grads").reshape(8, rows2, 128)
    dmod_all = got3[:, :6 * d // 128, :].reshape(8, 6 * d)
    sums = _stack_sum(got3, name="small_grad_sum")
    shapes = [(1, 6 * d), (1, d), (1, d), (1, d), (1, d), (1, FOX_HEADS), (1, GLA_RANK, GLA_KW), (1, GLA_KW),
              (1, FOX_HEADS, FOX_HD), (1, GLA_HEADS, GLA_DV)]
    sg = dict(zip(["b_ada"] + SMALL_GRAD_ORDER[1:], _unpack(sums, shapes)))
    sg["w_gla_a2"] = lax.dynamic_slice_in_dim(sg["w_gla_a2"], chip * (GLA_KW // 4), GLA_KW // 4, axis=2)
    sg["g_gla_out"] = lax.dynamic_slice_in_dim(sg["g_gla_out"], chip * (GLA_DV // 4), GLA_DV // 4, axis=2)
    small_names = ["b_ada", "g_pre_mix", "g_post_mix", "b_fgate", "w_gla_a2", "b_gla_a2", "g_fox_out", "g_gla_out",
                   "g_pre_mlp", "g_post_mlp"]
    small_w = dict(b_ada=(b_ada, m_b_ada, v_b_ada), g_pre_mix=(g_pre_mix, m_g_pre_mix, v_g_pre_mix),
                   g_post_mix=(g_post_mix, m_g_post_mix, v_g_post_mix), b_fgate=(b_fgate, m_b_fgate, v_b_fgate),
                   w_gla_a2=(w_gla_a2, m_w_gla_a2, v_w_gla_a2), b_gla_a2=(b_gla_a2, m_b_gla_a2, v_b_gla_a2),
                   g_fox_out=(g_fox_out, m_g_fox_out, v_g_fox_out), g_gla_out=(g_gla_out, m_g_gla_out, v_g_gla_out),
                   g_pre_mlp=(g_pre_mlp, m_g_pre_mlp, v_g_pre_mlp), g_post_mlp=(g_post_mlp, m_g_post_mlp, v_g_post_mlp))
    sshapes = [small_w[k][0].shape for k in small_names]
    pg = _pack([sg[k] for k in small_names])
    pw, pm, pv = [_pack([small_w[k][q] for k in small_names]) for q in range(3)]
    pd, pmn, pvn = _adam(pg, pw, pm, pv, name="adam_small")
    s_delta = dict(zip(small_names, _unpack(pd, sshapes)))
    s_m = dict(zip(small_names, _unpack(pmn, sshapes)))
    s_v = dict(zip(small_names, _unpack(pvn, sshapes)))

    dmod_cols = lax.dynamic_slice_in_dim(dmod_all, chip * cols, cols, axis=1)
    g_ada, d_ada, m_ada, v_ada = _ada_grad_adam(c_all.T, dmod_cols, w_ada[0], m_w_ada[0], v_w_ada[0], name="ada_grad_adam")

    order = ["w_ada", "b_ada", "g_pre_mix", "g_post_mix", "w_in", "b_fgate", "w_gla_a2", "b_gla_a2", "g_fox_out",
             "g_gla_out", "w_out", "g_pre_mlp", "g_post_mlp", "w_mlp_in", "w_mlp_out"]
    res = {"w_ada": (g_ada[None], d_ada[None], m_ada[None], v_ada[None]),
           "w_in": big_res[0], "w_out": big_res[1], "w_mlp_in": big_res[2], "w_mlp_out": big_res[3]}
    for k in small_names:
        res[k] = (sg[k], s_delta[k], s_m[k], s_v[k])
    return (loss, grad_x[None], *[res[k][0] for k in order], *[res[k][1] for k in order],
            *[res[k][2] for k in order], *[res[k][3] for k in order])
```

```python
import functools

import jax
import jax.numpy as jnp
from jax import lax
from jax.experimental import pallas as pl
from jax.experimental.pallas import tpu as pltpu

F32 = jnp.float32
BF16 = jnp.bfloat16
MESH = pl.DeviceIdType.MESH
HIGHEST = lax.Precision.HIGHEST

D_MODEL = 2048
FOX_HEADS = 8
FOX_HD = 128
FOX_W = FOX_HEADS * FOX_HD
GLA_HEADS = 4
GLA_DK = 128
GLA_DV = 256
GLA_KW = GLA_HEADS * GLA_DK
GLA_W = GLA_HEADS * GLA_DV
GLA_RANK = 16
GLA_TEMP = 16.0
CHUNK = 64
D_FF = 4 * D_MODEL
EPS = 1e-6
MAIN_W = 3 * FOX_W + 2 * GLA_KW + 2 * GLA_W
SMALL_W = 128
NEG = -1e30

ADAM_LR = 0.001
ADAM_B1 = 0.9
ADAM_B2 = 0.999
ADAM_EPS = 1e-08
ADAM_WD = 0.01
ADAM_STEP = 10

VMEM_LIMIT = 52 * 1024 * 1024
ROW_TILE = 256
WIDE_ROW_TILE = 512
FOX_TQ = 512
FOX_TK = 512
GLA_ROWS = 512
GATE_TS = 512
MM_T = 1024
MM_TK = 2048
MM_TM = 2048


def _cp(*sem):
    return pltpu.CompilerParams(dimension_semantics=sem, vmem_limit_bytes=VMEM_LIMIT)


def _dot_nn(a, b, precision=None):
    return jnp.dot(a, b, preferred_element_type=F32, precision=precision)


def _dot_nt(a, b, precision=None):
    return lax.dot_general(a, b, (((1,), (1,)), ((), ())), preferred_element_type=F32, precision=precision)


def _dot_tn(a, b, precision=None):
    return lax.dot_general(a, b, (((0,), (0,)), ((), ())), preferred_element_type=F32, precision=precision)


def _sigmoid(x):
    return 1.0 / (1.0 + jnp.exp(-x))


def _log_sigmoid(x):
    return jnp.minimum(x, 0.0) - jnp.log(1.0 + jnp.exp(-jnp.abs(x)))


class _Side:
    def __init__(self, inputs, out_shapes, plan_fn, n_copies, aliases=None):
        self.inputs, self.out_shapes, self.plan_fn, self.n_copies = list(inputs), list(out_shapes), plan_fn, n_copies
        self.aliases = dict(aliases or {})

    def scratch(self):
        return [pltpu.SemaphoreType.DMA((self.n_copies,)), pltpu.SemaphoreType.DMA((self.n_copies,))]

    def run(self, in_refs, out_refs, sems, first, last):
        @pl.when(first)
        def _():
            _plan_start(self.plan_fn(in_refs, out_refs), *sems)

        @pl.when(last)
        def _():
            _plan_wait(self.plan_fn(in_refs, out_refs), *sems)


def _mm(a, b, *, mode, out_dtypes, name, tm=None, tn=None, tk=None, extras=(), epi=None,
        out_shapes=None, out_specs=None, side=None, b_slots=0):
    tm, tn, tk = tm or MM_T, tn or MM_T, tk or MM_TK
    b2 = (b.shape[1], b_slots * b.shape[2]) if b_slots else b.shape
    if mode == "nn":
        (m, k), n = a.shape, b2[1]
    elif mode == "nt":
        (m, k), n = a.shape, b2[0]
    else:
        (k, m), n = a.shape, b2[1]
    tm, tn, tk = min(tm, m), min(tn, n), min(tk, k)
    if b_slots:
        tn = min(tn, b.shape[2]) if mode == "nn" else tn
        tk = min(tk, b.shape[2]) if mode == "nt" else tk
    assert m % tm == 0 and n % tn == 0 and k % tk == 0, (name, m, n, k)
    nk = k // tk
    n_out, n_ex = len(out_dtypes), len(extras)
    if epi is None:
        epi = lambda acc: tuple(acc for _ in range(n_out))
    dot = {"nn": _dot_nn, "nt": _dot_nt, "tn": _dot_tn}[mode]

    n_si = len(side.inputs) if side else 0
    n_so = len(side.out_shapes) if side else 0
    grid = (m // tm, n // tn, nk)

    def body(*refs):
        a_ref, b_ref = refs[0], refs[1]
        ex_refs = refs[2:2 + n_ex]
        base = 2 + n_ex + n_si
        o_refs = refs[base:base + n_out]
        scratch = refs[base + n_out + n_so:]
        if side:
            pos = [pl.program_id(q) for q in range(3)]
            first = (pos[0] == 0) & (pos[1] == 0) & (pos[2] == 0)
            last = (pos[0] == grid[0] - 1) & (pos[1] == grid[1] - 1) & (pos[2] == grid[2] - 1)
            side.run(refs[2 + n_ex:base], refs[base + n_out:base + n_out + n_so], scratch[-2:], first, last)
        part = dot(a_ref[...], b_ref[...])

        def finish(acc):
            outs = epi(acc, *[e[...] for e in ex_refs])
            for o_ref, val in zip(o_refs, outs):
                o_ref[...] = val.reshape(o_ref.shape).astype(o_ref.dtype)

        if nk == 1:
            finish(part)
        else:
            acc_ref = scratch[0]
            kk = pl.program_id(2)

            @pl.when(kk == 0)
            def _():
                acc_ref[...] = part

            @pl.when(kk > 0)
            def _():
                acc_ref[...] += part

            @pl.when(kk == nk - 1)
            def _():
                finish(acc_ref[...])

    if mode == "nn":
        a_spec = pl.BlockSpec((tm, tk), lambda i, j, kk: (i, kk))
        b_spec = pl.BlockSpec((tk, tn), lambda i, j, kk: (kk, j))
        if b_slots:
            per = b.shape[2] // tn
            b_spec = pl.BlockSpec((None, tk, tn), lambda i, j, kk: (j // per, kk, j % per))
    elif mode == "nt":
        a_spec = pl.BlockSpec((tm, tk), lambda i, j, kk: (i, kk))
        b_spec = pl.BlockSpec((tn, tk), lambda i, j, kk: (j, kk))
        if b_slots:
            per = b.shape[2] // tk
            b_spec = pl.BlockSpec((None, tn, tk), lambda i, j, kk: (kk // per, j, kk % per))
    else:
        assert not b_slots
        a_spec = pl.BlockSpec((tk, tm), lambda i, j, kk: (kk, i))
        b_spec = pl.BlockSpec((tk, tn), lambda i, j, kk: (kk, j))
    tile_spec = pl.BlockSpec((tm, tn), lambda i, j, kk: (i, j))
    if out_shapes is None:
        out_shapes = [jax.ShapeDtypeStruct((m, n), dt) for dt in out_dtypes]
    if out_specs is None:
        out_specs = [tile_spec for _ in out_dtypes]
    any_spec = pl.BlockSpec(memory_space=pl.ANY)
    res = pl.pallas_call(
        body,
        grid=grid,
        in_specs=[a_spec, b_spec] + [tile_spec for _ in extras] + [any_spec] * n_si,
        out_specs=list(out_specs) + [any_spec] * n_so,
        out_shape=list(out_shapes) + (side.out_shapes if side else []),
        scratch_shapes=([pltpu.VMEM((tm, tn), F32)] if nk > 1 else []) + (side.scratch() if side else []),
        compiler_params=_cp("arbitrary", "arbitrary", "arbitrary") if side else _cp("parallel", "parallel", "arbitrary"),
        input_output_aliases={2 + n_ex + si: n_out + so for si, so in side.aliases.items()} if side else {},
        name=name,
    )(a, b, *extras, *(side.inputs if side else []))
    return res


def _row_spec(ts, d):
    return pl.BlockSpec((ts, d), lambda i: (i, 0))


def _vec_spec(d):
    return pl.BlockSpec((1, d), lambda i: (0, 0))


def _side_args(side, n_in, n_out):
    if side is None:
        return [], [], [], [], [], {}
    any_spec = pl.BlockSpec(memory_space=pl.ANY)
    return ([any_spec] * len(side.inputs), [any_spec] * len(side.out_shapes), side.out_shapes, side.scratch(),
            side.inputs, {n_in + si: n_out + so for si, so in side.aliases.items()})


def _pre_fwd(x, avec, shift, *, name, side=None):
    s, d = x.shape
    ts = min(WIDE_ROW_TILE, s)
    nb = s // ts
    s_in, s_out, s_shapes, s_scratch, s_ops, s_alias = _side_args(side, 3, 1)

    def body(x_ref, a_ref, s_ref, *rest):
        h_ref = rest[len(s_in)]
        if side:
            step = pl.program_id(0)
            side.run(rest[:len(s_in)], rest[len(s_in) + 1:len(s_in) + 1 + len(s_out)],
                     rest[len(s_in) + 1 + len(s_out):], step == 0, step == nb - 1)
        xv = x_ref[...]
        r = lax.rsqrt(jnp.mean(xv * xv, axis=-1, keepdims=True) + EPS)
        h_ref[...] = (xv * r * a_ref[...] + s_ref[...]).astype(BF16)

    res = pl.pallas_call(
        body, grid=(nb,),
        in_specs=[_row_spec(ts, d), _vec_spec(d), _vec_spec(d)] + s_in,
        out_specs=[_row_spec(ts, d)] + s_out,
        out_shape=[jax.ShapeDtypeStruct((s, d), BF16)] + s_shapes,
        scratch_shapes=s_scratch, input_output_aliases=s_alias,
        compiler_params=_cp("arbitrary" if side else "parallel"), name=name,
    )(x, avec, shift, *s_ops)
    return res if side else res[0]


def _post_pre_fwd(x, y, gate, g, avec, shift, *, name):
    s, d = x.shape
    ts = min(WIDE_ROW_TILE, s)

    def body(x_ref, y_ref, gate_ref, g_ref, a_ref, s_ref, o_ref, h_ref):
        yv = y_ref[...]
        r = lax.rsqrt(jnp.mean(yv * yv, axis=-1, keepdims=True) + EPS)
        x1 = x_ref[...] + gate_ref[...] * (yv * r * g_ref[...])
        o_ref[...] = x1
        r1 = lax.rsqrt(jnp.mean(x1 * x1, axis=-1, keepdims=True) + EPS)
        h_ref[...] = (x1 * r1 * a_ref[...] + s_ref[...]).astype(BF16)

    return pl.pallas_call(
        body, grid=(s // ts,),
        in_specs=[_row_spec(ts, d), _row_spec(ts, d)] + [_vec_spec(d)] * 4,
        out_specs=[_row_spec(ts, d), _row_spec(ts, d)],
        out_shape=[jax.ShapeDtypeStruct((s, d), F32), jax.ShapeDtypeStruct((s, d), BF16)],
        compiler_params=_cp("parallel"), name=name,
    )(x, y, gate, g, avec, shift)


def _post_bwd_math(dxv, yv, gatev, gv):
    r = lax.rsqrt(jnp.mean(yv * yv, axis=-1, keepdims=True) + EPS)
    yhat = yv * r
    dn = dxv * gatev
    dyhat = dn * gv
    dy = r * (dyhat - yhat * jnp.mean(dyhat * yhat, axis=-1, keepdims=True))
    return dy, dxv * (yhat * gv), dn * yhat


def _accumulate(first, pairs):
    @pl.when(first)
    def _():
        for ref, _ in pairs:
            ref[...] = jnp.zeros_like(ref)

    for ref, val in pairs:
        ref[...] += jnp.sum(val, axis=0, keepdims=True)


def _post_loss_bwd(x, y, gate, g, target, *, name):
    s, d = x.shape
    ts = min(ROW_TILE, s)

    def body(x_ref, y_ref, gate_ref, g_ref, t_ref, dx_ref, dy_ref, loss_ref, dgate_ref, dg_ref):
        yv, gatev, gv = y_ref[...], gate_ref[...], g_ref[...]
        r = lax.rsqrt(jnp.mean(yv * yv, axis=-1, keepdims=True) + EPS)
        diff = x_ref[...] + gatev * (yv * r * gv) - t_ref[...]
        dxv = diff * (1.0 / d)
        dx_ref[...] = dxv
        dy, dgate_rows, dg_rows = _post_bwd_math(dxv, yv, gatev, gv)
        dy_ref[...] = dy.astype(BF16)
        first = pl.program_id(0) == 0
        _accumulate(first, [(dgate_ref, dgate_rows), (dg_ref, dg_rows)])

        @pl.when(first)
        def _():
            loss_ref[...] = jnp.zeros_like(loss_ref)

        loss_ref[...] += jnp.sum(jnp.mean(diff * diff, axis=-1, keepdims=True)) * 0.5

    return pl.pallas_call(
        body, grid=(s // ts,),
        in_specs=[_row_spec(ts, d), _row_spec(ts, d), _vec_spec(d), _vec_spec(d), _row_spec(ts, d)],
        out_specs=[_row_spec(ts, d), _row_spec(ts, d), pl.BlockSpec((1, 128), lambda i: (0, 0)), _vec_spec(d),
                   _vec_spec(d)],
        out_shape=[jax.ShapeDtypeStruct((s, d), F32), jax.ShapeDtypeStruct((s, d), BF16),
                   jax.ShapeDtypeStruct((1, 128), F32), jax.ShapeDtypeStruct((1, d), F32),
                   jax.ShapeDtypeStruct((1, d), F32)],
        compiler_params=_cp("arbitrary"), name=name,
    )(x, y, gate, g, target)


def _pre_post_bwd(dh, xin, dres, avec, y, gate, g, *, name):
    s, d = xin.shape
    ts = min(ROW_TILE, s)

    def body(dh_ref, x_ref, dres_ref, a_ref, y_ref, gate_ref, g_ref, dx_ref, dshift_ref, da_ref, dy_ref,
             dgate_ref, dg_ref):
        xv, dhv = x_ref[...], dh_ref[...]
        r = lax.rsqrt(jnp.mean(xv * xv, axis=-1, keepdims=True) + EPS)
        xhat = xv * r
        dxhat = dhv * a_ref[...]
        dxv = dres_ref[...] + r * (dxhat - xhat * jnp.mean(dxhat * xhat, axis=-1, keepdims=True))
        dx_ref[...] = dxv
        dy, dgate_rows, dg_rows = _post_bwd_math(dxv, y_ref[...], gate_ref[...], g_ref[...])
        dy_ref[...] = dy.astype(BF16)
        _accumulate(pl.program_id(0) == 0, [(dshift_ref, dhv), (da_ref, dhv * xhat), (dgate_ref, dgate_rows),
                                            (dg_ref, dg_rows)])

    return pl.pallas_call(
        body, grid=(s // ts,),
        in_specs=[_row_spec(ts, d), _row_spec(ts, d), _row_spec(ts, d), _vec_spec(d), _row_spec(ts, d),
                  _vec_spec(d), _vec_spec(d)],
        out_specs=[_row_spec(ts, d), _vec_spec(d), _vec_spec(d), _row_spec(ts, d), _vec_spec(d), _vec_spec(d)],
        out_shape=[jax.ShapeDtypeStruct((s, d), F32), jax.ShapeDtypeStruct((1, d), F32),
                   jax.ShapeDtypeStruct((1, d), F32), jax.ShapeDtypeStruct((s, d), BF16),
                   jax.ShapeDtypeStruct((1, d), F32), jax.ShapeDtypeStruct((1, d), F32)],
        compiler_params=_cp("arbitrary"), name=name,
    )(dh, xin, dres, avec, y, gate, g)


def _pre_bwd(dh, xin, dres, avec, *, name, side=None):
    s, d = xin.shape
    ts = min(WIDE_ROW_TILE, s)
    nb = s // ts
    s_in, s_out, s_shapes, s_scratch, s_ops, s_alias = _side_args(side, 4, 3)

    def body(dh_ref, x_ref, dres_ref, a_ref, *rest):
        dx_ref, dshift_ref, da_ref = rest[len(s_in):len(s_in) + 3]
        if side:
            step = pl.program_id(0)
            side.run(rest[:len(s_in)], rest[len(s_in) + 3:len(s_in) + 3 + len(s_out)],
                     rest[len(s_in) + 3 + len(s_out):], step == 0, step == nb - 1)
        xv, dhv = x_ref[...], dh_ref[...]
        r = lax.rsqrt(jnp.mean(xv * xv, axis=-1, keepdims=True) + EPS)
        xhat = xv * r
        dxhat = dhv * a_ref[...]
        dx_ref[...] = dres_ref[...] + r * (dxhat - xhat * jnp.mean(dxhat * xhat, axis=-1, keepdims=True))

        @pl.when(pl.program_id(0) == 0)
        def _():
            dshift_ref[...] = jnp.zeros_like(dshift_ref)
            da_ref[...] = jnp.zeros_like(da_ref)

        dshift_ref[...] += jnp.sum(dhv, axis=0, keepdims=True)
        da_ref[...] += jnp.sum(dhv * xhat, axis=0, keepdims=True)

    return pl.pallas_call(
        body, grid=(nb,),
        in_specs=[_row_spec(ts, d), _row_spec(ts, d), _row_spec(ts, d), _vec_spec(d)] + s_in,
        out_specs=[_row_spec(ts, d), _vec_spec(d), _vec_spec(d)] + s_out,
        out_shape=[jax.ShapeDtypeStruct((s, d), F32), jax.ShapeDtypeStruct((1, d), F32),
                   jax.ShapeDtypeStruct((1, d), F32)] + s_shapes,
        scratch_shapes=s_scratch, input_output_aliases=s_alias,
        compiler_params=_cp("arbitrary"), name=name,
    )(dh, xin, dres, avec, *s_ops)


def _tri(n, strict=False, upper=False):
    r = lax.broadcasted_iota(jnp.int32, (n, n), 0)
    c = lax.broadcasted_iota(jnp.int32, (n, n), 1)
    if upper:
        r, c = c, r
    return ((r > c) if strict else (r >= c)).astype(F32)


def _gates_fwd(ps, bf, w2p, b2, *, name):
    s = ps.shape[0]
    ts = min(GATE_TS, s)

    def body(ps_ref, bf_ref, w_ref, b2_ref, cum_ref, la_ref, carry_ref):
        @pl.when(pl.program_id(0) == 0)
        def _():
            carry_ref[...] = jnp.zeros_like(carry_ref)

        psv = ps_ref[...]
        lf = _log_sigmoid(psv + bf_ref[...])
        cum = _dot_nn(_tri(ts), lf, HIGHEST) + carry_ref[...]
        cum_ref[...] = cum
        carry_ref[...] = cum[ts - 1:ts, :]
        z = _dot_nn(psv, w_ref[...], HIGHEST) + b2_ref[...]
        la_ref[...] = _log_sigmoid(z) * (1.0 / GLA_TEMP)

    return pl.pallas_call(
        body, grid=(s // ts,),
        in_specs=[_row_spec(ts, SMALL_W), _vec_spec(SMALL_W),
                  pl.BlockSpec((SMALL_W, GLA_KW), lambda i: (0, 0)), _vec_spec(GLA_KW)],
        out_specs=[_row_spec(ts, SMALL_W), _row_spec(ts, GLA_KW)],
        out_shape=[jax.ShapeDtypeStruct((s, SMALL_W), F32), jax.ShapeDtypeStruct((s, GLA_KW), F32)],
        scratch_shapes=[pltpu.VMEM((1, SMALL_W), F32)],
        compiler_params=_cp("arbitrary"), name=name,
    )(ps, bf, w2p, b2)


def _gates_bwd(dck, ps, bf, w2p, b2, dla, *, name):
    s = ps.shape[0]
    ts = min(GATE_TS, s)
    nb = s // ts
    rev = lambda i: (nb - 1 - i, 0)

    def body(dck_ref, ps_ref, bf_ref, w_ref, b2_ref, dla_ref, dps_ref, dbf_ref, dw_ref, db2_ref, carry_ref):
        @pl.when(pl.program_id(0) == 0)
        def _():
            carry_ref[...] = jnp.zeros_like(carry_ref)
            dbf_ref[...] = jnp.zeros_like(dbf_ref)
            dw_ref[...] = jnp.zeros_like(dw_ref)
            db2_ref[...] = jnp.zeros_like(db2_ref)

        psv, dckv = ps_ref[...], dck_ref[...]
        dlf = _dot_nn(_tri(ts, upper=True), dckv, HIGHEST) + carry_ref[...]
        carry_ref[...] += jnp.sum(dckv, axis=0, keepdims=True)
        lane = lax.broadcasted_iota(jnp.int32, (ts, SMALL_W), 1)
        dff = jnp.where(lane < FOX_HEADS, dlf * _sigmoid(-(psv + bf_ref[...])), 0.0)
        z = _dot_nn(psv, w_ref[...], HIGHEST) + b2_ref[...]
        dz = dla_ref[...] * _sigmoid(-z) * (1.0 / GLA_TEMP)
        dps_ref[...] = (_dot_nt(dz, w_ref[...], HIGHEST) + dff).astype(BF16)
        dbf_ref[...] += jnp.sum(dff, axis=0, keepdims=True)
        dw_ref[...] += _dot_tn(psv, dz, HIGHEST)
        db2_ref[...] += jnp.sum(dz, axis=0, keepdims=True)

    return pl.pallas_call(
        body, grid=(nb,),
        in_specs=[pl.BlockSpec((ts, SMALL_W), rev), pl.BlockSpec((ts, SMALL_W), rev), _vec_spec(SMALL_W),
                  pl.BlockSpec((SMALL_W, GLA_KW), lambda i: (0, 0)), _vec_spec(GLA_KW),
                  pl.BlockSpec((ts, GLA_KW), rev)],
        out_specs=[pl.BlockSpec((ts, SMALL_W), rev), _vec_spec(SMALL_W),
                   pl.BlockSpec((SMALL_W, GLA_KW), lambda i: (0, 0)), _vec_spec(GLA_KW)],
        out_shape=[jax.ShapeDtypeStruct((s, SMALL_W), BF16), jax.ShapeDtypeStruct((1, SMALL_W), F32),
                   jax.ShapeDtypeStruct((SMALL_W, GLA_KW), F32), jax.ShapeDtypeStruct((1, GLA_KW), F32)],
        scratch_shapes=[pltpu.VMEM((1, SMALL_W), F32)],
        compiler_params=_cp("arbitrary"), name=name,
    )(dck, ps, bf, w2p, b2, dla)


def _hs(h, hd=FOX_HD):
    return slice(h * hd, (h + 1) * hd)


def _fox_fwd(proj, cum_t, g_fox, *, name, side=None):
    s = proj.shape[0]
    tq, tk = min(FOX_TQ, s), min(FOX_TK, s)
    scale = FOX_HD ** -0.5
    n_si = len(side.inputs) if side else 0
    n_so = len(side.out_shapes) if side else 0
    grid = (s // tq, s // tk)

    def body(*refs):
        q_ref, k_ref, v_ref, ck_ref, g_ref = refs[:5]
        o_ref, n_ref, lse_ref = refs[5 + n_si:8 + n_si]
        m_sc, acc_sc = refs[8 + n_si + n_so:10 + n_si + n_so]
        i, j = pl.program_id(0), pl.program_id(1)
        if side:
            side.run(refs[5:5 + n_si], refs[8 + n_si:8 + n_si + n_so], refs[10 + n_si + n_so:],
                     (i == 0) & (j == 0), (i == grid[0] - 1) & (j == grid[1] - 1))

        @pl.when(j == 0)
        def _():
            m_sc[...] = jnp.full_like(m_sc, NEG)
            acc_sc[...] = jnp.zeros_like(acc_sc)

        def block(masked):
            mask = _causal_mask(i, j, tq, tk) if masked else None
            ones = jnp.ones((tk, FOX_HD), BF16)
            for h in range(FOX_HEADS):
                sc = _fox_logits(_dot_nt(q_ref[:, _hs(h)], k_ref[:, _hs(h)]), ck_ref[h:h + 1, :], mask, scale)
                m_prev = m_sc[h]
                m_new = jnp.maximum(m_prev, jnp.max(sc, axis=-1, keepdims=True))
                alpha = jnp.exp(m_prev - m_new)
                p = jnp.exp(sc - m_new).astype(BF16)
                v_one = jnp.concatenate([v_ref[:, _hs(h)], ones], axis=1)
                acc_sc[:, _hs(h, 2 * FOX_HD)] = alpha * acc_sc[:, _hs(h, 2 * FOX_HD)] + _dot_nn(p, v_one)
                m_sc[h] = m_new

        pl.when(j < i)(functools.partial(block, False))

        @pl.when(j == i)
        def _():
            block(True)
            lane = lax.broadcasted_iota(jnp.int32, (tq, 128), 1)
            lse = jnp.zeros((tq, 128), F32)
            for h in range(FOX_HEADS):
                l_rep = acc_sc[:, 2 * h * FOX_HD + FOX_HD:2 * (h + 1) * FOX_HD]
                o = acc_sc[:, 2 * h * FOX_HD:2 * h * FOX_HD + FOX_HD] / l_rep
                o_ref[:, _hs(h)] = o
                r = lax.rsqrt(jnp.mean(o * o, axis=-1, keepdims=True) + EPS)
                n_ref[:, _hs(h)] = (o * r * g_ref[h:h + 1, :]).astype(BF16)
                lse = jnp.where(lane == h, m_sc[h] + jnp.log(l_rep), lse)
            lse_ref[...] = lse

    kv = lambda col: (lambda i, j: (jnp.minimum(j, i), col))
    any_spec = pl.BlockSpec(memory_space=pl.ANY)
    return pl.pallas_call(
        body, grid=grid,
        in_specs=[pl.BlockSpec((tq, FOX_W), lambda i, j: (i, 0)),
                  pl.BlockSpec((tk, FOX_W), kv(1)),
                  pl.BlockSpec((tk, FOX_W), kv(2)),
                  pl.BlockSpec((FOX_HEADS, tk), lambda i, j: (0, jnp.minimum(j, i))),
                  pl.BlockSpec((FOX_HEADS, FOX_HD), lambda i, j: (0, 0))] + [any_spec] * n_si,
        out_specs=[pl.BlockSpec((tq, FOX_W), lambda i, j: (i, 0)),
                   pl.BlockSpec((tq, FOX_W), lambda i, j: (i, 0)),
                   pl.BlockSpec((tq, 128), lambda i, j: (i, 0))] + [any_spec] * n_so,
        out_shape=[jax.ShapeDtypeStruct((s, FOX_W), F32), jax.ShapeDtypeStruct((s, FOX_W), BF16),
                   jax.ShapeDtypeStruct((s, 128), F32)] + (side.out_shapes if side else []),
        scratch_shapes=[pltpu.VMEM((FOX_HEADS, tq, 1), F32), pltpu.VMEM((tq, 2 * FOX_W), F32)]
        + (side.scratch() if side else []),
        compiler_params=_cp("arbitrary", "arbitrary"), name=name,
    )(proj, proj, proj, cum_t, g_fox, *(side.inputs if side else []))


def _causal_mask(i, j, tq, tk):
    rows = i * tq + lax.broadcasted_iota(jnp.int32, (tq, tk), 0)
    cols = j * tk + lax.broadcasted_iota(jnp.int32, (tq, tk), 1)
    return rows >= cols


def _fox_logits(qk, ck, mask, scale):
    sc = qk * scale - ck
    return sc if mask is None else jnp.where(mask, sc, NEG)


def _fox_bwd(proj, do, cum_t, lse, delta, *, name, side=None):
    s = proj.shape[0]
    tq, tk = min(FOX_TQ, s), min(FOX_TK, s)
    nk, nq = s // tk, s // tq
    scale = FOX_HD ** -0.5
    n_si = len(side.inputs) if side else 0
    n_so = len(side.out_shapes) if side else 0

    def body(*refs):
        q_ref, k_ref, v_ref, do_ref, ck_ref, lse_ref, dl_ref = refs[:7]
        dq_hbm, dk_ref, dv_ref, dcq_hbm, dck_ref = refs[7 + n_si:12 + n_si]
        dq_sc, dcq_sc, dk_sc, dv_sc, dck_sc, out_sems = refs[12 + n_si + n_so:18 + n_si + n_so]
        j, i = pl.program_id(0), pl.program_id(1)
        if side:
            side.run(refs[7:7 + n_si], refs[12 + n_si:12 + n_si + n_so], refs[18 + n_si + n_so:],
                     (j == 0) & (i == 0), (j == nk - 1) & (i == nq - 1))

        @pl.when((j == 0) & (i == 0))
        def _():
            dq_sc[...] = jnp.zeros_like(dq_sc)
            dcq_sc[...] = jnp.zeros_like(dcq_sc)

        @pl.when(i == 0)
        def _():
            dk_sc[...] = jnp.zeros_like(dk_sc)
            dv_sc[...] = jnp.zeros_like(dv_sc)
            dck_sc[...] = jnp.zeros_like(dck_sc)

        def block(masked):
            mask = _causal_mask(i, j, tq, tk) if masked else None
            qrows = pl.ds(pl.multiple_of(i * tq, tq), tq)
            for h in range(FOX_HEADS):
                sc = _fox_logits(_dot_nt(q_ref[:, _hs(h)], k_ref[:, _hs(h)]), ck_ref[h:h + 1, :], mask, scale)
                p = jnp.exp(sc - lse_ref[:, h:h + 1])
                ds = p * (_dot_nt(do_ref[:, _hs(h)], v_ref[:, _hs(h)]) - dl_ref[:, h:h + 1])
                dsb = ds.astype(BF16)
                dv_sc[:, _hs(h)] += _dot_tn(p.astype(BF16), do_ref[:, _hs(h)])
                dk_sc[:, _hs(h)] += _dot_tn(dsb, q_ref[:, _hs(h)])
                dq_sc[qrows, _hs(h)] += _dot_nn(dsb, k_ref[:, _hs(h)]) * scale
                dck_sc[h:h + 1, :] -= jnp.sum(ds, axis=0, keepdims=True)
                dcq_sc[qrows, h:h + 1] += jnp.sum(ds, axis=-1, keepdims=True)

        pl.when(i > j)(functools.partial(block, False))
        pl.when(i == j)(functools.partial(block, True))

        @pl.when(i == nq - 1)
        def _():
            dk_ref[...] = (dk_sc[...] * scale).astype(BF16)
            dv_ref[...] = dv_sc[...].astype(BF16)
            dck_ref[...] = dck_sc[...]

        @pl.when((j == nk - 1) & (i == nq - 1))
        def _():
            out_q = pltpu.make_async_copy(dq_sc, dq_hbm, out_sems.at[0])
            out_c = pltpu.make_async_copy(dcq_sc, dcq_hbm, out_sems.at[1])
            out_q.start()
            out_c.start()
            out_q.wait()
            out_c.wait()

    qrow = lambda j, i: (jnp.maximum(i, j), 0)
    krow = lambda col: (lambda j, i: (j, col))
    any_spec = pl.BlockSpec(memory_space=pl.ANY)
    return pl.pallas_call(
        body, grid=(nk, nq),
        in_specs=[pl.BlockSpec((tq, FOX_W), qrow), pl.BlockSpec((tk, FOX_W), krow(1)),
                  pl.BlockSpec((tk, FOX_W), krow(2)),
                  pl.BlockSpec((tq, FOX_W), qrow),
                  pl.BlockSpec((FOX_HEADS, tk), lambda j, i: (0, j)),
                  pl.BlockSpec((tq, 128), qrow), pl.BlockSpec((tq, 128), qrow)] + [any_spec] * n_si,
        out_specs=[any_spec, pl.BlockSpec((tk, FOX_W), lambda j, i: (j, 0)),
                   pl.BlockSpec((tk, FOX_W), lambda j, i: (j, 0)), any_spec,
                   pl.BlockSpec((FOX_HEADS, tk), lambda j, i: (0, j))] + [any_spec] * n_so,
        out_shape=[jax.ShapeDtypeStruct((s, FOX_W), F32), jax.ShapeDtypeStruct((s, FOX_W), BF16),
                   jax.ShapeDtypeStruct((s, FOX_W), BF16), jax.ShapeDtypeStruct((s, 128), F32),
                   jax.ShapeDtypeStruct((FOX_HEADS, s), F32)] + (side.out_shapes if side else []),
        scratch_shapes=[pltpu.VMEM((s, FOX_W), F32), pltpu.VMEM((s, 128), F32),
                        pltpu.VMEM((tk, FOX_W), F32), pltpu.VMEM((tk, FOX_W), F32), pltpu.VMEM((FOX_HEADS, tk), F32),
                        pltpu.SemaphoreType.DMA((2,))] + (side.scratch() if side else []),
        compiler_params=_cp("arbitrary", "arbitrary"), name=name,
    )(proj, proj, proj, do, cum_t, lse, delta, *(side.inputs if side else []))


def _head_norm_bwd(dn_in, o, g, gr_src, *, nh, hd, dn_col, gr_col, name):
    s, w = o.shape
    ts = min(ROW_TILE, s)
    gated = gr_src is not None

    def body(*refs):
        if gated:
            dn_ref, o_ref, g_ref, gr_ref, do_ref, dgr_ref, dl_ref, dg_ref = refs
        else:
            dn_ref, o_ref, g_ref, do_ref, dl_ref, dg_ref = refs

        @pl.when(pl.program_id(0) == 0)
        def _():
            dg_ref[...] = jnp.zeros_like(dg_ref)

        lane = lax.broadcasted_iota(jnp.int32, (ts, 128), 1)
        delta = jnp.zeros((ts, 128), F32)
        for h in range(nh):
            sl = _hs(h, hd)
            ov = o_ref[:, sl]
            dnv = dn_ref[:, sl].astype(F32)
            gv = g_ref[h:h + 1, :]
            r = lax.rsqrt(jnp.mean(ov * ov, axis=-1, keepdims=True) + EPS)
            ohat = ov * r
            if gated:
                grv = gr_ref[:, sl].astype(F32)
                sig = _sigmoid(grv)
                dgr_ref[:, sl] = (dnv * (ohat * gv) * (sig * (1.0 + grv * (1.0 - sig)))).astype(BF16)
                dnv = dnv * (grv * sig)
            dg_ref[h:h + 1, :] += jnp.sum(dnv * ohat, axis=0, keepdims=True)
            dohat = dnv * gv
            do = r * (dohat - ohat * jnp.mean(dohat * ohat, axis=-1, keepdims=True))
            do_ref[:, sl] = do.astype(BF16)
            delta = jnp.where(lane == h, jnp.sum(do.astype(BF16).astype(F32) * ov, axis=-1, keepdims=True), delta)
        dl_ref[...] = delta

    in_specs = [pl.BlockSpec((ts, w), lambda i: (i, dn_col)), _row_spec(ts, w),
                pl.BlockSpec((nh, hd), lambda i: (0, 0))]
    args = [dn_in, o, g]
    out_specs = [_row_spec(ts, w)]
    out_shape = [jax.ShapeDtypeStruct((s, w), BF16)]
    if gated:
        in_specs.append(pl.BlockSpec((ts, w), lambda i: (i, gr_col)))
        args.append(gr_src)
        out_specs.append(_row_spec(ts, w))
        out_shape.append(jax.ShapeDtypeStruct((s, w), BF16))
    out_specs += [_row_spec(ts, 128), pl.BlockSpec((nh, hd), lambda i: (0, 0))]
    out_shape += [jax.ShapeDtypeStruct((s, 128), F32), jax.ShapeDtypeStruct((nh, hd), F32)]
    return pl.pallas_call(
        body, grid=(s // ts,), in_specs=in_specs, out_specs=out_specs, out_shape=out_shape,
        compiler_params=_cp("arbitrary"), name=name,
    )(*args)


GQ_BLK = 3 * FOX_W // GLA_DK
GK_BLK = GQ_BLK + GLA_HEADS
GV_BLK = (3 * FOX_W + 2 * GLA_KW) // GLA_DV
GR_BLK = GV_BLK + GLA_HEADS


def _gla_chunk_terms(la):
    cum = _dot_nn(_tri(CHUNK), la, HIGHEST)
    total = cum[CHUNK - 1:CHUNK, :]
    return jnp.exp(total - cum), jnp.exp(total)


def _gla_fwd(proj, log_a, g_gla, *, name):
    s = proj.shape[0]
    rows = min(GLA_ROWS, s)
    cb = rows // CHUNK
    nblk = s // rows
    scale = GLA_DK ** -0.5

    def body(q_ref, k_ref, v_ref, gr_ref, la_ref, g_ref, o_ref, n_ref, st_ref, st_sc):
        h = pl.program_id(0)

        @pl.when(pl.program_id(1) == 0)
        def _():
            st_sc[...] = jnp.zeros_like(st_sc)

        gv = g_ref[pl.ds(h, 1), :]
        for ci in range(cb):
            sl = slice(ci * CHUNK, (ci + 1) * CHUNK)
            e, dec = _gla_chunk_terms(la_ref[sl, :])
            k_dec = (k_ref[sl, :].astype(F32) * e).astype(BF16)
            st = st_sc[...] * dec + _dot_tn(v_ref[sl, :], k_dec)
            st_sc[...] = st
            st_ref[0, ci] = st
            qs = (q_ref[sl, :].astype(F32) * scale).astype(BF16)
            o = _dot_nt(qs, st.astype(BF16))
            o_ref[sl, :] = o
            r = lax.rsqrt(jnp.mean(o * o, axis=-1, keepdims=True) + EPS)
            grv = gr_ref[sl, :].astype(F32)
            n_ref[sl, :] = (o * r * gv * (grv * _sigmoid(grv))).astype(BF16)

    return pl.pallas_call(
        body, grid=(GLA_HEADS, nblk),
        in_specs=[pl.BlockSpec((rows, GLA_DK), lambda h, n: (n, GQ_BLK + h)),
                  pl.BlockSpec((rows, GLA_DK), lambda h, n: (n, GK_BLK + h)),
                  pl.BlockSpec((rows, GLA_DV), lambda h, n: (n, GV_BLK + h)),
                  pl.BlockSpec((rows, GLA_DV), lambda h, n: (n, GR_BLK + h)),
                  pl.BlockSpec((rows, GLA_DK), lambda h, n: (n, h)),
                  pl.BlockSpec((GLA_HEADS, GLA_DV), lambda h, n: (0, 0))],
        out_specs=[pl.BlockSpec((rows, GLA_DV), lambda h, n: (n, h)),
                   pl.BlockSpec((rows, GLA_DV), lambda h, n: (n, h)),
                   pl.BlockSpec((1, cb, GLA_DV, GLA_DK), lambda h, n: (h, n, 0, 0))],
        out_shape=[jax.ShapeDtypeStruct((s, GLA_W), F32), jax.ShapeDtypeStruct((s, GLA_W), BF16),
                   jax.ShapeDtypeStruct((GLA_HEADS, s // CHUNK, GLA_DV, GLA_DK), F32)],
        scratch_shapes=[pltpu.VMEM((GLA_DV, GLA_DK), F32)],
        compiler_params=_cp("parallel", "arbitrary"), name=name,
    )(proj, proj, proj, proj, log_a, g_gla)


def _gla_bwd(proj, log_a, do, states, *, name):
    s = proj.shape[0]
    rows = min(GLA_ROWS, s)
    cb = rows // CHUNK
    nblk = s // rows
    scale = GLA_DK ** -0.5

    def body(q_ref, k_ref, v_ref, la_ref, do_ref, st_ref, prev_ref, dq_ref, dk_ref, dv_ref, dla_ref, g_sc):
        nrev = pl.program_id(1)
        blk = nblk - 1 - nrev

        @pl.when(nrev == 0)
        def _():
            g_sc[...] = jnp.zeros_like(g_sc)

        for ci in reversed(range(cb)):
            sl = slice(ci * CHUNK, (ci + 1) * CHUNK)
            e, dec = _gla_chunk_terms(la_ref[sl, :])
            kd = k_ref[sl, :].astype(F32) * e
            qs = (q_ref[sl, :].astype(F32) * scale).astype(BF16)
            dov = do_ref[sl, :]
            st = st_ref[0, ci]
            if ci > 0:
                st_prev = st_ref[0, ci - 1]
            else:
                st_prev = prev_ref[0, 0] * (blk > 0).astype(F32)
            dq_ref[sl, :] = (_dot_nn(dov, st.astype(BF16)) * scale).astype(BF16)
            gt = g_sc[...] + _dot_tn(dov, qs)
            gtb = gt.astype(BF16)
            dkd = _dot_nn(v_ref[sl, :], gtb)
            dv_ref[sl, :] = _dot_nt(kd.astype(BF16), gtb).astype(BF16)
            dk_ref[sl, :] = (dkd * e).astype(BF16)
            ddec = jnp.sum(gt * st_prev, axis=0, keepdims=True) * dec
            dla_ref[sl, :] = _dot_nn(_tri(CHUNK, strict=True), dkd * kd, HIGHEST) + ddec
            g_sc[...] = gt * dec

    rev = lambda col0: (lambda h, n: (nblk - 1 - n, col0 + h))
    return pl.pallas_call(
        body, grid=(GLA_HEADS, nblk),
        in_specs=[pl.BlockSpec((rows, GLA_DK), rev(GQ_BLK)),
                  pl.BlockSpec((rows, GLA_DK), rev(GK_BLK)),
                  pl.BlockSpec((rows, GLA_DV), rev(GV_BLK)),
                  pl.BlockSpec((rows, GLA_DK), rev(0)),
                  pl.BlockSpec((rows, GLA_DV), rev(0)),
                  pl.BlockSpec((1, cb, GLA_DV, GLA_DK), lambda h, n: (h, nblk - 1 - n, 0, 0)),
                  pl.BlockSpec((1, 1, GLA_DV, GLA_DK),
                               lambda h, n: (h, jnp.maximum((nblk - 1 - n) * cb - 1, 0), 0, 0))],
        out_specs=[pl.BlockSpec((rows, GLA_DK), rev(0)), pl.BlockSpec((rows, GLA_DK), rev(0)),
                   pl.BlockSpec((rows, GLA_DV), rev(0)), pl.BlockSpec((rows, GLA_DK), rev(0))],
        out_shape=[jax.ShapeDtypeStruct((s, GLA_KW), BF16), jax.ShapeDtypeStruct((s, GLA_KW), BF16),
                   jax.ShapeDtypeStruct((s, GLA_W), BF16), jax.ShapeDtypeStruct((s, GLA_KW), F32)],
        scratch_shapes=[pltpu.VMEM((GLA_DV, GLA_DK), F32)],
        compiler_params=_cp("parallel", "arbitrary"), name=name,
    )(proj, proj, proj, log_a, do, states, states)


def _row_tile(r):
    tr = min(ROW_TILE, r)
    while r % tr or tr % 8:
        tr -= 1
    return tr


def _adamw_math(w, g, m, v):
    m = ADAM_B1 * m + (1.0 - ADAM_B1) * g
    v = ADAM_B2 * v + (1.0 - ADAM_B2) * (g * g)
    m_hat = m / (1.0 - ADAM_B1 ** ADAM_STEP)
    v_hat = v / (1.0 - ADAM_B2 ** ADAM_STEP)
    delta = -ADAM_LR * (m_hat / (jnp.sqrt(v_hat) + ADAM_EPS) + ADAM_WD * w)
    return delta, m, v


COL_TILE = 256


def _tile_2d(r, c):
    if r % 8 == 0 and _row_tile(r) >= 64:
        return _row_tile(r), c
    assert c % COL_TILE == 0, (r, c)
    return r, COL_TILE


def _half_shape(shape):
    r, c = shape[-2:]
    return tuple(shape[:-2]) + ((r // 2, c) if _half_axis(r) == 0 else (r, c // 2))


def _adam(g, w, m, v, *, name):
    r, c = w.shape
    tr, tc = _tile_2d(r, c)

    def body(g_ref, w_ref, m_ref, v_ref, d_ref, mo_ref, vo_ref):
        d, mn, vn = _adamw_math(w_ref[...], g_ref[...], m_ref[...], v_ref[...])
        d_ref[...] = d
        mo_ref[...] = mn
        vo_ref[...] = vn

    spec = pl.BlockSpec((tr, tc), lambda i, j: (i, j))
    return pl.pallas_call(
        body, grid=(r // tr, c // tc), in_specs=[spec] * 4, out_specs=[spec] * 3,
        out_shape=[jax.ShapeDtypeStruct((r, c), F32)] * 3,
        compiler_params=_cp("parallel", "parallel"), name=name,
    )(g, w, m, v)


def _ada_grad_adam(c_all_t, dmod_cols, w, m, v, *, name):
    r, c = w.shape
    tr, tc = min(512, r), min(1024, c)

    def body(ct_ref, dm_ref, w_ref, m_ref, v_ref, g_ref, d_ref, mo_ref, vo_ref):
        g = _dot_nn(ct_ref[...], dm_ref[...], HIGHEST)
        g_ref[...] = g
        d, mn, vn = _adamw_math(w_ref[...], g, m_ref[...], v_ref[...])
        d_ref[...] = d
        mo_ref[...] = mn
        vo_ref[...] = vn

    spec = pl.BlockSpec((tr, tc), lambda i, j: (i, j))
    nb = c_all_t.shape[1]
    return pl.pallas_call(
        body, grid=(r // tr, c // tc),
        in_specs=[pl.BlockSpec((tr, nb), lambda i, j: (i, 0)), pl.BlockSpec((nb, tc), lambda i, j: (0, j)),
                  spec, spec, spec],
        out_specs=[spec] * 4, out_shape=[jax.ShapeDtypeStruct((r, c), F32)] * 4,
        compiler_params=_cp("parallel", "parallel"), name=name,
    )(c_all_t, dmod_cols, w, m, v)


def _mod_shard(c_all, w, b, *, name):
    k, c = w.shape
    tc = min(512, c)
    nb = c_all.shape[0]

    def body(c_ref, w_ref, b_ref, o_ref):
        o_ref[...] = _dot_nn(c_ref[...], w_ref[...], HIGHEST) + b_ref[...]

    return pl.pallas_call(
        body, grid=(c // tc,),
        in_specs=[pl.BlockSpec((nb, k), lambda j: (0, 0)), pl.BlockSpec((k, tc), lambda j: (0, j)),
                  pl.BlockSpec((1, tc), lambda j: (0, j))],
        out_specs=pl.BlockSpec((nb, tc), lambda j: (0, j)),
        out_shape=jax.ShapeDtypeStruct((nb, c), F32),
        compiler_params=_cp("parallel"), name=name,
    )(c_all, w, b)


def _silu_rows(c, *, name):
    def body(c_ref, o_ref):
        cv = c_ref[...]
        o_ref[...] = cv * _sigmoid(cv)

    return pl.pallas_call(body, out_shape=jax.ShapeDtypeStruct(c.shape, F32), name=name)(c)


def _pair_sum(g, got, idx, *, name):
    p, r, c = g.shape
    ax = _half_axis(r)
    hr, hc = _half_shape((r, c))
    tr, tc = _tile_2d(hr, hc)
    nbr, nbc = hr // tr, hc // tc

    def body(idx_ref, a_ref, b_ref, o_ref):
        o_ref[...] = (a_ref[...].astype(F32) + b_ref[...].astype(F32)).astype(BF16)

    def own_map(i, j, k, idx_ref):
        return (i, j + (idx_ref[0] * nbr if ax == 0 else 0), k + (idx_ref[0] * nbc if ax == 1 else 0))

    half_spec = pl.BlockSpec((1, tr, tc), lambda i, j, k, idx_ref: (i, j, k))
    return pl.pallas_call(
        body,
        grid_spec=pltpu.PrefetchScalarGridSpec(
            num_scalar_prefetch=1, grid=(p, nbr, nbc),
            in_specs=[pl.BlockSpec((1, tr, tc), own_map), half_spec],
            out_specs=half_spec),
        out_shape=jax.ShapeDtypeStruct((p, hr, hc), BF16),
        compiler_params=_cp("parallel", "parallel", "parallel"), name=name,
    )(idx, g, got)


def _final_sum(own, parts, idx, shard_shape, *, name):
    ax = _half_axis(shard_shape[0])
    hr, hc = own.shape[1:]
    tr, tc = _tile_2d(hr, hc)
    nbr, nbc = hr // tr, hc // tc

    def body(idx_ref, own_ref, parts_ref, o_ref):
        acc = own_ref[0].astype(F32)
        for q in range(3):
            acc = acc + parts_ref[q].astype(F32)
        o_ref[...] = acc

    def out_map(j, k, idx_ref):
        return (j + (idx_ref[0] * nbr if ax == 0 else 0), k + (idx_ref[0] * nbc if ax == 1 else 0))

    return pl.pallas_call(
        body,
        grid_spec=pltpu.PrefetchScalarGridSpec(
            num_scalar_prefetch=1, grid=(nbr, nbc),
            in_specs=[pl.BlockSpec((1, tr, tc), lambda j, k, idx_ref: (idx_ref[1], j, k)),
                      pl.BlockSpec((3, tr, tc), lambda j, k, idx_ref: (0, j, k))],
            out_specs=pl.BlockSpec((tr, tc), out_map)),
        out_shape=jax.ShapeDtypeStruct(tuple(shard_shape), F32),
        compiler_params=_cp("parallel", "parallel"), name=name,
    )(idx, own, parts)


def _stack_sum(x, *, name):
    p, r, c = x.shape
    tr = _row_tile(r)

    def body(x_ref, o_ref):
        acc = x_ref[0].astype(F32)
        for q in range(1, p):
            acc = acc + x_ref[q].astype(F32)
        o_ref[...] = acc

    return pl.pallas_call(
        body, grid=(r // tr,),
        in_specs=[pl.BlockSpec((p, tr, c), lambda i: (0, i, 0))],
        out_specs=pl.BlockSpec((tr, c), lambda i: (i, 0)),
        out_shape=jax.ShapeDtypeStruct((r, c), F32),
        compiler_params=_cp("parallel"), name=name,
    )(x)


def _place():
    x, y, c = lax.axis_index("x"), lax.axis_index("y"), lax.axis_index("c")
    chips = [(1 - x, y), (x, 1 - y), (1 - x, 1 - y)]
    return x, y, c, chips


def _gather8(x_shard, *, name):
    m_per, n = x_shard.shape

    def body(x_ref, out_ref, send_sems, recv_sems, local_sem):
        x, y, c, chips = _place()
        me, sibling = (x, y, c), (x, y, 1 - c)

        def rows(px, py, pc):
            return out_ref.at[pl.ds((4 * px + 2 * py + pc) * m_per, m_per), :]

        def copy(k, block, to, src=None):
            return pltpu.make_async_remote_copy(
                src_ref=rows(*block) if src is None else src, dst_ref=rows(*block),
                send_sem=send_sems.at[k], recv_sem=recv_sems.at[k], device_id=to, device_id_type=MESH)

        mine = pltpu.make_async_copy(x_ref, rows(*me), local_sem)
        mine.start()
        first = [copy(0, me, sibling, src=x_ref)]
        first += [copy(1 + j, me, (*chip, c), src=x_ref) for j, chip in enumerate(chips)]
        for cp in first:
            cp.start()
        passed = [copy(4 + j, (*chip, c), sibling) for j, chip in enumerate(chips)]
        for j, chip in enumerate(chips):
            copy(1 + j, (*chip, c), me).wait_recv()
            passed[j].start()
        copy(0, sibling, me).wait_recv()
        for j, chip in enumerate(chips):
            copy(4 + j, (*chip, 1 - c), me).wait_recv()
        for cp in first + passed:
            cp.wait_send()
        mine.wait()

    return pl.pallas_call(
        body,
        out_shape=jax.ShapeDtypeStruct((8 * m_per, n), x_shard.dtype),
        in_specs=[pl.BlockSpec(memory_space=pltpu.VMEM)],
        out_specs=pl.BlockSpec(memory_space=pltpu.VMEM),
        scratch_shapes=[pltpu.SemaphoreType.DMA((7,)), pltpu.SemaphoreType.DMA((7,)), pltpu.SemaphoreType.DMA],
        name=name,
    )(x_shard)


def _gather_weights(shards, *, name):
    return _comm_call(lambda ins, outs: [cp for i, o in zip(ins, outs) for cp in _plan_gather_ici(i, o)],
                      shards, [jax.ShapeDtypeStruct((4,) + s.shape, s.dtype) for s in shards], name=name)


def _plan_start(plan, send_sems, recv_sems):
    for k, (src, dst, _, peer) in enumerate(plan):
        pltpu.make_async_remote_copy(src_ref=src, dst_ref=dst, send_sem=send_sems.at[k], recv_sem=recv_sems.at[k],
                                     device_id=peer, device_id_type=MESH).start()


def _plan_wait(plan, send_sems, recv_sems):
    for k, (src, _, land, peer) in enumerate(plan):
        pltpu.make_async_remote_copy(src_ref=src, dst_ref=land, send_sem=send_sems.at[k], recv_sem=recv_sems.at[k],
                                     device_id=peer, device_id_type=MESH).wait_recv()
    for k, (src, dst, _, peer) in enumerate(plan):
        pltpu.make_async_remote_copy(src_ref=src, dst_ref=dst, send_sem=send_sems.at[k], recv_sem=recv_sems.at[k],
                                     device_id=peer, device_id_type=MESH).wait_send()


def _half_axis(rows):
    return 0 if rows % 32 == 0 else 1


def _rows_half(ref, hc, axis, part=None):
    size = ref.shape[axis] // 2
    start = hc * size
    if part is not None:
        size //= part[1]
        start = start + part[0] * size
    idx = [slice(None)] * len(ref.shape)
    idx[axis] = pl.ds(start, size)
    return ref.at[tuple(idx)]


def _plan_gather_ici(shard, full, part=None):
    x, y, c, chips = _place()
    ax = _half_axis(shard.shape[0])
    src = _rows_half(shard, c, ax, part)
    return [(src, _rows_half(full.at[2 * x + y], c, ax, part), _rows_half(full.at[2 * cx + cy], c, ax, part),
             (cx, cy, c)) for cx, cy in chips]


def _plan_gather_d2d(full, own):
    x, y, c, chips = _place()
    ax = _half_axis(full.shape[1])
    plan = []
    for cx, cy in chips:
        slot = full.at[2 * cx + cy]
        plan.append((_rows_half(slot, c, ax), _rows_half(slot, c, ax), _rows_half(slot, 1 - c, ax), (x, y, 1 - c)))
    mine = full.at[2 * x + y]
    plan.append((own, mine, mine, (x, y, 1 - c)))
    return plan


def _plan_pair(grad, got):
    x, y, c, _ = _place()
    return [(_rows_half(grad, 1 - c, 1 + _half_axis(grad.shape[1])), got, got, (x, y, 1 - c))]


def _plan_shard_ici(sums, parts, piece=None):
    _, _, c, chips = _place()

    def rows(ref):
        if piece is None:
            return ref
        k, n = piece
        if ref.shape[0] % (16 * n) == 0:
            size = ref.shape[0] // n
            return ref.at[pl.ds(k * size, size), :]
        size = ref.shape[1] // n
        return ref.at[:, pl.ds(k * size, size)]

    return [(rows(sums.at[2 * cx + cy]), rows(parts.at[k]), rows(parts.at[k]), (cx, cy, c))
            for k, (cx, cy) in enumerate(chips)]


def _plan_half(buf):
    x, y, c, _ = _place()
    ax = _half_axis(buf.shape[0])
    mine = _rows_half(buf, c, ax)
    return [(mine, mine, _rows_half(buf, 1 - c, ax), (x, y, 1 - c))]


def _comm_call(plan_fn, inputs, out_shapes, *, name, aliases=None):
    ni, no = len(inputs), len(out_shapes)

    def body(*refs):
        plan = plan_fn(refs[:ni], refs[ni:ni + no])
        send_sems, recv_sems = refs[ni + no:]
        _plan_start(plan, send_sems, recv_sems)
        _plan_wait(plan, send_sems, recv_sems)

    any_spec = pl.BlockSpec(memory_space=pl.ANY)
    n_copies = 3 * max(ni, no)
    return pl.pallas_call(
        body, out_shape=list(out_shapes), in_specs=[any_spec] * ni, out_specs=[any_spec] * no,
        scratch_shapes=[pltpu.SemaphoreType.DMA((n_copies,)), pltpu.SemaphoreType.DMA((n_copies,))],
        input_output_aliases=aliases or {}, name=name,
    )(*inputs)


def _gather_forward(full, own, *, name):
    return _comm_call(lambda ins, outs: _plan_gather_d2d(outs[0], ins[1]), [full, own],
                      [jax.ShapeDtypeStruct(full.shape, full.dtype)], name=name, aliases={0: 0})[0]


def _half_exchange(bufs, *, name):
    return _comm_call(lambda ins, outs: [cp for o in outs for cp in _plan_half(o)],
                      bufs, [jax.ShapeDtypeStruct(b.shape, b.dtype) for b in bufs], name=name,
                      aliases={k: k for k in range(len(bufs))})


def _split_w_in(w_in_t):
    d = w_in_t.shape[1]
    main = jnp.concatenate([w_in_t[0:3072], w_in_t[3080:5128], w_in_t[5144:6168]], axis=0)
    small = jnp.concatenate([w_in_t[3072:3080], w_in_t[5128:5144], jnp.zeros((SMALL_W - 24, d), w_in_t.dtype)], axis=0)
    return main, small


def _merge_dw_in(dw_main, dw_small):
    return jnp.concatenate([dw_main[0:3072], dw_small[0:8], dw_main[3072:5120], dw_small[8:24], dw_main[5120:6144]],
                           axis=0)


def _gather_side(shards):
    return _Side(shards, [jax.ShapeDtypeStruct((4,) + w.shape, w.dtype) for w in shards],
                 lambda ins, outs: [cp for i, o in zip(ins, outs) for cp in _plan_gather_ici(i, o)], 3 * len(shards))


def _forward_side(full, own):
    return _Side([full, own], [jax.ShapeDtypeStruct(full.shape, full.dtype)],
                 lambda ins, outs: _plan_gather_d2d(outs[0], ins[1]), 4, aliases={0: 0})


def _half_side(bufs):
    return _Side(bufs, [jax.ShapeDtypeStruct(b.shape, b.dtype) for b in bufs],
                 lambda ins, outs: [cp for o in outs for cp in _plan_half(o)], len(bufs),
                 aliases={k: k for k in range(len(bufs))})


def _parts_shape(sums):
    return jax.ShapeDtypeStruct((3,) + sums.shape[1:], sums.dtype)


def _got_shape(grad):
    return jax.ShapeDtypeStruct(_half_shape(grad.shape), grad.dtype)


def _pair_side(grad):
    return _Side([grad], [_got_shape(grad)], lambda ins, outs: _plan_pair(ins[0], outs[0]), 1)


def _local_step(x, target, mod, g_pre_mix, g_post_mix, g_pre_mlp, g_post_mlp, gw_in, b_fgate, w_gla_a2,
                b_gla_a2, g_fox, g_gla, own_w_in, own_w_out, own_w_mlp_in, own_w_mlp_out, idx):
    s, d = x.shape
    shift_m, scale_m, gate_m, shift_f, scale_f, gate_f = [mod[:, i * d:(i + 1) * d] for i in range(6)]
    a1 = g_pre_mix * (1.0 + scale_m)
    a2 = g_pre_mlp * (1.0 + scale_f)
    bf = jnp.concatenate([b_fgate, jnp.zeros((1, SMALL_W - FOX_HEADS), F32)], axis=1)
    w2p = jnp.zeros((SMALL_W, GLA_KW), F32).at[FOX_HEADS:FOX_HEADS + GLA_RANK].set(w_gla_a2)

    h1, gw_in = _pre_fwd(x, a1, shift_m, name="pre_mix_fwd", side=_forward_side(gw_in, own_w_in))
    w_in_t = gw_in.reshape(-1, d)
    w_main, w_small = _split_w_in(w_in_t)
    full_shape = lambda w: jax.ShapeDtypeStruct((4,) + w.shape, w.dtype)
    first_side = _Side(
        [own_w_out, own_w_mlp_out], [full_shape(own_w_out), full_shape(own_w_mlp_out)],
        lambda ins, outs: _plan_gather_ici(ins[0], outs[0]) + _plan_gather_ici(ins[1], outs[1], part=(0, 4)), 6)
    proj, gw_out, gw_mlp_out = _mm(h1, w_main, mode="nt", out_dtypes=[BF16], name="in_proj_main", side=first_side)
    ps, gw_out = _mm(h1, w_small, mode="nt", out_dtypes=[F32], name="in_proj_small",
                     side=_forward_side(gw_out, own_w_out))
    w_out_full = gw_out.reshape(-1, d)
    cum, log_a = _gates_fwd(ps, bf, w2p, b_gla_a2, name="gates_fwd")
    cum_t = cum[:, :FOX_HEADS].T
    o_fox, fox_n, lse, gw_mlp_in = _fox_fwd(proj, cum_t, g_fox, name="fox_fwd", side=_gather_side([own_w_mlp_in]))
    o_gla, gla_n, states = _gla_fwd(proj, log_a, g_gla, name="gla_fwd")
    mixed = jnp.concatenate([fox_n, gla_n], axis=1)
    y1, gw_mlp_in = _mm(mixed, w_out_full, mode="nn", out_dtypes=[F32], name="out_proj",
                        side=_forward_side(gw_mlp_in, own_w_mlp_in))
    x1, h2 = _post_pre_fwd(x, y1, gate_m, g_post_mix, a2, shift_f, name="post_mix_pre_mlp_fwd")

    def mlp_act(acc):
        r = jnp.maximum(acc, 0.0)
        return acc, r * r

    rest_side = _Side([own_w_mlp_out, gw_mlp_out], [full_shape(own_w_mlp_out)],
                      lambda ins, outs: [cp for q in (1, 2, 3) for cp in _plan_gather_ici(ins[0], outs[0], part=(q, 4))],
                      9, aliases={1: 0})
    u, act, gw_mlp_out = _mm(h2, gw_mlp_in, mode="nn", out_dtypes=[BF16, BF16], epi=mlp_act, name="mlp_in",
                             b_slots=4, tm=MM_TM, side=rest_side)
    gw_mlp_out = _gather_forward(gw_mlp_out, own_w_mlp_out, name="gather_w_mlp_out_d2d")
    w_mlp_out_full = gw_mlp_out.reshape(-1, d)
    y2, = _mm(act, w_mlp_out_full, mode="nn", out_dtypes=[F32], name="mlp_out")
    dx2, dy2, loss_part, dgate_f, dg_post_mlp = _post_loss_bwd(x1, y2, gate_f, g_post_mlp, target,
                                                               name="post_mlp_loss_bwd")
    dw_mlp_out, = _mm(act, dy2, mode="tn", out_dtypes=[BF16], name="dw_mlp_out")
    dw_mlp_out = dw_mlp_out.reshape(4, D_FF // 4, d)

    def act_bwd(acc, uv):
        return (acc * (2.0 * jnp.maximum(uv.astype(F32), 0.0)),)

    du, got_mlp_out = _mm(dy2, w_mlp_out_full, mode="nt", out_dtypes=[BF16], extras=[u], epi=act_bwd,
                          name="d_mlp_hidden", tm=MM_TM, side=_pair_side(dw_mlp_out))
    sum_mlp_out = _pair_sum(dw_mlp_out, got_mlp_out, idx, name="grad_pair_sum_mlp_out")
    nj = D_FF // 4 // min(MM_T, D_FF // 4)
    tmw = min(MM_T, d)
    dw_mlp_in, parts_mlp_out = _mm(
        h2, du, mode="tn", out_dtypes=[BF16], name="dw_mlp_in",
        out_shapes=[jax.ShapeDtypeStruct((4, d, D_FF // 4), BF16)],
        out_specs=[pl.BlockSpec((1, tmw, min(MM_T, D_FF // 4)), lambda i, j, kk: (j // nj, i, j % nj))],
        side=_Side([sum_mlp_out], [_parts_shape(sum_mlp_out)],
                   lambda ins, outs: _plan_shard_ici(ins[0], outs[0], piece=(0, 2)), 3))
    dh2, got_mlp_in, parts_mlp_out = _mm(
        du, gw_mlp_in, mode="nt", out_dtypes=[F32], name="d_mlp_in", b_slots=4,
        side=_Side([dw_mlp_in, sum_mlp_out, parts_mlp_out], [_got_shape(dw_mlp_in), _parts_shape(sum_mlp_out)],
                   lambda ins, outs: _plan_pair(ins[0], outs[0]) + _plan_shard_ici(ins[1], outs[1], piece=(1, 2)),
                   4, aliases={2: 1}))
    sum_mlp_in = _pair_sum(dw_mlp_in, got_mlp_in, idx, name="grad_pair_sum_mlp_in")
    dx1, dshift_f, da2, dy1, dgate_m, dg_post_mix = _pre_post_bwd(dh2, x1, dx2, a2, y1, gate_m, g_post_mix,
                                                                  name="pre_mlp_post_mix_bwd")
    buf_mlp_out = _final_sum(sum_mlp_out, parts_mlp_out, idx, (D_FF // 4, d), name="grad_final_sum_mlp_out")
    dw_out, g_mlp_out = _mm(mixed, dy1, mode="tn", out_dtypes=[BF16], name="dw_out", side=_half_side([buf_mlp_out]))
    dw_out = dw_out.reshape(4, d // 4, d)
    dmixed, got_out = _mm(dy1, w_out_full, mode="nt", out_dtypes=[BF16], name="d_mixed", side=_pair_side(dw_out))
    sum_out = _pair_sum(dw_out, got_out, idx, name="grad_pair_sum_out")
    do_fox, delta, dg_fox = _head_norm_bwd(dmixed, o_fox, g_fox, None, nh=FOX_HEADS, hd=FOX_HD, dn_col=0,
                                           gr_col=0, name="fox_norm_bwd")
    do_gla, dgr, _, dg_gla = _head_norm_bwd(dmixed, o_gla, g_gla, proj, nh=GLA_HEADS, hd=GLA_DV, dn_col=1,
                                            gr_col=(3 * FOX_W + 2 * GLA_KW + GLA_W) // GLA_W, name="gla_norm_bwd")
    dq_fox, dk_fox, dv_fox, dcq, dck_t, parts_mlp_in, parts_out = _fox_bwd(
        proj, do_fox, cum_t, lse, delta, name="fox_bwd",
        side=_Side([sum_mlp_in, sum_out], [_parts_shape(sum_mlp_in), _parts_shape(sum_out)],
                   lambda ins, outs: _plan_shard_ici(ins[0], outs[0]) + _plan_shard_ici(ins[1], outs[1]), 6))
    dgq, dgk, dgv, dla = _gla_bwd(proj, log_a, do_gla, states, name="gla_bwd")
    dck = dcq + jnp.concatenate([dck_t.T, jnp.zeros((s, SMALL_W - FOX_HEADS), F32)], axis=1)
    dps, dbf, dw2p, db2 = _gates_bwd(dck, ps, bf, w2p, b_gla_a2, dla, name="gates_bwd")
    dproj = jnp.concatenate([dq_fox.astype(BF16), dk_fox, dv_fox, dgq, dgk, dgv, dgr], axis=1)
    buf_mlp_in = _final_sum(sum_mlp_in, parts_mlp_in, idx, (d, D_FF // 4), name="grad_final_sum_mlp_in")
    buf_out = _final_sum(sum_out, parts_out, idx, (d // 4, d), name="grad_final_sum_out")
    dw_main, g_mlp_in, g_out = _mm(dproj, h1, mode="tn", out_dtypes=[BF16], name="dw_in_main",
                                   side=_half_side([buf_mlp_in, buf_out]))
    dw_small, = _mm(dps, h1, mode="tn", out_dtypes=[BF16], name="dw_in_small")
    rs_in = w_in_t.shape[0] // 4
    dw_in = _merge_dw_in(dw_main, dw_small).reshape(4, rs_in, d)
    dh1_small, got_in = _mm(dps, w_small, mode="nn", out_dtypes=[F32], name="d_h1_small", side=_pair_side(dw_in))
    sum_in = _pair_sum(dw_in, got_in, idx, name="grad_pair_sum_in")
    dh1, parts_in = _mm(
        dproj, w_main, mode="nn", out_dtypes=[F32], extras=[dh1_small], epi=lambda acc, e: (acc + e,), name="d_h1",
        side=_Side([sum_in], [_parts_shape(sum_in)],
                   lambda ins, outs: [cp for q in range(3) for cp in _plan_shard_ici(ins[0], outs[0], piece=(q, 4))],
                   9))
    grad_x, dshift_m, da1, parts_in = _pre_bwd(
        dh1, x, dx1, a1, name="pre_mix_bwd",
        side=_Side([sum_in, parts_in], [_parts_shape(sum_in)],
                   lambda ins, outs: _plan_shard_ici(ins[0], outs[0], piece=(3, 4)), 3, aliases={1: 0}))
    buf_in = _final_sum(sum_in, parts_in, idx, (rs_in, d), name="grad_final_sum_in")
    g_in, = _half_exchange([buf_in], name="grad_half_exchange_in")
    g_big = [g_in, g_out, g_mlp_in, g_mlp_out]

    dmod = jnp.concatenate([dshift_m, da1 * g_pre_mix, dgate_m, dshift_f, da2 * g_pre_mlp, dgate_f], axis=1)
    small = dict(
        dmod=dmod, g_pre_mix=da1 * (1.0 + scale_m), g_post_mix=dg_post_mix, g_pre_mlp=da2 * (1.0 + scale_f),
        g_post_mlp=dg_post_mlp, b_fgate=dbf[:, :FOX_HEADS], w_gla_a2=dw2p[FOX_HEADS:FOX_HEADS + GLA_RANK],
        b_gla_a2=db2, g_fox_out=dg_fox, g_gla_out=dg_gla)
    return loss_part, grad_x, g_big, small


def _pack(arrays):
    flat = jnp.concatenate([a.reshape(-1).astype(F32) for a in arrays])
    n = flat.shape[0]
    rows = -(-n // 128)
    rows = -(-rows // 8) * 8
    return jnp.pad(flat, (0, rows * 128 - n)).reshape(rows, 128)


def _unpack(buf, shapes):
    flat = buf.reshape(-1)
    out, off = [], 0
    for shp in shapes:
        n = 1
        for q in shp:
            n *= q
        out.append(flat[off:off + n].reshape(shp))
        off += n
    return out


SMALL_GRAD_ORDER = ["dmod", "g_pre_mix", "g_post_mix", "g_pre_mlp", "g_post_mlp", "b_fgate", "w_gla_a2", "b_gla_a2",
                    "g_fox_out", "g_gla_out"]


def kernel(x, c, w_ada, b_ada, g_pre_mix, g_post_mix, w_in, b_fgate, w_gla_a2, b_gla_a2, g_fox_out, g_gla_out, w_out, g_pre_mlp, g_post_mlp, w_mlp_in, w_mlp_out, loss_target, m_w_ada, m_b_ada, m_g_pre_mix, m_g_post_mix, m_w_in, m_b_fgate, m_w_gla_a2, m_b_gla_a2, m_g_fox_out, m_g_gla_out, m_w_out, m_g_pre_mlp, m_g_post_mlp, m_w_mlp_in, m_w_mlp_out, v_w_ada, v_b_ada, v_g_pre_mix, v_g_post_mix, v_w_in, v_b_fgate, v_w_gla_a2, v_b_gla_a2, v_g_fox_out, v_g_gla_out, v_w_out, v_g_pre_mlp, v_g_post_mlp, v_w_mlp_in, v_w_mlp_out):
    ix, iy, ic = lax.axis_index("x"), lax.axis_index("y"), lax.axis_index("c")
    chip = 2 * ix + iy
    dev = 4 * ix + 2 * iy + ic
    d = D_MODEL

    c_act = _silu_rows(c, name="silu_c")
    pack1 = _pack([c_act, w_gla_a2[0], g_gla_out[0]])
    rows1 = pack1.shape[0]
    got1 = _gather8(pack1, name="gather_small_fwd").reshape(8, rows1, 128)
    per_dev = [_unpack(got1[q], [(d,), (GLA_RANK, GLA_KW // 4), (GLA_HEADS, GLA_DV // 4)]) for q in range(8)]
    c_all = jnp.stack([p[0] for p in per_dev])
    w_gla_a2_full = jnp.concatenate([per_dev[2 * j][1] for j in range(4)], axis=1)
    g_gla_full = jnp.concatenate([per_dev[2 * j][2] for j in range(4)], axis=1)
    cols = w_ada.shape[2]
    b_ada_shard = lax.dynamic_slice_in_dim(b_ada, chip * cols, cols, axis=1)
    mod_sh = _mod_shard(c_all, w_ada[0], b_ada_shard, name="ada_mod")
    got2 = _gather8(mod_sh, name="gather_mod").reshape(8, 8, cols)
    mod_all = jnp.concatenate([got2[2 * j] for j in range(4)], axis=1)
    mod = lax.dynamic_slice_in_dim(mod_all, dev, 1, axis=0)

    tr_in = lambda a: jnp.transpose(a[0])
    own_bf = [tr_in(w_in).astype(BF16), w_out[0].astype(BF16), w_mlp_in[0].astype(BF16), w_mlp_out[0].astype(BF16)]
    gw_in, = _gather_weights(own_bf[:1], name="gather_w_in_ici")
    idx = jnp.stack([ic, chip]).astype(jnp.int32)
    loss_part, grad_x, g_big, small = _local_step(
        x[0], loss_target[0], mod, g_pre_mix, g_post_mix, g_pre_mlp, g_post_mlp, gw_in, b_fgate,
        w_gla_a2_full, b_gla_a2, g_fox_out[0], g_gla_full, own_bf[0], own_bf[1], own_bf[2], own_bf[3], idx)
    loss = lax.psum(loss_part[0, 0], ("x", "y", "c"))

    big_w = [(tr_in(w_in), tr_in(m_w_in), tr_in(v_w_in)), (w_out[0], m_w_out[0], v_w_out[0]),
             (w_mlp_in[0], m_w_mlp_in[0], v_w_mlp_in[0]), (w_mlp_out[0], m_w_mlp_out[0], v_w_mlp_out[0])]
    big_res = []
    for q, (g, (w, m, v)) in enumerate(zip(g_big, big_w)):
        res4 = (g,) + tuple(_adam(g, w, m, v, name=f"adam_big_{q}"))
        big_res.append(tuple((jnp.transpose(a) if q == 0 else a)[None] for a in res4))

    pack2 = _pack([small[k] for k in SMALL_GRAD_ORDER])
    rows2 = pack2.shape[0]
    got3 = _gather8(pack2, name="gather_small_grads").reshape(8, rows2, 128)
    dmod_all = got3[:, :6 * d // 128, :].reshape(8, 6 * d)
    sums = _stack_sum(got3, name="small_grad_sum")
    shapes = [(1, 6 * d), (1, d), (1, d), (1, d), (1, d), (1, FOX_HEADS), (1, GLA_RANK, GLA_KW), (1, GLA_KW),
              (1, FOX_HEADS, FOX_HD), (1, GLA_HEADS, GLA_DV)]
    sg = dict(zip(["b_ada"] + SMALL_GRAD_ORDER[1:], _unpack(sums, shapes)))
    sg["w_gla_a2"] = lax.dynamic_slice_in_dim(sg["w_gla_a2"], chip * (GLA_KW // 4), GLA_KW // 4, axis=2)
    sg["g_gla_out"] = lax.dynamic_slice_in_dim(sg["g_gla_out"], chip * (GLA_DV // 4), GLA_DV // 4, axis=2)
    small_names = ["b_ada", "g_pre_mix", "g_post_mix", "b_fgate", "w_gla_a2", "b_gla_a2", "g_fox_out", "g_gla_out",
                   "g_pre_mlp", "g_post_mlp"]
    small_w = dict(b_ada=(b_ada, m_b_ada, v_b_ada), g_pre_mix=(g_pre_mix, m_g_pre_mix, v_g_pre_mix),
                   g_post_mix=(g_post_mix, m_g_post_mix, v_g_post_mix), b_fgate=(b_fgate, m_b_fgate, v_b_fgate),
                   w_gla_a2=(w_gla_a2, m_w_gla_a2, v_w_gla_a2), b_gla_a2=(b_gla_a2, m_b_gla_a2, v_b_gla_a2),
                   g_fox_out=(g_fox_out, m_g_fox_out, v_g_fox_out), g_gla_out=(g_gla_out, m_g_gla_out, v_g_gla_out),
                   g_pre_mlp=(g_pre_mlp, m_g_pre_mlp, v_g_pre_mlp), g_post_mlp=(g_post_mlp, m_g_post_mlp, v_g_post_mlp))
    sshapes = [small_w[k][0].shape for k in small_names]
    pg = _pack([sg[k] for k in small_names])
    pw, pm, pv = [_pack([small_w[k][q] for k in small_names]) for q in range(3)]
    pd, pmn, pvn = _adam(pg, pw, pm, pv, name="adam_small")
    s_delta = dict(zip(small_names, _unpack(pd, sshapes)))
    s_m = dict(zip(small_names, _unpack(pmn, sshapes)))
    s_v = dict(zip(small_names, _unpack(pvn, sshapes)))

    dmod_cols = lax.dynamic_slice_in_dim(dmod_all, chip * cols, cols, axis=1)
    g_ada, d_ada, m_ada, v_ada = _ada_grad_adam(c_all.T, dmod_cols, w_ada[0], m_w_ada[0], v_w_ada[0], name="ada_grad_adam")

    order = ["w_ada", "b_ada", "g_pre_mix", "g_post_mix", "w_in", "b_fgate", "w_gla_a2", "b_gla_a2", "g_fox_out",
             "g_gla_out", "w_out", "g_pre_mlp", "g_post_mlp", "w_mlp_in", "w_mlp_out"]
    res = {"w_ada": (g_ada[None], d_ada[None], m_ada[None], v_ada[None]),
           "w_in": big_res[0], "w_out": big_res[1], "w_mlp_in": big_res[2], "w_mlp_out": big_res[3]}
    for k in small_names:
        res[k] = (sg[k], s_delta[k], s_m[k], s_v[k])
    return (loss, grad_x[None], *[res[k][0] for k in order], *[res[k][1] for k in order],
            *[res[k][2] for k in order], *[res[k][3] for k in order])
```

```python
import functools

import jax
import jax.numpy as jnp
from jax import lax
from jax.experimental import pallas as pl
from jax.experimental.pallas import tpu as pltpu

F32 = jnp.float32
BF16 = jnp.bfloat16
MESH = pl.DeviceIdType.MESH
HIGHEST = lax.Precision.HIGHEST

D_MODEL = 2048
FOX_HEADS = 8
FOX_HD = 128
FOX_W = FOX_HEADS * FOX_HD
GLA_HEADS = 4
GLA_DK = 128
GLA_DV = 256
GLA_KW = GLA_HEADS * GLA_DK
GLA_W = GLA_HEADS * GLA_DV
GLA_RANK = 16
GLA_TEMP = 16.0
CHUNK = 64
D_FF = 4 * D_MODEL
EPS = 1e-6
MAIN_W = 3 * FOX_W + 2 * GLA_KW + 2 * GLA_W
SMALL_W = 128
NEG = -1e30

ADAM_LR = 0.001
ADAM_B1 = 0.9
ADAM_B2 = 0.999
ADAM_EPS = 1e-08
ADAM_WD = 0.01
ADAM_STEP = 10

VMEM_LIMIT = 52 * 1024 * 1024
ROW_TILE = 256
WIDE_ROW_TILE = 512
FOX_TQ = 512
FOX_TK = 512
GLA_ROWS = 512
GATE_TS = 512
MM_T = 1024
MM_TK = 2048
MM_TM = 2048


def _cp(*sem):
    return pltpu.CompilerParams(dimension_semantics=sem, vmem_limit_bytes=VMEM_LIMIT)


def _dot_nn(a, b, precision=None):
    return jnp.dot(a, b, preferred_element_type=F32, precision=precision)


def _dot_nt(a, b, precision=None):
    return lax.dot_general(a, b, (((1,), (1,)), ((), ())), preferred_element_type=F32, precision=precision)


def _dot_tn(a, b, precision=None):
    return lax.dot_general(a, b, (((0,), (0,)), ((), ())), preferred_element_type=F32, precision=precision)


def _sigmoid(x):
    return 1.0 / (1.0 + jnp.exp(-x))


def _log_sigmoid(x):
    return jnp.minimum(x, 0.0) - jnp.log(1.0 + jnp.exp(-jnp.abs(x)))


class _Side:
    def __init__(self, inputs, out_shapes, plan_fn, n_copies, aliases=None):
        self.inputs, self.out_shapes, self.plan_fn, self.n_copies = list(inputs), list(out_shapes), plan_fn, n_copies
        self.aliases = dict(aliases or {})

    def scratch(self):
        return [pltpu.SemaphoreType.DMA((self.n_copies,)), pltpu.SemaphoreType.DMA((self.n_copies,))]

    def run(self, in_refs, out_refs, sems, first, last):
        @pl.when(first)
        def _():
            _plan_start(self.plan_fn(in_refs, out_refs), *sems)

        @pl.when(last)
        def _():
            _plan_wait(self.plan_fn(in_refs, out_refs), *sems)


def _mm(a, b, *, mode, out_dtypes, name, tm=None, tn=None, tk=None, extras=(), epi=None,
        out_shapes=None, out_specs=None, side=None, b_slots=0):
    tm, tn, tk = tm or MM_T, tn or MM_T, tk or MM_TK
    b2 = (b.shape[1], b_slots * b.shape[2]) if b_slots else b.shape
    if mode == "nn":
        (m, k), n = a.shape, b2[1]
    elif mode == "nt":
        (m, k), n = a.shape, b2[0]
    else:
        (k, m), n = a.shape, b2[1]
    tm, tn, tk = min(tm, m), min(tn, n), min(tk, k)
    if b_slots:
        tn = min(tn, b.shape[2]) if mode == "nn" else tn
        tk = min(tk, b.shape[2]) if mode == "nt" else tk
    assert m % tm == 0 and n % tn == 0 and k % tk == 0, (name, m, n, k)
    nk = k // tk
    n_out, n_ex = len(out_dtypes), len(extras)
    if epi is None:
        epi = lambda acc: tuple(acc for _ in range(n_out))
    dot = {"nn": _dot_nn, "nt": _dot_nt, "tn": _dot_tn}[mode]

    n_si = len(side.inputs) if side else 0
    n_so = len(side.out_shapes) if side else 0
    grid = (m // tm, n // tn, nk)

    def body(*refs):
        a_ref, b_ref = refs[0], refs[1]
        ex_refs = refs[2:2 + n_ex]
        base = 2 + n_ex + n_si
        o_refs = refs[base:base + n_out]
        scratch = refs[base + n_out + n_so:]
        if side:
            pos = [pl.program_id(q) for q in range(3)]
            first = (pos[0] == 0) & (pos[1] == 0) & (pos[2] == 0)
            last = (pos[0] == grid[0] - 1) & (pos[1] == grid[1] - 1) & (pos[2] == grid[2] - 1)
            side.run(refs[2 + n_ex:base], refs[base + n_out:base + n_out + n_so], scratch[-2:], first, last)
        part = dot(a_ref[...], b_ref[...])

        def finish(acc):
            outs = epi(acc, *[e[...] for e in ex_refs])
            for o_ref, val in zip(o_refs, outs):
                o_ref[...] = val.reshape(o_ref.shape).astype(o_ref.dtype)

        if nk == 1:
            finish(part)
        else:
            acc_ref = scratch[0]
            kk = pl.program_id(2)

            @pl.when(kk == 0)
            def _():
                acc_ref[...] = part

            @pl.when(kk > 0)
            def _():
                acc_ref[...] += part

            @pl.when(kk == nk - 1)
            def _():
                finish(acc_ref[...])

    if mode == "nn":
        a_spec = pl.BlockSpec((tm, tk), lambda i, j, kk: (i, kk))
        b_spec = pl.BlockSpec((tk, tn), lambda i, j, kk: (kk, j))
        if b_slots:
            per = b.shape[2] // tn
            b_spec = pl.BlockSpec((None, tk, tn), lambda i, j, kk: (j // per, kk, j % per))
    elif mode == "nt":
        a_spec = pl.BlockSpec((tm, tk), lambda i, j, kk: (i, kk))
        b_spec = pl.BlockSpec((tn, tk), lambda i, j, kk: (j, kk))
        if b_slots:
            per = b.shape[2] // tk
            b_spec = pl.BlockSpec((None, tn, tk), lambda i, j, kk: (kk // per, j, kk % per))
    else:
        assert not b_slots
        a_spec = pl.BlockSpec((tk, tm), lambda i, j, kk: (kk, i))
        b_spec = pl.BlockSpec((tk, tn), lambda i, j, kk: (kk, j))
    tile_spec = pl.BlockSpec((tm, tn), lambda i, j, kk: (i, j))
    if out_shapes is None:
        out_shapes = [jax.ShapeDtypeStruct((m, n), dt) for dt in out_dtypes]
    if out_specs is None:
        out_specs = [tile_spec for _ in out_dtypes]
    any_spec = pl.BlockSpec(memory_space=pl.ANY)
    res = pl.pallas_call(
        body,
        grid=grid,
        in_specs=[a_spec, b_spec] + [tile_spec for _ in extras] + [any_spec] * n_si,
        out_specs=list(out_specs) + [any_spec] * n_so,
        out_shape=list(out_shapes) + (side.out_shapes if side else []),
        scratch_shapes=([pltpu.VMEM((tm, tn), F32)] if nk > 1 else []) + (side.scratch() if side else []),
        compiler_params=_cp("arbitrary", "arbitrary", "arbitrary") if side else _cp("parallel", "parallel", "arbitrary"),
        input_output_aliases={2 + n_ex + si: n_out + so for si, so in side.aliases.items()} if side else {},
        name=name,
    )(a, b, *extras, *(side.inputs if side else []))
    return res


def _row_spec(ts, d):
    return pl.BlockSpec((ts, d), lambda i: (i, 0))


def _vec_spec(d):
    return pl.BlockSpec((1, d), lambda i: (0, 0))


def _side_args(side, n_in, n_out):
    if side is None:
        return [], [], [], [], [], {}
    any_spec = pl.BlockSpec(memory_space=pl.ANY)
    return ([any_spec] * len(side.inputs), [any_spec] * len(side.out_shapes), side.out_shapes, side.scratch(),
            side.inputs, {n_in + si: n_out + so for si, so in side.aliases.items()})


def _pre_fwd(x, avec, shift, *, name, side=None):
    s, d = x.shape
    ts = min(WIDE_ROW_TILE, s)
    nb = s // ts
    s_in, s_out, s_shapes, s_scratch, s_ops, s_alias = _side_args(side, 3, 1)

    def body(x_ref, a_ref, s_ref, *rest):
        h_ref = rest[len(s_in)]
        if side:
            step = pl.program_id(0)
            side.run(rest[:len(s_in)], rest[len(s_in) + 1:len(s_in) + 1 + len(s_out)],
                     rest[len(s_in) + 1 + len(s_out):], step == 0, step == nb - 1)
        xv = x_ref[...]
        r = lax.rsqrt(jnp.mean(xv * xv, axis=-1, keepdims=True) + EPS)
        h_ref[...] = (xv * r * a_ref[...] + s_ref[...]).astype(BF16)

    res = pl.pallas_call(
        body, grid=(nb,),
        in_specs=[_row_spec(ts, d), _vec_spec(d), _vec_spec(d)] + s_in,
        out_specs=[_row_spec(ts, d)] + s_out,
        out_shape=[jax.ShapeDtypeStruct((s, d), BF16)] + s_shapes,
        scratch_shapes=s_scratch, input_output_aliases=s_alias,
        compiler_params=_cp("arbitrary" if side else "parallel"), name=name,
    )(x, avec, shift, *s_ops)
    return res if side else res[0]


def _post_pre_fwd(x, y, gate, g, avec, shift, *, name):
    s, d = x.shape
    ts = min(WIDE_ROW_TILE, s)

    def body(x_ref, y_ref, gate_ref, g_ref, a_ref, s_ref, o_ref, h_ref):
        yv = y_ref[...]
        r = lax.rsqrt(jnp.mean(yv * yv, axis=-1, keepdims=True) + EPS)
        x1 = x_ref[...] + gate_ref[...] * (yv * r * g_ref[...])
        o_ref[...] = x1
        r1 = lax.rsqrt(jnp.mean(x1 * x1, axis=-1, keepdims=True) + EPS)
        h_ref[...] = (x1 * r1 * a_ref[...] + s_ref[...]).astype(BF16)

    return pl.pallas_call(
        body, grid=(s // ts,),
        in_specs=[_row_spec(ts, d), _row_spec(ts, d)] + [_vec_spec(d)] * 4,
        out_specs=[_row_spec(ts, d), _row_spec(ts, d)],
        out_shape=[jax.ShapeDtypeStruct((s, d), F32), jax.ShapeDtypeStruct((s, d), BF16)],
        compiler_params=_cp("parallel"), name=name,
    )(x, y, gate, g, avec, shift)


def _post_bwd_math(dxv, yv, gatev, gv):
    r = lax.rsqrt(jnp.mean(yv * yv, axis=-1, keepdims=True) + EPS)
    yhat = yv * r
    dn = dxv * gatev
    dyhat = dn * gv
    dy = r * (dyhat - yhat * jnp.mean(dyhat * yhat, axis=-1, keepdims=True))
    return dy, dxv * (yhat * gv), dn * yhat


def _accumulate(first, pairs):
    @pl.when(first)
    def _():
        for ref, _ in pairs:
            ref[...] = jnp.zeros_like(ref)

    for ref, val in pairs:
        ref[...] += jnp.sum(val, axis=0, keepdims=True)


def _post_loss_bwd(x, y, gate, g, target, *, name):
    s, d = x.shape
    ts = min(ROW_TILE, s)

    def body(x_ref, y_ref, gate_ref, g_ref, t_ref, dx_ref, dy_ref, loss_ref, dgate_ref, dg_ref):
        yv, gatev, gv = y_ref[...], gate_ref[...], g_ref[...]
        r = lax.rsqrt(jnp.mean(yv * yv, axis=-1, keepdims=True) + EPS)
        diff = x_ref[...] + gatev * (yv * r * gv) - t_ref[...]
        dxv = diff * (1.0 / d)
        dx_ref[...] = dxv
        dy, dgate_rows, dg_rows = _post_bwd_math(dxv, yv, gatev, gv)
        dy_ref[...] = dy.astype(BF16)
        first = pl.program_id(0) == 0
        _accumulate(first, [(dgate_ref, dgate_rows), (dg_ref, dg_rows)])

        @pl.when(first)
        def _():
            loss_ref[...] = jnp.zeros_like(loss_ref)

        loss_ref[...] += jnp.sum(jnp.mean(diff * diff, axis=-1, keepdims=True)) * 0.5

    return pl.pallas_call(
        body, grid=(s // ts,),
        in_specs=[_row_spec(ts, d), _row_spec(ts, d), _vec_spec(d), _vec_spec(d), _row_spec(ts, d)],
        out_specs=[_row_spec(ts, d), _row_spec(ts, d), pl.BlockSpec((1, 128), lambda i: (0, 0)), _vec_spec(d),
                   _vec_spec(d)],
        out_shape=[jax.ShapeDtypeStruct((s, d), F32), jax.ShapeDtypeStruct((s, d), BF16),
                   jax.ShapeDtypeStruct((1, 128), F32), jax.ShapeDtypeStruct((1, d), F32),
                   jax.ShapeDtypeStruct((1, d), F32)],
        compiler_params=_cp("arbitrary"), name=name,
    )(x, y, gate, g, target)


def _pre_post_bwd(dh, xin, dres, avec, y, gate, g, *, name):
    s, d = xin.shape
    ts = min(ROW_TILE, s)

    def body(dh_ref, x_ref, dres_ref, a_ref, y_ref, gate_ref, g_ref, dx_ref, dshift_ref, da_ref, dy_ref,
             dgate_ref, dg_ref):
        xv, dhv = x_ref[...], dh_ref[...]
        r = lax.rsqrt(jnp.mean(xv * xv, axis=-1, keepdims=True) + EPS)
        xhat = xv * r
        dxhat = dhv * a_ref[...]
        dxv = dres_ref[...] + r * (dxhat - xhat * jnp.mean(dxhat * xhat, axis=-1, keepdims=True))
        dx_ref[...] = dxv
        dy, dgate_rows, dg_rows = _post_bwd_math(dxv, y_ref[...], gate_ref[...], g_ref[...])
        dy_ref[...] = dy.astype(BF16)
        _accumulate(pl.program_id(0) == 0, [(dshift_ref, dhv), (da_ref, dhv * xhat), (dgate_ref, dgate_rows),
                                            (dg_ref, dg_rows)])

    return pl.pallas_call(
        body, grid=(s // ts,),
        in_specs=[_row_spec(ts, d), _row_spec(ts, d), _row_spec(ts, d), _vec_spec(d), _row_spec(ts, d),
                  _vec_spec(d), _vec_spec(d)],
        out_specs=[_row_spec(ts, d), _vec_spec(d), _vec_spec(d), _row_spec(ts, d), _vec_spec(d), _vec_spec(d)],
        out_shape=[jax.ShapeDtypeStruct((s, d), F32), jax.ShapeDtypeStruct((1, d), F32),
                   jax.ShapeDtypeStruct((1, d), F32), jax.ShapeDtypeStruct((s, d), BF16),
                   jax.ShapeDtypeStruct((1, d), F32), jax.ShapeDtypeStruct((1, d), F32)],
        compiler_params=_cp("arbitrary"), name=name,
    )(dh, xin, dres, avec, y, gate, g)


def _pre_bwd(dh, xin, dres, avec, *, name, side=None):
    s, d = xin.shape
    ts = min(WIDE_ROW_TILE, s)
    nb = s // ts
    s_in, s_out, s_shapes, s_scratch, s_ops, s_alias = _side_args(side, 4, 3)

    def body(dh_ref, x_ref, dres_ref, a_ref, *rest):
        dx_ref, dshift_ref, da_ref = rest[len(s_in):len(s_in) + 3]
        if side:
            step = pl.program_id(0)
            side.run(rest[:len(s_in)], rest[len(s_in) + 3:len(s_in) + 3 + len(s_out)],
                     rest[len(s_in) + 3 + len(s_out):], step == 0, step == nb - 1)
        xv, dhv = x_ref[...], dh_ref[...]
        r = lax.rsqrt(jnp.mean(xv * xv, axis=-1, keepdims=True) + EPS)
        xhat = xv * r
        dxhat = dhv * a_ref[...]
        dx_ref[...] = dres_ref[...] + r * (dxhat - xhat * jnp.mean(dxhat * xhat, axis=-1, keepdims=True))

        @pl.when(pl.program_id(0) == 0)
        def _():
            dshift_ref[...] = jnp.zeros_like(dshift_ref)
            da_ref[...] = jnp.zeros_like(da_ref)

        dshift_ref[...] += jnp.sum(dhv, axis=0, keepdims=True)
        da_ref[...] += jnp.sum(dhv * xhat, axis=0, keepdims=True)

    return pl.pallas_call(
        body, grid=(nb,),
        in_specs=[_row_spec(ts, d), _row_spec(ts, d), _row_spec(ts, d), _vec_spec(d)] + s_in,
        out_specs=[_row_spec(ts, d), _vec_spec(d), _vec_spec(d)] + s_out,
        out_shape=[jax.ShapeDtypeStruct((s, d), F32), jax.ShapeDtypeStruct((1, d), F32),
                   jax.ShapeDtypeStruct((1, d), F32)] + s_shapes,
        scratch_shapes=s_scratch, input_output_aliases=s_alias,
        compiler_params=_cp("arbitrary"), name=name,
    )(dh, xin, dres, avec, *s_ops)


def _tri(n, strict=False, upper=False):
    r = lax.broadcasted_iota(jnp.int32, (n, n), 0)
    c = lax.broadcasted_iota(jnp.int32, (n, n), 1)
    if upper:
        r, c = c, r
    return ((r > c) if strict else (r >= c)).astype(F32)


def _gates_fwd(ps, bf, w2p, b2, *, name):
    s = ps.shape[0]
    ts = min(GATE_TS, s)

    def body(ps_ref, bf_ref, w_ref, b2_ref, cum_ref, la_ref, carry_ref):
        @pl.when(pl.program_id(0) == 0)
        def _():
            carry_ref[...] = jnp.zeros_like(carry_ref)

        psv = ps_ref[...]
        lf = _log_sigmoid(psv + bf_ref[...])
        cum = _dot_nn(_tri(ts), lf, HIGHEST) + carry_ref[...]
        cum_ref[...] = cum
        carry_ref[...] = cum[ts - 1:ts, :]
        z = _dot_nn(psv, w_ref[...], HIGHEST) + b2_ref[...]
        la_ref[...] = _log_sigmoid(z) * (1.0 / GLA_TEMP)

    return pl.pallas_call(
        body, grid=(s // ts,),
        in_specs=[_row_spec(ts, SMALL_W), _vec_spec(SMALL_W),
                  pl.BlockSpec((SMALL_W, GLA_KW), lambda i: (0, 0)), _vec_spec(GLA_KW)],
        out_specs=[_row_spec(ts, SMALL_W), _row_spec(ts, GLA_KW)],
        out_shape=[jax.ShapeDtypeStruct((s, SMALL_W), F32), jax.ShapeDtypeStruct((s, GLA_KW), F32)],
        scratch_shapes=[pltpu.VMEM((1, SMALL_W), F32)],
        compiler_params=_cp("arbitrary"), name=name,
    )(ps, bf, w2p, b2)


def _gates_bwd(dck, ps, bf, w2p, b2, dla, *, name):
    s = ps.shape[0]
    ts = min(GATE_TS, s)
    nb = s // ts
    rev = lambda i: (nb - 1 - i, 0)

    def body(dck_ref, ps_ref, bf_ref, w_ref, b2_ref, dla_ref, dps_ref, dbf_ref, dw_ref, db2_ref, carry_ref):
        @pl.when(pl.program_id(0) == 0)
        def _():
            carry_ref[...] = jnp.zeros_like(carry_ref)
            dbf_ref[...] = jnp.zeros_like(dbf_ref)
            dw_ref[...] = jnp.zeros_like(dw_ref)
            db2_ref[...] = jnp.zeros_like(db2_ref)

        psv, dckv = ps_ref[...], dck_ref[...]
        dlf = _dot_nn(_tri(ts, upper=True), dckv, HIGHEST) + carry_ref[...]
        carry_ref[...] += jnp.sum(dckv, axis=0, keepdims=True)
        lane = lax.broadcasted_iota(jnp.int32, (ts, SMALL_W), 1)
        dff = jnp.where(lane < FOX_HEADS, dlf * _sigmoid(-(psv + bf_ref[...])), 0.0)
        z = _dot_nn(psv, w_ref[...], HIGHEST) + b2_ref[...]
        dz = dla_ref[...] * _sigmoid(-z) * (1.0 / GLA_TEMP)
        dps_ref[...] = (_dot_nt(dz, w_ref[...], HIGHEST) + dff).astype(BF16)
        dbf_ref[...] += jnp.sum(dff, axis=0, keepdims=True)
        dw_ref[...] += _dot_tn(psv, dz, HIGHEST)
        db2_ref[...] += jnp.sum(dz, axis=0, keepdims=True)

    return pl.pallas_call(
        body, grid=(nb,),
        in_specs=[pl.BlockSpec((ts, SMALL_W), rev), pl.BlockSpec((ts, SMALL_W), rev), _vec_spec(SMALL_W),
                  pl.BlockSpec((SMALL_W, GLA_KW), lambda i: (0, 0)), _vec_spec(GLA_KW),
                  pl.BlockSpec((ts, GLA_KW), rev)],
        out_specs=[pl.BlockSpec((ts, SMALL_W), rev), _vec_spec(SMALL_W),
                   pl.BlockSpec((SMALL_W, GLA_KW), lambda i: (0, 0)), _vec_spec(GLA_KW)],
        out_shape=[jax.ShapeDtypeStruct((s, SMALL_W), BF16), jax.ShapeDtypeStruct((1, SMALL_W), F32),
                   jax.ShapeDtypeStruct((SMALL_W, GLA_KW), F32), jax.ShapeDtypeStruct((1, GLA_KW), F32)],
        scratch_shapes=[pltpu.VMEM((1, SMALL_W), F32)],
        compiler_params=_cp("arbitrary"), name=name,
    )(dck, ps, bf, w2p, b2, dla)


def _hs(h, hd=FOX_HD):
    return slice(h * hd, (h + 1) * hd)


def _fox_fwd(proj, cum_t, g_fox, *, name, side=None):
    s = proj.shape[0]
    tq, tk = min(FOX_TQ, s), min(FOX_TK, s)
    scale = FOX_HD ** -0.5
    n_si = len(side.inputs) if side else 0
    n_so = len(side.out_shapes) if side else 0
    grid = (s // tq, s // tk)

    def body(*refs):
        q_ref, k_ref, v_ref, ck_ref, g_ref = refs[:5]
        o_ref, n_ref, lse_ref = refs[5 + n_si:8 + n_si]
        m_sc, acc_sc = refs[8 + n_si + n_so:10 + n_si + n_so]
        i, j = pl.program_id(0), pl.program_id(1)
        if side:
            side.run(refs[5:5 + n_si], refs[8 + n_si:8 + n_si + n_so], refs[10 + n_si + n_so:],
                     (i == 0) & (j == 0), (i == grid[0] - 1) & (j == grid[1] - 1))

        @pl.when(j == 0)
        def _():
            m_sc[...] = jnp.full_like(m_sc, NEG)
            acc_sc[...] = jnp.zeros_like(acc_sc)

        def block(masked):
            mask = _causal_mask(i, j, tq, tk) if masked else None
            ones = jnp.ones((tk, FOX_HD), BF16)
            for h in range(FOX_HEADS):
                sc = _fox_logits(_dot_nt(q_ref[:, _hs(h)], k_ref[:, _hs(h)]), ck_ref[h:h + 1, :], mask, scale)
                m_prev = m_sc[h]
                m_new = jnp.maximum(m_prev, jnp.max(sc, axis=-1, keepdims=True))
                alpha = jnp.exp(m_prev - m_new)
                p = jnp.exp(sc - m_new).astype(BF16)
                v_one = jnp.concatenate([v_ref[:, _hs(h)], ones], axis=1)
                acc_sc[:, _hs(h, 2 * FOX_HD)] = alpha * acc_sc[:, _hs(h, 2 * FOX_HD)] + _dot_nn(p, v_one)
                m_sc[h] = m_new

        pl.when(j < i)(functools.partial(block, False))

        @pl.when(j == i)
        def _():
            block(True)
            lane = lax.broadcasted_iota(jnp.int32, (tq, 128), 1)
            lse = jnp.zeros((tq, 128), F32)
            for h in range(FOX_HEADS):
                l_rep = acc_sc[:, 2 * h * FOX_HD + FOX_HD:2 * (h + 1) * FOX_HD]
                o = acc_sc[:, 2 * h * FOX_HD:2 * h * FOX_HD + FOX_HD] / l_rep
                o_ref[:, _hs(h)] = o
                r = lax.rsqrt(jnp.mean(o * o, axis=-1, keepdims=True) + EPS)
                n_ref[:, _hs(h)] = (o * r * g_ref[h:h + 1, :]).astype(BF16)
                lse = jnp.where(lane == h, m_sc[h] + jnp.log(l_rep), lse)
            lse_ref[...] = lse

    kv = lambda col: (lambda i, j: (jnp.minimum(j, i), col))
    any_spec = pl.BlockSpec(memory_space=pl.ANY)
    return pl.pallas_call(
        body, grid=grid,
        in_specs=[pl.BlockSpec((tq, FOX_W), lambda i, j: (i, 0)),
                  pl.BlockSpec((tk, FOX_W), kv(1)),
                  pl.BlockSpec((tk, FOX_W), kv(2)),
                  pl.BlockSpec((FOX_HEADS, tk), lambda i, j: (0, jnp.minimum(j, i))),
                  pl.BlockSpec((FOX_HEADS, FOX_HD), lambda i, j: (0, 0))] + [any_spec] * n_si,
        out_specs=[pl.BlockSpec((tq, FOX_W), lambda i, j: (i, 0)),
                   pl.BlockSpec((tq, FOX_W), lambda i, j: (i, 0)),
                   pl.BlockSpec((tq, 128), lambda i, j: (i, 0))] + [any_spec] * n_so,
        out_shape=[jax.ShapeDtypeStruct((s, FOX_W), F32), jax.ShapeDtypeStruct((s, FOX_W), BF16),
                   jax.ShapeDtypeStruct((s, 128), F32)] + (side.out_shapes if side else []),
        scratch_shapes=[pltpu.VMEM((FOX_HEADS, tq, 1), F32), pltpu.VMEM((tq, 2 * FOX_W), F32)]
        + (side.scratch() if side else []),
        compiler_params=_cp("arbitrary", "arbitrary"), name=name,
    )(proj, proj, proj, cum_t, g_fox, *(side.inputs if side else []))


def _causal_mask(i, j, tq, tk):
    rows = i * tq + lax.broadcasted_iota(jnp.int32, (tq, tk), 0)
    cols = j * tk + lax.broadcasted_iota(jnp.int32, (tq, tk), 1)
    return rows >= cols


def _fox_logits(qk, ck, mask, scale):
    sc = qk * scale - ck
    return sc if mask is None else jnp.where(mask, sc, NEG)


def _fox_bwd(proj, do, cum_t, lse, delta, *, name, side=None):
    s = proj.shape[0]
    tq, tk = min(FOX_TQ, s), min(FOX_TK, s)
    nk, nq = s // tk, s // tq
    scale = FOX_HD ** -0.5
    n_si = len(side.inputs) if side else 0
    n_so = len(side.out_shapes) if side else 0

    def body(*refs):
        q_ref, k_ref, v_ref, do_ref, ck_ref, lse_ref, dl_ref = refs[:7]
        dq_hbm, dk_ref, dv_ref, dcq_hbm, dck_ref = refs[7 + n_si:12 + n_si]
        dq_sc, dcq_sc, dk_sc, dv_sc, dck_sc, out_sems = refs[12 + n_si + n_so:18 + n_si + n_so]
        j, i = pl.program_id(0), pl.program_id(1)
        if side:
            side.run(refs[7:7 + n_si], refs[12 + n_si:12 + n_si + n_so], refs[18 + n_si + n_so:],
                     (j == 0) & (i == 0), (j == nk - 1) & (i == nq - 1))

        @pl.when((j == 0) & (i == 0))
        def _():
            dq_sc[...] = jnp.zeros_like(dq_sc)
            dcq_sc[...] = jnp.zeros_like(dcq_sc)

        @pl.when(i == 0)
        def _():
            dk_sc[...] = jnp.zeros_like(dk_sc)
            dv_sc[...] = jnp.zeros_like(dv_sc)
            dck_sc[...] = jnp.zeros_like(dck_sc)

        def block(masked):
            mask = _causal_mask(i, j, tq, tk) if masked else None
            qrows = pl.ds(pl.multiple_of(i * tq, tq), tq)
            for h in range(FOX_HEADS):
                sc = _fox_logits(_dot_nt(q_ref[:, _hs(h)], k_ref[:, _hs(h)]), ck_ref[h:h + 1, :], mask, scale)
                p = jnp.exp(sc - lse_ref[:, h:h + 1])
                ds = p * (_dot_nt(do_ref[:, _hs(h)], v_ref[:, _hs(h)]) - dl_ref[:, h:h + 1])
                dsb = ds.astype(BF16)
                dv_sc[:, _hs(h)] += _dot_tn(p.astype(BF16), do_ref[:, _hs(h)])
                dk_sc[:, _hs(h)] += _dot_tn(dsb, q_ref[:, _hs(h)])
                dq_sc[qrows, _hs(h)] += _dot_nn(dsb, k_ref[:, _hs(h)]) * scale
                dck_sc[h:h + 1, :] -= jnp.sum(ds, axis=0, keepdims=True)
                dcq_sc[qrows, h:h + 1] += jnp.sum(ds, axis=-1, keepdims=True)

        pl.when(i > j)(functools.partial(block, False))
        pl.when(i == j)(functools.partial(block, True))

        @pl.when(i == nq - 1)
        def _():
            dk_ref[...] = (dk_sc[...] * scale).astype(BF16)
            dv_ref[...] = dv_sc[...].astype(BF16)
            dck_ref[...] = dck_sc[...]

        @pl.when((j == nk - 1) & (i == nq - 1))
        def _():
            out_q = pltpu.make_async_copy(dq_sc, dq_hbm, out_sems.at[0])
            out_c = pltpu.make_async_copy(dcq_sc, dcq_hbm, out_sems.at[1])
            out_q.start()
            out_c.start()
            out_q.wait()
            out_c.wait()

    qrow = lambda j, i: (jnp.maximum(i, j), 0)
    krow = lambda col: (lambda j, i: (j, col))
    any_spec = pl.BlockSpec(memory_space=pl.ANY)
    return pl.pallas_call(
        body, grid=(nk, nq),
        in_specs=[pl.BlockSpec((tq, FOX_W), qrow), pl.BlockSpec((tk, FOX_W), krow(1)),
                  pl.BlockSpec((tk, FOX_W), krow(2)),
                  pl.BlockSpec((tq, FOX_W), qrow),
                  pl.BlockSpec((FOX_HEADS, tk), lambda j, i: (0, j)),
                  pl.BlockSpec((tq, 128), qrow), pl.BlockSpec((tq, 128), qrow)] + [any_spec] * n_si,
        out_specs=[any_spec, pl.BlockSpec((tk, FOX_W), lambda j, i: (j, 0)),
                   pl.BlockSpec((tk, FOX_W), lambda j, i: (j, 0)), any_spec,
                   pl.BlockSpec((FOX_HEADS, tk), lambda j, i: (0, j))] + [any_spec] * n_so,
        out_shape=[jax.ShapeDtypeStruct((s, FOX_W), F32), jax.ShapeDtypeStruct((s, FOX_W), BF16),
                   jax.ShapeDtypeStruct((s, FOX_W), BF16), jax.ShapeDtypeStruct((s, 128), F32),
                   jax.ShapeDtypeStruct((FOX_HEADS, s), F32)] + (side.out_shapes if side else []),
        scratch_shapes=[pltpu.VMEM((s, FOX_W), F32), pltpu.VMEM((s, 128), F32),
                        pltpu.VMEM((tk, FOX_W), F32), pltpu.VMEM((tk, FOX_W), F32), pltpu.VMEM((FOX_HEADS, tk), F32),
                        pltpu.SemaphoreType.DMA((2,))] + (side.scratch() if side else []),
        compiler_params=_cp("arbitrary", "arbitrary"), name=name,
    )(proj, proj, proj, do, cum_t, lse, delta, *(side.inputs if side else []))


def _head_norm_bwd(dn_in, o, g, gr_src, *, nh, hd, dn_col, gr_col, name):
    s, w = o.shape
    ts = min(ROW_TILE, s)
    gated = gr_src is not None

    def body(*refs):
        if gated:
            dn_ref, o_ref, g_ref, gr_ref, do_ref, dgr_ref, dl_ref, dg_ref = refs
        else:
            dn_ref, o_ref, g_ref, do_ref, dl_ref, dg_ref = refs

        @pl.when(pl.program_id(0) == 0)
        def _():
            dg_ref[...] = jnp.zeros_like(dg_ref)

        lane = lax.broadcasted_iota(jnp.int32, (ts, 128), 1)
        delta = jnp.zeros((ts, 128), F32)
        for h in range(nh):
            sl = _hs(h, hd)
            ov = o_ref[:, sl]
            dnv = dn_ref[:, sl].astype(F32)
            gv = g_ref[h:h + 1, :]
            r = lax.rsqrt(jnp.mean(ov * ov, axis=-1, keepdims=True) + EPS)
            ohat = ov * r
            if gated:
                grv = gr_ref[:, sl].astype(F32)
                sig = _sigmoid(grv)
                dgr_ref[:, sl] = (dnv * (ohat * gv) * (sig * (1.0 + grv * (1.0 - sig)))).astype(BF16)
                dnv = dnv * (grv * sig)
            dg_ref[h:h + 1, :] += jnp.sum(dnv * ohat, axis=0, keepdims=True)
            dohat = dnv * gv
            do = r * (dohat - ohat * jnp.mean(dohat * ohat, axis=-1, keepdims=True))
            do_ref[:, sl] = do.astype(BF16)
            delta = jnp.where(lane == h, jnp.sum(do.astype(BF16).astype(F32) * ov, axis=-1, keepdims=True), delta)
        dl_ref[...] = delta

    in_specs = [pl.BlockSpec((ts, w), lambda i: (i, dn_col)), _row_spec(ts, w),
                pl.BlockSpec((nh, hd), lambda i: (0, 0))]
    args = [dn_in, o, g]
    out_specs = [_row_spec(ts, w)]
    out_shape = [jax.ShapeDtypeStruct((s, w), BF16)]
    if gated:
        in_specs.append(pl.BlockSpec((ts, w), lambda i: (i, gr_col)))
        args.append(gr_src)
        out_specs.append(_row_spec(ts, w))
        out_shape.append(jax.ShapeDtypeStruct((s, w), BF16))
    out_specs += [_row_spec(ts, 128), pl.BlockSpec((nh, hd), lambda i: (0, 0))]
    out_shape += [jax.ShapeDtypeStruct((s, 128), F32), jax.ShapeDtypeStruct((nh, hd), F32)]
    return pl.pallas_call(
        body, grid=(s // ts,), in_specs=in_specs, out_specs=out_specs, out_shape=out_shape,
        compiler_params=_cp("arbitrary"), name=name,
    )(*args)


GQ_BLK = 3 * FOX_W // GLA_DK
GK_BLK = GQ_BLK + GLA_HEADS
GV_BLK = (3 * FOX_W + 2 * GLA_KW) // GLA_DV
GR_BLK = GV_BLK + GLA_HEADS


def _gla_chunk_terms(la):
    cum = _dot_nn(_tri(CHUNK), la, HIGHEST)
    total = cum[CHUNK - 1:CHUNK, :]
    return jnp.exp(total - cum), jnp.exp(total)


def _gla_fwd(proj, log_a, g_gla, *, name):
    s = proj.shape[0]
    rows = min(GLA_ROWS, s)
    cb = rows // CHUNK
    nblk = s // rows
    scale = GLA_DK ** -0.5

    def body(q_ref, k_ref, v_ref, gr_ref, la_ref, g_ref, o_ref, n_ref, st_ref, st_sc):
        h = pl.program_id(0)

        @pl.when(pl.program_id(1) == 0)
        def _():
            st_sc[...] = jnp.zeros_like(st_sc)

        gv = g_ref[pl.ds(h, 1), :]
        for ci in range(cb):
            sl = slice(ci * CHUNK, (ci + 1) * CHUNK)
            e, dec = _gla_chunk_terms(la_ref[sl, :])
            k_dec = (k_ref[sl, :].astype(F32) * e).astype(BF16)
            st = st_sc[...] * dec + _dot_tn(v_ref[sl, :], k_dec)
            st_sc[...] = st
            st_ref[0, ci] = st
            qs = (q_ref[sl, :].astype(F32) * scale).astype(BF16)
            o = _dot_nt(qs, st.astype(BF16))
            o_ref[sl, :] = o
            r = lax.rsqrt(jnp.mean(o * o, axis=-1, keepdims=True) + EPS)
            grv = gr_ref[sl, :].astype(F32)
            n_ref[sl, :] = (o * r * gv * (grv * _sigmoid(grv))).astype(BF16)

    return pl.pallas_call(
        body, grid=(GLA_HEADS, nblk),
        in_specs=[pl.BlockSpec((rows, GLA_DK), lambda h, n: (n, GQ_BLK + h)),
                  pl.BlockSpec((rows, GLA_DK), lambda h, n: (n, GK_BLK + h)),
                  pl.BlockSpec((rows, GLA_DV), lambda h, n: (n, GV_BLK + h)),
                  pl.BlockSpec((rows, GLA_DV), lambda h, n: (n, GR_BLK + h)),
                  pl.BlockSpec((rows, GLA_DK), lambda h, n: (n, h)),
                  pl.BlockSpec((GLA_HEADS, GLA_DV), lambda h, n: (0, 0))],
        out_specs=[pl.BlockSpec((rows, GLA_DV), lambda h, n: (n, h)),
                   pl.BlockSpec((rows, GLA_DV), lambda h, n: (n, h)),
                   pl.BlockSpec((1, cb, GLA_DV, GLA_DK), lambda h, n: (h, n, 0, 0))],
        out_shape=[jax.ShapeDtypeStruct((s, GLA_W), F32), jax.ShapeDtypeStruct((s, GLA_W), BF16),
                   jax.ShapeDtypeStruct((GLA_HEADS, s // CHUNK, GLA_DV, GLA_DK), F32)],
        scratch_shapes=[pltpu.VMEM((GLA_DV, GLA_DK), F32)],
        compiler_params=_cp("parallel", "arbitrary"), name=name,
    )(proj, proj, proj, proj, log_a, g_gla)


def _gla_bwd(proj, log_a, do, states, *, name):
    s = proj.shape[0]
    rows = min(GLA_ROWS, s)
    cb = rows // CHUNK
    nblk = s // rows
    scale = GLA_DK ** -0.5

    def body(q_ref, k_ref, v_ref, la_ref, do_ref, st_ref, prev_ref, dq_ref, dk_ref, dv_ref, dla_ref, g_sc):
        nrev = pl.program_id(1)
        blk = nblk - 1 - nrev

        @pl.when(nrev == 0)
        def _():
            g_sc[...] = jnp.zeros_like(g_sc)

        for ci in reversed(range(cb)):
            sl = slice(ci * CHUNK, (ci + 1) * CHUNK)
            e, dec = _gla_chunk_terms(la_ref[sl, :])
            kd = k_ref[sl, :].astype(F32) * e
            qs = (q_ref[sl, :].astype(F32) * scale).astype(BF16)
            dov = do_ref[sl, :]
            st = st_ref[0, ci]
            if ci > 0:
                st_prev = st_ref[0, ci - 1]
            else:
                st_prev = prev_ref[0, 0] * (blk > 0).astype(F32)
            dq_ref[sl, :] = (_dot_nn(dov, st.astype(BF16)) * scale).astype(BF16)
            gt = g_sc[...] + _dot_tn(dov, qs)
            gtb = gt.astype(BF16)
            dkd = _dot_nn(v_ref[sl, :], gtb)
            dv_ref[sl, :] = _dot_nt(kd.astype(BF16), gtb).astype(BF16)
            dk_ref[sl, :] = (dkd * e).astype(BF16)
            ddec = jnp.sum(gt * st_prev, axis=0, keepdims=True) * dec
            dla_ref[sl, :] = _dot_nn(_tri(CHUNK, strict=True), dkd * kd, HIGHEST) + ddec
            g_sc[...] = gt * dec

    rev = lambda col0: (lambda h, n: (nblk - 1 - n, col0 + h))
    return pl.pallas_call(
        body, grid=(GLA_HEADS, nblk),
        in_specs=[pl.BlockSpec((rows, GLA_DK), rev(GQ_BLK)),
                  pl.BlockSpec((rows, GLA_DK), rev(GK_BLK)),
                  pl.BlockSpec((rows, GLA_DV), rev(GV_BLK)),
                  pl.BlockSpec((rows, GLA_DK), rev(0)),
                  pl.BlockSpec((rows, GLA_DV), rev(0)),
                  pl.BlockSpec((1, cb, GLA_DV, GLA_DK), lambda h, n: (h, nblk - 1 - n, 0, 0)),
                  pl.BlockSpec((1, 1, GLA_DV, GLA_DK),
                               lambda h, n: (h, jnp.maximum((nblk - 1 - n) * cb - 1, 0), 0, 0))],
        out_specs=[pl.BlockSpec((rows, GLA_DK), rev(0)), pl.BlockSpec((rows, GLA_DK), rev(0)),
                   pl.BlockSpec((rows, GLA_DV), rev(0)), pl.BlockSpec((rows, GLA_DK), rev(0))],
        out_shape=[jax.ShapeDtypeStruct((s, GLA_KW), BF16), jax.ShapeDtypeStruct((s, GLA_KW), BF16),
                   jax.ShapeDtypeStruct((s, GLA_W), BF16), jax.ShapeDtypeStruct((s, GLA_KW), F32)],
        scratch_shapes=[pltpu.VMEM((GLA_DV, GLA_DK), F32)],
        compiler_params=_cp("parallel", "arbitrary"), name=name,
    )(proj, proj, proj, log_a, do, states, states)


def _row_tile(r):
    tr = min(ROW_TILE, r)
    while r % tr or tr % 8:
        tr -= 1
    return tr


def _adamw_math(w, g, m, v):
    m = ADAM_B1 * m + (1.0 - ADAM_B1) * g
    v = ADAM_B2 * v + (1.0 - ADAM_B2) * (g * g)
    m_hat = m / (1.0 - ADAM_B1 ** ADAM_STEP)
    v_hat = v / (1.0 - ADAM_B2 ** ADAM_STEP)
    delta = -ADAM_LR * (m_hat / (jnp.sqrt(v_hat) + ADAM_EPS) + ADAM_WD * w)
    return delta, m, v


COL_TILE = 256


def _tile_2d(r, c):
    if r % 8 == 0 and _row_tile(r) >= 64:
        return _row_tile(r), c
    assert c % COL_TILE == 0, (r, c)
    return r, COL_TILE


def _half_shape(shape):
    r, c = shape[-2:]
    return tuple(shape[:-2]) + ((r // 2, c) if _half_axis(r) == 0 else (r, c // 2))


def _adam(g, w, m, v, *, name):
    r, c = w.shape
    tr, tc = _tile_2d(r, c)

    def body(g_ref, w_ref, m_ref, v_ref, d_ref, mo_ref, vo_ref):
        d, mn, vn = _adamw_math(w_ref[...], g_ref[...], m_ref[...], v_ref[...])
        d_ref[...] = d
        mo_ref[...] = mn
        vo_ref[...] = vn

    spec = pl.BlockSpec((tr, tc), lambda i, j: (i, j))
    return pl.pallas_call(
        body, grid=(r // tr, c // tc), in_specs=[spec] * 4, out_specs=[spec] * 3,
        out_shape=[jax.ShapeDtypeStruct((r, c), F32)] * 3,
        compiler_params=_cp("parallel", "parallel"), name=name,
    )(g, w, m, v)


def _ada_grad_adam(c_all_t, dmod_cols, w, m, v, *, name):
    r, c = w.shape
    tr, tc = min(512, r), min(1024, c)

    def body(ct_ref, dm_ref, w_ref, m_ref, v_ref, g_ref, d_ref, mo_ref, vo_ref):
        g = _dot_nn(ct_ref[...], dm_ref[...], HIGHEST)
        g_ref[...] = g
        d, mn, vn = _adamw_math(w_ref[...], g, m_ref[...], v_ref[...])
        d_ref[...] = d
        mo_ref[...] = mn
        vo_ref[...] = vn

    spec = pl.BlockSpec((tr, tc), lambda i, j: (i, j))
    nb = c_all_t.shape[1]
    return pl.pallas_call(
        body, grid=(r // tr, c // tc),
        in_specs=[pl.BlockSpec((tr, nb), lambda i, j: (i, 0)), pl.BlockSpec((nb, tc), lambda i, j: (0, j)),
                  spec, spec, spec],
        out_specs=[spec] * 4, out_shape=[jax.ShapeDtypeStruct((r, c), F32)] * 4,
        compiler_params=_cp("parallel", "parallel"), name=name,
    )(c_all_t, dmod_cols, w, m, v)


def _mod_shard(c_all, w, b, *, name):
    k, c = w.shape
    tc = min(512, c)
    nb = c_all.shape[0]

    def body(c_ref, w_ref, b_ref, o_ref):
        o_ref[...] = _dot_nn(c_ref[...], w_ref[...], HIGHEST) + b_ref[...]

    return pl.pallas_call(
        body, grid=(c // tc,),
        in_specs=[pl.BlockSpec((nb, k), lambda j: (0, 0)), pl.BlockSpec((k, tc), lambda j: (0, j)),
                  pl.BlockSpec((1, tc), lambda j: (0, j))],
        out_specs=pl.BlockSpec((nb, tc), lambda j: (0, j)),
        out_shape=jax.ShapeDtypeStruct((nb, c), F32),
        compiler_params=_cp("parallel"), name=name,
    )(c_all, w, b)


def _silu_rows(c, *, name):
    def body(c_ref, o_ref):
        cv = c_ref[...]
        o_ref[...] = cv * _sigmoid(cv)

    return pl.pallas_call(body, out_shape=jax.ShapeDtypeStruct(c.shape, F32), name=name)(c)


def _pair_sum(g, got, idx, *, name):
    p, r, c = g.shape
    ax = _half_axis(r)
    hr, hc = _half_shape((r, c))
    tr, tc = _tile_2d(hr, hc)
    nbr, nbc = hr // tr, hc // tc

    def body(idx_ref, a_ref, b_ref, o_ref):
        o_ref[...] = (a_ref[...].astype(F32) + b_ref[...].astype(F32)).astype(BF16)

    def own_map(i, j, k, idx_ref):
        return (i, j + (idx_ref[0] * nbr if ax == 0 else 0), k + (idx_ref[0] * nbc if ax == 1 else 0))

    half_spec = pl.BlockSpec((1, tr, tc), lambda i, j, k, idx_ref: (i, j, k))
    return pl.pallas_call(
        body,
        grid_spec=pltpu.PrefetchScalarGridSpec(
            num_scalar_prefetch=1, grid=(p, nbr, nbc),
            in_specs=[pl.BlockSpec((1, tr, tc), own_map), half_spec],
            out_specs=half_spec),
        out_shape=jax.ShapeDtypeStruct((p, hr, hc), BF16),
        compiler_params=_cp("parallel", "parallel", "parallel"), name=name,
    )(idx, g, got)


def _final_sum(own, parts, idx, shard_shape, *, name):
    ax = _half_axis(shard_shape[0])
    hr, hc = own.shape[1:]
    tr, tc = _tile_2d(hr, hc)
    nbr, nbc = hr // tr, hc // tc

    def body(idx_ref, own_ref, parts_ref, o_ref):
        acc = own_ref[0].astype(F32)
        for q in range(3):
            acc = acc + parts_ref[q].astype(F32)
        o_ref[...] = acc

    def out_map(j, k, idx_ref):
        return (j + (idx_ref[0] * nbr if ax == 0 else 0), k + (idx_ref[0] * nbc if ax == 1 else 0))

    return pl.pallas_call(
        body,
        grid_spec=pltpu.PrefetchScalarGridSpec(
            num_scalar_prefetch=1, grid=(nbr, nbc),
            in_specs=[pl.BlockSpec((1, tr, tc), lambda j, k, idx_ref: (idx_ref[1], j, k)),
                      pl.BlockSpec((3, tr, tc), lambda j, k, idx_ref: (0, j, k))],
            out_specs=pl.BlockSpec((tr, tc), out_map)),
        out_shape=jax.ShapeDtypeStruct(tuple(shard_shape), F32),
        compiler_params=_cp("parallel", "parallel"), name=name,
    )(idx, own, parts)


def _stack_sum(x, *, name):
    p, r, c = x.shape
    tr = _row_tile(r)

    def body(x_ref, o_ref):
        acc = x_ref[0].astype(F32)
        for q in range(1, p):
            acc = acc + x_ref[q].astype(F32)
        o_ref[...] = acc

    return pl.pallas_call(
        body, grid=(r // tr,),
        in_specs=[pl.BlockSpec((p, tr, c), lambda i: (0, i, 0))],
        out_specs=pl.BlockSpec((tr, c), lambda i: (i, 0)),
        out_shape=jax.ShapeDtypeStruct((r, c), F32),
        compiler_params=_cp("parallel"), name=name,
    )(x)


def _place():
    x, y, c = lax.axis_index("x"), lax.axis_index("y"), lax.axis_index("c")
    chips = [(1 - x, y), (x, 1 - y), (1 - x, 1 - y)]
    return x, y, c, chips


def _gather8(x_shard, *, name):
    m_per, n = x_shard.shape

    def body(x_ref, out_ref, send_sems, recv_sems, local_sem):
        x, y, c, chips = _place()
        me, sibling = (x, y, c), (x, y, 1 - c)

        def rows(px, py, pc):
            return out_ref.at[pl.ds((4 * px + 2 * py + pc) * m_per, m_per), :]

        def copy(k, block, to, src=None):
            return pltpu.make_async_remote_copy(
                src_ref=rows(*block) if src is None else src, dst_ref=rows(*block),
                send_sem=send_sems.at[k], recv_sem=recv_sems.at[k], device_id=to, device_id_type=MESH)

        mine = pltpu.make_async_copy(x_ref, rows(*me), local_sem)
        mine.start()
        first = [copy(0, me, sibling, src=x_ref)]
        first += [copy(1 + j, me, (*chip, c), src=x_ref) for j, chip in enumerate(chips)]
        for cp in first:
            cp.start()
        passed = [copy(4 + j, (*chip, c), sibling) for j, chip in enumerate(chips)]
        for j, chip in enumerate(chips):
            copy(1 + j, (*chip, c), me).wait_recv()
            passed[j].start()
        copy(0, sibling, me).wait_recv()
        for j, chip in enumerate(chips):
            copy(4 + j, (*chip, 1 - c), me).wait_recv()
        for cp in first + passed:
            cp.wait_send()
        mine.wait()

    return pl.pallas_call(
        body,
        out_shape=jax.ShapeDtypeStruct((8 * m_per, n), x_shard.dtype),
        in_specs=[pl.BlockSpec(memory_space=pltpu.VMEM)],
        out_specs=pl.BlockSpec(memory_space=pltpu.VMEM),
        scratch_shapes=[pltpu.SemaphoreType.DMA((7,)), pltpu.SemaphoreType.DMA((7,)), pltpu.SemaphoreType.DMA],
        name=name,
    )(x_shard)


def _gather_relayed(shard, *, name):
    def body(shard_ref, full_ref, send_sems, recv_sems):
        x, y, c, _ = _place()
        ax = _half_axis(shard_ref.shape[0])
        me, xn, yn, dg = 2 * x + y, 2 * (1 - x) + y, 2 * x + (1 - y), 2 * (1 - x) + (1 - y)
        to_x, to_y = (1 - x, y, c), (x, 1 - y, c)

        def half(slot, piece=None):
            return _rows_half(full_ref.at[slot], c, ax, piece)

        def copy(k, src, dst, peer):
            return pltpu.make_async_remote_copy(src_ref=src, dst_ref=dst, send_sem=send_sems.at[k],
                                                recv_sem=recv_sems.at[k], device_id=peer, device_id_type=MESH)

        mine = _rows_half(shard_ref, c, ax)
        sends = [copy(0, mine, half(me), to_x), copy(1, mine, half(me), to_y)]
        for cp in sends:
            cp.start()
        copy(0, mine, half(xn), to_x).wait_recv()
        relay_y = copy(2, half(xn, (0, 2)), half(xn, (0, 2)), to_y)
        relay_y.start()
        copy(1, mine, half(yn), to_y).wait_recv()
        relay_x = copy(3, half(yn, (1, 2)), half(yn, (1, 2)), to_x)
        relay_x.start()
        copy(2, half(xn, (0, 2)), half(dg, (0, 2)), to_y).wait_recv()
        copy(3, half(yn, (1, 2)), half(dg, (1, 2)), to_x).wait_recv()
        for cp in sends + [relay_y, relay_x]:
            cp.wait_send()

    any_spec = pl.BlockSpec(memory_space=pl.ANY)
    return pl.pallas_call(
        body, out_shape=jax.ShapeDtypeStruct((4,) + shard.shape, shard.dtype),
        in_specs=[any_spec], out_specs=any_spec,
        scratch_shapes=[pltpu.SemaphoreType.DMA((4,)), pltpu.SemaphoreType.DMA((4,))], name=name,
    )(shard)


def _plan_start(plan, send_sems, recv_sems):
    for k, (src, dst, _, peer) in enumerate(plan):
        pltpu.make_async_remote_copy(src_ref=src, dst_ref=dst, send_sem=send_sems.at[k], recv_sem=recv_sems.at[k],
                                     device_id=peer, device_id_type=MESH).start()


def _plan_wait(plan, send_sems, recv_sems):
    for k, (src, _, land, peer) in enumerate(plan):
        pltpu.make_async_remote_copy(src_ref=src, dst_ref=land, send_sem=send_sems.at[k], recv_sem=recv_sems.at[k],
                                     device_id=peer, device_id_type=MESH).wait_recv()
    for k, (src, dst, _, peer) in enumerate(plan):
        pltpu.make_async_remote_copy(src_ref=src, dst_ref=dst, send_sem=send_sems.at[k], recv_sem=recv_sems.at[k],
                                     device_id=peer, device_id_type=MESH).wait_send()


def _half_axis(rows):
    return 0 if rows % 32 == 0 else 1


def _rows_half(ref, hc, axis, part=None):
    size = ref.shape[axis] // 2
    start = hc * size
    if part is not None:
        size //= part[1]
        start = start + part[0] * size
    idx = [slice(None)] * len(ref.shape)
    idx[axis] = pl.ds(start, size)
    return ref.at[tuple(idx)]


def _plan_gather_ici(shard, full, part=None):
    x, y, c, chips = _place()
    ax = _half_axis(shard.shape[0])
    src = _rows_half(shard, c, ax, part)
    return [(src, _rows_half(full.at[2 * x + y], c, ax, part), _rows_half(full.at[2 * cx + cy], c, ax, part),
             (cx, cy, c)) for cx, cy in chips]


def _plan_gather_d2d(full, own):
    x, y, c, chips = _place()
    ax = _half_axis(full.shape[1])
    plan = []
    for cx, cy in chips:
        slot = full.at[2 * cx + cy]
        plan.append((_rows_half(slot, c, ax), _rows_half(slot, c, ax), _rows_half(slot, 1 - c, ax), (x, y, 1 - c)))
    mine = full.at[2 * x + y]
    plan.append((own, mine, mine, (x, y, 1 - c)))
    return plan


def _plan_pair(grad, got):
    x, y, c, _ = _place()
    return [(_rows_half(grad, 1 - c, 1 + _half_axis(grad.shape[1])), got, got, (x, y, 1 - c))]


def _plan_shard_ici(sums, parts, piece=None):
    _, _, c, chips = _place()

    def rows(ref):
        if piece is None:
            return ref
        k, n = piece
        if ref.shape[0] % (16 * n) == 0:
            size = ref.shape[0] // n
            return ref.at[pl.ds(k * size, size), :]
        size = ref.shape[1] // n
        return ref.at[:, pl.ds(k * size, size)]

    return [(rows(sums.at[2 * cx + cy]), rows(parts.at[k]), rows(parts.at[k]), (cx, cy, c))
            for k, (cx, cy) in enumerate(chips)]


def _plan_half(buf):
    x, y, c, _ = _place()
    ax = _half_axis(buf.shape[0])
    mine = _rows_half(buf, c, ax)
    return [(mine, mine, _rows_half(buf, 1 - c, ax), (x, y, 1 - c))]


def _comm_call(plan_fn, inputs, out_shapes, *, name, aliases=None):
    ni, no = len(inputs), len(out_shapes)

    def body(*refs):
        plan = plan_fn(refs[:ni], refs[ni:ni + no])
        send_sems, recv_sems = refs[ni + no:]
        _plan_start(plan, send_sems, recv_sems)
        _plan_wait(plan, send_sems, recv_sems)

    any_spec = pl.BlockSpec(memory_space=pl.ANY)
    n_copies = 3 * max(ni, no)
    return pl.pallas_call(
        body, out_shape=list(out_shapes), in_specs=[any_spec] * ni, out_specs=[any_spec] * no,
        scratch_shapes=[pltpu.SemaphoreType.DMA((n_copies,)), pltpu.SemaphoreType.DMA((n_copies,))],
        input_output_aliases=aliases or {}, name=name,
    )(*inputs)


def _gather_forward(full, own, *, name):
    return _comm_call(lambda ins, outs: _plan_gather_d2d(outs[0], ins[1]), [full, own],
                      [jax.ShapeDtypeStruct(full.shape, full.dtype)], name=name, aliases={0: 0})[0]


def _half_exchange(bufs, *, name):
    return _comm_call(lambda ins, outs: [cp for o in outs for cp in _plan_half(o)],
                      bufs, [jax.ShapeDtypeStruct(b.shape, b.dtype) for b in bufs], name=name,
                      aliases={k: k for k in range(len(bufs))})


def _split_w_in(w_in_t):
    d = w_in_t.shape[1]
    main = jnp.concatenate([w_in_t[0:3072], w_in_t[3080:5128], w_in_t[5144:6168]], axis=0)
    small = jnp.concatenate([w_in_t[3072:3080], w_in_t[5128:5144], jnp.zeros((SMALL_W - 24, d), w_in_t.dtype)], axis=0)
    return main, small


def _merge_dw_in(dw_main, dw_small):
    return jnp.concatenate([dw_main[0:3072], dw_small[0:8], dw_main[3072:5120], dw_small[8:24], dw_main[5120:6144]],
                           axis=0)


def _gather_side(shards):
    return _Side(shards, [jax.ShapeDtypeStruct((4,) + w.shape, w.dtype) for w in shards],
                 lambda ins, outs: [cp for i, o in zip(ins, outs) for cp in _plan_gather_ici(i, o)], 3 * len(shards))


def _forward_side(full, own):
    return _Side([full, own], [jax.ShapeDtypeStruct(full.shape, full.dtype)],
                 lambda ins, outs: _plan_gather_d2d(outs[0], ins[1]), 4, aliases={0: 0})


def _half_side(bufs):
    return _Side(bufs, [jax.ShapeDtypeStruct(b.shape, b.dtype) for b in bufs],
                 lambda ins, outs: [cp for o in outs for cp in _plan_half(o)], len(bufs),
                 aliases={k: k for k in range(len(bufs))})


def _parts_shape(sums):
    return jax.ShapeDtypeStruct((3,) + sums.shape[1:], sums.dtype)


def _got_shape(grad):
    return jax.ShapeDtypeStruct(_half_shape(grad.shape), grad.dtype)


def _pair_side(grad):
    return _Side([grad], [_got_shape(grad)], lambda ins, outs: _plan_pair(ins[0], outs[0]), 1)


def _local_step(x, target, mod, g_pre_mix, g_post_mix, g_pre_mlp, g_post_mlp, gw_in, b_fgate, w_gla_a2,
                b_gla_a2, g_fox, g_gla, own_w_in, own_w_out, own_w_mlp_in, own_w_mlp_out, idx):
    s, d = x.shape
    shift_m, scale_m, gate_m, shift_f, scale_f, gate_f = [mod[:, i * d:(i + 1) * d] for i in range(6)]
    a1 = g_pre_mix * (1.0 + scale_m)
    a2 = g_pre_mlp * (1.0 + scale_f)
    bf = jnp.concatenate([b_fgate, jnp.zeros((1, SMALL_W - FOX_HEADS), F32)], axis=1)
    w2p = jnp.zeros((SMALL_W, GLA_KW), F32).at[FOX_HEADS:FOX_HEADS + GLA_RANK].set(w_gla_a2)

    h1, gw_in = _pre_fwd(x, a1, shift_m, name="pre_mix_fwd", side=_forward_side(gw_in, own_w_in))
    w_in_t = gw_in.reshape(-1, d)
    w_main, w_small = _split_w_in(w_in_t)
    full_shape = lambda w: jax.ShapeDtypeStruct((4,) + w.shape, w.dtype)
    first_side = _Side(
        [own_w_out, own_w_mlp_out], [full_shape(own_w_out), full_shape(own_w_mlp_out)],
        lambda ins, outs: _plan_gather_ici(ins[0], outs[0]) + _plan_gather_ici(ins[1], outs[1], part=(0, 4)), 6)
    proj, gw_out, gw_mlp_out = _mm(h1, w_main, mode="nt", out_dtypes=[BF16], name="in_proj_main", side=first_side)
    ps, gw_out = _mm(h1, w_small, mode="nt", out_dtypes=[F32], name="in_proj_small",
                     side=_forward_side(gw_out, own_w_out))
    w_out_full = gw_out.reshape(-1, d)
    cum, log_a = _gates_fwd(ps, bf, w2p, b_gla_a2, name="gates_fwd")
    cum_t = cum[:, :FOX_HEADS].T
    o_fox, fox_n, lse, gw_mlp_in = _fox_fwd(proj, cum_t, g_fox, name="fox_fwd", side=_gather_side([own_w_mlp_in]))
    o_gla, gla_n, states = _gla_fwd(proj, log_a, g_gla, name="gla_fwd")
    mixed = jnp.concatenate([fox_n, gla_n], axis=1)
    y1, gw_mlp_in = _mm(mixed, w_out_full, mode="nn", out_dtypes=[F32], name="out_proj",
                        side=_forward_side(gw_mlp_in, own_w_mlp_in))
    x1, h2 = _post_pre_fwd(x, y1, gate_m, g_post_mix, a2, shift_f, name="post_mix_pre_mlp_fwd")

    def mlp_act(acc):
        r = jnp.maximum(acc, 0.0)
        return acc, r * r

    rest_side = _Side([own_w_mlp_out, gw_mlp_out], [full_shape(own_w_mlp_out)],
                      lambda ins, outs: [cp for q in (1, 2, 3) for cp in _plan_gather_ici(ins[0], outs[0], part=(q, 4))],
                      9, aliases={1: 0})
    u, act, gw_mlp_out = _mm(h2, gw_mlp_in, mode="nn", out_dtypes=[BF16, BF16], epi=mlp_act, name="mlp_in",
                             b_slots=4, tm=MM_TM, side=rest_side)
    gw_mlp_out = _gather_forward(gw_mlp_out, own_w_mlp_out, name="gather_w_mlp_out_d2d")
    w_mlp_out_full = gw_mlp_out.reshape(-1, d)
    y2, = _mm(act, w_mlp_out_full, mode="nn", out_dtypes=[F32], name="mlp_out")
    dx2, dy2, loss_part, dgate_f, dg_post_mlp = _post_loss_bwd(x1, y2, gate_f, g_post_mlp, target,
                                                               name="post_mlp_loss_bwd")
    dw_mlp_out, = _mm(act, dy2, mode="tn", out_dtypes=[BF16], name="dw_mlp_out")
    dw_mlp_out = dw_mlp_out.reshape(4, D_FF // 4, d)

    def act_bwd(acc, uv):
        return (acc * (2.0 * jnp.maximum(uv.astype(F32), 0.0)),)

    du, got_mlp_out = _mm(dy2, w_mlp_out_full, mode="nt", out_dtypes=[BF16], extras=[u], epi=act_bwd,
                          name="d_mlp_hidden", tm=MM_TM, side=_pair_side(dw_mlp_out))
    sum_mlp_out = _pair_sum(dw_mlp_out, got_mlp_out, idx, name="grad_pair_sum_mlp_out")
    nj = D_FF // 4 // min(MM_T, D_FF // 4)
    tmw = min(MM_T, d)
    dw_mlp_in, parts_mlp_out = _mm(
        h2, du, mode="tn", out_dtypes=[BF16], name="dw_mlp_in",
        out_shapes=[jax.ShapeDtypeStruct((4, d, D_FF // 4), BF16)],
        out_specs=[pl.BlockSpec((1, tmw, min(MM_T, D_FF // 4)), lambda i, j, kk: (j // nj, i, j % nj))],
        side=_Side([sum_mlp_out], [_parts_shape(sum_mlp_out)],
                   lambda ins, outs: _plan_shard_ici(ins[0], outs[0], piece=(0, 2)), 3))
    dh2, got_mlp_in, parts_mlp_out = _mm(
        du, gw_mlp_in, mode="nt", out_dtypes=[F32], name="d_mlp_in", b_slots=4,
        side=_Side([dw_mlp_in, sum_mlp_out, parts_mlp_out], [_got_shape(dw_mlp_in), _parts_shape(sum_mlp_out)],
                   lambda ins, outs: _plan_pair(ins[0], outs[0]) + _plan_shard_ici(ins[1], outs[1], piece=(1, 2)),
                   4, aliases={2: 1}))
    sum_mlp_in = _pair_sum(dw_mlp_in, got_mlp_in, idx, name="grad_pair_sum_mlp_in")
    dx1, dshift_f, da2, dy1, dgate_m, dg_post_mix = _pre_post_bwd(dh2, x1, dx2, a2, y1, gate_m, g_post_mix,
                                                                  name="pre_mlp_post_mix_bwd")
    buf_mlp_out = _final_sum(sum_mlp_out, parts_mlp_out, idx, (D_FF // 4, d), name="grad_final_sum_mlp_out")
    dw_out, g_mlp_out = _mm(mixed, dy1, mode="tn", out_dtypes=[BF16], name="dw_out", side=_half_side([buf_mlp_out]))
    dw_out = dw_out.reshape(4, d // 4, d)
    dmixed, got_out = _mm(dy1, w_out_full, mode="nt", out_dtypes=[BF16], name="d_mixed", side=_pair_side(dw_out))
    sum_out = _pair_sum(dw_out, got_out, idx, name="grad_pair_sum_out")
    do_fox, delta, dg_fox = _head_norm_bwd(dmixed, o_fox, g_fox, None, nh=FOX_HEADS, hd=FOX_HD, dn_col=0,
                                           gr_col=0, name="fox_norm_bwd")
    do_gla, dgr, _, dg_gla = _head_norm_bwd(dmixed, o_gla, g_gla, proj, nh=GLA_HEADS, hd=GLA_DV, dn_col=1,
                                            gr_col=(3 * FOX_W + 2 * GLA_KW + GLA_W) // GLA_W, name="gla_norm_bwd")
    dq_fox, dk_fox, dv_fox, dcq, dck_t, parts_mlp_in, parts_out = _fox_bwd(
        proj, do_fox, cum_t, lse, delta, name="fox_bwd",
        side=_Side([sum_mlp_in, sum_out], [_parts_shape(sum_mlp_in), _parts_shape(sum_out)],
                   lambda ins, outs: _plan_shard_ici(ins[0], outs[0]) + _plan_shard_ici(ins[1], outs[1]), 6))
    dgq, dgk, dgv, dla = _gla_bwd(proj, log_a, do_gla, states, name="gla_bwd")
    dck = dcq + jnp.concatenate([dck_t.T, jnp.zeros((s, SMALL_W - FOX_HEADS), F32)], axis=1)
    dps, dbf, dw2p, db2 = _gates_bwd(dck, ps, bf, w2p, b_gla_a2, dla, name="gates_bwd")
    dproj = jnp.concatenate([dq_fox.astype(BF16), dk_fox, dv_fox, dgq, dgk, dgv, dgr], axis=1)
    buf_mlp_in = _final_sum(sum_mlp_in, parts_mlp_in, idx, (d, D_FF // 4), name="grad_final_sum_mlp_in")
    buf_out = _final_sum(sum_out, parts_out, idx, (d // 4, d), name="grad_final_sum_out")
    dw_main, g_mlp_in, g_out = _mm(dproj, h1, mode="tn", out_dtypes=[BF16], name="dw_in_main",
                                   side=_half_side([buf_mlp_in, buf_out]))
    dw_small, = _mm(dps, h1, mode="tn", out_dtypes=[BF16], name="dw_in_small")
    rs_in = w_in_t.shape[0] // 4
    dw_in = _merge_dw_in(dw_main, dw_small).reshape(4, rs_in, d)
    dh1_small, got_in = _mm(dps, w_small, mode="nn", out_dtypes=[F32], name="d_h1_small", side=_pair_side(dw_in))
    sum_in = _pair_sum(dw_in, got_in, idx, name="grad_pair_sum_in")
    dh1, parts_in = _mm(
        dproj, w_main, mode="nn", out_dtypes=[F32], extras=[dh1_small], epi=lambda acc, e: (acc + e,), name="d_h1",
        side=_Side([sum_in], [_parts_shape(sum_in)],
                   lambda ins, outs: [cp for q in range(3) for cp in _plan_shard_ici(ins[0], outs[0], piece=(q, 4))],
                   9))
    grad_x, dshift_m, da1, parts_in = _pre_bwd(
        dh1, x, dx1, a1, name="pre_mix_bwd",
        side=_Side([sum_in, parts_in], [_parts_shape(sum_in)],
                   lambda ins, outs: _plan_shard_ici(ins[0], outs[0], piece=(3, 4)), 3, aliases={1: 0}))
    buf_in = _final_sum(sum_in, parts_in, idx, (rs_in, d), name="grad_final_sum_in")
    g_in, = _half_exchange([buf_in], name="grad_half_exchange_in")
    g_big = [g_in, g_out, g_mlp_in, g_mlp_out]

    dmod = jnp.concatenate([dshift_m, da1 * g_pre_mix, dgate_m, dshift_f, da2 * g_pre_mlp, dgate_f], axis=1)
    small = dict(
        dmod=dmod, g_pre_mix=da1 * (1.0 + scale_m), g_post_mix=dg_post_mix, g_pre_mlp=da2 * (1.0 + scale_f),
        g_post_mlp=dg_post_mlp, b_fgate=dbf[:, :FOX_HEADS], w_gla_a2=dw2p[FOX_HEADS:FOX_HEADS + GLA_RANK],
        b_gla_a2=db2, g_fox_out=dg_fox, g_gla_out=dg_gla)
    return loss_part, grad_x, g_big, small


def _pack(arrays):
    flat = jnp.concatenate([a.reshape(-1).astype(F32) for a in arrays])
    n = flat.shape[0]
    rows = -(-n // 128)
    rows = -(-rows // 8) * 8
    return jnp.pad(flat, (0, rows * 128 - n)).reshape(rows, 128)


def _unpack(buf, shapes):
    flat = buf.reshape(-1)
    out, off = [], 0
    for shp in shapes:
        n = 1
        for q in shp:
            n *= q
        out.append(flat[off:off + n].reshape(shp))
        off += n
    return out


SMALL_GRAD_ORDER = ["dmod", "g_pre_mix", "g_post_mix", "g_pre_mlp", "g_post_mlp", "b_fgate", "w_gla_a2", "b_gla_a2",
                    "g_fox_out", "g_gla_out"]


def kernel(x, c, w_ada, b_ada, g_pre_mix, g_post_mix, w_in, b_fgate, w_gla_a2, b_gla_a2, g_fox_out, g_gla_out, w_out, g_pre_mlp, g_post_mlp, w_mlp_in, w_mlp_out, loss_target, m_w_ada, m_b_ada, m_g_pre_mix, m_g_post_mix, m_w_in, m_b_fgate, m_w_gla_a2, m_b_gla_a2, m_g_fox_out, m_g_gla_out, m_w_out, m_g_pre_mlp, m_g_post_mlp, m_w_mlp_in, m_w_mlp_out, v_w_ada, v_b_ada, v_g_pre_mix, v_g_post_mix, v_w_in, v_b_fgate, v_w_gla_a2, v_b_gla_a2, v_g_fox_out, v_g_gla_out, v_w_out, v_g_pre_mlp, v_g_post_mlp, v_w_mlp_in, v_w_mlp_out):
    ix, iy, ic = lax.axis_index("x"), lax.axis_index("y"), lax.axis_index("c")
    chip = 2 * ix + iy
    dev = 4 * ix + 2 * iy + ic
    d = D_MODEL

    c_act = _silu_rows(c, name="silu_c")
    pack1 = _pack([c_act, w_gla_a2[0], g_gla_out[0]])
    rows1 = pack1.shape[0]
    got1 = _gather8(pack1, name="gather_small_fwd").reshape(8, rows1, 128)
    per_dev = [_unpack(got1[q], [(d,), (GLA_RANK, GLA_KW // 4), (GLA_HEADS, GLA_DV // 4)]) for q in range(8)]
    c_all = jnp.stack([p[0] for p in per_dev])
    w_gla_a2_full = jnp.concatenate([per_dev[2 * j][1] for j in range(4)], axis=1)
    g_gla_full = jnp.concatenate([per_dev[2 * j][2] for j in range(4)], axis=1)
    cols = w_ada.shape[2]
    b_ada_shard = lax.dynamic_slice_in_dim(b_ada, chip * cols, cols, axis=1)
    mod_sh = _mod_shard(c_all, w_ada[0], b_ada_shard, name="ada_mod")
    got2 = _gather8(mod_sh, name="gather_mod").reshape(8, 8, cols)
    mod_all = jnp.concatenate([got2[2 * j] for j in range(4)], axis=1)
    mod = lax.dynamic_slice_in_dim(mod_all, dev, 1, axis=0)

    tr_in = lambda a: jnp.transpose(a[0])
    own_bf = [tr_in(w_in).astype(BF16), w_out[0].astype(BF16), w_mlp_in[0].astype(BF16), w_mlp_out[0].astype(BF16)]
    gw_in = _gather_relayed(own_bf[0], name="gather_w_in_ici")
    idx = jnp.stack([ic, chip]).astype(jnp.int32)
    loss_part, grad_x, g_big, small = _local_step(
        x[0], loss_target[0], mod, g_pre_mix, g_post_mix, g_pre_mlp, g_post_mlp, gw_in, b_fgate,
        w_gla_a2_full, b_gla_a2, g_fox_out[0], g_gla_full, own_bf[0], own_bf[1], own_bf[2], own_bf[3], idx)
    loss = lax.psum(loss_part[0, 0], ("x", "y", "c"))

    big_w = [(tr_in(w_in), tr_in(m_w_in), tr_in(v_w_in)), (w_out[0], m_w_out[0], v_w_out[0]),
             (w_mlp_in[0], m_w_mlp_in[0], v_w_mlp_in[0]), (w_mlp_out[0], m_w_mlp_out[0], v_w_mlp_out[0])]
    big_res = []
    for q, (g, (w, m, v)) in enumerate(zip(g_big, big_w)):
        res4 = (g,) + tuple(_adam(g, w, m, v, name=f"adam_big_{q}"))
        big_res.append(tuple((jnp.transpose(a) if q == 0 else a)[None] for a in res4))

    pack2 = _pack([small[k] for k in SMALL_GRAD_ORDER])
    rows2 = pack2.shape[0]
    got3 = _gather8(pack2, name="gather_small_grads").reshape(8, rows2, 128)
    dmod_all = got3[:, :6 * d // 128, :].reshape(8, 6 * d)
    sums = _stack_sum(got3, name="small_grad_sum")
    shapes = [(1, 6 * d), (1, d), (1, d), (1, d), (1, d), (1, FOX_HEADS), (1, GLA_RANK, GLA_KW), (1, GLA_KW),
              (1, FOX_HEADS, FOX_HD), (1, GLA_HEADS, GLA_DV)]
    sg = dict(zip(["b_ada"] + SMALL_GRAD_ORDER[1:], _unpack(sums, shapes)))
    sg["w_gla_a2"] = lax.dynamic_slice_in_dim(sg["w_gla_a2"], chip * (GLA_KW // 4), GLA_KW // 4, axis=2)
    sg["g_gla_out"] = lax.dynamic_slice_in_dim(sg["g_gla_out"], chip * (GLA_DV // 4), GLA_DV // 4, axis=2)
    small_names = ["b_ada", "g_pre_mix", "g_post_mix", "b_fgate", "w_gla_a2", "b_gla_a2", "g_fox_out", "g_gla_out",
                   "g_pre_mlp", "g_post_mlp"]
    small_w = dict(b_ada=(b_ada, m_b_ada, v_b_ada), g_pre_mix=(g_pre_mix, m_g_pre_mix, v_g_pre_mix),
                   g_post_mix=(g_post_mix, m_g_post_mix, v_g_post_mix), b_fgate=(b_fgate, m_b_fgate, v_b_fgate),
                   w_gla_a2=(w_gla_a2, m_w_gla_a2, v_w_gla_a2), b_gla_a2=(b_gla_a2, m_b_gla_a2, v_b_gla_a2),
                   g_fox_out=(g_fox_out, m_g_fox_out, v_g_fox_out), g_gla_out=(g_gla_out, m_g_gla_out, v_g_gla_out),
                   g_pre_mlp=(g_pre_mlp, m_g_pre_mlp, v_g_pre_mlp), g_post_mlp=(g_post_mlp, m_g_post_mlp, v_g_post_mlp))
    sshapes = [small_w[k][0].shape for k in small_names]
    pg = _pack([sg[k] for k in small_names])
    pw, pm, pv = [_pack([small_w[k][q] for k in small_names]) for q in range(3)]
    pd, pmn, pvn = _adam(pg, pw, pm, pv, name="adam_small")
    s_delta = dict(zip(small_names, _unpack(pd, sshapes)))
    s_m = dict(zip(small_names, _unpack(pmn, sshapes)))
    s_v = dict(zip(small_names, _unpack(pvn, sshapes)))

    dmod_cols = lax.dynamic_slice_in_dim(dmod_all, chip * cols, cols, axis=1)
    g_ada, d_ada, m_ada, v_ada = _ada_grad_adam(c_all.T, dmod_cols, w_ada[0], m_w_ada[0], v_w_ada[0], name="ada_grad_adam")

    order = ["w_ada", "b_ada", "g_pre_mix", "g_post_mix", "w_in", "b_fgate", "w_gla_a2", "b_gla_a2", "g_fox_out",
             "g_gla_out", "w_out", "g_pre_mlp", "g_post_mlp", "w_mlp_in", "w_mlp_out"]
    res = {"w_ada": (g_ada[None], d_ada[None], m_ada[None], v_ada[None]),
           "w_in": big_res[0], "w_out": big_res[1], "w_mlp_in": big_res[2], "w_mlp_out": big_res[3]}
    for k in small_names:
        res[k] = (sg[k], s_delta[k], s_m[k], s_v[k])
    return (loss, grad_x[None], *[res[k][0] for k in order], *[res[k][1] for k in order],
            *[res[k][2] for k in order], *[res[k][3] for k in order])
```

```python
import functools

import jax
import jax.numpy as jnp
from jax import lax
from jax.experimental import pallas as pl
from jax.experimental.pallas import tpu as pltpu

F32 = jnp.float32
BF16 = jnp.bfloat16
MESH = pl.DeviceIdType.MESH
HIGHEST = lax.Precision.HIGHEST

D_MODEL = 2048
FOX_HEADS = 8
FOX_HD = 128
FOX_W = FOX_HEADS * FOX_HD
GLA_HEADS = 4
GLA_DK = 128
GLA_DV = 256
GLA_KW = GLA_HEADS * GLA_DK
GLA_W = GLA_HEADS * GLA_DV
GLA_RANK = 16
GLA_TEMP = 16.0
CHUNK = 64
D_FF = 4 * D_MODEL
EPS = 1e-6
MAIN_W = 3 * FOX_W + 2 * GLA_KW + 2 * GLA_W
SMALL_W = 128
NEG = -1e30

ADAM_LR = 0.001
ADAM_B1 = 0.9
ADAM_B2 = 0.999
ADAM_EPS = 1e-08
ADAM_WD = 0.01
ADAM_STEP = 10

VMEM_LIMIT = 52 * 1024 * 1024
ROW_TILE = 256
WIDE_ROW_TILE = 512
FOX_TQ = 512
FOX_TK = 512
GLA_ROWS = 512
GATE_TS = 512
MM_T = 1024
MM_TK = 2048
MM_TM = 2048


def _cp(*sem):
    return pltpu.CompilerParams(dimension_semantics=sem, vmem_limit_bytes=VMEM_LIMIT)


def _dot_nn(a, b, precision=None):
    return jnp.dot(a, b, preferred_element_type=F32, precision=precision)


def _dot_nt(a, b, precision=None):
    return lax.dot_general(a, b, (((1,), (1,)), ((), ())), preferred_element_type=F32, precision=precision)


def _dot_tn(a, b, precision=None):
    return lax.dot_general(a, b, (((0,), (0,)), ((), ())), preferred_element_type=F32, precision=precision)


def _sigmoid(x):
    return 1.0 / (1.0 + jnp.exp(-x))


def _log_sigmoid(x):
    return jnp.minimum(x, 0.0) - jnp.log(1.0 + jnp.exp(-jnp.abs(x)))


class _Side:
    def __init__(self, inputs, out_shapes, plan_fn, n_copies, aliases=None):
        self.inputs, self.out_shapes, self.plan_fn, self.n_copies = list(inputs), list(out_shapes), plan_fn, n_copies
        self.aliases = dict(aliases or {})

    def scratch(self):
        return [pltpu.SemaphoreType.DMA((self.n_copies,)), pltpu.SemaphoreType.DMA((self.n_copies,))]

    def run(self, in_refs, out_refs, sems, first, last):
        @pl.when(first)
        def _():
            _plan_start(self.plan_fn(in_refs, out_refs), *sems)

        @pl.when(last)
        def _():
            _plan_wait(self.plan_fn(in_refs, out_refs), *sems)


def _mm(a, b, *, mode, out_dtypes, name, tm=None, tn=None, tk=None, extras=(), epi=None,
        out_shapes=None, out_specs=None, side=None, b_slots=0):
    tm, tn, tk = tm or MM_T, tn or MM_T, tk or MM_TK
    b2 = (b.shape[1], b_slots * b.shape[2]) if b_slots else b.shape
    if mode == "nn":
        (m, k), n = a.shape, b2[1]
    elif mode == "nt":
        (m, k), n = a.shape, b2[0]
    else:
        (k, m), n = a.shape, b2[1]
    tm, tn, tk = min(tm, m), min(tn, n), min(tk, k)
    if b_slots:
        tn = min(tn, b.shape[2]) if mode == "nn" else tn
        tk = min(tk, b.shape[2]) if mode == "nt" else tk
    assert m % tm == 0 and n % tn == 0 and k % tk == 0, (name, m, n, k)
    nk = k // tk
    n_out, n_ex = len(out_dtypes), len(extras)
    if epi is None:
        epi = lambda acc: tuple(acc for _ in range(n_out))
    dot = {"nn": _dot_nn, "nt": _dot_nt, "tn": _dot_tn}[mode]

    n_si = len(side.inputs) if side else 0
    n_so = len(side.out_shapes) if side else 0
    grid = (m // tm, n // tn, nk)

    def body(*refs):
        a_ref, b_ref = refs[0], refs[1]
        ex_refs = refs[2:2 + n_ex]
        base = 2 + n_ex + n_si
        o_refs = refs[base:base + n_out]
        scratch = refs[base + n_out + n_so:]
        if side:
            pos = [pl.program_id(q) for q in range(3)]
            first = (pos[0] == 0) & (pos[1] == 0) & (pos[2] == 0)
            last = (pos[0] == grid[0] - 1) & (pos[1] == grid[1] - 1) & (pos[2] == grid[2] - 1)
            side.run(refs[2 + n_ex:base], refs[base + n_out:base + n_out + n_so], scratch[-2:], first, last)
        part = dot(a_ref[...], b_ref[...])

        def finish(acc):
            outs = epi(acc, *[e[...] for e in ex_refs])
            for o_ref, val in zip(o_refs, outs):
                o_ref[...] = val.reshape(o_ref.shape).astype(o_ref.dtype)

        if nk == 1:
            finish(part)
        else:
            acc_ref = scratch[0]
            kk = pl.program_id(2)

            @pl.when(kk == 0)
            def _():
                acc_ref[...] = part

            @pl.when(kk > 0)
            def _():
                acc_ref[...] += part

            @pl.when(kk == nk - 1)
            def _():
                finish(acc_ref[...])

    if mode == "nn":
        a_spec = pl.BlockSpec((tm, tk), lambda i, j, kk: (i, kk))
        b_spec = pl.BlockSpec((tk, tn), lambda i, j, kk: (kk, j))
        if b_slots:
            per = b.shape[2] // tn
            b_spec = pl.BlockSpec((None, tk, tn), lambda i, j, kk: (j // per, kk, j % per))
    elif mode == "nt":
        a_spec = pl.BlockSpec((tm, tk), lambda i, j, kk: (i, kk))
        b_spec = pl.BlockSpec((tn, tk), lambda i, j, kk: (j, kk))
        if b_slots:
            per = b.shape[2] // tk
            b_spec = pl.BlockSpec((None, tn, tk), lambda i, j, kk: (kk // per, j, kk % per))
    else:
        assert not b_slots
        a_spec = pl.BlockSpec((tk, tm), lambda i, j, kk: (kk, i))
        b_spec = pl.BlockSpec((tk, tn), lambda i, j, kk: (kk, j))
    tile_spec = pl.BlockSpec((tm, tn), lambda i, j, kk: (i, j))
    if out_shapes is None:
        out_shapes = [jax.ShapeDtypeStruct((m, n), dt) for dt in out_dtypes]
    if out_specs is None:
        out_specs = [tile_spec for _ in out_dtypes]
    any_spec = pl.BlockSpec(memory_space=pl.ANY)
    res = pl.pallas_call(
        body,
        grid=grid,
        in_specs=[a_spec, b_spec] + [tile_spec for _ in extras] + [any_spec] * n_si,
        out_specs=list(out_specs) + [any_spec] * n_so,
        out_shape=list(out_shapes) + (side.out_shapes if side else []),
        scratch_shapes=([pltpu.VMEM((tm, tn), F32)] if nk > 1 else []) + (side.scratch() if side else []),
        compiler_params=_cp("arbitrary", "arbitrary", "arbitrary") if side else _cp("parallel", "parallel", "arbitrary"),
        input_output_aliases={2 + n_ex + si: n_out + so for si, so in side.aliases.items()} if side else {},
        name=name,
    )(a, b, *extras, *(side.inputs if side else []))
    return res


def _row_spec(ts, d):
    return pl.BlockSpec((ts, d), lambda i: (i, 0))


def _vec_spec(d):
    return pl.BlockSpec((1, d), lambda i: (0, 0))


def _side_args(side, n_in, n_out):
    if side is None:
        return [], [], [], [], [], {}
    any_spec = pl.BlockSpec(memory_space=pl.ANY)
    return ([any_spec] * len(side.inputs), [any_spec] * len(side.out_shapes), side.out_shapes, side.scratch(),
            side.inputs, {n_in + si: n_out + so for si, so in side.aliases.items()})


def _pre_fwd(x, avec, shift, *, name, side=None):
    s, d = x.shape
    ts = min(WIDE_ROW_TILE, s)
    nb = s // ts
    s_in, s_out, s_shapes, s_scratch, s_ops, s_alias = _side_args(side, 3, 1)

    def body(x_ref, a_ref, s_ref, *rest):
        h_ref = rest[len(s_in)]
        if side:
            step = pl.program_id(0)
            side.run(rest[:len(s_in)], rest[len(s_in) + 1:len(s_in) + 1 + len(s_out)],
                     rest[len(s_in) + 1 + len(s_out):], step == 0, step == nb - 1)
        xv = x_ref[...]
        r = lax.rsqrt(jnp.mean(xv * xv, axis=-1, keepdims=True) + EPS)
        h_ref[...] = (xv * r * a_ref[...] + s_ref[...]).astype(BF16)

    res = pl.pallas_call(
        body, grid=(nb,),
        in_specs=[_row_spec(ts, d), _vec_spec(d), _vec_spec(d)] + s_in,
        out_specs=[_row_spec(ts, d)] + s_out,
        out_shape=[jax.ShapeDtypeStruct((s, d), BF16)] + s_shapes,
        scratch_shapes=s_scratch, input_output_aliases=s_alias,
        compiler_params=_cp("arbitrary" if side else "parallel"), name=name,
    )(x, avec, shift, *s_ops)
    return res if side else res[0]


def _post_pre_fwd(x, y, gate, g, avec, shift, *, name):
    s, d = x.shape
    ts = min(WIDE_ROW_TILE, s)

    def body(x_ref, y_ref, gate_ref, g_ref, a_ref, s_ref, o_ref, h_ref):
        yv = y_ref[...]
        r = lax.rsqrt(jnp.mean(yv * yv, axis=-1, keepdims=True) + EPS)
        x1 = x_ref[...] + gate_ref[...] * (yv * r * g_ref[...])
        o_ref[...] = x1
        r1 = lax.rsqrt(jnp.mean(x1 * x1, axis=-1, keepdims=True) + EPS)
        h_ref[...] = (x1 * r1 * a_ref[...] + s_ref[...]).astype(BF16)

    return pl.pallas_call(
        body, grid=(s // ts,),
        in_specs=[_row_spec(ts, d), _row_spec(ts, d)] + [_vec_spec(d)] * 4,
        out_specs=[_row_spec(ts, d), _row_spec(ts, d)],
        out_shape=[jax.ShapeDtypeStruct((s, d), F32), jax.ShapeDtypeStruct((s, d), BF16)],
        compiler_params=_cp("parallel"), name=name,
    )(x, y, gate, g, avec, shift)


def _post_bwd_math(dxv, yv, gatev, gv):
    r = lax.rsqrt(jnp.mean(yv * yv, axis=-1, keepdims=True) + EPS)
    yhat = yv * r
    dn = dxv * gatev
    dyhat = dn * gv
    dy = r * (dyhat - yhat * jnp.mean(dyhat * yhat, axis=-1, keepdims=True))
    return dy, dxv * (yhat * gv), dn * yhat


def _accumulate(first, pairs):
    @pl.when(first)
    def _():
        for ref, _ in pairs:
            ref[...] = jnp.zeros_like(ref)

    for ref, val in pairs:
        ref[...] += jnp.sum(val, axis=0, keepdims=True)


def _post_loss_bwd(x, y, gate, g, target, *, name):
    s, d = x.shape
    ts = min(ROW_TILE, s)

    def body(x_ref, y_ref, gate_ref, g_ref, t_ref, dx_ref, dy_ref, loss_ref, dgate_ref, dg_ref):
        yv, gatev, gv = y_ref[...], gate_ref[...], g_ref[...]
        r = lax.rsqrt(jnp.mean(yv * yv, axis=-1, keepdims=True) + EPS)
        diff = x_ref[...] + gatev * (yv * r * gv) - t_ref[...]
        dxv = diff * (1.0 / d)
        dx_ref[...] = dxv
        dy, dgate_rows, dg_rows = _post_bwd_math(dxv, yv, gatev, gv)
        dy_ref[...] = dy.astype(BF16)
        first = pl.program_id(0) == 0
        _accumulate(first, [(dgate_ref, dgate_rows), (dg_ref, dg_rows)])

        @pl.when(first)
        def _():
            loss_ref[...] = jnp.zeros_like(loss_ref)

        loss_ref[...] += jnp.sum(jnp.mean(diff * diff, axis=-1, keepdims=True)) * 0.5

    return pl.pallas_call(
        body, grid=(s // ts,),
        in_specs=[_row_spec(ts, d), _row_spec(ts, d), _vec_spec(d), _vec_spec(d), _row_spec(ts, d)],
        out_specs=[_row_spec(ts, d), _row_spec(ts, d), pl.BlockSpec((1, 128), lambda i: (0, 0)), _vec_spec(d),
                   _vec_spec(d)],
        out_shape=[jax.ShapeDtypeStruct((s, d), F32), jax.ShapeDtypeStruct((s, d), BF16),
                   jax.ShapeDtypeStruct((1, 128), F32), jax.ShapeDtypeStruct((1, d), F32),
                   jax.ShapeDtypeStruct((1, d), F32)],
        compiler_params=_cp("arbitrary"), name=name,
    )(x, y, gate, g, target)


def _pre_post_bwd(dh, xin, dres, avec, y, gate, g, *, name):
    s, d = xin.shape
    ts = min(ROW_TILE, s)

    def body(dh_ref, x_ref, dres_ref, a_ref, y_ref, gate_ref, g_ref, dx_ref, dshift_ref, da_ref, dy_ref,
             dgate_ref, dg_ref):
        xv, dhv = x_ref[...], dh_ref[...]
        r = lax.rsqrt(jnp.mean(xv * xv, axis=-1, keepdims=True) + EPS)
        xhat = xv * r
        dxhat = dhv * a_ref[...]
        dxv = dres_ref[...] + r * (dxhat - xhat * jnp.mean(dxhat * xhat, axis=-1, keepdims=True))
        dx_ref[...] = dxv
        dy, dgate_rows, dg_rows = _post_bwd_math(dxv, y_ref[...], gate_ref[...], g_ref[...])
        dy_ref[...] = dy.astype(BF16)
        _accumulate(pl.program_id(0) == 0, [(dshift_ref, dhv), (da_ref, dhv * xhat), (dgate_ref, dgate_rows),
                                            (dg_ref, dg_rows)])

    return pl.pallas_call(
        body, grid=(s // ts,),
        in_specs=[_row_spec(ts, d), _row_spec(ts, d), _row_spec(ts, d), _vec_spec(d), _row_spec(ts, d),
                  _vec_spec(d), _vec_spec(d)],
        out_specs=[_row_spec(ts, d), _vec_spec(d), _vec_spec(d), _row_spec(ts, d), _vec_spec(d), _vec_spec(d)],
        out_shape=[jax.ShapeDtypeStruct((s, d), F32), jax.ShapeDtypeStruct((1, d), F32),
                   jax.ShapeDtypeStruct((1, d), F32), jax.ShapeDtypeStruct((s, d), BF16),
                   jax.ShapeDtypeStruct((1, d), F32), jax.ShapeDtypeStruct((1, d), F32)],
        compiler_params=_cp("arbitrary"), name=name,
    )(dh, xin, dres, avec, y, gate, g)


def _pre_bwd(dh, xin, dres, avec, *, name, side=None):
    s, d = xin.shape
    ts = min(WIDE_ROW_TILE, s)
    nb = s // ts
    s_in, s_out, s_shapes, s_scratch, s_ops, s_alias = _side_args(side, 4, 3)

    def body(dh_ref, x_ref, dres_ref, a_ref, *rest):
        dx_ref, dshift_ref, da_ref = rest[len(s_in):len(s_in) + 3]
        if side:
            step = pl.program_id(0)
            side.run(rest[:len(s_in)], rest[len(s_in) + 3:len(s_in) + 3 + len(s_out)],
                     rest[len(s_in) + 3 + len(s_out):], step == 0, step == nb - 1)
        xv, dhv = x_ref[...], dh_ref[...]
        r = lax.rsqrt(jnp.mean(xv * xv, axis=-1, keepdims=True) + EPS)
        xhat = xv * r
        dxhat = dhv * a_ref[...]
        dx_ref[...] = dres_ref[...] + r * (dxhat - xhat * jnp.mean(dxhat * xhat, axis=-1, keepdims=True))

        @pl.when(pl.program_id(0) == 0)
        def _():
            dshift_ref[...] = jnp.zeros_like(dshift_ref)
            da_ref[...] = jnp.zeros_like(da_ref)

        dshift_ref[...] += jnp.sum(dhv, axis=0, keepdims=True)
        da_ref[...] += jnp.sum(dhv * xhat, axis=0, keepdims=True)

    return pl.pallas_call(
        body, grid=(nb,),
        in_specs=[_row_spec(ts, d), _row_spec(ts, d), _row_spec(ts, d), _vec_spec(d)] + s_in,
        out_specs=[_row_spec(ts, d), _vec_spec(d), _vec_spec(d)] + s_out,
        out_shape=[jax.ShapeDtypeStruct((s, d), F32), jax.ShapeDtypeStruct((1, d), F32),
                   jax.ShapeDtypeStruct((1, d), F32)] + s_shapes,
        scratch_shapes=s_scratch, input_output_aliases=s_alias,
        compiler_params=_cp("arbitrary"), name=name,
    )(dh, xin, dres, avec, *s_ops)


def _tri(n, strict=False, upper=False):
    r = lax.broadcasted_iota(jnp.int32, (n, n), 0)
    c = lax.broadcasted_iota(jnp.int32, (n, n), 1)
    if upper:
        r, c = c, r
    return ((r > c) if strict else (r >= c)).astype(F32)


def _gates_fwd(ps, bf, w2p, b2, *, name):
    s = ps.shape[0]
    ts = min(GATE_TS, s)

    def body(ps_ref, bf_ref, w_ref, b2_ref, cum_ref, la_ref, carry_ref):
        @pl.when(pl.program_id(0) == 0)
        def _():
            carry_ref[...] = jnp.zeros_like(carry_ref)

        psv = ps_ref[...]
        lf = _log_sigmoid(psv + bf_ref[...])
        cum = _dot_nn(_tri(ts), lf, HIGHEST) + carry_ref[...]
        cum_ref[...] = cum
        carry_ref[...] = cum[ts - 1:ts, :]
        z = _dot_nn(psv, w_ref[...], HIGHEST) + b2_ref[...]
        la_ref[...] = _log_sigmoid(z) * (1.0 / GLA_TEMP)

    return pl.pallas_call(
        body, grid=(s // ts,),
        in_specs=[_row_spec(ts, SMALL_W), _vec_spec(SMALL_W),
                  pl.BlockSpec((SMALL_W, GLA_KW), lambda i: (0, 0)), _vec_spec(GLA_KW)],
        out_specs=[_row_spec(ts, SMALL_W), _row_spec(ts, GLA_KW)],
        out_shape=[jax.ShapeDtypeStruct((s, SMALL_W), F32), jax.ShapeDtypeStruct((s, GLA_KW), F32)],
        scratch_shapes=[pltpu.VMEM((1, SMALL_W), F32)],
        compiler_params=_cp("arbitrary"), name=name,
    )(ps, bf, w2p, b2)


def _gates_bwd(dck, ps, bf, w2p, b2, dla, *, name):
    s = ps.shape[0]
    ts = min(GATE_TS, s)
    nb = s // ts
    rev = lambda i: (nb - 1 - i, 0)

    def body(dck_ref, ps_ref, bf_ref, w_ref, b2_ref, dla_ref, dps_ref, dbf_ref, dw_ref, db2_ref, carry_ref):
        @pl.when(pl.program_id(0) == 0)
        def _():
            carry_ref[...] = jnp.zeros_like(carry_ref)
            dbf_ref[...] = jnp.zeros_like(dbf_ref)
            dw_ref[...] = jnp.zeros_like(dw_ref)
            db2_ref[...] = jnp.zeros_like(db2_ref)

        psv, dckv = ps_ref[...], dck_ref[...]
        dlf = _dot_nn(_tri(ts, upper=True), dckv, HIGHEST) + carry_ref[...]
        carry_ref[...] += jnp.sum(dckv, axis=0, keepdims=True)
        lane = lax.broadcasted_iota(jnp.int32, (ts, SMALL_W), 1)
        dff = jnp.where(lane < FOX_HEADS, dlf * _sigmoid(-(psv + bf_ref[...])), 0.0)
        z = _dot_nn(psv, w_ref[...], HIGHEST) + b2_ref[...]
        dz = dla_ref[...] * _sigmoid(-z) * (1.0 / GLA_TEMP)
        dps_ref[...] = (_dot_nt(dz, w_ref[...], HIGHEST) + dff).astype(BF16)
        dbf_ref[...] += jnp.sum(dff, axis=0, keepdims=True)
        dw_ref[...] += _dot_tn(psv, dz, HIGHEST)
        db2_ref[...] += jnp.sum(dz, axis=0, keepdims=True)

    return pl.pallas_call(
        body, grid=(nb,),
        in_specs=[pl.BlockSpec((ts, SMALL_W), rev), pl.BlockSpec((ts, SMALL_W), rev), _vec_spec(SMALL_W),
                  pl.BlockSpec((SMALL_W, GLA_KW), lambda i: (0, 0)), _vec_spec(GLA_KW),
                  pl.BlockSpec((ts, GLA_KW), rev)],
        out_specs=[pl.BlockSpec((ts, SMALL_W), rev), _vec_spec(SMALL_W),
                   pl.BlockSpec((SMALL_W, GLA_KW), lambda i: (0, 0)), _vec_spec(GLA_KW)],
        out_shape=[jax.ShapeDtypeStruct((s, SMALL_W), BF16), jax.ShapeDtypeStruct((1, SMALL_W), F32),
                   jax.ShapeDtypeStruct((SMALL_W, GLA_KW), F32), jax.ShapeDtypeStruct((1, GLA_KW), F32)],
        scratch_shapes=[pltpu.VMEM((1, SMALL_W), F32)],
        compiler_params=_cp("arbitrary"), name=name,
    )(dck, ps, bf, w2p, b2, dla)


def _hs(h, hd=FOX_HD):
    return slice(h * hd, (h + 1) * hd)


def _fox_fwd(proj, cum_t, g_fox, *, name, side=None):
    s = proj.shape[0]
    tq, tk = min(FOX_TQ, s), min(FOX_TK, s)
    scale = FOX_HD ** -0.5
    n_si = len(side.inputs) if side else 0
    n_so = len(side.out_shapes) if side else 0
    grid = (s // tq, s // tk)

    def body(*refs):
        q_ref, k_ref, v_ref, ck_ref, g_ref = refs[:5]
        o_ref, n_ref, lse_ref = refs[5 + n_si:8 + n_si]
        m_sc, acc_sc = refs[8 + n_si + n_so:10 + n_si + n_so]
        i, j = pl.program_id(0), pl.program_id(1)
        if side:
            side.run(refs[5:5 + n_si], refs[8 + n_si:8 + n_si + n_so], refs[10 + n_si + n_so:],
                     (i == 0) & (j == 0), (i == grid[0] - 1) & (j == grid[1] - 1))

        @pl.when(j == 0)
        def _():
            m_sc[...] = jnp.full_like(m_sc, NEG)
            acc_sc[...] = jnp.zeros_like(acc_sc)

        def block(masked):
            mask = _causal_mask(i, j, tq, tk) if masked else None
            ones = jnp.ones((tk, FOX_HD), BF16)
            for h in range(FOX_HEADS):
                sc = _fox_logits(_dot_nt(q_ref[:, _hs(h)], k_ref[:, _hs(h)]), ck_ref[h:h + 1, :], mask, scale)
                m_prev = m_sc[h]
                m_new = jnp.maximum(m_prev, jnp.max(sc, axis=-1, keepdims=True))
                alpha = jnp.exp(m_prev - m_new)
                p = jnp.exp(sc - m_new).astype(BF16)
                v_one = jnp.concatenate([v_ref[:, _hs(h)], ones], axis=1)
                acc_sc[:, _hs(h, 2 * FOX_HD)] = alpha * acc_sc[:, _hs(h, 2 * FOX_HD)] + _dot_nn(p, v_one)
                m_sc[h] = m_new

        pl.when(j < i)(functools.partial(block, False))

        @pl.when(j == i)
        def _():
            block(True)
            lane = lax.broadcasted_iota(jnp.int32, (tq, 128), 1)
            lse = jnp.zeros((tq, 128), F32)
            for h in range(FOX_HEADS):
                l_rep = acc_sc[:, 2 * h * FOX_HD + FOX_HD:2 * (h + 1) * FOX_HD]
                o = acc_sc[:, 2 * h * FOX_HD:2 * h * FOX_HD + FOX_HD] / l_rep
                o_ref[:, _hs(h)] = o
                r = lax.rsqrt(jnp.mean(o * o, axis=-1, keepdims=True) + EPS)
                n_ref[:, _hs(h)] = (o * r * g_ref[h:h + 1, :]).astype(BF16)
                lse = jnp.where(lane == h, m_sc[h] + jnp.log(l_rep), lse)
            lse_ref[...] = lse

    kv = lambda col: (lambda i, j: (jnp.minimum(j, i), col))
    any_spec = pl.BlockSpec(memory_space=pl.ANY)
    return pl.pallas_call(
        body, grid=grid,
        in_specs=[pl.BlockSpec((tq, FOX_W), lambda i, j: (i, 0)),
                  pl.BlockSpec((tk, FOX_W), kv(1)),
                  pl.BlockSpec((tk, FOX_W), kv(2)),
                  pl.BlockSpec((FOX_HEADS, tk), lambda i, j: (0, jnp.minimum(j, i))),
                  pl.BlockSpec((FOX_HEADS, FOX_HD), lambda i, j: (0, 0))] + [any_spec] * n_si,
        out_specs=[pl.BlockSpec((tq, FOX_W), lambda i, j: (i, 0)),
                   pl.BlockSpec((tq, FOX_W), lambda i, j: (i, 0)),
                   pl.BlockSpec((tq, 128), lambda i, j: (i, 0))] + [any_spec] * n_so,
        out_shape=[jax.ShapeDtypeStruct((s, FOX_W), F32), jax.ShapeDtypeStruct((s, FOX_W), BF16),
                   jax.ShapeDtypeStruct((s, 128), F32)] + (side.out_shapes if side else []),
        scratch_shapes=[pltpu.VMEM((FOX_HEADS, tq, 1), F32), pltpu.VMEM((tq, 2 * FOX_W), F32)]
        + (side.scratch() if side else []),
        compiler_params=_cp("arbitrary", "arbitrary"), name=name,
    )(proj, proj, proj, cum_t, g_fox, *(side.inputs if side else []))


def _causal_mask(i, j, tq, tk):
    rows = i * tq + lax.broadcasted_iota(jnp.int32, (tq, tk), 0)
    cols = j * tk + lax.broadcasted_iota(jnp.int32, (tq, tk), 1)
    return rows >= cols


def _fox_logits(qk, ck, mask, scale):
    sc = qk * scale - ck
    return sc if mask is None else jnp.where(mask, sc, NEG)


def _fox_bwd(proj, do, cum_t, lse, delta, *, name, side=None):
    s = proj.shape[0]
    tq, tk = min(FOX_TQ, s), min(FOX_TK, s)
    nk, nq = s // tk, s // tq
    scale = FOX_HD ** -0.5
    n_si = len(side.inputs) if side else 0
    n_so = len(side.out_shapes) if side else 0

    def body(*refs):
        q_ref, k_ref, v_ref, do_ref, ck_ref, lse_ref, dl_ref = refs[:7]
        dq_hbm, dk_ref, dv_ref, dcq_hbm, dck_ref = refs[7 + n_si:12 + n_si]
        dq_sc, dcq_sc, dk_sc, dv_sc, dck_sc, out_sems = refs[12 + n_si + n_so:18 + n_si + n_so]
        j, i = pl.program_id(0), pl.program_id(1)
        if side:
            side.run(refs[7:7 + n_si], refs[12 + n_si:12 + n_si + n_so], refs[18 + n_si + n_so:],
                     (j == 0) & (i == 0), (j == nk - 1) & (i == nq - 1))

        @pl.when((j == 0) & (i == 0))
        def _():
            dq_sc[...] = jnp.zeros_like(dq_sc)
            dcq_sc[...] = jnp.zeros_like(dcq_sc)

        @pl.when(i == 0)
        def _():
            dk_sc[...] = jnp.zeros_like(dk_sc)
            dv_sc[...] = jnp.zeros_like(dv_sc)
            dck_sc[...] = jnp.zeros_like(dck_sc)

        def block(masked):
            mask = _causal_mask(i, j, tq, tk) if masked else None
            qrows = pl.ds(pl.multiple_of(i * tq, tq), tq)
            for h in range(FOX_HEADS):
                sc = _fox_logits(_dot_nt(q_ref[:, _hs(h)], k_ref[:, _hs(h)]), ck_ref[h:h + 1, :], mask, scale)
                p = jnp.exp(sc - lse_ref[:, h:h + 1])
                ds = p * (_dot_nt(do_ref[:, _hs(h)], v_ref[:, _hs(h)]) - dl_ref[:, h:h + 1])
                dsb = ds.astype(BF16)
                dv_sc[:, _hs(h)] += _dot_tn(p.astype(BF16), do_ref[:, _hs(h)])
                dk_sc[:, _hs(h)] += _dot_tn(dsb, q_ref[:, _hs(h)])
                dq_sc[qrows, _hs(h)] += _dot_nn(dsb, k_ref[:, _hs(h)]) * scale
                dck_sc[h:h + 1, :] -= jnp.sum(ds, axis=0, keepdims=True)
                dcq_sc[qrows, h:h + 1] += jnp.sum(ds, axis=-1, keepdims=True)

        pl.when(i > j)(functools.partial(block, False))
        pl.when(i == j)(functools.partial(block, True))

        @pl.when(i == nq - 1)
        def _():
            dk_ref[...] = (dk_sc[...] * scale).astype(BF16)
            dv_ref[...] = dv_sc[...].astype(BF16)
            dck_ref[...] = dck_sc[...]

        @pl.when((j == nk - 1) & (i == nq - 1))
        def _():
            out_q = pltpu.make_async_copy(dq_sc, dq_hbm, out_sems.at[0])
            out_c = pltpu.make_async_copy(dcq_sc, dcq_hbm, out_sems.at[1])
            out_q.start()
            out_c.start()
            out_q.wait()
            out_c.wait()

    qrow = lambda j, i: (jnp.maximum(i, j), 0)
    krow = lambda col: (lambda j, i: (j, col))
    any_spec = pl.BlockSpec(memory_space=pl.ANY)
    return pl.pallas_call(
        body, grid=(nk, nq),
        in_specs=[pl.BlockSpec((tq, FOX_W), qrow), pl.BlockSpec((tk, FOX_W), krow(1)),
                  pl.BlockSpec((tk, FOX_W), krow(2)),
                  pl.BlockSpec((tq, FOX_W), qrow),
                  pl.BlockSpec((FOX_HEADS, tk), lambda j, i: (0, j)),
                  pl.BlockSpec((tq, 128), qrow), pl.BlockSpec((tq, 128), qrow)] + [any_spec] * n_si,
        out_specs=[any_spec, pl.BlockSpec((tk, FOX_W), lambda j, i: (j, 0)),
                   pl.BlockSpec((tk, FOX_W), lambda j, i: (j, 0)), any_spec,
                   pl.BlockSpec((FOX_HEADS, tk), lambda j, i: (0, j))] + [any_spec] * n_so,
        out_shape=[jax.ShapeDtypeStruct((s, FOX_W), F32), jax.ShapeDtypeStruct((s, FOX_W), BF16),
                   jax.ShapeDtypeStruct((s, FOX_W), BF16), jax.ShapeDtypeStruct((s, 128), F32),
                   jax.ShapeDtypeStruct((FOX_HEADS, s), F32)] + (side.out_shapes if side else []),
        scratch_shapes=[pltpu.VMEM((s, FOX_W), F32), pltpu.VMEM((s, 128), F32),
                        pltpu.VMEM((tk, FOX_W), F32), pltpu.VMEM((tk, FOX_W), F32), pltpu.VMEM((FOX_HEADS, tk), F32),
                        pltpu.SemaphoreType.DMA((2,))] + (side.scratch() if side else []),
        compiler_params=_cp("arbitrary", "arbitrary"), name=name,
    )(proj, proj, proj, do, cum_t, lse, delta, *(side.inputs if side else []))


def _head_norm_bwd(dn_in, o, g, gr_src, *, nh, hd, dn_col, gr_col, name):
    s, w = o.shape
    ts = min(ROW_TILE, s)
    gated = gr_src is not None

    def body(*refs):
        if gated:
            dn_ref, o_ref, g_ref, gr_ref, do_ref, dgr_ref, dl_ref, dg_ref = refs
        else:
            dn_ref, o_ref, g_ref, do_ref, dl_ref, dg_ref = refs

        @pl.when(pl.program_id(0) == 0)
        def _():
            dg_ref[...] = jnp.zeros_like(dg_ref)

        lane = lax.broadcasted_iota(jnp.int32, (ts, 128), 1)
        delta = jnp.zeros((ts, 128), F32)
        for h in range(nh):
            sl = _hs(h, hd)
            ov = o_ref[:, sl]
            dnv = dn_ref[:, sl].astype(F32)
            gv = g_ref[h:h + 1, :]
            r = lax.rsqrt(jnp.mean(ov * ov, axis=-1, keepdims=True) + EPS)
            ohat = ov * r
            if gated:
                grv = gr_ref[:, sl].astype(F32)
                sig = _sigmoid(grv)
                dgr_ref[:, sl] = (dnv * (ohat * gv) * (sig * (1.0 + grv * (1.0 - sig)))).astype(BF16)
                dnv = dnv * (grv * sig)
            dg_ref[h:h + 1, :] += jnp.sum(dnv * ohat, axis=0, keepdims=True)
            dohat = dnv * gv
            do = r * (dohat - ohat * jnp.mean(dohat * ohat, axis=-1, keepdims=True))
            do_ref[:, sl] = do.astype(BF16)
            delta = jnp.where(lane == h, jnp.sum(do.astype(BF16).astype(F32) * ov, axis=-1, keepdims=True), delta)
        dl_ref[...] = delta

    in_specs = [pl.BlockSpec((ts, w), lambda i: (i, dn_col)), _row_spec(ts, w),
                pl.BlockSpec((nh, hd), lambda i: (0, 0))]
    args = [dn_in, o, g]
    out_specs = [_row_spec(ts, w)]
    out_shape = [jax.ShapeDtypeStruct((s, w), BF16)]
    if gated:
        in_specs.append(pl.BlockSpec((ts, w), lambda i: (i, gr_col)))
        args.append(gr_src)
        out_specs.append(_row_spec(ts, w))
        out_shape.append(jax.ShapeDtypeStruct((s, w), BF16))
    out_specs += [_row_spec(ts, 128), pl.BlockSpec((nh, hd), lambda i: (0, 0))]
    out_shape += [jax.ShapeDtypeStruct((s, 128), F32), jax.ShapeDtypeStruct((nh, hd), F32)]
    return pl.pallas_call(
        body, grid=(s // ts,), in_specs=in_specs, out_specs=out_specs, out_shape=out_shape,
        compiler_params=_cp("arbitrary"), name=name,
    )(*args)


GQ_BLK = 3 * FOX_W // GLA_DK
GK_BLK = GQ_BLK + GLA_HEADS
GV_BLK = (3 * FOX_W + 2 * GLA_KW) // GLA_DV
GR_BLK = GV_BLK + GLA_HEADS


def _gla_chunk_terms(la):
    cum = _dot_nn(_tri(CHUNK), la, HIGHEST)
    total = cum[CHUNK - 1:CHUNK, :]
    return jnp.exp(total - cum), jnp.exp(total)


def _gla_fwd(proj, log_a, g_gla, *, name):
    s = proj.shape[0]
    rows = min(GLA_ROWS, s)
    cb = rows // CHUNK
    nblk = s // rows
    scale = GLA_DK ** -0.5

    def body(q_ref, k_ref, v_ref, gr_ref, la_ref, g_ref, o_ref, n_ref, st_ref, st_sc):
        h = pl.program_id(0)

        @pl.when(pl.program_id(1) == 0)
        def _():
            st_sc[...] = jnp.zeros_like(st_sc)

        gv = g_ref[pl.ds(h, 1), :]
        for ci in range(cb):
            sl = slice(ci * CHUNK, (ci + 1) * CHUNK)
            e, dec = _gla_chunk_terms(la_ref[sl, :])
            k_dec = (k_ref[sl, :].astype(F32) * e).astype(BF16)
            st = st_sc[...] * dec + _dot_tn(v_ref[sl, :], k_dec)
            st_sc[...] = st
            st_ref[0, ci] = st
            qs = (q_ref[sl, :].astype(F32) * scale).astype(BF16)
            o = _dot_nt(qs, st.astype(BF16))
            o_ref[sl, :] = o
            r = lax.rsqrt(jnp.mean(o * o, axis=-1, keepdims=True) + EPS)
            grv = gr_ref[sl, :].astype(F32)
            n_ref[sl, :] = (o * r * gv * (grv * _sigmoid(grv))).astype(BF16)

    return pl.pallas_call(
        body, grid=(GLA_HEADS, nblk),
        in_specs=[pl.BlockSpec((rows, GLA_DK), lambda h, n: (n, GQ_BLK + h)),
                  pl.BlockSpec((rows, GLA_DK), lambda h, n: (n, GK_BLK + h)),
                  pl.BlockSpec((rows, GLA_DV), lambda h, n: (n, GV_BLK + h)),
                  pl.BlockSpec((rows, GLA_DV), lambda h, n: (n, GR_BLK + h)),
                  pl.BlockSpec((rows, GLA_DK), lambda h, n: (n, h)),
                  pl.BlockSpec((GLA_HEADS, GLA_DV), lambda h, n: (0, 0))],
        out_specs=[pl.BlockSpec((rows, GLA_DV), lambda h, n: (n, h)),
                   pl.BlockSpec((rows, GLA_DV), lambda h, n: (n, h)),
                   pl.BlockSpec((1, cb, GLA_DV, GLA_DK), lambda h, n: (h, n, 0, 0))],
        out_shape=[jax.ShapeDtypeStruct((s, GLA_W), F32), jax.ShapeDtypeStruct((s, GLA_W), BF16),
                   jax.ShapeDtypeStruct((GLA_HEADS, s // CHUNK, GLA_DV, GLA_DK), F32)],
        scratch_shapes=[pltpu.VMEM((GLA_DV, GLA_DK), F32)],
        compiler_params=_cp("parallel", "arbitrary"), name=name,
    )(proj, proj, proj, proj, log_a, g_gla)


def _gla_bwd(proj, log_a, do, states, *, name):
    s = proj.shape[0]
    rows = min(GLA_ROWS, s)
    cb = rows // CHUNK
    nblk = s // rows
    scale = GLA_DK ** -0.5

    def body(q_ref, k_ref, v_ref, la_ref, do_ref, st_ref, prev_ref, dq_ref, dk_ref, dv_ref, dla_ref, g_sc):
        nrev = pl.program_id(1)
        blk = nblk - 1 - nrev

        @pl.when(nrev == 0)
        def _():
            g_sc[...] = jnp.zeros_like(g_sc)

        for ci in reversed(range(cb)):
            sl = slice(ci * CHUNK, (ci + 1) * CHUNK)
            e, dec = _gla_chunk_terms(la_ref[sl, :])
            kd = k_ref[sl, :].astype(F32) * e
            qs = (q_ref[sl, :].astype(F32) * scale).astype(BF16)
            dov = do_ref[sl, :]
            st = st_ref[0, ci]
            if ci > 0:
                st_prev = st_ref[0, ci - 1]
            else:
                st_prev = prev_ref[0, 0] * (blk > 0).astype(F32)
            dq_ref[sl, :] = (_dot_nn(dov, st.astype(BF16)) * scale).astype(BF16)
            gt = g_sc[...] + _dot_tn(dov, qs)
            gtb = gt.astype(BF16)
            dkd = _dot_nn(v_ref[sl, :], gtb)
            dv_ref[sl, :] = _dot_nt(kd.astype(BF16), gtb).astype(BF16)
            dk_ref[sl, :] = (dkd * e).astype(BF16)
            ddec = jnp.sum(gt * st_prev, axis=0, keepdims=True) * dec
            dla_ref[sl, :] = _dot_nn(_tri(CHUNK, strict=True), dkd * kd, HIGHEST) + ddec
            g_sc[...] = gt * dec

    rev = lambda col0: (lambda h, n: (nblk - 1 - n, col0 + h))
    return pl.pallas_call(
        body, grid=(GLA_HEADS, nblk),
        in_specs=[pl.BlockSpec((rows, GLA_DK), rev(GQ_BLK)),
                  pl.BlockSpec((rows, GLA_DK), rev(GK_BLK)),
                  pl.BlockSpec((rows, GLA_DV), rev(GV_BLK)),
                  pl.BlockSpec((rows, GLA_DK), rev(0)),
                  pl.BlockSpec((rows, GLA_DV), rev(0)),
                  pl.BlockSpec((1, cb, GLA_DV, GLA_DK), lambda h, n: (h, nblk - 1 - n, 0, 0)),
                  pl.BlockSpec((1, 1, GLA_DV, GLA_DK),
                               lambda h, n: (h, jnp.maximum((nblk - 1 - n) * cb - 1, 0), 0, 0))],
        out_specs=[pl.BlockSpec((rows, GLA_DK), rev(0)), pl.BlockSpec((rows, GLA_DK), rev(0)),
                   pl.BlockSpec((rows, GLA_DV), rev(0)), pl.BlockSpec((rows, GLA_DK), rev(0))],
        out_shape=[jax.ShapeDtypeStruct((s, GLA_KW), BF16), jax.ShapeDtypeStruct((s, GLA_KW), BF16),
                   jax.ShapeDtypeStruct((s, GLA_W), BF16), jax.ShapeDtypeStruct((s, GLA_KW), F32)],
        scratch_shapes=[pltpu.VMEM((GLA_DV, GLA_DK), F32)],
        compiler_params=_cp("parallel", "arbitrary"), name=name,
    )(proj, proj, proj, log_a, do, states, states)


def _row_tile(r):
    tr = min(ROW_TILE, r)
    while r % tr or tr % 8:
        tr -= 1
    return tr


def _adamw_math(w, g, m, v):
    m = ADAM_B1 * m + (1.0 - ADAM_B1) * g
    v = ADAM_B2 * v + (1.0 - ADAM_B2) * (g * g)
    m_hat = m / (1.0 - ADAM_B1 ** ADAM_STEP)
    v_hat = v / (1.0 - ADAM_B2 ** ADAM_STEP)
    delta = -ADAM_LR * (m_hat / (jnp.sqrt(v_hat) + ADAM_EPS) + ADAM_WD * w)
    return delta, m, v


COL_TILE = 256


def _tile_2d(r, c):
    if r % 8 == 0 and _row_tile(r) >= 64:
        return _row_tile(r), c
    assert c % COL_TILE == 0, (r, c)
    return r, COL_TILE


def _half_shape(shape):
    r, c = shape[-2:]
    return tuple(shape[:-2]) + ((r // 2, c) if _half_axis(r) == 0 else (r, c // 2))


def _adam(g, w, m, v, *, name):
    r, c = w.shape
    tr, tc = _tile_2d(r, c)

    def body(g_ref, w_ref, m_ref, v_ref, d_ref, mo_ref, vo_ref):
        d, mn, vn = _adamw_math(w_ref[...], g_ref[...], m_ref[...], v_ref[...])
        d_ref[...] = d
        mo_ref[...] = mn
        vo_ref[...] = vn

    spec = pl.BlockSpec((tr, tc), lambda i, j: (i, j))
    return pl.pallas_call(
        body, grid=(r // tr, c // tc), in_specs=[spec] * 4, out_specs=[spec] * 3,
        out_shape=[jax.ShapeDtypeStruct((r, c), F32)] * 3,
        compiler_params=_cp("parallel", "parallel"), name=name,
    )(g, w, m, v)


def _ada_grad_adam(c_all_t, dmod_cols, w, m, v, *, name):
    r, c = w.shape
    tr, tc = min(512, r), min(1024, c)

    def body(ct_ref, dm_ref, w_ref, m_ref, v_ref, g_ref, d_ref, mo_ref, vo_ref):
        g = _dot_nn(ct_ref[...], dm_ref[...], HIGHEST)
        g_ref[...] = g
        d, mn, vn = _adamw_math(w_ref[...], g, m_ref[...], v_ref[...])
        d_ref[...] = d
        mo_ref[...] = mn
        vo_ref[...] = vn

    spec = pl.BlockSpec((tr, tc), lambda i, j: (i, j))
    nb = c_all_t.shape[1]
    return pl.pallas_call(
        body, grid=(r // tr, c // tc),
        in_specs=[pl.BlockSpec((tr, nb), lambda i, j: (i, 0)), pl.BlockSpec((nb, tc), lambda i, j: (0, j)),
                  spec, spec, spec],
        out_specs=[spec] * 4, out_shape=[jax.ShapeDtypeStruct((r, c), F32)] * 4,
        compiler_params=_cp("parallel", "parallel"), name=name,
    )(c_all_t, dmod_cols, w, m, v)


def _mod_shard(c_all, w, b, *, name):
    k, c = w.shape
    tc = min(512, c)
    nb = c_all.shape[0]

    def body(c_ref, w_ref, b_ref, o_ref):
        o_ref[...] = _dot_nn(c_ref[...], w_ref[...], HIGHEST) + b_ref[...]

    return pl.pallas_call(
        body, grid=(c // tc,),
        in_specs=[pl.BlockSpec((nb, k), lambda j: (0, 0)), pl.BlockSpec((k, tc), lambda j: (0, j)),
                  pl.BlockSpec((1, tc), lambda j: (0, j))],
        out_specs=pl.BlockSpec((nb, tc), lambda j: (0, j)),
        out_shape=jax.ShapeDtypeStruct((nb, c), F32),
        compiler_params=_cp("parallel"), name=name,
    )(c_all, w, b)


def _silu_rows(c, *, name):
    def body(c_ref, o_ref):
        cv = c_ref[...]
        o_ref[...] = cv * _sigmoid(cv)

    return pl.pallas_call(body, out_shape=jax.ShapeDtypeStruct(c.shape, F32), name=name)(c)


def _pair_sum(g, got, idx, *, name):
    p, r, c = g.shape
    ax = _half_axis(r)
    hr, hc = _half_shape((r, c))
    tr, tc = _tile_2d(hr, hc)
    nbr, nbc = hr // tr, hc // tc

    def body(idx_ref, a_ref, b_ref, o_ref):
        o_ref[...] = (a_ref[...].astype(F32) + b_ref[...].astype(F32)).astype(BF16)

    def slot(i, idx_ref):
        return i + jnp.where(i >= idx_ref[1], 1, 0)

    def own_map(i, j, k, idx_ref):
        return (slot(i, idx_ref), j + (idx_ref[0] * nbr if ax == 0 else 0), k + (idx_ref[0] * nbc if ax == 1 else 0))

    half_spec = pl.BlockSpec((1, tr, tc), lambda i, j, k, idx_ref: (slot(i, idx_ref), j, k))
    return pl.pallas_call(
        body,
        grid_spec=pltpu.PrefetchScalarGridSpec(
            num_scalar_prefetch=1, grid=(p - 1, nbr, nbc),
            in_specs=[pl.BlockSpec((1, tr, tc), own_map), half_spec],
            out_specs=half_spec),
        out_shape=jax.ShapeDtypeStruct((p, hr, hc), BF16),
        compiler_params=_cp("parallel", "parallel", "parallel"), name=name,
    )(idx, g, got)


def _final_sum(g, got, parts, idx, *, name):
    shard_shape = g.shape[1:]
    ax = _half_axis(shard_shape[0])
    hr, hc = got.shape[1:]
    tr, tc = _tile_2d(hr, hc)
    nbr, nbc = hr // tr, hc // tc

    def body(idx_ref, g_ref, got_ref, parts_ref, o_ref):
        acc = g_ref[0].astype(F32) + got_ref[0].astype(F32)
        for q in range(3):
            acc = acc + parts_ref[q].astype(F32)
        o_ref[...] = acc

    def half_c(j, k, idx_ref):
        return (j + (idx_ref[0] * nbr if ax == 0 else 0), k + (idx_ref[0] * nbc if ax == 1 else 0))

    return pl.pallas_call(
        body,
        grid_spec=pltpu.PrefetchScalarGridSpec(
            num_scalar_prefetch=1, grid=(nbr, nbc),
            in_specs=[pl.BlockSpec((1, tr, tc), lambda j, k, idx_ref: (idx_ref[1],) + half_c(j, k, idx_ref)),
                      pl.BlockSpec((1, tr, tc), lambda j, k, idx_ref: (idx_ref[1], j, k)),
                      pl.BlockSpec((3, tr, tc), lambda j, k, idx_ref: (0, j, k))],
            out_specs=pl.BlockSpec((tr, tc), half_c)),
        out_shape=jax.ShapeDtypeStruct(tuple(shard_shape), F32),
        compiler_params=_cp("parallel", "parallel"), name=name,
    )(idx, g, got, parts)


def _stack_sum(x, *, name):
    p, r, c = x.shape
    tr = _row_tile(r)

    def body(x_ref, o_ref):
        acc = x_ref[0].astype(F32)
        for q in range(1, p):
            acc = acc + x_ref[q].astype(F32)
        o_ref[...] = acc

    return pl.pallas_call(
        body, grid=(r // tr,),
        in_specs=[pl.BlockSpec((p, tr, c), lambda i: (0, i, 0))],
        out_specs=pl.BlockSpec((tr, c), lambda i: (i, 0)),
        out_shape=jax.ShapeDtypeStruct((r, c), F32),
        compiler_params=_cp("parallel"), name=name,
    )(x)


def _place():
    x, y, c = lax.axis_index("x"), lax.axis_index("y"), lax.axis_index("c")
    chips = [(1 - x, y), (x, 1 - y), (1 - x, 1 - y)]
    return x, y, c, chips


def _gather8(x_shard, *, name):
    m_per, n = x_shard.shape

    def body(x_ref, out_ref, send_sems, recv_sems, local_sem):
        x, y, c, chips = _place()
        me, sibling = (x, y, c), (x, y, 1 - c)

        def rows(px, py, pc):
            return out_ref.at[pl.ds((4 * px + 2 * py + pc) * m_per, m_per), :]

        def copy(k, block, to, src=None):
            return pltpu.make_async_remote_copy(
                src_ref=rows(*block) if src is None else src, dst_ref=rows(*block),
                send_sem=send_sems.at[k], recv_sem=recv_sems.at[k], device_id=to, device_id_type=MESH)

        mine = pltpu.make_async_copy(x_ref, rows(*me), local_sem)
        mine.start()
        first = [copy(0, me, sibling, src=x_ref)]
        first += [copy(1 + j, me, (*chip, c), src=x_ref) for j, chip in enumerate(chips)]
        for cp in first:
            cp.start()
        passed = [copy(4 + j, (*chip, c), sibling) for j, chip in enumerate(chips)]
        for j, chip in enumerate(chips):
            copy(1 + j, (*chip, c), me).wait_recv()
            passed[j].start()
        copy(0, sibling, me).wait_recv()
        for j, chip in enumerate(chips):
            copy(4 + j, (*chip, 1 - c), me).wait_recv()
        for cp in first + passed:
            cp.wait_send()
        mine.wait()

    return pl.pallas_call(
        body,
        out_shape=jax.ShapeDtypeStruct((8 * m_per, n), x_shard.dtype),
        in_specs=[pl.BlockSpec(memory_space=pltpu.VMEM)],
        out_specs=pl.BlockSpec(memory_space=pltpu.VMEM),
        scratch_shapes=[pltpu.SemaphoreType.DMA((7,)), pltpu.SemaphoreType.DMA((7,)), pltpu.SemaphoreType.DMA],
        name=name,
    )(x_shard)


def _gather_relayed(shard, *, name):
    def body(shard_ref, full_ref, send_sems, recv_sems):
        x, y, c, _ = _place()
        ax = _half_axis(shard_ref.shape[0])
        me, xn, yn, dg = 2 * x + y, 2 * (1 - x) + y, 2 * x + (1 - y), 2 * (1 - x) + (1 - y)
        to_x, to_y = (1 - x, y, c), (x, 1 - y, c)

        def half(slot, piece=None):
            return _rows_half(full_ref.at[slot], c, ax, piece)

        def copy(k, src, dst, peer):
            return pltpu.make_async_remote_copy(src_ref=src, dst_ref=dst, send_sem=send_sems.at[k],
                                                recv_sem=recv_sems.at[k], device_id=peer, device_id_type=MESH)

        mine = _rows_half(shard_ref, c, ax)
        sends = [copy(0, mine, half(me), to_x), copy(1, mine, half(me), to_y)]
        for cp in sends:
            cp.start()
        copy(0, mine, half(xn), to_x).wait_recv()
        relay_y = copy(2, half(xn, (0, 2)), half(xn, (0, 2)), to_y)
        relay_y.start()
        copy(1, mine, half(yn), to_y).wait_recv()
        relay_x = copy(3, half(yn, (1, 2)), half(yn, (1, 2)), to_x)
        relay_x.start()
        copy(2, half(xn, (0, 2)), half(dg, (0, 2)), to_y).wait_recv()
        copy(3, half(yn, (1, 2)), half(dg, (1, 2)), to_x).wait_recv()
        for cp in sends + [relay_y, relay_x]:
            cp.wait_send()

    any_spec = pl.BlockSpec(memory_space=pl.ANY)
    return pl.pallas_call(
        body, out_shape=jax.ShapeDtypeStruct((4,) + shard.shape, shard.dtype),
        in_specs=[any_spec], out_specs=any_spec,
        scratch_shapes=[pltpu.SemaphoreType.DMA((4,)), pltpu.SemaphoreType.DMA((4,))], name=name,
    )(shard)


def _plan_start(plan, send_sems, recv_sems):
    for k, (src, dst, _, peer) in enumerate(plan):
        pltpu.make_async_remote_copy(src_ref=src, dst_ref=dst, send_sem=send_sems.at[k], recv_sem=recv_sems.at[k],
                                     device_id=peer, device_id_type=MESH).start()


def _plan_wait(plan, send_sems, recv_sems):
    for k, (src, _, land, peer) in enumerate(plan):
        pltpu.make_async_remote_copy(src_ref=src, dst_ref=land, send_sem=send_sems.at[k], recv_sem=recv_sems.at[k],
                                     device_id=peer, device_id_type=MESH).wait_recv()
    for k, (src, dst, _, peer) in enumerate(plan):
        pltpu.make_async_remote_copy(src_ref=src, dst_ref=dst, send_sem=send_sems.at[k], recv_sem=recv_sems.at[k],
                                     device_id=peer, device_id_type=MESH).wait_send()


def _half_axis(rows):
    return 0 if rows % 32 == 0 else 1


def _rows_half(ref, hc, axis, part=None):
    size = ref.shape[axis] // 2
    start = hc * size
    if part is not None:
        size //= part[1]
        start = start + part[0] * size
    idx = [slice(None)] * len(ref.shape)
    idx[axis] = pl.ds(start, size)
    return ref.at[tuple(idx)]


def _plan_gather_ici(shard, full, part=None):
    x, y, c, chips = _place()
    ax = _half_axis(shard.shape[0])
    src = _rows_half(shard, c, ax, part)
    return [(src, _rows_half(full.at[2 * x + y], c, ax, part), _rows_half(full.at[2 * cx + cy], c, ax, part),
             (cx, cy, c)) for cx, cy in chips]


def _plan_gather_d2d(full, own):
    x, y, c, chips = _place()
    ax = _half_axis(full.shape[1])
    plan = []
    for cx, cy in chips:
        slot = full.at[2 * cx + cy]
        plan.append((_rows_half(slot, c, ax), _rows_half(slot, c, ax), _rows_half(slot, 1 - c, ax), (x, y, 1 - c)))
    mine = full.at[2 * x + y]
    plan.append((own, mine, mine, (x, y, 1 - c)))
    return plan


def _plan_pair(grad, got):
    x, y, c, _ = _place()
    return [(_rows_half(grad, 1 - c, 1 + _half_axis(grad.shape[1])), got, got, (x, y, 1 - c))]


def _plan_shard_ici(sums, parts, piece=None):
    _, _, c, chips = _place()

    def rows(ref):
        if piece is None:
            return ref
        k, n = piece
        if ref.shape[0] % (16 * n) == 0:
            size = ref.shape[0] // n
            return ref.at[pl.ds(k * size, size), :]
        size = ref.shape[1] // n
        return ref.at[:, pl.ds(k * size, size)]

    return [(rows(sums.at[2 * cx + cy]), rows(parts.at[k]), rows(parts.at[k]), (cx, cy, c))
            for k, (cx, cy) in enumerate(chips)]


def _plan_half(buf):
    x, y, c, _ = _place()
    ax = _half_axis(buf.shape[0])
    mine = _rows_half(buf, c, ax)
    return [(mine, mine, _rows_half(buf, 1 - c, ax), (x, y, 1 - c))]


def _comm_call(plan_fn, inputs, out_shapes, *, name, aliases=None):
    ni, no = len(inputs), len(out_shapes)

    def body(*refs):
        plan = plan_fn(refs[:ni], refs[ni:ni + no])
        send_sems, recv_sems = refs[ni + no:]
        _plan_start(plan, send_sems, recv_sems)
        _plan_wait(plan, send_sems, recv_sems)

    any_spec = pl.BlockSpec(memory_space=pl.ANY)
    n_copies = 3 * max(ni, no)
    return pl.pallas_call(
        body, out_shape=list(out_shapes), in_specs=[any_spec] * ni, out_specs=[any_spec] * no,
        scratch_shapes=[pltpu.SemaphoreType.DMA((n_copies,)), pltpu.SemaphoreType.DMA((n_copies,))],
        input_output_aliases=aliases or {}, name=name,
    )(*inputs)


def _gather_forward(full, own, *, name):
    return _comm_call(lambda ins, outs: _plan_gather_d2d(outs[0], ins[1]), [full, own],
                      [jax.ShapeDtypeStruct(full.shape, full.dtype)], name=name, aliases={0: 0})[0]


def _half_exchange(bufs, *, name):
    return _comm_call(lambda ins, outs: [cp for o in outs for cp in _plan_half(o)],
                      bufs, [jax.ShapeDtypeStruct(b.shape, b.dtype) for b in bufs], name=name,
                      aliases={k: k for k in range(len(bufs))})


def _split_w_in(w_in_t):
    d = w_in_t.shape[1]
    main = jnp.concatenate([w_in_t[0:3072], w_in_t[3080:5128], w_in_t[5144:6168]], axis=0)
    small = jnp.concatenate([w_in_t[3072:3080], w_in_t[5128:5144], jnp.zeros((SMALL_W - 24, d), w_in_t.dtype)], axis=0)
    return main, small


def _merge_dw_in(dw_main, dw_small):
    return jnp.concatenate([dw_main[0:3072], dw_small[0:8], dw_main[3072:5120], dw_small[8:24], dw_main[5120:6144]],
                           axis=0)


def _gather_side(shards):
    return _Side(shards, [jax.ShapeDtypeStruct((4,) + w.shape, w.dtype) for w in shards],
                 lambda ins, outs: [cp for i, o in zip(ins, outs) for cp in _plan_gather_ici(i, o)], 3 * len(shards))


def _forward_side(full, own):
    return _Side([full, own], [jax.ShapeDtypeStruct(full.shape, full.dtype)],
                 lambda ins, outs: _plan_gather_d2d(outs[0], ins[1]), 4, aliases={0: 0})


def _half_side(bufs):
    return _Side(bufs, [jax.ShapeDtypeStruct(b.shape, b.dtype) for b in bufs],
                 lambda ins, outs: [cp for o in outs for cp in _plan_half(o)], len(bufs),
                 aliases={k: k for k in range(len(bufs))})


def _parts_shape(sums):
    return jax.ShapeDtypeStruct((3,) + sums.shape[1:], sums.dtype)


def _got_shape(grad):
    return jax.ShapeDtypeStruct(_half_shape(grad.shape), grad.dtype)


def _pair_side(grad):
    return _Side([grad], [_got_shape(grad)], lambda ins, outs: _plan_pair(ins[0], outs[0]), 1)


def _local_step(x, target, mod, g_pre_mix, g_post_mix, g_pre_mlp, g_post_mlp, gw_in, b_fgate, w_gla_a2,
                b_gla_a2, g_fox, g_gla, own_w_in, own_w_out, own_w_mlp_in, own_w_mlp_out, idx):
    s, d = x.shape
    shift_m, scale_m, gate_m, shift_f, scale_f, gate_f = [mod[:, i * d:(i + 1) * d] for i in range(6)]
    a1 = g_pre_mix * (1.0 + scale_m)
    a2 = g_pre_mlp * (1.0 + scale_f)
    bf = jnp.concatenate([b_fgate, jnp.zeros((1, SMALL_W - FOX_HEADS), F32)], axis=1)
    w2p = jnp.zeros((SMALL_W, GLA_KW), F32).at[FOX_HEADS:FOX_HEADS + GLA_RANK].set(w_gla_a2)

    h1, gw_in = _pre_fwd(x, a1, shift_m, name="pre_mix_fwd", side=_forward_side(gw_in, own_w_in))
    w_in_t = gw_in.reshape(-1, d)
    w_main, w_small = _split_w_in(w_in_t)
    full_shape = lambda w: jax.ShapeDtypeStruct((4,) + w.shape, w.dtype)
    first_side = _Side(
        [own_w_out, own_w_mlp_out], [full_shape(own_w_out), full_shape(own_w_mlp_out)],
        lambda ins, outs: _plan_gather_ici(ins[0], outs[0]) + _plan_gather_ici(ins[1], outs[1], part=(0, 4)), 6)
    proj, gw_out, gw_mlp_out = _mm(h1, w_main, mode="nt", out_dtypes=[BF16], name="in_proj_main", side=first_side)
    ps, gw_out = _mm(h1, w_small, mode="nt", out_dtypes=[F32], name="in_proj_small",
                     side=_forward_side(gw_out, own_w_out))
    w_out_full = gw_out.reshape(-1, d)
    cum, log_a = _gates_fwd(ps, bf, w2p, b_gla_a2, name="gates_fwd")
    cum_t = cum[:, :FOX_HEADS].T
    o_fox, fox_n, lse, gw_mlp_in = _fox_fwd(proj, cum_t, g_fox, name="fox_fwd", side=_gather_side([own_w_mlp_in]))
    o_gla, gla_n, states = _gla_fwd(proj, log_a, g_gla, name="gla_fwd")
    mixed = jnp.concatenate([fox_n, gla_n], axis=1)
    y1, gw_mlp_in = _mm(mixed, w_out_full, mode="nn", out_dtypes=[F32], name="out_proj",
                        side=_forward_side(gw_mlp_in, own_w_mlp_in))
    x1, h2 = _post_pre_fwd(x, y1, gate_m, g_post_mix, a2, shift_f, name="post_mix_pre_mlp_fwd")

    def mlp_act(acc):
        r = jnp.maximum(acc, 0.0)
        return acc, r * r

    rest_side = _Side([own_w_mlp_out, gw_mlp_out], [full_shape(own_w_mlp_out)],
                      lambda ins, outs: [cp for q in (1, 2, 3) for cp in _plan_gather_ici(ins[0], outs[0], part=(q, 4))],
                      9, aliases={1: 0})
    u, act, gw_mlp_out = _mm(h2, gw_mlp_in, mode="nn", out_dtypes=[BF16, BF16], epi=mlp_act, name="mlp_in",
                             b_slots=4, tm=MM_TM, side=rest_side)
    gw_mlp_out = _gather_forward(gw_mlp_out, own_w_mlp_out, name="gather_w_mlp_out_d2d")
    w_mlp_out_full = gw_mlp_out.reshape(-1, d)
    y2, = _mm(act, w_mlp_out_full, mode="nn", out_dtypes=[F32], name="mlp_out")
    dx2, dy2, loss_part, dgate_f, dg_post_mlp = _post_loss_bwd(x1, y2, gate_f, g_post_mlp, target,
                                                               name="post_mlp_loss_bwd")
    dw_mlp_out, = _mm(act, dy2, mode="tn", out_dtypes=[BF16], name="dw_mlp_out")
    dw_mlp_out = dw_mlp_out.reshape(4, D_FF // 4, d)

    def act_bwd(acc, uv):
        return (acc * (2.0 * jnp.maximum(uv.astype(F32), 0.0)),)

    du, got_mlp_out = _mm(dy2, w_mlp_out_full, mode="nt", out_dtypes=[BF16], extras=[u], epi=act_bwd,
                          name="d_mlp_hidden", tm=MM_TM, side=_pair_side(dw_mlp_out))
    sum_mlp_out = _pair_sum(dw_mlp_out, got_mlp_out, idx, name="grad_pair_sum_mlp_out")
    nj = D_FF // 4 // min(MM_T, D_FF // 4)
    tmw = min(MM_T, d)
    dw_mlp_in, parts_mlp_out = _mm(
        h2, du, mode="tn", out_dtypes=[BF16], name="dw_mlp_in",
        out_shapes=[jax.ShapeDtypeStruct((4, d, D_FF // 4), BF16)],
        out_specs=[pl.BlockSpec((1, tmw, min(MM_T, D_FF // 4)), lambda i, j, kk: (j // nj, i, j % nj))],
        side=_Side([sum_mlp_out], [_parts_shape(sum_mlp_out)],
                   lambda ins, outs: _plan_shard_ici(ins[0], outs[0], piece=(0, 2)), 3))
    dh2, got_mlp_in, parts_mlp_out = _mm(
        du, gw_mlp_in, mode="nt", out_dtypes=[F32], name="d_mlp_in", b_slots=4,
        side=_Side([dw_mlp_in, sum_mlp_out, parts_mlp_out], [_got_shape(dw_mlp_in), _parts_shape(sum_mlp_out)],
                   lambda ins, outs: _plan_pair(ins[0], outs[0]) + _plan_shard_ici(ins[1], outs[1], piece=(1, 2)),
                   4, aliases={2: 1}))
    sum_mlp_in = _pair_sum(dw_mlp_in, got_mlp_in, idx, name="grad_pair_sum_mlp_in")
    dx1, dshift_f, da2, dy1, dgate_m, dg_post_mix = _pre_post_bwd(dh2, x1, dx2, a2, y1, gate_m, g_post_mix,
                                                                  name="pre_mlp_post_mix_bwd")
    buf_mlp_out = _final_sum(dw_mlp_out, got_mlp_out, parts_mlp_out, idx, name="grad_final_sum_mlp_out")
    dw_out, g_mlp_out = _mm(mixed, dy1, mode="tn", out_dtypes=[BF16], name="dw_out", side=_half_side([buf_mlp_out]))
    dw_out = dw_out.reshape(4, d // 4, d)
    dmixed, got_out = _mm(dy1, w_out_full, mode="nt", out_dtypes=[BF16], name="d_mixed", side=_pair_side(dw_out))
    sum_out = _pair_sum(dw_out, got_out, idx, name="grad_pair_sum_out")
    do_fox, delta, dg_fox = _head_norm_bwd(dmixed, o_fox, g_fox, None, nh=FOX_HEADS, hd=FOX_HD, dn_col=0,
                                           gr_col=0, name="fox_norm_bwd")
    do_gla, dgr, _, dg_gla = _head_norm_bwd(dmixed, o_gla, g_gla, proj, nh=GLA_HEADS, hd=GLA_DV, dn_col=1,
                                            gr_col=(3 * FOX_W + 2 * GLA_KW + GLA_W) // GLA_W, name="gla_norm_bwd")
    dq_fox, dk_fox, dv_fox, dcq, dck_t, parts_mlp_in, parts_out = _fox_bwd(
        proj, do_fox, cum_t, lse, delta, name="fox_bwd",
        side=_Side([sum_mlp_in, sum_out], [_parts_shape(sum_mlp_in), _parts_shape(sum_out)],
                   lambda ins, outs: _plan_shard_ici(ins[0], outs[0]) + _plan_shard_ici(ins[1], outs[1]), 6))
    dgq, dgk, dgv, dla = _gla_bwd(proj, log_a, do_gla, states, name="gla_bwd")
    dck = dcq + jnp.concatenate([dck_t.T, jnp.zeros((s, SMALL_W - FOX_HEADS), F32)], axis=1)
    dps, dbf, dw2p, db2 = _gates_bwd(dck, ps, bf, w2p, b_gla_a2, dla, name="gates_bwd")
    dproj = jnp.concatenate([dq_fox.astype(BF16), dk_fox, dv_fox, dgq, dgk, dgv, dgr], axis=1)
    buf_mlp_in = _final_sum(dw_mlp_in, got_mlp_in, parts_mlp_in, idx, name="grad_final_sum_mlp_in")
    buf_out = _final_sum(dw_out, got_out, parts_out, idx, name="grad_final_sum_out")
    dw_main, g_mlp_in, g_out = _mm(dproj, h1, mode="tn", out_dtypes=[BF16], name="dw_in_main",
                                   side=_half_side([buf_mlp_in, buf_out]))
    dw_small, = _mm(dps, h1, mode="tn", out_dtypes=[BF16], name="dw_in_small")
    rs_in = w_in_t.shape[0] // 4
    dw_in = _merge_dw_in(dw_main, dw_small).reshape(4, rs_in, d)
    dh1_small, got_in = _mm(dps, w_small, mode="nn", out_dtypes=[F32], name="d_h1_small", side=_pair_side(dw_in))
    sum_in = _pair_sum(dw_in, got_in, idx, name="grad_pair_sum_in")
    dh1, parts_in = _mm(
        dproj, w_main, mode="nn", out_dtypes=[F32], extras=[dh1_small], epi=lambda acc, e: (acc + e,), name="d_h1",
        side=_Side([sum_in], [_parts_shape(sum_in)],
                   lambda ins, outs: [cp for q in range(3) for cp in _plan_shard_ici(ins[0], outs[0], piece=(q, 4))],
                   9))
    grad_x, dshift_m, da1, parts_in = _pre_bwd(
        dh1, x, dx1, a1, name="pre_mix_bwd",
        side=_Side([sum_in, parts_in], [_parts_shape(sum_in)],
                   lambda ins, outs: _plan_shard_ici(ins[0], outs[0], piece=(3, 4)), 3, aliases={1: 0}))
    buf_in = _final_sum(dw_in, got_in, parts_in, idx, name="grad_final_sum_in")
    g_in, = _half_exchange([buf_in], name="grad_half_exchange_in")
    g_big = [g_in, g_out, g_mlp_in, g_mlp_out]

    dmod = jnp.concatenate([dshift_m, da1 * g_pre_mix, dgate_m, dshift_f, da2 * g_pre_mlp, dgate_f], axis=1)
    small = dict(
        dmod=dmod, g_pre_mix=da1 * (1.0 + scale_m), g_post_mix=dg_post_mix, g_pre_mlp=da2 * (1.0 + scale_f),
        g_post_mlp=dg_post_mlp, b_fgate=dbf[:, :FOX_HEADS], w_gla_a2=dw2p[FOX_HEADS:FOX_HEADS + GLA_RANK],
        b_gla_a2=db2, g_fox_out=dg_fox, g_gla_out=dg_gla)
    return loss_part, grad_x, g_big, small


def _pack(arrays):
    flat = jnp.concatenate([a.reshape(-1).astype(F32) for a in arrays])
    n = flat.shape[0]
    rows = -(-n // 128)
    rows = -(-rows // 8) * 8
    return jnp.pad(flat, (0, rows * 128 - n)).reshape(rows, 128)


def _unpack(buf, shapes):
    flat = buf.reshape(-1)
    out, off = [], 0
    for shp in shapes:
        n = 1
        for q in shp:
            n *= q
        out.append(flat[off:off + n].reshape(shp))
        off += n
    return out


SMALL_GRAD_ORDER = ["dmod", "g_pre_mix", "g_post_mix", "g_pre_mlp", "g_post_mlp", "b_fgate", "w_gla_a2", "b_gla_a2",
                    "g_fox_out", "g_gla_out"]


def kernel(x, c, w_ada, b_ada, g_pre_mix, g_post_mix, w_in, b_fgate, w_gla_a2, b_gla_a2, g_fox_out, g_gla_out, w_out, g_pre_mlp, g_post_mlp, w_mlp_in, w_mlp_out, loss_target, m_w_ada, m_b_ada, m_g_pre_mix, m_g_post_mix, m_w_in, m_b_fgate, m_w_gla_a2, m_b_gla_a2, m_g_fox_out, m_g_gla_out, m_w_out, m_g_pre_mlp, m_g_post_mlp, m_w_mlp_in, m_w_mlp_out, v_w_ada, v_b_ada, v_g_pre_mix, v_g_post_mix, v_w_in, v_b_fgate, v_w_gla_a2, v_b_gla_a2, v_g_fox_out, v_g_gla_out, v_w_out, v_g_pre_mlp, v_g_post_mlp, v_w_mlp_in, v_w_mlp_out):
    ix, iy, ic = lax.axis_index("x"), lax.axis_index("y"), lax.axis_index("c")
    chip = 2 * ix + iy
    dev = 4 * ix + 2 * iy + ic
    d = D_MODEL

    c_act = _silu_rows(c, name="silu_c")
    pack1 = _pack([c_act, w_gla_a2[0], g_gla_out[0]])
    rows1 = pack1.shape[0]
    got1 = _gather8(pack1, name="gather_small_fwd").reshape(8, rows1, 128)
    per_dev = [_unpack(got1[q], [(d,), (GLA_RANK, GLA_KW // 4), (GLA_HEADS, GLA_DV // 4)]) for q in range(8)]
    c_all = jnp.stack([p[0] for p in per_dev])
    w_gla_a2_full = jnp.concatenate([per_dev[2 * j][1] for j in range(4)], axis=1)
    g_gla_full = jnp.concatenate([per_dev[2 * j][2] for j in range(4)], axis=1)
    cols = w_ada.shape[2]
    b_ada_shard = lax.dynamic_slice_in_dim(b_ada, chip * cols, cols, axis=1)
    mod_sh = _mod_shard(c_all, w_ada[0], b_ada_shard, name="ada_mod")
    got2 = _gather8(mod_sh, name="gather_mod").reshape(8, 8, cols)
    mod_all = jnp.concatenate([got2[2 * j] for j in range(4)], axis=1)
    mod = lax.dynamic_slice_in_dim(mod_all, dev, 1, axis=0)

    tr_in = lambda a: jnp.transpose(a[0])
    own_bf = [tr_in(w_in).astype(BF16), w_out[0].astype(BF16), w_mlp_in[0].astype(BF16), w_mlp_out[0].astype(BF16)]
    gw_in = _gather_relayed(own_bf[0], name="gather_w_in_ici")
    idx = jnp.stack([ic, chip]).astype(jnp.int32)
    loss_part, grad_x, g_big, small = _local_step(
        x[0], loss_target[0], mod, g_pre_mix, g_post_mix, g_pre_mlp, g_post_mlp, gw_in, b_fgate,
        w_gla_a2_full, b_gla_a2, g_fox_out[0], g_gla_full, own_bf[0], own_bf[1], own_bf[2], own_bf[3], idx)
    loss = lax.psum(loss_part[0, 0], ("x", "y", "c"))

    big_w = [(tr_in(w_in), tr_in(m_w_in), tr_in(v_w_in)), (w_out[0], m_w_out[0], v_w_out[0]),
             (w_mlp_in[0], m_w_mlp_in[0], v_w_mlp_in[0]), (w_mlp_out[0], m_w_mlp_out[0], v_w_mlp_out[0])]
    big_res = []
    for q, (g, (w, m, v)) in enumerate(zip(g_big, big_w)):
        res4 = (g,) + tuple(_adam(g, w, m, v, name=f"adam_big_{q}"))
        big_res.append(tuple((jnp.transpose(a) if q == 0 else a)[None] for a in res4))

    pack2 = _pack([small[k] for k in SMALL_GRAD_ORDER])
    rows2 = pack2.shape[0]
    got3 = _gather8(pack2, name="gather_small_grads").reshape(8, rows2, 128)
    dmod_all = got3[:, :6 * d // 128, :].reshape(8, 6 * d)
    sums = _stack_sum(got3, name="small_grad_sum")
    shapes = [(1, 6 * d), (1, d), (1, d), (1, d), (1, d), (1, FOX_HEADS), (1, GLA_RANK, GLA_KW), (1, GLA_KW),
              (1, FOX_HEADS, FOX_HD), (1, GLA_HEADS, GLA_DV)]
    sg = dict(zip(["b_ada"] + SMALL_GRAD_ORDER[1:], _unpack(sums, shapes)))
    sg["w_gla_a2"] = lax.dynamic_slice_in_dim(sg["w_gla_a2"], chip * (GLA_KW // 4), GLA_KW // 4, axis=2)
    sg["g_gla_out"] = lax.dynamic_slice_in_dim(sg["g_gla_out"], chip * (GLA_DV // 4), GLA_DV // 4, axis=2)
    small_names = ["b_ada", "g_pre_mix", "g_post_mix", "b_fgate", "w_gla_a2", "b_gla_a2", "g_fox_out", "g_gla_out",
                   "g_pre_mlp", "g_post_mlp"]
    small_w = dict(b_ada=(b_ada, m_b_ada, v_b_ada), g_pre_mix=(g_pre_mix, m_g_pre_mix, v_g_pre_mix),
                   g_post_mix=(g_post_mix, m_g_post_mix, v_g_post_mix), b_fgate=(b_fgate, m_b_fgate, v_b_fgate),
                   w_gla_a2=(w_gla_a2, m_w_gla_a2, v_w_gla_a2), b_gla_a2=(b_gla_a2, m_b_gla_a2, v_b_gla_a2),
                   g_fox_out=(g_fox_out, m_g_fox_out, v_g_fox_out), g_gla_out=(g_gla_out, m_g_gla_out, v_g_gla_out),
                   g_pre_mlp=(g_pre_mlp, m_g_pre_mlp, v_g_pre_mlp), g_post_mlp=(g_post_mlp, m_g_post_mlp, v_g_post_mlp))
    sshapes = [small_w[k][0].shape for k in small_names]
    pg = _pack([sg[k] for k in small_names])
    pw, pm, pv = [_pack([small_w[k][q] for k in small_names]) for q in range(3)]
    pd, pmn, pvn = _adam(pg, pw, pm, pv, name="adam_small")
    s_delta = dict(zip(small_names, _unpack(pd, sshapes)))
    s_m = dict(zip(small_names, _unpack(pmn, sshapes)))
    s_v = dict(zip(small_names, _unpack(pvn, sshapes)))

    dmod_cols = lax.dynamic_slice_in_dim(dmod_all, chip * cols, cols, axis=1)
    g_ada, d_ada, m_ada, v_ada = _ada_grad_adam(c_all.T, dmod_cols, w_ada[0], m_w_ada[0], v_w_ada[0], name="ada_grad_adam")

    order = ["w_ada", "b_ada", "g_pre_mix", "g_post_mix", "w_in", "b_fgate", "w_gla_a2", "b_gla_a2", "g_fox_out",
             "g_gla_out", "w_out", "g_pre_mlp", "g_post_mlp", "w_mlp_in", "w_mlp_out"]
    res = {"w_ada": (g_ada[None], d_ada[None], m_ada[None], v_ada[None]),
           "w_in": big_res[0], "w_out": big_res[1], "w_mlp_in": big_res[2], "w_mlp_out": big_res[3]}
    for k in small_names:
        res[k] = (sg[k], s_delta[k], s_m[k], s_v[k])
    return (loss, grad_x[None], *[res[k][0] for k in order], *[res[k][1] for k in order],
            *[res[k][2] for k in order], *[res[k][3] for k in order])
```

```python
import functools

import jax
import jax.numpy as jnp
from jax import lax
from jax.experimental import pallas as pl
from jax.experimental.pallas import tpu as pltpu

F32 = jnp.float32
BF16 = jnp.bfloat16
MESH = pl.DeviceIdType.MESH
HIGHEST = lax.Precision.HIGHEST

D_MODEL = 2048
FOX_HEADS = 8
FOX_HD = 128
FOX_W = FOX_HEADS * FOX_HD
GLA_HEADS = 4
GLA_DK = 128
GLA_DV = 256
GLA_KW = GLA_HEADS * GLA_DK
GLA_W = GLA_HEADS * GLA_DV
GLA_RANK = 16
GLA_TEMP = 16.0
CHUNK = 64
D_FF = 4 * D_MODEL
EPS = 1e-6
MAIN_W = 3 * FOX_W + 2 * GLA_KW + 2 * GLA_W
SMALL_W = 128
NEG = -1e30

ADAM_LR = 0.001
ADAM_B1 = 0.9
ADAM_B2 = 0.999
ADAM_EPS = 1e-08
ADAM_WD = 0.01
ADAM_STEP = 10

VMEM_LIMIT = 52 * 1024 * 1024
ROW_TILE = 256
WIDE_ROW_TILE = 512
FOX_TQ = 512
FOX_TK = 512
GLA_ROWS = 512
GATE_TS = 512
MM_T = 1024
MM_TK = 2048
MM_TK_TOKENS = 4096
MM_TM = 2048


def _cp(*sem):
    return pltpu.CompilerParams(dimension_semantics=sem, vmem_limit_bytes=VMEM_LIMIT)


def _dot_nn(a, b, precision=None):
    return jnp.dot(a, b, preferred_element_type=F32, precision=precision)


def _dot_nt(a, b, precision=None):
    return lax.dot_general(a, b, (((1,), (1,)), ((), ())), preferred_element_type=F32, precision=precision)


def _dot_tn(a, b, precision=None):
    return lax.dot_general(a, b, (((0,), (0,)), ((), ())), preferred_element_type=F32, precision=precision)


def _sigmoid(x):
    return 1.0 / (1.0 + jnp.exp(-x))


def _log_sigmoid(x):
    return jnp.minimum(x, 0.0) - jnp.log(1.0 + jnp.exp(-jnp.abs(x)))


class _Side:
    def __init__(self, inputs, out_shapes, plan_fn, n_copies, aliases=None):
        self.inputs, self.out_shapes, self.plan_fn, self.n_copies = list(inputs), list(out_shapes), plan_fn, n_copies
        self.aliases = dict(aliases or {})

    def scratch(self):
        return [pltpu.SemaphoreType.DMA((self.n_copies,)), pltpu.SemaphoreType.DMA((self.n_copies,))]

    def run(self, in_refs, out_refs, sems, first, last):
        @pl.when(first)
        def _():
            _plan_start(self.plan_fn(in_refs, out_refs), *sems)

        @pl.when(last)
        def _():
            _plan_wait(self.plan_fn(in_refs, out_refs), *sems)


def _mm(a, b, *, mode, out_dtypes, name, tm=None, tn=None, tk=None, extras=(), epi=None,
        out_shapes=None, out_specs=None, side=None, b_slots=0):
    tm, tn, tk = tm or MM_T, tn or MM_T, tk or MM_TK
    b2 = (b.shape[1], b_slots * b.shape[2]) if b_slots else b.shape
    if mode == "nn":
        (m, k), n = a.shape, b2[1]
    elif mode == "nt":
        (m, k), n = a.shape, b2[0]
    else:
        (k, m), n = a.shape, b2[1]
    tm, tn, tk = min(tm, m), min(tn, n), min(tk, k)
    if b_slots:
        tn = min(tn, b.shape[2]) if mode == "nn" else tn
        tk = min(tk, b.shape[2]) if mode == "nt" else tk
    assert m % tm == 0 and n % tn == 0 and k % tk == 0, (name, m, n, k)
    nk = k // tk
    n_out, n_ex = len(out_dtypes), len(extras)
    if epi is None:
        epi = lambda acc: tuple(acc for _ in range(n_out))
    dot = {"nn": _dot_nn, "nt": _dot_nt, "tn": _dot_tn}[mode]

    n_si = len(side.inputs) if side else 0
    n_so = len(side.out_shapes) if side else 0
    grid = (m // tm, n // tn, nk)

    def body(*refs):
        a_ref, b_ref = refs[0], refs[1]
        ex_refs = refs[2:2 + n_ex]
        base = 2 + n_ex + n_si
        o_refs = refs[base:base + n_out]
        scratch = refs[base + n_out + n_so:]
        if side:
            pos = [pl.program_id(q) for q in range(3)]
            first = (pos[0] == 0) & (pos[1] == 0) & (pos[2] == 0)
            last = (pos[0] == grid[0] - 1) & (pos[1] == grid[1] - 1) & (pos[2] == grid[2] - 1)
            side.run(refs[2 + n_ex:base], refs[base + n_out:base + n_out + n_so], scratch[-2:], first, last)
        part = dot(a_ref[...], b_ref[...])

        def finish(acc):
            outs = epi(acc, *[e[...] for e in ex_refs])
            for o_ref, val in zip(o_refs, outs):
                o_ref[...] = val.reshape(o_ref.shape).astype(o_ref.dtype)

        if nk == 1:
            finish(part)
        else:
            acc_ref = scratch[0]
            kk = pl.program_id(2)

            @pl.when(kk == 0)
            def _():
                acc_ref[...] = part

            @pl.when(kk > 0)
            def _():
                acc_ref[...] += part

            @pl.when(kk == nk - 1)
            def _():
                finish(acc_ref[...])

    if mode == "nn":
        a_spec = pl.BlockSpec((tm, tk), lambda i, j, kk: (i, kk))
        b_spec = pl.BlockSpec((tk, tn), lambda i, j, kk: (kk, j))
        if b_slots:
            per = b.shape[2] // tn
            b_spec = pl.BlockSpec((None, tk, tn), lambda i, j, kk: (j // per, kk, j % per))
    elif mode == "nt":
        a_spec = pl.BlockSpec((tm, tk), lambda i, j, kk: (i, kk))
        b_spec = pl.BlockSpec((tn, tk), lambda i, j, kk: (j, kk))
        if b_slots:
            per = b.shape[2] // tk
            b_spec = pl.BlockSpec((None, tn, tk), lambda i, j, kk: (kk // per, j, kk % per))
    else:
        assert not b_slots
        a_spec = pl.BlockSpec((tk, tm), lambda i, j, kk: (kk, i))
        b_spec = pl.BlockSpec((tk, tn), lambda i, j, kk: (kk, j))
    tile_spec = pl.BlockSpec((tm, tn), lambda i, j, kk: (i, j))
    if out_shapes is None:
        out_shapes = [jax.ShapeDtypeStruct((m, n), dt) for dt in out_dtypes]
    if out_specs is None:
        out_specs = [tile_spec for _ in out_dtypes]
    any_spec = pl.BlockSpec(memory_space=pl.ANY)
    res = pl.pallas_call(
        body,
        grid=grid,
        in_specs=[a_spec, b_spec] + [tile_spec for _ in extras] + [any_spec] * n_si,
        out_specs=list(out_specs) + [any_spec] * n_so,
        out_shape=list(out_shapes) + (side.out_shapes if side else []),
        scratch_shapes=([pltpu.VMEM((tm, tn), F32)] if nk > 1 else []) + (side.scratch() if side else []),
        compiler_params=_cp("arbitrary", "arbitrary", "arbitrary") if side else _cp("parallel", "parallel", "arbitrary"),
        input_output_aliases={2 + n_ex + si: n_out + so for si, so in side.aliases.items()} if side else {},
        name=name,
    )(a, b, *extras, *(side.inputs if side else []))
    return res


def _row_spec(ts, d):
    return pl.BlockSpec((ts, d), lambda i: (i, 0))


def _vec_spec(d):
    return pl.BlockSpec((1, d), lambda i: (0, 0))


def _side_args(side, n_in, n_out):
    if side is None:
        return [], [], [], [], [], {}
    any_spec = pl.BlockSpec(memory_space=pl.ANY)
    return ([any_spec] * len(side.inputs), [any_spec] * len(side.out_shapes), side.out_shapes, side.scratch(),
            side.inputs, {n_in + si: n_out + so for si, so in side.aliases.items()})


def _pre_fwd(x, avec, shift, *, name, side=None):
    s, d = x.shape
    ts = min(WIDE_ROW_TILE, s)
    nb = s // ts
    s_in, s_out, s_shapes, s_scratch, s_ops, s_alias = _side_args(side, 3, 1)

    def body(x_ref, a_ref, s_ref, *rest):
        h_ref = rest[len(s_in)]
        if side:
            step = pl.program_id(0)
            side.run(rest[:len(s_in)], rest[len(s_in) + 1:len(s_in) + 1 + len(s_out)],
                     rest[len(s_in) + 1 + len(s_out):], step == 0, step == nb - 1)
        xv = x_ref[...]
        r = lax.rsqrt(jnp.mean(xv * xv, axis=-1, keepdims=True) + EPS)
        h_ref[...] = (xv * r * a_ref[...] + s_ref[...]).astype(BF16)

    res = pl.pallas_call(
        body, grid=(nb,),
        in_specs=[_row_spec(ts, d), _vec_spec(d), _vec_spec(d)] + s_in,
        out_specs=[_row_spec(ts, d)] + s_out,
        out_shape=[jax.ShapeDtypeStruct((s, d), BF16)] + s_shapes,
        scratch_shapes=s_scratch, input_output_aliases=s_alias,
        compiler_params=_cp("arbitrary" if side else "parallel"), name=name,
    )(x, avec, shift, *s_ops)
    return res if side else res[0]


def _post_pre_fwd(x, y, gate, g, avec, shift, *, name):
    s, d = x.shape
    ts = min(WIDE_ROW_TILE, s)

    def body(x_ref, y_ref, gate_ref, g_ref, a_ref, s_ref, o_ref, h_ref):
        yv = y_ref[...]
        r = lax.rsqrt(jnp.mean(yv * yv, axis=-1, keepdims=True) + EPS)
        x1 = x_ref[...] + gate_ref[...] * (yv * r * g_ref[...])
        o_ref[...] = x1
        r1 = lax.rsqrt(jnp.mean(x1 * x1, axis=-1, keepdims=True) + EPS)
        h_ref[...] = (x1 * r1 * a_ref[...] + s_ref[...]).astype(BF16)

    return pl.pallas_call(
        body, grid=(s // ts,),
        in_specs=[_row_spec(ts, d), _row_spec(ts, d)] + [_vec_spec(d)] * 4,
        out_specs=[_row_spec(ts, d), _row_spec(ts, d)],
        out_shape=[jax.ShapeDtypeStruct((s, d), F32), jax.ShapeDtypeStruct((s, d), BF16)],
        compiler_params=_cp("parallel"), name=name,
    )(x, y, gate, g, avec, shift)


def _post_bwd_math(dxv, yv, gatev, gv):
    r = lax.rsqrt(jnp.mean(yv * yv, axis=-1, keepdims=True) + EPS)
    yhat = yv * r
    dn = dxv * gatev
    dyhat = dn * gv
    dy = r * (dyhat - yhat * jnp.mean(dyhat * yhat, axis=-1, keepdims=True))
    return dy, dxv * (yhat * gv), dn * yhat


def _accumulate(first, pairs):
    @pl.when(first)
    def _():
        for ref, _ in pairs:
            ref[...] = jnp.zeros_like(ref)

    for ref, val in pairs:
        ref[...] += jnp.sum(val, axis=0, keepdims=True)


def _post_loss_bwd(x, y, gate, g, target, *, name):
    s, d = x.shape
    ts = min(ROW_TILE, s)

    def body(x_ref, y_ref, gate_ref, g_ref, t_ref, dx_ref, dy_ref, loss_ref, dgate_ref, dg_ref):
        yv, gatev, gv = y_ref[...], gate_ref[...], g_ref[...]
        r = lax.rsqrt(jnp.mean(yv * yv, axis=-1, keepdims=True) + EPS)
        diff = x_ref[...] + gatev * (yv * r * gv) - t_ref[...]
        dxv = diff * (1.0 / d)
        dx_ref[...] = dxv
        dy, dgate_rows, dg_rows = _post_bwd_math(dxv, yv, gatev, gv)
        dy_ref[...] = dy.astype(BF16)
        first = pl.program_id(0) == 0
        _accumulate(first, [(dgate_ref, dgate_rows), (dg_ref, dg_rows)])

        @pl.when(first)
        def _():
            loss_ref[...] = jnp.zeros_like(loss_ref)

        loss_ref[...] += jnp.sum(jnp.mean(diff * diff, axis=-1, keepdims=True)) * 0.5

    return pl.pallas_call(
        body, grid=(s // ts,),
        in_specs=[_row_spec(ts, d), _row_spec(ts, d), _vec_spec(d), _vec_spec(d), _row_spec(ts, d)],
        out_specs=[_row_spec(ts, d), _row_spec(ts, d), pl.BlockSpec((1, 128), lambda i: (0, 0)), _vec_spec(d),
                   _vec_spec(d)],
        out_shape=[jax.ShapeDtypeStruct((s, d), F32), jax.ShapeDtypeStruct((s, d), BF16),
                   jax.ShapeDtypeStruct((1, 128), F32), jax.ShapeDtypeStruct((1, d), F32),
                   jax.ShapeDtypeStruct((1, d), F32)],
        compiler_params=_cp("arbitrary"), name=name,
    )(x, y, gate, g, target)


def _pre_post_bwd(dh, xin, dres, avec, y, gate, g, *, name):
    s, d = xin.shape
    ts = min(ROW_TILE, s)

    def body(dh_ref, x_ref, dres_ref, a_ref, y_ref, gate_ref, g_ref, dx_ref, dshift_ref, da_ref, dy_ref,
             dgate_ref, dg_ref):
        xv, dhv = x_ref[...], dh_ref[...]
        r = lax.rsqrt(jnp.mean(xv * xv, axis=-1, keepdims=True) + EPS)
        xhat = xv * r
        dxhat = dhv * a_ref[...]
        dxv = dres_ref[...] + r * (dxhat - xhat * jnp.mean(dxhat * xhat, axis=-1, keepdims=True))
        dx_ref[...] = dxv
        dy, dgate_rows, dg_rows = _post_bwd_math(dxv, y_ref[...], gate_ref[...], g_ref[...])
        dy_ref[...] = dy.astype(BF16)
        _accumulate(pl.program_id(0) == 0, [(dshift_ref, dhv), (da_ref, dhv * xhat), (dgate_ref, dgate_rows),
                                            (dg_ref, dg_rows)])

    return pl.pallas_call(
        body, grid=(s // ts,),
        in_specs=[_row_spec(ts, d), _row_spec(ts, d), _row_spec(ts, d), _vec_spec(d), _row_spec(ts, d),
                  _vec_spec(d), _vec_spec(d)],
        out_specs=[_row_spec(ts, d), _vec_spec(d), _vec_spec(d), _row_spec(ts, d), _vec_spec(d), _vec_spec(d)],
        out_shape=[jax.ShapeDtypeStruct((s, d), F32), jax.ShapeDtypeStruct((1, d), F32),
                   jax.ShapeDtypeStruct((1, d), F32), jax.ShapeDtypeStruct((s, d), BF16),
                   jax.ShapeDtypeStruct((1, d), F32), jax.ShapeDtypeStruct((1, d), F32)],
        compiler_params=_cp("arbitrary"), name=name,
    )(dh, xin, dres, avec, y, gate, g)


def _pre_bwd(dh, xin, dres, avec, *, name, side=None):
    s, d = xin.shape
    ts = min(WIDE_ROW_TILE, s)
    nb = s // ts
    s_in, s_out, s_shapes, s_scratch, s_ops, s_alias = _side_args(side, 4, 3)

    def body(dh_ref, x_ref, dres_ref, a_ref, *rest):
        dx_ref, dshift_ref, da_ref = rest[len(s_in):len(s_in) + 3]
        if side:
            step = pl.program_id(0)
            side.run(rest[:len(s_in)], rest[len(s_in) + 3:len(s_in) + 3 + len(s_out)],
                     rest[len(s_in) + 3 + len(s_out):], step == 0, step == nb - 1)
        xv, dhv = x_ref[...], dh_ref[...]
        r = lax.rsqrt(jnp.mean(xv * xv, axis=-1, keepdims=True) + EPS)
        xhat = xv * r
        dxhat = dhv * a_ref[...]
        dx_ref[...] = dres_ref[...] + r * (dxhat - xhat * jnp.mean(dxhat * xhat, axis=-1, keepdims=True))

        @pl.when(pl.program_id(0) == 0)
        def _():
            dshift_ref[...] = jnp.zeros_like(dshift_ref)
            da_ref[...] = jnp.zeros_like(da_ref)

        dshift_ref[...] += jnp.sum(dhv, axis=0, keepdims=True)
        da_ref[...] += jnp.sum(dhv * xhat, axis=0, keepdims=True)

    return pl.pallas_call(
        body, grid=(nb,),
        in_specs=[_row_spec(ts, d), _row_spec(ts, d), _row_spec(ts, d), _vec_spec(d)] + s_in,
        out_specs=[_row_spec(ts, d), _vec_spec(d), _vec_spec(d)] + s_out,
        out_shape=[jax.ShapeDtypeStruct((s, d), F32), jax.ShapeDtypeStruct((1, d), F32),
                   jax.ShapeDtypeStruct((1, d), F32)] + s_shapes,
        scratch_shapes=s_scratch, input_output_aliases=s_alias,
        compiler_params=_cp("arbitrary"), name=name,
    )(dh, xin, dres, avec, *s_ops)


def _tri(n, strict=False, upper=False):
    r = lax.broadcasted_iota(jnp.int32, (n, n), 0)
    c = lax.broadcasted_iota(jnp.int32, (n, n), 1)
    if upper:
        r, c = c, r
    return ((r > c) if strict else (r >= c)).astype(F32)


def _gates_fwd(ps, bf, w2p, b2, *, name):
    s = ps.shape[0]
    ts = min(GATE_TS, s)

    def body(ps_ref, bf_ref, w_ref, b2_ref, cum_ref, la_ref, carry_ref):
        @pl.when(pl.program_id(0) == 0)
        def _():
            carry_ref[...] = jnp.zeros_like(carry_ref)

        psv = ps_ref[...]
        lf = _log_sigmoid(psv + bf_ref[...])
        cum = _dot_nn(_tri(ts), lf, HIGHEST) + carry_ref[...]
        cum_ref[...] = cum
        carry_ref[...] = cum[ts - 1:ts, :]
        z = _dot_nn(psv, w_ref[...], HIGHEST) + b2_ref[...]
        la_ref[...] = _log_sigmoid(z) * (1.0 / GLA_TEMP)

    return pl.pallas_call(
        body, grid=(s // ts,),
        in_specs=[_row_spec(ts, SMALL_W), _vec_spec(SMALL_W),
                  pl.BlockSpec((SMALL_W, GLA_KW), lambda i: (0, 0)), _vec_spec(GLA_KW)],
        out_specs=[_row_spec(ts, SMALL_W), _row_spec(ts, GLA_KW)],
        out_shape=[jax.ShapeDtypeStruct((s, SMALL_W), F32), jax.ShapeDtypeStruct((s, GLA_KW), F32)],
        scratch_shapes=[pltpu.VMEM((1, SMALL_W), F32)],
        compiler_params=_cp("arbitrary"), name=name,
    )(ps, bf, w2p, b2)


def _gates_bwd(dck, ps, bf, w2p, b2, dla, *, name):
    s = ps.shape[0]
    ts = min(GATE_TS, s)
    nb = s // ts
    rev = lambda i: (nb - 1 - i, 0)

    def body(dck_ref, ps_ref, bf_ref, w_ref, b2_ref, dla_ref, dps_ref, dbf_ref, dw_ref, db2_ref, carry_ref):
        @pl.when(pl.program_id(0) == 0)
        def _():
            carry_ref[...] = jnp.zeros_like(carry_ref)
            dbf_ref[...] = jnp.zeros_like(dbf_ref)
            dw_ref[...] = jnp.zeros_like(dw_ref)
            db2_ref[...] = jnp.zeros_like(db2_ref)

        psv, dckv = ps_ref[...], dck_ref[...]
        dlf = _dot_nn(_tri(ts, upper=True), dckv, HIGHEST) + carry_ref[...]
        carry_ref[...] += jnp.sum(dckv, axis=0, keepdims=True)
        lane = lax.broadcasted_iota(jnp.int32, (ts, SMALL_W), 1)
        dff = jnp.where(lane < FOX_HEADS, dlf * _sigmoid(-(psv + bf_ref[...])), 0.0)
        z = _dot_nn(psv, w_ref[...], HIGHEST) + b2_ref[...]
        dz = dla_ref[...] * _sigmoid(-z) * (1.0 / GLA_TEMP)
        dps_ref[...] = (_dot_nt(dz, w_ref[...], HIGHEST) + dff).astype(BF16)
        dbf_ref[...] += jnp.sum(dff, axis=0, keepdims=True)
        dw_ref[...] += _dot_tn(psv, dz, HIGHEST)
        db2_ref[...] += jnp.sum(dz, axis=0, keepdims=True)

    return pl.pallas_call(
        body, grid=(nb,),
        in_specs=[pl.BlockSpec((ts, SMALL_W), rev), pl.BlockSpec((ts, SMALL_W), rev), _vec_spec(SMALL_W),
                  pl.BlockSpec((SMALL_W, GLA_KW), lambda i: (0, 0)), _vec_spec(GLA_KW),
                  pl.BlockSpec((ts, GLA_KW), rev)],
        out_specs=[pl.BlockSpec((ts, SMALL_W), rev), _vec_spec(SMALL_W),
                   pl.BlockSpec((SMALL_W, GLA_KW), lambda i: (0, 0)), _vec_spec(GLA_KW)],
        out_shape=[jax.ShapeDtypeStruct((s, SMALL_W), BF16), jax.ShapeDtypeStruct((1, SMALL_W), F32),
                   jax.ShapeDtypeStruct((SMALL_W, GLA_KW), F32), jax.ShapeDtypeStruct((1, GLA_KW), F32)],
        scratch_shapes=[pltpu.VMEM((1, SMALL_W), F32)],
        compiler_params=_cp("arbitrary"), name=name,
    )(dck, ps, bf, w2p, b2, dla)


def _hs(h, hd=FOX_HD):
    return slice(h * hd, (h + 1) * hd)


def _fox_fwd(proj, cum_t, g_fox, *, name, side=None):
    s = proj.shape[0]
    tq, tk = min(FOX_TQ, s), min(FOX_TK, s)
    scale = FOX_HD ** -0.5
    n_si = len(side.inputs) if side else 0
    n_so = len(side.out_shapes) if side else 0
    grid = (s // tq, s // tk)

    def body(*refs):
        q_ref, k_ref, v_ref, ck_ref, g_ref = refs[:5]
        o_ref, n_ref, lse_ref = refs[5 + n_si:8 + n_si]
        m_sc, acc_sc = refs[8 + n_si + n_so:10 + n_si + n_so]
        i, j = pl.program_id(0), pl.program_id(1)
        if side:
            side.run(refs[5:5 + n_si], refs[8 + n_si:8 + n_si + n_so], refs[10 + n_si + n_so:],
                     (i == 0) & (j == 0), (i == grid[0] - 1) & (j == grid[1] - 1))

        @pl.when(j == 0)
        def _():
            m_sc[...] = jnp.full_like(m_sc, NEG)
            acc_sc[...] = jnp.zeros_like(acc_sc)

        def block(masked):
            mask = _causal_mask(i, j, tq, tk) if masked else None
            ones = jnp.ones((tk, FOX_HD), BF16)
            for h in range(FOX_HEADS):
                sc = _fox_logits(_dot_nt(q_ref[:, _hs(h)], k_ref[:, _hs(h)]), ck_ref[h:h + 1, :], mask, scale)
                m_prev = m_sc[h]
                m_new = jnp.maximum(m_prev, jnp.max(sc, axis=-1, keepdims=True))
                alpha = jnp.exp(m_prev - m_new)
                p = jnp.exp(sc - m_new).astype(BF16)
                v_one = jnp.concatenate([v_ref[:, _hs(h)], ones], axis=1)
                acc_sc[:, _hs(h, 2 * FOX_HD)] = alpha * acc_sc[:, _hs(h, 2 * FOX_HD)] + _dot_nn(p, v_one)
                m_sc[h] = m_new

        pl.when(j < i)(functools.partial(block, False))

        @pl.when(j == i)
        def _():
            block(True)
            lane = lax.broadcasted_iota(jnp.int32, (tq, 128), 1)
            lse = jnp.zeros((tq, 128), F32)
            for h in range(FOX_HEADS):
                l_rep = acc_sc[:, 2 * h * FOX_HD + FOX_HD:2 * (h + 1) * FOX_HD]
                o = acc_sc[:, 2 * h * FOX_HD:2 * h * FOX_HD + FOX_HD] / l_rep
                o_ref[:, _hs(h)] = o
                r = lax.rsqrt(jnp.mean(o * o, axis=-1, keepdims=True) + EPS)
                n_ref[:, _hs(h)] = (o * r * g_ref[h:h + 1, :]).astype(BF16)
                lse = jnp.where(lane == h, m_sc[h] + jnp.log(l_rep), lse)
            lse_ref[...] = lse

    kv = lambda col: (lambda i, j: (jnp.minimum(j, i), col))
    any_spec = pl.BlockSpec(memory_space=pl.ANY)
    return pl.pallas_call(
        body, grid=grid,
        in_specs=[pl.BlockSpec((tq, FOX_W), lambda i, j: (i, 0)),
                  pl.BlockSpec((tk, FOX_W), kv(1)),
                  pl.BlockSpec((tk, FOX_W), kv(2)),
                  pl.BlockSpec((FOX_HEADS, tk), lambda i, j: (0, jnp.minimum(j, i))),
                  pl.BlockSpec((FOX_HEADS, FOX_HD), lambda i, j: (0, 0))] + [any_spec] * n_si,
        out_specs=[pl.BlockSpec((tq, FOX_W), lambda i, j: (i, 0)),
                   pl.BlockSpec((tq, FOX_W), lambda i, j: (i, 0)),
                   pl.BlockSpec((tq, 128), lambda i, j: (i, 0))] + [any_spec] * n_so,
        out_shape=[jax.ShapeDtypeStruct((s, FOX_W), F32), jax.ShapeDtypeStruct((s, FOX_W), BF16),
                   jax.ShapeDtypeStruct((s, 128), F32)] + (side.out_shapes if side else []),
        scratch_shapes=[pltpu.VMEM((FOX_HEADS, tq, 1), F32), pltpu.VMEM((tq, 2 * FOX_W), F32)]
        + (side.scratch() if side else []),
        compiler_params=_cp("arbitrary", "arbitrary"), name=name,
    )(proj, proj, proj, cum_t, g_fox, *(side.inputs if side else []))


def _causal_mask(i, j, tq, tk):
    rows = i * tq + lax.broadcasted_iota(jnp.int32, (tq, tk), 0)
    cols = j * tk + lax.broadcasted_iota(jnp.int32, (tq, tk), 1)
    return rows >= cols


def _fox_logits(qk, ck, mask, scale):
    sc = qk * scale - ck
    return sc if mask is None else jnp.where(mask, sc, NEG)


def _fox_bwd(proj, do, cum_t, lse, delta, *, name, side=None):
    s = proj.shape[0]
    tq, tk = min(FOX_TQ, s), min(FOX_TK, s)
    nk, nq = s // tk, s // tq
    scale = FOX_HD ** -0.5
    n_si = len(side.inputs) if side else 0
    n_so = len(side.out_shapes) if side else 0

    def body(*refs):
        q_ref, k_ref, v_ref, do_ref, ck_ref, lse_ref, dl_ref = refs[:7]
        dq_hbm, dk_ref, dv_ref, dcq_hbm, dck_ref = refs[7 + n_si:12 + n_si]
        dq_sc, dcq_sc, dk_sc, dv_sc, dck_sc, out_sems = refs[12 + n_si + n_so:18 + n_si + n_so]
        j, i = pl.program_id(0), pl.program_id(1)
        if side:
            side.run(refs[7:7 + n_si], refs[12 + n_si:12 + n_si + n_so], refs[18 + n_si + n_so:],
                     (j == 0) & (i == 0), (j == nk - 1) & (i == nq - 1))

        @pl.when((j == 0) & (i == 0))
        def _():
            dq_sc[...] = jnp.zeros_like(dq_sc)
            dcq_sc[...] = jnp.zeros_like(dcq_sc)

        @pl.when(i == 0)
        def _():
            dk_sc[...] = jnp.zeros_like(dk_sc)
            dv_sc[...] = jnp.zeros_like(dv_sc)
            dck_sc[...] = jnp.zeros_like(dck_sc)

        def block(masked):
            mask = _causal_mask(i, j, tq, tk) if masked else None
            qrows = pl.ds(pl.multiple_of(i * tq, tq), tq)
            for h in range(FOX_HEADS):
                sc = _fox_logits(_dot_nt(q_ref[:, _hs(h)], k_ref[:, _hs(h)]), ck_ref[h:h + 1, :], mask, scale)
                p = jnp.exp(sc - lse_ref[:, h:h + 1])
                ds = p * (_dot_nt(do_ref[:, _hs(h)], v_ref[:, _hs(h)]) - dl_ref[:, h:h + 1])
                dsb = ds.astype(BF16)
                dv_sc[:, _hs(h)] += _dot_tn(p.astype(BF16), do_ref[:, _hs(h)])
                dk_sc[:, _hs(h)] += _dot_tn(dsb, q_ref[:, _hs(h)])
                dq_sc[qrows, _hs(h)] += _dot_nn(dsb, k_ref[:, _hs(h)]) * scale
                dck_sc[h:h + 1, :] -= jnp.sum(ds, axis=0, keepdims=True)
                dcq_sc[qrows, h:h + 1] += jnp.sum(ds, axis=-1, keepdims=True)

        pl.when(i > j)(functools.partial(block, False))
        pl.when(i == j)(functools.partial(block, True))

        @pl.when(i == nq - 1)
        def _():
            dk_ref[...] = (dk_sc[...] * scale).astype(BF16)
            dv_ref[...] = dv_sc[...].astype(BF16)
            dck_ref[...] = dck_sc[...]

        @pl.when((j == nk - 1) & (i == nq - 1))
        def _():
            out_q = pltpu.make_async_copy(dq_sc, dq_hbm, out_sems.at[0])
            out_c = pltpu.make_async_copy(dcq_sc, dcq_hbm, out_sems.at[1])
            out_q.start()
            out_c.start()
            out_q.wait()
            out_c.wait()

    qrow = lambda j, i: (jnp.maximum(i, j), 0)
    krow = lambda col: (lambda j, i: (j, col))
    any_spec = pl.BlockSpec(memory_space=pl.ANY)
    return pl.pallas_call(
        body, grid=(nk, nq),
        in_specs=[pl.BlockSpec((tq, FOX_W), qrow), pl.BlockSpec((tk, FOX_W), krow(1)),
                  pl.BlockSpec((tk, FOX_W), krow(2)),
                  pl.BlockSpec((tq, FOX_W), qrow),
                  pl.BlockSpec((FOX_HEADS, tk), lambda j, i: (0, j)),
                  pl.BlockSpec((tq, 128), qrow), pl.BlockSpec((tq, 128), qrow)] + [any_spec] * n_si,
        out_specs=[any_spec, pl.BlockSpec((tk, FOX_W), lambda j, i: (j, 0)),
                   pl.BlockSpec((tk, FOX_W), lambda j, i: (j, 0)), any_spec,
                   pl.BlockSpec((FOX_HEADS, tk), lambda j, i: (0, j))] + [any_spec] * n_so,
        out_shape=[jax.ShapeDtypeStruct((s, FOX_W), F32), jax.ShapeDtypeStruct((s, FOX_W), BF16),
                   jax.ShapeDtypeStruct((s, FOX_W), BF16), jax.ShapeDtypeStruct((s, 128), F32),
                   jax.ShapeDtypeStruct((FOX_HEADS, s), F32)] + (side.out_shapes if side else []),
        scratch_shapes=[pltpu.VMEM((s, FOX_W), F32), pltpu.VMEM((s, 128), F32),
                        pltpu.VMEM((tk, FOX_W), F32), pltpu.VMEM((tk, FOX_W), F32), pltpu.VMEM((FOX_HEADS, tk), F32),
                        pltpu.SemaphoreType.DMA((2,))] + (side.scratch() if side else []),
        compiler_params=_cp("arbitrary", "arbitrary"), name=name,
    )(proj, proj, proj, do, cum_t, lse, delta, *(side.inputs if side else []))


def _head_norm_bwd(dn_in, o, g, gr_src, *, nh, hd, dn_col, gr_col, name):
    s, w = o.shape
    ts = min(ROW_TILE, s)
    gated = gr_src is not None

    def body(*refs):
        if gated:
            dn_ref, o_ref, g_ref, gr_ref, do_ref, dgr_ref, dl_ref, dg_ref = refs
        else:
            dn_ref, o_ref, g_ref, do_ref, dl_ref, dg_ref = refs

        @pl.when(pl.program_id(0) == 0)
        def _():
            dg_ref[...] = jnp.zeros_like(dg_ref)

        lane = lax.broadcasted_iota(jnp.int32, (ts, 128), 1)
        delta = jnp.zeros((ts, 128), F32)
        for h in range(nh):
            sl = _hs(h, hd)
            ov = o_ref[:, sl]
            dnv = dn_ref[:, sl].astype(F32)
            gv = g_ref[h:h + 1, :]
            r = lax.rsqrt(jnp.mean(ov * ov, axis=-1, keepdims=True) + EPS)
            ohat = ov * r
            if gated:
                grv = gr_ref[:, sl].astype(F32)
                sig = _sigmoid(grv)
                dgr_ref[:, sl] = (dnv * (ohat * gv) * (sig * (1.0 + grv * (1.0 - sig)))).astype(BF16)
                dnv = dnv * (grv * sig)
            dg_ref[h:h + 1, :] += jnp.sum(dnv * ohat, axis=0, keepdims=True)
            dohat = dnv * gv
            do = r * (dohat - ohat * jnp.mean(dohat * ohat, axis=-1, keepdims=True))
            do_ref[:, sl] = do.astype(BF16)
            delta = jnp.where(lane == h, jnp.sum(do.astype(BF16).astype(F32) * ov, axis=-1, keepdims=True), delta)
        dl_ref[...] = delta

    in_specs = [pl.BlockSpec((ts, w), lambda i: (i, dn_col)), _row_spec(ts, w),
                pl.BlockSpec((nh, hd), lambda i: (0, 0))]
    args = [dn_in, o, g]
    out_specs = [_row_spec(ts, w)]
    out_shape = [jax.ShapeDtypeStruct((s, w), BF16)]
    if gated:
        in_specs.append(pl.BlockSpec((ts, w), lambda i: (i, gr_col)))
        args.append(gr_src)
        out_specs.append(_row_spec(ts, w))
        out_shape.append(jax.ShapeDtypeStruct((s, w), BF16))
    out_specs += [_row_spec(ts, 128), pl.BlockSpec((nh, hd), lambda i: (0, 0))]
    out_shape += [jax.ShapeDtypeStruct((s, 128), F32), jax.ShapeDtypeStruct((nh, hd), F32)]
    return pl.pallas_call(
        body, grid=(s // ts,), in_specs=in_specs, out_specs=out_specs, out_shape=out_shape,
        compiler_params=_cp("arbitrary"), name=name,
    )(*args)


GQ_BLK = 3 * FOX_W // GLA_DK
GK_BLK = GQ_BLK + GLA_HEADS
GV_BLK = (3 * FOX_W + 2 * GLA_KW) // GLA_DV
GR_BLK = GV_BLK + GLA_HEADS


def _gla_chunk_terms(la):
    cum = _dot_nn(_tri(CHUNK), la, HIGHEST)
    total = cum[CHUNK - 1:CHUNK, :]
    return jnp.exp(total - cum), jnp.exp(total)


def _gla_fwd(proj, log_a, g_gla, *, name):
    s = proj.shape[0]
    rows = min(GLA_ROWS, s)
    cb = rows // CHUNK
    nblk = s // rows
    scale = GLA_DK ** -0.5

    def body(q_ref, k_ref, v_ref, gr_ref, la_ref, g_ref, o_ref, n_ref, st_ref, st_sc):
        h = pl.program_id(0)

        @pl.when(pl.program_id(1) == 0)
        def _():
            st_sc[...] = jnp.zeros_like(st_sc)

        gv = g_ref[pl.ds(h, 1), :]
        for ci in range(cb):
            sl = slice(ci * CHUNK, (ci + 1) * CHUNK)
            e, dec = _gla_chunk_terms(la_ref[sl, :])
            k_dec = (k_ref[sl, :].astype(F32) * e).astype(BF16)
            st = st_sc[...] * dec + _dot_tn(v_ref[sl, :], k_dec)
            st_sc[...] = st
            st_ref[0, ci] = st
            qs = (q_ref[sl, :].astype(F32) * scale).astype(BF16)
            o = _dot_nt(qs, st.astype(BF16))
            o_ref[sl, :] = o
            r = lax.rsqrt(jnp.mean(o * o, axis=-1, keepdims=True) + EPS)
            grv = gr_ref[sl, :].astype(F32)
            n_ref[sl, :] = (o * r * gv * (grv * _sigmoid(grv))).astype(BF16)

    return pl.pallas_call(
        body, grid=(GLA_HEADS, nblk),
        in_specs=[pl.BlockSpec((rows, GLA_DK), lambda h, n: (n, GQ_BLK + h)),
                  pl.BlockSpec((rows, GLA_DK), lambda h, n: (n, GK_BLK + h)),
                  pl.BlockSpec((rows, GLA_DV), lambda h, n: (n, GV_BLK + h)),
                  pl.BlockSpec((rows, GLA_DV), lambda h, n: (n, GR_BLK + h)),
                  pl.BlockSpec((rows, GLA_DK), lambda h, n: (n, h)),
                  pl.BlockSpec((GLA_HEADS, GLA_DV), lambda h, n: (0, 0))],
        out_specs=[pl.BlockSpec((rows, GLA_DV), lambda h, n: (n, h)),
                   pl.BlockSpec((rows, GLA_DV), lambda h, n: (n, h)),
                   pl.BlockSpec((1, cb, GLA_DV, GLA_DK), lambda h, n: (h, n, 0, 0))],
        out_shape=[jax.ShapeDtypeStruct((s, GLA_W), F32), jax.ShapeDtypeStruct((s, GLA_W), BF16),
                   jax.ShapeDtypeStruct((GLA_HEADS, s // CHUNK, GLA_DV, GLA_DK), F32)],
        scratch_shapes=[pltpu.VMEM((GLA_DV, GLA_DK), F32)],
        compiler_params=_cp("parallel", "arbitrary"), name=name,
    )(proj, proj, proj, proj, log_a, g_gla)


def _gla_bwd(proj, log_a, do, states, *, name):
    s = proj.shape[0]
    rows = min(GLA_ROWS, s)
    cb = rows // CHUNK
    nblk = s // rows
    scale = GLA_DK ** -0.5

    def body(q_ref, k_ref, v_ref, la_ref, do_ref, st_ref, prev_ref, dq_ref, dk_ref, dv_ref, dla_ref, g_sc):
        nrev = pl.program_id(1)
        blk = nblk - 1 - nrev

        @pl.when(nrev == 0)
        def _():
            g_sc[...] = jnp.zeros_like(g_sc)

        for ci in reversed(range(cb)):
            sl = slice(ci * CHUNK, (ci + 1) * CHUNK)
            e, dec = _gla_chunk_terms(la_ref[sl, :])
            kd = k_ref[sl, :].astype(F32) * e
            qs = (q_ref[sl, :].astype(F32) * scale).astype(BF16)
            dov = do_ref[sl, :]
            st = st_ref[0, ci]
            if ci > 0:
                st_prev = st_ref[0, ci - 1]
            else:
                st_prev = prev_ref[0, 0] * (blk > 0).astype(F32)
            dq_ref[sl, :] = (_dot_nn(dov, st.astype(BF16)) * scale).astype(BF16)
            gt = g_sc[...] + _dot_tn(dov, qs)
            gtb = gt.astype(BF16)
            dkd = _dot_nn(v_ref[sl, :], gtb)
            dv_ref[sl, :] = _dot_nt(kd.astype(BF16), gtb).astype(BF16)
            dk_ref[sl, :] = (dkd * e).astype(BF16)
            ddec = jnp.sum(gt * st_prev, axis=0, keepdims=True) * dec
            dla_ref[sl, :] = _dot_nn(_tri(CHUNK, strict=True), dkd * kd, HIGHEST) + ddec
            g_sc[...] = gt * dec

    rev = lambda col0: (lambda h, n: (nblk - 1 - n, col0 + h))
    return pl.pallas_call(
        body, grid=(GLA_HEADS, nblk),
        in_specs=[pl.BlockSpec((rows, GLA_DK), rev(GQ_BLK)),
                  pl.BlockSpec((rows, GLA_DK), rev(GK_BLK)),
                  pl.BlockSpec((rows, GLA_DV), rev(GV_BLK)),
                  pl.BlockSpec((rows, GLA_DK), rev(0)),
                  pl.BlockSpec((rows, GLA_DV), rev(0)),
                  pl.BlockSpec((1, cb, GLA_DV, GLA_DK), lambda h, n: (h, nblk - 1 - n, 0, 0)),
                  pl.BlockSpec((1, 1, GLA_DV, GLA_DK),
                               lambda h, n: (h, jnp.maximum((nblk - 1 - n) * cb - 1, 0), 0, 0))],
        out_specs=[pl.BlockSpec((rows, GLA_DK), rev(0)), pl.BlockSpec((rows, GLA_DK), rev(0)),
                   pl.BlockSpec((rows, GLA_DV), rev(0)), pl.BlockSpec((rows, GLA_DK), rev(0))],
        out_shape=[jax.ShapeDtypeStruct((s, GLA_KW), BF16), jax.ShapeDtypeStruct((s, GLA_KW), BF16),
                   jax.ShapeDtypeStruct((s, GLA_W), BF16), jax.ShapeDtypeStruct((s, GLA_KW), F32)],
        scratch_shapes=[pltpu.VMEM((GLA_DV, GLA_DK), F32)],
        compiler_params=_cp("parallel", "arbitrary"), name=name,
    )(proj, proj, proj, log_a, do, states, states)


def _row_tile(r):
    tr = min(ROW_TILE, r)
    while r % tr or tr % 8:
        tr -= 1
    return tr


def _adamw_math(w, g, m, v):
    m = ADAM_B1 * m + (1.0 - ADAM_B1) * g
    v = ADAM_B2 * v + (1.0 - ADAM_B2) * (g * g)
    m_hat = m / (1.0 - ADAM_B1 ** ADAM_STEP)
    v_hat = v / (1.0 - ADAM_B2 ** ADAM_STEP)
    delta = -ADAM_LR * (m_hat / (jnp.sqrt(v_hat) + ADAM_EPS) + ADAM_WD * w)
    return delta, m, v


COL_TILE = 256


def _tile_2d(r, c):
    if r % 8 == 0 and _row_tile(r) >= 64:
        return _row_tile(r), c
    assert c % COL_TILE == 0, (r, c)
    return r, COL_TILE


def _half_shape(shape):
    r, c = shape[-2:]
    return tuple(shape[:-2]) + ((r // 2, c) if _half_axis(r) == 0 else (r, c // 2))


def _adam(g, w, m, v, *, name):
    r, c = w.shape
    tr, tc = _tile_2d(r, c)

    def body(g_ref, w_ref, m_ref, v_ref, d_ref, mo_ref, vo_ref):
        d, mn, vn = _adamw_math(w_ref[...], g_ref[...], m_ref[...], v_ref[...])
        d_ref[...] = d
        mo_ref[...] = mn
        vo_ref[...] = vn

    spec = pl.BlockSpec((tr, tc), lambda i, j: (i, j))
    return pl.pallas_call(
        body, grid=(r // tr, c // tc), in_specs=[spec] * 4, out_specs=[spec] * 3,
        out_shape=[jax.ShapeDtypeStruct((r, c), F32)] * 3,
        compiler_params=_cp("parallel", "parallel"), name=name,
    )(g, w, m, v)


def _ada_grad_adam(c_all_t, dmod_cols, w, m, v, *, name):
    r, c = w.shape
    tr, tc = min(512, r), min(1024, c)

    def body(ct_ref, dm_ref, w_ref, m_ref, v_ref, g_ref, d_ref, mo_ref, vo_ref):
        g = _dot_nn(ct_ref[...], dm_ref[...], HIGHEST)
        g_ref[...] = g
        d, mn, vn = _adamw_math(w_ref[...], g, m_ref[...], v_ref[...])
        d_ref[...] = d
        mo_ref[...] = mn
        vo_ref[...] = vn

    spec = pl.BlockSpec((tr, tc), lambda i, j: (i, j))
    nb = c_all_t.shape[1]
    return pl.pallas_call(
        body, grid=(r // tr, c // tc),
        in_specs=[pl.BlockSpec((tr, nb), lambda i, j: (i, 0)), pl.BlockSpec((nb, tc), lambda i, j: (0, j)),
                  spec, spec, spec],
        out_specs=[spec] * 4, out_shape=[jax.ShapeDtypeStruct((r, c), F32)] * 4,
        compiler_params=_cp("parallel", "parallel"), name=name,
    )(c_all_t, dmod_cols, w, m, v)


def _mod_shard(c_all, w, b, *, name):
    k, c = w.shape
    tc = min(512, c)
    nb = c_all.shape[0]

    def body(c_ref, w_ref, b_ref, o_ref):
        o_ref[...] = _dot_nn(c_ref[...], w_ref[...], HIGHEST) + b_ref[...]

    return pl.pallas_call(
        body, grid=(c // tc,),
        in_specs=[pl.BlockSpec((nb, k), lambda j: (0, 0)), pl.BlockSpec((k, tc), lambda j: (0, j)),
                  pl.BlockSpec((1, tc), lambda j: (0, j))],
        out_specs=pl.BlockSpec((nb, tc), lambda j: (0, j)),
        out_shape=jax.ShapeDtypeStruct((nb, c), F32),
        compiler_params=_cp("parallel"), name=name,
    )(c_all, w, b)


def _silu_rows(c, *, name):
    def body(c_ref, o_ref):
        cv = c_ref[...]
        o_ref[...] = cv * _sigmoid(cv)

    return pl.pallas_call(body, out_shape=jax.ShapeDtypeStruct(c.shape, F32), name=name)(c)


def _pair_sum(g, got, idx, *, name):
    p, r, c = g.shape
    ax = _half_axis(r)
    hr, hc = _half_shape((r, c))
    tr, tc = _tile_2d(hr, hc)
    nbr, nbc = hr // tr, hc // tc

    def body(idx_ref, a_ref, b_ref, o_ref):
        o_ref[...] = (a_ref[...].astype(F32) + b_ref[...].astype(F32)).astype(BF16)

    def slot(i, idx_ref):
        return i + jnp.where(i >= idx_ref[1], 1, 0)

    def own_map(i, j, k, idx_ref):
        return (slot(i, idx_ref), j + (idx_ref[0] * nbr if ax == 0 else 0), k + (idx_ref[0] * nbc if ax == 1 else 0))

    half_spec = pl.BlockSpec((1, tr, tc), lambda i, j, k, idx_ref: (slot(i, idx_ref), j, k))
    return pl.pallas_call(
        body,
        grid_spec=pltpu.PrefetchScalarGridSpec(
            num_scalar_prefetch=1, grid=(p - 1, nbr, nbc),
            in_specs=[pl.BlockSpec((1, tr, tc), own_map), half_spec],
            out_specs=half_spec),
        out_shape=jax.ShapeDtypeStruct((p, hr, hc), BF16),
        compiler_params=_cp("parallel", "parallel", "parallel"), name=name,
    )(idx, g, got)


def _final_sum(g, got, parts, idx, *, name):
    shard_shape = g.shape[1:]
    ax = _half_axis(shard_shape[0])
    hr, hc = got.shape[1:]
    tr, tc = _tile_2d(hr, hc)
    nbr, nbc = hr // tr, hc // tc

    def body(idx_ref, g_ref, got_ref, parts_ref, o_ref):
        acc = g_ref[0].astype(F32) + got_ref[0].astype(F32)
        for q in range(3):
            acc = acc + parts_ref[q].astype(F32)
        o_ref[...] = acc

    def half_c(j, k, idx_ref):
        return (j + (idx_ref[0] * nbr if ax == 0 else 0), k + (idx_ref[0] * nbc if ax == 1 else 0))

    return pl.pallas_call(
        body,
        grid_spec=pltpu.PrefetchScalarGridSpec(
            num_scalar_prefetch=1, grid=(nbr, nbc),
            in_specs=[pl.BlockSpec((1, tr, tc), lambda j, k, idx_ref: (idx_ref[1],) + half_c(j, k, idx_ref)),
                      pl.BlockSpec((1, tr, tc), lambda j, k, idx_ref: (idx_ref[1], j, k)),
                      pl.BlockSpec((3, tr, tc), lambda j, k, idx_ref: (0, j, k))],
            out_specs=pl.BlockSpec((tr, tc), half_c)),
        out_shape=jax.ShapeDtypeStruct(tuple(shard_shape), F32),
        compiler_params=_cp("parallel", "parallel"), name=name,
    )(idx, g, got, parts)


def _stack_sum(x, *, name):
    p, r, c = x.shape
    tr = _row_tile(r)

    def body(x_ref, o_ref):
        acc = x_ref[0].astype(F32)
        for q in range(1, p):
            acc = acc + x_ref[q].astype(F32)
        o_ref[...] = acc

    return pl.pallas_call(
        body, grid=(r // tr,),
        in_specs=[pl.BlockSpec((p, tr, c), lambda i: (0, i, 0))],
        out_specs=pl.BlockSpec((tr, c), lambda i: (i, 0)),
        out_shape=jax.ShapeDtypeStruct((r, c), F32),
        compiler_params=_cp("parallel"), name=name,
    )(x)


def _place():
    x, y, c = lax.axis_index("x"), lax.axis_index("y"), lax.axis_index("c")
    chips = [(1 - x, y), (x, 1 - y), (1 - x, 1 - y)]
    return x, y, c, chips


def _gather8(x_shard, *, name):
    m_per, n = x_shard.shape

    def body(x_ref, out_ref, send_sems, recv_sems, local_sem):
        x, y, c, chips = _place()
        me, sibling = (x, y, c), (x, y, 1 - c)

        def rows(px, py, pc):
            return out_ref.at[pl.ds((4 * px + 2 * py + pc) * m_per, m_per), :]

        def copy(k, block, to, src=None):
            return pltpu.make_async_remote_copy(
                src_ref=rows(*block) if src is None else src, dst_ref=rows(*block),
                send_sem=send_sems.at[k], recv_sem=recv_sems.at[k], device_id=to, device_id_type=MESH)

        mine = pltpu.make_async_copy(x_ref, rows(*me), local_sem)
        mine.start()
        first = [copy(0, me, sibling, src=x_ref)]
        first += [copy(1 + j, me, (*chip, c), src=x_ref) for j, chip in enumerate(chips)]
        for cp in first:
            cp.start()
        passed = [copy(4 + j, (*chip, c), sibling) for j, chip in enumerate(chips)]
        for j, chip in enumerate(chips):
            copy(1 + j, (*chip, c), me).wait_recv()
            passed[j].start()
        copy(0, sibling, me).wait_recv()
        for j, chip in enumerate(chips):
            copy(4 + j, (*chip, 1 - c), me).wait_recv()
        for cp in first + passed:
            cp.wait_send()
        mine.wait()

    return pl.pallas_call(
        body,
        out_shape=jax.ShapeDtypeStruct((8 * m_per, n), x_shard.dtype),
        in_specs=[pl.BlockSpec(memory_space=pltpu.VMEM)],
        out_specs=pl.BlockSpec(memory_space=pltpu.VMEM),
        scratch_shapes=[pltpu.SemaphoreType.DMA((7,)), pltpu.SemaphoreType.DMA((7,)), pltpu.SemaphoreType.DMA],
        name=name,
    )(x_shard)


def _gather_relayed(shard, *, name):
    def body(shard_ref, full_ref, send_sems, recv_sems):
        x, y, c, _ = _place()
        ax = _half_axis(shard_ref.shape[0])
        me, xn, yn, dg = 2 * x + y, 2 * (1 - x) + y, 2 * x + (1 - y), 2 * (1 - x) + (1 - y)
        to_x, to_y = (1 - x, y, c), (x, 1 - y, c)

        def half(slot, piece=None):
            return _rows_half(full_ref.at[slot], c, ax, piece)

        def copy(k, src, dst, peer):
            return pltpu.make_async_remote_copy(src_ref=src, dst_ref=dst, send_sem=send_sems.at[k],
                                                recv_sem=recv_sems.at[k], device_id=peer, device_id_type=MESH)

        mine = _rows_half(shard_ref, c, ax)
        sends = [copy(0, mine, half(me), to_x), copy(1, mine, half(me), to_y)]
        for cp in sends:
            cp.start()
        copy(0, mine, half(xn), to_x).wait_recv()
        relay_y = copy(2, half(xn, (0, 2)), half(xn, (0, 2)), to_y)
        relay_y.start()
        copy(1, mine, half(yn), to_y).wait_recv()
        relay_x = copy(3, half(yn, (1, 2)), half(yn, (1, 2)), to_x)
        relay_x.start()
        copy(2, half(xn, (0, 2)), half(dg, (0, 2)), to_y).wait_recv()
        copy(3, half(yn, (1, 2)), half(dg, (1, 2)), to_x).wait_recv()
        for cp in sends + [relay_y, relay_x]:
            cp.wait_send()

    any_spec = pl.BlockSpec(memory_space=pl.ANY)
    return pl.pallas_call(
        body, out_shape=jax.ShapeDtypeStruct((4,) + shard.shape, shard.dtype),
        in_specs=[any_spec], out_specs=any_spec,
        scratch_shapes=[pltpu.SemaphoreType.DMA((4,)), pltpu.SemaphoreType.DMA((4,))], name=name,
    )(shard)


def _plan_start(plan, send_sems, recv_sems):
    for k, (src, dst, _, peer) in enumerate(plan):
        pltpu.make_async_remote_copy(src_ref=src, dst_ref=dst, send_sem=send_sems.at[k], recv_sem=recv_sems.at[k],
                                     device_id=peer, device_id_type=MESH).start()


def _plan_wait(plan, send_sems, recv_sems):
    for k, (src, _, land, peer) in enumerate(plan):
        pltpu.make_async_remote_copy(src_ref=src, dst_ref=land, send_sem=send_sems.at[k], recv_sem=recv_sems.at[k],
                                     device_id=peer, device_id_type=MESH).wait_recv()
    for k, (src, dst, _, peer) in enumerate(plan):
        pltpu.make_async_remote_copy(src_ref=src, dst_ref=dst, send_sem=send_sems.at[k], recv_sem=recv_sems.at[k],
                                     device_id=peer, device_id_type=MESH).wait_send()


def _half_axis(rows):
    return 0 if rows % 32 == 0 else 1


def _rows_half(ref, hc, axis, part=None):
    size = ref.shape[axis] // 2
    start = hc * size
    if part is not None:
        size //= part[1]
        start = start + part[0] * size
    idx = [slice(None)] * len(ref.shape)
    idx[axis] = pl.ds(start, size)
    return ref.at[tuple(idx)]


def _plan_gather_ici(shard, full, part=None):
    x, y, c, chips = _place()
    ax = _half_axis(shard.shape[0])
    src = _rows_half(shard, c, ax, part)
    return [(src, _rows_half(full.at[2 * x + y], c, ax, part), _rows_half(full.at[2 * cx + cy], c, ax, part),
             (cx, cy, c)) for cx, cy in chips]


def _plan_gather_d2d(full, own):
    x, y, c, chips = _place()
    ax = _half_axis(full.shape[1])
    plan = []
    for cx, cy in chips:
        slot = full.at[2 * cx + cy]
        plan.append((_rows_half(slot, c, ax), _rows_half(slot, c, ax), _rows_half(slot, 1 - c, ax), (x, y, 1 - c)))
    mine = full.at[2 * x + y]
    plan.append((own, mine, mine, (x, y, 1 - c)))
    return plan


def _plan_pair(grad, got):
    x, y, c, _ = _place()
    return [(_rows_half(grad, 1 - c, 1 + _half_axis(grad.shape[1])), got, got, (x, y, 1 - c))]


def _plan_shard_ici(sums, parts, piece=None):
    _, _, c, chips = _place()

    def rows(ref):
        if piece is None:
            return ref
        k, n = piece
        if ref.shape[0] % (16 * n) == 0:
            size = ref.shape[0] // n
            return ref.at[pl.ds(k * size, size), :]
        size = ref.shape[1] // n
        return ref.at[:, pl.ds(k * size, size)]

    return [(rows(sums.at[2 * cx + cy]), rows(parts.at[k]), rows(parts.at[k]), (cx, cy, c))
            for k, (cx, cy) in enumerate(chips)]


def _plan_half(buf):
    x, y, c, _ = _place()
    ax = _half_axis(buf.shape[0])
    mine = _rows_half(buf, c, ax)
    return [(mine, mine, _rows_half(buf, 1 - c, ax), (x, y, 1 - c))]


def _comm_call(plan_fn, inputs, out_shapes, *, name, aliases=None):
    ni, no = len(inputs), len(out_shapes)

    def body(*refs):
        plan = plan_fn(refs[:ni], refs[ni:ni + no])
        send_sems, recv_sems = refs[ni + no:]
        _plan_start(plan, send_sems, recv_sems)
        _plan_wait(plan, send_sems, recv_sems)

    any_spec = pl.BlockSpec(memory_space=pl.ANY)
    n_copies = 3 * max(ni, no)
    return pl.pallas_call(
        body, out_shape=list(out_shapes), in_specs=[any_spec] * ni, out_specs=[any_spec] * no,
        scratch_shapes=[pltpu.SemaphoreType.DMA((n_copies,)), pltpu.SemaphoreType.DMA((n_copies,))],
        input_output_aliases=aliases or {}, name=name,
    )(*inputs)


def _gather_forward(full, own, *, name):
    return _comm_call(lambda ins, outs: _plan_gather_d2d(outs[0], ins[1]), [full, own],
                      [jax.ShapeDtypeStruct(full.shape, full.dtype)], name=name, aliases={0: 0})[0]


def _half_exchange(bufs, *, name):
    return _comm_call(lambda ins, outs: [cp for o in outs for cp in _plan_half(o)],
                      bufs, [jax.ShapeDtypeStruct(b.shape, b.dtype) for b in bufs], name=name,
                      aliases={k: k for k in range(len(bufs))})


def _split_w_in(w_in_t):
    d = w_in_t.shape[1]
    main = jnp.concatenate([w_in_t[0:3072], w_in_t[3080:5128], w_in_t[5144:6168]], axis=0)
    small = jnp.concatenate([w_in_t[3072:3080], w_in_t[5128:5144], jnp.zeros((SMALL_W - 24, d), w_in_t.dtype)], axis=0)
    return main, small


def _merge_dw_in(dw_main, dw_small):
    return jnp.concatenate([dw_main[0:3072], dw_small[0:8], dw_main[3072:5120], dw_small[8:24], dw_main[5120:6144]],
                           axis=0)


def _gather_side(shards):
    return _Side(shards, [jax.ShapeDtypeStruct((4,) + w.shape, w.dtype) for w in shards],
                 lambda ins, outs: [cp for i, o in zip(ins, outs) for cp in _plan_gather_ici(i, o)], 3 * len(shards))


def _forward_side(full, own):
    return _Side([full, own], [jax.ShapeDtypeStruct(full.shape, full.dtype)],
                 lambda ins, outs: _plan_gather_d2d(outs[0], ins[1]), 4, aliases={0: 0})


def _half_side(bufs):
    return _Side(bufs, [jax.ShapeDtypeStruct(b.shape, b.dtype) for b in bufs],
                 lambda ins, outs: [cp for o in outs for cp in _plan_half(o)], len(bufs),
                 aliases={k: k for k in range(len(bufs))})


def _parts_shape(sums):
    return jax.ShapeDtypeStruct((3,) + sums.shape[1:], sums.dtype)


def _got_shape(grad):
    return jax.ShapeDtypeStruct(_half_shape(grad.shape), grad.dtype)


def _pair_side(grad):
    return _Side([grad], [_got_shape(grad)], lambda ins, outs: _plan_pair(ins[0], outs[0]), 1)


def _local_step(x, target, mod, g_pre_mix, g_post_mix, g_pre_mlp, g_post_mlp, gw_in, b_fgate, w_gla_a2,
                b_gla_a2, g_fox, g_gla, own_w_in, own_w_out, own_w_mlp_in, own_w_mlp_out, idx):
    s, d = x.shape
    shift_m, scale_m, gate_m, shift_f, scale_f, gate_f = [mod[:, i * d:(i + 1) * d] for i in range(6)]
    a1 = g_pre_mix * (1.0 + scale_m)
    a2 = g_pre_mlp * (1.0 + scale_f)
    bf = jnp.concatenate([b_fgate, jnp.zeros((1, SMALL_W - FOX_HEADS), F32)], axis=1)
    w2p = jnp.zeros((SMALL_W, GLA_KW), F32).at[FOX_HEADS:FOX_HEADS + GLA_RANK].set(w_gla_a2)

    h1, gw_in = _pre_fwd(x, a1, shift_m, name="pre_mix_fwd", side=_forward_side(gw_in, own_w_in))
    w_in_t = gw_in.reshape(-1, d)
    w_main, w_small = _split_w_in(w_in_t)
    full_shape = lambda w: jax.ShapeDtypeStruct((4,) + w.shape, w.dtype)
    first_side = _Side(
        [own_w_out, own_w_mlp_out], [full_shape(own_w_out), full_shape(own_w_mlp_out)],
        lambda ins, outs: _plan_gather_ici(ins[0], outs[0]) + _plan_gather_ici(ins[1], outs[1], part=(0, 4)), 6)
    proj, gw_out, gw_mlp_out = _mm(h1, w_main, mode="nt", out_dtypes=[BF16], name="in_proj_main", side=first_side)
    ps, gw_out = _mm(h1, w_small, mode="nt", out_dtypes=[F32], name="in_proj_small",
                     side=_forward_side(gw_out, own_w_out))
    w_out_full = gw_out.reshape(-1, d)
    cum, log_a = _gates_fwd(ps, bf, w2p, b_gla_a2, name="gates_fwd")
    cum_t = cum[:, :FOX_HEADS].T
    o_fox, fox_n, lse, gw_mlp_in = _fox_fwd(proj, cum_t, g_fox, name="fox_fwd", side=_gather_side([own_w_mlp_in]))
    o_gla, gla_n, states = _gla_fwd(proj, log_a, g_gla, name="gla_fwd")
    mixed = jnp.concatenate([fox_n, gla_n], axis=1)
    y1, gw_mlp_in = _mm(mixed, w_out_full, mode="nn", out_dtypes=[F32], name="out_proj",
                        side=_forward_side(gw_mlp_in, own_w_mlp_in))
    x1, h2 = _post_pre_fwd(x, y1, gate_m, g_post_mix, a2, shift_f, name="post_mix_pre_mlp_fwd")

    def mlp_act(acc):
        r = jnp.maximum(acc, 0.0)
        return acc, r * r

    rest_side = _Side([own_w_mlp_out, gw_mlp_out], [full_shape(own_w_mlp_out)],
                      lambda ins, outs: [cp for q in (1, 2, 3) for cp in _plan_gather_ici(ins[0], outs[0], part=(q, 4))],
                      9, aliases={1: 0})
    u, act, gw_mlp_out = _mm(h2, gw_mlp_in, mode="nn", out_dtypes=[BF16, BF16], epi=mlp_act, name="mlp_in",
                             b_slots=4, tm=MM_TM, side=rest_side)
    gw_mlp_out = _gather_forward(gw_mlp_out, own_w_mlp_out, name="gather_w_mlp_out_d2d")
    w_mlp_out_full = gw_mlp_out.reshape(-1, d)
    y2, = _mm(act, w_mlp_out_full, mode="nn", out_dtypes=[F32], name="mlp_out")
    dx2, dy2, loss_part, dgate_f, dg_post_mlp = _post_loss_bwd(x1, y2, gate_f, g_post_mlp, target,
                                                               name="post_mlp_loss_bwd")
    dw_mlp_out, = _mm(act, dy2, mode="tn", out_dtypes=[BF16], name="dw_mlp_out", tk=MM_TK_TOKENS)
    dw_mlp_out = dw_mlp_out.reshape(4, D_FF // 4, d)

    def act_bwd(acc, uv):
        return (acc * (2.0 * jnp.maximum(uv.astype(F32), 0.0)),)

    du, got_mlp_out = _mm(dy2, w_mlp_out_full, mode="nt", out_dtypes=[BF16], extras=[u], epi=act_bwd,
                          name="d_mlp_hidden", tm=MM_TM, side=_pair_side(dw_mlp_out))
    sum_mlp_out = _pair_sum(dw_mlp_out, got_mlp_out, idx, name="grad_pair_sum_mlp_out")
    nj = D_FF // 4 // min(MM_T, D_FF // 4)
    tmw = min(MM_T, d)
    dw_mlp_in, parts_mlp_out = _mm(
        h2, du, mode="tn", out_dtypes=[BF16], name="dw_mlp_in", tk=MM_TK_TOKENS,
        out_shapes=[jax.ShapeDtypeStruct((4, d, D_FF // 4), BF16)],
        out_specs=[pl.BlockSpec((1, tmw, min(MM_T, D_FF // 4)), lambda i, j, kk: (j // nj, i, j % nj))],
        side=_Side([sum_mlp_out], [_parts_shape(sum_mlp_out)],
                   lambda ins, outs: _plan_shard_ici(ins[0], outs[0], piece=(0, 2)), 3))
    dh2, got_mlp_in, parts_mlp_out = _mm(
        du, gw_mlp_in, mode="nt", out_dtypes=[F32], name="d_mlp_in", b_slots=4,
        side=_Side([dw_mlp_in, sum_mlp_out, parts_mlp_out], [_got_shape(dw_mlp_in), _parts_shape(sum_mlp_out)],
                   lambda ins, outs: _plan_pair(ins[0], outs[0]) + _plan_shard_ici(ins[1], outs[1], piece=(1, 2)),
                   4, aliases={2: 1}))
    sum_mlp_in = _pair_sum(dw_mlp_in, got_mlp_in, idx, name="grad_pair_sum_mlp_in")
    dx1, dshift_f, da2, dy1, dgate_m, dg_post_mix = _pre_post_bwd(dh2, x1, dx2, a2, y1, gate_m, g_post_mix,
                                                                  name="pre_mlp_post_mix_bwd")
    buf_mlp_out = _final_sum(dw_mlp_out, got_mlp_out, parts_mlp_out, idx, name="grad_final_sum_mlp_out")
    dw_out, g_mlp_out = _mm(mixed, dy1, mode="tn", out_dtypes=[BF16], name="dw_out", tk=MM_TK_TOKENS,
                            side=_half_side([buf_mlp_out]))
    dw_out = dw_out.reshape(4, d // 4, d)
    dmixed, got_out = _mm(dy1, w_out_full, mode="nt", out_dtypes=[BF16], name="d_mixed", side=_pair_side(dw_out))
    sum_out = _pair_sum(dw_out, got_out, idx, name="grad_pair_sum_out")
    do_fox, delta, dg_fox = _head_norm_bwd(dmixed, o_fox, g_fox, None, nh=FOX_HEADS, hd=FOX_HD, dn_col=0,
                                           gr_col=0, name="fox_norm_bwd")
    do_gla, dgr, _, dg_gla = _head_norm_bwd(dmixed, o_gla, g_gla, proj, nh=GLA_HEADS, hd=GLA_DV, dn_col=1,
                                            gr_col=(3 * FOX_W + 2 * GLA_KW + GLA_W) // GLA_W, name="gla_norm_bwd")
    dq_fox, dk_fox, dv_fox, dcq, dck_t, parts_mlp_in, parts_out = _fox_bwd(
        proj, do_fox, cum_t, lse, delta, name="fox_bwd",
        side=_Side([sum_mlp_in, sum_out], [_parts_shape(sum_mlp_in), _parts_shape(sum_out)],
                   lambda ins, outs: _plan_shard_ici(ins[0], outs[0]) + _plan_shard_ici(ins[1], outs[1]), 6))
    dgq, dgk, dgv, dla = _gla_bwd(proj, log_a, do_gla, states, name="gla_bwd")
    dck = dcq + jnp.concatenate([dck_t.T, jnp.zeros((s, SMALL_W - FOX_HEADS), F32)], axis=1)
    dps, dbf, dw2p, db2 = _gates_bwd(dck, ps, bf, w2p, b_gla_a2, dla, name="gates_bwd")
    dproj = jnp.concatenate([dq_fox.astype(BF16), dk_fox, dv_fox, dgq, dgk, dgv, dgr], axis=1)
    buf_mlp_in = _final_sum(dw_mlp_in, got_mlp_in, parts_mlp_in, idx, name="grad_final_sum_mlp_in")
    buf_out = _final_sum(dw_out, got_out, parts_out, idx, name="grad_final_sum_out")
    dw_main, g_mlp_in, g_out = _mm(dproj, h1, mode="tn", out_dtypes=[BF16], name="dw_in_main", tk=MM_TK_TOKENS,
                                   side=_half_side([buf_mlp_in, buf_out]))
    dw_small, = _mm(dps, h1, mode="tn", out_dtypes=[BF16], name="dw_in_small")
    rs_in = w_in_t.shape[0] // 4
    dw_in = _merge_dw_in(dw_main, dw_small).reshape(4, rs_in, d)
    dh1_small, got_in = _mm(dps, w_small, mode="nn", out_dtypes=[F32], name="d_h1_small", side=_pair_side(dw_in))
    sum_in = _pair_sum(dw_in, got_in, idx, name="grad_pair_sum_in")
    dh1, parts_in = _mm(
        dproj, w_main, mode="nn", out_dtypes=[F32], extras=[dh1_small], epi=lambda acc, e: (acc + e,), name="d_h1",
        side=_Side([sum_in], [_parts_shape(sum_in)],
                   lambda ins, outs: [cp for q in range(3) for cp in _plan_shard_ici(ins[0], outs[0], piece=(q, 4))],
                   9))
    grad_x, dshift_m, da1, parts_in = _pre_bwd(
        dh1, x, dx1, a1, name="pre_mix_bwd",
        side=_Side([sum_in, parts_in], [_parts_shape(sum_in)],
                   lambda ins, outs: _plan_shard_ici(ins[0], outs[0], piece=(3, 4)), 3, aliases={1: 0}))
    buf_in = _final_sum(dw_in, got_in, parts_in, idx, name="grad_final_sum_in")
    g_in, = _half_exchange([buf_in], name="grad_half_exchange_in")
    g_big = [g_in, g_out, g_mlp_in, g_mlp_out]

    dmod = jnp.concatenate([dshift_m, da1 * g_pre_mix, dgate_m, dshift_f, da2 * g_pre_mlp, dgate_f], axis=1)
    small = dict(
        dmod=dmod, g_pre_mix=da1 * (1.0 + scale_m), g_post_mix=dg_post_mix, g_pre_mlp=da2 * (1.0 + scale_f),
        g_post_mlp=dg_post_mlp, b_fgate=dbf[:, :FOX_HEADS], w_gla_a2=dw2p[FOX_HEADS:FOX_HEADS + GLA_RANK],
        b_gla_a2=db2, g_fox_out=dg_fox, g_gla_out=dg_gla)
    return loss_part, grad_x, g_big, small


def _pack(arrays):
    flat = jnp.concatenate([a.reshape(-1).astype(F32) for a in arrays])
    n = flat.shape[0]
    rows = -(-n // 128)
    rows = -(-rows // 8) * 8
    return jnp.pad(flat, (0, rows * 128 - n)).reshape(rows, 128)


def _unpack(buf, shapes):
    flat = buf.reshape(-1)
    out, off = [], 0
    for shp in shapes:
        n = 1
        for q in shp:
            n *= q
        out.append(flat[off:off + n].reshape(shp))
        off += n
    return out


SMALL_GRAD_ORDER = ["dmod", "g_pre_mix", "g_post_mix", "g_pre_mlp", "g_post_mlp", "b_fgate", "w_gla_a2", "b_gla_a2",
                    "g_fox_out", "g_gla_out"]


def kernel(x, c, w_ada, b_ada, g_pre_mix, g_post_mix, w_in, b_fgate, w_gla_a2, b_gla_a2, g_fox_out, g_gla_out, w_out, g_pre_mlp, g_post_mlp, w_mlp_in, w_mlp_out, loss_target, m_w_ada, m_b_ada, m_g_pre_mix, m_g_post_mix, m_w_in, m_b_fgate, m_w_gla_a2, m_b_gla_a2, m_g_fox_out, m_g_gla_out, m_w_out, m_g_pre_mlp, m_g_post_mlp, m_w_mlp_in, m_w_mlp_out, v_w_ada, v_b_ada, v_g_pre_mix, v_g_post_mix, v_w_in, v_b_fgate, v_w_gla_a2, v_b_gla_a2, v_g_fox_out, v_g_gla_out, v_w_out, v_g_pre_mlp, v_g_post_mlp, v_w_mlp_in, v_w_mlp_out):
    ix, iy, ic = lax.axis_index("x"), lax.axis_index("y"), lax.axis_index("c")
    chip = 2 * ix + iy
    dev = 4 * ix + 2 * iy + ic
    d = D_MODEL

    c_act = _silu_rows(c, name="silu_c")
    pack1 = _pack([c_act, w_gla_a2[0], g_gla_out[0]])
    rows1 = pack1.shape[0]
    got1 = _gather8(pack1, name="gather_small_fwd").reshape(8, rows1, 128)
    per_dev = [_unpack(got1[q], [(d,), (GLA_RANK, GLA_KW // 4), (GLA_HEADS, GLA_DV // 4)]) for q in range(8)]
    c_all = jnp.stack([p[0] for p in per_dev])
    w_gla_a2_full = jnp.concatenate([per_dev[2 * j][1] for j in range(4)], axis=1)
    g_gla_full = jnp.concatenate([per_dev[2 * j][2] for j in range(4)], axis=1)
    cols = w_ada.shape[2]
    b_ada_shard = lax.dynamic_slice_in_dim(b_ada, chip * cols, cols, axis=1)
    mod_sh = _mod_shard(c_all, w_ada[0], b_ada_shard, name="ada_mod")
    got2 = _gather8(mod_sh, name="gather_mod").reshape(8, 8, cols)
    mod_all = jnp.concatenate([got2[2 * j] for j in range(4)], axis=1)
    mod = lax.dynamic_slice_in_dim(mod_all, dev, 1, axis=0)

    tr_in = lambda a: jnp.transpose(a[0])
    own_bf = [tr_in(w_in).astype(BF16), w_out[0].astype(BF16), w_mlp_in[0].astype(BF16), w_mlp_out[0].astype(BF16)]
    gw_in = _gather_relayed(own_bf[0], name="gather_w_in_ici")
    idx = jnp.stack([ic, chip]).astype(jnp.int32)
    loss_part, grad_x, g_big, small = _local_step(
        x[0], loss_target[0], mod, g_pre_mix, g_post_mix, g_pre_mlp, g_post_mlp, gw_in, b_fgate,
        w_gla_a2_full, b_gla_a2, g_fox_out[0], g_gla_full, own_bf[0], own_bf[1], own_bf[2], own_bf[3], idx)
    loss = lax.psum(loss_part[0, 0], ("x", "y", "c"))

    big_w = [(tr_in(w_in), tr_in(m_w_in), tr_in(v_w_in)), (w_out[0], m_w_out[0], v_w_out[0]),
             (w_mlp_in[0], m_w_mlp_in[0], v_w_mlp_in[0]), (w_mlp_out[0], m_w_mlp_out[0], v_w_mlp_out[0])]
    big_res = []
    for q, (g, (w, m, v)) in enumerate(zip(g_big, big_w)):
        res4 = (g,) + tuple(_adam(g, w, m, v, name=f"adam_big_{q}"))
        big_res.append(tuple((jnp.transpose(a) if q == 0 else a)[None] for a in res4))

    pack2 = _pack([small[k] for k in SMALL_GRAD_ORDER])
    rows2 = pack2.shape[0]
    got3 = _gather8(pack2, name="gather_small_grads").reshape(8, rows2, 128)
    dmod_all = got3[:, :6 * d // 128, :].reshape(8, 6 * d)
    sums = _stack_sum(got3, name="small_grad_sum")
    shapes = [(1, 6 * d), (1, d), (1, d), (1, d), (1, d), (1, FOX_HEADS), (1, GLA_RANK, GLA_KW), (1, GLA_KW),
              (1, FOX_HEADS, FOX_HD), (1, GLA_HEADS, GLA_DV)]
    sg = dict(zip(["b_ada"] + SMALL_GRAD_ORDER[1:], _unpack(sums, shapes)))
    sg["w_gla_a2"] = lax.dynamic_slice_in_dim(sg["w_gla_a2"], chip * (GLA_KW // 4), GLA_KW // 4, axis=2)
    sg["g_gla_out"] = lax.dynamic_slice_in_dim(sg["g_gla_out"], chip * (GLA_DV // 4), GLA_DV // 4, axis=2)
    small_names = ["b_ada", "g_pre_mix", "g_post_mix", "b_fgate", "w_gla_a2", "b_gla_a2", "g_fox_out", "g_gla_out",
                   "g_pre_mlp", "g_post_mlp"]
    small_w = dict(b_ada=(b_ada, m_b_ada, v_b_ada), g_pre_mix=(g_pre_mix, m_g_pre_mix, v_g_pre_mix),
                   g_post_mix=(g_post_mix, m_g_post_mix, v_g_post_mix), b_fgate=(b_fgate, m_b_fgate, v_b_fgate),
                   w_gla_a2=(w_gla_a2, m_w_gla_a2, v_w_gla_a2), b_gla_a2=(b_gla_a2, m_b_gla_a2, v_b_gla_a2),
                   g_fox_out=(g_fox_out, m_g_fox_out, v_g_fox_out), g_gla_out=(g_gla_out, m_g_gla_out, v_g_gla_out),
                   g_pre_mlp=(g_pre_mlp, m_g_pre_mlp, v_g_pre_mlp), g_post_mlp=(g_post_mlp, m_g_post_mlp, v_g_post_mlp))
    sshapes = [small_w[k][0].shape for k in small_names]
    pg = _pack([sg[k] for k in small_names])
    pw, pm, pv = [_pack([small_w[k][q] for k in small_names]) for q in range(3)]
    pd, pmn, pvn = _adam(pg, pw, pm, pv, name="adam_small")
    s_delta = dict(zip(small_names, _unpack(pd, sshapes)))
    s_m = dict(zip(small_names, _unpack(pmn, sshapes)))
    s_v = dict(zip(small_names, _unpack(pvn, sshapes)))

    dmod_cols = lax.dynamic_slice_in_dim(dmod_all, chip * cols, cols, axis=1)
    g_ada, d_ada, m_ada, v_ada = _ada_grad_adam(c_all.T, dmod_cols, w_ada[0], m_w_ada[0], v_w_ada[0], name="ada_grad_adam")

    order = ["w_ada", "b_ada", "g_pre_mix", "g_post_mix", "w_in", "b_fgate", "w_gla_a2", "b_gla_a2", "g_fox_out",
             "g_gla_out", "w_out", "g_pre_mlp", "g_post_mlp", "w_mlp_in", "w_mlp_out"]
    res = {"w_ada": (g_ada[None], d_ada[None], m_ada[None], v_ada[None]),
           "w_in": big_res[0], "w_out": big_res[1], "w_mlp_in": big_res[2], "w_mlp_out": big_res[3]}
    for k in small_names:
        res[k] = (sg[k], s_delta[k], s_m[k], s_v[k])
    return (loss, grad_x[None], *[res[k][0] for k in order], *[res[k][1] for k in order],
            *[res[k][2] for k in order], *[res[k][3] for k in order])
```

```python
import functools

import jax
import jax.numpy as jnp
from jax import lax
from jax.experimental import pallas as pl
from jax.experimental.pallas import tpu as pltpu

F32 = jnp.float32
BF16 = jnp.bfloat16
MESH = pl.DeviceIdType.MESH
HIGHEST = lax.Precision.HIGHEST

D_MODEL = 2048
FOX_HEADS = 8
FOX_HD = 128
FOX_W = FOX_HEADS * FOX_HD
GLA_HEADS = 4
GLA_DK = 128
GLA_DV = 256
GLA_KW = GLA_HEADS * GLA_DK
GLA_W = GLA_HEADS * GLA_DV
GLA_RANK = 16
GLA_TEMP = 16.0
CHUNK = 64
D_FF = 4 * D_MODEL
EPS = 1e-6
MAIN_W = 3 * FOX_W + 2 * GLA_KW + 2 * GLA_W
SMALL_W = 128
NEG = -1e30

ADAM_LR = 0.001
ADAM_B1 = 0.9
ADAM_B2 = 0.999
ADAM_EPS = 1e-08
ADAM_WD = 0.01
ADAM_STEP = 10

VMEM_LIMIT = 52 * 1024 * 1024
ROW_TILE = 256
WIDE_ROW_TILE = 512
FOX_TQ = 512
FOX_TK = 512
GLA_ROWS = 512
GATE_TS = 512
MM_T = 1024
MM_TK = 2048
MM_TK_TOKENS = 4096
MM_TM = 2048


def _cp(*sem):
    return pltpu.CompilerParams(dimension_semantics=sem, vmem_limit_bytes=VMEM_LIMIT)


def _dot_nn(a, b, precision=None):
    return jnp.dot(a, b, preferred_element_type=F32, precision=precision)


def _dot_nt(a, b, precision=None):
    return lax.dot_general(a, b, (((1,), (1,)), ((), ())), preferred_element_type=F32, precision=precision)


def _dot_tn(a, b, precision=None):
    return lax.dot_general(a, b, (((0,), (0,)), ((), ())), preferred_element_type=F32, precision=precision)


def _sigmoid(x):
    return 1.0 / (1.0 + jnp.exp(-x))


def _log_sigmoid(x):
    return jnp.minimum(x, 0.0) - jnp.log(1.0 + jnp.exp(-jnp.abs(x)))


class _Side:
    def __init__(self, inputs, out_shapes, plan_fn, n_copies, aliases=None):
        self.inputs, self.out_shapes, self.plan_fn, self.n_copies = list(inputs), list(out_shapes), plan_fn, n_copies
        self.aliases = dict(aliases or {})

    def scratch(self):
        return [pltpu.SemaphoreType.DMA((self.n_copies,)), pltpu.SemaphoreType.DMA((self.n_copies,))]

    def run(self, in_refs, out_refs, sems, first, last):
        @pl.when(first)
        def _():
            _plan_start(self.plan_fn(in_refs, out_refs), *sems)

        @pl.when(last)
        def _():
            _plan_wait(self.plan_fn(in_refs, out_refs), *sems)


def _mm(a, b, *, mode, out_dtypes, name, tm=None, tn=None, tk=None, extras=(), epi=None,
        out_shapes=None, out_specs=None, side=None, b_slots=0):
    tm, tn, tk = tm or MM_T, tn or MM_T, tk or MM_TK
    b2 = (b.shape[1], b_slots * b.shape[2]) if b_slots else b.shape
    if mode == "nn":
        (m, k), n = a.shape, b2[1]
    elif mode == "nt":
        (m, k), n = a.shape, b2[0]
    else:
        (k, m), n = a.shape, b2[1]
    tm, tn, tk = min(tm, m), min(tn, n), min(tk, k)
    if b_slots:
        tn = min(tn, b.shape[2]) if mode == "nn" else tn
        tk = min(tk, b.shape[2]) if mode == "nt" else tk
    assert m % tm == 0 and n % tn == 0 and k % tk == 0, (name, m, n, k)
    nk = k // tk
    n_out, n_ex = len(out_dtypes), len(extras)
    if epi is None:
        epi = lambda acc: tuple(acc for _ in range(n_out))
    dot = {"nn": _dot_nn, "nt": _dot_nt, "tn": _dot_tn}[mode]

    n_si = len(side.inputs) if side else 0
    n_so = len(side.out_shapes) if side else 0
    grid = (m // tm, n // tn, nk)

    def body(*refs):
        a_ref, b_ref = refs[0], refs[1]
        ex_refs = refs[2:2 + n_ex]
        base = 2 + n_ex + n_si
        o_refs = refs[base:base + n_out]
        scratch = refs[base + n_out + n_so:]
        if side:
            pos = [pl.program_id(q) for q in range(3)]
            first = (pos[0] == 0) & (pos[1] == 0) & (pos[2] == 0)
            last = (pos[0] == grid[0] - 1) & (pos[1] == grid[1] - 1) & (pos[2] == grid[2] - 1)
            side.run(refs[2 + n_ex:base], refs[base + n_out:base + n_out + n_so], scratch[-2:], first, last)
        part = dot(a_ref[...], b_ref[...])

        def finish(acc):
            outs = epi(acc, *[e[...] for e in ex_refs])
            for o_ref, val in zip(o_refs, outs):
                o_ref[...] = val.reshape(o_ref.shape).astype(o_ref.dtype)

        if nk == 1:
            finish(part)
        else:
            acc_ref = scratch[0]
            kk = pl.program_id(2)

            @pl.when(kk == 0)
            def _():
                acc_ref[...] = part

            @pl.when(kk > 0)
            def _():
                acc_ref[...] += part

            @pl.when(kk == nk - 1)
            def _():
                finish(acc_ref[...])

    if mode == "nn":
        a_spec = pl.BlockSpec((tm, tk), lambda i, j, kk: (i, kk))
        b_spec = pl.BlockSpec((tk, tn), lambda i, j, kk: (kk, j))
        if b_slots:
            per = b.shape[2] // tn
            b_spec = pl.BlockSpec((None, tk, tn), lambda i, j, kk: (j // per, kk, j % per))
    elif mode == "nt":
        a_spec = pl.BlockSpec((tm, tk), lambda i, j, kk: (i, kk))
        b_spec = pl.BlockSpec((tn, tk), lambda i, j, kk: (j, kk))
        if b_slots:
            per = b.shape[2] // tk
            b_spec = pl.BlockSpec((None, tn, tk), lambda i, j, kk: (kk // per, j, kk % per))
    else:
        assert not b_slots
        a_spec = pl.BlockSpec((tk, tm), lambda i, j, kk: (kk, i))
        b_spec = pl.BlockSpec((tk, tn), lambda i, j, kk: (kk, j))
    tile_spec = pl.BlockSpec((tm, tn), lambda i, j, kk: (i, j))
    if out_shapes is None:
        out_shapes = [jax.ShapeDtypeStruct((m, n), dt) for dt in out_dtypes]
    if out_specs is None:
        out_specs = [tile_spec for _ in out_dtypes]
    any_spec = pl.BlockSpec(memory_space=pl.ANY)
    res = pl.pallas_call(
        body,
        grid=grid,
        in_specs=[a_spec, b_spec] + [tile_spec for _ in extras] + [any_spec] * n_si,
        out_specs=list(out_specs) + [any_spec] * n_so,
        out_shape=list(out_shapes) + (side.out_shapes if side else []),
        scratch_shapes=([pltpu.VMEM((tm, tn), F32)] if nk > 1 else []) + (side.scratch() if side else []),
        compiler_params=_cp("arbitrary", "arbitrary", "arbitrary") if side else _cp("parallel", "parallel", "arbitrary"),
        input_output_aliases={2 + n_ex + si: n_out + so for si, so in side.aliases.items()} if side else {},
        name=name,
    )(a, b, *extras, *(side.inputs if side else []))
    return res


def _row_spec(ts, d):
    return pl.BlockSpec((ts, d), lambda i: (i, 0))


def _vec_spec(d):
    return pl.BlockSpec((1, d), lambda i: (0, 0))


def _side_args(side, n_in, n_out):
    if side is None:
        return [], [], [], [], [], {}
    any_spec = pl.BlockSpec(memory_space=pl.ANY)
    return ([any_spec] * len(side.inputs), [any_spec] * len(side.out_shapes), side.out_shapes, side.scratch(),
            side.inputs, {n_in + si: n_out + so for si, so in side.aliases.items()})


def _pre_fwd(x, avec, shift, *, name, side=None):
    s, d = x.shape
    ts = min(WIDE_ROW_TILE, s)
    nb = s // ts
    s_in, s_out, s_shapes, s_scratch, s_ops, s_alias = _side_args(side, 3, 1)

    def body(x_ref, a_ref, s_ref, *rest):
        h_ref = rest[len(s_in)]
        if side:
            step = pl.program_id(0)
            side.run(rest[:len(s_in)], rest[len(s_in) + 1:len(s_in) + 1 + len(s_out)],
                     rest[len(s_in) + 1 + len(s_out):], step == 0, step == nb - 1)
        xv = x_ref[...]
        r = lax.rsqrt(jnp.mean(xv * xv, axis=-1, keepdims=True) + EPS)
        h_ref[...] = (xv * r * a_ref[...] + s_ref[...]).astype(BF16)

    res = pl.pallas_call(
        body, grid=(nb,),
        in_specs=[_row_spec(ts, d), _vec_spec(d), _vec_spec(d)] + s_in,
        out_specs=[_row_spec(ts, d)] + s_out,
        out_shape=[jax.ShapeDtypeStruct((s, d), BF16)] + s_shapes,
        scratch_shapes=s_scratch, input_output_aliases=s_alias,
        compiler_params=_cp("arbitrary" if side else "parallel"), name=name,
    )(x, avec, shift, *s_ops)
    return res if side else res[0]


def _post_pre_fwd(x, y, gate, g, avec, shift, *, name):
    s, d = x.shape
    ts = min(WIDE_ROW_TILE, s)

    def body(x_ref, y_ref, gate_ref, g_ref, a_ref, s_ref, o_ref, h_ref):
        yv = y_ref[...]
        r = lax.rsqrt(jnp.mean(yv * yv, axis=-1, keepdims=True) + EPS)
        x1 = x_ref[...] + gate_ref[...] * (yv * r * g_ref[...])
        o_ref[...] = x1
        r1 = lax.rsqrt(jnp.mean(x1 * x1, axis=-1, keepdims=True) + EPS)
        h_ref[...] = (x1 * r1 * a_ref[...] + s_ref[...]).astype(BF16)

    return pl.pallas_call(
        body, grid=(s // ts,),
        in_specs=[_row_spec(ts, d), _row_spec(ts, d)] + [_vec_spec(d)] * 4,
        out_specs=[_row_spec(ts, d), _row_spec(ts, d)],
        out_shape=[jax.ShapeDtypeStruct((s, d), F32), jax.ShapeDtypeStruct((s, d), BF16)],
        compiler_params=_cp("parallel"), name=name,
    )(x, y, gate, g, avec, shift)


def _post_bwd_math(dxv, yv, gatev, gv):
    r = lax.rsqrt(jnp.mean(yv * yv, axis=-1, keepdims=True) + EPS)
    yhat = yv * r
    dn = dxv * gatev
    dyhat = dn * gv
    dy = r * (dyhat - yhat * jnp.mean(dyhat * yhat, axis=-1, keepdims=True))
    return dy, dxv * (yhat * gv), dn * yhat


def _accumulate(first, pairs):
    @pl.when(first)
    def _():
        for ref, _ in pairs:
            ref[...] = jnp.zeros_like(ref)

    for ref, val in pairs:
        ref[...] += jnp.sum(val, axis=0, keepdims=True)


def _post_loss_bwd(x, y, gate, g, target, *, name):
    s, d = x.shape
    ts = min(ROW_TILE, s)

    def body(x_ref, y_ref, gate_ref, g_ref, t_ref, dx_ref, dy_ref, loss_ref, dgate_ref, dg_ref):
        yv, gatev, gv = y_ref[...], gate_ref[...], g_ref[...]
        r = lax.rsqrt(jnp.mean(yv * yv, axis=-1, keepdims=True) + EPS)
        diff = x_ref[...] + gatev * (yv * r * gv) - t_ref[...]
        dxv = diff * (1.0 / d)
        dx_ref[...] = dxv
        dy, dgate_rows, dg_rows = _post_bwd_math(dxv, yv, gatev, gv)
        dy_ref[...] = dy.astype(BF16)
        first = pl.program_id(0) == 0
        _accumulate(first, [(dgate_ref, dgate_rows), (dg_ref, dg_rows)])

        @pl.when(first)
        def _():
            loss_ref[...] = jnp.zeros_like(loss_ref)

        loss_ref[...] += jnp.sum(jnp.mean(diff * diff, axis=-1, keepdims=True)) * 0.5

    return pl.pallas_call(
        body, grid=(s // ts,),
        in_specs=[_row_spec(ts, d), _row_spec(ts, d), _vec_spec(d), _vec_spec(d), _row_spec(ts, d)],
        out_specs=[_row_spec(ts, d), _row_spec(ts, d), pl.BlockSpec((1, 128), lambda i: (0, 0)), _vec_spec(d),
                   _vec_spec(d)],
        out_shape=[jax.ShapeDtypeStruct((s, d), F32), jax.ShapeDtypeStruct((s, d), BF16),
                   jax.ShapeDtypeStruct((1, 128), F32), jax.ShapeDtypeStruct((1, d), F32),
                   jax.ShapeDtypeStruct((1, d), F32)],
        compiler_params=_cp("arbitrary"), name=name,
    )(x, y, gate, g, target)


def _pre_post_bwd(dh, xin, dres, avec, y, gate, g, *, name):
    s, d = xin.shape
    ts = min(ROW_TILE, s)

    def body(dh_ref, x_ref, dres_ref, a_ref, y_ref, gate_ref, g_ref, dx_ref, dshift_ref, da_ref, dy_ref,
             dgate_ref, dg_ref):
        xv, dhv = x_ref[...], dh_ref[...]
        r = lax.rsqrt(jnp.mean(xv * xv, axis=-1, keepdims=True) + EPS)
        xhat = xv * r
        dxhat = dhv * a_ref[...]
        dxv = dres_ref[...] + r * (dxhat - xhat * jnp.mean(dxhat * xhat, axis=-1, keepdims=True))
        dx_ref[...] = dxv
        dy, dgate_rows, dg_rows = _post_bwd_math(dxv, y_ref[...], gate_ref[...], g_ref[...])
        dy_ref[...] = dy.astype(BF16)
        _accumulate(pl.program_id(0) == 0, [(dshift_ref, dhv), (da_ref, dhv * xhat), (dgate_ref, dgate_rows),
                                            (dg_ref, dg_rows)])

    return pl.pallas_call(
        body, grid=(s // ts,),
        in_specs=[_row_spec(ts, d), _row_spec(ts, d), _row_spec(ts, d), _vec_spec(d), _row_spec(ts, d),
                  _vec_spec(d), _vec_spec(d)],
        out_specs=[_row_spec(ts, d), _vec_spec(d), _vec_spec(d), _row_spec(ts, d), _vec_spec(d), _vec_spec(d)],
        out_shape=[jax.ShapeDtypeStruct((s, d), F32), jax.ShapeDtypeStruct((1, d), F32),
                   jax.ShapeDtypeStruct((1, d), F32), jax.ShapeDtypeStruct((s, d), BF16),
                   jax.ShapeDtypeStruct((1, d), F32), jax.ShapeDtypeStruct((1, d), F32)],
        compiler_params=_cp("arbitrary"), name=name,
    )(dh, xin, dres, avec, y, gate, g)


def _pre_bwd(dh, xin, dres, avec, *, name, side=None):
    s, d = xin.shape
    ts = min(WIDE_ROW_TILE, s)
    nb = s // ts
    s_in, s_out, s_shapes, s_scratch, s_ops, s_alias = _side_args(side, 4, 3)

    def body(dh_ref, x_ref, dres_ref, a_ref, *rest):
        dx_ref, dshift_ref, da_ref = rest[len(s_in):len(s_in) + 3]
        if side:
            step = pl.program_id(0)
            side.run(rest[:len(s_in)], rest[len(s_in) + 3:len(s_in) + 3 + len(s_out)],
                     rest[len(s_in) + 3 + len(s_out):], step == 0, step == nb - 1)
        xv, dhv = x_ref[...], dh_ref[...]
        r = lax.rsqrt(jnp.mean(xv * xv, axis=-1, keepdims=True) + EPS)
        xhat = xv * r
        dxhat = dhv * a_ref[...]
        dx_ref[...] = dres_ref[...] + r * (dxhat - xhat * jnp.mean(dxhat * xhat, axis=-1, keepdims=True))

        @pl.when(pl.program_id(0) == 0)
        def _():
            dshift_ref[...] = jnp.zeros_like(dshift_ref)
            da_ref[...] = jnp.zeros_like(da_ref)

        dshift_ref[...] += jnp.sum(dhv, axis=0, keepdims=True)
        da_ref[...] += jnp.sum(dhv * xhat, axis=0, keepdims=True)

    return pl.pallas_call(
        body, grid=(nb,),
        in_specs=[_row_spec(ts, d), _row_spec(ts, d), _row_spec(ts, d), _vec_spec(d)] + s_in,
        out_specs=[_row_spec(ts, d), _vec_spec(d), _vec_spec(d)] + s_out,
        out_shape=[jax.ShapeDtypeStruct((s, d), F32), jax.ShapeDtypeStruct((1, d), F32),
                   jax.ShapeDtypeStruct((1, d), F32)] + s_shapes,
        scratch_shapes=s_scratch, input_output_aliases=s_alias,
        compiler_params=_cp("arbitrary"), name=name,
    )(dh, xin, dres, avec, *s_ops)


def _tri(n, strict=False, upper=False):
    r = lax.broadcasted_iota(jnp.int32, (n, n), 0)
    c = lax.broadcasted_iota(jnp.int32, (n, n), 1)
    if upper:
        r, c = c, r
    return ((r > c) if strict else (r >= c)).astype(F32)


def _gates_fwd(ps, bf, w2p, b2, *, name):
    s = ps.shape[0]
    ts = min(GATE_TS, s)

    def body(ps_ref, bf_ref, w_ref, b2_ref, cum_ref, la_ref, carry_ref):
        @pl.when(pl.program_id(0) == 0)
        def _():
            carry_ref[...] = jnp.zeros_like(carry_ref)

        psv = ps_ref[...]
        lf = _log_sigmoid(psv + bf_ref[...])
        cum = _dot_nn(_tri(ts), lf, HIGHEST) + carry_ref[...]
        cum_ref[...] = cum
        carry_ref[...] = cum[ts - 1:ts, :]
        z = _dot_nn(psv, w_ref[...], HIGHEST) + b2_ref[...]
        la_ref[...] = _log_sigmoid(z) * (1.0 / GLA_TEMP)

    return pl.pallas_call(
        body, grid=(s // ts,),
        in_specs=[_row_spec(ts, SMALL_W), _vec_spec(SMALL_W),
                  pl.BlockSpec((SMALL_W, GLA_KW), lambda i: (0, 0)), _vec_spec(GLA_KW)],
        out_specs=[_row_spec(ts, SMALL_W), _row_spec(ts, GLA_KW)],
        out_shape=[jax.ShapeDtypeStruct((s, SMALL_W), F32), jax.ShapeDtypeStruct((s, GLA_KW), F32)],
        scratch_shapes=[pltpu.VMEM((1, SMALL_W), F32)],
        compiler_params=_cp("arbitrary"), name=name,
    )(ps, bf, w2p, b2)


def _gates_bwd(dck, ps, bf, w2p, b2, dla, *, name):
    s = ps.shape[0]
    ts = min(GATE_TS, s)
    nb = s // ts
    rev = lambda i: (nb - 1 - i, 0)

    def body(dck_ref, ps_ref, bf_ref, w_ref, b2_ref, dla_ref, dps_ref, dbf_ref, dw_ref, db2_ref, carry_ref):
        @pl.when(pl.program_id(0) == 0)
        def _():
            carry_ref[...] = jnp.zeros_like(carry_ref)
            dbf_ref[...] = jnp.zeros_like(dbf_ref)
            dw_ref[...] = jnp.zeros_like(dw_ref)
            db2_ref[...] = jnp.zeros_like(db2_ref)

        psv, dckv = ps_ref[...], dck_ref[...]
        dlf = _dot_nn(_tri(ts, upper=True), dckv, HIGHEST) + carry_ref[...]
        carry_ref[...] += jnp.sum(dckv, axis=0, keepdims=True)
        lane = lax.broadcasted_iota(jnp.int32, (ts, SMALL_W), 1)
        dff = jnp.where(lane < FOX_HEADS, dlf * _sigmoid(-(psv + bf_ref[...])), 0.0)
        z = _dot_nn(psv, w_ref[...], HIGHEST) + b2_ref[...]
        dz = dla_ref[...] * _sigmoid(-z) * (1.0 / GLA_TEMP)
        dps_ref[...] = (_dot_nt(dz, w_ref[...], HIGHEST) + dff).astype(BF16)
        dbf_ref[...] += jnp.sum(dff, axis=0, keepdims=True)
        dw_ref[...] += _dot_tn(psv, dz, HIGHEST)
        db2_ref[...] += jnp.sum(dz, axis=0, keepdims=True)

    return pl.pallas_call(
        body, grid=(nb,),
        in_specs=[pl.BlockSpec((ts, SMALL_W), rev), pl.BlockSpec((ts, SMALL_W), rev), _vec_spec(SMALL_W),
                  pl.BlockSpec((SMALL_W, GLA_KW), lambda i: (0, 0)), _vec_spec(GLA_KW),
                  pl.BlockSpec((ts, GLA_KW), rev)],
        out_specs=[pl.BlockSpec((ts, SMALL_W), rev), _vec_spec(SMALL_W),
                   pl.BlockSpec((SMALL_W, GLA_KW), lambda i: (0, 0)), _vec_spec(GLA_KW)],
        out_shape=[jax.ShapeDtypeStruct((s, SMALL_W), BF16), jax.ShapeDtypeStruct((1, SMALL_W), F32),
                   jax.ShapeDtypeStruct((SMALL_W, GLA_KW), F32), jax.ShapeDtypeStruct((1, GLA_KW), F32)],
        scratch_shapes=[pltpu.VMEM((1, SMALL_W), F32)],
        compiler_params=_cp("arbitrary"), name=name,
    )(dck, ps, bf, w2p, b2, dla)


def _hs(h, hd=FOX_HD):
    return slice(h * hd, (h + 1) * hd)


def _fox_fwd(proj, cum_t, g_fox, *, name, side=None):
    s = proj.shape[0]
    tq, tk = min(FOX_TQ, s), min(FOX_TK, s)
    scale = FOX_HD ** -0.5
    n_si = len(side.inputs) if side else 0
    n_so = len(side.out_shapes) if side else 0
    grid = (s // tq, s // tk)

    def body(*refs):
        q_ref, k_ref, v_ref, ck_ref, g_ref = refs[:5]
        o_ref, n_ref, lse_ref = refs[5 + n_si:8 + n_si]
        m_sc, acc_sc = refs[8 + n_si + n_so:10 + n_si + n_so]
        i, j = pl.program_id(0), pl.program_id(1)
        if side:
            side.run(refs[5:5 + n_si], refs[8 + n_si:8 + n_si + n_so], refs[10 + n_si + n_so:],
                     (i == 0) & (j == 0), (i == grid[0] - 1) & (j == grid[1] - 1))

        @pl.when(j == 0)
        def _():
            m_sc[...] = jnp.full_like(m_sc, NEG)
            acc_sc[...] = jnp.zeros_like(acc_sc)

        def block(masked):
            mask = _causal_mask(i, j, tq, tk) if masked else None
            ones = jnp.ones((tk, FOX_HD), BF16)
            for h in range(FOX_HEADS):
                sc = _fox_logits(_dot_nt(q_ref[:, _hs(h)], k_ref[:, _hs(h)]), ck_ref[h:h + 1, :], mask, scale)
                m_prev = m_sc[h]
                m_new = jnp.maximum(m_prev, jnp.max(sc, axis=-1, keepdims=True))
                alpha = jnp.exp(m_prev - m_new)
                p = jnp.exp(sc - m_new).astype(BF16)
                v_one = jnp.concatenate([v_ref[:, _hs(h)], ones], axis=1)
                acc_sc[:, _hs(h, 2 * FOX_HD)] = alpha * acc_sc[:, _hs(h, 2 * FOX_HD)] + _dot_nn(p, v_one)
                m_sc[h] = m_new

        pl.when(j < i)(functools.partial(block, False))

        @pl.when(j == i)
        def _():
            block(True)
            lane = lax.broadcasted_iota(jnp.int32, (tq, 128), 1)
            lse = jnp.zeros((tq, 128), F32)
            for h in range(FOX_HEADS):
                l_rep = acc_sc[:, 2 * h * FOX_HD + FOX_HD:2 * (h + 1) * FOX_HD]
                o = acc_sc[:, 2 * h * FOX_HD:2 * h * FOX_HD + FOX_HD] / l_rep
                o_ref[:, _hs(h)] = o
                r = lax.rsqrt(jnp.mean(o * o, axis=-1, keepdims=True) + EPS)
                n_ref[:, _hs(h)] = (o * r * g_ref[h:h + 1, :]).astype(BF16)
                lse = jnp.where(lane == h, m_sc[h] + jnp.log(l_rep), lse)
            lse_ref[...] = lse

    kv = lambda col: (lambda i, j: (jnp.minimum(j, i), col))
    any_spec = pl.BlockSpec(memory_space=pl.ANY)
    return pl.pallas_call(
        body, grid=grid,
        in_specs=[pl.BlockSpec((tq, FOX_W), lambda i, j: (i, 0)),
                  pl.BlockSpec((tk, FOX_W), kv(1)),
                  pl.BlockSpec((tk, FOX_W), kv(2)),
                  pl.BlockSpec((FOX_HEADS, tk), lambda i, j: (0, jnp.minimum(j, i))),
                  pl.BlockSpec((FOX_HEADS, FOX_HD), lambda i, j: (0, 0))] + [any_spec] * n_si,
        out_specs=[pl.BlockSpec((tq, FOX_W), lambda i, j: (i, 0)),
                   pl.BlockSpec((tq, FOX_W), lambda i, j: (i, 0)),
                   pl.BlockSpec((tq, 128), lambda i, j: (i, 0))] + [any_spec] * n_so,
        out_shape=[jax.ShapeDtypeStruct((s, FOX_W), F32), jax.ShapeDtypeStruct((s, FOX_W), BF16),
                   jax.ShapeDtypeStruct((s, 128), F32)] + (side.out_shapes if side else []),
        scratch_shapes=[pltpu.VMEM((FOX_HEADS, tq, 1), F32), pltpu.VMEM((tq, 2 * FOX_W), F32)]
        + (side.scratch() if side else []),
        compiler_params=_cp("arbitrary", "arbitrary"), name=name,
    )(proj, proj, proj, cum_t, g_fox, *(side.inputs if side else []))


def _causal_mask(i, j, tq, tk):
    rows = i * tq + lax.broadcasted_iota(jnp.int32, (tq, tk), 0)
    cols = j * tk + lax.broadcasted_iota(jnp.int32, (tq, tk), 1)
    return rows >= cols


def _fox_logits(qk, ck, mask, scale):
    sc = qk * scale - ck
    return sc if mask is None else jnp.where(mask, sc, NEG)


def _fox_bwd(proj, do, cum_t, lse, delta, *, name, side=None):
    s = proj.shape[0]
    tq, tk = min(FOX_TQ, s), min(FOX_TK, s)
    nk, nq = s // tk, s // tq
    scale = FOX_HD ** -0.5
    n_si = len(side.inputs) if side else 0
    n_so = len(side.out_shapes) if side else 0

    def body(*refs):
        q_ref, k_ref, v_ref, do_ref, ck_ref, lse_ref, dl_ref = refs[:7]
        dq_hbm, dk_ref, dv_ref, dcq_hbm, dck_ref = refs[7 + n_si:12 + n_si]
        dq_sc, dcq_sc, dk_sc, dv_sc, dck_sc, out_sems = refs[12 + n_si + n_so:18 + n_si + n_so]
        j, i = pl.program_id(0), pl.program_id(1)
        if side:
            side.run(refs[7:7 + n_si], refs[12 + n_si:12 + n_si + n_so], refs[18 + n_si + n_so:],
                     (j == 0) & (i == 0), (j == nk - 1) & (i == nq - 1))

        @pl.when((j == 0) & (i == 0))
        def _():
            dq_sc[...] = jnp.zeros_like(dq_sc)
            dcq_sc[...] = jnp.zeros_like(dcq_sc)

        @pl.when(i == 0)
        def _():
            dk_sc[...] = jnp.zeros_like(dk_sc)
            dv_sc[...] = jnp.zeros_like(dv_sc)
            dck_sc[...] = jnp.zeros_like(dck_sc)

        def block(masked):
            mask = _causal_mask(i, j, tq, tk) if masked else None
            qrows = pl.ds(pl.multiple_of(i * tq, tq), tq)
            for h in range(FOX_HEADS):
                sc = _fox_logits(_dot_nt(q_ref[:, _hs(h)], k_ref[:, _hs(h)]), ck_ref[h:h + 1, :], mask, scale)
                p = jnp.exp(sc - lse_ref[:, h:h + 1])
                ds = p * (_dot_nt(do_ref[:, _hs(h)], v_ref[:, _hs(h)]) - dl_ref[:, h:h + 1])
                dsb = ds.astype(BF16)
                dv_sc[:, _hs(h)] += _dot_tn(p.astype(BF16), do_ref[:, _hs(h)])
                dk_sc[:, _hs(h)] += _dot_tn(dsb, q_ref[:, _hs(h)])
                dq_sc[qrows, _hs(h)] += _dot_nn(dsb, k_ref[:, _hs(h)]) * scale
                dck_sc[h:h + 1, :] -= jnp.sum(ds, axis=0, keepdims=True)
                dcq_sc[qrows, h:h + 1] += jnp.sum(ds, axis=-1, keepdims=True)

        pl.when(i > j)(functools.partial(block, False))
        pl.when(i == j)(functools.partial(block, True))

        @pl.when(i == nq - 1)
        def _():
            dk_ref[...] = (dk_sc[...] * scale).astype(BF16)
            dv_ref[...] = dv_sc[...].astype(BF16)
            dck_ref[...] = dck_sc[...]

        @pl.when((j == nk - 1) & (i == nq - 1))
        def _():
            out_q = pltpu.make_async_copy(dq_sc, dq_hbm, out_sems.at[0])
            out_c = pltpu.make_async_copy(dcq_sc, dcq_hbm, out_sems.at[1])
            out_q.start()
            out_c.start()
            out_q.wait()
            out_c.wait()

    qrow = lambda j, i: (jnp.maximum(i, j), 0)
    krow = lambda col: (lambda j, i: (j, col))
    any_spec = pl.BlockSpec(memory_space=pl.ANY)
    return pl.pallas_call(
        body, grid=(nk, nq),
        in_specs=[pl.BlockSpec((tq, FOX_W), qrow), pl.BlockSpec((tk, FOX_W), krow(1)),
                  pl.BlockSpec((tk, FOX_W), krow(2)),
                  pl.BlockSpec((tq, FOX_W), qrow),
                  pl.BlockSpec((FOX_HEADS, tk), lambda j, i: (0, j)),
                  pl.BlockSpec((tq, 128), qrow), pl.BlockSpec((tq, 128), qrow)] + [any_spec] * n_si,
        out_specs=[any_spec, pl.BlockSpec((tk, FOX_W), lambda j, i: (j, 0)),
                   pl.BlockSpec((tk, FOX_W), lambda j, i: (j, 0)), any_spec,
                   pl.BlockSpec((FOX_HEADS, tk), lambda j, i: (0, j))] + [any_spec] * n_so,
        out_shape=[jax.ShapeDtypeStruct((s, FOX_W), F32), jax.ShapeDtypeStruct((s, FOX_W), BF16),
                   jax.ShapeDtypeStruct((s, FOX_W), BF16), jax.ShapeDtypeStruct((s, 128), F32),
                   jax.ShapeDtypeStruct((FOX_HEADS, s), F32)] + (side.out_shapes if side else []),
        scratch_shapes=[pltpu.VMEM((s, FOX_W), F32), pltpu.VMEM((s, 128), F32),
                        pltpu.VMEM((tk, FOX_W), F32), pltpu.VMEM((tk, FOX_W), F32), pltpu.VMEM((FOX_HEADS, tk), F32),
                        pltpu.SemaphoreType.DMA((2,))] + (side.scratch() if side else []),
        compiler_params=_cp("arbitrary", "arbitrary"), name=name,
    )(proj, proj, proj, do, cum_t, lse, delta, *(side.inputs if side else []))


def _head_norm_bwd(dn_in, o, g, gr_src, *, nh, hd, dn_col, gr_col, name):
    s, w = o.shape
    ts = min(ROW_TILE, s)
    gated = gr_src is not None

    def body(*refs):
        if gated:
            dn_ref, o_ref, g_ref, gr_ref, do_ref, dgr_ref, dl_ref, dg_ref = refs
        else:
            dn_ref, o_ref, g_ref, do_ref, dl_ref, dg_ref = refs

        @pl.when(pl.program_id(0) == 0)
        def _():
            dg_ref[...] = jnp.zeros_like(dg_ref)

        lane = lax.broadcasted_iota(jnp.int32, (ts, 128), 1)
        delta = jnp.zeros((ts, 128), F32)
        for h in range(nh):
            sl = _hs(h, hd)
            ov = o_ref[:, sl]
            dnv = dn_ref[:, sl].astype(F32)
            gv = g_ref[h:h + 1, :]
            r = lax.rsqrt(jnp.mean(ov * ov, axis=-1, keepdims=True) + EPS)
            ohat = ov * r
            if gated:
                grv = gr_ref[:, sl].astype(F32)
                sig = _sigmoid(grv)
                dgr_ref[:, sl] = (dnv * (ohat * gv) * (sig * (1.0 + grv * (1.0 - sig)))).astype(BF16)
                dnv = dnv * (grv * sig)
            dg_ref[h:h + 1, :] += jnp.sum(dnv * ohat, axis=0, keepdims=True)
            dohat = dnv * gv
            do = r * (dohat - ohat * jnp.mean(dohat * ohat, axis=-1, keepdims=True))
            do_ref[:, sl] = do.astype(BF16)
            delta = jnp.where(lane == h, jnp.sum(do.astype(BF16).astype(F32) * ov, axis=-1, keepdims=True), delta)
        dl_ref[...] = delta

    in_specs = [pl.BlockSpec((ts, w), lambda i: (i, dn_col)), _row_spec(ts, w),
                pl.BlockSpec((nh, hd), lambda i: (0, 0))]
    args = [dn_in, o, g]
    out_specs = [_row_spec(ts, w)]
    out_shape = [jax.ShapeDtypeStruct((s, w), BF16)]
    if gated:
        in_specs.append(pl.BlockSpec((ts, w), lambda i: (i, gr_col)))
        args.append(gr_src)
        out_specs.append(_row_spec(ts, w))
        out_shape.append(jax.ShapeDtypeStruct((s, w), BF16))
    out_specs += [_row_spec(ts, 128), pl.BlockSpec((nh, hd), lambda i: (0, 0))]
    out_shape += [jax.ShapeDtypeStruct((s, 128), F32), jax.ShapeDtypeStruct((nh, hd), F32)]
    return pl.pallas_call(
        body, grid=(s // ts,), in_specs=in_specs, out_specs=out_specs, out_shape=out_shape,
        compiler_params=_cp("arbitrary"), name=name,
    )(*args)


GQ_BLK = 3 * FOX_W // GLA_DK
GK_BLK = GQ_BLK + GLA_HEADS
GV_BLK = (3 * FOX_W + 2 * GLA_KW) // GLA_DV
GR_BLK = GV_BLK + GLA_HEADS


def _gla_chunk_terms(la):
    cum = _dot_nn(_tri(CHUNK), la, HIGHEST)
    total = cum[CHUNK - 1:CHUNK, :]
    return jnp.exp(total - cum), jnp.exp(total)


def _gla_fwd(proj, log_a, g_gla, *, name):
    s = proj.shape[0]
    rows = min(GLA_ROWS, s)
    cb = rows // CHUNK
    nblk = s // rows
    scale = GLA_DK ** -0.5

    def body(q_ref, k_ref, v_ref, gr_ref, la_ref, g_ref, o_ref, n_ref, st_ref, st_sc):
        h = pl.program_id(0)

        @pl.when(pl.program_id(1) == 0)
        def _():
            st_sc[...] = jnp.zeros_like(st_sc)

        gv = g_ref[pl.ds(h, 1), :]
        for ci in range(cb):
            sl = slice(ci * CHUNK, (ci + 1) * CHUNK)
            e, dec = _gla_chunk_terms(la_ref[sl, :])
            k_dec = (k_ref[sl, :].astype(F32) * e).astype(BF16)
            st = st_sc[...] * dec + _dot_tn(v_ref[sl, :], k_dec)
            st_sc[...] = st
            st_ref[0, ci] = st
            qs = (q_ref[sl, :].astype(F32) * scale).astype(BF16)
            o = _dot_nt(qs, st.astype(BF16))
            o_ref[sl, :] = o
            r = lax.rsqrt(jnp.mean(o * o, axis=-1, keepdims=True) + EPS)
            grv = gr_ref[sl, :].astype(F32)
            n_ref[sl, :] = (o * r * gv * (grv * _sigmoid(grv))).astype(BF16)

    return pl.pallas_call(
        body, grid=(GLA_HEADS, nblk),
        in_specs=[pl.BlockSpec((rows, GLA_DK), lambda h, n: (n, GQ_BLK + h)),
                  pl.BlockSpec((rows, GLA_DK), lambda h, n: (n, GK_BLK + h)),
                  pl.BlockSpec((rows, GLA_DV), lambda h, n: (n, GV_BLK + h)),
                  pl.BlockSpec((rows, GLA_DV), lambda h, n: (n, GR_BLK + h)),
                  pl.BlockSpec((rows, GLA_DK), lambda h, n: (n, h)),
                  pl.BlockSpec((GLA_HEADS, GLA_DV), lambda h, n: (0, 0))],
        out_specs=[pl.BlockSpec((rows, GLA_DV), lambda h, n: (n, h)),
                   pl.BlockSpec((rows, GLA_DV), lambda h, n: (n, h)),
                   pl.BlockSpec((1, cb, GLA_DV, GLA_DK), lambda h, n: (h, n, 0, 0))],
        out_shape=[jax.ShapeDtypeStruct((s, GLA_W), F32), jax.ShapeDtypeStruct((s, GLA_W), BF16),
                   jax.ShapeDtypeStruct((GLA_HEADS, s // CHUNK, GLA_DV, GLA_DK), F32)],
        scratch_shapes=[pltpu.VMEM((GLA_DV, GLA_DK), F32)],
        compiler_params=_cp("parallel", "arbitrary"), name=name,
    )(proj, proj, proj, proj, log_a, g_gla)


def _gla_bwd(proj, log_a, do, states, *, name):
    s = proj.shape[0]
    rows = min(GLA_ROWS, s)
    cb = rows // CHUNK
    nblk = s // rows
    scale = GLA_DK ** -0.5

    def body(q_ref, k_ref, v_ref, la_ref, do_ref, st_ref, prev_ref, dq_ref, dk_ref, dv_ref, dla_ref, g_sc):
        nrev = pl.program_id(1)
        blk = nblk - 1 - nrev

        @pl.when(nrev == 0)
        def _():
            g_sc[...] = jnp.zeros_like(g_sc)

        for ci in reversed(range(cb)):
            sl = slice(ci * CHUNK, (ci + 1) * CHUNK)
            e, dec = _gla_chunk_terms(la_ref[sl, :])
            kd = k_ref[sl, :].astype(F32) * e
            qs = (q_ref[sl, :].astype(F32) * scale).astype(BF16)
            dov = do_ref[sl, :]
            st = st_ref[0, ci]
            if ci > 0:
                st_prev = st_ref[0, ci - 1]
            else:
                st_prev = prev_ref[0, 0] * (blk > 0).astype(F32)
            dq_ref[sl, :] = (_dot_nn(dov, st.astype(BF16)) * scale).astype(BF16)
            gt = g_sc[...] + _dot_tn(dov, qs)
            gtb = gt.astype(BF16)
            dkd = _dot_nn(v_ref[sl, :], gtb)
            dv_ref[sl, :] = _dot_nt(kd.astype(BF16), gtb).astype(BF16)
            dk_ref[sl, :] = (dkd * e).astype(BF16)
            ddec = jnp.sum(gt * st_prev, axis=0, keepdims=True) * dec
            dla_ref[sl, :] = _dot_nn(_tri(CHUNK, strict=True), dkd * kd, HIGHEST) + ddec
            g_sc[...] = gt * dec

    rev = lambda col0: (lambda h, n: (nblk - 1 - n, col0 + h))
    return pl.pallas_call(
        body, grid=(GLA_HEADS, nblk),
        in_specs=[pl.BlockSpec((rows, GLA_DK), rev(GQ_BLK)),
                  pl.BlockSpec((rows, GLA_DK), rev(GK_BLK)),
                  pl.BlockSpec((rows, GLA_DV), rev(GV_BLK)),
                  pl.BlockSpec((rows, GLA_DK), rev(0)),
                  pl.BlockSpec((rows, GLA_DV), rev(0)),
                  pl.BlockSpec((1, cb, GLA_DV, GLA_DK), lambda h, n: (h, nblk - 1 - n, 0, 0)),
                  pl.BlockSpec((1, 1, GLA_DV, GLA_DK),
                               lambda h, n: (h, jnp.maximum((nblk - 1 - n) * cb - 1, 0), 0, 0))],
        out_specs=[pl.BlockSpec((rows, GLA_DK), rev(0)), pl.BlockSpec((rows, GLA_DK), rev(0)),
                   pl.BlockSpec((rows, GLA_DV), rev(0)), pl.BlockSpec((rows, GLA_DK), rev(0))],
        out_shape=[jax.ShapeDtypeStruct((s, GLA_KW), BF16), jax.ShapeDtypeStruct((s, GLA_KW), BF16),
                   jax.ShapeDtypeStruct((s, GLA_W), BF16), jax.ShapeDtypeStruct((s, GLA_KW), F32)],
        scratch_shapes=[pltpu.VMEM((GLA_DV, GLA_DK), F32)],
        compiler_params=_cp("parallel", "arbitrary"), name=name,
    )(proj, proj, proj, log_a, do, states, states)


def _row_tile(r):
    tr = min(ROW_TILE, r)
    while r % tr or tr % 8:
        tr -= 1
    return tr


def _adamw_math(w, g, m, v):
    m = ADAM_B1 * m + (1.0 - ADAM_B1) * g
    v = ADAM_B2 * v + (1.0 - ADAM_B2) * (g * g)
    m_hat = m / (1.0 - ADAM_B1 ** ADAM_STEP)
    v_hat = v / (1.0 - ADAM_B2 ** ADAM_STEP)
    delta = -ADAM_LR * (m_hat / (jnp.sqrt(v_hat) + ADAM_EPS) + ADAM_WD * w)
    return delta, m, v


COL_TILE = 256


def _tile_2d(r, c):
    if r % 8 == 0 and _row_tile(r) >= 64:
        return _row_tile(r), c
    assert c % COL_TILE == 0, (r, c)
    return r, COL_TILE


def _half_shape(shape):
    r, c = shape[-2:]
    return tuple(shape[:-2]) + ((r // 2, c) if _half_axis(r) == 0 else (r, c // 2))


def _adam(g, w, m, v, *, name):
    r, c = w.shape
    tr, tc = _tile_2d(r, c)

    def body(g_ref, w_ref, m_ref, v_ref, d_ref, mo_ref, vo_ref):
        d, mn, vn = _adamw_math(w_ref[...], g_ref[...], m_ref[...], v_ref[...])
        d_ref[...] = d
        mo_ref[...] = mn
        vo_ref[...] = vn

    spec = pl.BlockSpec((tr, tc), lambda i, j: (i, j))
    return pl.pallas_call(
        body, grid=(r // tr, c // tc), in_specs=[spec] * 4, out_specs=[spec] * 3,
        out_shape=[jax.ShapeDtypeStruct((r, c), F32)] * 3,
        compiler_params=_cp("parallel", "parallel"), name=name,
    )(g, w, m, v)


def _ada_grad_adam(c_all_t, dmod_cols, w, m, v, *, name):
    r, c = w.shape
    tr, tc = min(512, r), min(1024, c)

    def body(ct_ref, dm_ref, w_ref, m_ref, v_ref, g_ref, d_ref, mo_ref, vo_ref):
        g = _dot_nn(ct_ref[...], dm_ref[...], HIGHEST)
        g_ref[...] = g
        d, mn, vn = _adamw_math(w_ref[...], g, m_ref[...], v_ref[...])
        d_ref[...] = d
        mo_ref[...] = mn
        vo_ref[...] = vn

    spec = pl.BlockSpec((tr, tc), lambda i, j: (i, j))
    nb = c_all_t.shape[1]
    return pl.pallas_call(
        body, grid=(r // tr, c // tc),
        in_specs=[pl.BlockSpec((tr, nb), lambda i, j: (i, 0)), pl.BlockSpec((nb, tc), lambda i, j: (0, j)),
                  spec, spec, spec],
        out_specs=[spec] * 4, out_shape=[jax.ShapeDtypeStruct((r, c), F32)] * 4,
        compiler_params=_cp("parallel", "parallel"), name=name,
    )(c_all_t, dmod_cols, w, m, v)


def _mod_shard(c_all, w, b, *, name):
    k, c = w.shape
    tc = min(512, c)
    nb = c_all.shape[0]

    def body(c_ref, w_ref, b_ref, o_ref):
        o_ref[...] = _dot_nn(c_ref[...], w_ref[...], HIGHEST) + b_ref[...]

    return pl.pallas_call(
        body, grid=(c // tc,),
        in_specs=[pl.BlockSpec((nb, k), lambda j: (0, 0)), pl.BlockSpec((k, tc), lambda j: (0, j)),
                  pl.BlockSpec((1, tc), lambda j: (0, j))],
        out_specs=pl.BlockSpec((nb, tc), lambda j: (0, j)),
        out_shape=jax.ShapeDtypeStruct((nb, c), F32),
        compiler_params=_cp("parallel"), name=name,
    )(c_all, w, b)


def _silu_rows(c, *, name):
    def body(c_ref, o_ref):
        cv = c_ref[...]
        o_ref[...] = cv * _sigmoid(cv)

    return pl.pallas_call(body, out_shape=jax.ShapeDtypeStruct(c.shape, F32), name=name)(c)


def _pair_sum(g, got, idx, *, name):
    p, r, c = g.shape
    ax = _half_axis(r)
    hr, hc = _half_shape((r, c))
    tr, tc = _tile_2d(hr, hc)
    nbr, nbc = hr // tr, hc // tc

    def body(idx_ref, a_ref, b_ref, o_ref):
        o_ref[...] = (a_ref[...].astype(F32) + b_ref[...].astype(F32)).astype(BF16)

    def slot(i, idx_ref):
        return i + jnp.where(i >= idx_ref[1], 1, 0)

    def own_map(i, j, k, idx_ref):
        return (slot(i, idx_ref), j + (idx_ref[0] * nbr if ax == 0 else 0), k + (idx_ref[0] * nbc if ax == 1 else 0))

    half_spec = pl.BlockSpec((1, tr, tc), lambda i, j, k, idx_ref: (slot(i, idx_ref), j, k))
    return pl.pallas_call(
        body,
        grid_spec=pltpu.PrefetchScalarGridSpec(
            num_scalar_prefetch=1, grid=(p - 1, nbr, nbc),
            in_specs=[pl.BlockSpec((1, tr, tc), own_map), half_spec],
            out_specs=half_spec),
        out_shape=jax.ShapeDtypeStruct((p, hr, hc), BF16),
        compiler_params=_cp("parallel", "parallel", "parallel"), name=name,
    )(idx, g, got)


def _final_sum(g, got, parts, idx, *, name):
    shard_shape = g.shape[1:]
    ax = _half_axis(shard_shape[0])
    hr, hc = got.shape[1:]
    tr, tc = _tile_2d(hr, hc)
    nbr, nbc = hr // tr, hc // tc

    def body(idx_ref, g_ref, got_ref, parts_ref, o_ref):
        acc = g_ref[0].astype(F32) + got_ref[0].astype(F32)
        for q in range(3):
            acc = acc + parts_ref[q].astype(F32)
        o_ref[...] = acc

    def half_c(j, k, idx_ref):
        return (j + (idx_ref[0] * nbr if ax == 0 else 0), k + (idx_ref[0] * nbc if ax == 1 else 0))

    return pl.pallas_call(
        body,
        grid_spec=pltpu.PrefetchScalarGridSpec(
            num_scalar_prefetch=1, grid=(nbr, nbc),
            in_specs=[pl.BlockSpec((1, tr, tc), lambda j, k, idx_ref: (idx_ref[1],) + half_c(j, k, idx_ref)),
                      pl.BlockSpec((1, tr, tc), lambda j, k, idx_ref: (idx_ref[1], j, k)),
                      pl.BlockSpec((3, tr, tc), lambda j, k, idx_ref: (0, j, k))],
            out_specs=pl.BlockSpec((tr, tc), half_c)),
        out_shape=jax.ShapeDtypeStruct(tuple(shard_shape), F32),
        compiler_params=_cp("parallel", "parallel"), name=name,
    )(idx, g, got, parts)


def _stack_sum(x, *, name):
    p, r, c = x.shape
    tr = _row_tile(r)

    def body(x_ref, o_ref):
        acc = x_ref[0].astype(F32)
        for q in range(1, p):
            acc = acc + x_ref[q].astype(F32)
        o_ref[...] = acc

    return pl.pallas_call(
        body, grid=(r // tr,),
        in_specs=[pl.BlockSpec((p, tr, c), lambda i: (0, i, 0))],
        out_specs=pl.BlockSpec((tr, c), lambda i: (i, 0)),
        out_shape=jax.ShapeDtypeStruct((r, c), F32),
        compiler_params=_cp("parallel"), name=name,
    )(x)


def _place():
    x, y, c = lax.axis_index("x"), lax.axis_index("y"), lax.axis_index("c")
    chips = [(1 - x, y), (x, 1 - y), (1 - x, 1 - y)]
    return x, y, c, chips


def _gather8(x_shard, *, name):
    m_per, n = x_shard.shape

    def body(x_ref, out_ref, send_sems, recv_sems, local_sem):
        x, y, c, chips = _place()
        me, sibling = (x, y, c), (x, y, 1 - c)

        def rows(px, py, pc):
            return out_ref.at[pl.ds((4 * px + 2 * py + pc) * m_per, m_per), :]

        def copy(k, block, to, src=None):
            return pltpu.make_async_remote_copy(
                src_ref=rows(*block) if src is None else src, dst_ref=rows(*block),
                send_sem=send_sems.at[k], recv_sem=recv_sems.at[k], device_id=to, device_id_type=MESH)

        mine = pltpu.make_async_copy(x_ref, rows(*me), local_sem)
        mine.start()
        first = [copy(0, me, sibling, src=x_ref)]
        first += [copy(1 + j, me, (*chip, c), src=x_ref) for j, chip in enumerate(chips)]
        for cp in first:
            cp.start()
        passed = [copy(4 + j, (*chip, c), sibling) for j, chip in enumerate(chips)]
        for j, chip in enumerate(chips):
            copy(1 + j, (*chip, c), me).wait_recv()
            passed[j].start()
        copy(0, sibling, me).wait_recv()
        for j, chip in enumerate(chips):
            copy(4 + j, (*chip, 1 - c), me).wait_recv()
        for cp in first + passed:
            cp.wait_send()
        mine.wait()

    return pl.pallas_call(
        body,
        out_shape=jax.ShapeDtypeStruct((8 * m_per, n), x_shard.dtype),
        in_specs=[pl.BlockSpec(memory_space=pltpu.VMEM)],
        out_specs=pl.BlockSpec(memory_space=pltpu.VMEM),
        scratch_shapes=[pltpu.SemaphoreType.DMA((7,)), pltpu.SemaphoreType.DMA((7,)), pltpu.SemaphoreType.DMA],
        name=name,
    )(x_shard)


def _gather_relayed(shard, *, name):
    def body(shard_ref, full_ref, send_sems, recv_sems):
        x, y, c, _ = _place()
        ax = _half_axis(shard_ref.shape[0])
        me, xn, yn, dg = 2 * x + y, 2 * (1 - x) + y, 2 * x + (1 - y), 2 * (1 - x) + (1 - y)
        to_x, to_y = (1 - x, y, c), (x, 1 - y, c)

        def half(slot, piece=None):
            return _rows_half(full_ref.at[slot], c, ax, piece)

        def copy(k, src, dst, peer):
            return pltpu.make_async_remote_copy(src_ref=src, dst_ref=dst, send_sem=send_sems.at[k],
                                                recv_sem=recv_sems.at[k], device_id=peer, device_id_type=MESH)

        mine = _rows_half(shard_ref, c, ax)
        sends = [copy(0, mine, half(me), to_x), copy(1, mine, half(me), to_y)]
        for cp in sends:
            cp.start()
        copy(0, mine, half(xn), to_x).wait_recv()
        relay_y = copy(2, half(xn, (0, 2)), half(xn, (0, 2)), to_y)
        relay_y.start()
        copy(1, mine, half(yn), to_y).wait_recv()
        relay_x = copy(3, half(yn, (1, 2)), half(yn, (1, 2)), to_x)
        relay_x.start()
        copy(2, half(xn, (0, 2)), half(dg, (0, 2)), to_y).wait_recv()
        copy(3, half(yn, (1, 2)), half(dg, (1, 2)), to_x).wait_recv()
        for cp in sends + [relay_y, relay_x]:
            cp.wait_send()

    any_spec = pl.BlockSpec(memory_space=pl.ANY)
    return pl.pallas_call(
        body, out_shape=jax.ShapeDtypeStruct((4,) + shard.shape, shard.dtype),
        in_specs=[any_spec], out_specs=any_spec,
        scratch_shapes=[pltpu.SemaphoreType.DMA((4,)), pltpu.SemaphoreType.DMA((4,))], name=name,
    )(shard)


def _plan_start(plan, send_sems, recv_sems):
    for k, (src, dst, _, peer) in enumerate(plan):
        pltpu.make_async_remote_copy(src_ref=src, dst_ref=dst, send_sem=send_sems.at[k], recv_sem=recv_sems.at[k],
                                     device_id=peer, device_id_type=MESH).start()


def _plan_wait(plan, send_sems, recv_sems):
    for k, (src, _, land, peer) in enumerate(plan):
        pltpu.make_async_remote_copy(src_ref=src, dst_ref=land, send_sem=send_sems.at[k], recv_sem=recv_sems.at[k],
                                     device_id=peer, device_id_type=MESH).wait_recv()
    for k, (src, dst, _, peer) in enumerate(plan):
        pltpu.make_async_remote_copy(src_ref=src, dst_ref=dst, send_sem=send_sems.at[k], recv_sem=recv_sems.at[k],
                                     device_id=peer, device_id_type=MESH).wait_send()


def _half_axis(rows):
    return 0 if rows % 32 == 0 else 1


def _rows_half(ref, hc, axis, part=None):
    size = ref.shape[axis] // 2
    start = hc * size
    if part is not None:
        size //= part[1]
        start = start + part[0] * size
    idx = [slice(None)] * len(ref.shape)
    idx[axis] = pl.ds(start, size)
    return ref.at[tuple(idx)]


def _plan_gather_ici(shard, full, part=None):
    x, y, c, chips = _place()
    ax = _half_axis(shard.shape[0])
    src = _rows_half(shard, c, ax, part)
    return [(src, _rows_half(full.at[2 * x + y], c, ax, part), _rows_half(full.at[2 * cx + cy], c, ax, part),
             (cx, cy, c)) for cx, cy in chips]


def _plan_gather_d2d(full, own):
    x, y, c, chips = _place()
    ax = _half_axis(full.shape[1])
    plan = []
    for cx, cy in chips:
        slot = full.at[2 * cx + cy]
        plan.append((_rows_half(slot, c, ax), _rows_half(slot, c, ax), _rows_half(slot, 1 - c, ax), (x, y, 1 - c)))
    mine = full.at[2 * x + y]
    plan.append((own, mine, mine, (x, y, 1 - c)))
    return plan


def _plan_pair(grad, got):
    x, y, c, _ = _place()
    return [(_rows_half(grad, 1 - c, 1 + _half_axis(grad.shape[1])), got, got, (x, y, 1 - c))]


def _plan_shard_ici(sums, parts, piece=None):
    _, _, c, chips = _place()

    def rows(ref):
        if piece is None:
            return ref
        k, n = piece
        if ref.shape[0] % (16 * n) == 0:
            size = ref.shape[0] // n
            return ref.at[pl.ds(k * size, size), :]
        size = ref.shape[1] // n
        return ref.at[:, pl.ds(k * size, size)]

    return [(rows(sums.at[2 * cx + cy]), rows(parts.at[k]), rows(parts.at[k]), (cx, cy, c))
            for k, (cx, cy) in enumerate(chips)]


def _plan_half(buf):
    x, y, c, _ = _place()
    ax = _half_axis(buf.shape[0])
    mine = _rows_half(buf, c, ax)
    return [(mine, mine, _rows_half(buf, 1 - c, ax), (x, y, 1 - c))]


def _comm_call(plan_fn, inputs, out_shapes, *, name, aliases=None):
    ni, no = len(inputs), len(out_shapes)

    def body(*refs):
        plan = plan_fn(refs[:ni], refs[ni:ni + no])
        send_sems, recv_sems = refs[ni + no:]
        _plan_start(plan, send_sems, recv_sems)
        _plan_wait(plan, send_sems, recv_sems)

    any_spec = pl.BlockSpec(memory_space=pl.ANY)
    n_copies = 3 * max(ni, no)
    return pl.pallas_call(
        body, out_shape=list(out_shapes), in_specs=[any_spec] * ni, out_specs=[any_spec] * no,
        scratch_shapes=[pltpu.SemaphoreType.DMA((n_copies,)), pltpu.SemaphoreType.DMA((n_copies,))],
        input_output_aliases=aliases or {}, name=name,
    )(*inputs)


def _gather_forward(full, own, *, name):
    return _comm_call(lambda ins, outs: _plan_gather_d2d(outs[0], ins[1]), [full, own],
                      [jax.ShapeDtypeStruct(full.shape, full.dtype)], name=name, aliases={0: 0})[0]


def _half_exchange(bufs, *, name):
    return _comm_call(lambda ins, outs: [cp for o in outs for cp in _plan_half(o)],
                      bufs, [jax.ShapeDtypeStruct(b.shape, b.dtype) for b in bufs], name=name,
                      aliases={k: k for k in range(len(bufs))})


def _split_w_in(w_in_t):
    d = w_in_t.shape[1]
    main = jnp.concatenate([w_in_t[0:3072], w_in_t[3080:5128], w_in_t[5144:6168]], axis=0)
    small = jnp.concatenate([w_in_t[3072:3080], w_in_t[5128:5144], jnp.zeros((SMALL_W - 24, d), w_in_t.dtype)], axis=0)
    return main, small


def _merge_dw_in(dw_main, dw_small):
    return jnp.concatenate([dw_main[0:3072], dw_small[0:8], dw_main[3072:5120], dw_small[8:24], dw_main[5120:6144]],
                           axis=0)


def _gather_side(shards):
    return _Side(shards, [jax.ShapeDtypeStruct((4,) + w.shape, w.dtype) for w in shards],
                 lambda ins, outs: [cp for i, o in zip(ins, outs) for cp in _plan_gather_ici(i, o)], 3 * len(shards))


def _forward_side(full, own):
    return _Side([full, own], [jax.ShapeDtypeStruct(full.shape, full.dtype)],
                 lambda ins, outs: _plan_gather_d2d(outs[0], ins[1]), 4, aliases={0: 0})


def _half_side(bufs):
    return _Side(bufs, [jax.ShapeDtypeStruct(b.shape, b.dtype) for b in bufs],
                 lambda ins, outs: [cp for o in outs for cp in _plan_half(o)], len(bufs),
                 aliases={k: k for k in range(len(bufs))})


def _parts_shape(sums):
    return jax.ShapeDtypeStruct((3,) + sums.shape[1:], sums.dtype)


def _got_shape(grad):
    return jax.ShapeDtypeStruct(_half_shape(grad.shape), grad.dtype)


def _pair_side(grad):
    return _Side([grad], [_got_shape(grad)], lambda ins, outs: _plan_pair(ins[0], outs[0]), 1)


def _local_step(x, target, mod, g_pre_mix, g_post_mix, g_pre_mlp, g_post_mlp, gw_in, b_fgate, w_gla_a2,
                b_gla_a2, g_fox, g_gla, own_w_in, own_w_out, own_w_mlp_in, own_w_mlp_out, idx):
    s, d = x.shape
    shift_m, scale_m, gate_m, shift_f, scale_f, gate_f = [mod[:, i * d:(i + 1) * d] for i in range(6)]
    a1 = g_pre_mix * (1.0 + scale_m)
    a2 = g_pre_mlp * (1.0 + scale_f)
    bf = jnp.concatenate([b_fgate, jnp.zeros((1, SMALL_W - FOX_HEADS), F32)], axis=1)
    w2p = jnp.zeros((SMALL_W, GLA_KW), F32).at[FOX_HEADS:FOX_HEADS + GLA_RANK].set(w_gla_a2)

    h1, gw_in = _pre_fwd(x, a1, shift_m, name="pre_mix_fwd", side=_forward_side(gw_in, own_w_in))
    w_in_t = gw_in.reshape(-1, d)
    w_main, w_small = _split_w_in(w_in_t)
    full_shape = lambda w: jax.ShapeDtypeStruct((4,) + w.shape, w.dtype)
    first_side = _Side(
        [own_w_out, own_w_mlp_out], [full_shape(own_w_out), full_shape(own_w_mlp_out)],
        lambda ins, outs: _plan_gather_ici(ins[0], outs[0]) + _plan_gather_ici(ins[1], outs[1], part=(0, 4)), 6)
    proj, gw_out, gw_mlp_out = _mm(h1, w_main, mode="nt", out_dtypes=[BF16], name="in_proj_main", side=first_side)
    ps, gw_out = _mm(h1, w_small, mode="nt", out_dtypes=[F32], name="in_proj_small",
                     side=_forward_side(gw_out, own_w_out))
    w_out_full = gw_out.reshape(-1, d)
    cum, log_a = _gates_fwd(ps, bf, w2p, b_gla_a2, name="gates_fwd")
    cum_t = cum[:, :FOX_HEADS].T
    o_fox, fox_n, lse, gw_mlp_in = _fox_fwd(proj, cum_t, g_fox, name="fox_fwd", side=_gather_side([own_w_mlp_in]))
    o_gla, gla_n, states = _gla_fwd(proj, log_a, g_gla, name="gla_fwd")
    mixed = jnp.concatenate([fox_n, gla_n], axis=1)
    y1, gw_mlp_in = _mm(mixed, w_out_full, mode="nn", out_dtypes=[F32], name="out_proj",
                        side=_forward_side(gw_mlp_in, own_w_mlp_in))
    x1, h2 = _post_pre_fwd(x, y1, gate_m, g_post_mix, a2, shift_f, name="post_mix_pre_mlp_fwd")

    def mlp_act(acc):
        r = jnp.maximum(acc, 0.0)
        return acc, r * r

    rest_side = _Side([own_w_mlp_out, gw_mlp_out], [full_shape(own_w_mlp_out)],
                      lambda ins, outs: [cp for q in (1, 2, 3) for cp in _plan_gather_ici(ins[0], outs[0], part=(q, 4))],
                      9, aliases={1: 0})
    u, act, gw_mlp_out = _mm(h2, gw_mlp_in, mode="nn", out_dtypes=[BF16, BF16], epi=mlp_act, name="mlp_in",
                             b_slots=4, tm=MM_TM, side=rest_side)
    gw_mlp_out = _gather_forward(gw_mlp_out, own_w_mlp_out, name="gather_w_mlp_out_d2d")
    w_mlp_out_full = gw_mlp_out.reshape(-1, d)
    y2, = _mm(act, w_mlp_out_full, mode="nn", out_dtypes=[F32], name="mlp_out")
    dx2, dy2, loss_part, dgate_f, dg_post_mlp = _post_loss_bwd(x1, y2, gate_f, g_post_mlp, target,
                                                               name="post_mlp_loss_bwd")
    dw_mlp_out, = _mm(act, dy2, mode="tn", out_dtypes=[BF16], name="dw_mlp_out", tk=MM_TK_TOKENS)
    dw_mlp_out = dw_mlp_out.reshape(4, D_FF // 4, d)

    def act_bwd(acc, uv):
        return (acc * (2.0 * jnp.maximum(uv.astype(F32), 0.0)),)

    du, got_mlp_out = _mm(dy2, w_mlp_out_full, mode="nt", out_dtypes=[BF16], extras=[u], epi=act_bwd,
                          name="d_mlp_hidden", tm=MM_TM, side=_pair_side(dw_mlp_out))
    sum_mlp_out = _pair_sum(dw_mlp_out, got_mlp_out, idx, name="grad_pair_sum_mlp_out")
    nj = D_FF // 4 // min(MM_T, D_FF // 4)
    tmw = min(MM_T, d)
    dw_mlp_in, parts_mlp_out = _mm(
        h2, du, mode="tn", out_dtypes=[BF16], name="dw_mlp_in", tk=MM_TK_TOKENS,
        out_shapes=[jax.ShapeDtypeStruct((4, d, D_FF // 4), BF16)],
        out_specs=[pl.BlockSpec((1, tmw, min(MM_T, D_FF // 4)), lambda i, j, kk: (j // nj, i, j % nj))],
        side=_Side([sum_mlp_out], [_parts_shape(sum_mlp_out)],
                   lambda ins, outs: _plan_shard_ici(ins[0], outs[0], piece=(0, 2)), 3))
    dh2, got_mlp_in, parts_mlp_out = _mm(
        du, gw_mlp_in, mode="nt", out_dtypes=[F32], name="d_mlp_in", b_slots=4,
        side=_Side([dw_mlp_in, sum_mlp_out, parts_mlp_out], [_got_shape(dw_mlp_in), _parts_shape(sum_mlp_out)],
                   lambda ins, outs: _plan_pair(ins[0], outs[0]) + _plan_shard_ici(ins[1], outs[1], piece=(1, 2)),
                   4, aliases={2: 1}))
    sum_mlp_in = _pair_sum(dw_mlp_in, got_mlp_in, idx, name="grad_pair_sum_mlp_in")
    dx1, dshift_f, da2, dy1, dgate_m, dg_post_mix = _pre_post_bwd(dh2, x1, dx2, a2, y1, gate_m, g_post_mix,
                                                                  name="pre_mlp_post_mix_bwd")
    buf_mlp_out = _final_sum(dw_mlp_out, got_mlp_out, parts_mlp_out, idx, name="grad_final_sum_mlp_out")
    dw_out, g_mlp_out = _mm(mixed, dy1, mode="tn", out_dtypes=[BF16], name="dw_out", tk=MM_TK_TOKENS,
                            side=_half_side([buf_mlp_out]))
    dw_out = dw_out.reshape(4, d // 4, d)
    dmixed, got_out = _mm(dy1, w_out_full, mode="nt", out_dtypes=[BF16], name="d_mixed", side=_pair_side(dw_out))
    sum_out = _pair_sum(dw_out, got_out, idx, name="grad_pair_sum_out")
    do_fox, delta, dg_fox = _head_norm_bwd(dmixed, o_fox, g_fox, None, nh=FOX_HEADS, hd=FOX_HD, dn_col=0,
                                           gr_col=0, name="fox_norm_bwd")
    do_gla, dgr, _, dg_gla = _head_norm_bwd(dmixed, o_gla, g_gla, proj, nh=GLA_HEADS, hd=GLA_DV, dn_col=1,
                                            gr_col=(3 * FOX_W + 2 * GLA_KW + GLA_W) // GLA_W, name="gla_norm_bwd")
    dq_fox, dk_fox, dv_fox, dcq, dck_t, parts_mlp_in, parts_out = _fox_bwd(
        proj, do_fox, cum_t, lse, delta, name="fox_bwd",
        side=_Side([sum_mlp_in, sum_out], [_parts_shape(sum_mlp_in), _parts_shape(sum_out)],
                   lambda ins, outs: _plan_shard_ici(ins[0], outs[0]) + _plan_shard_ici(ins[1], outs[1]), 6))
    dgq, dgk, dgv, dla = _gla_bwd(proj, log_a, do_gla, states, name="gla_bwd")
    dck = dcq + jnp.concatenate([dck_t.T, jnp.zeros((s, SMALL_W - FOX_HEADS), F32)], axis=1)
    dps, dbf, dw2p, db2 = _gates_bwd(dck, ps, bf, w2p, b_gla_a2, dla, name="gates_bwd")
    dproj = jnp.concatenate([dq_fox.astype(BF16), dk_fox, dv_fox, dgq, dgk, dgv, dgr], axis=1)
    buf_mlp_in = _final_sum(dw_mlp_in, got_mlp_in, parts_mlp_in, idx, name="grad_final_sum_mlp_in")
    buf_out = _final_sum(dw_out, got_out, parts_out, idx, name="grad_final_sum_out")
    dw_main, g_mlp_in, g_out = _mm(dproj, h1, mode="tn", out_dtypes=[BF16], name="dw_in_main", tk=MM_TK_TOKENS,
                                   side=_half_side([buf_mlp_in, buf_out]))
    dw_small, = _mm(dps, h1, mode="tn", out_dtypes=[BF16], name="dw_in_small")
    rs_in = w_in_t.shape[0] // 4
    dw_in = _merge_dw_in(dw_main, dw_small).reshape(4, rs_in, d)
    dh1_small, got_in = _mm(dps, w_small, mode="nn", out_dtypes=[F32], name="d_h1_small", side=_pair_side(dw_in))
    sum_in = _pair_sum(dw_in, got_in, idx, name="grad_pair_sum_in")
    dh1, parts_in = _mm(
        dproj, w_main, mode="nn", out_dtypes=[F32], extras=[dh1_small], epi=lambda acc, e: (acc + e,), name="d_h1",
        tk=MAIN_W // 2,
        side=_Side([sum_in], [_parts_shape(sum_in)],
                   lambda ins, outs: [cp for q in range(3) for cp in _plan_shard_ici(ins[0], outs[0], piece=(q, 4))],
                   9))
    grad_x, dshift_m, da1, parts_in = _pre_bwd(
        dh1, x, dx1, a1, name="pre_mix_bwd",
        side=_Side([sum_in, parts_in], [_parts_shape(sum_in)],
                   lambda ins, outs: _plan_shard_ici(ins[0], outs[0], piece=(3, 4)), 3, aliases={1: 0}))
    buf_in = _final_sum(dw_in, got_in, parts_in, idx, name="grad_final_sum_in")
    g_in, = _half_exchange([buf_in], name="grad_half_exchange_in")
    g_big = [g_in, g_out, g_mlp_in, g_mlp_out]

    dmod = jnp.concatenate([dshift_m, da1 * g_pre_mix, dgate_m, dshift_f, da2 * g_pre_mlp, dgate_f], axis=1)
    small = dict(
        dmod=dmod, g_pre_mix=da1 * (1.0 + scale_m), g_post_mix=dg_post_mix, g_pre_mlp=da2 * (1.0 + scale_f),
        g_post_mlp=dg_post_mlp, b_fgate=dbf[:, :FOX_HEADS], w_gla_a2=dw2p[FOX_HEADS:FOX_HEADS + GLA_RANK],
        b_gla_a2=db2, g_fox_out=dg_fox, g_gla_out=dg_gla)
    return loss_part, grad_x, g_big, small


def _pack(arrays):
    flat = jnp.concatenate([a.reshape(-1).astype(F32) for a in arrays])
    n = flat.shape[0]
    rows = -(-n // 128)
    rows = -(-rows // 8) * 8
    return jnp.pad(flat, (0, rows * 128 - n)).reshape(rows, 128)


def _unpack(buf, shapes):
    flat = buf.reshape(-1)
    out, off = [], 0
    for shp in shapes:
        n = 1
        for q in shp:
            n *= q
        out.append(flat[off:off + n].reshape(shp))
        off += n
    return out


SMALL_GRAD_ORDER = ["dmod", "g_pre_mix", "g_post_mix", "g_pre_mlp", "g_post_mlp", "b_fgate", "w_gla_a2", "b_gla_a2",
                    "g_fox_out", "g_gla_out"]


def kernel(x, c, w_ada, b_ada, g_pre_mix, g_post_mix, w_in, b_fgate, w_gla_a2, b_gla_a2, g_fox_out, g_gla_out, w_out, g_pre_mlp, g_post_mlp, w_mlp_in, w_mlp_out, loss_target, m_w_ada, m_b_ada, m_g_pre_mix, m_g_post_mix, m_w_in, m_b_fgate, m_w_gla_a2, m_b_gla_a2, m_g_fox_out, m_g_gla_out, m_w_out, m_g_pre_mlp, m_g_post_mlp, m_w_mlp_in, m_w_mlp_out, v_w_ada, v_b_ada, v_g_pre_mix, v_g_post_mix, v_w_in, v_b_fgate, v_w_gla_a2, v_b_gla_a2, v_g_fox_out, v_g_gla_out, v_w_out, v_g_pre_mlp, v_g_post_mlp, v_w_mlp_in, v_w_mlp_out):
    ix, iy, ic = lax.axis_index("x"), lax.axis_index("y"), lax.axis_index("c")
    chip = 2 * ix + iy
    dev = 4 * ix + 2 * iy + ic
    d = D_MODEL

    c_act = _silu_rows(c, name="silu_c")
    pack1 = _pack([c_act, w_gla_a2[0], g_gla_out[0]])
    rows1 = pack1.shape[0]
    got1 = _gather8(pack1, name="gather_small_fwd").reshape(8, rows1, 128)
    per_dev = [_unpack(got1[q], [(d,), (GLA_RANK, GLA_KW // 4), (GLA_HEADS, GLA_DV // 4)]) for q in range(8)]
    c_all = jnp.stack([p[0] for p in per_dev])
    w_gla_a2_full = jnp.concatenate([per_dev[2 * j][1] for j in range(4)], axis=1)
    g_gla_full = jnp.concatenate([per_dev[2 * j][2] for j in range(4)], axis=1)
    cols = w_ada.shape[2]
    b_ada_shard = lax.dynamic_slice_in_dim(b_ada, chip * cols, cols, axis=1)
    mod_sh = _mod_shard(c_all, w_ada[0], b_ada_shard, name="ada_mod")
    got2 = _gather8(mod_sh, name="gather_mod").reshape(8, 8, cols)
    mod_all = jnp.concatenate([got2[2 * j] for j in range(4)], axis=1)
    mod = lax.dynamic_slice_in_dim(mod_all, dev, 1, axis=0)

    tr_in = lambda a: jnp.transpose(a[0])
    own_bf = [tr_in(w_in).astype(BF16), w_out[0].astype(BF16), w_mlp_in[0].astype(BF16), w_mlp_out[0].astype(BF16)]
    gw_in = _gather_relayed(own_bf[0], name="gather_w_in_ici")
    idx = jnp.stack([ic, chip]).astype(jnp.int32)
    loss_part, grad_x, g_big, small = _local_step(
        x[0], loss_target[0], mod, g_pre_mix, g_post_mix, g_pre_mlp, g_post_mlp, gw_in, b_fgate,
        w_gla_a2_full, b_gla_a2, g_fox_out[0], g_gla_full, own_bf[0], own_bf[1], own_bf[2], own_bf[3], idx)
    loss = lax.psum(loss_part[0, 0], ("x", "y", "c"))

    big_w = [(tr_in(w_in), tr_in(m_w_in), tr_in(v_w_in)), (w_out[0], m_w_out[0], v_w_out[0]),
             (w_mlp_in[0], m_w_mlp_in[0], v_w_mlp_in[0]), (w_mlp_out[0], m_w_mlp_out[0], v_w_mlp_out[0])]
    big_res = []
    for q, (g, (w, m, v)) in enumerate(zip(g_big, big_w)):
        res4 = (g,) + tuple(_adam(g, w, m, v, name=f"adam_big_{q}"))
        big_res.append(tuple((jnp.transpose(a) if q == 0 else a)[None] for a in res4))

    pack2 = _pack([small[k] for k in SMALL_GRAD_ORDER])
    rows2 = pack2.shape[0]
    got3 = _gather8(pack2, name="gather_small_grads").reshape(8, rows2, 128)
    dmod_all = got3[:, :6 * d // 128, :].reshape(8, 6 * d)
    sums = _stack_sum(got3, name="small_grad_sum")
    shapes = [(1, 6 * d), (1, d), (1, d), (1, d), (1, d), (1, FOX_HEADS), (1, GLA_RANK, GLA_KW), (1, GLA_KW),
              (1, FOX_HEADS, FOX_HD), (1, GLA_HEADS, GLA_DV)]
    sg = dict(zip(["b_ada"] + SMALL_GRAD_ORDER[1:], _unpack(sums, shapes)))
    sg["w_gla_a2"] = lax.dynamic_slice_in_dim(sg["w_gla_a2"], chip * (GLA_KW // 4), GLA_KW // 4, axis=2)
    sg["g_gla_out"] = lax.dynamic_slice_in_dim(sg["g_gla_out"], chip * (GLA_DV // 4), GLA_DV // 4, axis=2)
    small_names = ["b_ada", "g_pre_mix", "g_post_mix", "b_fgate", "w_gla_a2", "b_gla_a2", "g_fox_out", "g_gla_out",
                   "g_pre_mlp", "g_post_mlp"]
    small_w = dict(b_ada=(b_ada, m_b_ada, v_b_ada), g_pre_mix=(g_pre_mix, m_g_pre_mix, v_g_pre_mix),
                   g_post_mix=(g_post_mix, m_g_post_mix, v_g_post_mix), b_fgate=(b_fgate, m_b_fgate, v_b_fgate),
                   w_gla_a2=(w_gla_a2, m_w_gla_a2, v_w_gla_a2), b_gla_a2=(b_gla_a2, m_b_gla_a2, v_b_gla_a2),
                   g_fox_out=(g_fox_out, m_g_fox_out, v_g_fox_out), g_gla_out=(g_gla_out, m_g_gla_out, v_g_gla_out),
                   g_pre_mlp=(g_pre_mlp, m_g_pre_mlp, v_g_pre_mlp), g_post_mlp=(g_post_mlp, m_g_post_mlp, v_g_post_mlp))
    sshapes = [small_w[k][0].shape for k in small_names]
    pg = _pack([sg[k] for k in small_names])
    pw, pm, pv = [_pack([small_w[k][q] for k in small_names]) for q in range(3)]
    pd, pmn, pvn = _adam(pg, pw, pm, pv, name="adam_small")
    s_delta = dict(zip(small_names, _unpack(pd, sshapes)))
    s_m = dict(zip(small_names, _unpack(pmn, sshapes)))
    s_v = dict(zip(small_names, _unpack(pvn, sshapes)))

    dmod_cols = lax.dynamic_slice_in_dim(dmod_all, chip * cols, cols, axis=1)
    g_ada, d_ada, m_ada, v_ada = _ada_grad_adam(c_all.T, dmod_cols, w_ada[0], m_w_ada[0], v_w_ada[0], name="ada_grad_adam")

    order = ["w_ada", "b_ada", "g_pre_mix", "g_post_mix", "w_in", "b_fgate", "w_gla_a2", "b_gla_a2", "g_fox_out",
             "g_gla_out", "w_out", "g_pre_mlp", "g_post_mlp", "w_mlp_in", "w_mlp_out"]
    res = {"w_ada": (g_ada[None], d_ada[None], m_ada[None], v_ada[None]),
           "w_in": big_res[0], "w_out": big_res[1], "w_mlp_in": big_res[2], "w_mlp_out": big_res[3]}
    for k in small_names:
        res[k] = (sg[k], s_delta[k], s_m[k], s_v[k])
    return (loss, grad_x[None], *[res[k][0] for k in order], *[res[k][1] for k in order],
            *[res[k][2] for k in order], *[res[k][3] for k in order])
```

```python
import functools

import jax
import jax.numpy as jnp
from jax import lax
from jax.experimental import pallas as pl
from jax.experimental.pallas import tpu as pltpu

F32 = jnp.float32
BF16 = jnp.bfloat16
MESH = pl.DeviceIdType.MESH
HIGHEST = lax.Precision.HIGHEST

D_MODEL = 2048
FOX_HEADS = 8
FOX_HD = 128
FOX_W = FOX_HEADS * FOX_HD
GLA_HEADS = 4
GLA_DK = 128
GLA_DV = 256
GLA_KW = GLA_HEADS * GLA_DK
GLA_W = GLA_HEADS * GLA_DV
GLA_RANK = 16
GLA_TEMP = 16.0
CHUNK = 64
D_FF = 4 * D_MODEL
EPS = 1e-6
MAIN_W = 3 * FOX_W + 2 * GLA_KW + 2 * GLA_W
SMALL_W = 128
NEG = -1e30

ADAM_LR = 0.001
ADAM_B1 = 0.9
ADAM_B2 = 0.999
ADAM_EPS = 1e-08
ADAM_WD = 0.01
ADAM_STEP = 10

VMEM_LIMIT = 52 * 1024 * 1024
ROW_TILE = 256
WIDE_ROW_TILE = 512
FOX_TQ = 512
FOX_TK = 512
GLA_ROWS = 512
GATE_TS = 512
MM_T = 1024
MM_TK = 2048
MM_TK_TOKENS = 4096
MM_TM = 2048


def _cp(*sem):
    return pltpu.CompilerParams(dimension_semantics=sem, vmem_limit_bytes=VMEM_LIMIT)


def _dot_nn(a, b, precision=None):
    return jnp.dot(a, b, preferred_element_type=F32, precision=precision)


def _dot_nt(a, b, precision=None):
    return lax.dot_general(a, b, (((1,), (1,)), ((), ())), preferred_element_type=F32, precision=precision)


def _dot_tn(a, b, precision=None):
    return lax.dot_general(a, b, (((0,), (0,)), ((), ())), preferred_element_type=F32, precision=precision)


def _sigmoid(x):
    return 1.0 / (1.0 + jnp.exp(-x))


def _log_sigmoid(x):
    return jnp.minimum(x, 0.0) - jnp.log(1.0 + jnp.exp(-jnp.abs(x)))


class _Side:
    def __init__(self, inputs, out_shapes, plan_fn, n_copies, aliases=None):
        self.inputs, self.out_shapes, self.plan_fn, self.n_copies = list(inputs), list(out_shapes), plan_fn, n_copies
        self.aliases = dict(aliases or {})

    def scratch(self):
        return [pltpu.SemaphoreType.DMA((self.n_copies,)), pltpu.SemaphoreType.DMA((self.n_copies,))]

    def run(self, in_refs, out_refs, sems, first, last):
        @pl.when(first)
        def _():
            _plan_start(self.plan_fn(in_refs, out_refs), *sems)

        @pl.when(last)
        def _():
            _plan_wait(self.plan_fn(in_refs, out_refs), *sems)


def _mm(a, b, *, mode, out_dtypes, name, tm=None, tn=None, tk=None, extras=(), epi=None,
        out_shapes=None, out_specs=None, side=None, b_slots=0):
    tm, tn, tk = tm or MM_T, tn or MM_T, tk or MM_TK
    b2 = (b.shape[1], b_slots * b.shape[2]) if b_slots else b.shape
    if mode == "nn":
        (m, k), n = a.shape, b2[1]
    elif mode == "nt":
        (m, k), n = a.shape, b2[0]
    else:
        (k, m), n = a.shape, b2[1]
    tm, tn, tk = min(tm, m), min(tn, n), min(tk, k)
    if b_slots:
        tn = min(tn, b.shape[2]) if mode == "nn" else tn
        tk = min(tk, b.shape[2]) if mode == "nt" else tk
    assert m % tm == 0 and n % tn == 0 and k % tk == 0, (name, m, n, k)
    nk = k // tk
    n_out, n_ex = len(out_dtypes), len(extras)
    if epi is None:
        epi = lambda acc: tuple(acc for _ in range(n_out))
    dot = {"nn": _dot_nn, "nt": _dot_nt, "tn": _dot_tn}[mode]

    n_si = len(side.inputs) if side else 0
    n_so = len(side.out_shapes) if side else 0
    grid = (m // tm, n // tn, nk)

    def body(*refs):
        a_ref, b_ref = refs[0], refs[1]
        ex_refs = refs[2:2 + n_ex]
        base = 2 + n_ex + n_si
        o_refs = refs[base:base + n_out]
        scratch = refs[base + n_out + n_so:]
        if side:
            pos = [pl.program_id(q) for q in range(3)]
            first = (pos[0] == 0) & (pos[1] == 0) & (pos[2] == 0)
            last = (pos[0] == grid[0] - 1) & (pos[1] == grid[1] - 1) & (pos[2] == grid[2] - 1)
            side.run(refs[2 + n_ex:base], refs[base + n_out:base + n_out + n_so], scratch[-2:], first, last)
        part = dot(a_ref[...], b_ref[...])

        def finish(acc):
            outs = epi(acc, *[e[...] for e in ex_refs])
            for o_ref, val in zip(o_refs, outs):
                o_ref[...] = val.reshape(o_ref.shape).astype(o_ref.dtype)

        if nk == 1:
            finish(part)
        else:
            acc_ref = scratch[0]
            kk = pl.program_id(2)

            @pl.when(kk == 0)
            def _():
                acc_ref[...] = part

            @pl.when((kk > 0) & (kk < nk - 1))
            def _():
                acc_ref[...] += part

            @pl.when(kk == nk - 1)
            def _():
                finish(acc_ref[...] + part)

    if mode == "nn":
        a_spec = pl.BlockSpec((tm, tk), lambda i, j, kk: (i, kk))
        b_spec = pl.BlockSpec((tk, tn), lambda i, j, kk: (kk, j))
        if b_slots:
            per = b.shape[2] // tn
            b_spec = pl.BlockSpec((None, tk, tn), lambda i, j, kk: (j // per, kk, j % per))
    elif mode == "nt":
        a_spec = pl.BlockSpec((tm, tk), lambda i, j, kk: (i, kk))
        b_spec = pl.BlockSpec((tn, tk), lambda i, j, kk: (j, kk))
        if b_slots:
            per = b.shape[2] // tk
            b_spec = pl.BlockSpec((None, tn, tk), lambda i, j, kk: (kk // per, j, kk % per))
    else:
        assert not b_slots
        a_spec = pl.BlockSpec((tk, tm), lambda i, j, kk: (kk, i))
        b_spec = pl.BlockSpec((tk, tn), lambda i, j, kk: (kk, j))
    tile_spec = pl.BlockSpec((tm, tn), lambda i, j, kk: (i, j))
    if out_shapes is None:
        out_shapes = [jax.ShapeDtypeStruct((m, n), dt) for dt in out_dtypes]
    if out_specs is None:
        out_specs = [tile_spec for _ in out_dtypes]
    any_spec = pl.BlockSpec(memory_space=pl.ANY)
    res = pl.pallas_call(
        body,
        grid=grid,
        in_specs=[a_spec, b_spec] + [tile_spec for _ in extras] + [any_spec] * n_si,
        out_specs=list(out_specs) + [any_spec] * n_so,
        out_shape=list(out_shapes) + (side.out_shapes if side else []),
        scratch_shapes=([pltpu.VMEM((tm, tn), F32)] if nk > 1 else []) + (side.scratch() if side else []),
        compiler_params=_cp("arbitrary", "arbitrary", "arbitrary") if side else _cp("parallel", "parallel", "arbitrary"),
        input_output_aliases={2 + n_ex + si: n_out + so for si, so in side.aliases.items()} if side else {},
        name=name,
    )(a, b, *extras, *(side.inputs if side else []))
    return res


def _row_spec(ts, d):
    return pl.BlockSpec((ts, d), lambda i: (i, 0))


def _vec_spec(d):
    return pl.BlockSpec((1, d), lambda i: (0, 0))


def _side_args(side, n_in, n_out):
    if side is None:
        return [], [], [], [], [], {}
    any_spec = pl.BlockSpec(memory_space=pl.ANY)
    return ([any_spec] * len(side.inputs), [any_spec] * len(side.out_shapes), side.out_shapes, side.scratch(),
            side.inputs, {n_in + si: n_out + so for si, so in side.aliases.items()})


def _pre_fwd(x, avec, shift, *, name, side=None):
    s, d = x.shape
    ts = min(WIDE_ROW_TILE, s)
    nb = s // ts
    s_in, s_out, s_shapes, s_scratch, s_ops, s_alias = _side_args(side, 3, 1)

    def body(x_ref, a_ref, s_ref, *rest):
        h_ref = rest[len(s_in)]
        if side:
            step = pl.program_id(0)
            side.run(rest[:len(s_in)], rest[len(s_in) + 1:len(s_in) + 1 + len(s_out)],
                     rest[len(s_in) + 1 + len(s_out):], step == 0, step == nb - 1)
        xv = x_ref[...]
        r = lax.rsqrt(jnp.mean(xv * xv, axis=-1, keepdims=True) + EPS)
        h_ref[...] = (xv * r * a_ref[...] + s_ref[...]).astype(BF16)

    res = pl.pallas_call(
        body, grid=(nb,),
        in_specs=[_row_spec(ts, d), _vec_spec(d), _vec_spec(d)] + s_in,
        out_specs=[_row_spec(ts, d)] + s_out,
        out_shape=[jax.ShapeDtypeStruct((s, d), BF16)] + s_shapes,
        scratch_shapes=s_scratch, input_output_aliases=s_alias,
        compiler_params=_cp("arbitrary" if side else "parallel"), name=name,
    )(x, avec, shift, *s_ops)
    return res if side else res[0]


def _post_pre_fwd(x, y, gate, g, avec, shift, *, name):
    s, d = x.shape
    ts = min(WIDE_ROW_TILE, s)

    def body(x_ref, y_ref, gate_ref, g_ref, a_ref, s_ref, o_ref, h_ref):
        yv = y_ref[...]
        r = lax.rsqrt(jnp.mean(yv * yv, axis=-1, keepdims=True) + EPS)
        x1 = x_ref[...] + gate_ref[...] * (yv * r * g_ref[...])
        o_ref[...] = x1
        r1 = lax.rsqrt(jnp.mean(x1 * x1, axis=-1, keepdims=True) + EPS)
        h_ref[...] = (x1 * r1 * a_ref[...] + s_ref[...]).astype(BF16)

    return pl.pallas_call(
        body, grid=(s // ts,),
        in_specs=[_row_spec(ts, d), _row_spec(ts, d)] + [_vec_spec(d)] * 4,
        out_specs=[_row_spec(ts, d), _row_spec(ts, d)],
        out_shape=[jax.ShapeDtypeStruct((s, d), F32), jax.ShapeDtypeStruct((s, d), BF16)],
        compiler_params=_cp("parallel"), name=name,
    )(x, y, gate, g, avec, shift)


def _post_bwd_math(dxv, yv, gatev, gv):
    r = lax.rsqrt(jnp.mean(yv * yv, axis=-1, keepdims=True) + EPS)
    yhat = yv * r
    dn = dxv * gatev
    dyhat = dn * gv
    dy = r * (dyhat - yhat * jnp.mean(dyhat * yhat, axis=-1, keepdims=True))
    return dy, dxv * (yhat * gv), dn * yhat


def _accumulate(first, pairs):
    @pl.when(first)
    def _():
        for ref, _ in pairs:
            ref[...] = jnp.zeros_like(ref)

    for ref, val in pairs:
        ref[...] += jnp.sum(val, axis=0, keepdims=True)


def _post_loss_bwd(x, y, gate, g, target, *, name):
    s, d = x.shape
    ts = min(ROW_TILE, s)

    def body(x_ref, y_ref, gate_ref, g_ref, t_ref, dx_ref, dy_ref, loss_ref, dgate_ref, dg_ref):
        yv, gatev, gv = y_ref[...], gate_ref[...], g_ref[...]
        r = lax.rsqrt(jnp.mean(yv * yv, axis=-1, keepdims=True) + EPS)
        diff = x_ref[...] + gatev * (yv * r * gv) - t_ref[...]
        dxv = diff * (1.0 / d)
        dx_ref[...] = dxv
        dy, dgate_rows, dg_rows = _post_bwd_math(dxv, yv, gatev, gv)
        dy_ref[...] = dy.astype(BF16)
        first = pl.program_id(0) == 0
        _accumulate(first, [(dgate_ref, dgate_rows), (dg_ref, dg_rows)])

        @pl.when(first)
        def _():
            loss_ref[...] = jnp.zeros_like(loss_ref)

        loss_ref[...] += jnp.sum(jnp.mean(diff * diff, axis=-1, keepdims=True)) * 0.5

    return pl.pallas_call(
        body, grid=(s // ts,),
        in_specs=[_row_spec(ts, d), _row_spec(ts, d), _vec_spec(d), _vec_spec(d), _row_spec(ts, d)],
        out_specs=[_row_spec(ts, d), _row_spec(ts, d), pl.BlockSpec((1, 128), lambda i: (0, 0)), _vec_spec(d),
                   _vec_spec(d)],
        out_shape=[jax.ShapeDtypeStruct((s, d), F32), jax.ShapeDtypeStruct((s, d), BF16),
                   jax.ShapeDtypeStruct((1, 128), F32), jax.ShapeDtypeStruct((1, d), F32),
                   jax.ShapeDtypeStruct((1, d), F32)],
        compiler_params=_cp("arbitrary"), name=name,
    )(x, y, gate, g, target)


def _pre_post_bwd(dh, xin, dres, avec, y, gate, g, *, name):
    s, d = xin.shape
    ts = min(ROW_TILE, s)

    def body(dh_ref, x_ref, dres_ref, a_ref, y_ref, gate_ref, g_ref, dx_ref, dshift_ref, da_ref, dy_ref,
             dgate_ref, dg_ref):
        xv, dhv = x_ref[...], dh_ref[...]
        r = lax.rsqrt(jnp.mean(xv * xv, axis=-1, keepdims=True) + EPS)
        xhat = xv * r
        dxhat = dhv * a_ref[...]
        dxv = dres_ref[...] + r * (dxhat - xhat * jnp.mean(dxhat * xhat, axis=-1, keepdims=True))
        dx_ref[...] = dxv
        dy, dgate_rows, dg_rows = _post_bwd_math(dxv, y_ref[...], gate_ref[...], g_ref[...])
        dy_ref[...] = dy.astype(BF16)
        _accumulate(pl.program_id(0) == 0, [(dshift_ref, dhv), (da_ref, dhv * xhat), (dgate_ref, dgate_rows),
                                            (dg_ref, dg_rows)])

    return pl.pallas_call(
        body, grid=(s // ts,),
        in_specs=[_row_spec(ts, d), _row_spec(ts, d), _row_spec(ts, d), _vec_spec(d), _row_spec(ts, d),
                  _vec_spec(d), _vec_spec(d)],
        out_specs=[_row_spec(ts, d), _vec_spec(d), _vec_spec(d), _row_spec(ts, d), _vec_spec(d), _vec_spec(d)],
        out_shape=[jax.ShapeDtypeStruct((s, d), F32), jax.ShapeDtypeStruct((1, d), F32),
                   jax.ShapeDtypeStruct((1, d), F32), jax.ShapeDtypeStruct((s, d), BF16),
                   jax.ShapeDtypeStruct((1, d), F32), jax.ShapeDtypeStruct((1, d), F32)],
        compiler_params=_cp("arbitrary"), name=name,
    )(dh, xin, dres, avec, y, gate, g)


def _pre_bwd(dh, xin, dres, avec, *, name, side=None):
    s, d = xin.shape
    ts = min(WIDE_ROW_TILE, s)
    nb = s // ts
    s_in, s_out, s_shapes, s_scratch, s_ops, s_alias = _side_args(side, 4, 3)

    def body(dh_ref, x_ref, dres_ref, a_ref, *rest):
        dx_ref, dshift_ref, da_ref = rest[len(s_in):len(s_in) + 3]
        if side:
            step = pl.program_id(0)
            side.run(rest[:len(s_in)], rest[len(s_in) + 3:len(s_in) + 3 + len(s_out)],
                     rest[len(s_in) + 3 + len(s_out):], step == 0, step == nb - 1)
        xv, dhv = x_ref[...], dh_ref[...]
        r = lax.rsqrt(jnp.mean(xv * xv, axis=-1, keepdims=True) + EPS)
        xhat = xv * r
        dxhat = dhv * a_ref[...]
        dx_ref[...] = dres_ref[...] + r * (dxhat - xhat * jnp.mean(dxhat * xhat, axis=-1, keepdims=True))

        @pl.when(pl.program_id(0) == 0)
        def _():
            dshift_ref[...] = jnp.zeros_like(dshift_ref)
            da_ref[...] = jnp.zeros_like(da_ref)

        dshift_ref[...] += jnp.sum(dhv, axis=0, keepdims=True)
        da_ref[...] += jnp.sum(dhv * xhat, axis=0, keepdims=True)

    return pl.pallas_call(
        body, grid=(nb,),
        in_specs=[_row_spec(ts, d), _row_spec(ts, d), _row_spec(ts, d), _vec_spec(d)] + s_in,
        out_specs=[_row_spec(ts, d), _vec_spec(d), _vec_spec(d)] + s_out,
        out_shape=[jax.ShapeDtypeStruct((s, d), F32), jax.ShapeDtypeStruct((1, d), F32),
                   jax.ShapeDtypeStruct((1, d), F32)] + s_shapes,
        scratch_shapes=s_scratch, input_output_aliases=s_alias,
        compiler_params=_cp("arbitrary"), name=name,
    )(dh, xin, dres, avec, *s_ops)


def _tri(n, strict=False, upper=False):
    r = lax.broadcasted_iota(jnp.int32, (n, n), 0)
    c = lax.broadcasted_iota(jnp.int32, (n, n), 1)
    if upper:
        r, c = c, r
    return ((r > c) if strict else (r >= c)).astype(F32)


def _gates_fwd(ps, bf, w2p, b2, *, name):
    s = ps.shape[0]
    ts = min(GATE_TS, s)

    def body(ps_ref, bf_ref, w_ref, b2_ref, cum_ref, la_ref, carry_ref):
        @pl.when(pl.program_id(0) == 0)
        def _():
            carry_ref[...] = jnp.zeros_like(carry_ref)

        psv = ps_ref[...]
        lf = _log_sigmoid(psv + bf_ref[...])
        cum = _dot_nn(_tri(ts), lf, HIGHEST) + carry_ref[...]
        cum_ref[...] = cum
        carry_ref[...] = cum[ts - 1:ts, :]
        z = _dot_nn(psv, w_ref[...], HIGHEST) + b2_ref[...]
        la_ref[...] = _log_sigmoid(z) * (1.0 / GLA_TEMP)

    return pl.pallas_call(
        body, grid=(s // ts,),
        in_specs=[_row_spec(ts, SMALL_W), _vec_spec(SMALL_W),
                  pl.BlockSpec((SMALL_W, GLA_KW), lambda i: (0, 0)), _vec_spec(GLA_KW)],
        out_specs=[_row_spec(ts, SMALL_W), _row_spec(ts, GLA_KW)],
        out_shape=[jax.ShapeDtypeStruct((s, SMALL_W), F32), jax.ShapeDtypeStruct((s, GLA_KW), F32)],
        scratch_shapes=[pltpu.VMEM((1, SMALL_W), F32)],
        compiler_params=_cp("arbitrary"), name=name,
    )(ps, bf, w2p, b2)


def _gates_bwd(dck, ps, bf, w2p, b2, dla, *, name):
    s = ps.shape[0]
    ts = min(GATE_TS, s)
    nb = s // ts
    rev = lambda i: (nb - 1 - i, 0)

    def body(dck_ref, ps_ref, bf_ref, w_ref, b2_ref, dla_ref, dps_ref, dbf_ref, dw_ref, db2_ref, carry_ref):
        @pl.when(pl.program_id(0) == 0)
        def _():
            carry_ref[...] = jnp.zeros_like(carry_ref)
            dbf_ref[...] = jnp.zeros_like(dbf_ref)
            dw_ref[...] = jnp.zeros_like(dw_ref)
            db2_ref[...] = jnp.zeros_like(db2_ref)

        psv, dckv = ps_ref[...], dck_ref[...]
        dlf = _dot_nn(_tri(ts, upper=True), dckv, HIGHEST) + carry_ref[...]
        carry_ref[...] += jnp.sum(dckv, axis=0, keepdims=True)
        lane = lax.broadcasted_iota(jnp.int32, (ts, SMALL_W), 1)
        dff = jnp.where(lane < FOX_HEADS, dlf * _sigmoid(-(psv + bf_ref[...])), 0.0)
        z = _dot_nn(psv, w_ref[...], HIGHEST) + b2_ref[...]
        dz = dla_ref[...] * _sigmoid(-z) * (1.0 / GLA_TEMP)
        dps_ref[...] = (_dot_nt(dz, w_ref[...], HIGHEST) + dff).astype(BF16)
        dbf_ref[...] += jnp.sum(dff, axis=0, keepdims=True)
        dw_ref[...] += _dot_tn(psv, dz, HIGHEST)
        db2_ref[...] += jnp.sum(dz, axis=0, keepdims=True)

    return pl.pallas_call(
        body, grid=(nb,),
        in_specs=[pl.BlockSpec((ts, SMALL_W), rev), pl.BlockSpec((ts, SMALL_W), rev), _vec_spec(SMALL_W),
                  pl.BlockSpec((SMALL_W, GLA_KW), lambda i: (0, 0)), _vec_spec(GLA_KW),
                  pl.BlockSpec((ts, GLA_KW), rev)],
        out_specs=[pl.BlockSpec((ts, SMALL_W), rev), _vec_spec(SMALL_W),
                   pl.BlockSpec((SMALL_W, GLA_KW), lambda i: (0, 0)), _vec_spec(GLA_KW)],
        out_shape=[jax.ShapeDtypeStruct((s, SMALL_W), BF16), jax.ShapeDtypeStruct((1, SMALL_W), F32),
                   jax.ShapeDtypeStruct((SMALL_W, GLA_KW), F32), jax.ShapeDtypeStruct((1, GLA_KW), F32)],
        scratch_shapes=[pltpu.VMEM((1, SMALL_W), F32)],
        compiler_params=_cp("arbitrary"), name=name,
    )(dck, ps, bf, w2p, b2, dla)


def _hs(h, hd=FOX_HD):
    return slice(h * hd, (h + 1) * hd)


def _fox_fwd(proj, cum_t, g_fox, *, name, side=None):
    s = proj.shape[0]
    tq, tk = min(FOX_TQ, s), min(FOX_TK, s)
    scale = FOX_HD ** -0.5
    n_si = len(side.inputs) if side else 0
    n_so = len(side.out_shapes) if side else 0
    grid = (s // tq, s // tk)

    def body(*refs):
        q_ref, k_ref, v_ref, ck_ref, g_ref = refs[:5]
        o_ref, n_ref, lse_ref = refs[5 + n_si:8 + n_si]
        m_sc, acc_sc = refs[8 + n_si + n_so:10 + n_si + n_so]
        i, j = pl.program_id(0), pl.program_id(1)
        if side:
            side.run(refs[5:5 + n_si], refs[8 + n_si:8 + n_si + n_so], refs[10 + n_si + n_so:],
                     (i == 0) & (j == 0), (i == grid[0] - 1) & (j == grid[1] - 1))

        @pl.when(j == 0)
        def _():
            m_sc[...] = jnp.full_like(m_sc, NEG)
            acc_sc[...] = jnp.zeros_like(acc_sc)

        def block(masked):
            mask = _causal_mask(i, j, tq, tk) if masked else None
            ones = jnp.ones((tk, FOX_HD), BF16)
            for h in range(FOX_HEADS):
                sc = _fox_logits(_dot_nt(q_ref[:, _hs(h)], k_ref[:, _hs(h)]), ck_ref[h:h + 1, :], mask, scale)
                m_prev = m_sc[h]
                m_new = jnp.maximum(m_prev, jnp.max(sc, axis=-1, keepdims=True))
                alpha = jnp.exp(m_prev - m_new)
                p = jnp.exp(sc - m_new).astype(BF16)
                v_one = jnp.concatenate([v_ref[:, _hs(h)], ones], axis=1)
                acc_sc[:, _hs(h, 2 * FOX_HD)] = alpha * acc_sc[:, _hs(h, 2 * FOX_HD)] + _dot_nn(p, v_one)
                m_sc[h] = m_new

        pl.when(j < i)(functools.partial(block, False))

        @pl.when(j == i)
        def _():
            block(True)
            lane = lax.broadcasted_iota(jnp.int32, (tq, 128), 1)
            lse = jnp.zeros((tq, 128), F32)
            for h in range(FOX_HEADS):
                l_rep = acc_sc[:, 2 * h * FOX_HD + FOX_HD:2 * (h + 1) * FOX_HD]
                o = acc_sc[:, 2 * h * FOX_HD:2 * h * FOX_HD + FOX_HD] / l_rep
                o_ref[:, _hs(h)] = o
                r = lax.rsqrt(jnp.mean(o * o, axis=-1, keepdims=True) + EPS)
                n_ref[:, _hs(h)] = (o * r * g_ref[h:h + 1, :]).astype(BF16)
                lse = jnp.where(lane == h, m_sc[h] + jnp.log(l_rep), lse)
            lse_ref[...] = lse

    kv = lambda col: (lambda i, j: (jnp.minimum(j, i), col))
    any_spec = pl.BlockSpec(memory_space=pl.ANY)
    return pl.pallas_call(
        body, grid=grid,
        in_specs=[pl.BlockSpec((tq, FOX_W), lambda i, j: (i, 0)),
                  pl.BlockSpec((tk, FOX_W), kv(1)),
                  pl.BlockSpec((tk, FOX_W), kv(2)),
                  pl.BlockSpec((FOX_HEADS, tk), lambda i, j: (0, jnp.minimum(j, i))),
                  pl.BlockSpec((FOX_HEADS, FOX_HD), lambda i, j: (0, 0))] + [any_spec] * n_si,
        out_specs=[pl.BlockSpec((tq, FOX_W), lambda i, j: (i, 0)),
                   pl.BlockSpec((tq, FOX_W), lambda i, j: (i, 0)),
                   pl.BlockSpec((tq, 128), lambda i, j: (i, 0))] + [any_spec] * n_so,
        out_shape=[jax.ShapeDtypeStruct((s, FOX_W), F32), jax.ShapeDtypeStruct((s, FOX_W), BF16),
                   jax.ShapeDtypeStruct((s, 128), F32)] + (side.out_shapes if side else []),
        scratch_shapes=[pltpu.VMEM((FOX_HEADS, tq, 1), F32), pltpu.VMEM((tq, 2 * FOX_W), F32)]
        + (side.scratch() if side else []),
        compiler_params=_cp("arbitrary", "arbitrary"), name=name,
    )(proj, proj, proj, cum_t, g_fox, *(side.inputs if side else []))


def _causal_mask(i, j, tq, tk):
    rows = i * tq + lax.broadcasted_iota(jnp.int32, (tq, tk), 0)
    cols = j * tk + lax.broadcasted_iota(jnp.int32, (tq, tk), 1)
    return rows >= cols


def _fox_logits(qk, ck, mask, scale):
    sc = qk * scale - ck
    return sc if mask is None else jnp.where(mask, sc, NEG)


def _fox_bwd(proj, do, cum_t, lse, delta, *, name, side=None):
    s = proj.shape[0]
    tq, tk = min(FOX_TQ, s), min(FOX_TK, s)
    nk, nq = s // tk, s // tq
    scale = FOX_HD ** -0.5
    n_si = len(side.inputs) if side else 0
    n_so = len(side.out_shapes) if side else 0

    def body(*refs):
        q_ref, k_ref, v_ref, do_ref, ck_ref, lse_ref, dl_ref = refs[:7]
        dq_hbm, dk_ref, dv_ref, dcq_hbm, dck_ref = refs[7 + n_si:12 + n_si]
        dq_sc, dcq_sc, dk_sc, dv_sc, dck_sc, out_sems = refs[12 + n_si + n_so:18 + n_si + n_so]
        j, i = pl.program_id(0), pl.program_id(1)
        if side:
            side.run(refs[7:7 + n_si], refs[12 + n_si:12 + n_si + n_so], refs[18 + n_si + n_so:],
                     (j == 0) & (i == 0), (j == nk - 1) & (i == nq - 1))

        @pl.when((j == 0) & (i == 0))
        def _():
            dq_sc[...] = jnp.zeros_like(dq_sc)
            dcq_sc[...] = jnp.zeros_like(dcq_sc)

        @pl.when(i == 0)
        def _():
            dk_sc[...] = jnp.zeros_like(dk_sc)
            dv_sc[...] = jnp.zeros_like(dv_sc)
            dck_sc[...] = jnp.zeros_like(dck_sc)

        def block(masked):
            mask = _causal_mask(i, j, tq, tk) if masked else None
            qrows = pl.ds(pl.multiple_of(i * tq, tq), tq)
            for h in range(FOX_HEADS):
                sc = _fox_logits(_dot_nt(q_ref[:, _hs(h)], k_ref[:, _hs(h)]), ck_ref[h:h + 1, :], mask, scale)
                p = jnp.exp(sc - lse_ref[:, h:h + 1])
                ds = p * (_dot_nt(do_ref[:, _hs(h)], v_ref[:, _hs(h)]) - dl_ref[:, h:h + 1])
                dsb = ds.astype(BF16)
                dv_sc[:, _hs(h)] += _dot_tn(p.astype(BF16), do_ref[:, _hs(h)])
                dk_sc[:, _hs(h)] += _dot_tn(dsb, q_ref[:, _hs(h)])
                dq_sc[qrows, _hs(h)] += _dot_nn(dsb, k_ref[:, _hs(h)]) * scale
                dck_sc[h:h + 1, :] -= jnp.sum(ds, axis=0, keepdims=True)
                dcq_sc[qrows, h:h + 1] += jnp.sum(ds, axis=-1, keepdims=True)

        pl.when(i > j)(functools.partial(block, False))
        pl.when(i == j)(functools.partial(block, True))

        @pl.when(i == nq - 1)
        def _():
            dk_ref[...] = (dk_sc[...] * scale).astype(BF16)
            dv_ref[...] = dv_sc[...].astype(BF16)
            dck_ref[...] = dck_sc[...]

        @pl.when((j == nk - 1) & (i == nq - 1))
        def _():
            out_q = pltpu.make_async_copy(dq_sc, dq_hbm, out_sems.at[0])
            out_c = pltpu.make_async_copy(dcq_sc, dcq_hbm, out_sems.at[1])
            out_q.start()
            out_c.start()
            out_q.wait()
            out_c.wait()

    qrow = lambda j, i: (jnp.maximum(i, j), 0)
    krow = lambda col: (lambda j, i: (j, col))
    any_spec = pl.BlockSpec(memory_space=pl.ANY)
    return pl.pallas_call(
        body, grid=(nk, nq),
        in_specs=[pl.BlockSpec((tq, FOX_W), qrow), pl.BlockSpec((tk, FOX_W), krow(1)),
                  pl.BlockSpec((tk, FOX_W), krow(2)),
                  pl.BlockSpec((tq, FOX_W), qrow),
                  pl.BlockSpec((FOX_HEADS, tk), lambda j, i: (0, j)),
                  pl.BlockSpec((tq, 128), qrow), pl.BlockSpec((tq, 128), qrow)] + [any_spec] * n_si,
        out_specs=[any_spec, pl.BlockSpec((tk, FOX_W), lambda j, i: (j, 0)),
                   pl.BlockSpec((tk, FOX_W), lambda j, i: (j, 0)), any_spec,
                   pl.BlockSpec((FOX_HEADS, tk), lambda j, i: (0, j))] + [any_spec] * n_so,
        out_shape=[jax.ShapeDtypeStruct((s, FOX_W), F32), jax.ShapeDtypeStruct((s, FOX_W), BF16),
                   jax.ShapeDtypeStruct((s, FOX_W), BF16), jax.ShapeDtypeStruct((s, 128), F32),
                   jax.ShapeDtypeStruct((FOX_HEADS, s), F32)] + (side.out_shapes if side else []),
        scratch_shapes=[pltpu.VMEM((s, FOX_W), F32), pltpu.VMEM((s, 128), F32),
                        pltpu.VMEM((tk, FOX_W), F32), pltpu.VMEM((tk, FOX_W), F32), pltpu.VMEM((FOX_HEADS, tk), F32),
                        pltpu.SemaphoreType.DMA((2,))] + (side.scratch() if side else []),
        compiler_params=_cp("arbitrary", "arbitrary"), name=name,
    )(proj, proj, proj, do, cum_t, lse, delta, *(side.inputs if side else []))


def _head_norm_bwd(dn_in, o, g, gr_src, *, nh, hd, dn_col, gr_col, name):
    s, w = o.shape
    ts = min(ROW_TILE, s)
    gated = gr_src is not None

    def body(*refs):
        if gated:
            dn_ref, o_ref, g_ref, gr_ref, do_ref, dgr_ref, dl_ref, dg_ref = refs
        else:
            dn_ref, o_ref, g_ref, do_ref, dl_ref, dg_ref = refs

        @pl.when(pl.program_id(0) == 0)
        def _():
            dg_ref[...] = jnp.zeros_like(dg_ref)

        lane = lax.broadcasted_iota(jnp.int32, (ts, 128), 1)
        delta = jnp.zeros((ts, 128), F32)
        for h in range(nh):
            sl = _hs(h, hd)
            ov = o_ref[:, sl]
            dnv = dn_ref[:, sl].astype(F32)
            gv = g_ref[h:h + 1, :]
            r = lax.rsqrt(jnp.mean(ov * ov, axis=-1, keepdims=True) + EPS)
            ohat = ov * r
            if gated:
                grv = gr_ref[:, sl].astype(F32)
                sig = _sigmoid(grv)
                dgr_ref[:, sl] = (dnv * (ohat * gv) * (sig * (1.0 + grv * (1.0 - sig)))).astype(BF16)
                dnv = dnv * (grv * sig)
            dg_ref[h:h + 1, :] += jnp.sum(dnv * ohat, axis=0, keepdims=True)
            dohat = dnv * gv
            do = r * (dohat - ohat * jnp.mean(dohat * ohat, axis=-1, keepdims=True))
            do_ref[:, sl] = do.astype(BF16)
            delta = jnp.where(lane == h, jnp.sum(do.astype(BF16).astype(F32) * ov, axis=-1, keepdims=True), delta)
        dl_ref[...] = delta

    in_specs = [pl.BlockSpec((ts, w), lambda i: (i, dn_col)), _row_spec(ts, w),
                pl.BlockSpec((nh, hd), lambda i: (0, 0))]
    args = [dn_in, o, g]
    out_specs = [_row_spec(ts, w)]
    out_shape = [jax.ShapeDtypeStruct((s, w), BF16)]
    if gated:
        in_specs.append(pl.BlockSpec((ts, w), lambda i: (i, gr_col)))
        args.append(gr_src)
        out_specs.append(_row_spec(ts, w))
        out_shape.append(jax.ShapeDtypeStruct((s, w), BF16))
    out_specs += [_row_spec(ts, 128), pl.BlockSpec((nh, hd), lambda i: (0, 0))]
    out_shape += [jax.ShapeDtypeStruct((s, 128), F32), jax.ShapeDtypeStruct((nh, hd), F32)]
    return pl.pallas_call(
        body, grid=(s // ts,), in_specs=in_specs, out_specs=out_specs, out_shape=out_shape,
        compiler_params=_cp("arbitrary"), name=name,
    )(*args)


GQ_BLK = 3 * FOX_W // GLA_DK
GK_BLK = GQ_BLK + GLA_HEADS
GV_BLK = (3 * FOX_W + 2 * GLA_KW) // GLA_DV
GR_BLK = GV_BLK + GLA_HEADS


def _gla_chunk_terms(la):
    cum = _dot_nn(_tri(CHUNK), la, HIGHEST)
    total = cum[CHUNK - 1:CHUNK, :]
    return jnp.exp(total - cum), jnp.exp(total)


def _gla_fwd(proj, log_a, g_gla, *, name):
    s = proj.shape[0]
    rows = min(GLA_ROWS, s)
    cb = rows // CHUNK
    nblk = s // rows
    scale = GLA_DK ** -0.5

    def body(q_ref, k_ref, v_ref, gr_ref, la_ref, g_ref, o_ref, n_ref, st_ref, st_sc):
        h = pl.program_id(0)

        @pl.when(pl.program_id(1) == 0)
        def _():
            st_sc[...] = jnp.zeros_like(st_sc)

        gv = g_ref[pl.ds(h, 1), :]
        for ci in range(cb):
            sl = slice(ci * CHUNK, (ci + 1) * CHUNK)
            e, dec = _gla_chunk_terms(la_ref[sl, :])
            k_dec = (k_ref[sl, :].astype(F32) * e).astype(BF16)
            st = st_sc[...] * dec + _dot_tn(v_ref[sl, :], k_dec)
            st_sc[...] = st
            st_ref[0, ci] = st
            qs = (q_ref[sl, :].astype(F32) * scale).astype(BF16)
            o = _dot_nt(qs, st.astype(BF16))
            o_ref[sl, :] = o
            r = lax.rsqrt(jnp.mean(o * o, axis=-1, keepdims=True) + EPS)
            grv = gr_ref[sl, :].astype(F32)
            n_ref[sl, :] = (o * r * gv * (grv * _sigmoid(grv))).astype(BF16)

    return pl.pallas_call(
        body, grid=(GLA_HEADS, nblk),
        in_specs=[pl.BlockSpec((rows, GLA_DK), lambda h, n: (n, GQ_BLK + h)),
                  pl.BlockSpec((rows, GLA_DK), lambda h, n: (n, GK_BLK + h)),
                  pl.BlockSpec((rows, GLA_DV), lambda h, n: (n, GV_BLK + h)),
                  pl.BlockSpec((rows, GLA_DV), lambda h, n: (n, GR_BLK + h)),
                  pl.BlockSpec((rows, GLA_DK), lambda h, n: (n, h)),
                  pl.BlockSpec((GLA_HEADS, GLA_DV), lambda h, n: (0, 0))],
        out_specs=[pl.BlockSpec((rows, GLA_DV), lambda h, n: (n, h)),
                   pl.BlockSpec((rows, GLA_DV), lambda h, n: (n, h)),
                   pl.BlockSpec((1, cb, GLA_DV, GLA_DK), lambda h, n: (h, n, 0, 0))],
        out_shape=[jax.ShapeDtypeStruct((s, GLA_W), F32), jax.ShapeDtypeStruct((s, GLA_W), BF16),
                   jax.ShapeDtypeStruct((GLA_HEADS, s // CHUNK, GLA_DV, GLA_DK), F32)],
        scratch_shapes=[pltpu.VMEM((GLA_DV, GLA_DK), F32)],
        compiler_params=_cp("parallel", "arbitrary"), name=name,
    )(proj, proj, proj, proj, log_a, g_gla)


def _gla_bwd(proj, log_a, do, states, *, name):
    s = proj.shape[0]
    rows = min(GLA_ROWS, s)
    cb = rows // CHUNK
    nblk = s // rows
    scale = GLA_DK ** -0.5

    def body(q_ref, k_ref, v_ref, la_ref, do_ref, st_ref, prev_ref, dq_ref, dk_ref, dv_ref, dla_ref, g_sc):
        nrev = pl.program_id(1)
        blk = nblk - 1 - nrev

        @pl.when(nrev == 0)
        def _():
            g_sc[...] = jnp.zeros_like(g_sc)

        for ci in reversed(range(cb)):
            sl = slice(ci * CHUNK, (ci + 1) * CHUNK)
            e, dec = _gla_chunk_terms(la_ref[sl, :])
            kd = k_ref[sl, :].astype(F32) * e
            qs = (q_ref[sl, :].astype(F32) * scale).astype(BF16)
            dov = do_ref[sl, :]
            st = st_ref[0, ci]
            if ci > 0:
                st_prev = st_ref[0, ci - 1]
            else:
                st_prev = prev_ref[0, 0] * (blk > 0).astype(F32)
            dq_ref[sl, :] = (_dot_nn(dov, st.astype(BF16)) * scale).astype(BF16)
            gt = g_sc[...] + _dot_tn(dov, qs)
            gtb = gt.astype(BF16)
            dkd = _dot_nn(v_ref[sl, :], gtb)
            dv_ref[sl, :] = _dot_nt(kd.astype(BF16), gtb).astype(BF16)
            dk_ref[sl, :] = (dkd * e).astype(BF16)
            ddec = jnp.sum(gt * st_prev, axis=0, keepdims=True) * dec
            dla_ref[sl, :] = _dot_nn(_tri(CHUNK, strict=True), dkd * kd, HIGHEST) + ddec
            g_sc[...] = gt * dec

    rev = lambda col0: (lambda h, n: (nblk - 1 - n, col0 + h))
    return pl.pallas_call(
        body, grid=(GLA_HEADS, nblk),
        in_specs=[pl.BlockSpec((rows, GLA_DK), rev(GQ_BLK)),
                  pl.BlockSpec((rows, GLA_DK), rev(GK_BLK)),
                  pl.BlockSpec((rows, GLA_DV), rev(GV_BLK)),
                  pl.BlockSpec((rows, GLA_DK), rev(0)),
                  pl.BlockSpec((rows, GLA_DV), rev(0)),
                  pl.BlockSpec((1, cb, GLA_DV, GLA_DK), lambda h, n: (h, nblk - 1 - n, 0, 0)),
                  pl.BlockSpec((1, 1, GLA_DV, GLA_DK),
                               lambda h, n: (h, jnp.maximum((nblk - 1 - n) * cb - 1, 0), 0, 0))],
        out_specs=[pl.BlockSpec((rows, GLA_DK), rev(0)), pl.BlockSpec((rows, GLA_DK), rev(0)),
                   pl.BlockSpec((rows, GLA_DV), rev(0)), pl.BlockSpec((rows, GLA_DK), rev(0))],
        out_shape=[jax.ShapeDtypeStruct((s, GLA_KW), BF16), jax.ShapeDtypeStruct((s, GLA_KW), BF16),
                   jax.ShapeDtypeStruct((s, GLA_W), BF16), jax.ShapeDtypeStruct((s, GLA_KW), F32)],
        scratch_shapes=[pltpu.VMEM((GLA_DV, GLA_DK), F32)],
        compiler_params=_cp("parallel", "arbitrary"), name=name,
    )(proj, proj, proj, log_a, do, states, states)


def _row_tile(r):
    tr = min(ROW_TILE, r)
    while r % tr or tr % 8:
        tr -= 1
    return tr


def _adamw_math(w, g, m, v):
    m = ADAM_B1 * m + (1.0 - ADAM_B1) * g
    v = ADAM_B2 * v + (1.0 - ADAM_B2) * (g * g)
    m_hat = m / (1.0 - ADAM_B1 ** ADAM_STEP)
    v_hat = v / (1.0 - ADAM_B2 ** ADAM_STEP)
    delta = -ADAM_LR * (m_hat / (jnp.sqrt(v_hat) + ADAM_EPS) + ADAM_WD * w)
    return delta, m, v


COL_TILE = 256


def _tile_2d(r, c):
    if r % 8 == 0 and _row_tile(r) >= 64:
        return _row_tile(r), c
    assert c % COL_TILE == 0, (r, c)
    return r, COL_TILE


def _half_shape(shape):
    r, c = shape[-2:]
    return tuple(shape[:-2]) + ((r // 2, c) if _half_axis(r) == 0 else (r, c // 2))


def _adam(g, w, m, v, *, name):
    r, c = w.shape
    tr, tc = _tile_2d(r, c)

    def body(g_ref, w_ref, m_ref, v_ref, d_ref, mo_ref, vo_ref):
        d, mn, vn = _adamw_math(w_ref[...], g_ref[...], m_ref[...], v_ref[...])
        d_ref[...] = d
        mo_ref[...] = mn
        vo_ref[...] = vn

    spec = pl.BlockSpec((tr, tc), lambda i, j: (i, j))
    return pl.pallas_call(
        body, grid=(r // tr, c // tc), in_specs=[spec] * 4, out_specs=[spec] * 3,
        out_shape=[jax.ShapeDtypeStruct((r, c), F32)] * 3,
        compiler_params=_cp("parallel", "parallel"), name=name,
    )(g, w, m, v)


def _ada_grad_adam(c_all_t, dmod_cols, w, m, v, *, name):
    r, c = w.shape
    tr, tc = min(512, r), min(1024, c)

    def body(ct_ref, dm_ref, w_ref, m_ref, v_ref, g_ref, d_ref, mo_ref, vo_ref):
        g = _dot_nn(ct_ref[...], dm_ref[...], HIGHEST)
        g_ref[...] = g
        d, mn, vn = _adamw_math(w_ref[...], g, m_ref[...], v_ref[...])
        d_ref[...] = d
        mo_ref[...] = mn
        vo_ref[...] = vn

    spec = pl.BlockSpec((tr, tc), lambda i, j: (i, j))
    nb = c_all_t.shape[1]
    return pl.pallas_call(
        body, grid=(r // tr, c // tc),
        in_specs=[pl.BlockSpec((tr, nb), lambda i, j: (i, 0)), pl.BlockSpec((nb, tc), lambda i, j: (0, j)),
                  spec, spec, spec],
        out_specs=[spec] * 4, out_shape=[jax.ShapeDtypeStruct((r, c), F32)] * 4,
        compiler_params=_cp("parallel", "parallel"), name=name,
    )(c_all_t, dmod_cols, w, m, v)


def _mod_shard(c_all, w, b, *, name):
    k, c = w.shape
    tc = min(512, c)
    nb = c_all.shape[0]

    def body(c_ref, w_ref, b_ref, o_ref):
        o_ref[...] = _dot_nn(c_ref[...], w_ref[...], HIGHEST) + b_ref[...]

    return pl.pallas_call(
        body, grid=(c // tc,),
        in_specs=[pl.BlockSpec((nb, k), lambda j: (0, 0)), pl.BlockSpec((k, tc), lambda j: (0, j)),
                  pl.BlockSpec((1, tc), lambda j: (0, j))],
        out_specs=pl.BlockSpec((nb, tc), lambda j: (0, j)),
        out_shape=jax.ShapeDtypeStruct((nb, c), F32),
        compiler_params=_cp("parallel"), name=name,
    )(c_all, w, b)


def _silu_rows(c, *, name):
    def body(c_ref, o_ref):
        cv = c_ref[...]
        o_ref[...] = cv * _sigmoid(cv)

    return pl.pallas_call(body, out_shape=jax.ShapeDtypeStruct(c.shape, F32), name=name)(c)


def _pair_sum(g, got, idx, *, name):
    p, r, c = g.shape
    ax = _half_axis(r)
    hr, hc = _half_shape((r, c))
    tr, tc = _tile_2d(hr, hc)
    nbr, nbc = hr // tr, hc // tc

    def body(idx_ref, a_ref, b_ref, o_ref):
        o_ref[...] = (a_ref[...].astype(F32) + b_ref[...].astype(F32)).astype(BF16)

    def slot(i, idx_ref):
        return i + jnp.where(i >= idx_ref[1], 1, 0)

    def own_map(i, j, k, idx_ref):
        return (slot(i, idx_ref), j + (idx_ref[0] * nbr if ax == 0 else 0), k + (idx_ref[0] * nbc if ax == 1 else 0))

    half_spec = pl.BlockSpec((1, tr, tc), lambda i, j, k, idx_ref: (slot(i, idx_ref), j, k))
    return pl.pallas_call(
        body,
        grid_spec=pltpu.PrefetchScalarGridSpec(
            num_scalar_prefetch=1, grid=(p - 1, nbr, nbc),
            in_specs=[pl.BlockSpec((1, tr, tc), own_map), half_spec],
            out_specs=half_spec),
        out_shape=jax.ShapeDtypeStruct((p, hr, hc), BF16),
        compiler_params=_cp("parallel", "parallel", "parallel"), name=name,
    )(idx, g, got)


def _final_sum(g, got, parts, idx, *, name):
    shard_shape = g.shape[1:]
    ax = _half_axis(shard_shape[0])
    hr, hc = got.shape[1:]
    tr, tc = _tile_2d(hr, hc)
    nbr, nbc = hr // tr, hc // tc

    def body(idx_ref, g_ref, got_ref, parts_ref, o_ref):
        acc = g_ref[0].astype(F32) + got_ref[0].astype(F32)
        for q in range(3):
            acc = acc + parts_ref[q].astype(F32)
        o_ref[...] = acc

    def half_c(j, k, idx_ref):
        return (j + (idx_ref[0] * nbr if ax == 0 else 0), k + (idx_ref[0] * nbc if ax == 1 else 0))

    return pl.pallas_call(
        body,
        grid_spec=pltpu.PrefetchScalarGridSpec(
            num_scalar_prefetch=1, grid=(nbr, nbc),
            in_specs=[pl.BlockSpec((1, tr, tc), lambda j, k, idx_ref: (idx_ref[1],) + half_c(j, k, idx_ref)),
                      pl.BlockSpec((1, tr, tc), lambda j, k, idx_ref: (idx_ref[1], j, k)),
                      pl.BlockSpec((3, tr, tc), lambda j, k, idx_ref: (0, j, k))],
            out_specs=pl.BlockSpec((tr, tc), half_c)),
        out_shape=jax.ShapeDtypeStruct(tuple(shard_shape), F32),
        compiler_params=_cp("parallel", "parallel"), name=name,
    )(idx, g, got, parts)


def _stack_sum(x, *, name):
    p, r, c = x.shape
    tr = _row_tile(r)

    def body(x_ref, o_ref):
        acc = x_ref[0].astype(F32)
        for q in range(1, p):
            acc = acc + x_ref[q].astype(F32)
        o_ref[...] = acc

    return pl.pallas_call(
        body, grid=(r // tr,),
        in_specs=[pl.BlockSpec((p, tr, c), lambda i: (0, i, 0))],
        out_specs=pl.BlockSpec((tr, c), lambda i: (i, 0)),
        out_shape=jax.ShapeDtypeStruct((r, c), F32),
        compiler_params=_cp("parallel"), name=name,
    )(x)


def _place():
    x, y, c = lax.axis_index("x"), lax.axis_index("y"), lax.axis_index("c")
    chips = [(1 - x, y), (x, 1 - y), (1 - x, 1 - y)]
    return x, y, c, chips


def _gather8(x_shard, *, name):
    m_per, n = x_shard.shape

    def body(x_ref, out_ref, send_sems, recv_sems, local_sem):
        x, y, c, chips = _place()
        me, sibling = (x, y, c), (x, y, 1 - c)

        def rows(px, py, pc):
            return out_ref.at[pl.ds((4 * px + 2 * py + pc) * m_per, m_per), :]

        def copy(k, block, to, src=None):
            return pltpu.make_async_remote_copy(
                src_ref=rows(*block) if src is None else src, dst_ref=rows(*block),
                send_sem=send_sems.at[k], recv_sem=recv_sems.at[k], device_id=to, device_id_type=MESH)

        mine = pltpu.make_async_copy(x_ref, rows(*me), local_sem)
        mine.start()
        first = [copy(0, me, sibling, src=x_ref)]
        first += [copy(1 + j, me, (*chip, c), src=x_ref) for j, chip in enumerate(chips)]
        for cp in first:
            cp.start()
        passed = [copy(4 + j, (*chip, c), sibling) for j, chip in enumerate(chips)]
        for j, chip in enumerate(chips):
            copy(1 + j, (*chip, c), me).wait_recv()
            passed[j].start()
        copy(0, sibling, me).wait_recv()
        for j, chip in enumerate(chips):
            copy(4 + j, (*chip, 1 - c), me).wait_recv()
        for cp in first + passed:
            cp.wait_send()
        mine.wait()

    return pl.pallas_call(
        body,
        out_shape=jax.ShapeDtypeStruct((8 * m_per, n), x_shard.dtype),
        in_specs=[pl.BlockSpec(memory_space=pltpu.VMEM)],
        out_specs=pl.BlockSpec(memory_space=pltpu.VMEM),
        scratch_shapes=[pltpu.SemaphoreType.DMA((7,)), pltpu.SemaphoreType.DMA((7,)), pltpu.SemaphoreType.DMA],
        name=name,
    )(x_shard)


def _gather_relayed(shard, *, name):
    def body(shard_ref, full_ref, send_sems, recv_sems):
        x, y, c, _ = _place()
        ax = _half_axis(shard_ref.shape[0])
        me, xn, yn, dg = 2 * x + y, 2 * (1 - x) + y, 2 * x + (1 - y), 2 * (1 - x) + (1 - y)
        to_x, to_y = (1 - x, y, c), (x, 1 - y, c)

        def half(slot, piece=None):
            return _rows_half(full_ref.at[slot], c, ax, piece)

        def copy(k, src, dst, peer):
            return pltpu.make_async_remote_copy(src_ref=src, dst_ref=dst, send_sem=send_sems.at[k],
                                                recv_sem=recv_sems.at[k], device_id=peer, device_id_type=MESH)

        mine = _rows_half(shard_ref, c, ax)
        sends = [copy(0, mine, half(me), to_x), copy(1, mine, half(me), to_y)]
        for cp in sends:
            cp.start()
        copy(0, mine, half(xn), to_x).wait_recv()
        relay_y = copy(2, half(xn, (0, 2)), half(xn, (0, 2)), to_y)
        relay_y.start()
        copy(1, mine, half(yn), to_y).wait_recv()
        relay_x = copy(3, half(yn, (1, 2)), half(yn, (1, 2)), to_x)
        relay_x.start()
        copy(2, half(xn, (0, 2)), half(dg, (0, 2)), to_y).wait_recv()
        copy(3, half(yn, (1, 2)), half(dg, (1, 2)), to_x).wait_recv()
        for cp in sends + [relay_y, relay_x]:
            cp.wait_send()

    any_spec = pl.BlockSpec(memory_space=pl.ANY)
    return pl.pallas_call(
        body, out_shape=jax.ShapeDtypeStruct((4,) + shard.shape, shard.dtype),
        in_specs=[any_spec], out_specs=any_spec,
        scratch_shapes=[pltpu.SemaphoreType.DMA((4,)), pltpu.SemaphoreType.DMA((4,))], name=name,
    )(shard)


def _plan_start(plan, send_sems, recv_sems):
    for k, (src, dst, _, peer) in enumerate(plan):
        pltpu.make_async_remote_copy(src_ref=src, dst_ref=dst, send_sem=send_sems.at[k], recv_sem=recv_sems.at[k],
                                     device_id=peer, device_id_type=MESH).start()


def _plan_wait(plan, send_sems, recv_sems):
    for k, (src, _, land, peer) in enumerate(plan):
        pltpu.make_async_remote_copy(src_ref=src, dst_ref=land, send_sem=send_sems.at[k], recv_sem=recv_sems.at[k],
                                     device_id=peer, device_id_type=MESH).wait_recv()
    for k, (src, dst, _, peer) in enumerate(plan):
        pltpu.make_async_remote_copy(src_ref=src, dst_ref=dst, send_sem=send_sems.at[k], recv_sem=recv_sems.at[k],
                                     device_id=peer, device_id_type=MESH).wait_send()


def _half_axis(rows):
    return 0 if rows % 32 == 0 else 1


def _rows_half(ref, hc, axis, part=None):
    size = ref.shape[axis] // 2
    start = hc * size
    if part is not None:
        size //= part[1]
        start = start + part[0] * size
    idx = [slice(None)] * len(ref.shape)
    idx[axis] = pl.ds(start, size)
    return ref.at[tuple(idx)]


def _plan_gather_ici(shard, full, part=None):
    x, y, c, chips = _place()
    ax = _half_axis(shard.shape[0])
    src = _rows_half(shard, c, ax, part)
    return [(src, _rows_half(full.at[2 * x + y], c, ax, part), _rows_half(full.at[2 * cx + cy], c, ax, part),
             (cx, cy, c)) for cx, cy in chips]


def _plan_gather_d2d(full, own):
    x, y, c, chips = _place()
    ax = _half_axis(full.shape[1])
    plan = []
    for cx, cy in chips:
        slot = full.at[2 * cx + cy]
        plan.append((_rows_half(slot, c, ax), _rows_half(slot, c, ax), _rows_half(slot, 1 - c, ax), (x, y, 1 - c)))
    mine = full.at[2 * x + y]
    plan.append((own, mine, mine, (x, y, 1 - c)))
    return plan


def _plan_pair(grad, got):
    x, y, c, _ = _place()
    return [(_rows_half(grad, 1 - c, 1 + _half_axis(grad.shape[1])), got, got, (x, y, 1 - c))]


def _plan_shard_ici(sums, parts, piece=None):
    _, _, c, chips = _place()

    def rows(ref):
        if piece is None:
            return ref
        k, n = piece
        if ref.shape[0] % (16 * n) == 0:
            size = ref.shape[0] // n
            return ref.at[pl.ds(k * size, size), :]
        size = ref.shape[1] // n
        return ref.at[:, pl.ds(k * size, size)]

    return [(rows(sums.at[2 * cx + cy]), rows(parts.at[k]), rows(parts.at[k]), (cx, cy, c))
            for k, (cx, cy) in enumerate(chips)]


def _plan_half(buf):
    x, y, c, _ = _place()
    ax = _half_axis(buf.shape[0])
    mine = _rows_half(buf, c, ax)
    return [(mine, mine, _rows_half(buf, 1 - c, ax), (x, y, 1 - c))]


def _comm_call(plan_fn, inputs, out_shapes, *, name, aliases=None):
    ni, no = len(inputs), len(out_shapes)

    def body(*refs):
        plan = plan_fn(refs[:ni], refs[ni:ni + no])
        send_sems, recv_sems = refs[ni + no:]
        _plan_start(plan, send_sems, recv_sems)
        _plan_wait(plan, send_sems, recv_sems)

    any_spec = pl.BlockSpec(memory_space=pl.ANY)
    n_copies = 3 * max(ni, no)
    return pl.pallas_call(
        body, out_shape=list(out_shapes), in_specs=[any_spec] * ni, out_specs=[any_spec] * no,
        scratch_shapes=[pltpu.SemaphoreType.DMA((n_copies,)), pltpu.SemaphoreType.DMA((n_copies,))],
        input_output_aliases=aliases or {}, name=name,
    )(*inputs)


def _gather_forward(full, own, *, name):
    return _comm_call(lambda ins, outs: _plan_gather_d2d(outs[0], ins[1]), [full, own],
                      [jax.ShapeDtypeStruct(full.shape, full.dtype)], name=name, aliases={0: 0})[0]


def _half_exchange(bufs, *, name):
    return _comm_call(lambda ins, outs: [cp for o in outs for cp in _plan_half(o)],
                      bufs, [jax.ShapeDtypeStruct(b.shape, b.dtype) for b in bufs], name=name,
                      aliases={k: k for k in range(len(bufs))})


def _split_w_in(w_in_t):
    d = w_in_t.shape[1]
    main = jnp.concatenate([w_in_t[0:3072], w_in_t[3080:5128], w_in_t[5144:6168]], axis=0)
    small = jnp.concatenate([w_in_t[3072:3080], w_in_t[5128:5144], jnp.zeros((SMALL_W - 24, d), w_in_t.dtype)], axis=0)
    return main, small


def _merge_dw_in(dw_main, dw_small):
    return jnp.concatenate([dw_main[0:3072], dw_small[0:8], dw_main[3072:5120], dw_small[8:24], dw_main[5120:6144]],
                           axis=0)


def _gather_side(shards):
    return _Side(shards, [jax.ShapeDtypeStruct((4,) + w.shape, w.dtype) for w in shards],
                 lambda ins, outs: [cp for i, o in zip(ins, outs) for cp in _plan_gather_ici(i, o)], 3 * len(shards))


def _forward_side(full, own):
    return _Side([full, own], [jax.ShapeDtypeStruct(full.shape, full.dtype)],
                 lambda ins, outs: _plan_gather_d2d(outs[0], ins[1]), 4, aliases={0: 0})


def _half_side(bufs):
    return _Side(bufs, [jax.ShapeDtypeStruct(b.shape, b.dtype) for b in bufs],
                 lambda ins, outs: [cp for o in outs for cp in _plan_half(o)], len(bufs),
                 aliases={k: k for k in range(len(bufs))})


def _parts_shape(sums):
    return jax.ShapeDtypeStruct((3,) + sums.shape[1:], sums.dtype)


def _got_shape(grad):
    return jax.ShapeDtypeStruct(_half_shape(grad.shape), grad.dtype)


def _pair_side(grad):
    return _Side([grad], [_got_shape(grad)], lambda ins, outs: _plan_pair(ins[0], outs[0]), 1)


def _local_step(x, target, mod, g_pre_mix, g_post_mix, g_pre_mlp, g_post_mlp, gw_in, b_fgate, w_gla_a2,
                b_gla_a2, g_fox, g_gla, own_w_in, own_w_out, own_w_mlp_in, own_w_mlp_out, idx):
    s, d = x.shape
    shift_m, scale_m, gate_m, shift_f, scale_f, gate_f = [mod[:, i * d:(i + 1) * d] for i in range(6)]
    a1 = g_pre_mix * (1.0 + scale_m)
    a2 = g_pre_mlp * (1.0 + scale_f)
    bf = jnp.concatenate([b_fgate, jnp.zeros((1, SMALL_W - FOX_HEADS), F32)], axis=1)
    w2p = jnp.zeros((SMALL_W, GLA_KW), F32).at[FOX_HEADS:FOX_HEADS + GLA_RANK].set(w_gla_a2)

    h1, gw_in = _pre_fwd(x, a1, shift_m, name="pre_mix_fwd", side=_forward_side(gw_in, own_w_in))
    w_in_t = gw_in.reshape(-1, d)
    w_main, w_small = _split_w_in(w_in_t)
    full_shape = lambda w: jax.ShapeDtypeStruct((4,) + w.shape, w.dtype)
    first_side = _Side(
        [own_w_out, own_w_mlp_out], [full_shape(own_w_out), full_shape(own_w_mlp_out)],
        lambda ins, outs: _plan_gather_ici(ins[0], outs[0]) + _plan_gather_ici(ins[1], outs[1], part=(0, 4)), 6)
    proj, gw_out, gw_mlp_out = _mm(h1, w_main, mode="nt", out_dtypes=[BF16], name="in_proj_main", side=first_side)
    ps, gw_out = _mm(h1, w_small, mode="nt", out_dtypes=[F32], name="in_proj_small",
                     side=_forward_side(gw_out, own_w_out))
    w_out_full = gw_out.reshape(-1, d)
    cum, log_a = _gates_fwd(ps, bf, w2p, b_gla_a2, name="gates_fwd")
    cum_t = cum[:, :FOX_HEADS].T
    o_fox, fox_n, lse, gw_mlp_in = _fox_fwd(proj, cum_t, g_fox, name="fox_fwd", side=_gather_side([own_w_mlp_in]))
    o_gla, gla_n, states = _gla_fwd(proj, log_a, g_gla, name="gla_fwd")
    mixed = jnp.concatenate([fox_n, gla_n], axis=1)
    y1, gw_mlp_in = _mm(mixed, w_out_full, mode="nn", out_dtypes=[F32], name="out_proj",
                        side=_forward_side(gw_mlp_in, own_w_mlp_in))
    x1, h2 = _post_pre_fwd(x, y1, gate_m, g_post_mix, a2, shift_f, name="post_mix_pre_mlp_fwd")

    def mlp_act(acc):
        r = jnp.maximum(acc, 0.0)
        return acc, r * r

    rest_side = _Side([own_w_mlp_out, gw_mlp_out], [full_shape(own_w_mlp_out)],
                      lambda ins, outs: [cp for q in (1, 2, 3) for cp in _plan_gather_ici(ins[0], outs[0], part=(q, 4))],
                      9, aliases={1: 0})
    u, act, gw_mlp_out = _mm(h2, gw_mlp_in, mode="nn", out_dtypes=[BF16, BF16], epi=mlp_act, name="mlp_in",
                             b_slots=4, tm=MM_TM, side=rest_side)
    gw_mlp_out = _gather_forward(gw_mlp_out, own_w_mlp_out, name="gather_w_mlp_out_d2d")
    w_mlp_out_full = gw_mlp_out.reshape(-1, d)
    y2, = _mm(act, w_mlp_out_full, mode="nn", out_dtypes=[F32], name="mlp_out")
    dx2, dy2, loss_part, dgate_f, dg_post_mlp = _post_loss_bwd(x1, y2, gate_f, g_post_mlp, target,
                                                               name="post_mlp_loss_bwd")
    dw_mlp_out, = _mm(act, dy2, mode="tn", out_dtypes=[BF16], name="dw_mlp_out", tk=MM_TK_TOKENS)
    dw_mlp_out = dw_mlp_out.reshape(4, D_FF // 4, d)

    def act_bwd(acc, uv):
        return (acc * (2.0 * jnp.maximum(uv.astype(F32), 0.0)),)

    du, got_mlp_out = _mm(dy2, w_mlp_out_full, mode="nt", out_dtypes=[BF16], extras=[u], epi=act_bwd,
                          name="d_mlp_hidden", tm=MM_TM, side=_pair_side(dw_mlp_out))
    sum_mlp_out = _pair_sum(dw_mlp_out, got_mlp_out, idx, name="grad_pair_sum_mlp_out")
    nj = D_FF // 4 // min(MM_T, D_FF // 4)
    tmw = min(MM_T, d)
    dw_mlp_in, parts_mlp_out = _mm(
        h2, du, mode="tn", out_dtypes=[BF16], name="dw_mlp_in", tk=MM_TK_TOKENS,
        out_shapes=[jax.ShapeDtypeStruct((4, d, D_FF // 4), BF16)],
        out_specs=[pl.BlockSpec((1, tmw, min(MM_T, D_FF // 4)), lambda i, j, kk: (j // nj, i, j % nj))],
        side=_Side([sum_mlp_out], [_parts_shape(sum_mlp_out)],
                   lambda ins, outs: _plan_shard_ici(ins[0], outs[0], piece=(0, 2)), 3))
    dh2, got_mlp_in, parts_mlp_out = _mm(
        du, gw_mlp_in, mode="nt", out_dtypes=[F32], name="d_mlp_in", b_slots=4,
        side=_Side([dw_mlp_in, sum_mlp_out, parts_mlp_out], [_got_shape(dw_mlp_in), _parts_shape(sum_mlp_out)],
                   lambda ins, outs: _plan_pair(ins[0], outs[0]) + _plan_shard_ici(ins[1], outs[1], piece=(1, 2)),
                   4, aliases={2: 1}))
    sum_mlp_in = _pair_sum(dw_mlp_in, got_mlp_in, idx, name="grad_pair_sum_mlp_in")
    dx1, dshift_f, da2, dy1, dgate_m, dg_post_mix = _pre_post_bwd(dh2, x1, dx2, a2, y1, gate_m, g_post_mix,
                                                                  name="pre_mlp_post_mix_bwd")
    buf_mlp_out = _final_sum(dw_mlp_out, got_mlp_out, parts_mlp_out, idx, name="grad_final_sum_mlp_out")
    dw_out, g_mlp_out = _mm(mixed, dy1, mode="tn", out_dtypes=[BF16], name="dw_out", tk=MM_TK_TOKENS,
                            side=_half_side([buf_mlp_out]))
    dw_out = dw_out.reshape(4, d // 4, d)
    dmixed, got_out = _mm(dy1, w_out_full, mode="nt", out_dtypes=[BF16], name="d_mixed", side=_pair_side(dw_out))
    sum_out = _pair_sum(dw_out, got_out, idx, name="grad_pair_sum_out")
    do_fox, delta, dg_fox = _head_norm_bwd(dmixed, o_fox, g_fox, None, nh=FOX_HEADS, hd=FOX_HD, dn_col=0,
                                           gr_col=0, name="fox_norm_bwd")
    do_gla, dgr, _, dg_gla = _head_norm_bwd(dmixed, o_gla, g_gla, proj, nh=GLA_HEADS, hd=GLA_DV, dn_col=1,
                                            gr_col=(3 * FOX_W + 2 * GLA_KW + GLA_W) // GLA_W, name="gla_norm_bwd")
    dq_fox, dk_fox, dv_fox, dcq, dck_t, parts_mlp_in, parts_out = _fox_bwd(
        proj, do_fox, cum_t, lse, delta, name="fox_bwd",
        side=_Side([sum_mlp_in, sum_out], [_parts_shape(sum_mlp_in), _parts_shape(sum_out)],
                   lambda ins, outs: _plan_shard_ici(ins[0], outs[0]) + _plan_shard_ici(ins[1], outs[1]), 6))
    dgq, dgk, dgv, dla = _gla_bwd(proj, log_a, do_gla, states, name="gla_bwd")
    dck = dcq + jnp.concatenate([dck_t.T, jnp.zeros((s, SMALL_W - FOX_HEADS), F32)], axis=1)
    dps, dbf, dw2p, db2 = _gates_bwd(dck, ps, bf, w2p, b_gla_a2, dla, name="gates_bwd")
    dproj = jnp.concatenate([dq_fox.astype(BF16), dk_fox, dv_fox, dgq, dgk, dgv, dgr], axis=1)
    buf_mlp_in = _final_sum(dw_mlp_in, got_mlp_in, parts_mlp_in, idx, name="grad_final_sum_mlp_in")
    buf_out = _final_sum(dw_out, got_out, parts_out, idx, name="grad_final_sum_out")
    dw_main, g_mlp_in, g_out = _mm(dproj, h1, mode="tn", out_dtypes=[BF16], name="dw_in_main", tk=MM_TK_TOKENS,
                                   side=_half_side([buf_mlp_in, buf_out]))
    dw_small, = _mm(dps, h1, mode="tn", out_dtypes=[BF16], name="dw_in_small")
    rs_in = w_in_t.shape[0] // 4
    dw_in = _merge_dw_in(dw_main, dw_small).reshape(4, rs_in, d)
    dh1_small, got_in = _mm(dps, w_small, mode="nn", out_dtypes=[F32], name="d_h1_small", side=_pair_side(dw_in))
    sum_in = _pair_sum(dw_in, got_in, idx, name="grad_pair_sum_in")
    dh1, parts_in = _mm(
        dproj, w_main, mode="nn", out_dtypes=[F32], extras=[dh1_small], epi=lambda acc, e: (acc + e,), name="d_h1",
        tk=MAIN_W // 2,
        side=_Side([sum_in], [_parts_shape(sum_in)],
                   lambda ins, outs: [cp for q in range(3) for cp in _plan_shard_ici(ins[0], outs[0], piece=(q, 4))],
                   9))
    grad_x, dshift_m, da1, parts_in = _pre_bwd(
        dh1, x, dx1, a1, name="pre_mix_bwd",
        side=_Side([sum_in, parts_in], [_parts_shape(sum_in)],
                   lambda ins, outs: _plan_shard_ici(ins[0], outs[0], piece=(3, 4)), 3, aliases={1: 0}))
    buf_in = _final_sum(dw_in, got_in, parts_in, idx, name="grad_final_sum_in")
    g_in, = _half_exchange([buf_in], name="grad_half_exchange_in")
    g_big = [g_in, g_out, g_mlp_in, g_mlp_out]

    dmod = jnp.concatenate([dshift_m, da1 * g_pre_mix, dgate_m, dshift_f, da2 * g_pre_mlp, dgate_f], axis=1)
    small = dict(
        dmod=dmod, g_pre_mix=da1 * (1.0 + scale_m), g_post_mix=dg_post_mix, g_pre_mlp=da2 * (1.0 + scale_f),
        g_post_mlp=dg_post_mlp, b_fgate=dbf[:, :FOX_HEADS], w_gla_a2=dw2p[FOX_HEADS:FOX_HEADS + GLA_RANK],
        b_gla_a2=db2, g_fox_out=dg_fox, g_gla_out=dg_gla)
    return loss_part, grad_x, g_big, small


def _pack(arrays):
    flat = jnp.concatenate([a.reshape(-1).astype(F32) for a in arrays])
    n = flat.shape[0]
    rows = -(-n // 128)
    rows = -(-rows // 8) * 8
    return jnp.pad(flat, (0, rows * 128 - n)).reshape(rows, 128)


def _unpack(buf, shapes):
    flat = buf.reshape(-1)
    out, off = [], 0
    for shp in shapes:
        n = 1
        for q in shp:
            n *= q
        out.append(flat[off:off + n].reshape(shp))
        off += n
    return out


SMALL_GRAD_ORDER = ["dmod", "g_pre_mix", "g_post_mix", "g_pre_mlp", "g_post_mlp", "b_fgate", "w_gla_a2", "b_gla_a2",
                    "g_fox_out", "g_gla_out"]


def kernel(x, c, w_ada, b_ada, g_pre_mix, g_post_mix, w_in, b_fgate, w_gla_a2, b_gla_a2, g_fox_out, g_gla_out, w_out, g_pre_mlp, g_post_mlp, w_mlp_in, w_mlp_out, loss_target, m_w_ada, m_b_ada, m_g_pre_mix, m_g_post_mix, m_w_in, m_b_fgate, m_w_gla_a2, m_b_gla_a2, m_g_fox_out, m_g_gla_out, m_w_out, m_g_pre_mlp, m_g_post_mlp, m_w_mlp_in, m_w_mlp_out, v_w_ada, v_b_ada, v_g_pre_mix, v_g_post_mix, v_w_in, v_b_fgate, v_w_gla_a2, v_b_gla_a2, v_g_fox_out, v_g_gla_out, v_w_out, v_g_pre_mlp, v_g_post_mlp, v_w_mlp_in, v_w_mlp_out):
    ix, iy, ic = lax.axis_index("x"), lax.axis_index("y"), lax.axis_index("c")
    chip = 2 * ix + iy
    dev = 4 * ix + 2 * iy + ic
    d = D_MODEL

    c_act = _silu_rows(c, name="silu_c")
    pack1 = _pack([c_act, w_gla_a2[0], g_gla_out[0]])
    rows1 = pack1.shape[0]
    got1 = _gather8(pack1, name="gather_small_fwd").reshape(8, rows1, 128)
    per_dev = [_unpack(got1[q], [(d,), (GLA_RANK, GLA_KW // 4), (GLA_HEADS, GLA_DV // 4)]) for q in range(8)]
    c_all = jnp.stack([p[0] for p in per_dev])
    w_gla_a2_full = jnp.concatenate([per_dev[2 * j][1] for j in range(4)], axis=1)
    g_gla_full = jnp.concatenate([per_dev[2 * j][2] for j in range(4)], axis=1)
    cols = w_ada.shape[2]
    b_ada_shard = lax.dynamic_slice_in_dim(b_ada, chip * cols, cols, axis=1)
    mod_sh = _mod_shard(c_all, w_ada[0], b_ada_shard, name="ada_mod")
    got2 = _gather8(mod_sh, name="gather_mod").reshape(8, 8, cols)
    mod_all = jnp.concatenate([got2[2 * j] for j in range(4)], axis=1)
    mod = lax.dynamic_slice_in_dim(mod_all, dev, 1, axis=0)

    tr_in = lambda a: jnp.transpose(a[0])
    own_bf = [tr_in(w_in).astype(BF16), w_out[0].astype(BF16), w_mlp_in[0].astype(BF16), w_mlp_out[0].astype(BF16)]
    gw_in = _gather_relayed(own_bf[0], name="gather_w_in_ici")
    idx = jnp.stack([ic, chip]).astype(jnp.int32)
    loss_part, grad_x, g_big, small = _local_step(
        x[0], loss_target[0], mod, g_pre_mix, g_post_mix, g_pre_mlp, g_post_mlp, gw_in, b_fgate,
        w_gla_a2_full, b_gla_a2, g_fox_out[0], g_gla_full, own_bf[0], own_bf[1], own_bf[2], own_bf[3], idx)
    loss = lax.psum(loss_part[0, 0], ("x", "y", "c"))

    big_w = [(tr_in(w_in), tr_in(m_w_in), tr_in(v_w_in)), (w_out[0], m_w_out[0], v_w_out[0]),
             (w_mlp_in[0], m_w_mlp_in[0], v_w_mlp_in[0]), (w_mlp_out[0], m_w_mlp_out[0], v_w_mlp_out[0])]
    big_res = []
    for q, (g, (w, m, v)) in enumerate(zip(g_big, big_w)):
        res4 = (g,) + tuple(_adam(g, w, m, v, name=f"adam_big_{q}"))
        big_res.append(tuple((jnp.transpose(a) if q == 0 else a)[None] for a in res4))

    pack2 = _pack([small[k] for k in SMALL_GRAD_ORDER])
    rows2 = pack2.shape[0]
    got3 = _gather8(pack2, name="gather_small_grads").reshape(8, rows2, 128)
    dmod_all = got3[:, :6 * d // 128, :].reshape(8, 6 * d)
    sums = _stack_sum(got3, name="small_grad_sum")
    shapes = [(1, 6 * d), (1, d), (1, d), (1, d), (1, d), (1, FOX_HEADS), (1, GLA_RANK, GLA_KW), (1, GLA_KW),
              (1, FOX_HEADS, FOX_HD), (1, GLA_HEADS, GLA_DV)]
    sg = dict(zip(["b_ada"] + SMALL_GRAD_ORDER[1:], _unpack(sums, shapes)))
    sg["w_gla_a2"] = lax.dynamic_slice_in_dim(sg["w_gla_a2"], chip * (GLA_KW // 4), GLA_KW // 4, axis=2)
    sg["g_gla_out"] = lax.dynamic_slice_in_dim(sg["g_gla_out"], chip * (GLA_DV // 4), GLA_DV // 4, axis=2)
    small_names = ["b_ada", "g_pre_mix", "g_post_mix", "b_fgate", "w_gla_a2", "b_gla_a2", "g_fox_out", "g_gla_out",
                   "g_pre_mlp", "g_post_mlp"]
    small_w = dict(b_ada=(b_ada, m_b_ada, v_b_ada), g_pre_mix=(g_pre_mix, m_g_pre_mix, v_g_pre_mix),
                   g_post_mix=(g_post_mix, m_g_post_mix, v_g_post_mix), b_fgate=(b_fgate, m_b_fgate, v_b_fgate),
                   w_gla_a2=(w_gla_a2, m_w_gla_a2, v_w_gla_a2), b_gla_a2=(b_gla_a2, m_b_gla_a2, v_b_gla_a2),
                   g_fox_out=(g_fox_out, m_g_fox_out, v_g_fox_out), g_gla_out=(g_gla_out, m_g_gla_out, v_g_gla_out),
                   g_pre_mlp=(g_pre_mlp, m_g_pre_mlp, v_g_pre_mlp), g_post_mlp=(g_post_mlp, m_g_post_mlp, v_g_post_mlp))
    sshapes = [small_w[k][0].shape for k in small_names]
    pg = _pack([sg[k] for k in small_names])
    pw, pm, pv = [_pack([small_w[k][q] for k in small_names]) for q in range(3)]
    pd, pmn, pvn = _adam(pg, pw, pm, pv, name="adam_small")
    s_delta = dict(zip(small_names, _unpack(pd, sshapes)))
    s_m = dict(zip(small_names, _unpack(pmn, sshapes)))
    s_v = dict(zip(small_names, _unpack(pvn, sshapes)))

    dmod_cols = lax.dynamic_slice_in_dim(dmod_all, chip * cols, cols, axis=1)
    g_ada, d_ada, m_ada, v_ada = _ada_grad_adam(c_all.T, dmod_cols, w_ada[0], m_w_ada[0], v_w_ada[0], name="ada_grad_adam")

    order = ["w_ada", "b_ada", "g_pre_mix", "g_post_mix", "w_in", "b_fgate", "w_gla_a2", "b_gla_a2", "g_fox_out",
             "g_gla_out", "w_out", "g_pre_mlp", "g_post_mlp", "w_mlp_in", "w_mlp_out"]
    res = {"w_ada": (g_ada[None], d_ada[None], m_ada[None], v_ada[None]),
           "w_in": big_res[0], "w_out": big_res[1], "w_mlp_in": big_res[2], "w_mlp_out": big_res[3]}
    for k in small_names:
        res[k] = (sg[k], s_delta[k], s_m[k], s_v[k])
    return (loss, grad_x[None], *[res[k][0] for k in order], *[res[k][1] for k in order],
            *[res[k][2] for k in order], *[res[k][3] for k in order])
```
